```python
import jax, jax.numpy as jnp
from jax import lax
import numpy as np

D_MODEL = 2048
BATCH = 8
SEQ = 4096
DEPTH = 1

N_META = 16
CHUNK = 128
Q_BLOCK = 128
RET_HEADS = 8
RET_QK_DIM = 128
RET_V_DIM = 128
MLA_HEADS = 8
MLA_NOPE = 128
MLA_ROPE = 64
MLA_V = 128
MLA_Q_RANK = 512
MLA_KV_RANK = 256
D_MIX = RET_HEADS * RET_V_DIM + MLA_HEADS * MLA_V
IN_SIZES = (RET_HEADS * RET_QK_DIM, RET_HEADS * RET_QK_DIM, RET_HEADS * RET_V_DIM,
            RET_HEADS * RET_V_DIM, MLA_Q_RANK, MLA_KV_RANK, MLA_ROPE)
D_IN = sum(IN_SIZES)
D_FF = 5632
ROPE_THETA = 10000.0
EPS = 1e-6

kernel_name = "hymba_retnet_mla_macaron_sandwich"


def rmsnorm(x, w):
    x32 = x.astype(jnp.float32)
    y = x32 * lax.rsqrt(jnp.mean(x32 * x32, axis=-1, keepdims=True) + EPS)
    return (y * w.astype(jnp.float32)).astype(x.dtype)


def swiglu(h, w_gate, w_up, w_down):
    return (jax.nn.silu(h @ w_gate) * (h @ w_up)) @ w_down


def rope_tables(pos, dim):
    inv = ROPE_THETA ** (-jnp.arange(0, dim, 2, dtype=jnp.float32) / dim)
    ang = pos[:, None] * inv[None, :]
    return jnp.cos(ang), jnp.sin(ang)


def apply_rope(x, cos, sin):
    x32 = x.astype(jnp.float32)
    x1, x2 = jnp.split(x32, 2, axis=-1)
    out = jnp.concatenate([x1 * cos - x2 * sin, x2 * cos + x1 * sin], axis=-1)
    return out.astype(x.dtype)


def retention(q, k, v):
    b, seq_len, n_heads, dv = v.shape
    dk = q.shape[-1]
    log_g = jnp.log(1.0 - 2.0 ** (-5.0 - jnp.arange(n_heads, dtype=jnp.float32)))

    def decay_mask(n):
        idx = jnp.arange(n, dtype=jnp.float32)
        diff = idx[:, None] - idx[None, :]
        return jnp.where(diff[None] >= 0,
                         jnp.exp(jnp.maximum(diff, 0.0)[None] * log_g[:, None, None]), 0.0)

    def intra(qc, kc, vc, dmask):
        s = jnp.einsum('bnhd,bmhd->bhnm', qc, kc) * dmask[None]
        return jnp.einsum('bhnm,bmhv->bnhv', s, vc)

    qm, km, vm = q[:, :N_META], k[:, :N_META], v[:, :N_META]
    o_meta = intra(qm, km, vm, decay_mask(N_META))
    pos_m = jnp.arange(N_META, dtype=jnp.float32)
    zeta_m = jnp.exp((N_META - 1 - pos_m)[:, None] * log_g[None, :])
    state0 = jnp.einsum('bmhd,bmhv,mh->bhdv', km, vm, zeta_m)

    n_chunks = (seq_len - N_META) // CHUNK

    def to_chunks(t):
        return t[:, N_META:].reshape(b, n_chunks, CHUNK, n_heads, t.shape[-1]).transpose(1, 0, 2, 3, 4)

    dmask_c = decay_mask(CHUNK)
    pos_c = jnp.arange(CHUNK, dtype=jnp.float32)
    xi = jnp.exp((pos_c + 1.0)[:, None] * log_g[None, :])
    zeta = jnp.exp((CHUNK - 1 - pos_c)[:, None] * log_g[None, :])
    g_chunk = jnp.exp(CHUNK * log_g)

    def step(state, qkv):
        qc, kc, vc = qkv
        o = intra(qc, kc, vc, dmask_c) + \
            jnp.einsum('bnhd,bhdv->bnhv', qc, state) * xi[None, :, :, None]
        state = state * g_chunk[None, :, None, None] + \
            jnp.einsum('bmhd,bmhv,mh->bhdv', kc, vc, zeta)
        return state, o

    _, o_real = lax.scan(step, state0, (to_chunks(q), to_chunks(k), to_chunks(v)))
    o_real = o_real.transpose(1, 0, 2, 3, 4).reshape(b, n_chunks * CHUNK, n_heads, dv)
    return jnp.concatenate([o_meta, o_real], axis=1)


def mla_attention(q_nope, q_rope, k_nope, k_rope, v):
    b, seq_len, n_heads, dv = v.shape
    scale = (MLA_NOPE + MLA_ROPE) ** -0.5
    kpos = jnp.arange(seq_len, dtype=jnp.int32)

    def attend(qn, qr, qpos, kn, kr, vv, kp):
        s = (jnp.einsum('bqhd,bkhd->bhqk', qn, kn) +
             jnp.einsum('bqhr,bkr->bhqk', qr, kr)).astype(jnp.float32) * scale
        s = jnp.where(kp[None, :] <= qpos[:, None], s, -1e30)
        p = jax.nn.softmax(s, axis=-1)
        return jnp.einsum('bhqk,bkhv->bqhv', p.astype(vv.dtype), vv)

    o_meta = attend(q_nope[:, :N_META], q_rope[:, :N_META], kpos[:N_META],
                    k_nope[:, :N_META], k_rope[:, :N_META], v[:, :N_META], kpos[:N_META])

    n_blocks = (seq_len - N_META) // Q_BLOCK

    def blocks(t):
        return t[:, N_META:].reshape(b, n_blocks, Q_BLOCK, *t.shape[2:]).swapaxes(0, 1)

    qpos_blocks = (N_META + jnp.arange(seq_len - N_META, dtype=jnp.int32)).reshape(n_blocks, Q_BLOCK)
    o_real = lax.map(lambda a: attend(a[0], a[1], a[2], k_nope, k_rope, v, kpos),
                     (blocks(q_nope), blocks(q_rope), qpos_blocks))
    o_real = o_real.swapaxes(0, 1).reshape(b, seq_len - N_META, n_heads, dv)
    return jnp.concatenate([o_meta, o_real], axis=1)


def hybrid_mixer(u, w_in, ret_group_norm, mla_q_norm, mla_w_uq, mla_kv_norm,
                 mla_w_uk, mla_w_uv, w_out, cos_r, sin_r, cos_m, sin_m):
    b, seq_len, _ = u.shape
    proj = u @ w_in
    offsets = np.cumsum(IN_SIZES)[:-1].tolist()
    rq, rk, rv, rg, cq, ckv, kr = jnp.split(proj, offsets, axis=-1)

    rq = apply_rope(rq.reshape(b, seq_len, RET_HEADS, RET_QK_DIM), cos_r[:, None], sin_r[:, None])
    rk = apply_rope(rk.reshape(b, seq_len, RET_HEADS, RET_QK_DIM), cos_r[:, None], sin_r[:, None])
    rv = rv.reshape(b, seq_len, RET_HEADS, RET_V_DIM)
    ret = retention(rq.astype(jnp.float32),
                    rk.astype(jnp.float32) * (RET_QK_DIM ** -0.5),
                    rv.astype(jnp.float32))
    ret = ret * lax.rsqrt(jnp.mean(ret * ret, axis=-1, keepdims=True) + EPS)
    ret = ret.reshape(b, seq_len, RET_HEADS * RET_V_DIM) * ret_group_norm.astype(jnp.float32)
    ret = (jax.nn.silu(rg.astype(jnp.float32)) * ret).astype(u.dtype)

    cq = rmsnorm(cq, mla_q_norm)
    q = (cq @ mla_w_uq).reshape(b, seq_len, MLA_HEADS, MLA_NOPE + MLA_ROPE)
    q_nope, q_rope = q[..., :MLA_NOPE], q[..., MLA_NOPE:]
    q_rope = apply_rope(q_rope, cos_m[:, None], sin_m[:, None])
    ckv = rmsnorm(ckv, mla_kv_norm)
    k_nope = (ckv @ mla_w_uk).reshape(b, seq_len, MLA_HEADS, MLA_NOPE)
    v = (ckv @ mla_w_uv).reshape(b, seq_len, MLA_HEADS, MLA_V)
    k_rope = apply_rope(kr, cos_m, sin_m)
    mla = mla_attention(q_nope, q_rope, k_nope, k_rope, v).reshape(b, seq_len, MLA_HEADS * MLA_V)

    return jnp.concatenate([ret, mla.astype(u.dtype)], axis=-1) @ w_out


def _normal(k, shape, fan_in):
    return jax.random.normal(k, shape, jnp.float32) * (fan_in ** -0.5)


def _gain(k, shape):
    return 1.0 + 0.02 * jax.random.normal(k, shape, jnp.float32)


def _fwd_setup_inputs(seed: int = 0) -> dict:
    key = jax.random.key(seed)
    ks = jax.random.split(key, 24)
    L = DEPTH
    return {
        "x": jax.random.normal(ks[0], (BATCH, SEQ, D_MODEL), jnp.float32),
        "meta_tokens": jax.random.normal(ks[1], (N_META, D_MODEL), jnp.float32),
        "ffn1_pre_norm": _gain(ks[2], (L, D_MODEL)),
        "ffn1_w_gate": _normal(ks[3], (L, D_MODEL, D_FF), D_MODEL),
        "ffn1_w_up": _normal(ks[4], (L, D_MODEL, D_FF), D_MODEL),
        "ffn1_w_down": _normal(ks[5], (L, D_FF, D_MODEL), D_FF),
        "ffn1_post_norm": _gain(ks[6], (L, D_MODEL)),
        "mix_pre_norm": _gain(ks[7], (L, D_MODEL)),
        "w_in": _normal(ks[8], (L, D_MODEL, D_IN), D_MODEL),
        "ret_group_norm": _gain(ks[9], (L, RET_HEADS * RET_V_DIM)),
        "mla_q_norm": _gain(ks[10], (L, MLA_Q_RANK)),
        "mla_w_uq": _normal(ks[11], (L, MLA_Q_RANK, MLA_HEADS * (MLA_NOPE + MLA_ROPE)), MLA_Q_RANK),
        "mla_kv_norm": _gain(ks[12], (L, MLA_KV_RANK)),
        "mla_w_uk": _normal(ks[13], (L, MLA_KV_RANK, MLA_HEADS * MLA_NOPE), MLA_KV_RANK),
        "mla_w_uv": _normal(ks[14], (L, MLA_KV_RANK, MLA_HEADS * MLA_V), MLA_KV_RANK),
        "w_out": _normal(ks[15], (L, D_MIX, D_MODEL), D_MIX),
        "mix_post_norm": _gain(ks[16], (L, D_MODEL)),
        "ffn2_pre_norm": _gain(ks[17], (L, D_MODEL)),
        "ffn2_w_gate": _normal(ks[18], (L, D_MODEL, D_FF), D_MODEL),
        "ffn2_w_up": _normal(ks[19], (L, D_MODEL, D_FF), D_MODEL),
        "ffn2_w_down": _normal(ks[20], (L, D_FF, D_MODEL), D_FF),
        "ffn2_post_norm": _gain(ks[21], (L, D_MODEL)),
    }


def _fwd_reference(x, meta_tokens, ffn1_pre_norm, ffn1_w_gate, ffn1_w_up, ffn1_w_down,
              ffn1_post_norm, mix_pre_norm, w_in, ret_group_norm, mla_q_norm, mla_w_uq,
              mla_kv_norm, mla_w_uk, mla_w_uv, w_out, mix_post_norm, ffn2_pre_norm,
              ffn2_w_gate, ffn2_w_up, ffn2_w_down, ffn2_post_norm):
    b = x.shape[0]
    meta = jnp.broadcast_to(meta_tokens.astype(x.dtype)[None], (b, N_META, x.shape[-1]))
    h = jnp.concatenate([meta, x], axis=1)
    seq_len = h.shape[1]
    pos = jnp.arange(seq_len, dtype=jnp.float32)
    cos_r, sin_r = rope_tables(pos, RET_QK_DIM)
    cos_m, sin_m = rope_tables(pos, MLA_ROPE)
    for l in range(DEPTH):
        f = swiglu(rmsnorm(h, ffn1_pre_norm[l]), ffn1_w_gate[l], ffn1_w_up[l], ffn1_w_down[l])
        h = h + 0.5 * rmsnorm(f, ffn1_post_norm[l])
        m = hybrid_mixer(rmsnorm(h, mix_pre_norm[l]), w_in[l], ret_group_norm[l], mla_q_norm[l],
                         mla_w_uq[l], mla_kv_norm[l], mla_w_uk[l], mla_w_uv[l], w_out[l],
                         cos_r, sin_r, cos_m, sin_m)
        h = h + rmsnorm(m, mix_post_norm[l])
        f = swiglu(rmsnorm(h, ffn2_pre_norm[l]), ffn2_w_gate[l], ffn2_w_up[l], ffn2_w_down[l])
        h = h + 0.5 * rmsnorm(f, ffn2_post_norm[l])
    return h[:, N_META:]


import jax as _jax
import jax.numpy as _jnp

TWIN_FORMAT = 'train_step'
FWD_PARAMS = ['x', 'meta_tokens', 'ffn1_pre_norm', 'ffn1_w_gate', 'ffn1_w_up', 'ffn1_w_down', 'ffn1_post_norm', 'mix_pre_norm', 'w_in', 'ret_group_norm', 'mla_q_norm', 'mla_w_uq', 'mla_kv_norm', 'mla_w_uk', 'mla_w_uv', 'w_out', 'mix_post_norm', 'ffn2_pre_norm', 'ffn2_w_gate', 'ffn2_w_up', 'ffn2_w_down', 'ffn2_post_norm']
TWIN_WEIGHTS = ['meta_tokens', 'ffn1_pre_norm', 'ffn1_w_gate', 'ffn1_w_up', 'ffn1_w_down', 'ffn1_post_norm', 'mix_pre_norm', 'w_in', 'ret_group_norm', 'mla_q_norm', 'mla_w_uq', 'mla_kv_norm', 'mla_w_uk', 'mla_w_uv', 'w_out', 'mix_post_norm', 'ffn2_pre_norm', 'ffn2_w_gate', 'ffn2_w_up', 'ffn2_w_down', 'ffn2_post_norm']
TWIN_DIFF_INPUT = 'x'
TWIN_INPUTS = ['x', 'meta_tokens', 'ffn1_pre_norm', 'ffn1_w_gate', 'ffn1_w_up', 'ffn1_w_down', 'ffn1_post_norm', 'mix_pre_norm', 'w_in', 'ret_group_norm', 'mla_q_norm', 'mla_w_uq', 'mla_kv_norm', 'mla_w_uk', 'mla_w_uv', 'w_out', 'mix_post_norm', 'ffn2_pre_norm', 'ffn2_w_gate', 'ffn2_w_up', 'ffn2_w_down', 'ffn2_post_norm', 'loss_target', 'm_meta_tokens', 'm_ffn1_pre_norm', 'm_ffn1_w_gate', 'm_ffn1_w_up', 'm_ffn1_w_down', 'm_ffn1_post_norm', 'm_mix_pre_norm', 'm_w_in', 'm_ret_group_norm', 'm_mla_q_norm', 'm_mla_w_uq', 'm_mla_kv_norm', 'm_mla_w_uk', 'm_mla_w_uv', 'm_w_out', 'm_mix_post_norm', 'm_ffn2_pre_norm', 'm_ffn2_w_gate', 'm_ffn2_w_up', 'm_ffn2_w_down', 'm_ffn2_post_norm', 'v_meta_tokens', 'v_ffn1_pre_norm', 'v_ffn1_w_gate', 'v_ffn1_w_up', 'v_ffn1_w_down', 'v_ffn1_post_norm', 'v_mix_pre_norm', 'v_w_in', 'v_ret_group_norm', 'v_mla_q_norm', 'v_mla_w_uq', 'v_mla_kv_norm', 'v_mla_w_uk', 'v_mla_w_uv', 'v_w_out', 'v_mix_post_norm', 'v_ffn2_pre_norm', 'v_ffn2_w_gate', 'v_ffn2_w_up', 'v_ffn2_w_down', 'v_ffn2_post_norm']
TWIN_OUTPUTS = ['loss', 'grad_x', 'grad_meta_tokens', 'grad_ffn1_pre_norm', 'grad_ffn1_w_gate', 'grad_ffn1_w_up', 'grad_ffn1_w_down', 'grad_ffn1_post_norm', 'grad_mix_pre_norm', 'grad_w_in', 'grad_ret_group_norm', 'grad_mla_q_norm', 'grad_mla_w_uq', 'grad_mla_kv_norm', 'grad_mla_w_uk', 'grad_mla_w_uv', 'grad_w_out', 'grad_mix_post_norm', 'grad_ffn2_pre_norm', 'grad_ffn2_w_gate', 'grad_ffn2_w_up', 'grad_ffn2_w_down', 'grad_ffn2_post_norm', 'delta_meta_tokens', 'delta_ffn1_pre_norm', 'delta_ffn1_w_gate', 'delta_ffn1_w_up', 'delta_ffn1_w_down', 'delta_ffn1_post_norm', 'delta_mix_pre_norm', 'delta_w_in', 'delta_ret_group_norm', 'delta_mla_q_norm', 'delta_mla_w_uq', 'delta_mla_kv_norm', 'delta_mla_w_uk', 'delta_mla_w_uv', 'delta_w_out', 'delta_mix_post_norm', 'delta_ffn2_pre_norm', 'delta_ffn2_w_gate', 'delta_ffn2_w_up', 'delta_ffn2_w_down', 'delta_ffn2_post_norm', 'new_m_meta_tokens', 'new_m_ffn1_pre_norm', 'new_m_ffn1_w_gate', 'new_m_ffn1_w_up', 'new_m_ffn1_w_down', 'new_m_ffn1_post_norm', 'new_m_mix_pre_norm', 'new_m_w_in', 'new_m_ret_group_norm', 'new_m_mla_q_norm', 'new_m_mla_w_uq', 'new_m_mla_kv_norm', 'new_m_mla_w_uk', 'new_m_mla_w_uv', 'new_m_w_out', 'new_m_mix_post_norm', 'new_m_ffn2_pre_norm', 'new_m_ffn2_w_gate', 'new_m_ffn2_w_up', 'new_m_ffn2_w_down', 'new_m_ffn2_post_norm', 'new_v_meta_tokens', 'new_v_ffn1_pre_norm', 'new_v_ffn1_w_gate', 'new_v_ffn1_w_up', 'new_v_ffn1_w_down', 'new_v_ffn1_post_norm', 'new_v_mix_pre_norm', 'new_v_w_in', 'new_v_ret_group_norm', 'new_v_mla_q_norm', 'new_v_mla_w_uq', 'new_v_mla_kv_norm', 'new_v_mla_w_uk', 'new_v_mla_w_uv', 'new_v_w_out', 'new_v_mix_post_norm', 'new_v_ffn2_pre_norm', 'new_v_ffn2_w_gate', 'new_v_ffn2_w_up', 'new_v_ffn2_w_down', 'new_v_ffn2_post_norm']
TWIN_LEAF_KINDS = {'loss': 'loss', 'grad_x': 'grad_x', 'grad_meta_tokens': 'grad_w', 'grad_ffn1_pre_norm': 'grad_w', 'grad_ffn1_w_gate': 'grad_w', 'grad_ffn1_w_up': 'grad_w', 'grad_ffn1_w_down': 'grad_w', 'grad_ffn1_post_norm': 'grad_w', 'grad_mix_pre_norm': 'grad_w', 'grad_w_in': 'grad_w', 'grad_ret_group_norm': 'grad_w', 'grad_mla_q_norm': 'grad_w', 'grad_mla_w_uq': 'grad_w', 'grad_mla_kv_norm': 'grad_w', 'grad_mla_w_uk': 'grad_w', 'grad_mla_w_uv': 'grad_w', 'grad_w_out': 'grad_w', 'grad_mix_post_norm': 'grad_w', 'grad_ffn2_pre_norm': 'grad_w', 'grad_ffn2_w_gate': 'grad_w', 'grad_ffn2_w_up': 'grad_w', 'grad_ffn2_w_down': 'grad_w', 'grad_ffn2_post_norm': 'grad_w', 'delta_meta_tokens': 'delta_w', 'delta_ffn1_pre_norm': 'delta_w', 'delta_ffn1_w_gate': 'delta_w', 'delta_ffn1_w_up': 'delta_w', 'delta_ffn1_w_down': 'delta_w', 'delta_ffn1_post_norm': 'delta_w', 'delta_mix_pre_norm': 'delta_w', 'delta_w_in': 'delta_w', 'delta_ret_group_norm': 'delta_w', 'delta_mla_q_norm': 'delta_w', 'delta_mla_w_uq': 'delta_w', 'delta_mla_kv_norm': 'delta_w', 'delta_mla_w_uk': 'delta_w', 'delta_mla_w_uv': 'delta_w', 'delta_w_out': 'delta_w', 'delta_mix_post_norm': 'delta_w', 'delta_ffn2_pre_norm': 'delta_w', 'delta_ffn2_w_gate': 'delta_w', 'delta_ffn2_w_up': 'delta_w', 'delta_ffn2_w_down': 'delta_w', 'delta_ffn2_post_norm': 'delta_w', 'new_m_meta_tokens': 'new_m', 'new_m_ffn1_pre_norm': 'new_m', 'new_m_ffn1_w_gate': 'new_m', 'new_m_ffn1_w_up': 'new_m', 'new_m_ffn1_w_down': 'new_m', 'new_m_ffn1_post_norm': 'new_m', 'new_m_mix_pre_norm': 'new_m', 'new_m_w_in': 'new_m', 'new_m_ret_group_norm': 'new_m', 'new_m_mla_q_norm': 'new_m', 'new_m_mla_w_uq': 'new_m', 'new_m_mla_kv_norm': 'new_m', 'new_m_mla_w_uk': 'new_m', 'new_m_mla_w_uv': 'new_m', 'new_m_w_out': 'new_m', 'new_m_mix_post_norm': 'new_m', 'new_m_ffn2_pre_norm': 'new_m', 'new_m_ffn2_w_gate': 'new_m', 'new_m_ffn2_w_up': 'new_m', 'new_m_ffn2_w_down': 'new_m', 'new_m_ffn2_post_norm': 'new_m', 'new_v_meta_tokens': 'new_v', 'new_v_ffn1_pre_norm': 'new_v', 'new_v_ffn1_w_gate': 'new_v', 'new_v_ffn1_w_up': 'new_v', 'new_v_ffn1_w_down': 'new_v', 'new_v_ffn1_post_norm': 'new_v', 'new_v_mix_pre_norm': 'new_v', 'new_v_w_in': 'new_v', 'new_v_ret_group_norm': 'new_v', 'new_v_mla_q_norm': 'new_v', 'new_v_mla_w_uq': 'new_v', 'new_v_mla_kv_norm': 'new_v', 'new_v_mla_w_uk': 'new_v', 'new_v_mla_w_uv': 'new_v', 'new_v_w_out': 'new_v', 'new_v_mix_post_norm': 'new_v', 'new_v_ffn2_pre_norm': 'new_v', 'new_v_ffn2_w_gate': 'new_v', 'new_v_ffn2_w_up': 'new_v', 'new_v_ffn2_w_down': 'new_v', 'new_v_ffn2_post_norm': 'new_v'}


def _forward(args):
    return _fwd_reference(*[args[k] for k in FWD_PARAMS])


def _output_shape():
    def fwd():
        inp = _fwd_setup_inputs(0)
        return _fwd_reference(*[inp[k] for k in FWD_PARAMS])
    out = _jax.eval_shape(fwd)
    return out.shape, out.dtype

N_MICROBATCH = 1
ADAM_LR = 0.001
ADAM_B1 = 0.9
ADAM_B2 = 0.999
ADAM_EPS = 1e-08
ADAM_WD = 0.01
ADAM_STEP = 10
PER_EXAMPLE_BATCH_AXIS = {'x': 0, 'loss_target': 0}
SHARED_INPUTS = []
_WEIGHT_DTYPES = {'meta_tokens': _jnp.float32, 'ffn1_pre_norm': _jnp.float32, 'ffn1_w_gate': _jnp.float32, 'ffn1_w_up': _jnp.float32, 'ffn1_w_down': _jnp.float32, 'ffn1_post_norm': _jnp.float32, 'mix_pre_norm': _jnp.float32, 'w_in': _jnp.float32, 'ret_group_norm': _jnp.float32, 'mla_q_norm': _jnp.float32, 'mla_w_uq': _jnp.float32, 'mla_kv_norm': _jnp.float32, 'mla_w_uk': _jnp.float32, 'mla_w_uv': _jnp.float32, 'w_out': _jnp.float32, 'mix_post_norm': _jnp.float32, 'ffn2_pre_norm': _jnp.float32, 'ffn2_w_gate': _jnp.float32, 'ffn2_w_up': _jnp.float32, 'ffn2_w_down': _jnp.float32, 'ffn2_post_norm': _jnp.float32}
MOMENT_SCALE = {'meta_tokens': 2.294751e-02, 'ffn1_pre_norm': 2.452410e-01, 'ffn1_w_gate': 1.045420e-01, 'ffn1_w_up': 1.044845e-01, 'ffn1_w_down': 1.736696e-01, 'ffn1_post_norm': 3.984537e+00, 'mix_pre_norm': 3.438477e-01, 'w_in': 2.126938e-01, 'ret_group_norm': 2.260006e-01, 'mla_q_norm': 8.661842e-02, 'mla_w_uq': 4.911983e-02, 'mla_kv_norm': 1.857647e-01, 'mla_w_uk': 5.095502e-02, 'mla_w_uv': 6.724141e-02, 'w_out': 1.727564e-01, 'mix_post_norm': 1.603846e+01, 'ffn2_pre_norm': 1.423706e-01, 'ffn2_w_gate': 4.994725e-02, 'ffn2_w_up': 6.692243e-02, 'ffn2_w_down': 1.109307e-01, 'ffn2_post_norm': 4.006323e+00}


def _to_microbatches(a, axis):
    t = _jnp.moveaxis(a, axis, 0)
    t = t.reshape((N_MICROBATCH, t.shape[0] // N_MICROBATCH) + t.shape[1:])
    return _jnp.moveaxis(t, 1, axis + 1)


def setup_inputs(seed: int = 0) -> dict:
    inp = _fwd_setup_inputs(seed)
    key = _jax.random.fold_in(_jax.random.key(seed), 7919)
    shape, _ = _output_shape()
    out = dict(inp)
    out["loss_target"] = _jax.random.normal(_jax.random.fold_in(key, 0), shape, _jnp.float32)
    for i, name in enumerate(TWIN_WEIGHTS):
        w = inp[name].astype(_jnp.float32)
        if MOMENT_SCALE is None:
            s = _jnp.sqrt(_jnp.mean(_jnp.square(w)) + 1e-30)
        else:
            s = MOMENT_SCALE[name]
        km, kv = _jax.random.split(_jax.random.fold_in(key, i + 1))
        out[name] = w
        out["m_" + name] = s * _jax.random.normal(km, w.shape, _jnp.float32)
        out["v_" + name] = (s * s) * _jax.random.uniform(kv, w.shape, _jnp.float32, 0.5, 1.5)
    if N_MICROBATCH > 1:
        for name, axis in PER_EXAMPLE_BATCH_AXIS.items():
            out[name] = _to_microbatches(out[name], axis)
    return {'x': out['x'], 'meta_tokens': out['meta_tokens'], 'ffn1_pre_norm': out['ffn1_pre_norm'], 'ffn1_w_gate': out['ffn1_w_gate'], 'ffn1_w_up': out['ffn1_w_up'], 'ffn1_w_down': out['ffn1_w_down'], 'ffn1_post_norm': out['ffn1_post_norm'], 'mix_pre_norm': out['mix_pre_norm'], 'w_in': out['w_in'], 'ret_group_norm': out['ret_group_norm'], 'mla_q_norm': out['mla_q_norm'], 'mla_w_uq': out['mla_w_uq'], 'mla_kv_norm': out['mla_kv_norm'], 'mla_w_uk': out['mla_w_uk'], 'mla_w_uv': out['mla_w_uv'], 'w_out': out['w_out'], 'mix_post_norm': out['mix_post_norm'], 'ffn2_pre_norm': out['ffn2_pre_norm'], 'ffn2_w_gate': out['ffn2_w_gate'], 'ffn2_w_up': out['ffn2_w_up'], 'ffn2_w_down': out['ffn2_w_down'], 'ffn2_post_norm': out['ffn2_post_norm'], 'loss_target': out['loss_target'], 'm_meta_tokens': out['m_meta_tokens'], 'm_ffn1_pre_norm': out['m_ffn1_pre_norm'], 'm_ffn1_w_gate': out['m_ffn1_w_gate'], 'm_ffn1_w_up': out['m_ffn1_w_up'], 'm_ffn1_w_down': out['m_ffn1_w_down'], 'm_ffn1_post_norm': out['m_ffn1_post_norm'], 'm_mix_pre_norm': out['m_mix_pre_norm'], 'm_w_in': out['m_w_in'], 'm_ret_group_norm': out['m_ret_group_norm'], 'm_mla_q_norm': out['m_mla_q_norm'], 'm_mla_w_uq': out['m_mla_w_uq'], 'm_mla_kv_norm': out['m_mla_kv_norm'], 'm_mla_w_uk': out['m_mla_w_uk'], 'm_mla_w_uv': out['m_mla_w_uv'], 'm_w_out': out['m_w_out'], 'm_mix_post_norm': out['m_mix_post_norm'], 'm_ffn2_pre_norm': out['m_ffn2_pre_norm'], 'm_ffn2_w_gate': out['m_ffn2_w_gate'], 'm_ffn2_w_up': out['m_ffn2_w_up'], 'm_ffn2_w_down': out['m_ffn2_w_down'], 'm_ffn2_post_norm': out['m_ffn2_post_norm'], 'v_meta_tokens': out['v_meta_tokens'], 'v_ffn1_pre_norm': out['v_ffn1_pre_norm'], 'v_ffn1_w_gate': out['v_ffn1_w_gate'], 'v_ffn1_w_up': out['v_ffn1_w_up'], 'v_ffn1_w_down': out['v_ffn1_w_down'], 'v_ffn1_post_norm': out['v_ffn1_post_norm'], 'v_mix_pre_norm': out['v_mix_pre_norm'], 'v_w_in': out['v_w_in'], 'v_ret_group_norm': out['v_ret_group_norm'], 'v_mla_q_norm': out['v_mla_q_norm'], 'v_mla_w_uq': out['v_mla_w_uq'], 'v_mla_kv_norm': out['v_mla_kv_norm'], 'v_mla_w_uk': out['v_mla_w_uk'], 'v_mla_w_uv': out['v_mla_w_uv'], 'v_w_out': out['v_w_out'], 'v_mix_post_norm': out['v_mix_post_norm'], 'v_ffn2_pre_norm': out['v_ffn2_pre_norm'], 'v_ffn2_w_gate': out['v_ffn2_w_gate'], 'v_ffn2_w_up': out['v_ffn2_w_up'], 'v_ffn2_w_down': out['v_ffn2_w_down'], 'v_ffn2_post_norm': out['v_ffn2_post_norm']}


def _loss(weights, diff, rest, loss_target):
    with _jax.named_scope("forward"):
        args = {**rest, TWIN_DIFF_INPUT: diff, **{k: w.astype(_WEIGHT_DTYPES[k]) for k, w in weights.items()}}
        y = _forward(args)
    with _jax.named_scope("loss_head"):
        err = _jnp.square(y.astype(_jnp.float32) - loss_target)
        return 0.5 * _jnp.sum(_jnp.mean(err, axis=-1)) if err.ndim else 0.5 * err


def _adamw(w, g, m, v):
    m = ADAM_B1 * m + (1.0 - ADAM_B1) * g
    v = ADAM_B2 * v + (1.0 - ADAM_B2) * _jnp.square(g)
    m_hat = m / (1.0 - ADAM_B1 ** ADAM_STEP)
    v_hat = v / (1.0 - ADAM_B2 ** ADAM_STEP)
    delta = -ADAM_LR * (m_hat / (_jnp.sqrt(v_hat) + ADAM_EPS) + ADAM_WD * w)
    return delta, m, v


def reference(x, meta_tokens, ffn1_pre_norm, ffn1_w_gate, ffn1_w_up, ffn1_w_down, ffn1_post_norm, mix_pre_norm, w_in, ret_group_norm, mla_q_norm, mla_w_uq, mla_kv_norm, mla_w_uk, mla_w_uv, w_out, mix_post_norm, ffn2_pre_norm, ffn2_w_gate, ffn2_w_up, ffn2_w_down, ffn2_post_norm, loss_target, m_meta_tokens, m_ffn1_pre_norm, m_ffn1_w_gate, m_ffn1_w_up, m_ffn1_w_down, m_ffn1_post_norm, m_mix_pre_norm, m_w_in, m_ret_group_norm, m_mla_q_norm, m_mla_w_uq, m_mla_kv_norm, m_mla_w_uk, m_mla_w_uv, m_w_out, m_mix_post_norm, m_ffn2_pre_norm, m_ffn2_w_gate, m_ffn2_w_up, m_ffn2_w_down, m_ffn2_post_norm, v_meta_tokens, v_ffn1_pre_norm, v_ffn1_w_gate, v_ffn1_w_up, v_ffn1_w_down, v_ffn1_post_norm, v_mix_pre_norm, v_w_in, v_ret_group_norm, v_mla_q_norm, v_mla_w_uq, v_mla_kv_norm, v_mla_w_uk, v_mla_w_uv, v_w_out, v_mix_post_norm, v_ffn2_pre_norm, v_ffn2_w_gate, v_ffn2_w_up, v_ffn2_w_down, v_ffn2_post_norm):
    given = dict(x=x, meta_tokens=meta_tokens, ffn1_pre_norm=ffn1_pre_norm, ffn1_w_gate=ffn1_w_gate, ffn1_w_up=ffn1_w_up, ffn1_w_down=ffn1_w_down, ffn1_post_norm=ffn1_post_norm, mix_pre_norm=mix_pre_norm, w_in=w_in, ret_group_norm=ret_group_norm, mla_q_norm=mla_q_norm, mla_w_uq=mla_w_uq, mla_kv_norm=mla_kv_norm, mla_w_uk=mla_w_uk, mla_w_uv=mla_w_uv, w_out=w_out, mix_post_norm=mix_post_norm, ffn2_pre_norm=ffn2_pre_norm, ffn2_w_gate=ffn2_w_gate, ffn2_w_up=ffn2_w_up, ffn2_w_down=ffn2_w_down, ffn2_post_norm=ffn2_post_norm, loss_target=loss_target, m_meta_tokens=m_meta_tokens, m_ffn1_pre_norm=m_ffn1_pre_norm, m_ffn1_w_gate=m_ffn1_w_gate, m_ffn1_w_up=m_ffn1_w_up, m_ffn1_w_down=m_ffn1_w_down, m_ffn1_post_norm=m_ffn1_post_norm, m_mix_pre_norm=m_mix_pre_norm, m_w_in=m_w_in, m_ret_group_norm=m_ret_group_norm, m_mla_q_norm=m_mla_q_norm, m_mla_w_uq=m_mla_w_uq, m_mla_kv_norm=m_mla_kv_norm, m_mla_w_uk=m_mla_w_uk, m_mla_w_uv=m_mla_w_uv, m_w_out=m_w_out, m_mix_post_norm=m_mix_post_norm, m_ffn2_pre_norm=m_ffn2_pre_norm, m_ffn2_w_gate=m_ffn2_w_gate, m_ffn2_w_up=m_ffn2_w_up, m_ffn2_w_down=m_ffn2_w_down, m_ffn2_post_norm=m_ffn2_post_norm, v_meta_tokens=v_meta_tokens, v_ffn1_pre_norm=v_ffn1_pre_norm, v_ffn1_w_gate=v_ffn1_w_gate, v_ffn1_w_up=v_ffn1_w_up, v_ffn1_w_down=v_ffn1_w_down, v_ffn1_post_norm=v_ffn1_post_norm, v_mix_pre_norm=v_mix_pre_norm, v_w_in=v_w_in, v_ret_group_norm=v_ret_group_norm, v_mla_q_norm=v_mla_q_norm, v_mla_w_uq=v_mla_w_uq, v_mla_kv_norm=v_mla_kv_norm, v_mla_w_uk=v_mla_w_uk, v_mla_w_uv=v_mla_w_uv, v_w_out=v_w_out, v_mix_post_norm=v_mix_post_norm, v_ffn2_pre_norm=v_ffn2_pre_norm, v_ffn2_w_gate=v_ffn2_w_gate, v_ffn2_w_up=v_ffn2_w_up, v_ffn2_w_down=v_ffn2_w_down, v_ffn2_post_norm=v_ffn2_post_norm)
    weights = {n: given[n] for n in TWIN_WEIGHTS}
    shared = {n: given[n] for n in SHARED_INPUTS}
    per_example = {n: given[n] for n in ['x']}
    grad_fn = _jax.value_and_grad(_loss, argnums=(0, 1))

    def one_microbatch(ex, loss_target):
        ex = dict(ex)
        diff = ex.pop(TWIN_DIFF_INPUT)
        return grad_fn(weights, diff, {**shared, **ex}, loss_target)

    if N_MICROBATCH == 1:
        loss, (grad_w, grad_x) = one_microbatch(per_example, given["loss_target"])
    else:
        def body(carry, xs):
            loss_sum, grad_sum = carry
            l_k, (gw_k, gx_k) = one_microbatch(xs[0], xs[1])
            with _jax.named_scope("update"):
                return (loss_sum + l_k, _jax.tree.map(_jnp.add, grad_sum, gw_k)), gx_k

        init = (_jnp.zeros((), _jnp.float32), _jax.tree.map(_jnp.zeros_like, weights))
        (loss, grad_w), grad_x = _jax.lax.scan(body, init, (per_example, given["loss_target"]))
    with _jax.named_scope("update"):
        delta_w, new_m, new_v = {}, {}, {}
        for n in TWIN_WEIGHTS:
            delta_w[n], new_m[n], new_v[n] = _adamw(weights[n], grad_w[n], given["m_" + n], given["v_" + n])
    return (loss, grad_x, *[grad_w[n] for n in TWIN_WEIGHTS], *[delta_w[n] for n in TWIN_WEIGHTS],
            *[new_m[n] for n in TWIN_WEIGHTS], *[new_v[n] for n in TWIN_WEIGHTS])
```

```python
import functools
import math

import jax
import jax.numpy as jnp
from jax import lax
from jax.experimental import pallas as pl
from jax.experimental.pallas import tpu as pltpu

N_DEV = 8
N_META = 16
BLK = 128
HEADS = 8
HD = 128
ROPE = 64
Q_RANK = 512
KV_RANK = 256
QH = 2 * HD
D_INP = 4 * HEADS * HD + Q_RANK + KV_RANK + BLK
ROPE_THETA = 10000.0
EPS = 1e-6
ADAM_LR = 0.001
ADAM_B1 = 0.9
ADAM_B2 = 0.999
ADAM_EPS = 1e-08
ADAM_WD = 0.01
ADAM_STEP = 10
V7X_VMEM_LIMIT = 48 * 1024 * 1024
MESH = pl.DeviceIdType.MESH
F32 = jnp.float32
BF16 = jnp.bfloat16

WEIGHTS = ['meta_tokens', 'ffn1_pre_norm', 'ffn1_w_gate', 'ffn1_w_up', 'ffn1_w_down', 'ffn1_post_norm',
           'mix_pre_norm', 'w_in', 'ret_group_norm', 'mla_q_norm', 'mla_w_uq', 'mla_kv_norm', 'mla_w_uk',
           'mla_w_uv', 'w_out', 'mix_post_norm', 'ffn2_pre_norm', 'ffn2_w_gate', 'ffn2_w_up', 'ffn2_w_down',
           'ffn2_post_norm']
SMALL = ['ffn1_pre_norm', 'ffn1_post_norm', 'mix_pre_norm', 'ret_group_norm', 'mla_q_norm', 'mla_kv_norm',
         'mix_post_norm', 'ffn2_pre_norm', 'ffn2_post_norm']
BIG = ['ffn1_w_gate', 'ffn1_w_up', 'ffn1_w_down', 'w_in', 'mla_w_uq', 'mla_w_uk', 'mla_w_uv', 'w_out',
       'ffn2_w_gate', 'ffn2_w_up', 'ffn2_w_down']

_DIMS = {'nn': (((1,), (0,)), ((), ())), 'nt': (((1,), (1,)), ((), ())), 'tn': (((0,), (0,)), ((), ()))}


def _tile(n, target, mult=16):
    best = None
    for t in range(mult, min(n, target) + 1, mult):
        if n % t == 0:
            best = t
    return best if best is not None else n


def _params(sem):
    return pltpu.CompilerParams(dimension_semantics=sem, vmem_limit_bytes=V7X_VMEM_LIMIT)


def _dot(a, b, dims):
    return lax.dot_general(a, b, _DIMS[dims], preferred_element_type=F32)


def _sigmoid(x):
    return 1.0 / (1.0 + jnp.exp(-x))


def _me_and_peers():
    x, y, c = lax.axis_index("x"), lax.axis_index("y"), lax.axis_index("c")

    def peer(j):
        px = 1 - x if (j >> 2) & 1 else x
        py = 1 - y if (j >> 1) & 1 else y
        pc = 1 - c if j & 1 else c
        return (px, py, pc), 4 * px + 2 * py + pc

    return 4 * x + 2 * y + c, peer


def _exchange(name, arrays, per_peer):
    n = len(arrays)
    blocks = [a.shape[1:] if per_peer else a.shape for a in arrays]

    def body(*refs):
        src, dst = refs[:n], refs[n:2 * n]
        send_sems, recv_sems, local_sems = refs[2 * n:]
        me, peer = _me_and_peers()
        started = []
        for k in range(n):
            own = src[k].at[me] if per_peer else src[k]
            local = pltpu.make_async_copy(own, dst[k].at[me], local_sems.at[k])
            local.start()
            started.append(local)
        sends = []
        for k in range(n):
            for j in range(1, N_DEV):
                pid, pidx = peer(j)
                cp = pltpu.make_async_remote_copy(
                    src_ref=src[k].at[pidx] if per_peer else src[k], dst_ref=dst[k].at[me],
                    send_sem=send_sems.at[k * 7 + j - 1], recv_sem=recv_sems.at[k * 7 + j - 1],
                    device_id=pid, device_id_type=MESH)
                cp.start()
                sends.append(cp)
        for k in range(n):
            for j in range(1, N_DEV):
                pid, pidx = peer(j)
                pltpu.make_async_remote_copy(
                    src_ref=src[k].at[pidx] if per_peer else src[k], dst_ref=dst[k].at[pidx],
                    send_sem=send_sems.at[k * 7 + j - 1], recv_sem=recv_sems.at[k * 7 + j - 1],
                    device_id=pid, device_id_type=MESH).wait_recv()
        for cp in sends:
            cp.wait_send()
        for local in started:
            local.wait()

    any_spec = pl.BlockSpec(memory_space=pl.ANY)
    return pl.pallas_call(
        body, name=name,
        out_shape=[jax.ShapeDtypeStruct((N_DEV,) + tuple(b), a.dtype) for a, b in zip(arrays, blocks)],
        in_specs=[any_spec] * n, out_specs=[any_spec] * n,
        scratch_shapes=[pltpu.SemaphoreType.DMA((7 * n,)), pltpu.SemaphoreType.DMA((7 * n,)),
                        pltpu.SemaphoreType.DMA((n,))],
    )(*arrays)


def _allreduce_small(v):
    rows = v.shape[0]

    def body(v_ref, out_ref, buf, send_sems, recv_sems):
        me, peer = _me_and_peers()
        buf[pl.ds(me, 1)] = v_ref[...][None]
        sends = []
        for j in range(1, N_DEV):
            pid, _ = peer(j)
            cp = pltpu.make_async_remote_copy(src_ref=v_ref, dst_ref=buf.at[me], send_sem=send_sems.at[j - 1],
                                              recv_sem=recv_sems.at[j - 1], device_id=pid, device_id_type=MESH)
            cp.start()
            sends.append(cp)
        for j in range(1, N_DEV):
            pid, pidx = peer(j)
            pltpu.make_async_remote_copy(src_ref=v_ref, dst_ref=buf.at[pidx], send_sem=send_sems.at[j - 1],
                                         recv_sem=recv_sems.at[j - 1], device_id=pid,
                                         device_id_type=MESH).wait_recv()
        for cp in sends:
            cp.wait_send()
        acc = buf[0]
        for s in range(1, N_DEV):
            acc = acc + buf[s]
        out_ref[...] = acc

    vm = pl.BlockSpec(memory_space=pltpu.VMEM)
    return pl.pallas_call(
        body, name="allreduce_small", out_shape=jax.ShapeDtypeStruct(v.shape, F32),
        in_specs=[vm], out_specs=vm,
        scratch_shapes=[pltpu.VMEM((N_DEV, rows, 128), F32), pltpu.SemaphoreType.DMA((7,)),
                        pltpu.SemaphoreType.DMA((7,))],
    )(v)


def _mm(name, grid, sem, k_axis, ops, op_specs, pairs, acc_shapes, extras, extra_specs, epilogue, outs, out_specs):
    n_op, n_ex, n_out = len(ops), len(extras), len(outs)
    nk = grid[k_axis] if k_axis is not None else 1

    def body(*refs):
        op_refs = refs[:n_op]
        ex_refs = refs[n_op:n_op + n_ex]
        out_refs = refs[n_op + n_ex:n_op + n_ex + n_out]
        acc_refs = refs[n_op + n_ex + n_out:]
        parts = [None] * len(acc_shapes)
        for li, ri, dims, ai in pairs:
            d = _dot(op_refs[li][...], op_refs[ri][...], dims)
            parts[ai] = d if parts[ai] is None else parts[ai] + d

        def finish(vals):
            res = epilogue(*vals, *[e[...] for e in ex_refs])
            for o, r in zip(out_refs, res):
                o[...] = r.astype(o.dtype)

        if nk == 1:
            finish(parts)
        else:
            k = pl.program_id(k_axis)

            @pl.when(k == 0)
            def _():
                for a, p in zip(acc_refs, parts):
                    a[...] = p

            @pl.when(k > 0)
            def _():
                for a, p in zip(acc_refs, parts):
                    a[...] += p

            @pl.when(k == nk - 1)
            def _():
                finish([a[...] for a in acc_refs])

    scratch = [pltpu.VMEM(s, F32) for s in acc_shapes] if nk > 1 else []
    return pl.pallas_call(
        body, name=name, grid=grid, out_shape=outs,
        in_specs=list(op_specs) + list(extra_specs), out_specs=list(out_specs),
        scratch_shapes=scratch, compiler_params=_params(sem),
    )(*ops, *extras)


def _mm_nn(name, a, w, out_dtype, tm_target=704, tn_target=1664, epilogue=None, extras=(), extra_specs=()):
    L, K = a.shape
    N = w.shape[1]
    tm, tn = _tile(L, tm_target), _tile(N, tn_target, 128)
    ep = epilogue if epilogue is not None else (lambda acc: (acc,))
    return _mm(name, (L // tm, N // tn), ("parallel", "parallel"), None,
               [a, w], [pl.BlockSpec((tm, K), lambda i, j: (i, 0)), pl.BlockSpec((K, tn), lambda i, j: (0, j))],
               [(0, 1, 'nn', 0)], [(tm, tn)], list(extras), list(extra_specs), ep,
               [jax.ShapeDtypeStruct((L, N), out_dtype)], [pl.BlockSpec((tm, tn), lambda i, j: (i, j))])[0]


def _mm_nt(name, pairs_aw, out_dtype, tm_target=704, tn_target=512):
    L = pairs_aw[0][0].shape[0]
    N = pairs_aw[0][1].shape[0]
    tm, tn = _tile(L, tm_target), _tile(N, tn_target, 128)
    ops, specs, pairs = [], [], []
    for t, (a, w) in enumerate(pairs_aw):
        K = a.shape[1]
        ops += [a, w]
        specs += [pl.BlockSpec((tm, K), lambda i, j: (i, 0)), pl.BlockSpec((tn, K), lambda i, j: (j, 0))]
        pairs.append((2 * t, 2 * t + 1, 'nt', 0))
    return _mm(name, (L // tm, N // tn), ("parallel", "parallel"), None, ops, specs, pairs, [(tm, tn)], [], [],
               lambda acc: (acc,), [jax.ShapeDtypeStruct((L, N), out_dtype)],
               [pl.BlockSpec((tm, tn), lambda i, j: (i, j))])[0]


def _mm_tn(name, a, bs, out_dtype=BF16, tk_target=528, tn_target=1664):
    L, M = a.shape
    N = bs[0].shape[1]
    tk, tn = _tile(L, tk_target), _tile(N, tn_target, 128)
    nb = len(bs)
    ops = [a] + list(bs)
    specs = [pl.BlockSpec((tk, M), lambda j, k: (k, 0))] + [pl.BlockSpec((tk, tn), lambda j, k: (k, j))] * nb
    return _mm(name, (N // tn, L // tk), ("parallel", "arbitrary"), 1, ops, specs,
               [(0, 1 + t, 'tn', t) for t in range(nb)], [(M, tn)] * nb, [], [], lambda *acc: acc,
               [jax.ShapeDtypeStruct((M, N), out_dtype)] * nb, [pl.BlockSpec((M, tn), lambda j, k: (0, j))] * nb)


def _norm_fwd(x, w):
    L, D = x.shape
    tr = _tile(L, 512)

    def body(x_ref, w_ref, y_ref):
        v = x_ref[...]
        r = lax.rsqrt(jnp.mean(v * v, axis=-1, keepdims=True) + EPS)
        y_ref[...] = (v * r * w_ref[...]).astype(y_ref.dtype)

    return pl.pallas_call(
        body, name="norm_fwd", grid=(L // tr,), out_shape=jax.ShapeDtypeStruct((L, D), BF16),
        in_specs=[pl.BlockSpec((tr, D), lambda i: (i, 0)), pl.BlockSpec((1, D), lambda i: (0, 0))],
        out_specs=pl.BlockSpec((tr, D), lambda i: (i, 0)), compiler_params=_params(("parallel",)),
    )(x, w)


def _norm_bwd_math(x, w, dy):
    r = lax.rsqrt(jnp.mean(x * x, axis=-1, keepdims=True) + EPS)
    gy = dy * w
    dx = r * (gy - x * (r * r) * jnp.mean(gy * x, axis=-1, keepdims=True))
    dw = jnp.sum(dy * x * r, axis=0, keepdims=True)
    return dx, dw


def _norm_bwd(x, w, dy, res, scale, out_dtype):
    L, D = x.shape
    tr = _tile(L, 384)
    has_res = res is not None

    def body(*refs):
        x_ref, w_ref, dy_ref = refs[:3]
        res_ref = refs[3] if has_res else None
        dx_ref, dw_ref = refs[-2:]
        dx, dw = _norm_bwd_math(x_ref[...], w_ref[...], dy_ref[...].astype(F32))
        dx = scale * dx
        if has_res:
            dx = dx + res_ref[...]
        dx_ref[...] = dx.astype(dx_ref.dtype)

        @pl.when(pl.program_id(0) == 0)
        def _():
            dw_ref[...] = jnp.zeros_like(dw_ref)

        dw_ref[...] += scale * dw

    row = pl.BlockSpec((tr, D), lambda i: (i, 0))
    vec = pl.BlockSpec((1, D), lambda i: (0, 0))
    return pl.pallas_call(
        body, name="norm_bwd", grid=(L // tr,),
        out_shape=[jax.ShapeDtypeStruct((L, D), out_dtype), jax.ShapeDtypeStruct((1, D), F32)],
        in_specs=[row, vec, row] + ([row] if has_res else []), out_specs=[row, vec],
        compiler_params=_params(("arbitrary",)),
    )(*([x, w, dy] + ([res] if has_res else [])))


def _loss(h, target):
    L, D = h.shape

    def body(h_ref, t_ref, dh_ref, loss_ref):
        i = pl.program_id(0)

        @pl.when(i == 0)
        def _():
            dh_ref[...] = jnp.zeros_like(dh_ref)
            loss_ref[...] = jnp.zeros_like(loss_ref)

        @pl.when(i > 0)
        def _():
            diff = h_ref[...] - t_ref[...]
            dh_ref[...] = diff * (1.0 / D)
            loss_ref[...] += 0.5 * jnp.sum(diff * diff) * (1.0 / D)

    return pl.pallas_call(
        body, name="loss", grid=(L // BLK,),
        out_shape=[jax.ShapeDtypeStruct((L, D), F32), jax.ShapeDtypeStruct((8, 128), F32)],
        in_specs=[pl.BlockSpec((BLK, D), lambda i: (i, 0)),
                  pl.BlockSpec((BLK, D), lambda i: (jnp.maximum(i - 1, 0), 0))],
        out_specs=[pl.BlockSpec((BLK, D), lambda i: (i, 0)), pl.BlockSpec((8, 128), lambda i: (0, 0))],
        compiler_params=_params(("arbitrary",)),
    )(h, target)


def _ffn_up(a, wg, wu):
    L, D = a.shape
    F = wg.shape[2]
    tm = _tile(L, 704)

    def ep(g, u):
        return g, u, g * _sigmoid(g) * u

    hspec = pl.BlockSpec((None, tm, F), lambda i, j: (j, i, 0))
    wspec = pl.BlockSpec((None, D, F), lambda i, j: (j, 0, 0))
    return _mm("ffn_up", (L // tm, N_DEV), ("parallel", "parallel"), None,
               [a, wg, wu], [pl.BlockSpec((tm, D), lambda i, j: (i, 0)), wspec, wspec],
               [(0, 1, 'nn', 0), (0, 2, 'nn', 1)], [(tm, F)] * 2, [], [], ep,
               [jax.ShapeDtypeStruct((N_DEV, L, F), BF16)] * 3, [hspec] * 3)


def _resnorm_epilogue(scale):
    def ep(acc, h, w):
        r = lax.rsqrt(jnp.mean(acc * acc, axis=-1, keepdims=True) + EPS)
        return acc, h + scale * (acc * r * w)
    return ep


def _ffn_down(hid, wd, h_in, post):
    _, L, F = hid.shape
    D = wd.shape[2]
    tm = _tile(L, 384)
    row = pl.BlockSpec((tm, D), lambda i, j: (i, 0))
    return _mm("ffn_down", (L // tm, N_DEV), ("parallel", "arbitrary"), 1,
               [hid, wd], [pl.BlockSpec((None, tm, F), lambda i, j: (j, i, 0)),
                           pl.BlockSpec((None, F, D), lambda i, j: (j, 0, 0))],
               [(0, 1, 'nn', 0)], [(tm, D)], [h_in, post], [row, pl.BlockSpec((1, D), lambda i, j: (0, 0))],
               _resnorm_epilogue(0.5), [jax.ShapeDtypeStruct((L, D), F32)] * 2, [row, row])


def _ffn_dhid(df, wd, g, u):
    L, D = df.shape
    F = wd.shape[1]
    tm = _tile(L, 704)

    def ep(dhid, g_, u_):
        g32, u32 = g_.astype(F32), u_.astype(F32)
        sg = _sigmoid(g32)
        return dhid * u32 * sg * (1.0 + g32 * (1.0 - sg)), dhid * g32 * sg

    hspec = pl.BlockSpec((None, tm, F), lambda i, j: (j, i, 0))
    return _mm("ffn_dhid", (L // tm, N_DEV), ("parallel", "parallel"), None,
               [df, wd], [pl.BlockSpec((tm, D), lambda i, j: (i, 0)),
                          pl.BlockSpec((None, F, D), lambda i, j: (j, 0, 0))],
               [(0, 1, 'nt', 0)], [(tm, F)], [g, u], [hspec, hspec], ep,
               [jax.ShapeDtypeStruct((N_DEV, L, F), BF16)] * 2, [hspec, hspec])


def _ffn_dwd(hid, df):
    _, L, F = hid.shape
    D = df.shape[1]
    tk = _tile(L, 528)
    return _mm("ffn_dwd", (N_DEV, L // tk), ("parallel", "arbitrary"), 1,
               [hid, df], [pl.BlockSpec((None, tk, F), lambda j, k: (j, k, 0)),
                           pl.BlockSpec((tk, D), lambda j, k: (k, 0))],
               [(0, 1, 'tn', 0)], [(F, D)], [], [], lambda acc: (acc,),
               [jax.ShapeDtypeStruct((N_DEV, F, D), BF16)], [pl.BlockSpec((None, F, D), lambda j, k: (j, 0, 0))])[0]


def _ffn_dwgu(a, dg, du):
    L, D = a.shape
    F = dg.shape[2]
    tk = _tile(L, 528)
    hspec = pl.BlockSpec((None, tk, F), lambda j, k: (j, k, 0))
    wspec = pl.BlockSpec((None, D, F), lambda j, k: (j, 0, 0))
    return _mm("ffn_dwgu", (N_DEV, L // tk), ("parallel", "arbitrary"), 1,
               [a, dg, du], [pl.BlockSpec((tk, D), lambda j, k: (k, 0)), hspec, hspec],
               [(0, 1, 'tn', 0), (0, 2, 'tn', 1)], [(D, F)] * 2, [], [], lambda *acc: acc,
               [jax.ShapeDtypeStruct((N_DEV, D, F), BF16)] * 2, [wspec, wspec])


def _ffn_da(dg, du, wg, wu):
    _, L, F = dg.shape
    D = wg.shape[1]
    tm = _tile(L, 384)
    hspec = pl.BlockSpec((None, tm, F), lambda i, j: (j, i, 0))
    wspec = pl.BlockSpec((None, D, F), lambda i, j: (j, 0, 0))
    row = pl.BlockSpec((tm, D), lambda i, j: (i, 0))
    return _mm("ffn_da", (L // tm, N_DEV), ("parallel", "arbitrary"), 1,
               [dg, du, wg, wu], [hspec, hspec, wspec, wspec],
               [(0, 2, 'nt', 0), (1, 3, 'nt', 0)], [(tm, D)], [], [], lambda acc: (acc,),
               [jax.ShapeDtypeStruct((L, D), F32)], [row])[0]


def _ffn_fwd(h, pre, post, wg, wu, wd):
    a = _norm_fwd(h, pre)
    g, u, hid = _ffn_up(a, wg, wu)
    f, h_out = _ffn_down(hid, wd, h, post)
    return h_out, (a, g, u, hid, f)


def _ffn_bwd(dh_out, h_in, pre, post, wg, wu, wd, saved):
    a, g, u, hid, f = saved
    df, dpost = _norm_bwd(f, post, dh_out, None, 0.5, BF16)
    dg, du = _ffn_dhid(df, wd, g, u)
    dwd = _ffn_dwd(hid, df)
    dwg, dwu = _ffn_dwgu(a, dg, du)
    da = _ffn_da(dg, du, wg, wu)
    dh_in, dpre = _norm_bwd(h_in, pre, da, dh_out, 1.0, F32)
    return dh_in, dpre, dpost, dwg, dwu, dwd


def _rope_tables(L):
    rows = jnp.arange(L, dtype=F32)
    pos = jnp.where(rows < BLK, rows, rows - (BLK - N_META))
    inv_r = ROPE_THETA ** (-jnp.arange(0, HD, 2, dtype=F32) / HD)
    ang_r = pos[:, None] * inv_r[None, :]
    cr = jnp.concatenate([jnp.cos(ang_r), jnp.cos(ang_r)], axis=1)
    sr = jnp.concatenate([-jnp.sin(ang_r), jnp.sin(ang_r)], axis=1)
    inv_m = ROPE_THETA ** (-jnp.arange(0, ROPE, 2, dtype=F32) / ROPE)
    ang_m = pos[:, None] * inv_m[None, :]
    z32 = jnp.zeros((L, ROPE // 2), F32)
    z64 = jnp.zeros((L, HD - ROPE), F32)
    cm = jnp.concatenate([jnp.cos(ang_m), jnp.cos(ang_m), z64], axis=1)
    sa = jnp.concatenate([-jnp.sin(ang_m), z32, z64], axis=1)
    sb = jnp.concatenate([z32, jnp.sin(ang_m), z64], axis=1)
    return cr, sr, cm, sa, sb


def _rope_ret(x, cr, sr):
    return x * cr + pltpu.roll(x, HD // 2, 1) * sr


def _rope_ret_t(d, cr, sr):
    return d * cr + pltpu.roll(d * sr, HD // 2, 1)


def _rope_mla(x, cm, sa, sb):
    return x * cm + pltpu.roll(x, HD - ROPE // 2, 1) * sa + pltpu.roll(x, ROPE // 2, 1) * sb


def _rope_mla_t(d, cm, sa, sb):
    return d * cm + pltpu.roll(d * sa, ROPE // 2, 1) + pltpu.roll(d * sb, HD - ROPE // 2, 1)


C_RQ, C_RK, C_RV, C_RG = 0, HEADS * HD, 2 * HEADS * HD, 3 * HEADS * HD
C_CQ = 4 * HEADS * HD
C_CKV = C_CQ + Q_RANK
C_KR = C_CKV + KV_RANK
RET_K_SCALE = HD ** -0.5


def _prep(proj, tabs, qn, kvn):
    L = proj.shape[0]
    tr = _tile(L, 256)
    W = HEADS * HD

    def body(p_ref, cr_ref, sr_ref, cm_ref, sa_ref, sb_ref, qn_ref, kvn_ref, q_ref, k_ref, v_ref, cq_ref, ckv_ref,
             kr_ref):
        cr, sr = cr_ref[...], sr_ref[...]
        for h in range(HEADS):
            sl = slice(h * HD, (h + 1) * HD)
            q_ref[:, sl] = _rope_ret(p_ref[:, C_RQ + h * HD:C_RQ + (h + 1) * HD], cr, sr).astype(BF16)
            k_ref[:, sl] = (_rope_ret(p_ref[:, C_RK + h * HD:C_RK + (h + 1) * HD], cr, sr)
                            * RET_K_SCALE).astype(BF16)
        v_ref[...] = p_ref[:, C_RV:C_RV + W].astype(BF16)
        cq = p_ref[:, C_CQ:C_CQ + Q_RANK]
        cq_ref[...] = (cq * lax.rsqrt(jnp.mean(cq * cq, axis=-1, keepdims=True) + EPS) * qn_ref[...]).astype(BF16)
        ckv = p_ref[:, C_CKV:C_CKV + KV_RANK]
        ckv_ref[...] = (ckv * lax.rsqrt(jnp.mean(ckv * ckv, axis=-1, keepdims=True) + EPS)
                        * kvn_ref[...]).astype(BF16)
        kr_ref[...] = _rope_mla(p_ref[:, C_KR:C_KR + HD], cm_ref[...], sa_ref[...], sb_ref[...]).astype(BF16)

    row = lambda w: pl.BlockSpec((tr, w), lambda i: (i, 0))
    vec = lambda w: pl.BlockSpec((1, w), lambda i: (0, 0))
    return pl.pallas_call(
        body, name="mix_prep", grid=(L // tr,),
        out_shape=[jax.ShapeDtypeStruct((L, W), BF16)] * 3 + [jax.ShapeDtypeStruct((L, Q_RANK), BF16),
                                                              jax.ShapeDtypeStruct((L, KV_RANK), BF16),
                                                              jax.ShapeDtypeStruct((L, HD), BF16)],
        in_specs=[row(D_INP)] + [row(HD)] * 5 + [vec(Q_RANK), vec(KV_RANK)],
        out_specs=[row(W)] * 3 + [row(Q_RANK), row(KV_RANK), row(HD)],
        compiler_params=_params(("parallel",)),
    )(proj, *tabs, qn, kvn)


def _prep_bwd(proj, dq, dk, dv, drg, dcqn, dckvn, dkr8, tabs, qn, kvn):
    L = proj.shape[0]
    tr = _tile(L, 192)
    W = HEADS * HD

    def body(p_ref, dq_ref, dk_ref, dv_ref, drg_ref, dcq_ref, dckv_ref, dkr_ref, cr_ref, sr_ref, cm_ref, sa_ref,
             sb_ref, qn_ref, kvn_ref, dp_ref, dqn_ref, dkvn_ref):
        cr, sr = cr_ref[...], sr_ref[...]
        dkr = None
        for h in range(HEADS):
            sl = slice(h * HD, (h + 1) * HD)
            dp_ref[:, C_RQ + h * HD:C_RQ + (h + 1) * HD] = _rope_ret_t(dq_ref[:, sl], cr, sr).astype(BF16)
            dp_ref[:, C_RK + h * HD:C_RK + (h + 1) * HD] = (_rope_ret_t(dk_ref[:, sl], cr, sr)
                                                            * RET_K_SCALE).astype(BF16)
            part = dkr_ref[:, sl]
            dkr = part if dkr is None else dkr + part
        dp_ref[:, C_RV:C_RV + W] = dv_ref[...].astype(BF16)
        dp_ref[:, C_RG:C_RG + W] = drg_ref[...].astype(BF16)
        dcq, dqn = _norm_bwd_math(p_ref[:, C_CQ:C_CQ + Q_RANK], qn_ref[...], dcq_ref[...])
        dp_ref[:, C_CQ:C_CQ + Q_RANK] = dcq.astype(BF16)
        dckv, dkvn = _norm_bwd_math(p_ref[:, C_CKV:C_CKV + KV_RANK], kvn_ref[...], dckv_ref[...])
        dp_ref[:, C_CKV:C_CKV + KV_RANK] = dckv.astype(BF16)
        dp_ref[:, C_KR:C_KR + HD] = _rope_mla_t(dkr, cm_ref[...], sa_ref[...], sb_ref[...]).astype(BF16)

        @pl.when(pl.program_id(0) == 0)
        def _():
            dqn_ref[...] = jnp.zeros_like(dqn_ref)
            dkvn_ref[...] = jnp.zeros_like(dkvn_ref)

        dqn_ref[...] += dqn
        dkvn_ref[...] += dkvn

    row = lambda w: pl.BlockSpec((tr, w), lambda i: (i, 0))
    vec = lambda w: pl.BlockSpec((1, w), lambda i: (0, 0))
    return pl.pallas_call(
        body, name="mix_prep_bwd", grid=(L // tr,),
        out_shape=[jax.ShapeDtypeStruct((L, D_INP), BF16), jax.ShapeDtypeStruct((1, Q_RANK), F32),
                   jax.ShapeDtypeStruct((1, KV_RANK), F32)],
        in_specs=[row(D_INP)] + [row(W)] * 4 + [row(Q_RANK), row(KV_RANK), row(W)] + [row(HD)] * 5
                 + [vec(Q_RANK), vec(KV_RANK)],
        out_specs=[row(D_INP), vec(Q_RANK), vec(KV_RANK)],
        compiler_params=_params(("arbitrary",)),
    )(proj, dq, dk, dv, drg, dcqn, dckvn, dkr8, *tabs, qn, kvn)


def _post(o_ret, proj, gn):
    L, W = o_ret.shape
    tr = _tile(L, 384)

    def body(o_ref, rg_ref, gn_ref, out_ref):
        for h in range(HEADS):
            sl = slice(h * HD, (h + 1) * HD)
            o = o_ref[:, sl]
            rg = rg_ref[:, sl]
            n = o * lax.rsqrt(jnp.mean(o * o, axis=-1, keepdims=True) + EPS)
            out_ref[:, sl] = (n * gn_ref[:, sl] * (rg * _sigmoid(rg))).astype(BF16)

    row = pl.BlockSpec((tr, W), lambda i: (i, 0))
    return pl.pallas_call(
        body, name="ret_post", grid=(L // tr,), out_shape=jax.ShapeDtypeStruct((L, W), BF16),
        in_specs=[row, pl.BlockSpec((tr, W), lambda i: (i, C_RG // W)), pl.BlockSpec((1, W), lambda i: (0, 0))],
        out_specs=row, compiler_params=_params(("parallel",)),
    )(o_ret, proj, gn)


def _post_bwd(o_ret, proj, gn, dcat):
    L, W = o_ret.shape
    tr = _tile(L, 384)

    def body(o_ref, rg_ref, gn_ref, d_ref, do_ref, drg_ref, dgn_ref):
        @pl.when(pl.program_id(0) == 0)
        def _():
            dgn_ref[...] = jnp.zeros_like(dgn_ref)

        for h in range(HEADS):
            sl = slice(h * HD, (h + 1) * HD)
            o = o_ref[:, sl]
            rg = rg_ref[:, sl]
            d = d_ref[:, sl].astype(F32)
            gw = gn_ref[:, sl]
            r = lax.rsqrt(jnp.mean(o * o, axis=-1, keepdims=True) + EPS)
            n = o * r
            sg = _sigmoid(rg)
            si = rg * sg
            dn = d * gw * si
            dgn_ref[:, sl] += jnp.sum(d * n * si, axis=0, keepdims=True)
            drg_ref[:, sl] = d * n * gw * sg * (1.0 + rg * (1.0 - sg))
            do_ref[:, sl] = (r * (dn - o * (r * r) * jnp.mean(dn * o, axis=-1, keepdims=True))).astype(BF16)

    row = pl.BlockSpec((tr, W), lambda i: (i, 0))
    vec = pl.BlockSpec((1, W), lambda i: (0, 0))
    return pl.pallas_call(
        body, name="ret_post_bwd", grid=(L // tr,),
        out_shape=[jax.ShapeDtypeStruct((L, W), BF16), jax.ShapeDtypeStruct((L, W), F32),
                   jax.ShapeDtypeStruct((1, W), F32)],
        in_specs=[row, pl.BlockSpec((tr, W), lambda i: (i, C_RG // W)), vec, row],
        out_specs=[row, row, vec], compiler_params=_params(("arbitrary",)),
    )(o_ret, proj, gn, dcat)


def _lin_attn(name, q, k, v, lg, reverse):
    L, W = q.shape
    nc = L // BLK - 1

    def body(q_ref, k_ref, v_ref, lg_ref, o_ref, s_ref):
        lgv = lg_ref[0:1, :]
        n = lax.broadcasted_iota(jnp.int32, (BLK, BLK), 0).astype(F32)
        m = lax.broadcasted_iota(jnp.int32, (BLK, BLK), 1).astype(F32)
        dist = (m - n) if reverse else (n - m)
        dmask = jnp.where(dist >= 0, jnp.exp(lgv * jnp.maximum(dist, 0.0)), 0.0)
        dmask0 = jnp.where((n < N_META) & (m < N_META), dmask, 0.0)
        gl = jnp.exp(lgv * float(BLK))
        if reverse:
            inter = jnp.exp(lgv * (float(BLK) - n))
            inter0 = jnp.where(n < N_META, jnp.exp(lgv * jnp.maximum(float(N_META) - n, 0.0)), 0.0)
            upd = jnp.exp(lgv * n)
        else:
            inter = jnp.exp(lgv * (n + 1.0))
            upd = jnp.exp(lgv * (float(BLK) - 1.0 - n))
            upd0 = jnp.where(n < N_META, jnp.exp(lgv * jnp.maximum(float(N_META) - 1.0 - n, 0.0)), 0.0)

        def chunk(c):
            rows = pl.ds(pl.multiple_of(c * BLK, BLK), BLK)
            qc, kc, vc = q_ref[rows, :], k_ref[rows, :], v_ref[rows, :]
            a = _dot(qc, kc, 'nt') * dmask
            o = _dot(a.astype(BF16), vc, 'nn') + _dot(qc, s_ref[...].astype(BF16), 'nn') * inter
            o_ref[rows, :] = o
            s_ref[...] = s_ref[...] * gl + _dot((kc.astype(F32) * upd).astype(BF16), vc, 'tn')

        q0, k0, v0 = q_ref[0:BLK, :], k_ref[0:BLK, :], v_ref[0:BLK, :]
        a0 = _dot(q0, k0, 'nt') * dmask0
        if reverse:
            s_ref[...] = jnp.zeros_like(s_ref)

            def step(t, carry):
                chunk(nc - t)
                return carry

            lax.fori_loop(0, nc, step, 0)
            o_ref[0:BLK, :] = _dot(a0.astype(BF16), v0, 'nn') + _dot(q0, s_ref[...].astype(BF16), 'nn') * inter0
        else:
            o_ref[0:BLK, :] = _dot(a0.astype(BF16), v0, 'nn')
            s_ref[...] = _dot((k0.astype(F32) * upd0).astype(BF16), v0, 'tn')

            def step(t, carry):
                chunk(t + 1)
                return carry

            lax.fori_loop(0, nc, step, 0)

    col = pl.BlockSpec((L, HD), lambda h: (0, h))
    return pl.pallas_call(
        body, name=name, grid=(HEADS,), out_shape=jax.ShapeDtypeStruct((L, W), F32),
        in_specs=[col, col, col, pl.BlockSpec((None, 8, HD), lambda h: (h, 0, 0))], out_specs=col,
        scratch_shapes=[pltpu.VMEM((HD, HD), F32)], compiler_params=_params(("parallel",)),
    )(q, k, v, lg)


ATT_SCALE = (HD + ROPE) ** -0.5
NEG = -1e30


def _att_mask(i, j):
    rows = i * BLK + lax.broadcasted_iota(jnp.int32, (BLK, BLK), 0)
    cols = j * BLK + lax.broadcasted_iota(jnp.int32, (BLK, BLK), 1)
    return (cols <= rows) & ((cols < N_META) | (cols >= BLK))


def _attn_fwd(qm, kn, krr, vm):
    L = qm.shape[0]
    W = HEADS * HD
    nb = L // BLK

    def body(q_ref, kn_ref, kr_ref, v_ref, o_ref, lse_ref):
        i = pl.program_id(1)
        q = q_ref[...]

        def step(j, carry):
            m, l, acc = carry
            rows = pl.ds(pl.multiple_of(j * BLK, BLK), BLK)
            k = jnp.concatenate([kn_ref[rows, :], kr_ref[rows, :]], axis=1)
            s = jnp.where(_att_mask(i, j), _dot(q, k, 'nt') * ATT_SCALE, NEG)
            m_new = jnp.maximum(m, jnp.max(s, axis=-1, keepdims=True))
            p = jnp.exp(s - m_new)
            alpha = jnp.exp(m - m_new)
            return (m_new, alpha * l + jnp.sum(p, axis=-1, keepdims=True),
                    alpha * acc + _dot(p.astype(BF16), v_ref[rows, :], 'nn'))

        m, l, acc = lax.fori_loop(0, i + 1, step, (jnp.full((BLK, 1), NEG, F32), jnp.zeros((BLK, 1), F32),
                                                   jnp.zeros((BLK, HD), F32)))
        o_ref[...] = (acc / l).astype(o_ref.dtype)
        lse_ref[...] = jnp.broadcast_to(m + jnp.log(l), (BLK, HD))

    return pl.pallas_call(
        body, name="attn_fwd", grid=(HEADS, nb),
        out_shape=[jax.ShapeDtypeStruct((L, W), BF16), jax.ShapeDtypeStruct((HEADS, L, HD), F32)],
        in_specs=[pl.BlockSpec((BLK, QH), lambda h, i: (i, h)), pl.BlockSpec((L, HD), lambda h, i: (0, h)),
                  pl.BlockSpec((L, HD), lambda h, i: (0, 0)), pl.BlockSpec((L, HD), lambda h, i: (0, h))],
        out_specs=[pl.BlockSpec((BLK, HD), lambda h, i: (i, h)), pl.BlockSpec((None, BLK, HD), lambda h, i: (h, i, 0))],
        compiler_params=_params(("parallel", "parallel")),
    )(qm, kn, krr, vm)


def _attn_dq(qm, kn, krr, vm, o, dcat, lse, tabs_m):
    L = qm.shape[0]
    nb = L // BLK

    def body(q_ref, kn_ref, kr_ref, v_ref, o_ref, do_ref, lse_ref, cm_ref, sa_ref, sb_ref, dq_ref, dl_ref):
        i = pl.program_id(1)
        q = q_ref[...]
        do = do_ref[...]
        delta = jnp.sum(do.astype(F32) * o_ref[...].astype(F32), axis=-1, keepdims=True)
        lse = lse_ref[:, 0:1]

        def step(j, dq):
            rows = pl.ds(pl.multiple_of(j * BLK, BLK), BLK)
            k = jnp.concatenate([kn_ref[rows, :], kr_ref[rows, :]], axis=1)
            s = jnp.where(_att_mask(i, j), _dot(q, k, 'nt') * ATT_SCALE, NEG)
            p = jnp.exp(s - lse)
            ds = p * (_dot(do, v_ref[rows, :], 'nt') - delta) * ATT_SCALE
            return dq + _dot(ds.astype(BF16), k, 'nn')

        dq = lax.fori_loop(0, i + 1, step, jnp.zeros((BLK, QH), F32))
        dq_ref[:, 0:HD] = dq[:, 0:HD].astype(BF16)
        dq_ref[:, HD:QH] = _rope_mla_t(dq[:, HD:QH], cm_ref[...], sa_ref[...], sb_ref[...]).astype(BF16)
        dl_ref[...] = jnp.broadcast_to(delta, (BLK, HD))

    tab = pl.BlockSpec((BLK, HD), lambda h, i: (i, 0))
    stat = pl.BlockSpec((None, BLK, HD), lambda h, i: (h, i, 0))
    return pl.pallas_call(
        body, name="attn_dq", grid=(HEADS, nb),
        out_shape=[jax.ShapeDtypeStruct((L, HEADS * QH), BF16), jax.ShapeDtypeStruct((HEADS, L, HD), F32)],
        in_specs=[pl.BlockSpec((BLK, QH), lambda h, i: (i, h)), pl.BlockSpec((L, HD), lambda h, i: (0, h)),
                  pl.BlockSpec((L, HD), lambda h, i: (0, 0)), pl.BlockSpec((L, HD), lambda h, i: (0, h)),
                  pl.BlockSpec((BLK, HD), lambda h, i: (i, h)), pl.BlockSpec((BLK, HD), lambda h, i: (i, HEADS + h)),
                  stat, tab, tab, tab],
        out_specs=[pl.BlockSpec((BLK, QH), lambda h, i: (i, h)), stat],
        compiler_params=_params(("parallel", "parallel")),
    )(qm, kn, krr, vm, o, dcat, lse, *tabs_m)


def _attn_dkv(qm, kn, krr, vm, dcat, lse, delta):
    L = qm.shape[0]
    W = HEADS * HD
    nb = L // BLK

    def body(q_ref, kn_ref, kr_ref, v_ref, do_ref, lse_ref, dl_ref, dkn_ref, dkr_ref, dv_ref):
        j = pl.program_id(1)
        k = jnp.concatenate([kn_ref[...], kr_ref[...]], axis=1)
        v = v_ref[...]

        def step(i, carry):
            dk, dv = carry
            rows = pl.ds(pl.multiple_of(i * BLK, BLK), BLK)
            q = q_ref[rows, :]
            do = do_ref[rows, :]
            s = jnp.where(_att_mask(i, j), _dot(q, k, 'nt') * ATT_SCALE, NEG)
            p = jnp.exp(s - lse_ref[rows, 0:1])
            ds = p * (_dot(do, v, 'nt') - dl_ref[rows, 0:1]) * ATT_SCALE
            return dk + _dot(ds.astype(BF16), q, 'tn'), dv + _dot(p.astype(BF16), do, 'tn')

        dk, dv = lax.fori_loop(j, nb, step, (jnp.zeros((BLK, QH), F32), jnp.zeros((BLK, HD), F32)))
        dkn_ref[...] = dk[:, 0:HD].astype(BF16)
        dkr_ref[...] = dk[:, HD:QH]
        dv_ref[...] = dv.astype(BF16)

    blk = pl.BlockSpec((BLK, HD), lambda h, j: (j, h))
    stat = pl.BlockSpec((None, L, HD), lambda h, j: (h, 0, 0))
    return pl.pallas_call(
        body, name="attn_dkv", grid=(HEADS, nb),
        out_shape=[jax.ShapeDtypeStruct((L, W), BF16), jax.ShapeDtypeStruct((L, W), F32),
                   jax.ShapeDtypeStruct((L, W), BF16)],
        in_specs=[pl.BlockSpec((L, QH), lambda h, j: (0, h)), blk, pl.BlockSpec((BLK, HD), lambda h, j: (j, 0)), blk,
                  pl.BlockSpec((L, HD), lambda h, j: (0, HEADS + h)), stat, stat],
        out_specs=[blk, blk, blk], compiler_params=_params(("parallel", "parallel")),
    )(qm, kn, krr, vm, dcat, lse, delta)


def _q_up(cqn, wuq_p, tabs_m):
    L = cqn.shape[0]
    tm = _tile(L, 704)

    def ep(acc, cm, sa, sb):
        parts = []
        for h in range(HEADS):
            parts.append(acc[:, h * QH:h * QH + HD])
            parts.append(_rope_mla(acc[:, h * QH + HD:(h + 1) * QH], cm, sa, sb))
        return (jnp.concatenate(parts, axis=1),)

    tab = pl.BlockSpec((tm, HD), lambda i, j: (i, 0))
    return _mm("mla_q_up", (L // tm, 1), ("parallel", "parallel"), None,
               [cqn, wuq_p], [pl.BlockSpec((tm, Q_RANK), lambda i, j: (i, 0)),
                              pl.BlockSpec((Q_RANK, HEADS * QH), lambda i, j: (0, 0))],
               [(0, 1, 'nn', 0)], [(tm, HEADS * QH)], list(tabs_m), [tab] * 3, ep,
               [jax.ShapeDtypeStruct((L, HEADS * QH), BF16)], [pl.BlockSpec((tm, HEADS * QH), lambda i, j: (i, 0))])[0]


def _mix_out(cat, w_out, h_in, post):
    L, K = cat.shape
    D = w_out.shape[1]
    tm, tk = _tile(L, 384), _tile(K, 512, 128)
    row = pl.BlockSpec((tm, D), lambda i, k: (i, 0))
    return _mm("mix_out", (L // tm, K // tk), ("parallel", "arbitrary"), 1,
               [cat, w_out], [pl.BlockSpec((tm, tk), lambda i, k: (i, k)), pl.BlockSpec((tk, D), lambda i, k: (k, 0))],
               [(0, 1, 'nn', 0)], [(tm, D)], [h_in, post], [row, pl.BlockSpec((1, D), lambda i, k: (0, 0))],
               _resnorm_epilogue(1.0), [jax.ShapeDtypeStruct((L, D), F32)] * 2, [row, row])


def _mixer_fwd(h, p, W, tabs, lg):
    u = _norm_fwd(h, p['mix_pre_norm'])
    proj = _mm_nn("mix_in", u, W['w_in'], F32)
    qr, kr, vr, cqn, ckvn, krr = _prep(proj, tabs, p['mla_q_norm'], p['mla_kv_norm'])
    qm = _q_up(cqn, W['wuq'], tabs[2:])
    kn = _mm_nn("mla_k_up", ckvn, W['wuk'], BF16)
    vm = _mm_nn("mla_v_up", ckvn, W['wuv'], BF16)
    o_mla, lse = _attn_fwd(qm, kn, krr, vm)
    o_ret = _lin_attn("ret_fwd", qr, kr, vr, lg, False)
    ret = _post(o_ret, proj, p['ret_group_norm'])
    cat = jnp.concatenate([ret, o_mla], axis=1)
    m, h_out = _mix_out(cat, W['w_out'], h, p['mix_post_norm'])
    return h_out, (u, proj, qr, kr, vr, cqn, ckvn, krr, qm, kn, vm, o_mla, lse, o_ret, cat, m)


def _mixer_bwd(dh_out, h_in, p, W, tabs, lg, saved):
    u, proj, qr, kr, vr, cqn, ckvn, krr, qm, kn, vm, o_mla, lse, o_ret, cat, m = saved
    dm, dmixpost = _norm_bwd(m, p['mix_post_norm'], dh_out, None, 1.0, BF16)
    dcat = _mm_nt("mix_dcat", [(dm, W['w_out'])], BF16)
    dwout = _mm_tn("mix_dwout", cat, [dm])[0]
    do_ret, drg, dgn = _post_bwd(o_ret, proj, p['ret_group_norm'], dcat)
    dqr = _lin_attn("ret_dq", do_ret, vr, kr, lg, False)
    dkr = _lin_attn("ret_dk", vr, do_ret, qr, lg, True)
    dvr = _lin_attn("ret_dv", kr, qr, do_ret, lg, True)
    dqp, delta = _attn_dq(qm, kn, krr, vm, o_mla, dcat, lse, tabs[2:])
    dkn, dkr8, dvm = _attn_dkv(qm, kn, krr, vm, dcat, lse, delta)
    dwuq = _mm_tn("mla_dwuq", cqn, [dqp])[0]
    dcqn = _mm_nt("mla_dcq", [(dqp, W['wuq'])], F32)
    dwuk, dwuv = _mm_tn("mla_dwukv", ckvn, [dkn, dvm])
    dckvn = _mm_nt("mla_dckv", [(dkn, W['wuk']), (dvm, W['wuv'])], F32)
    dproj, dqn, dkvn = _prep_bwd(proj, dqr, dkr, dvr, drg, dcqn, dckvn, dkr8, tabs, p['mla_q_norm'],
                                 p['mla_kv_norm'])
    dwin = _mm_tn("mix_dwin", u, [dproj])[0]
    du = _mm_nt("mix_du", [(dproj, W['w_in'])], F32)
    dh_in, dmixpre = _norm_bwd(h_in, p['mix_pre_norm'], du, dh_out, 1.0, F32)
    return dh_in, dict(mix_pre_norm=dmixpre, mix_post_norm=dmixpost, ret_group_norm=dgn, mla_q_norm=dqn,
                       mla_kv_norm=dkvn), dwin, dwuq, dwuk, dwuv, dwout


def _adam_math(w, g, m, v):
    m = ADAM_B1 * m + (1.0 - ADAM_B1) * g
    v = ADAM_B2 * v + (1.0 - ADAM_B2) * (g * g)
    m_hat = m / (1.0 - ADAM_B1 ** ADAM_STEP)
    v_hat = v / (1.0 - ADAM_B2 ** ADAM_STEP)
    delta = -ADAM_LR * (m_hat / (jnp.sqrt(v_hat) + ADAM_EPS) + ADAM_WD * w)
    return delta, m, v


def _adam(name, w, m, v, g_slots=None, g=None):
    R, C = w.shape
    tr = _tile(R, 256, 8)
    from_slots = g_slots is not None

    def body(w_ref, m_ref, v_ref, g_ref, go_ref, d_ref, mo_ref, vo_ref):
        if from_slots:
            grad = g_ref[0].astype(F32)
            for s in range(1, N_DEV):
                grad = grad + g_ref[s].astype(F32)
        else:
            grad = g_ref[...]
        delta, mn, vn = _adam_math(w_ref[...], grad, m_ref[...], v_ref[...])
        go_ref[...] = grad
        d_ref[...] = delta
        mo_ref[...] = mn
        vo_ref[...] = vn

    row = pl.BlockSpec((tr, C), lambda i: (i, 0))
    gspec = pl.BlockSpec((N_DEV, tr, C), lambda i: (0, i, 0)) if from_slots else row
    return pl.pallas_call(
        body, name=name, grid=(R // tr,), out_shape=[jax.ShapeDtypeStruct((R, C), F32)] * 4,
        in_specs=[row, row, row, gspec], out_specs=[row] * 4, compiler_params=_params(("parallel",)),
    )(w, m, v, g_slots if from_slots else g)


def _unblock(gathered):
    n, r, c = gathered.shape
    return jnp.transpose(gathered, (1, 0, 2)).reshape(r, n * c)


def _reblock(full, c):
    r = full.shape[0]
    return jnp.transpose(full[:, :N_DEV * c].reshape(r, N_DEV, c), (1, 0, 2))


def _step(x, target, w, mom, vel):
    S, D = x.shape[1], x.shape[2]
    L = S + BLK
    sq = lambda a: a.reshape(a.shape[1:]) if a.ndim == 3 else a
    p = {n: sq(w[n]) for n in WEIGHTS if n != 'meta_tokens'}

    names = BIG + ['meta_tokens']
    shards = [p[n].astype(BF16) for n in BIG] + [w['meta_tokens']]
    G = dict(zip(names, _exchange("gather_weights", shards, False)))
    in_s = p['w_in'].shape[1]
    uq_s = p['mla_w_uq'].shape[1]
    w_in_full = _unblock(G['w_in'])
    Wm = dict(
        w_in=jnp.pad(w_in_full, ((0, 0), (0, D_INP - w_in_full.shape[1]))),
        wuq=jnp.pad(_unblock(G['mla_w_uq']).reshape(Q_RANK, HEADS, HD + ROPE),
                    ((0, 0), (0, 0), (0, QH - HD - ROPE))).reshape(Q_RANK, HEADS * QH),
        wuk=_unblock(G['mla_w_uk']), wuv=_unblock(G['mla_w_uv']),
        w_out=G['w_out'].reshape(-1, D))
    meta = _unblock(G['meta_tokens'])

    tabs = _rope_tables(L)
    lg = jnp.broadcast_to(jnp.log(1.0 - 2.0 ** (-5.0 - jnp.arange(HEADS, dtype=F32)))[:, None, None], (HEADS, 8, HD))

    h0 = jnp.concatenate([meta, jnp.zeros((BLK - N_META, D), F32), x[0]], axis=0)
    h1, s1 = _ffn_fwd(h0, p['ffn1_pre_norm'], p['ffn1_post_norm'], G['ffn1_w_gate'], G['ffn1_w_up'], G['ffn1_w_down'])
    h2, sm = _mixer_fwd(h1, p, Wm, tabs, lg)
    h3, s2 = _ffn_fwd(h2, p['ffn2_pre_norm'], p['ffn2_post_norm'], G['ffn2_w_gate'], G['ffn2_w_up'], G['ffn2_w_down'])
    dh3, loss_blk = _loss(h3, target[0])

    dh2, dpre2, dpost2, dwg2, dwu2, dwd2 = _ffn_bwd(dh3, h2, p['ffn2_pre_norm'], p['ffn2_post_norm'],
                                                    G['ffn2_w_gate'], G['ffn2_w_up'], G['ffn2_w_down'], s2)
    dh1, dsmall, dwin, dwuq, dwuk, dwuv, dwout = _mixer_bwd(dh2, h1, p, Wm, tabs, lg, sm)
    dh0, dpre1, dpost1, dwg1, dwu1, dwd1 = _ffn_bwd(dh1, h0, p['ffn1_pre_norm'], p['ffn1_post_norm'],
                                                    G['ffn1_w_gate'], G['ffn1_w_up'], G['ffn1_w_down'], s1)
    dsmall.update(ffn1_pre_norm=dpre1, ffn1_post_norm=dpost1, ffn2_pre_norm=dpre2, ffn2_post_norm=dpost2)

    pieces = [dsmall[n].reshape(-1, 128) for n in SMALL] + [dh0[:N_META].reshape(-1, 128), loss_blk]
    packed = jnp.concatenate(pieces, axis=0)
    packed = jnp.pad(packed, ((0, (-packed.shape[0]) % 8), (0, 0)))
    red = _allreduce_small(packed)
    offs = 0
    gsmall = {}
    for n in SMALL:
        r = p[n].shape[-1] // 128
        gsmall[n] = red[offs:offs + r]
        offs += r
    n_small = offs
    gmeta_full = red[offs:offs + N_META * D // 128].reshape(N_META, D)
    offs += N_META * D // 128
    loss = red[offs, 0]

    partial = dict(
        ffn1_w_gate=dwg1, ffn1_w_up=dwu1, ffn1_w_down=dwd1, ffn2_w_gate=dwg2, ffn2_w_up=dwu2, ffn2_w_down=dwd2,
        w_in=_reblock(dwin, in_s),
        mla_w_uq=_reblock(dwuq.reshape(Q_RANK, HEADS, QH)[:, :, :HD + ROPE].reshape(Q_RANK, HEADS * (HD + ROPE)), uq_s),
        mla_w_uk=_reblock(dwuk, p['mla_w_uk'].shape[1]), mla_w_uv=_reblock(dwuv, p['mla_w_uv'].shape[1]),
        w_out=dwout.reshape(N_DEV, -1, D))
    R = dict(zip(BIG, _exchange("scatter_grads", [partial[n] for n in BIG], True)))

    grad, delta, new_m, new_v = {}, {}, {}, {}
    for n in BIG:
        outs = _adam("adam_" + n, p[n], sq(mom[n]), sq(vel[n]), g_slots=R[n])
        grad[n], delta[n], new_m[n], new_v[n] = [o.reshape(w[n].shape) for o in outs]
    pack = lambda d, f: jnp.concatenate([f(d[n]).reshape(-1, 128) for n in SMALL], axis=0)
    outs = _adam("adam_small", pack(w, sq), pack(mom, sq), pack(vel, sq), g=red[:n_small])
    offs = 0
    for n in SMALL:
        r = p[n].shape[-1] // 128
        grad[n], delta[n], new_m[n], new_v[n] = [o[offs:offs + r].reshape(w[n].shape) for o in outs]
        offs += r
    dev = 4 * lax.axis_index("x") + 2 * lax.axis_index("y") + lax.axis_index("c")
    mcols = w['meta_tokens'].shape[1]
    gmeta = lax.dynamic_slice(gmeta_full, (0, dev * mcols), (N_META, mcols))
    outs = _adam("adam_meta", w['meta_tokens'], mom['meta_tokens'], vel['meta_tokens'], g=gmeta)
    grad['meta_tokens'], delta['meta_tokens'], new_m['meta_tokens'], new_v['meta_tokens'] = outs

    return (loss, dh0[BLK:][None], *[grad[n] for n in WEIGHTS], *[delta[n] for n in WEIGHTS],
            *[new_m[n] for n in WEIGHTS], *[new_v[n] for n in WEIGHTS])


def kernel(x, meta_tokens, ffn1_pre_norm, ffn1_w_gate, ffn1_w_up, ffn1_w_down, ffn1_post_norm, mix_pre_norm, w_in, ret_group_norm, mla_q_norm, mla_w_uq, mla_kv_norm, mla_w_uk, mla_w_uv, w_out, mix_post_norm, ffn2_pre_norm, ffn2_w_gate, ffn2_w_up, ffn2_w_down, ffn2_post_norm, loss_target, m_meta_tokens, m_ffn1_pre_norm, m_ffn1_w_gate, m_ffn1_w_up, m_ffn1_w_down, m_ffn1_post_norm, m_mix_pre_norm, m_w_in, m_ret_group_norm, m_mla_q_norm, m_mla_w_uq, m_mla_kv_norm, m_mla_w_uk, m_mla_w_uv, m_w_out, m_mix_post_norm, m_ffn2_pre_norm, m_ffn2_w_gate, m_ffn2_w_up, m_ffn2_w_down, m_ffn2_post_norm, v_meta_tokens, v_ffn1_pre_norm, v_ffn1_w_gate, v_ffn1_w_up, v_ffn1_w_down, v_ffn1_post_norm, v_mix_pre_norm, v_w_in, v_ret_group_norm, v_mla_q_norm, v_mla_w_uq, v_mla_kv_norm, v_mla_w_uk, v_mla_w_uv, v_w_out, v_mix_post_norm, v_ffn2_pre_norm, v_ffn2_w_gate, v_ffn2_w_up, v_ffn2_w_down, v_ffn2_post_norm):
    w = dict(zip(WEIGHTS, (meta_tokens, ffn1_pre_norm, ffn1_w_gate, ffn1_w_up, ffn1_w_down, ffn1_post_norm,
                           mix_pre_norm, w_in, ret_group_norm, mla_q_norm, mla_w_uq, mla_kv_norm, mla_w_uk, mla_w_uv,
                           w_out, mix_post_norm, ffn2_pre_norm, ffn2_w_gate, ffn2_w_up, ffn2_w_down, ffn2_post_norm)))
    mom = dict(zip(WEIGHTS, (m_meta_tokens, m_ffn1_pre_norm, m_ffn1_w_gate, m_ffn1_w_up, m_ffn1_w_down,
                             m_ffn1_post_norm, m_mix_pre_norm, m_w_in, m_ret_group_norm, m_mla_q_norm, m_mla_w_uq,
                             m_mla_kv_norm, m_mla_w_uk, m_mla_w_uv, m_w_out, m_mix_post_norm, m_ffn2_pre_norm,
                             m_ffn2_w_gate, m_ffn2_w_up, m_ffn2_w_down, m_ffn2_post_norm)))
    vel = dict(zip(WEIGHTS, (v_meta_tokens, v_ffn1_pre_norm, v_ffn1_w_gate, v_ffn1_w_up, v_ffn1_w_down,
                             v_ffn1_post_norm, v_mix_pre_norm, v_w_in, v_ret_group_norm, v_mla_q_norm, v_mla_w_uq,
                             v_mla_kv_norm, v_mla_w_uk, v_mla_w_uv, v_w_out, v_mix_post_norm, v_ffn2_pre_norm,
                             v_ffn2_w_gate, v_ffn2_w_up, v_ffn2_w_down, v_ffn2_post_norm)))
    return _step(x, loss_target, w, mom, vel)
```

```python
import functools
import math

import jax
import jax.numpy as jnp
from jax import lax
from jax.experimental import pallas as pl
from jax.experimental.pallas import tpu as pltpu

N_DEV = 8
N_META = 16
BLK = 128
HEADS = 8
HD = 128
ROPE = 64
Q_RANK = 512
KV_RANK = 256
QH = 2 * HD
D_INP = 4 * HEADS * HD + Q_RANK + KV_RANK + BLK
ROPE_THETA = 10000.0
EPS = 1e-6
ADAM_LR = 0.001
ADAM_B1 = 0.9
ADAM_B2 = 0.999
ADAM_EPS = 1e-08
ADAM_WD = 0.01
ADAM_STEP = 10
V7X_VMEM_LIMIT = 48 * 1024 * 1024
MESH = pl.DeviceIdType.MESH
F32 = jnp.float32
BF16 = jnp.bfloat16

WEIGHTS = ['meta_tokens', 'ffn1_pre_norm', 'ffn1_w_gate', 'ffn1_w_up', 'ffn1_w_down', 'ffn1_post_norm',
           'mix_pre_norm', 'w_in', 'ret_group_norm', 'mla_q_norm', 'mla_w_uq', 'mla_kv_norm', 'mla_w_uk',
           'mla_w_uv', 'w_out', 'mix_post_norm', 'ffn2_pre_norm', 'ffn2_w_gate', 'ffn2_w_up', 'ffn2_w_down',
           'ffn2_post_norm']
SMALL = ['ffn1_pre_norm', 'ffn1_post_norm', 'mix_pre_norm', 'ret_group_norm', 'mla_q_norm', 'mla_kv_norm',
         'mix_post_norm', 'ffn2_pre_norm', 'ffn2_post_norm']
BIG = ['ffn1_w_gate', 'ffn1_w_up', 'ffn1_w_down', 'w_in', 'mla_w_uq', 'mla_w_uk', 'mla_w_uv', 'w_out',
       'ffn2_w_gate', 'ffn2_w_up', 'ffn2_w_down']

_DIMS = {'nn': (((1,), (0,)), ((), ())), 'nt': (((1,), (1,)), ((), ())), 'tn': (((0,), (0,)), ((), ()))}


def _tile(n, target, mult=16):
    best = None
    for t in range(mult, min(n, target) + 1, mult):
        if n % t == 0:
            best = t
    return best if best is not None else n


def _params(sem):
    return pltpu.CompilerParams(dimension_semantics=sem, vmem_limit_bytes=V7X_VMEM_LIMIT)


def _dot(a, b, dims):
    return lax.dot_general(a, b, _DIMS[dims], preferred_element_type=F32)


def _sigmoid(x):
    return 1.0 / (1.0 + jnp.exp(-x))


def _me_and_peers():
    x, y, c = lax.axis_index("x"), lax.axis_index("y"), lax.axis_index("c")

    def peer(j):
        px = 1 - x if (j >> 2) & 1 else x
        py = 1 - y if (j >> 1) & 1 else y
        pc = 1 - c if j & 1 else c
        return (px, py, pc), 4 * px + 2 * py + pc

    return 4 * x + 2 * y + c, peer


def _exchange(name, arrays, per_peer):
    n = len(arrays)
    blocks = [a.shape[1:] if per_peer else a.shape for a in arrays]

    def body(*refs):
        src, dst = refs[:n], refs[n:2 * n]
        send_sems, recv_sems, local_sems = refs[2 * n:]
        me, peer = _me_and_peers()
        started = []
        for k in range(n):
            own = src[k].at[me] if per_peer else src[k]
            local = pltpu.make_async_copy(own, dst[k].at[me], local_sems.at[k])
            local.start()
            started.append(local)
        sends = []
        for k in range(n):
            for j in range(1, N_DEV):
                pid, pidx = peer(j)
                cp = pltpu.make_async_remote_copy(
                    src_ref=src[k].at[pidx] if per_peer else src[k], dst_ref=dst[k].at[me],
                    send_sem=send_sems.at[k * 7 + j - 1], recv_sem=recv_sems.at[k * 7 + j - 1],
                    device_id=pid, device_id_type=MESH)
                cp.start()
                sends.append(cp)
        for k in range(n):
            for j in range(1, N_DEV):
                pid, pidx = peer(j)
                pltpu.make_async_remote_copy(
                    src_ref=src[k].at[pidx] if per_peer else src[k], dst_ref=dst[k].at[pidx],
                    send_sem=send_sems.at[k * 7 + j - 1], recv_sem=recv_sems.at[k * 7 + j - 1],
                    device_id=pid, device_id_type=MESH).wait_recv()
        for cp in sends:
            cp.wait_send()
        for local in started:
            local.wait()

    any_spec = pl.BlockSpec(memory_space=pl.ANY)
    return pl.pallas_call(
        body, name=name,
        out_shape=[jax.ShapeDtypeStruct((N_DEV,) + tuple(b), a.dtype) for a, b in zip(arrays, blocks)],
        in_specs=[any_spec] * n, out_specs=[any_spec] * n,
        scratch_shapes=[pltpu.SemaphoreType.DMA((7 * n,)), pltpu.SemaphoreType.DMA((7 * n,)),
                        pltpu.SemaphoreType.DMA((n,))],
    )(*arrays)


def _allreduce_small(v):
    rows = v.shape[0]

    def body(v_ref, out_ref, buf, send_sems, recv_sems):
        me, peer = _me_and_peers()
        buf[pl.ds(me, 1)] = v_ref[...][None]
        sends = []
        for j in range(1, N_DEV):
            pid, _ = peer(j)
            cp = pltpu.make_async_remote_copy(src_ref=v_ref, dst_ref=buf.at[me], send_sem=send_sems.at[j - 1],
                                              recv_sem=recv_sems.at[j - 1], device_id=pid, device_id_type=MESH)
            cp.start()
            sends.append(cp)
        for j in range(1, N_DEV):
            pid, pidx = peer(j)
            pltpu.make_async_remote_copy(src_ref=v_ref, dst_ref=buf.at[pidx], send_sem=send_sems.at[j - 1],
                                         recv_sem=recv_sems.at[j - 1], device_id=pid,
                                         device_id_type=MESH).wait_recv()
        for cp in sends:
            cp.wait_send()
        acc = buf[0]
        for s in range(1, N_DEV):
            acc = acc + buf[s]
        out_ref[...] = acc

    vm = pl.BlockSpec(memory_space=pltpu.VMEM)
    return pl.pallas_call(
        body, name="allreduce_small", out_shape=jax.ShapeDtypeStruct(v.shape, F32),
        in_specs=[vm], out_specs=vm,
        scratch_shapes=[pltpu.VMEM((N_DEV, rows, 128), F32), pltpu.SemaphoreType.DMA((7,)),
                        pltpu.SemaphoreType.DMA((7,))],
    )(v)


def _mm(name, grid, sem, k_axis, ops, op_specs, pairs, acc_shapes, extras, extra_specs, epilogue, outs, out_specs):
    n_op, n_ex, n_out = len(ops), len(extras), len(outs)
    nk = grid[k_axis] if k_axis is not None else 1

    def body(*refs):
        op_refs = refs[:n_op]
        ex_refs = refs[n_op:n_op + n_ex]
        out_refs = refs[n_op + n_ex:n_op + n_ex + n_out]
        acc_refs = refs[n_op + n_ex + n_out:]
        parts = [None] * len(acc_shapes)
        for li, ri, dims, ai in pairs:
            d = _dot(op_refs[li][...], op_refs[ri][...], dims)
            parts[ai] = d if parts[ai] is None else parts[ai] + d

        def finish(vals):
            res = epilogue(*vals, *[e[...] for e in ex_refs])
            for o, r in zip(out_refs, res):
                o[...] = r.astype(o.dtype)

        if nk == 1:
            finish(parts)
        else:
            k = pl.program_id(k_axis)

            @pl.when(k == 0)
            def _():
                for a, p in zip(acc_refs, parts):
                    a[...] = p

            @pl.when(k > 0)
            def _():
                for a, p in zip(acc_refs, parts):
                    a[...] += p

            @pl.when(k == nk - 1)
            def _():
                finish([a[...] for a in acc_refs])

    scratch = [pltpu.VMEM(s, F32) for s in acc_shapes] if nk > 1 else []
    return pl.pallas_call(
        body, name=name, grid=grid, out_shape=outs,
        in_specs=list(op_specs) + list(extra_specs), out_specs=list(out_specs),
        scratch_shapes=scratch, compiler_params=_params(sem),
    )(*ops, *extras)


def _mm_nn(name, a, w, out_dtype, tm_target=704, tn_target=1664, epilogue=None, extras=(), extra_specs=()):
    L, K = a.shape
    N = w.shape[1]
    tm, tn = _tile(L, tm_target), _tile(N, tn_target, 128)
    ep = epilogue if epilogue is not None else (lambda acc: (acc,))
    return _mm(name, (L // tm, N // tn), ("parallel", "parallel"), None,
               [a, w], [pl.BlockSpec((tm, K), lambda i, j: (i, 0)), pl.BlockSpec((K, tn), lambda i, j: (0, j))],
               [(0, 1, 'nn', 0)], [(tm, tn)], list(extras), list(extra_specs), ep,
               [jax.ShapeDtypeStruct((L, N), out_dtype)], [pl.BlockSpec((tm, tn), lambda i, j: (i, j))])[0]


def _mm_nt(name, pairs_aw, out_dtype, tm_target=704, tn_target=512):
    L = pairs_aw[0][0].shape[0]
    N = pairs_aw[0][1].shape[0]
    tm, tn = _tile(L, tm_target), _tile(N, tn_target, 128)
    ops, specs, pairs = [], [], []
    for t, (a, w) in enumerate(pairs_aw):
        K = a.shape[1]
        ops += [a, w]
        specs += [pl.BlockSpec((tm, K), lambda i, j: (i, 0)), pl.BlockSpec((tn, K), lambda i, j: (j, 0))]
        pairs.append((2 * t, 2 * t + 1, 'nt', 0))
    return _mm(name, (L // tm, N // tn), ("parallel", "parallel"), None, ops, specs, pairs, [(tm, tn)], [], [],
               lambda acc: (acc,), [jax.ShapeDtypeStruct((L, N), out_dtype)],
               [pl.BlockSpec((tm, tn), lambda i, j: (i, j))])[0]


def _mm_tn(name, a, bs, out_dtype=BF16, tk_target=528, tn_target=1664):
    L, M = a.shape
    N = bs[0].shape[1]
    tk, tn = _tile(L, tk_target), _tile(N, tn_target, 128)
    nb = len(bs)
    ops = [a] + list(bs)
    specs = [pl.BlockSpec((tk, M), lambda j, k: (k, 0))] + [pl.BlockSpec((tk, tn), lambda j, k: (k, j))] * nb
    return _mm(name, (N // tn, L // tk), ("parallel", "arbitrary"), 1, ops, specs,
               [(0, 1 + t, 'tn', t) for t in range(nb)], [(M, tn)] * nb, [], [], lambda *acc: acc,
               [jax.ShapeDtypeStruct((M, N), out_dtype)] * nb, [pl.BlockSpec((M, tn), lambda j, k: (0, j))] * nb)


def _norm_fwd(x, w):
    L, D = x.shape
    tr = _tile(L, 512)

    def body(x_ref, w_ref, y_ref):
        v = x_ref[...]
        r = lax.rsqrt(jnp.mean(v * v, axis=-1, keepdims=True) + EPS)
        y_ref[...] = (v * r * w_ref[...]).astype(y_ref.dtype)

    return pl.pallas_call(
        body, name="norm_fwd", grid=(L // tr,), out_shape=jax.ShapeDtypeStruct((L, D), BF16),
        in_specs=[pl.BlockSpec((tr, D), lambda i: (i, 0)), pl.BlockSpec((1, D), lambda i: (0, 0))],
        out_specs=pl.BlockSpec((tr, D), lambda i: (i, 0)), compiler_params=_params(("parallel",)),
    )(x, w)


def _norm_bwd_math(x, w, dy):
    r = lax.rsqrt(jnp.mean(x * x, axis=-1, keepdims=True) + EPS)
    gy = dy * w
    dx = r * (gy - x * (r * r) * jnp.mean(gy * x, axis=-1, keepdims=True))
    dw = jnp.sum(dy * x * r, axis=0, keepdims=True)
    return dx, dw


def _norm_bwd(x, w, dy, res, scale, out_dtype):
    L, D = x.shape
    tr = _tile(L, 384)
    has_res = res is not None

    def body(*refs):
        x_ref, w_ref, dy_ref = refs[:3]
        res_ref = refs[3] if has_res else None
        dx_ref, dw_ref = refs[-2:]
        dx, dw = _norm_bwd_math(x_ref[...], w_ref[...], dy_ref[...].astype(F32))
        dx = scale * dx
        if has_res:
            dx = dx + res_ref[...]
        dx_ref[...] = dx.astype(dx_ref.dtype)

        @pl.when(pl.program_id(0) == 0)
        def _():
            dw_ref[...] = jnp.zeros_like(dw_ref)

        dw_ref[...] += scale * dw

    row = pl.BlockSpec((tr, D), lambda i: (i, 0))
    vec = pl.BlockSpec((1, D), lambda i: (0, 0))
    return pl.pallas_call(
        body, name="norm_bwd", grid=(L // tr,),
        out_shape=[jax.ShapeDtypeStruct((L, D), out_dtype), jax.ShapeDtypeStruct((1, D), F32)],
        in_specs=[row, vec, row] + ([row] if has_res else []), out_specs=[row, vec],
        compiler_params=_params(("arbitrary",)),
    )(*([x, w, dy] + ([res] if has_res else [])))


def _loss(h, target):
    L, D = h.shape

    def body(h_ref, t_ref, dh_ref, loss_ref):
        i = pl.program_id(0)

        @pl.when(i == 0)
        def _():
            dh_ref[...] = jnp.zeros_like(dh_ref)
            loss_ref[...] = jnp.zeros_like(loss_ref)

        @pl.when(i > 0)
        def _():
            diff = h_ref[...] - t_ref[...]
            dh_ref[...] = diff * (1.0 / D)
            loss_ref[...] += 0.5 * jnp.sum(diff * diff) * (1.0 / D)

    return pl.pallas_call(
        body, name="loss", grid=(L // BLK,),
        out_shape=[jax.ShapeDtypeStruct((L, D), F32), jax.ShapeDtypeStruct((8, 128), F32)],
        in_specs=[pl.BlockSpec((BLK, D), lambda i: (i, 0)),
                  pl.BlockSpec((BLK, D), lambda i: (jnp.maximum(i - 1, 0), 0))],
        out_specs=[pl.BlockSpec((BLK, D), lambda i: (i, 0)), pl.BlockSpec((8, 128), lambda i: (0, 0))],
        compiler_params=_params(("arbitrary",)),
    )(h, target)


def _ffn_up(a, wg, wu):
    L, D = a.shape
    F = wg.shape[2]
    tm = _tile(L, 704)

    def ep(g, u):
        return g, u, g * _sigmoid(g) * u

    hspec = pl.BlockSpec((None, tm, F), lambda i, j: (j, i, 0))
    wspec = pl.BlockSpec((None, D, F), lambda i, j: (j, 0, 0))
    return _mm("ffn_up", (L // tm, N_DEV), ("parallel", "parallel"), None,
               [a, wg, wu], [pl.BlockSpec((tm, D), lambda i, j: (i, 0)), wspec, wspec],
               [(0, 1, 'nn', 0), (0, 2, 'nn', 1)], [(tm, F)] * 2, [], [], ep,
               [jax.ShapeDtypeStruct((N_DEV, L, F), BF16)] * 3, [hspec] * 3)


def _resnorm_epilogue(scale):
    def ep(acc, h, w):
        r = lax.rsqrt(jnp.mean(acc * acc, axis=-1, keepdims=True) + EPS)
        return acc, h + scale * (acc * r * w)
    return ep


def _ffn_down(hid, wd, h_in, post):
    _, L, F = hid.shape
    D = wd.shape[2]
    tm = _tile(L, 384)
    row = pl.BlockSpec((tm, D), lambda i, j: (i, 0))
    return _mm("ffn_down", (L // tm, N_DEV), ("parallel", "arbitrary"), 1,
               [hid, wd], [pl.BlockSpec((None, tm, F), lambda i, j: (j, i, 0)),
                           pl.BlockSpec((None, F, D), lambda i, j: (j, 0, 0))],
               [(0, 1, 'nn', 0)], [(tm, D)], [h_in, post], [row, pl.BlockSpec((1, D), lambda i, j: (0, 0))],
               _resnorm_epilogue(0.5), [jax.ShapeDtypeStruct((L, D), F32)] * 2, [row, row])


def _ffn_dhid(df, wd, g, u):
    L, D = df.shape
    F = wd.shape[1]
    tm = _tile(L, 704)

    def ep(dhid, g_, u_):
        g32, u32 = g_.astype(F32), u_.astype(F32)
        sg = _sigmoid(g32)
        return dhid * u32 * sg * (1.0 + g32 * (1.0 - sg)), dhid * g32 * sg

    hspec = pl.BlockSpec((None, tm, F), lambda i, j: (j, i, 0))
    return _mm("ffn_dhid", (L // tm, N_DEV), ("parallel", "parallel"), None,
               [df, wd], [pl.BlockSpec((tm, D), lambda i, j: (i, 0)),
                          pl.BlockSpec((None, F, D), lambda i, j: (j, 0, 0))],
               [(0, 1, 'nt', 0)], [(tm, F)], [g, u], [hspec, hspec], ep,
               [jax.ShapeDtypeStruct((N_DEV, L, F), BF16)] * 2, [hspec, hspec])


def _ffn_dwd(hid, df):
    _, L, F = hid.shape
    D = df.shape[1]
    tk = _tile(L, 528)
    return _mm("ffn_dwd", (N_DEV, L // tk), ("parallel", "arbitrary"), 1,
               [hid, df], [pl.BlockSpec((None, tk, F), lambda j, k: (j, k, 0)),
                           pl.BlockSpec((tk, D), lambda j, k: (k, 0))],
               [(0, 1, 'tn', 0)], [(F, D)], [], [], lambda acc: (acc,),
               [jax.ShapeDtypeStruct((N_DEV, F, D), BF16)], [pl.BlockSpec((None, F, D), lambda j, k: (j, 0, 0))])[0]


def _ffn_dwgu(a, dg, du):
    L, D = a.shape
    F = dg.shape[2]
    tk = _tile(L, 528)
    hspec = pl.BlockSpec((None, tk, F), lambda j, k: (j, k, 0))
    wspec = pl.BlockSpec((None, D, F), lambda j, k: (j, 0, 0))
    return _mm("ffn_dwgu", (N_DEV, L // tk), ("parallel", "arbitrary"), 1,
               [a, dg, du], [pl.BlockSpec((tk, D), lambda j, k: (k, 0)), hspec, hspec],
               [(0, 1, 'tn', 0), (0, 2, 'tn', 1)], [(D, F)] * 2, [], [], lambda *acc: acc,
               [jax.ShapeDtypeStruct((N_DEV, D, F), BF16)] * 2, [wspec, wspec])


def _ffn_da(dg, du, wg, wu):
    _, L, F = dg.shape
    D = wg.shape[1]
    tm = _tile(L, 384)
    hspec = pl.BlockSpec((None, tm, F), lambda i, j: (j, i, 0))
    wspec = pl.BlockSpec((None, D, F), lambda i, j: (j, 0, 0))
    row = pl.BlockSpec((tm, D), lambda i, j: (i, 0))
    return _mm("ffn_da", (L // tm, N_DEV), ("parallel", "arbitrary"), 1,
               [dg, du, wg, wu], [hspec, hspec, wspec, wspec],
               [(0, 2, 'nt', 0), (1, 3, 'nt', 0)], [(tm, D)], [], [], lambda acc: (acc,),
               [jax.ShapeDtypeStruct((L, D), F32)], [row])[0]


def _ffn_fwd(h, pre, post, wg, wu, wd):
    a = _norm_fwd(h, pre)
    g, u, hid = _ffn_up(a, wg, wu)
    f, h_out = _ffn_down(hid, wd, h, post)
    return h_out, (a, g, u, hid, f)


def _ffn_bwd(dh_out, h_in, pre, post, wg, wu, wd, saved):
    a, g, u, hid, f = saved
    df, dpost = _norm_bwd(f, post, dh_out, None, 0.5, BF16)
    dg, du = _ffn_dhid(df, wd, g, u)
    dwd = _ffn_dwd(hid, df)
    dwg, dwu = _ffn_dwgu(a, dg, du)
    da = _ffn_da(dg, du, wg, wu)
    dh_in, dpre = _norm_bwd(h_in, pre, da, dh_out, 1.0, F32)
    return dh_in, dpre, dpost, dwg, dwu, dwd


def _rope_tables(L):
    rows = jnp.arange(L, dtype=F32)
    pos = jnp.where(rows < BLK, rows, rows - (BLK - N_META))
    inv_r = ROPE_THETA ** (-jnp.arange(0, HD, 2, dtype=F32) / HD)
    ang_r = pos[:, None] * inv_r[None, :]
    cr = jnp.concatenate([jnp.cos(ang_r), jnp.cos(ang_r)], axis=1)
    sr = jnp.concatenate([-jnp.sin(ang_r), jnp.sin(ang_r)], axis=1)
    inv_m = ROPE_THETA ** (-jnp.arange(0, ROPE, 2, dtype=F32) / ROPE)
    ang_m = pos[:, None] * inv_m[None, :]
    z32 = jnp.zeros((L, ROPE // 2), F32)
    z64 = jnp.zeros((L, HD - ROPE), F32)
    cm = jnp.concatenate([jnp.cos(ang_m), jnp.cos(ang_m), z64], axis=1)
    sa = jnp.concatenate([-jnp.sin(ang_m), z32, z64], axis=1)
    sb = jnp.concatenate([z32, jnp.sin(ang_m), z64], axis=1)
    return cr, sr, cm, sa, sb


def _rope_ret(x, cr, sr):
    return x * cr + pltpu.roll(x, HD // 2, 1) * sr


def _rope_ret_t(d, cr, sr):
    return d * cr + pltpu.roll(d * sr, HD // 2, 1)


def _rope_mla(x, cm, sa, sb):
    return x * cm + pltpu.roll(x, HD - ROPE // 2, 1) * sa + pltpu.roll(x, ROPE // 2, 1) * sb


def _rope_mla_t(d, cm, sa, sb):
    return d * cm + pltpu.roll(d * sa, ROPE // 2, 1) + pltpu.roll(d * sb, HD - ROPE // 2, 1)


C_RQ, C_RK, C_RV, C_RG = 0, HEADS * HD, 2 * HEADS * HD, 3 * HEADS * HD
C_CQ = 4 * HEADS * HD
C_CKV = C_CQ + Q_RANK
C_KR = C_CKV + KV_RANK
RET_K_SCALE = HD ** -0.5


def _prep(proj, tabs, qn, kvn):
    L = proj.shape[0]
    tr = _tile(L, 256)
    W = HEADS * HD

    def body(p_ref, cr_ref, sr_ref, cm_ref, sa_ref, sb_ref, qn_ref, kvn_ref, q_ref, k_ref, v_ref, cq_ref, ckv_ref,
             kr_ref):
        cr, sr = cr_ref[...], sr_ref[...]
        for h in range(HEADS):
            sl = slice(h * HD, (h + 1) * HD)
            q_ref[:, sl] = _rope_ret(p_ref[:, C_RQ + h * HD:C_RQ + (h + 1) * HD], cr, sr).astype(BF16)
            k_ref[:, sl] = (_rope_ret(p_ref[:, C_RK + h * HD:C_RK + (h + 1) * HD], cr, sr)
                            * RET_K_SCALE).astype(BF16)
        v_ref[...] = p_ref[:, C_RV:C_RV + W].astype(BF16)
        cq = p_ref[:, C_CQ:C_CQ + Q_RANK]
        cq_ref[...] = (cq * lax.rsqrt(jnp.mean(cq * cq, axis=-1, keepdims=True) + EPS) * qn_ref[...]).astype(BF16)
        ckv = p_ref[:, C_CKV:C_CKV + KV_RANK]
        ckv_ref[...] = (ckv * lax.rsqrt(jnp.mean(ckv * ckv, axis=-1, keepdims=True) + EPS)
                        * kvn_ref[...]).astype(BF16)
        kr_ref[...] = _rope_mla(p_ref[:, C_KR:C_KR + HD], cm_ref[...], sa_ref[...], sb_ref[...]).astype(BF16)

    row = lambda w: pl.BlockSpec((tr, w), lambda i: (i, 0))
    vec = lambda w: pl.BlockSpec((1, w), lambda i: (0, 0))
    return pl.pallas_call(
        body, name="mix_prep", grid=(L // tr,),
        out_shape=[jax.ShapeDtypeStruct((L, W), BF16)] * 3 + [jax.ShapeDtypeStruct((L, Q_RANK), BF16),
                                                              jax.ShapeDtypeStruct((L, KV_RANK), BF16),
                                                              jax.ShapeDtypeStruct((L, HD), BF16)],
        in_specs=[row(D_INP)] + [row(HD)] * 5 + [vec(Q_RANK), vec(KV_RANK)],
        out_specs=[row(W)] * 3 + [row(Q_RANK), row(KV_RANK), row(HD)],
        compiler_params=_params(("parallel",)),
    )(proj, *tabs, qn, kvn)


def _prep_bwd(proj, dq, dk, dv, drg, dcqn, dckvn, dkr8, tabs, qn, kvn):
    L = proj.shape[0]
    tr = _tile(L, 192)
    W = HEADS * HD

    def body(p_ref, dq_ref, dk_ref, dv_ref, drg_ref, dcq_ref, dckv_ref, dkr_ref, cr_ref, sr_ref, cm_ref, sa_ref,
             sb_ref, qn_ref, kvn_ref, dp_ref, dqn_ref, dkvn_ref):
        cr, sr = cr_ref[...], sr_ref[...]
        dkr = None
        for h in range(HEADS):
            sl = slice(h * HD, (h + 1) * HD)
            dp_ref[:, C_RQ + h * HD:C_RQ + (h + 1) * HD] = _rope_ret_t(dq_ref[:, sl], cr, sr).astype(BF16)
            dp_ref[:, C_RK + h * HD:C_RK + (h + 1) * HD] = (_rope_ret_t(dk_ref[:, sl], cr, sr)
                                                            * RET_K_SCALE).astype(BF16)
            part = dkr_ref[:, sl]
            dkr = part if dkr is None else dkr + part
        dp_ref[:, C_RV:C_RV + W] = dv_ref[...].astype(BF16)
        dp_ref[:, C_RG:C_RG + W] = drg_ref[...].astype(BF16)
        dcq, dqn = _norm_bwd_math(p_ref[:, C_CQ:C_CQ + Q_RANK], qn_ref[...], dcq_ref[...])
        dp_ref[:, C_CQ:C_CQ + Q_RANK] = dcq.astype(BF16)
        dckv, dkvn = _norm_bwd_math(p_ref[:, C_CKV:C_CKV + KV_RANK], kvn_ref[...], dckv_ref[...])
        dp_ref[:, C_CKV:C_CKV + KV_RANK] = dckv.astype(BF16)
        dp_ref[:, C_KR:C_KR + HD] = _rope_mla_t(dkr, cm_ref[...], sa_ref[...], sb_ref[...]).astype(BF16)

        @pl.when(pl.program_id(0) == 0)
        def _():
            dqn_ref[...] = jnp.zeros_like(dqn_ref)
            dkvn_ref[...] = jnp.zeros_like(dkvn_ref)

        dqn_ref[...] += dqn
        dkvn_ref[...] += dkvn

    row = lambda w: pl.BlockSpec((tr, w), lambda i: (i, 0))
    vec = lambda w: pl.BlockSpec((1, w), lambda i: (0, 0))
    return pl.pallas_call(
        body, name="mix_prep_bwd", grid=(L // tr,),
        out_shape=[jax.ShapeDtypeStruct((L, D_INP), BF16), jax.ShapeDtypeStruct((1, Q_RANK), F32),
                   jax.ShapeDtypeStruct((1, KV_RANK), F32)],
        in_specs=[row(D_INP)] + [row(W)] * 4 + [row(Q_RANK), row(KV_RANK), row(W)] + [row(HD)] * 5
                 + [vec(Q_RANK), vec(KV_RANK)],
        out_specs=[row(D_INP), vec(Q_RANK), vec(KV_RANK)],
        compiler_params=_params(("arbitrary",)),
    )(proj, dq, dk, dv, drg, dcqn, dckvn, dkr8, *tabs, qn, kvn)


def _post(o_ret, proj, gn):
    L, W = o_ret.shape
    tr = _tile(L, 384)

    def body(o_ref, rg_ref, gn_ref, out_ref):
        for h in range(HEADS):
            sl = slice(h * HD, (h + 1) * HD)
            o = o_ref[:, sl]
            rg = rg_ref[:, sl]
            n = o * lax.rsqrt(jnp.mean(o * o, axis=-1, keepdims=True) + EPS)
            out_ref[:, sl] = (n * gn_ref[:, sl] * (rg * _sigmoid(rg))).astype(BF16)

    row = pl.BlockSpec((tr, W), lambda i: (i, 0))
    return pl.pallas_call(
        body, name="ret_post", grid=(L // tr,), out_shape=jax.ShapeDtypeStruct((L, W), BF16),
        in_specs=[row, pl.BlockSpec((tr, W), lambda i: (i, C_RG // W)), pl.BlockSpec((1, W), lambda i: (0, 0))],
        out_specs=row, compiler_params=_params(("parallel",)),
    )(o_ret, proj, gn)


def _post_bwd(o_ret, proj, gn, dcat):
    L, W = o_ret.shape
    tr = _tile(L, 384)

    def body(o_ref, rg_ref, gn_ref, d_ref, do_ref, drg_ref, dgn_ref):
        @pl.when(pl.program_id(0) == 0)
        def _():
            dgn_ref[...] = jnp.zeros_like(dgn_ref)

        for h in range(HEADS):
            sl = slice(h * HD, (h + 1) * HD)
            o = o_ref[:, sl]
            rg = rg_ref[:, sl]
            d = d_ref[:, sl].astype(F32)
            gw = gn_ref[:, sl]
            r = lax.rsqrt(jnp.mean(o * o, axis=-1, keepdims=True) + EPS)
            n = o * r
            sg = _sigmoid(rg)
            si = rg * sg
            dn = d * gw * si
            dgn_ref[:, sl] += jnp.sum(d * n * si, axis=0, keepdims=True)
            drg_ref[:, sl] = d * n * gw * sg * (1.0 + rg * (1.0 - sg))
            do_ref[:, sl] = (r * (dn - o * (r * r) * jnp.mean(dn * o, axis=-1, keepdims=True))).astype(BF16)

    row = pl.BlockSpec((tr, W), lambda i: (i, 0))
    vec = pl.BlockSpec((1, W), lambda i: (0, 0))
    return pl.pallas_call(
        body, name="ret_post_bwd", grid=(L // tr,),
        out_shape=[jax.ShapeDtypeStruct((L, W), BF16), jax.ShapeDtypeStruct((L, W), F32),
                   jax.ShapeDtypeStruct((1, W), F32)],
        in_specs=[row, pl.BlockSpec((tr, W), lambda i: (i, C_RG // W)), vec, row],
        out_specs=[row, row, vec], compiler_params=_params(("arbitrary",)),
    )(o_ret, proj, gn, dcat)


def _lin_attn(name, q, k, v, lg, reverse):
    L, W = q.shape
    nc = L // BLK - 1

    def body(q_ref, k_ref, v_ref, lg_ref, o_ref, s_ref):
        lgv = lg_ref[0:1, :]
        n = lax.broadcasted_iota(jnp.int32, (BLK, BLK), 0).astype(F32)
        m = lax.broadcasted_iota(jnp.int32, (BLK, BLK), 1).astype(F32)
        dist = (m - n) if reverse else (n - m)
        dmask = jnp.where(dist >= 0, jnp.exp(lgv * jnp.maximum(dist, 0.0)), 0.0)
        dmask0 = jnp.where((n < N_META) & (m < N_META), dmask, 0.0)
        gl = jnp.exp(lgv * float(BLK))
        if reverse:
            inter = jnp.exp(lgv * (float(BLK) - n))
            inter0 = jnp.where(n < N_META, jnp.exp(lgv * jnp.maximum(float(N_META) - n, 0.0)), 0.0)
            upd = jnp.exp(lgv * n)
        else:
            inter = jnp.exp(lgv * (n + 1.0))
            upd = jnp.exp(lgv * (float(BLK) - 1.0 - n))
            upd0 = jnp.where(n < N_META, jnp.exp(lgv * jnp.maximum(float(N_META) - 1.0 - n, 0.0)), 0.0)

        def chunk(c):
            rows = pl.ds(pl.multiple_of(c * BLK, BLK), BLK)
            qc, kc, vc = q_ref[rows, :], k_ref[rows, :], v_ref[rows, :]
            a = _dot(qc, kc, 'nt') * dmask
            o = _dot(a.astype(BF16), vc, 'nn') + _dot(qc, s_ref[...].astype(BF16), 'nn') * inter
            o_ref[rows, :] = o
            s_ref[...] = s_ref[...] * gl + _dot((kc.astype(F32) * upd).astype(BF16), vc, 'tn')

        q0, k0, v0 = q_ref[0:BLK, :], k_ref[0:BLK, :], v_ref[0:BLK, :]
        a0 = _dot(q0, k0, 'nt') * dmask0
        if reverse:
            s_ref[...] = jnp.zeros_like(s_ref)

            def step(t, carry):
                chunk(nc - t)
                return carry

            lax.fori_loop(0, nc, step, 0)
            o_ref[0:BLK, :] = _dot(a0.astype(BF16), v0, 'nn') + _dot(q0, s_ref[...].astype(BF16), 'nn') * inter0
        else:
            o_ref[0:BLK, :] = _dot(a0.astype(BF16), v0, 'nn')
            s_ref[...] = _dot((k0.astype(F32) * upd0).astype(BF16), v0, 'tn')

            def step(t, carry):
                chunk(t + 1)
                return carry

            lax.fori_loop(0, nc, step, 0)

    col = pl.BlockSpec((L, HD), lambda h: (0, h))
    return pl.pallas_call(
        body, name=name, grid=(HEADS,), out_shape=jax.ShapeDtypeStruct((L, W), F32),
        in_specs=[col, col, col, pl.BlockSpec((None, 8, HD), lambda h: (h, 0, 0))], out_specs=col,
        scratch_shapes=[pltpu.VMEM((HD, HD), F32)], compiler_params=_params(("parallel",)),
    )(q, k, v, lg)


ATT_SCALE = (HD + ROPE) ** -0.5
NEG = -1e30


ATT_TILE = 384


def _att_valid(T, row0, col0):
    r = lax.broadcasted_iota(jnp.int32, (T, T), 0) + row0
    c = lax.broadcasted_iota(jnp.int32, (T, T), 1) + col0
    return (c <= r) & ((c < N_META) | (c >= BLK))


def _attn_fwd(qm, kn, krr, vm):
    L = qm.shape[0]
    W = HEADS * HD
    T = _tile(L, ATT_TILE, BLK)
    nb = L // T

    def body(q_ref, kn_ref, kr_ref, v_ref, o_ref, lse_ref, m_sc, l_sc, acc_sc):
        i = pl.program_id(1)
        q = q_ref[...]
        m_sc[...] = jnp.full_like(m_sc, NEG)
        l_sc[...] = jnp.zeros_like(l_sc)
        acc_sc[...] = jnp.zeros_like(acc_sc)

        def tile(j, masked):
            rows = pl.ds(pl.multiple_of(j * T, T), T)
            k = jnp.concatenate([kn_ref[rows, :], kr_ref[rows, :]], axis=1)
            s = _dot(q, k, 'nt') * ATT_SCALE
            if masked:
                s = jnp.where(_att_valid(T, i * T, j * T), s, NEG)
            m_prev = m_sc[...]
            m_new = jnp.maximum(m_prev, jnp.max(s, axis=-1, keepdims=True))
            p = jnp.exp(s - m_new)
            alpha = jnp.exp(m_prev - m_new)
            l_sc[...] = alpha * l_sc[...] + jnp.sum(p, axis=-1, keepdims=True)
            acc_sc[...] = alpha * acc_sc[...] + _dot(p.astype(BF16), v_ref[rows, :], 'nn')
            m_sc[...] = m_new

        tile(0, True)

        def mid(j, carry):
            tile(j, False)
            return carry

        lax.fori_loop(1, i, mid, 0)

        @pl.when(i > 0)
        def _():
            tile(i, True)

        l = l_sc[...]
        o_ref[...] = (acc_sc[...] / l).astype(o_ref.dtype)
        lse_ref[...] = jnp.broadcast_to(m_sc[...] + jnp.log(l), (T, HD))

    return pl.pallas_call(
        body, name="attn_fwd", grid=(HEADS, nb),
        out_shape=[jax.ShapeDtypeStruct((L, W), BF16), jax.ShapeDtypeStruct((HEADS, L, HD), F32)],
        in_specs=[pl.BlockSpec((T, QH), lambda h, i: (i, h)), pl.BlockSpec((L, HD), lambda h, i: (0, h)),
                  pl.BlockSpec((L, HD), lambda h, i: (0, 0)), pl.BlockSpec((L, HD), lambda h, i: (0, h))],
        out_specs=[pl.BlockSpec((T, HD), lambda h, i: (i, h)), pl.BlockSpec((None, T, HD), lambda h, i: (h, i, 0))],
        scratch_shapes=[pltpu.VMEM((T, 1), F32), pltpu.VMEM((T, 1), F32), pltpu.VMEM((T, HD), F32)],
        compiler_params=_params(("parallel", "arbitrary")),
    )(qm, kn, krr, vm)


def _attn_bwd(qm, kn, krr, vm, o, dcat, lse):
    L = qm.shape[0]
    W = HEADS * HD
    T = _tile(L, ATT_TILE, BLK)
    nb = L // T

    def body(q_ref, kn_ref, kr_ref, v_ref, o_ref, do_ref, lse_ref, dq_ref, dkn_ref, dkr_ref, dv_ref, dl_sc, dk_sc,
             dv_sc):
        j = pl.program_id(1)

        @pl.when(j == 0)
        def _():
            dq_ref[...] = jnp.zeros_like(dq_ref)

            def rowsum(t, carry):
                rows = pl.ds(pl.multiple_of(t * T, T), T)
                dl_sc[rows, :] = jnp.sum(do_ref[rows, :].astype(F32) * o_ref[rows, :].astype(F32), axis=-1,
                                         keepdims=True)
                return carry

            lax.fori_loop(0, nb, rowsum, 0)

        k = jnp.concatenate([kn_ref[...], kr_ref[...]], axis=1)
        v = v_ref[...]
        dk_sc[...] = jnp.zeros_like(dk_sc)
        dv_sc[...] = jnp.zeros_like(dv_sc)

        def tile(i, masked):
            rows = pl.ds(pl.multiple_of(i * T, T), T)
            q = q_ref[rows, :]
            do = do_ref[rows, :]
            s = _dot(q, k, 'nt') * ATT_SCALE
            if masked:
                s = jnp.where(_att_valid(T, i * T, j * T), s, NEG)
            p = jnp.exp(s - lse_ref[rows, 0:1])
            dv_sc[...] += _dot(p.astype(BF16), do, 'tn')
            ds = (p * (_dot(do, v, 'nt') - dl_sc[rows, :]) * ATT_SCALE).astype(BF16)
            dk_sc[...] += _dot(ds, q, 'tn')
            dq_ref[rows, :] += _dot(ds, k, 'nn')

        tile(j, True)

        def rest(masked):
            def step(i, carry):
                tile(i, masked)
                return carry
            lax.fori_loop(j + 1, nb, step, 0)

        @pl.when(j == 0)
        def _():
            rest(True)

        @pl.when(j > 0)
        def _():
            rest(False)

        dk = dk_sc[...]
        dkn_ref[...] = dk[:, 0:HD].astype(BF16)
        dkr_ref[...] = dk[:, HD:QH]
        dv_ref[...] = dv_sc[...].astype(BF16)

    blk = pl.BlockSpec((T, HD), lambda h, j: (j, h))
    return pl.pallas_call(
        body, name="attn_bwd", grid=(HEADS, nb),
        out_shape=[jax.ShapeDtypeStruct((L, HEADS * QH), F32), jax.ShapeDtypeStruct((L, W), BF16),
                   jax.ShapeDtypeStruct((L, W), F32), jax.ShapeDtypeStruct((L, W), BF16)],
        in_specs=[pl.BlockSpec((L, QH), lambda h, j: (0, h)), blk, pl.BlockSpec((T, HD), lambda h, j: (j, 0)), blk,
                  pl.BlockSpec((L, HD), lambda h, j: (0, h)), pl.BlockSpec((L, HD), lambda h, j: (0, HEADS + h)),
                  pl.BlockSpec((None, L, HD), lambda h, j: (h, 0, 0))],
        out_specs=[pl.BlockSpec((L, QH), lambda h, j: (0, h)), blk, blk, blk],
        scratch_shapes=[pltpu.VMEM((L, 1), F32), pltpu.VMEM((T, QH), F32), pltpu.VMEM((T, HD), F32)],
        compiler_params=_params(("parallel", "arbitrary")),
    )(qm, kn, krr, vm, o, dcat, lse)


def _unrope_q(dqm, tabs_m):
    L, W = dqm.shape
    tr = _tile(L, 384)

    def body(d_ref, cm_ref, sa_ref, sb_ref, out_ref):
        cm, sa, sb = cm_ref[...], sa_ref[...], sb_ref[...]
        for h in range(HEADS):
            out_ref[:, h * QH:h * QH + HD] = d_ref[:, h * QH:h * QH + HD].astype(BF16)
            out_ref[:, h * QH + HD:(h + 1) * QH] = _rope_mla_t(d_ref[:, h * QH + HD:(h + 1) * QH], cm, sa,
                                                               sb).astype(BF16)

    row = pl.BlockSpec((tr, W), lambda i: (i, 0))
    tab = pl.BlockSpec((tr, HD), lambda i: (i, 0))
    return pl.pallas_call(
        body, name="unrope_q", grid=(L // tr,), out_shape=jax.ShapeDtypeStruct((L, W), BF16),
        in_specs=[row, tab, tab, tab], out_specs=row, compiler_params=_params(("parallel",)),
    )(dqm, *tabs_m)


def _q_up(cqn, wuq_p, tabs_m):
    L = cqn.shape[0]
    tm = _tile(L, 704)

    def ep(acc, cm, sa, sb):
        parts = []
        for h in range(HEADS):
            parts.append(acc[:, h * QH:h * QH + HD])
            parts.append(_rope_mla(acc[:, h * QH + HD:(h + 1) * QH], cm, sa, sb))
        return (jnp.concatenate(parts, axis=1),)

    tab = pl.BlockSpec((tm, HD), lambda i, j: (i, 0))
    return _mm("mla_q_up", (L // tm, 1), ("parallel", "parallel"), None,
               [cqn, wuq_p], [pl.BlockSpec((tm, Q_RANK), lambda i, j: (i, 0)),
                              pl.BlockSpec((Q_RANK, HEADS * QH), lambda i, j: (0, 0))],
               [(0, 1, 'nn', 0)], [(tm, HEADS * QH)], list(tabs_m), [tab] * 3, ep,
               [jax.ShapeDtypeStruct((L, HEADS * QH), BF16)], [pl.BlockSpec((tm, HEADS * QH), lambda i, j: (i, 0))])[0]


def _mix_out(cat, w_out, h_in, post):
    L, K = cat.shape
    D = w_out.shape[1]
    tm, tk = _tile(L, 384), _tile(K, 512, 128)
    row = pl.BlockSpec((tm, D), lambda i, k: (i, 0))
    return _mm("mix_out", (L // tm, K // tk), ("parallel", "arbitrary"), 1,
               [cat, w_out], [pl.BlockSpec((tm, tk), lambda i, k: (i, k)), pl.BlockSpec((tk, D), lambda i, k: (k, 0))],
               [(0, 1, 'nn', 0)], [(tm, D)], [h_in, post], [row, pl.BlockSpec((1, D), lambda i, k: (0, 0))],
               _resnorm_epilogue(1.0), [jax.ShapeDtypeStruct((L, D), F32)] * 2, [row, row])


def _mixer_fwd(h, p, W, tabs, lg):
    u = _norm_fwd(h, p['mix_pre_norm'])
    proj = _mm_nn("mix_in", u, W['w_in'], F32)
    qr, kr, vr, cqn, ckvn, krr = _prep(proj, tabs, p['mla_q_norm'], p['mla_kv_norm'])
    qm = _q_up(cqn, W['wuq'], tabs[2:])
    kn = _mm_nn("mla_k_up", ckvn, W['wuk'], BF16)
    vm = _mm_nn("mla_v_up", ckvn, W['wuv'], BF16)
    o_mla, lse = _attn_fwd(qm, kn, krr, vm)
    o_ret = _lin_attn("ret_fwd", qr, kr, vr, lg, False)
    ret = _post(o_ret, proj, p['ret_group_norm'])
    cat = jnp.concatenate([ret, o_mla], axis=1)
    m, h_out = _mix_out(cat, W['w_out'], h, p['mix_post_norm'])
    return h_out, (u, proj, qr, kr, vr, cqn, ckvn, krr, qm, kn, vm, o_mla, lse, o_ret, cat, m)


def _mixer_bwd(dh_out, h_in, p, W, tabs, lg, saved):
    u, proj, qr, kr, vr, cqn, ckvn, krr, qm, kn, vm, o_mla, lse, o_ret, cat, m = saved
    dm, dmixpost = _norm_bwd(m, p['mix_post_norm'], dh_out, None, 1.0, BF16)
    dcat = _mm_nt("mix_dcat", [(dm, W['w_out'])], BF16)
    dwout = _mm_tn("mix_dwout", cat, [dm])[0]
    do_ret, drg, dgn = _post_bwd(o_ret, proj, p['ret_group_norm'], dcat)
    dqr = _lin_attn("ret_dq", do_ret, vr, kr, lg, False)
    dkr = _lin_attn("ret_dk", vr, do_ret, qr, lg, True)
    dvr = _lin_attn("ret_dv", kr, qr, do_ret, lg, True)
    dqm, dkn, dkr8, dvm = _attn_bwd(qm, kn, krr, vm, o_mla, dcat, lse)
    dqp = _unrope_q(dqm, tabs[2:])
    dwuq = _mm_tn("mla_dwuq", cqn, [dqp])[0]
    dcqn = _mm_nt("mla_dcq", [(dqp, W['wuq'])], F32)
    dwuk, dwuv = _mm_tn("mla_dwukv", ckvn, [dkn, dvm])
    dckvn = _mm_nt("mla_dckv", [(dkn, W['wuk']), (dvm, W['wuv'])], F32)
    dproj, dqn, dkvn = _prep_bwd(proj, dqr, dkr, dvr, drg, dcqn, dckvn, dkr8, tabs, p['mla_q_norm'],
                                 p['mla_kv_norm'])
    dwin = _mm_tn("mix_dwin", u, [dproj])[0]
    du = _mm_nt("mix_du", [(dproj, W['w_in'])], F32)
    dh_in, dmixpre = _norm_bwd(h_in, p['mix_pre_norm'], du, dh_out, 1.0, F32)
    return dh_in, dict(mix_pre_norm=dmixpre, mix_post_norm=dmixpost, ret_group_norm=dgn, mla_q_norm=dqn,
                       mla_kv_norm=dkvn), dwin, dwuq, dwuk, dwuv, dwout


def _adam_math(w, g, m, v):
    m = ADAM_B1 * m + (1.0 - ADAM_B1) * g
    v = ADAM_B2 * v + (1.0 - ADAM_B2) * (g * g)
    m_hat = m / (1.0 - ADAM_B1 ** ADAM_STEP)
    v_hat = v / (1.0 - ADAM_B2 ** ADAM_STEP)
    delta = -ADAM_LR * (m_hat / (jnp.sqrt(v_hat) + ADAM_EPS) + ADAM_WD * w)
    return delta, m, v


def _adam(name, w, m, v, g_slots=None, g=None):
    R, C = w.shape
    tr = _tile(R, 256, 8)
    from_slots = g_slots is not None

    def body(w_ref, m_ref, v_ref, g_ref, go_ref, d_ref, mo_ref, vo_ref):
        if from_slots:
            grad = g_ref[0].astype(F32)
            for s in range(1, N_DEV):
                grad = grad + g_ref[s].astype(F32)
        else:
            grad = g_ref[...]
        delta, mn, vn = _adam_math(w_ref[...], grad, m_ref[...], v_ref[...])
        go_ref[...] = grad
        d_ref[...] = delta
        mo_ref[...] = mn
        vo_ref[...] = vn

    row = pl.BlockSpec((tr, C), lambda i: (i, 0))
    gspec = pl.BlockSpec((N_DEV, tr, C), lambda i: (0, i, 0)) if from_slots else row
    return pl.pallas_call(
        body, name=name, grid=(R // tr,), out_shape=[jax.ShapeDtypeStruct((R, C), F32)] * 4,
        in_specs=[row, row, row, gspec], out_specs=[row] * 4, compiler_params=_params(("parallel",)),
    )(w, m, v, g_slots if from_slots else g)


def _unblock(gathered):
    n, r, c = gathered.shape
    return jnp.transpose(gathered, (1, 0, 2)).reshape(r, n * c)


def _reblock(full, c):
    r = full.shape[0]
    return jnp.transpose(full[:, :N_DEV * c].reshape(r, N_DEV, c), (1, 0, 2))


def _step(x, target, w, mom, vel):
    S, D = x.shape[1], x.shape[2]
    L = S + BLK
    sq = lambda a: a.reshape(a.shape[1:]) if a.ndim == 3 else a
    p = {n: sq(w[n]) for n in WEIGHTS if n != 'meta_tokens'}

    names = BIG + ['meta_tokens']
    shards = [p[n].astype(BF16) for n in BIG] + [w['meta_tokens']]
    G = dict(zip(names, _exchange("gather_weights", shards, False)))
    in_s = p['w_in'].shape[1]
    uq_s = p['mla_w_uq'].shape[1]
    w_in_full = _unblock(G['w_in'])
    Wm = dict(
        w_in=jnp.pad(w_in_full, ((0, 0), (0, D_INP - w_in_full.shape[1]))),
        wuq=jnp.pad(_unblock(G['mla_w_uq']).reshape(Q_RANK, HEADS, HD + ROPE),
                    ((0, 0), (0, 0), (0, QH - HD - ROPE))).reshape(Q_RANK, HEADS * QH),
        wuk=_unblock(G['mla_w_uk']), wuv=_unblock(G['mla_w_uv']),
        w_out=G['w_out'].reshape(-1, D))
    meta = _unblock(G['meta_tokens'])

    tabs = _rope_tables(L)
    lg = jnp.broadcast_to(jnp.log(1.0 - 2.0 ** (-5.0 - jnp.arange(HEADS, dtype=F32)))[:, None, None], (HEADS, 8, HD))

    h0 = jnp.concatenate([meta, jnp.zeros((BLK - N_META, D), F32), x[0]], axis=0)
    h1, s1 = _ffn_fwd(h0, p['ffn1_pre_norm'], p['ffn1_post_norm'], G['ffn1_w_gate'], G['ffn1_w_up'], G['ffn1_w_down'])
    h2, sm = _mixer_fwd(h1, p, Wm, tabs, lg)
    h3, s2 = _ffn_fwd(h2, p['ffn2_pre_norm'], p['ffn2_post_norm'], G['ffn2_w_gate'], G['ffn2_w_up'], G['ffn2_w_down'])
    dh3, loss_blk = _loss(h3, target[0])

    dh2, dpre2, dpost2, dwg2, dwu2, dwd2 = _ffn_bwd(dh3, h2, p['ffn2_pre_norm'], p['ffn2_post_norm'],
                                                    G['ffn2_w_gate'], G['ffn2_w_up'], G['ffn2_w_down'], s2)
    dh1, dsmall, dwin, dwuq, dwuk, dwuv, dwout = _mixer_bwd(dh2, h1, p, Wm, tabs, lg, sm)
    dh0, dpre1, dpost1, dwg1, dwu1, dwd1 = _ffn_bwd(dh1, h0, p['ffn1_pre_norm'], p['ffn1_post_norm'],
                                                    G['ffn1_w_gate'], G['ffn1_w_up'], G['ffn1_w_down'], s1)
    dsmall.update(ffn1_pre_norm=dpre1, ffn1_post_norm=dpost1, ffn2_pre_norm=dpre2, ffn2_post_norm=dpost2)

    pieces = [dsmall[n].reshape(-1, 128) for n in SMALL] + [dh0[:N_META].reshape(-1, 128), loss_blk]
    packed = jnp.concatenate(pieces, axis=0)
    packed = jnp.pad(packed, ((0, (-packed.shape[0]) % 8), (0, 0)))
    red = _allreduce_small(packed)
    offs = 0
    gsmall = {}
    for n in SMALL:
        r = p[n].shape[-1] // 128
        gsmall[n] = red[offs:offs + r]
        offs += r
    n_small = offs
    gmeta_full = red[offs:offs + N_META * D // 128].reshape(N_META, D)
    offs += N_META * D // 128
    loss = red[offs, 0]

    partial = dict(
        ffn1_w_gate=dwg1, ffn1_w_up=dwu1, ffn1_w_down=dwd1, ffn2_w_gate=dwg2, ffn2_w_up=dwu2, ffn2_w_down=dwd2,
        w_in=_reblock(dwin, in_s),
        mla_w_uq=_reblock(dwuq.reshape(Q_RANK, HEADS, QH)[:, :, :HD + ROPE].reshape(Q_RANK, HEADS * (HD + ROPE)), uq_s),
        mla_w_uk=_reblock(dwuk, p['mla_w_uk'].shape[1]), mla_w_uv=_reblock(dwuv, p['mla_w_uv'].shape[1]),
        w_out=dwout.reshape(N_DEV, -1, D))
    R = dict(zip(BIG, _exchange("scatter_grads", [partial[n] for n in BIG], True)))

    grad, delta, new_m, new_v = {}, {}, {}, {}
    for n in BIG:
        outs = _adam("adam_" + n, p[n], sq(mom[n]), sq(vel[n]), g_slots=R[n])
        grad[n], delta[n], new_m[n], new_v[n] = [o.reshape(w[n].shape) for o in outs]
    pack = lambda d, f: jnp.concatenate([f(d[n]).reshape(-1, 128) for n in SMALL], axis=0)
    outs = _adam("adam_small", pack(w, sq), pack(mom, sq), pack(vel, sq), g=red[:n_small])
    offs = 0
    for n in SMALL:
        r = p[n].shape[-1] // 128
        grad[n], delta[n], new_m[n], new_v[n] = [o[offs:offs + r].reshape(w[n].shape) for o in outs]
        offs += r
    dev = 4 * lax.axis_index("x") + 2 * lax.axis_index("y") + lax.axis_index("c")
    mcols = w['meta_tokens'].shape[1]
    gmeta = lax.dynamic_slice(gmeta_full, (0, dev * mcols), (N_META, mcols))
    outs = _adam("adam_meta", w['meta_tokens'], mom['meta_tokens'], vel['meta_tokens'], g=gmeta)
    grad['meta_tokens'], delta['meta_tokens'], new_m['meta_tokens'], new_v['meta_tokens'] = outs

    return (loss, dh0[BLK:][None], *[grad[n] for n in WEIGHTS], *[delta[n] for n in WEIGHTS],
            *[new_m[n] for n in WEIGHTS], *[new_v[n] for n in WEIGHTS])


def kernel(x, meta_tokens, ffn1_pre_norm, ffn1_w_gate, ffn1_w_up, ffn1_w_down, ffn1_post_norm, mix_pre_norm, w_in, ret_group_norm, mla_q_norm, mla_w_uq, mla_kv_norm, mla_w_uk, mla_w_uv, w_out, mix_post_norm, ffn2_pre_norm, ffn2_w_gate, ffn2_w_up, ffn2_w_down, ffn2_post_norm, loss_target, m_meta_tokens, m_ffn1_pre_norm, m_ffn1_w_gate, m_ffn1_w_up, m_ffn1_w_down, m_ffn1_post_norm, m_mix_pre_norm, m_w_in, m_ret_group_norm, m_mla_q_norm, m_mla_w_uq, m_mla_kv_norm, m_mla_w_uk, m_mla_w_uv, m_w_out, m_mix_post_norm, m_ffn2_pre_norm, m_ffn2_w_gate, m_ffn2_w_up, m_ffn2_w_down, m_ffn2_post_norm, v_meta_tokens, v_ffn1_pre_norm, v_ffn1_w_gate, v_ffn1_w_up, v_ffn1_w_down, v_ffn1_post_norm, v_mix_pre_norm, v_w_in, v_ret_group_norm, v_mla_q_norm, v_mla_w_uq, v_mla_kv_norm, v_mla_w_uk, v_mla_w_uv, v_w_out, v_mix_post_norm, v_ffn2_pre_norm, v_ffn2_w_gate, v_ffn2_w_up, v_ffn2_w_down, v_ffn2_post_norm):
    w = dict(zip(WEIGHTS, (meta_tokens, ffn1_pre_norm, ffn1_w_gate, ffn1_w_up, ffn1_w_down, ffn1_post_norm,
                           mix_pre_norm, w_in, ret_group_norm, mla_q_norm, mla_w_uq, mla_kv_norm, mla_w_uk, mla_w_uv,
                           w_out, mix_post_norm, ffn2_pre_norm, ffn2_w_gate, ffn2_w_up, ffn2_w_down, ffn2_post_norm)))
    mom = dict(zip(WEIGHTS, (m_meta_tokens, m_ffn1_pre_norm, m_ffn1_w_gate, m_ffn1_w_up, m_ffn1_w_down,
                             m_ffn1_post_norm, m_mix_pre_norm, m_w_in, m_ret_group_norm, m_mla_q_norm, m_mla_w_uq,
                             m_mla_kv_norm, m_mla_w_uk, m_mla_w_uv, m_w_out, m_mix_post_norm, m_ffn2_pre_norm,
                             m_ffn2_w_gate, m_ffn2_w_up, m_ffn2_w_down, m_ffn2_post_norm)))
    vel = dict(zip(WEIGHTS, (v_meta_tokens, v_ffn1_pre_norm, v_ffn1_w_gate, v_ffn1_w_up, v_ffn1_w_down,
                             v_ffn1_post_norm, v_mix_pre_norm, v_w_in, v_ret_group_norm, v_mla_q_norm, v_mla_w_uq,
                             v_mla_kv_norm, v_mla_w_uk, v_mla_w_uv, v_w_out, v_mix_post_norm, v_ffn2_pre_norm,
                             v_ffn2_w_gate, v_ffn2_w_up, v_ffn2_w_down, v_ffn2_post_norm)))
    return _step(x, loss_target, w, mom, vel)
```

```python
import functools
import math

import jax
import jax.numpy as jnp
from jax import lax
from jax.experimental import pallas as pl
from jax.experimental.pallas import tpu as pltpu

N_DEV = 8
N_META = 16
BLK = 128
HEADS = 8
HD = 128
ROPE = 64
Q_RANK = 512
KV_RANK = 256
QH = 2 * HD
D_INP = 4 * HEADS * HD + Q_RANK + KV_RANK + BLK
ROPE_THETA = 10000.0
EPS = 1e-6
ADAM_LR = 0.001
ADAM_B1 = 0.9
ADAM_B2 = 0.999
ADAM_EPS = 1e-08
ADAM_WD = 0.01
ADAM_STEP = 10
V7X_VMEM_LIMIT = 48 * 1024 * 1024
MESH = pl.DeviceIdType.MESH
F32 = jnp.float32
BF16 = jnp.bfloat16

WEIGHTS = ['meta_tokens', 'ffn1_pre_norm', 'ffn1_w_gate', 'ffn1_w_up', 'ffn1_w_down', 'ffn1_post_norm',
           'mix_pre_norm', 'w_in', 'ret_group_norm', 'mla_q_norm', 'mla_w_uq', 'mla_kv_norm', 'mla_w_uk',
           'mla_w_uv', 'w_out', 'mix_post_norm', 'ffn2_pre_norm', 'ffn2_w_gate', 'ffn2_w_up', 'ffn2_w_down',
           'ffn2_post_norm']
SMALL = ['ffn1_pre_norm', 'ffn1_post_norm', 'mix_pre_norm', 'ret_group_norm', 'mla_q_norm', 'mla_kv_norm',
         'mix_post_norm', 'ffn2_pre_norm', 'ffn2_post_norm']
BIG = ['ffn1_w_gate', 'ffn1_w_up', 'ffn1_w_down', 'w_in', 'mla_w_uq', 'mla_w_uk', 'mla_w_uv', 'w_out',
       'ffn2_w_gate', 'ffn2_w_up', 'ffn2_w_down']

_DIMS = {'nn': (((1,), (0,)), ((), ())), 'nt': (((1,), (1,)), ((), ())), 'tn': (((0,), (0,)), ((), ()))}


def _tile(n, target, mult=16):
    best = None
    for t in range(mult, min(n, target) + 1, mult):
        if n % t == 0:
            best = t
    return best if best is not None else n


def _params(sem):
    return pltpu.CompilerParams(dimension_semantics=sem, vmem_limit_bytes=V7X_VMEM_LIMIT)


def _dot(a, b, dims):
    return lax.dot_general(a, b, _DIMS[dims], preferred_element_type=F32)


def _sigmoid(x):
    return 1.0 / (1.0 + jnp.exp(-x))


def _me_and_peers():
    x, y, c = lax.axis_index("x"), lax.axis_index("y"), lax.axis_index("c")

    def peer(j):
        px = 1 - x if (j >> 2) & 1 else x
        py = 1 - y if (j >> 1) & 1 else y
        pc = 1 - c if j & 1 else c
        return (px, py, pc), 4 * px + 2 * py + pc

    return 4 * x + 2 * y + c, peer


class _Exchange:
    def __init__(self, arrays, per_peer):
        self.arrays = list(arrays)
        self.per_peer = per_peer
        self.n = len(self.arrays)
        self.out_shapes = [jax.ShapeDtypeStruct((N_DEV,) + tuple(a.shape[1:] if per_peer else a.shape), a.dtype)
                           for a in self.arrays]
        self.specs = [pl.BlockSpec(memory_space=pl.ANY)] * self.n
        self.scratch = [pltpu.SemaphoreType.DMA((7 * self.n,)), pltpu.SemaphoreType.DMA((7 * self.n,)),
                        pltpu.SemaphoreType.DMA((self.n,))]

    def _copies(self, src, dst, sems):
        send_sems, recv_sems, local_sems = sems
        me, peer = _me_and_peers()
        local, sends, recvs = [], [], []
        for k in range(self.n):
            own = src[k].at[me] if self.per_peer else src[k]
            local.append(pltpu.make_async_copy(own, dst[k].at[me], local_sems.at[k]))
            for j in range(1, N_DEV):
                pid, pidx = peer(j)
                out = src[k].at[pidx] if self.per_peer else src[k]
                sem = dict(send_sem=send_sems.at[k * 7 + j - 1], recv_sem=recv_sems.at[k * 7 + j - 1], device_id=pid,
                           device_id_type=MESH)
                sends.append(pltpu.make_async_remote_copy(src_ref=out, dst_ref=dst[k].at[me], **sem))
                recvs.append(pltpu.make_async_remote_copy(src_ref=out, dst_ref=dst[k].at[pidx], **sem))
        return local, sends, recvs

    def start(self, src, dst, sems):
        local, sends, _ = self._copies(src, dst, sems)
        for cp in local + sends:
            cp.start()

    def finish(self, src, dst, sems):
        local, sends, recvs = self._copies(src, dst, sems)
        for cp in recvs:
            cp.wait_recv()
        for cp in sends:
            cp.wait_send()
        for cp in local:
            cp.wait()


def _grid_edges(grid):
    first, last = None, None
    for a, n in enumerate(grid):
        f, l = pl.program_id(a) == 0, pl.program_id(a) == n - 1
        first = f if first is None else first & f
        last = l if last is None else last & l
    return first, last


def _exchange(name, arrays, per_peer):
    ex = _Exchange(arrays, per_peer)
    n = ex.n

    def body(*refs):
        ex.start(refs[:n], refs[n:2 * n], refs[2 * n:])
        ex.finish(refs[:n], refs[n:2 * n], refs[2 * n:])

    return pl.pallas_call(body, name=name, out_shape=ex.out_shapes, in_specs=ex.specs, out_specs=ex.specs,
                          scratch_shapes=ex.scratch)(*arrays)


def _allreduce_small(v):
    rows = v.shape[0]

    def body(v_ref, out_ref, buf, send_sems, recv_sems):
        me, peer = _me_and_peers()
        buf[pl.ds(me, 1)] = v_ref[...][None]
        sends = []
        for j in range(1, N_DEV):
            pid, _ = peer(j)
            cp = pltpu.make_async_remote_copy(src_ref=v_ref, dst_ref=buf.at[me], send_sem=send_sems.at[j - 1],
                                              recv_sem=recv_sems.at[j - 1], device_id=pid, device_id_type=MESH)
            cp.start()
            sends.append(cp)
        for j in range(1, N_DEV):
            pid, pidx = peer(j)
            pltpu.make_async_remote_copy(src_ref=v_ref, dst_ref=buf.at[pidx], send_sem=send_sems.at[j - 1],
                                         recv_sem=recv_sems.at[j - 1], device_id=pid,
                                         device_id_type=MESH).wait_recv()
        for cp in sends:
            cp.wait_send()
        acc = buf[0]
        for s in range(1, N_DEV):
            acc = acc + buf[s]
        out_ref[...] = acc

    vm = pl.BlockSpec(memory_space=pltpu.VMEM)
    return pl.pallas_call(
        body, name="allreduce_small", out_shape=jax.ShapeDtypeStruct(v.shape, F32),
        in_specs=[vm], out_specs=vm,
        scratch_shapes=[pltpu.VMEM((N_DEV, rows, 128), F32), pltpu.SemaphoreType.DMA((7,)),
                        pltpu.SemaphoreType.DMA((7,))],
    )(v)


def _mm(name, grid, sem, k_axis, ops, op_specs, pairs, acc_shapes, extras, extra_specs, epilogue, outs, out_specs,
        comm=None):
    n_op, n_ex, n_out = len(ops), len(extras), len(outs)
    nk = grid[k_axis] if k_axis is not None else 1
    n_acc = len(acc_shapes) if nk > 1 else 0
    n_cm = comm.n if comm is not None else 0

    def body(*refs):
        op_refs = refs[:n_op]
        ex_refs = refs[n_op:n_op + n_ex]
        n_in = n_op + n_ex + n_cm
        out_refs = refs[n_in:n_in + n_out]
        acc_refs = refs[n_in + n_out + n_cm:n_in + n_out + n_cm + n_acc]
        if comm is not None:
            cm_refs = (refs[n_op + n_ex:n_in], refs[n_in + n_out:n_in + n_out + n_cm],
                       refs[n_in + n_out + n_cm + n_acc:])
            first, last = _grid_edges(grid)

            @pl.when(first)
            def _():
                comm.start(*cm_refs)

        parts = [None] * len(acc_shapes)
        for li, ri, dims, ai in pairs:
            d = _dot(op_refs[li][...], op_refs[ri][...], dims)
            parts[ai] = d if parts[ai] is None else parts[ai] + d

        def finish(vals):
            res = epilogue(*vals, *[e[...] for e in ex_refs])
            for o, r in zip(out_refs, res):
                o[...] = r.astype(o.dtype)

        if nk == 1:
            finish(parts)
        else:
            k = pl.program_id(k_axis)

            @pl.when(k == 0)
            def _():
                for a, p in zip(acc_refs, parts):
                    a[...] = p

            @pl.when(k > 0)
            def _():
                for a, p in zip(acc_refs, parts):
                    a[...] += p

            @pl.when(k == nk - 1)
            def _():
                finish([a[...] for a in acc_refs])

        if comm is not None:
            @pl.when(last)
            def _():
                comm.finish(*cm_refs)

    scratch = [pltpu.VMEM(s, F32) for s in acc_shapes] if nk > 1 else []
    if comm is None:
        return pl.pallas_call(
            body, name=name, grid=grid, out_shape=outs,
            in_specs=list(op_specs) + list(extra_specs), out_specs=list(out_specs),
            scratch_shapes=scratch, compiler_params=_params(sem),
        )(*ops, *extras)
    res = pl.pallas_call(
        body, name=name, grid=grid, out_shape=list(outs) + comm.out_shapes,
        in_specs=list(op_specs) + list(extra_specs) + comm.specs, out_specs=list(out_specs) + comm.specs,
        scratch_shapes=scratch + comm.scratch, compiler_params=_params(("arbitrary",) * len(grid)),
    )(*ops, *extras, *comm.arrays)
    return res[:n_out], res[n_out:]


def _with_comm(res, comm, pick):
    if comm is None:
        return pick(res)
    return pick(res[0]), res[1]


def _mm_nn(name, a, w, out_dtype, tm_target=704, tn_target=1664, epilogue=None, extras=(), extra_specs=(), comm=None):
    L, K = a.shape
    N = w.shape[1]
    tm, tn = _tile(L, tm_target), _tile(N, tn_target, 128)
    ep = epilogue if epilogue is not None else (lambda acc: (acc,))
    res = _mm(name, (L // tm, N // tn), ("parallel", "parallel"), None,
              [a, w], [pl.BlockSpec((tm, K), lambda i, j: (i, 0)), pl.BlockSpec((K, tn), lambda i, j: (0, j))],
              [(0, 1, 'nn', 0)], [(tm, tn)], list(extras), list(extra_specs), ep,
              [jax.ShapeDtypeStruct((L, N), out_dtype)], [pl.BlockSpec((tm, tn), lambda i, j: (i, j))], comm=comm)
    return _with_comm(res, comm, lambda o: o[0])


def _mm_nt(name, pairs_aw, out_dtype, tm_target=704, tn_target=512):
    L = pairs_aw[0][0].shape[0]
    N = pairs_aw[0][1].shape[0]
    tm, tn = _tile(L, tm_target), _tile(N, tn_target, 128)
    ops, specs, pairs = [], [], []
    for t, (a, w) in enumerate(pairs_aw):
        K = a.shape[1]
        ops += [a, w]
        specs += [pl.BlockSpec((tm, K), lambda i, j: (i, 0)), pl.BlockSpec((tn, K), lambda i, j: (j, 0))]
        pairs.append((2 * t, 2 * t + 1, 'nt', 0))
    return _mm(name, (L // tm, N // tn), ("parallel", "parallel"), None, ops, specs, pairs, [(tm, tn)], [], [],
               lambda acc: (acc,), [jax.ShapeDtypeStruct((L, N), out_dtype)],
               [pl.BlockSpec((tm, tn), lambda i, j: (i, j))])[0]


def _mm_tn(name, a, bs, out_dtype=BF16, tk_target=528, tn_target=1664, comm=None):
    L, M = a.shape
    N = bs[0].shape[1]
    tk, tn = _tile(L, tk_target), _tile(N, tn_target, 128)
    nb = len(bs)
    ops = [a] + list(bs)
    specs = [pl.BlockSpec((tk, M), lambda j, k: (k, 0))] + [pl.BlockSpec((tk, tn), lambda j, k: (k, j))] * nb
    res = _mm(name, (N // tn, L // tk), ("parallel", "arbitrary"), 1, ops, specs,
              [(0, 1 + t, 'tn', t) for t in range(nb)], [(M, tn)] * nb, [], [], lambda *acc: acc,
              [jax.ShapeDtypeStruct((M, N), out_dtype)] * nb, [pl.BlockSpec((M, tn), lambda j, k: (0, j))] * nb,
              comm=comm)
    return _with_comm(res, comm, lambda o: o)


def _norm_fwd(x, w):
    L, D = x.shape
    tr = _tile(L, 512)

    def body(x_ref, w_ref, y_ref):
        v = x_ref[...]
        r = lax.rsqrt(jnp.mean(v * v, axis=-1, keepdims=True) + EPS)
        y_ref[...] = (v * r * w_ref[...]).astype(y_ref.dtype)

    return pl.pallas_call(
        body, name="norm_fwd", grid=(L // tr,), out_shape=jax.ShapeDtypeStruct((L, D), BF16),
        in_specs=[pl.BlockSpec((tr, D), lambda i: (i, 0)), pl.BlockSpec((1, D), lambda i: (0, 0))],
        out_specs=pl.BlockSpec((tr, D), lambda i: (i, 0)), compiler_params=_params(("parallel",)),
    )(x, w)


def _norm_bwd_math(x, w, dy):
    r = lax.rsqrt(jnp.mean(x * x, axis=-1, keepdims=True) + EPS)
    gy = dy * w
    dx = r * (gy - x * (r * r) * jnp.mean(gy * x, axis=-1, keepdims=True))
    dw = jnp.sum(dy * x * r, axis=0, keepdims=True)
    return dx, dw


def _norm_bwd(x, w, dy, res, scale, out_dtype):
    L, D = x.shape
    tr = _tile(L, 384)
    has_res = res is not None

    def body(*refs):
        x_ref, w_ref, dy_ref = refs[:3]
        res_ref = refs[3] if has_res else None
        dx_ref, dw_ref = refs[-2:]
        dx, dw = _norm_bwd_math(x_ref[...], w_ref[...], dy_ref[...].astype(F32))
        dx = scale * dx
        if has_res:
            dx = dx + res_ref[...]
        dx_ref[...] = dx.astype(dx_ref.dtype)

        @pl.when(pl.program_id(0) == 0)
        def _():
            dw_ref[...] = jnp.zeros_like(dw_ref)

        dw_ref[...] += scale * dw

    row = pl.BlockSpec((tr, D), lambda i: (i, 0))
    vec = pl.BlockSpec((1, D), lambda i: (0, 0))
    return pl.pallas_call(
        body, name="norm_bwd", grid=(L // tr,),
        out_shape=[jax.ShapeDtypeStruct((L, D), out_dtype), jax.ShapeDtypeStruct((1, D), F32)],
        in_specs=[row, vec, row] + ([row] if has_res else []), out_specs=[row, vec],
        compiler_params=_params(("arbitrary",)),
    )(*([x, w, dy] + ([res] if has_res else [])))


def _loss(h, target):
    L, D = h.shape

    def body(h_ref, t_ref, dh_ref, loss_ref):
        i = pl.program_id(0)

        @pl.when(i == 0)
        def _():
            dh_ref[...] = jnp.zeros_like(dh_ref)
            loss_ref[...] = jnp.zeros_like(loss_ref)

        @pl.when(i > 0)
        def _():
            diff = h_ref[...] - t_ref[...]
            dh_ref[...] = diff * (1.0 / D)
            loss_ref[...] += 0.5 * jnp.sum(diff * diff) * (1.0 / D)

    return pl.pallas_call(
        body, name="loss", grid=(L // BLK,),
        out_shape=[jax.ShapeDtypeStruct((L, D), F32), jax.ShapeDtypeStruct((8, 128), F32)],
        in_specs=[pl.BlockSpec((BLK, D), lambda i: (i, 0)),
                  pl.BlockSpec((BLK, D), lambda i: (jnp.maximum(i - 1, 0), 0))],
        out_specs=[pl.BlockSpec((BLK, D), lambda i: (i, 0)), pl.BlockSpec((8, 128), lambda i: (0, 0))],
        compiler_params=_params(("arbitrary",)),
    )(h, target)


def _ffn_up(a, wg, wu, comm=None):
    L, D = a.shape
    F = wg.shape[2]
    tm = _tile(L, 704)

    def ep(g, u):
        return g, u, g * _sigmoid(g) * u

    hspec = pl.BlockSpec((None, tm, F), lambda i, j: (j, i, 0))
    wspec = pl.BlockSpec((None, D, F), lambda i, j: (j, 0, 0))
    res = _mm("ffn_up", (L // tm, N_DEV), ("parallel", "parallel"), None,
              [a, wg, wu], [pl.BlockSpec((tm, D), lambda i, j: (i, 0)), wspec, wspec],
              [(0, 1, 'nn', 0), (0, 2, 'nn', 1)], [(tm, F)] * 2, [], [], ep,
              [jax.ShapeDtypeStruct((N_DEV, L, F), BF16)] * 3, [hspec] * 3, comm=comm)
    return _with_comm(res, comm, lambda o: o)


def _resnorm_epilogue(scale):
    def ep(acc, h, w):
        r = lax.rsqrt(jnp.mean(acc * acc, axis=-1, keepdims=True) + EPS)
        return acc, h + scale * (acc * r * w)
    return ep


def _ffn_down(hid, wd, h_in, post, comm=None):
    _, L, F = hid.shape
    D = wd.shape[2]
    tm = _tile(L, 384)
    row = pl.BlockSpec((tm, D), lambda i, j: (i, 0))
    res = _mm("ffn_down", (L // tm, N_DEV), ("parallel", "arbitrary"), 1,
              [hid, wd], [pl.BlockSpec((None, tm, F), lambda i, j: (j, i, 0)),
                          pl.BlockSpec((None, F, D), lambda i, j: (j, 0, 0))],
              [(0, 1, 'nn', 0)], [(tm, D)], [h_in, post], [row, pl.BlockSpec((1, D), lambda i, j: (0, 0))],
              _resnorm_epilogue(0.5), [jax.ShapeDtypeStruct((L, D), F32)] * 2, [row, row], comm=comm)
    return _with_comm(res, comm, lambda o: o)


def _ffn_dhid(df, wd, g, u, comm=None):
    L, D = df.shape
    F = wd.shape[1]
    tm = _tile(L, 704)

    def ep(dhid, g_, u_):
        g32, u32 = g_.astype(F32), u_.astype(F32)
        sg = _sigmoid(g32)
        return dhid * u32 * sg * (1.0 + g32 * (1.0 - sg)), dhid * g32 * sg

    hspec = pl.BlockSpec((None, tm, F), lambda i, j: (j, i, 0))
    res = _mm("ffn_dhid", (L // tm, N_DEV), ("parallel", "parallel"), None,
              [df, wd], [pl.BlockSpec((tm, D), lambda i, j: (i, 0)),
                         pl.BlockSpec((None, F, D), lambda i, j: (j, 0, 0))],
              [(0, 1, 'nt', 0)], [(tm, F)], [g, u], [hspec, hspec], ep,
              [jax.ShapeDtypeStruct((N_DEV, L, F), BF16)] * 2, [hspec, hspec], comm=comm)
    return _with_comm(res, comm, lambda o: o)


def _ffn_dwd(hid, df, comm=None):
    _, L, F = hid.shape
    D = df.shape[1]
    tk = _tile(L, 528)
    res = _mm("ffn_dwd", (N_DEV, L // tk), ("parallel", "arbitrary"), 1,
              [hid, df], [pl.BlockSpec((None, tk, F), lambda j, k: (j, k, 0)),
                          pl.BlockSpec((tk, D), lambda j, k: (k, 0))],
              [(0, 1, 'tn', 0)], [(F, D)], [], [], lambda acc: (acc,),
              [jax.ShapeDtypeStruct((N_DEV, F, D), BF16)], [pl.BlockSpec((None, F, D), lambda j, k: (j, 0, 0))],
              comm=comm)
    return _with_comm(res, comm, lambda o: o[0])


def _ffn_dwgu(a, dg, du, comm=None):
    L, D = a.shape
    F = dg.shape[2]
    tk = _tile(L, 528)
    hspec = pl.BlockSpec((None, tk, F), lambda j, k: (j, k, 0))
    wspec = pl.BlockSpec((None, D, F), lambda j, k: (j, 0, 0))
    res = _mm("ffn_dwgu", (N_DEV, L // tk), ("parallel", "arbitrary"), 1,
              [a, dg, du], [pl.BlockSpec((tk, D), lambda j, k: (k, 0)), hspec, hspec],
              [(0, 1, 'tn', 0), (0, 2, 'tn', 1)], [(D, F)] * 2, [], [], lambda *acc: acc,
              [jax.ShapeDtypeStruct((N_DEV, D, F), BF16)] * 2, [wspec, wspec], comm=comm)
    return _with_comm(res, comm, lambda o: o)


def _ffn_da(dg, du, wg, wu, comm=None):
    _, L, F = dg.shape
    D = wg.shape[1]
    tm = _tile(L, 384)
    hspec = pl.BlockSpec((None, tm, F), lambda i, j: (j, i, 0))
    wspec = pl.BlockSpec((None, D, F), lambda i, j: (j, 0, 0))
    row = pl.BlockSpec((tm, D), lambda i, j: (i, 0))
    res = _mm("ffn_da", (L // tm, N_DEV), ("parallel", "arbitrary"), 1,
              [dg, du, wg, wu], [hspec, hspec, wspec, wspec],
              [(0, 2, 'nt', 0), (1, 3, 'nt', 0)], [(tm, D)], [], [], lambda acc: (acc,),
              [jax.ShapeDtypeStruct((L, D), F32)], [row], comm=comm)
    return _with_comm(res, comm, lambda o: o[0])


def _rope_tables(L):
    rows = jnp.arange(L, dtype=F32)
    pos = jnp.where(rows < BLK, rows, rows - (BLK - N_META))
    inv_r = ROPE_THETA ** (-jnp.arange(0, HD, 2, dtype=F32) / HD)
    ang_r = pos[:, None] * inv_r[None, :]
    cr = jnp.concatenate([jnp.cos(ang_r), jnp.cos(ang_r)], axis=1)
    sr = jnp.concatenate([-jnp.sin(ang_r), jnp.sin(ang_r)], axis=1)
    inv_m = ROPE_THETA ** (-jnp.arange(0, ROPE, 2, dtype=F32) / ROPE)
    ang_m = pos[:, None] * inv_m[None, :]
    z32 = jnp.zeros((L, ROPE // 2), F32)
    z64 = jnp.zeros((L, HD - ROPE), F32)
    cm = jnp.concatenate([jnp.cos(ang_m), jnp.cos(ang_m), z64], axis=1)
    sa = jnp.concatenate([-jnp.sin(ang_m), z32, z64], axis=1)
    sb = jnp.concatenate([z32, jnp.sin(ang_m), z64], axis=1)
    return cr, sr, cm, sa, sb


def _rope_ret(x, cr, sr):
    return x * cr + pltpu.roll(x, HD // 2, 1) * sr


def _rope_ret_t(d, cr, sr):
    return d * cr + pltpu.roll(d * sr, HD // 2, 1)


def _rope_mla(x, cm, sa, sb):
    return x * cm + pltpu.roll(x, HD - ROPE // 2, 1) * sa + pltpu.roll(x, ROPE // 2, 1) * sb


def _rope_mla_t(d, cm, sa, sb):
    return d * cm + pltpu.roll(d * sa, ROPE // 2, 1) + pltpu.roll(d * sb, HD - ROPE // 2, 1)


C_RQ, C_RK, C_RV, C_RG = 0, HEADS * HD, 2 * HEADS * HD, 3 * HEADS * HD
C_CQ = 4 * HEADS * HD
C_CKV = C_CQ + Q_RANK
C_KR = C_CKV + KV_RANK
RET_K_SCALE = HD ** -0.5


def _prep(proj, tabs, qn, kvn):
    L = proj.shape[0]
    tr = _tile(L, 256)
    W = HEADS * HD

    def body(p_ref, cr_ref, sr_ref, cm_ref, sa_ref, sb_ref, qn_ref, kvn_ref, q_ref, k_ref, v_ref, cq_ref, ckv_ref,
             kr_ref):
        cr, sr = cr_ref[...], sr_ref[...]
        for h in range(HEADS):
            sl = slice(h * HD, (h + 1) * HD)
            q_ref[:, sl] = _rope_ret(p_ref[:, C_RQ + h * HD:C_RQ + (h + 1) * HD], cr, sr).astype(BF16)
            k_ref[:, sl] = (_rope_ret(p_ref[:, C_RK + h * HD:C_RK + (h + 1) * HD], cr, sr)
                            * RET_K_SCALE).astype(BF16)
        v_ref[...] = p_ref[:, C_RV:C_RV + W].astype(BF16)
        cq = p_ref[:, C_CQ:C_CQ + Q_RANK]
        cq_ref[...] = (cq * lax.rsqrt(jnp.mean(cq * cq, axis=-1, keepdims=True) + EPS) * qn_ref[...]).astype(BF16)
        ckv = p_ref[:, C_CKV:C_CKV + KV_RANK]
        ckv_ref[...] = (ckv * lax.rsqrt(jnp.mean(ckv * ckv, axis=-1, keepdims=True) + EPS)
                        * kvn_ref[...]).astype(BF16)
        kr_ref[...] = _rope_mla(p_ref[:, C_KR:C_KR + HD], cm_ref[...], sa_ref[...], sb_ref[...]).astype(BF16)

    row = lambda w: pl.BlockSpec((tr, w), lambda i: (i, 0))
    vec = lambda w: pl.BlockSpec((1, w), lambda i: (0, 0))
    return pl.pallas_call(
        body, name="mix_prep", grid=(L // tr,),
        out_shape=[jax.ShapeDtypeStruct((L, W), BF16)] * 3 + [jax.ShapeDtypeStruct((L, Q_RANK), BF16),
                                                              jax.ShapeDtypeStruct((L, KV_RANK), BF16),
                                                              jax.ShapeDtypeStruct((L, HD), BF16)],
        in_specs=[row(D_INP)] + [row(HD)] * 5 + [vec(Q_RANK), vec(KV_RANK)],
        out_specs=[row(W)] * 3 + [row(Q_RANK), row(KV_RANK), row(HD)],
        compiler_params=_params(("parallel",)),
    )(proj, *tabs, qn, kvn)


def _prep_bwd(proj, dq, dk, dv, drg, dcqn, dckvn, dkr8, tabs, qn, kvn):
    L = proj.shape[0]
    tr = _tile(L, 192)
    W = HEADS * HD

    def body(p_ref, dq_ref, dk_ref, dv_ref, drg_ref, dcq_ref, dckv_ref, dkr_ref, cr_ref, sr_ref, cm_ref, sa_ref,
             sb_ref, qn_ref, kvn_ref, dp_ref, dqn_ref, dkvn_ref):
        cr, sr = cr_ref[...], sr_ref[...]
        dkr = None
        for h in range(HEADS):
            sl = slice(h * HD, (h + 1) * HD)
            dp_ref[:, C_RQ + h * HD:C_RQ + (h + 1) * HD] = _rope_ret_t(dq_ref[:, sl], cr, sr).astype(BF16)
            dp_ref[:, C_RK + h * HD:C_RK + (h + 1) * HD] = (_rope_ret_t(dk_ref[:, sl], cr, sr)
                                                            * RET_K_SCALE).astype(BF16)
            part = dkr_ref[:, sl]
            dkr = part if dkr is None else dkr + part
        dp_ref[:, C_RV:C_RV + W] = dv_ref[...].astype(BF16)
        dp_ref[:, C_RG:C_RG + W] = drg_ref[...].astype(BF16)
        dcq, dqn = _norm_bwd_math(p_ref[:, C_CQ:C_CQ + Q_RANK], qn_ref[...], dcq_ref[...])
        dp_ref[:, C_CQ:C_CQ + Q_RANK] = dcq.astype(BF16)
        dckv, dkvn = _norm_bwd_math(p_ref[:, C_CKV:C_CKV + KV_RANK], kvn_ref[...], dckv_ref[...])
        dp_ref[:, C_CKV:C_CKV + KV_RANK] = dckv.astype(BF16)
        dp_ref[:, C_KR:C_KR + HD] = _rope_mla_t(dkr, cm_ref[...], sa_ref[...], sb_ref[...]).astype(BF16)

        @pl.when(pl.program_id(0) == 0)
        def _():
            dqn_ref[...] = jnp.zeros_like(dqn_ref)
            dkvn_ref[...] = jnp.zeros_like(dkvn_ref)

        dqn_ref[...] += dqn
        dkvn_ref[...] += dkvn

    row = lambda w: pl.BlockSpec((tr, w), lambda i: (i, 0))
    vec = lambda w: pl.BlockSpec((1, w), lambda i: (0, 0))
    return pl.pallas_call(
        body, name="mix_prep_bwd", grid=(L // tr,),
        out_shape=[jax.ShapeDtypeStruct((L, D_INP), BF16), jax.ShapeDtypeStruct((1, Q_RANK), F32),
                   jax.ShapeDtypeStruct((1, KV_RANK), F32)],
        in_specs=[row(D_INP)] + [row(W)] * 4 + [row(Q_RANK), row(KV_RANK), row(W)] + [row(HD)] * 5
                 + [vec(Q_RANK), vec(KV_RANK)],
        out_specs=[row(D_INP), vec(Q_RANK), vec(KV_RANK)],
        compiler_params=_params(("arbitrary",)),
    )(proj, dq, dk, dv, drg, dcqn, dckvn, dkr8, *tabs, qn, kvn)


def _post(o_ret, proj, gn):
    L, W = o_ret.shape
    tr = _tile(L, 384)

    def body(o_ref, rg_ref, gn_ref, out_ref):
        for h in range(HEADS):
            sl = slice(h * HD, (h + 1) * HD)
            o = o_ref[:, sl]
            rg = rg_ref[:, sl]
            n = o * lax.rsqrt(jnp.mean(o * o, axis=-1, keepdims=True) + EPS)
            out_ref[:, sl] = (n * gn_ref[:, sl] * (rg * _sigmoid(rg))).astype(BF16)

    row = pl.BlockSpec((tr, W), lambda i: (i, 0))
    return pl.pallas_call(
        body, name="ret_post", grid=(L // tr,), out_shape=jax.ShapeDtypeStruct((L, W), BF16),
        in_specs=[row, pl.BlockSpec((tr, W), lambda i: (i, C_RG // W)), pl.BlockSpec((1, W), lambda i: (0, 0))],
        out_specs=row, compiler_params=_params(("parallel",)),
    )(o_ret, proj, gn)


def _post_bwd(o_ret, proj, gn, dcat):
    L, W = o_ret.shape
    tr = _tile(L, 384)

    def body(o_ref, rg_ref, gn_ref, d_ref, do_ref, drg_ref, dgn_ref):
        @pl.when(pl.program_id(0) == 0)
        def _():
            dgn_ref[...] = jnp.zeros_like(dgn_ref)

        for h in range(HEADS):
            sl = slice(h * HD, (h + 1) * HD)
            o = o_ref[:, sl]
            rg = rg_ref[:, sl]
            d = d_ref[:, sl].astype(F32)
            gw = gn_ref[:, sl]
            r = lax.rsqrt(jnp.mean(o * o, axis=-1, keepdims=True) + EPS)
            n = o * r
            sg = _sigmoid(rg)
            si = rg * sg
            dn = d * gw * si
            dgn_ref[:, sl] += jnp.sum(d * n * si, axis=0, keepdims=True)
            drg_ref[:, sl] = d * n * gw * sg * (1.0 + rg * (1.0 - sg))
            do_ref[:, sl] = (r * (dn - o * (r * r) * jnp.mean(dn * o, axis=-1, keepdims=True))).astype(BF16)

    row = pl.BlockSpec((tr, W), lambda i: (i, 0))
    vec = pl.BlockSpec((1, W), lambda i: (0, 0))
    return pl.pallas_call(
        body, name="ret_post_bwd", grid=(L // tr,),
        out_shape=[jax.ShapeDtypeStruct((L, W), BF16), jax.ShapeDtypeStruct((L, W), F32),
                   jax.ShapeDtypeStruct((1, W), F32)],
        in_specs=[row, pl.BlockSpec((tr, W), lambda i: (i, C_RG // W)), vec, row],
        out_specs=[row, row, vec], compiler_params=_params(("arbitrary",)),
    )(o_ret, proj, gn, dcat)


def _lin_attn(name, q, k, v, lg, reverse):
    L, W = q.shape
    nc = L // BLK - 1

    def body(q_ref, k_ref, v_ref, lg_ref, o_ref, s_ref):
        lgv = lg_ref[0:1, :]
        n = lax.broadcasted_iota(jnp.int32, (BLK, BLK), 0).astype(F32)
        m = lax.broadcasted_iota(jnp.int32, (BLK, BLK), 1).astype(F32)
        dist = (m - n) if reverse else (n - m)
        dmask = jnp.where(dist >= 0, jnp.exp(lgv * jnp.maximum(dist, 0.0)), 0.0)
        dmask0 = jnp.where((n < N_META) & (m < N_META), dmask, 0.0)
        gl = jnp.exp(lgv * float(BLK))
        if reverse:
            inter = jnp.exp(lgv * (float(BLK) - n))
            inter0 = jnp.where(n < N_META, jnp.exp(lgv * jnp.maximum(float(N_META) - n, 0.0)), 0.0)
            upd = jnp.exp(lgv * n)
        else:
            inter = jnp.exp(lgv * (n + 1.0))
            upd = jnp.exp(lgv * (float(BLK) - 1.0 - n))
            upd0 = jnp.where(n < N_META, jnp.exp(lgv * jnp.maximum(float(N_META) - 1.0 - n, 0.0)), 0.0)

        def chunk(c):
            rows = pl.ds(pl.multiple_of(c * BLK, BLK), BLK)
            qc, kc, vc = q_ref[rows, :], k_ref[rows, :], v_ref[rows, :]
            a = _dot(qc, kc, 'nt') * dmask
            o = _dot(a.astype(BF16), vc, 'nn') + _dot(qc, s_ref[...].astype(BF16), 'nn') * inter
            o_ref[rows, :] = o
            s_ref[...] = s_ref[...] * gl + _dot((kc.astype(F32) * upd).astype(BF16), vc, 'tn')

        q0, k0, v0 = q_ref[0:BLK, :], k_ref[0:BLK, :], v_ref[0:BLK, :]
        a0 = _dot(q0, k0, 'nt') * dmask0
        if reverse:
            s_ref[...] = jnp.zeros_like(s_ref)

            def step(t, carry):
                chunk(nc - t)
                return carry

            lax.fori_loop(0, nc, step, 0)
            o_ref[0:BLK, :] = _dot(a0.astype(BF16), v0, 'nn') + _dot(q0, s_ref[...].astype(BF16), 'nn') * inter0
        else:
            o_ref[0:BLK, :] = _dot(a0.astype(BF16), v0, 'nn')
            s_ref[...] = _dot((k0.astype(F32) * upd0).astype(BF16), v0, 'tn')

            def step(t, carry):
                chunk(t + 1)
                return carry

            lax.fori_loop(0, nc, step, 0)

    col = pl.BlockSpec((L, HD), lambda h: (0, h))
    return pl.pallas_call(
        body, name=name, grid=(HEADS,), out_shape=jax.ShapeDtypeStruct((L, W), F32),
        in_specs=[col, col, col, pl.BlockSpec((None, 8, HD), lambda h: (h, 0, 0))], out_specs=col,
        scratch_shapes=[pltpu.VMEM((HD, HD), F32)], compiler_params=_params(("parallel",)),
    )(q, k, v, lg)


ATT_SCALE = (HD + ROPE) ** -0.5
NEG = -1e30


ATT_TILE = 384


def _att_valid(T, row0, col0):
    r = lax.broadcasted_iota(jnp.int32, (T, T), 0) + row0
    c = lax.broadcasted_iota(jnp.int32, (T, T), 1) + col0
    return (c <= r) & ((c < N_META) | (c >= BLK))


def _attn_fwd(qm, kn, krr, vm, comm=None):
    L = qm.shape[0]
    W = HEADS * HD
    T = _tile(L, ATT_TILE, BLK)
    nb = L // T
    n_cm = comm.n if comm is not None else 0

    def body(*refs):
        q_ref, kn_ref, kr_ref, v_ref = refs[:4]
        o_ref, lse_ref = refs[4 + n_cm:6 + n_cm]
        m_sc, l_sc, acc_sc = refs[6 + 2 * n_cm:9 + 2 * n_cm]
        if comm is not None:
            cm_refs = (refs[4:4 + n_cm], refs[6 + n_cm:6 + 2 * n_cm], refs[9 + 2 * n_cm:])
            first, last = _grid_edges((HEADS, nb))

            @pl.when(first)
            def _():
                comm.start(*cm_refs)

        i = pl.program_id(1)
        q = q_ref[...]
        m_sc[...] = jnp.full_like(m_sc, NEG)
        l_sc[...] = jnp.zeros_like(l_sc)
        acc_sc[...] = jnp.zeros_like(acc_sc)

        def tile(j, masked):
            rows = pl.ds(pl.multiple_of(j * T, T), T)
            k = jnp.concatenate([kn_ref[rows, :], kr_ref[rows, :]], axis=1)
            s = _dot(q, k, 'nt') * ATT_SCALE
            if masked:
                s = jnp.where(_att_valid(T, i * T, j * T), s, NEG)
            m_prev = m_sc[...]
            m_new = jnp.maximum(m_prev, jnp.max(s, axis=-1, keepdims=True))
            p = jnp.exp(s - m_new)
            alpha = jnp.exp(m_prev - m_new)
            l_sc[...] = alpha * l_sc[...] + jnp.sum(p, axis=-1, keepdims=True)
            acc_sc[...] = alpha * acc_sc[...] + _dot(p.astype(BF16), v_ref[rows, :], 'nn')
            m_sc[...] = m_new

        tile(0, True)

        def mid(j, carry):
            tile(j, False)
            return carry

        lax.fori_loop(1, i, mid, 0)

        @pl.when(i > 0)
        def _():
            tile(i, True)

        l = l_sc[...]
        o_ref[...] = (acc_sc[...] / l).astype(o_ref.dtype)
        lse_ref[...] = jnp.broadcast_to(m_sc[...] + jnp.log(l), (T, HD))

        if comm is not None:
            @pl.when(last)
            def _():
                comm.finish(*cm_refs)

    cm_specs = comm.specs if comm is not None else []
    res = pl.pallas_call(
        body, name="attn_fwd", grid=(HEADS, nb),
        out_shape=[jax.ShapeDtypeStruct((L, W), BF16), jax.ShapeDtypeStruct((HEADS, L, HD), F32)]
        + (comm.out_shapes if comm is not None else []),
        in_specs=[pl.BlockSpec((T, QH), lambda h, i: (i, h)), pl.BlockSpec((L, HD), lambda h, i: (0, h)),
                  pl.BlockSpec((L, HD), lambda h, i: (0, 0)), pl.BlockSpec((L, HD), lambda h, i: (0, h))] + cm_specs,
        out_specs=[pl.BlockSpec((T, HD), lambda h, i: (i, h)),
                   pl.BlockSpec((None, T, HD), lambda h, i: (h, i, 0))] + cm_specs,
        scratch_shapes=[pltpu.VMEM((T, 1), F32), pltpu.VMEM((T, 1), F32), pltpu.VMEM((T, HD), F32)]
        + (comm.scratch if comm is not None else []),
        compiler_params=_params(("arbitrary", "arbitrary")),
    )(qm, kn, krr, vm, *(comm.arrays if comm is not None else []))
    return res[:2], res[2:]


def _attn_bwd(qm, kn, krr, vm, o, dcat, lse, comm=None):
    L = qm.shape[0]
    W = HEADS * HD
    T = _tile(L, ATT_TILE, BLK)
    nb = L // T
    n_cm = comm.n if comm is not None else 0

    def body(*refs):
        q_ref, kn_ref, kr_ref, v_ref, o_ref, do_ref, lse_ref = refs[:7]
        dq_ref, dkn_ref, dkr_ref, dv_ref = refs[7 + n_cm:11 + n_cm]
        dl_sc, dk_sc, dv_sc = refs[11 + 2 * n_cm:14 + 2 * n_cm]
        if comm is not None:
            cm_refs = (refs[7:7 + n_cm], refs[11 + n_cm:11 + 2 * n_cm], refs[14 + 2 * n_cm:])
            first, last = _grid_edges((HEADS, nb))

            @pl.when(first)
            def _():
                comm.start(*cm_refs)

        j = pl.program_id(1)

        @pl.when(j == 0)
        def _():
            dq_ref[...] = jnp.zeros_like(dq_ref)

            def rowsum(t, carry):
                rows = pl.ds(pl.multiple_of(t * T, T), T)
                dl_sc[rows, :] = jnp.sum(do_ref[rows, :].astype(F32) * o_ref[rows, :].astype(F32), axis=-1,
                                         keepdims=True)
                return carry

            lax.fori_loop(0, nb, rowsum, 0)

        k = jnp.concatenate([kn_ref[...], kr_ref[...]], axis=1)
        v = v_ref[...]
        dk_sc[...] = jnp.zeros_like(dk_sc)
        dv_sc[...] = jnp.zeros_like(dv_sc)

        def tile(i, masked):
            rows = pl.ds(pl.multiple_of(i * T, T), T)
            q = q_ref[rows, :]
            do = do_ref[rows, :]
            s = _dot(q, k, 'nt') * ATT_SCALE
            if masked:
                s = jnp.where(_att_valid(T, i * T, j * T), s, NEG)
            p = jnp.exp(s - lse_ref[rows, 0:1])
            dv_sc[...] += _dot(p.astype(BF16), do, 'tn')
            ds = (p * (_dot(do, v, 'nt') - dl_sc[rows, :]) * ATT_SCALE).astype(BF16)
            dk_sc[...] += _dot(ds, q, 'tn')
            dq_ref[rows, :] += _dot(ds, k, 'nn')

        tile(j, True)

        def rest(masked):
            def step(i, carry):
                tile(i, masked)
                return carry
            lax.fori_loop(j + 1, nb, step, 0)

        @pl.when(j == 0)
        def _():
            rest(True)

        @pl.when(j > 0)
        def _():
            rest(False)

        dk = dk_sc[...]
        dkn_ref[...] = dk[:, 0:HD].astype(BF16)
        dkr_ref[...] = dk[:, HD:QH]
        dv_ref[...] = dv_sc[...].astype(BF16)

        if comm is not None:
            @pl.when(last)
            def _():
                comm.finish(*cm_refs)

    blk = pl.BlockSpec((T, HD), lambda h, j: (j, h))
    cm_specs = comm.specs if comm is not None else []
    res = pl.pallas_call(
        body, name="attn_bwd", grid=(HEADS, nb),
        out_shape=[jax.ShapeDtypeStruct((L, HEADS * QH), F32), jax.ShapeDtypeStruct((L, W), BF16),
                   jax.ShapeDtypeStruct((L, W), F32), jax.ShapeDtypeStruct((L, W), BF16)]
        + (comm.out_shapes if comm is not None else []),
        in_specs=[pl.BlockSpec((L, QH), lambda h, j: (0, h)), blk, pl.BlockSpec((T, HD), lambda h, j: (j, 0)), blk,
                  pl.BlockSpec((L, HD), lambda h, j: (0, h)), pl.BlockSpec((L, HD), lambda h, j: (0, HEADS + h)),
                  pl.BlockSpec((None, L, HD), lambda h, j: (h, 0, 0))] + cm_specs,
        out_specs=[pl.BlockSpec((L, QH), lambda h, j: (0, h)), blk, blk, blk] + cm_specs,
        scratch_shapes=[pltpu.VMEM((L, 1), F32), pltpu.VMEM((T, QH), F32), pltpu.VMEM((T, HD), F32)]
        + (comm.scratch if comm is not None else []),
        compiler_params=_params(("arbitrary", "arbitrary")),
    )(qm, kn, krr, vm, o, dcat, lse, *(comm.arrays if comm is not None else []))
    return res[:4], res[4:]


def _unrope_q(dqm, tabs_m):
    L, W = dqm.shape
    tr = _tile(L, 384)

    def body(d_ref, cm_ref, sa_ref, sb_ref, out_ref):
        cm, sa, sb = cm_ref[...], sa_ref[...], sb_ref[...]
        for h in range(HEADS):
            out_ref[:, h * QH:h * QH + HD] = d_ref[:, h * QH:h * QH + HD].astype(BF16)
            out_ref[:, h * QH + HD:(h + 1) * QH] = _rope_mla_t(d_ref[:, h * QH + HD:(h + 1) * QH], cm, sa,
                                                               sb).astype(BF16)

    row = pl.BlockSpec((tr, W), lambda i: (i, 0))
    tab = pl.BlockSpec((tr, HD), lambda i: (i, 0))
    return pl.pallas_call(
        body, name="unrope_q", grid=(L // tr,), out_shape=jax.ShapeDtypeStruct((L, W), BF16),
        in_specs=[row, tab, tab, tab], out_specs=row, compiler_params=_params(("parallel",)),
    )(dqm, *tabs_m)


def _q_up(cqn, wuq_p, tabs_m):
    L = cqn.shape[0]
    tm = _tile(L, 704)

    def ep(acc, cm, sa, sb):
        parts = []
        for h in range(HEADS):
            parts.append(acc[:, h * QH:h * QH + HD])
            parts.append(_rope_mla(acc[:, h * QH + HD:(h + 1) * QH], cm, sa, sb))
        return (jnp.concatenate(parts, axis=1),)

    tab = pl.BlockSpec((tm, HD), lambda i, j: (i, 0))
    return _mm("mla_q_up", (L // tm, 1), ("parallel", "parallel"), None,
               [cqn, wuq_p], [pl.BlockSpec((tm, Q_RANK), lambda i, j: (i, 0)),
                              pl.BlockSpec((Q_RANK, HEADS * QH), lambda i, j: (0, 0))],
               [(0, 1, 'nn', 0)], [(tm, HEADS * QH)], list(tabs_m), [tab] * 3, ep,
               [jax.ShapeDtypeStruct((L, HEADS * QH), BF16)], [pl.BlockSpec((tm, HEADS * QH), lambda i, j: (i, 0))])[0]


def _mix_out(cat, w_out, h_in, post):
    L, K = cat.shape
    D = w_out.shape[1]
    tm, tk = _tile(L, 384), _tile(K, 512, 128)
    row = pl.BlockSpec((tm, D), lambda i, k: (i, 0))
    return _mm("mix_out", (L // tm, K // tk), ("parallel", "arbitrary"), 1,
               [cat, w_out], [pl.BlockSpec((tm, tk), lambda i, k: (i, k)), pl.BlockSpec((tk, D), lambda i, k: (k, 0))],
               [(0, 1, 'nn', 0)], [(tm, D)], [h_in, post], [row, pl.BlockSpec((1, D), lambda i, k: (0, 0))],
               _resnorm_epilogue(1.0), [jax.ShapeDtypeStruct((L, D), F32)] * 2, [row, row])


def _adam_math(w, g, m, v):
    m = ADAM_B1 * m + (1.0 - ADAM_B1) * g
    v = ADAM_B2 * v + (1.0 - ADAM_B2) * (g * g)
    m_hat = m / (1.0 - ADAM_B1 ** ADAM_STEP)
    v_hat = v / (1.0 - ADAM_B2 ** ADAM_STEP)
    delta = -ADAM_LR * (m_hat / (jnp.sqrt(v_hat) + ADAM_EPS) + ADAM_WD * w)
    return delta, m, v


def _adam(name, w, m, v, g_slots=None, g=None):
    R, C = w.shape
    tr = _tile(R, 256, 8)
    from_slots = g_slots is not None

    def body(w_ref, m_ref, v_ref, g_ref, go_ref, d_ref, mo_ref, vo_ref):
        if from_slots:
            grad = g_ref[0].astype(F32)
            for s in range(1, N_DEV):
                grad = grad + g_ref[s].astype(F32)
        else:
            grad = g_ref[...]
        delta, mn, vn = _adam_math(w_ref[...], grad, m_ref[...], v_ref[...])
        go_ref[...] = grad
        d_ref[...] = delta
        mo_ref[...] = mn
        vo_ref[...] = vn

    row = pl.BlockSpec((tr, C), lambda i: (i, 0))
    gspec = pl.BlockSpec((N_DEV, tr, C), lambda i: (0, i, 0)) if from_slots else row
    return pl.pallas_call(
        body, name=name, grid=(R // tr,), out_shape=[jax.ShapeDtypeStruct((R, C), F32)] * 4,
        in_specs=[row, row, row, gspec], out_specs=[row] * 4, compiler_params=_params(("parallel",)),
    )(w, m, v, g_slots if from_slots else g)


def _unblock(gathered):
    n, r, c = gathered.shape
    return jnp.transpose(gathered, (1, 0, 2)).reshape(r, n * c)


def _reblock(full, c):
    r = full.shape[0]
    return jnp.transpose(full[:, :N_DEV * c].reshape(r, N_DEV, c), (1, 0, 2))


def _step(x, target, w, mom, vel):
    S, D = x.shape[1], x.shape[2]
    L = S + BLK
    sq = lambda a: a.reshape(a.shape[1:]) if a.ndim == 3 else a
    p = {n: sq(w[n]) for n in WEIGHTS if n != 'meta_tokens'}
    gather = lambda names: _Exchange([p[n].astype(BF16) for n in names], False)
    scatter = lambda blocks: _Exchange(blocks, True)
    in_s, uq_s = p['w_in'].shape[1], p['mla_w_uq'].shape[1]
    tabs = _rope_tables(L)
    tabs_m = tabs[2:]
    lg = jnp.broadcast_to(jnp.log(1.0 - 2.0 ** (-5.0 - jnp.arange(HEADS, dtype=F32)))[:, None, None], (HEADS, 8, HD))
    R = {}

    wg1, wu1, meta = _exchange("gather_first", [p['ffn1_w_gate'].astype(BF16), p['ffn1_w_up'].astype(BF16),
                                                w['meta_tokens']], False)
    h0 = jnp.concatenate([_unblock(meta), jnp.zeros((BLK - N_META, D), F32), x[0]], axis=0)
    a1 = _norm_fwd(h0, p['ffn1_pre_norm'])
    (g1, u1, hid1), (wd1,) = _ffn_up(a1, wg1, wu1, comm=gather(['ffn1_w_down']))
    (f1, h1), (w_in_g,) = _ffn_down(hid1, wd1, h0, p['ffn1_post_norm'], comm=gather(['w_in']))

    w_in_full = _unblock(w_in_g)
    w_in = jnp.pad(w_in_full, ((0, 0), (0, D_INP - w_in_full.shape[1])))
    um = _norm_fwd(h1, p['mix_pre_norm'])
    proj, (uq_g, uk_g, uv_g, wout_g) = _mm_nn("mix_in", um, w_in, F32,
                                              comm=gather(['mla_w_uq', 'mla_w_uk', 'mla_w_uv', 'w_out']))
    wuq = jnp.pad(_unblock(uq_g).reshape(Q_RANK, HEADS, HD + ROPE),
                  ((0, 0), (0, 0), (0, QH - HD - ROPE))).reshape(Q_RANK, HEADS * QH)
    wuk, wuv, w_out = _unblock(uk_g), _unblock(uv_g), wout_g.reshape(-1, D)
    qr, kr, vr, cqn, ckvn, krr = _prep(proj, tabs, p['mla_q_norm'], p['mla_kv_norm'])
    qm = _q_up(cqn, wuq, tabs_m)
    kn = _mm_nn("mla_k_up", ckvn, wuk, BF16)
    vm = _mm_nn("mla_v_up", ckvn, wuv, BF16)
    (o_mla, lse), (wg2, wu2) = _attn_fwd(qm, kn, krr, vm, comm=gather(['ffn2_w_gate', 'ffn2_w_up']))
    o_ret = _lin_attn("ret_fwd", qr, kr, vr, lg, False)
    ret = _post(o_ret, proj, p['ret_group_norm'])
    cat = jnp.concatenate([ret, o_mla], axis=1)
    m, h2 = _mix_out(cat, w_out, h1, p['mix_post_norm'])

    a2 = _norm_fwd(h2, p['ffn2_pre_norm'])
    (g2, u2, hid2), (wd2,) = _ffn_up(a2, wg2, wu2, comm=gather(['ffn2_w_down']))
    f2, h3 = _ffn_down(hid2, wd2, h2, p['ffn2_post_norm'])
    dh3, loss_blk = _loss(h3, target[0])

    dsmall = {}
    df2, dsmall['ffn2_post_norm'] = _norm_bwd(f2, p['ffn2_post_norm'], dh3, None, 0.5, BF16)
    dg2, du2 = _ffn_dhid(df2, wd2, g2, u2)
    dwd2 = _ffn_dwd(hid2, df2)
    (dwg2, dwu2), (R['ffn2_w_down'],) = _ffn_dwgu(a2, dg2, du2, comm=scatter([dwd2]))
    da2, (R['ffn2_w_gate'],) = _ffn_da(dg2, du2, wg2, wu2, comm=scatter([dwg2]))
    dh2, dsmall['ffn2_pre_norm'] = _norm_bwd(h2, p['ffn2_pre_norm'], da2, dh3, 1.0, F32)

    dm, dsmall['mix_post_norm'] = _norm_bwd(m, p['mix_post_norm'], dh2, None, 1.0, BF16)
    dcat = _mm_nt("mix_dcat", [(dm, w_out)], BF16)
    dwout = _mm_tn("mix_dwout", cat, [dm])[0]
    do_ret, drg, dsmall['ret_group_norm'] = _post_bwd(o_ret, proj, p['ret_group_norm'], dcat)
    dqr = _lin_attn("ret_dq", do_ret, vr, kr, lg, False)
    dkr = _lin_attn("ret_dk", vr, do_ret, qr, lg, True)
    dvr = _lin_attn("ret_dv", kr, qr, do_ret, lg, True)
    (dqm, dkn, dkr8, dvm), (R['ffn2_w_up'], R['w_out']) = _attn_bwd(
        qm, kn, krr, vm, o_mla, dcat, lse, comm=scatter([dwu2, dwout.reshape(N_DEV, -1, D)]))
    dqp = _unrope_q(dqm, tabs_m)
    dwuq = _mm_tn("mla_dwuq", cqn, [dqp])[0]
    dcqn = _mm_nt("mla_dcq", [(dqp, wuq)], F32)
    dwuk, dwuv = _mm_tn("mla_dwukv", ckvn, [dkn, dvm])
    dckvn = _mm_nt("mla_dckv", [(dkn, wuk), (dvm, wuv)], F32)
    dproj, dsmall['mla_q_norm'], dsmall['mla_kv_norm'] = _prep_bwd(
        proj, dqr, dkr, dvr, drg, dcqn, dckvn, dkr8, tabs, p['mla_q_norm'], p['mla_kv_norm'])
    dwuq_b = _reblock(dwuq.reshape(Q_RANK, HEADS, QH)[:, :, :HD + ROPE].reshape(Q_RANK, HEADS * (HD + ROPE)), uq_s)
    (dwin,), (R['mla_w_uq'], R['mla_w_uk'], R['mla_w_uv']) = _mm_tn(
        "mix_dwin", um, [dproj], comm=scatter([dwuq_b, _reblock(dwuk, p['mla_w_uk'].shape[1]),
                                               _reblock(dwuv, p['mla_w_uv'].shape[1])]))
    dum = _mm_nt("mix_du", [(dproj, w_in)], F32)
    dh1, dsmall['mix_pre_norm'] = _norm_bwd(h1, p['mix_pre_norm'], dum, dh2, 1.0, F32)

    df1, dsmall['ffn1_post_norm'] = _norm_bwd(f1, p['ffn1_post_norm'], dh1, None, 0.5, BF16)
    (dg1, du1), (R['w_in'],) = _ffn_dhid(df1, wd1, g1, u1, comm=scatter([_reblock(dwin, in_s)]))
    dwg1, dwu1 = _ffn_dwgu(a1, dg1, du1)
    dwd1, (R['ffn1_w_gate'],) = _ffn_dwd(hid1, df1, comm=scatter([dwg1]))
    da1, (R['ffn1_w_up'],) = _ffn_da(dg1, du1, wg1, wu1, comm=scatter([dwu1]))
    dh0, dsmall['ffn1_pre_norm'] = _norm_bwd(h0, p['ffn1_pre_norm'], da1, dh1, 1.0, F32)
    R['ffn1_w_down'], = _exchange("scatter_last", [dwd1], True)

    def slab(a):
        a = a.reshape(-1, 128)
        return jnp.pad(a, ((0, (-a.shape[0]) % 8), (0, 0)))

    slab_rows = lambda n: -(-(p[n].shape[-1] // 128) // 8) * 8
    packed = jnp.concatenate([slab(dsmall[n]) for n in SMALL] + [slab(dh0[:N_META]), loss_blk], axis=0)
    red = _allreduce_small(packed)
    offs = sum(slab_rows(n) for n in SMALL)
    n_small = offs
    gmeta_full = red[offs:offs + N_META * D // 128].reshape(N_META, D)
    offs += N_META * D // 128
    loss = red[offs, 0]

    grad, delta, new_m, new_v = {}, {}, {}, {}
    for n in BIG:
        outs = _adam("adam_" + n, p[n], sq(mom[n]), sq(vel[n]), g_slots=R[n])
        grad[n], delta[n], new_m[n], new_v[n] = [o.reshape(w[n].shape) for o in outs]
    pack = lambda d: jnp.concatenate([slab(d[n]) for n in SMALL], axis=0)
    outs = _adam("adam_small", pack(w), pack(mom), pack(vel), g=red[:n_small])
    offs = 0
    for n in SMALL:
        r = p[n].shape[-1] // 128
        grad[n], delta[n], new_m[n], new_v[n] = [o[offs:offs + r].reshape(w[n].shape) for o in outs]
        offs += slab_rows(n)
    dev = 4 * lax.axis_index("x") + 2 * lax.axis_index("y") + lax.axis_index("c")
    mcols = w['meta_tokens'].shape[1]
    gmeta = lax.dynamic_slice(gmeta_full, (0, dev * mcols), (N_META, mcols))
    outs = _adam("adam_meta", w['meta_tokens'], mom['meta_tokens'], vel['meta_tokens'], g=gmeta)
    grad['meta_tokens'], delta['meta_tokens'], new_m['meta_tokens'], new_v['meta_tokens'] = outs

    return (loss, dh0[BLK:][None], *[grad[n] for n in WEIGHTS], *[delta[n] for n in WEIGHTS],
            *[new_m[n] for n in WEIGHTS], *[new_v[n] for n in WEIGHTS])


def kernel(x, meta_tokens, ffn1_pre_norm, ffn1_w_gate, ffn1_w_up, ffn1_w_down, ffn1_post_norm, mix_pre_norm, w_in, ret_group_norm, mla_q_norm, mla_w_uq, mla_kv_norm, mla_w_uk, mla_w_uv, w_out, mix_post_norm, ffn2_pre_norm, ffn2_w_gate, ffn2_w_up, ffn2_w_down, ffn2_post_norm, loss_target, m_meta_tokens, m_ffn1_pre_norm, m_ffn1_w_gate, m_ffn1_w_up, m_ffn1_w_down, m_ffn1_post_norm, m_mix_pre_norm, m_w_in, m_ret_group_norm, m_mla_q_norm, m_mla_w_uq, m_mla_kv_norm, m_mla_w_uk, m_mla_w_uv, m_w_out, m_mix_post_norm, m_ffn2_pre_norm, m_ffn2_w_gate, m_ffn2_w_up, m_ffn2_w_down, m_ffn2_post_norm, v_meta_tokens, v_ffn1_pre_norm, v_ffn1_w_gate, v_ffn1_w_up, v_ffn1_w_down, v_ffn1_post_norm, v_mix_pre_norm, v_w_in, v_ret_group_norm, v_mla_q_norm, v_mla_w_uq, v_mla_kv_norm, v_mla_w_uk, v_mla_w_uv, v_w_out, v_mix_post_norm, v_ffn2_pre_norm, v_ffn2_w_gate, v_ffn2_w_up, v_ffn2_w_down, v_ffn2_post_norm):
    w = dict(zip(WEIGHTS, (meta_tokens, ffn1_pre_norm, ffn1_w_gate, ffn1_w_up, ffn1_w_down, ffn1_post_norm,
                           mix_pre_norm, w_in, ret_group_norm, mla_q_norm, mla_w_uq, mla_kv_norm, mla_w_uk, mla_w_uv,
                           w_out, mix_post_norm, ffn2_pre_norm, ffn2_w_gate, ffn2_w_up, ffn2_w_down, ffn2_post_norm)))
    mom = dict(zip(WEIGHTS, (m_meta_tokens, m_ffn1_pre_norm, m_ffn1_w_gate, m_ffn1_w_up, m_ffn1_w_down,
                             m_ffn1_post_norm, m_mix_pre_norm, m_w_in, m_ret_group_norm, m_mla_q_norm, m_mla_w_uq,
                             m_mla_kv_norm, m_mla_w_uk, m_mla_w_uv, m_w_out, m_mix_post_norm, m_ffn2_pre_norm,
                             m_ffn2_w_gate, m_ffn2_w_up, m_ffn2_w_down, m_ffn2_post_norm)))
    vel = dict(zip(WEIGHTS, (v_meta_tokens, v_ffn1_pre_norm, v_ffn1_w_gate, v_ffn1_w_up, v_ffn1_w_down,
                             v_ffn1_post_norm, v_mix_pre_norm, v_w_in, v_ret_group_norm, v_mla_q_norm, v_mla_w_uq,
                             v_mla_kv_norm, v_mla_w_uk, v_mla_w_uv, v_w_out, v_mix_post_norm, v_ffn2_pre_norm,
                             v_ffn2_w_gate, v_ffn2_w_up, v_ffn2_w_down, v_ffn2_post_norm)))
    return _step(x, loss_target, w, mom, vel)
```

```python
import functools
import math

import jax
import jax.numpy as jnp
from jax import lax
from jax.experimental import pallas as pl
from jax.experimental.pallas import tpu as pltpu

N_DEV = 8
N_META = 16
BLK = 128
HEADS = 8
HD = 128
ROPE = 64
Q_RANK = 512
KV_RANK = 256
QH = 2 * HD
D_INP = 4 * HEADS * HD + Q_RANK + KV_RANK + BLK
ROPE_THETA = 10000.0
EPS = 1e-6
ADAM_LR = 0.001
ADAM_B1 = 0.9
ADAM_B2 = 0.999
ADAM_EPS = 1e-08
ADAM_WD = 0.01
ADAM_STEP = 10
V7X_VMEM_LIMIT = 48 * 1024 * 1024
MESH = pl.DeviceIdType.MESH
F32 = jnp.float32
BF16 = jnp.bfloat16

WEIGHTS = ['meta_tokens', 'ffn1_pre_norm', 'ffn1_w_gate', 'ffn1_w_up', 'ffn1_w_down', 'ffn1_post_norm',
           'mix_pre_norm', 'w_in', 'ret_group_norm', 'mla_q_norm', 'mla_w_uq', 'mla_kv_norm', 'mla_w_uk',
           'mla_w_uv', 'w_out', 'mix_post_norm', 'ffn2_pre_norm', 'ffn2_w_gate', 'ffn2_w_up', 'ffn2_w_down',
           'ffn2_post_norm']
SMALL = ['ffn1_pre_norm', 'ffn1_post_norm', 'mix_pre_norm', 'ret_group_norm', 'mla_q_norm', 'mla_kv_norm',
         'mix_post_norm', 'ffn2_pre_norm', 'ffn2_post_norm']
BIG = ['ffn1_w_gate', 'ffn1_w_up', 'ffn1_w_down', 'w_in', 'mla_w_uq', 'mla_w_uk', 'mla_w_uv', 'w_out',
       'ffn2_w_gate', 'ffn2_w_up', 'ffn2_w_down']

_DIMS = {'nn': (((1,), (0,)), ((), ())), 'nt': (((1,), (1,)), ((), ())), 'tn': (((0,), (0,)), ((), ()))}


def _tile(n, target, mult=16):
    best = None
    for t in range(mult, min(n, target) + 1, mult):
        if n % t == 0:
            best = t
    return best if best is not None else n


def _params(sem):
    return pltpu.CompilerParams(dimension_semantics=sem, vmem_limit_bytes=V7X_VMEM_LIMIT)


def _dot(a, b, dims):
    return lax.dot_general(a, b, _DIMS[dims], preferred_element_type=F32)


def _sigmoid(x):
    return 1.0 / (1.0 + jnp.exp(-x))


def _me_and_peers():
    x, y, c = lax.axis_index("x"), lax.axis_index("y"), lax.axis_index("c")

    def peer(j):
        px = 1 - x if (j >> 2) & 1 else x
        py = 1 - y if (j >> 1) & 1 else y
        pc = 1 - c if j & 1 else c
        return (px, py, pc), 4 * px + 2 * py + pc

    return 4 * x + 2 * y + c, peer


class _Exchange:
    def __init__(self, arrays, per_peer):
        self.arrays = list(arrays)
        self.per_peer = per_peer
        self.n = len(self.arrays)
        self.out_shapes = [jax.ShapeDtypeStruct((N_DEV,) + tuple(a.shape[1:] if per_peer else a.shape), a.dtype)
                           for a in self.arrays]
        self.specs = [pl.BlockSpec(memory_space=pl.ANY)] * self.n
        self.scratch = [pltpu.SemaphoreType.DMA((7 * self.n,)), pltpu.SemaphoreType.DMA((7 * self.n,)),
                        pltpu.SemaphoreType.DMA((self.n,))]

    def _copies(self, src, dst, sems):
        send_sems, recv_sems, local_sems = sems
        me, peer = _me_and_peers()
        sib, _ = peer(1)
        local, sends, recvs, passes = [], {}, {}, {}
        for k in range(self.n):
            own = src[k].at[me] if self.per_peer else src[k]
            local.append(pltpu.make_async_copy(own, dst[k].at[me], local_sems.at[k]))
            for j in range(1, N_DEV):
                pid, pidx = peer(j)
                out = src[k].at[pidx] if self.per_peer else src[k]
                sem = dict(send_sem=send_sems.at[k * 7 + j - 1], recv_sem=recv_sems.at[k * 7 + j - 1])
                recvs[k, j] = pltpu.make_async_remote_copy(src_ref=out, dst_ref=dst[k].at[pidx], device_id=pid,
                                                           device_id_type=MESH, **sem)
                if self.per_peer or j in (1, 2, 4, 6):
                    sends[k, j] = pltpu.make_async_remote_copy(src_ref=out, dst_ref=dst[k].at[me], device_id=pid,
                                                               device_id_type=MESH, **sem)
                else:
                    _, origin = peer(j ^ 1)
                    passes[k, j ^ 1] = pltpu.make_async_remote_copy(
                        src_ref=dst[k].at[origin], dst_ref=dst[k].at[origin], device_id=sib, device_id_type=MESH, **sem)
        return local, sends, recvs, passes

    def start(self, src, dst, sems):
        local, sends, _, _ = self._copies(src, dst, sems)
        for cp in local + list(sends.values()):
            cp.start()

    def finish(self, src, dst, sems):
        local, sends, recvs, passes = self._copies(src, dst, sems)
        for key, cp in passes.items():
            recvs[key].wait_recv()
            cp.start()
        for key, cp in recvs.items():
            if key not in passes:
                cp.wait_recv()
        for cp in list(sends.values()) + list(passes.values()):
            cp.wait_send()
        for cp in local:
            cp.wait()


def _grid_edges(grid):
    first, last = None, None
    for a, n in enumerate(grid):
        f, l = pl.program_id(a) == 0, pl.program_id(a) == n - 1
        first = f if first is None else first & f
        last = l if last is None else last & l
    return first, last


def _exchange(name, arrays, per_peer):
    ex = _Exchange(arrays, per_peer)
    n = ex.n

    def body(*refs):
        ex.start(refs[:n], refs[n:2 * n], refs[2 * n:])
        ex.finish(refs[:n], refs[n:2 * n], refs[2 * n:])

    return pl.pallas_call(body, name=name, out_shape=ex.out_shapes, in_specs=ex.specs, out_specs=ex.specs,
                          scratch_shapes=ex.scratch)(*arrays)


def _allreduce_small(v):
    rows = v.shape[0]

    def body(v_ref, out_ref, buf, send_sems, recv_sems):
        me, peer = _me_and_peers()
        buf[pl.ds(me, 1)] = v_ref[...][None]
        sends = []
        for j in range(1, N_DEV):
            pid, _ = peer(j)
            cp = pltpu.make_async_remote_copy(src_ref=v_ref, dst_ref=buf.at[me], send_sem=send_sems.at[j - 1],
                                              recv_sem=recv_sems.at[j - 1], device_id=pid, device_id_type=MESH)
            cp.start()
            sends.append(cp)
        for j in range(1, N_DEV):
            pid, pidx = peer(j)
            pltpu.make_async_remote_copy(src_ref=v_ref, dst_ref=buf.at[pidx], send_sem=send_sems.at[j - 1],
                                         recv_sem=recv_sems.at[j - 1], device_id=pid,
                                         device_id_type=MESH).wait_recv()
        for cp in sends:
            cp.wait_send()
        acc = buf[0]
        for s in range(1, N_DEV):
            acc = acc + buf[s]
        out_ref[...] = acc

    vm = pl.BlockSpec(memory_space=pltpu.VMEM)
    return pl.pallas_call(
        body, name="allreduce_small", out_shape=jax.ShapeDtypeStruct(v.shape, F32),
        in_specs=[vm], out_specs=vm,
        scratch_shapes=[pltpu.VMEM((N_DEV, rows, 128), F32), pltpu.SemaphoreType.DMA((7,)),
                        pltpu.SemaphoreType.DMA((7,))],
    )(v)


def _mm(name, grid, sem, k_axis, ops, op_specs, pairs, acc_shapes, extras, extra_specs, epilogue, outs, out_specs,
        comm=None):
    n_op, n_ex, n_out = len(ops), len(extras), len(outs)
    nk = grid[k_axis] if k_axis is not None else 1
    n_acc = len(acc_shapes) if nk > 1 else 0
    n_cm = comm.n if comm is not None else 0

    def body(*refs):
        op_refs = refs[:n_op]
        ex_refs = refs[n_op:n_op + n_ex]
        n_in = n_op + n_ex + n_cm
        out_refs = refs[n_in:n_in + n_out]
        acc_refs = refs[n_in + n_out + n_cm:n_in + n_out + n_cm + n_acc]
        if comm is not None:
            cm_refs = (refs[n_op + n_ex:n_in], refs[n_in + n_out:n_in + n_out + n_cm],
                       refs[n_in + n_out + n_cm + n_acc:])
            first, last = _grid_edges(grid)

            @pl.when(first)
            def _():
                comm.start(*cm_refs)

        def finish(vals):
            res = epilogue(*vals, *[e[...] for e in ex_refs])
            for o, r in zip(out_refs, res):
                o[...] = r.astype(o.dtype)

        if nk == 1:
            parts = [None] * len(acc_shapes)
            for li, ri, dims, ai in pairs:
                d = _dot(op_refs[li][...], op_refs[ri][...], dims)
                parts[ai] = d if parts[ai] is None else parts[ai] + d
            finish(parts)
        else:
            k = pl.program_id(k_axis)

            @pl.when(k == 0)
            def _():
                for a in acc_refs:
                    a[...] = jnp.zeros_like(a)

            for li, ri, dims, ai in pairs:
                acc_refs[ai][...] += _dot(op_refs[li][...], op_refs[ri][...], dims)

            @pl.when(k == nk - 1)
            def _():
                finish([a[...] for a in acc_refs])

        if comm is not None:
            @pl.when(last)
            def _():
                comm.finish(*cm_refs)

    scratch = [pltpu.VMEM(s, F32) for s in acc_shapes] if nk > 1 else []
    if comm is None:
        return pl.pallas_call(
            body, name=name, grid=grid, out_shape=outs,
            in_specs=list(op_specs) + list(extra_specs), out_specs=list(out_specs),
            scratch_shapes=scratch, compiler_params=_params(sem),
        )(*ops, *extras)
    res = pl.pallas_call(
        body, name=name, grid=grid, out_shape=list(outs) + comm.out_shapes,
        in_specs=list(op_specs) + list(extra_specs) + comm.specs, out_specs=list(out_specs) + comm.specs,
        scratch_shapes=scratch + comm.scratch, compiler_params=_params(("arbitrary",) * len(grid)),
    )(*ops, *extras, *comm.arrays)
    return res[:n_out], res[n_out:]


def _with_comm(res, comm, pick):
    if comm is None:
        return pick(res)
    return pick(res[0]), res[1]


def _mm_nn(name, a, w, out_dtype, tm_target=704, tn_target=1664, epilogue=None, extras=(), extra_specs=(), comm=None):
    L, K = a.shape
    N = w.shape[1]
    tm, tn = _tile(L, tm_target), _tile(N, tn_target, 128)
    ep = epilogue if epilogue is not None else (lambda acc: (acc,))
    res = _mm(name, (L // tm, N // tn), ("parallel", "parallel"), None,
              [a, w], [pl.BlockSpec((tm, K), lambda i, j: (i, 0)), pl.BlockSpec((K, tn), lambda i, j: (0, j))],
              [(0, 1, 'nn', 0)], [(tm, tn)], list(extras), list(extra_specs), ep,
              [jax.ShapeDtypeStruct((L, N), out_dtype)], [pl.BlockSpec((tm, tn), lambda i, j: (i, j))], comm=comm)
    return _with_comm(res, comm, lambda o: o[0])


def _mm_nt(name, pairs_aw, out_dtype, tm_target=704, tn_target=512):
    L = pairs_aw[0][0].shape[0]
    N = pairs_aw[0][1].shape[0]
    tm, tn = _tile(L, tm_target), _tile(N, tn_target, 128)
    ops, specs, pairs = [], [], []
    for t, (a, w) in enumerate(pairs_aw):
        K = a.shape[1]
        ops += [a, w]
        specs += [pl.BlockSpec((tm, K), lambda i, j: (i, 0)), pl.BlockSpec((tn, K), lambda i, j: (j, 0))]
        pairs.append((2 * t, 2 * t + 1, 'nt', 0))
    return _mm(name, (L // tm, N // tn), ("parallel", "parallel"), None, ops, specs, pairs, [(tm, tn)], [], [],
               lambda acc: (acc,), [jax.ShapeDtypeStruct((L, N), out_dtype)],
               [pl.BlockSpec((tm, tn), lambda i, j: (i, j))])[0]


def _mm_tn(name, a, bs, out_dtype=BF16, tk_target=704, tn_target=1664, comm=None):
    L, M = a.shape
    N = bs[0].shape[1]
    tk, tn = _tile(L, tk_target), _tile(N, tn_target, 128)
    nb = len(bs)
    ops = [a] + list(bs)
    specs = [pl.BlockSpec((tk, M), lambda j, k: (k, 0))] + [pl.BlockSpec((tk, tn), lambda j, k: (k, j))] * nb
    res = _mm(name, (N // tn, L // tk), ("parallel", "arbitrary"), 1, ops, specs,
              [(0, 1 + t, 'tn', t) for t in range(nb)], [(M, tn)] * nb, [], [], lambda *acc: acc,
              [jax.ShapeDtypeStruct((M, N), out_dtype)] * nb, [pl.BlockSpec((M, tn), lambda j, k: (0, j))] * nb,
              comm=comm)
    return _with_comm(res, comm, lambda o: o)


def _norm_fwd(x, w):
    L, D = x.shape
    tr = _tile(L, 512)

    def body(x_ref, w_ref, y_ref):
        v = x_ref[...]
        r = lax.rsqrt(jnp.mean(v * v, axis=-1, keepdims=True) + EPS)
        y_ref[...] = (v * r * w_ref[...]).astype(y_ref.dtype)

    return pl.pallas_call(
        body, name="norm_fwd", grid=(L // tr,), out_shape=jax.ShapeDtypeStruct((L, D), BF16),
        in_specs=[pl.BlockSpec((tr, D), lambda i: (i, 0)), pl.BlockSpec((1, D), lambda i: (0, 0))],
        out_specs=pl.BlockSpec((tr, D), lambda i: (i, 0)), compiler_params=_params(("parallel",)),
    )(x, w)


def _norm_bwd_math(x, w, dy):
    r = lax.rsqrt(jnp.mean(x * x, axis=-1, keepdims=True) + EPS)
    gy = dy * w
    dx = r * (gy - x * (r * r) * jnp.mean(gy * x, axis=-1, keepdims=True))
    dw = jnp.sum(dy * x * r, axis=0, keepdims=True)
    return dx, dw


def _norm_bwd(x, w, dy, res, scale, out_dtype):
    L, D = x.shape
    tr = _tile(L, 384)
    has_res = res is not None

    def body(*refs):
        x_ref, w_ref, dy_ref = refs[:3]
        res_ref = refs[3] if has_res else None
        dx_ref, dw_ref = refs[-2:]
        dx, dw = _norm_bwd_math(x_ref[...], w_ref[...], dy_ref[...].astype(F32))
        dx = scale * dx
        if has_res:
            dx = dx + res_ref[...]
        dx_ref[...] = dx.astype(dx_ref.dtype)

        @pl.when(pl.program_id(0) == 0)
        def _():
            dw_ref[...] = jnp.zeros_like(dw_ref)

        dw_ref[...] += scale * dw

    row = pl.BlockSpec((tr, D), lambda i: (i, 0))
    vec = pl.BlockSpec((1, D), lambda i: (0, 0))
    return pl.pallas_call(
        body, name="norm_bwd", grid=(L // tr,),
        out_shape=[jax.ShapeDtypeStruct((L, D), out_dtype), jax.ShapeDtypeStruct((1, D), F32)],
        in_specs=[row, vec, row] + ([row] if has_res else []), out_specs=[row, vec],
        compiler_params=_params(("arbitrary",)),
    )(*([x, w, dy] + ([res] if has_res else [])))


def _loss(h, target):
    L, D = h.shape

    def body(h_ref, t_ref, dh_ref, loss_ref):
        i = pl.program_id(0)

        @pl.when(i == 0)
        def _():
            dh_ref[...] = jnp.zeros_like(dh_ref)
            loss_ref[...] = jnp.zeros_like(loss_ref)

        @pl.when(i > 0)
        def _():
            diff = h_ref[...] - t_ref[...]
            dh_ref[...] = diff * (1.0 / D)
            loss_ref[...] += 0.5 * jnp.sum(diff * diff) * (1.0 / D)

    return pl.pallas_call(
        body, name="loss", grid=(L // BLK,),
        out_shape=[jax.ShapeDtypeStruct((L, D), F32), jax.ShapeDtypeStruct((8, 128), F32)],
        in_specs=[pl.BlockSpec((BLK, D), lambda i: (i, 0)),
                  pl.BlockSpec((BLK, D), lambda i: (jnp.maximum(i - 1, 0), 0))],
        out_specs=[pl.BlockSpec((BLK, D), lambda i: (i, 0)), pl.BlockSpec((8, 128), lambda i: (0, 0))],
        compiler_params=_params(("arbitrary",)),
    )(h, target)


def _ffn_up(a, wg, wu, comm=None):
    L, D = a.shape
    F = wg.shape[2]
    tm = _tile(L, 704)

    def ep(g, u):
        return g, u, g * _sigmoid(g) * u

    hspec = pl.BlockSpec((None, tm, F), lambda i, j: (j, i, 0))
    wspec = pl.BlockSpec((None, D, F), lambda i, j: (j, 0, 0))
    res = _mm("ffn_up", (L // tm, N_DEV), ("parallel", "parallel"), None,
              [a, wg, wu], [pl.BlockSpec((tm, D), lambda i, j: (i, 0)), wspec, wspec],
              [(0, 1, 'nn', 0), (0, 2, 'nn', 1)], [(tm, F)] * 2, [], [], ep,
              [jax.ShapeDtypeStruct((N_DEV, L, F), BF16)] * 3, [hspec] * 3, comm=comm)
    return _with_comm(res, comm, lambda o: o)


def _resnorm_epilogue(scale):
    def ep(acc, h, w):
        r = lax.rsqrt(jnp.mean(acc * acc, axis=-1, keepdims=True) + EPS)
        return acc, h + scale * (acc * r * w)
    return ep


def _ffn_down(hid, wd, h_in, post, comm=None):
    _, L, F = hid.shape
    D = wd.shape[2]
    tm = _tile(L, 384)
    row = pl.BlockSpec((tm, D), lambda i, j: (i, 0))
    res = _mm("ffn_down", (L // tm, N_DEV), ("parallel", "arbitrary"), 1,
              [hid, wd], [pl.BlockSpec((None, tm, F), lambda i, j: (j, i, 0)),
                          pl.BlockSpec((None, F, D), lambda i, j: (j, 0, 0))],
              [(0, 1, 'nn', 0)], [(tm, D)], [h_in, post], [row, pl.BlockSpec((1, D), lambda i, j: (0, 0))],
              _resnorm_epilogue(0.5), [jax.ShapeDtypeStruct((L, D), F32)] * 2, [row, row], comm=comm)
    return _with_comm(res, comm, lambda o: o)


def _ffn_dhid(df, wd, g, u, comm=None):
    L, D = df.shape
    F = wd.shape[1]
    tm = _tile(L, 704)

    def ep(dhid, g_, u_):
        g32, u32 = g_.astype(F32), u_.astype(F32)
        sg = _sigmoid(g32)
        return dhid * u32 * sg * (1.0 + g32 * (1.0 - sg)), dhid * g32 * sg

    hspec = pl.BlockSpec((None, tm, F), lambda i, j: (j, i, 0))
    res = _mm("ffn_dhid", (L // tm, N_DEV), ("parallel", "parallel"), None,
              [df, wd], [pl.BlockSpec((tm, D), lambda i, j: (i, 0)),
                         pl.BlockSpec((None, F, D), lambda i, j: (j, 0, 0))],
              [(0, 1, 'nt', 0)], [(tm, F)], [g, u], [hspec, hspec], ep,
              [jax.ShapeDtypeStruct((N_DEV, L, F), BF16)] * 2, [hspec, hspec], comm=comm)
    return _with_comm(res, comm, lambda o: o)


def _ffn_dwd(hid, df, comm=None):
    _, L, F = hid.shape
    D = df.shape[1]
    tk = _tile(L, 1408)
    res = _mm("ffn_dwd", (N_DEV, L // tk), ("parallel", "arbitrary"), 1,
              [hid, df], [pl.BlockSpec((None, tk, F), lambda j, k: (j, k, 0)),
                          pl.BlockSpec((tk, D), lambda j, k: (k, 0))],
              [(0, 1, 'tn', 0)], [(F, D)], [], [], lambda acc: (acc,),
              [jax.ShapeDtypeStruct((N_DEV, F, D), BF16)], [pl.BlockSpec((None, F, D), lambda j, k: (j, 0, 0))],
              comm=comm)
    return _with_comm(res, comm, lambda o: o[0])


def _ffn_dwgu(a, dg, du, comm=None):
    L, D = a.shape
    F = dg.shape[2]
    tk = _tile(L, 704)
    hspec = pl.BlockSpec((None, tk, F), lambda j, k: (j, k, 0))
    wspec = pl.BlockSpec((None, D, F), lambda j, k: (j, 0, 0))
    res = _mm("ffn_dwgu", (N_DEV, L // tk), ("parallel", "arbitrary"), 1,
              [a, dg, du], [pl.BlockSpec((tk, D), lambda j, k: (k, 0)), hspec, hspec],
              [(0, 1, 'tn', 0), (0, 2, 'tn', 1)], [(D, F)] * 2, [], [], lambda *acc: acc,
              [jax.ShapeDtypeStruct((N_DEV, D, F), BF16)] * 2, [wspec, wspec], comm=comm)
    return _with_comm(res, comm, lambda o: o)


def _ffn_da(dg, du, wg, wu, comm=None):
    _, L, F = dg.shape
    D = wg.shape[1]
    tm = _tile(L, 384)
    hspec = pl.BlockSpec((None, tm, F), lambda i, j: (j, i, 0))
    wspec = pl.BlockSpec((None, D, F), lambda i, j: (j, 0, 0))
    row = pl.BlockSpec((tm, D), lambda i, j: (i, 0))
    res = _mm("ffn_da", (L // tm, N_DEV), ("parallel", "arbitrary"), 1,
              [dg, du, wg, wu], [hspec, hspec, wspec, wspec],
              [(0, 2, 'nt', 0), (1, 3, 'nt', 0)], [(tm, D)], [], [], lambda acc: (acc,),
              [jax.ShapeDtypeStruct((L, D), F32)], [row], comm=comm)
    return _with_comm(res, comm, lambda o: o[0])


def _rope_tables(L):
    rows = jnp.arange(L, dtype=F32)
    pos = jnp.where(rows < BLK, rows, rows - (BLK - N_META))
    inv_r = ROPE_THETA ** (-jnp.arange(0, HD, 2, dtype=F32) / HD)
    ang_r = pos[:, None] * inv_r[None, :]
    cr = jnp.concatenate([jnp.cos(ang_r), jnp.cos(ang_r)], axis=1)
    sr = jnp.concatenate([-jnp.sin(ang_r), jnp.sin(ang_r)], axis=1)
    inv_m = ROPE_THETA ** (-jnp.arange(0, ROPE, 2, dtype=F32) / ROPE)
    ang_m = pos[:, None] * inv_m[None, :]
    z32 = jnp.zeros((L, ROPE // 2), F32)
    z64 = jnp.zeros((L, HD - ROPE), F32)
    cm = jnp.concatenate([jnp.cos(ang_m), jnp.cos(ang_m), z64], axis=1)
    sa = jnp.concatenate([-jnp.sin(ang_m), z32, z64], axis=1)
    sb = jnp.concatenate([z32, jnp.sin(ang_m), z64], axis=1)
    return cr, sr, cm, sa, sb


def _rope_ret(x, cr, sr):
    return x * cr + pltpu.roll(x, HD // 2, 1) * sr


def _rope_ret_t(d, cr, sr):
    return d * cr + pltpu.roll(d * sr, HD // 2, 1)


def _rope_mla(x, cm, sa, sb):
    return x * cm + pltpu.roll(x, HD - ROPE // 2, 1) * sa + pltpu.roll(x, ROPE // 2, 1) * sb


def _rope_mla_t(d, cm, sa, sb):
    return d * cm + pltpu.roll(d * sa, ROPE // 2, 1) + pltpu.roll(d * sb, HD - ROPE // 2, 1)


C_RQ, C_RK, C_RV, C_RG = 0, HEADS * HD, 2 * HEADS * HD, 3 * HEADS * HD
C_CQ = 4 * HEADS * HD
C_CKV = C_CQ + Q_RANK
C_KR = C_CKV + KV_RANK
RET_K_SCALE = HD ** -0.5


def _prep(proj, tabs, qn, kvn):
    L = proj.shape[0]
    tr = _tile(L, 256)
    W = HEADS * HD

    def body(p_ref, cr_ref, sr_ref, cm_ref, sa_ref, sb_ref, qn_ref, kvn_ref, q_ref, k_ref, v_ref, cq_ref, ckv_ref,
             kr_ref):
        cr, sr = cr_ref[...], sr_ref[...]
        for h in range(HEADS):
            sl = slice(h * HD, (h + 1) * HD)
            q_ref[:, sl] = _rope_ret(p_ref[:, C_RQ + h * HD:C_RQ + (h + 1) * HD], cr, sr).astype(BF16)
            k_ref[:, sl] = (_rope_ret(p_ref[:, C_RK + h * HD:C_RK + (h + 1) * HD], cr, sr)
                            * RET_K_SCALE).astype(BF16)
        v_ref[...] = p_ref[:, C_RV:C_RV + W].astype(BF16)
        cq = p_ref[:, C_CQ:C_CQ + Q_RANK]
        cq_ref[...] = (cq * lax.rsqrt(jnp.mean(cq * cq, axis=-1, keepdims=True) + EPS) * qn_ref[...]).astype(BF16)
        ckv = p_ref[:, C_CKV:C_CKV + KV_RANK]
        ckv_ref[...] = (ckv * lax.rsqrt(jnp.mean(ckv * ckv, axis=-1, keepdims=True) + EPS)
                        * kvn_ref[...]).astype(BF16)
        kr_ref[...] = _rope_mla(p_ref[:, C_KR:C_KR + HD], cm_ref[...], sa_ref[...], sb_ref[...]).astype(BF16)

    row = lambda w: pl.BlockSpec((tr, w), lambda i: (i, 0))
    vec = lambda w: pl.BlockSpec((1, w), lambda i: (0, 0))
    return pl.pallas_call(
        body, name="mix_prep", grid=(L // tr,),
        out_shape=[jax.ShapeDtypeStruct((L, W), BF16)] * 3 + [jax.ShapeDtypeStruct((L, Q_RANK), BF16),
                                                              jax.ShapeDtypeStruct((L, KV_RANK), BF16),
                                                              jax.ShapeDtypeStruct((L, HD), BF16)],
        in_specs=[row(D_INP)] + [row(HD)] * 5 + [vec(Q_RANK), vec(KV_RANK)],
        out_specs=[row(W)] * 3 + [row(Q_RANK), row(KV_RANK), row(HD)],
        compiler_params=_params(("parallel",)),
    )(proj, *tabs, qn, kvn)


def _prep_bwd(proj, dq, dk, dv, drg, dcqn, dckvn, dkr8, tabs, qn, kvn):
    L = proj.shape[0]
    tr = _tile(L, 192)
    W = HEADS * HD

    def body(p_ref, dq_ref, dk_ref, dv_ref, drg_ref, dcq_ref, dckv_ref, dkr_ref, cr_ref, sr_ref, cm_ref, sa_ref,
             sb_ref, qn_ref, kvn_ref, dp_ref, dqn_ref, dkvn_ref):
        cr, sr = cr_ref[...], sr_ref[...]
        dkr = None
        for h in range(HEADS):
            sl = slice(h * HD, (h + 1) * HD)
            dp_ref[:, C_RQ + h * HD:C_RQ + (h + 1) * HD] = _rope_ret_t(dq_ref[:, sl], cr, sr).astype(BF16)
            dp_ref[:, C_RK + h * HD:C_RK + (h + 1) * HD] = (_rope_ret_t(dk_ref[:, sl], cr, sr)
                                                            * RET_K_SCALE).astype(BF16)
            part = dkr_ref[:, sl]
            dkr = part if dkr is None else dkr + part
        dp_ref[:, C_RV:C_RV + W] = dv_ref[...].astype(BF16)
        dp_ref[:, C_RG:C_RG + W] = drg_ref[...].astype(BF16)
        dcq, dqn = _norm_bwd_math(p_ref[:, C_CQ:C_CQ + Q_RANK], qn_ref[...], dcq_ref[...])
        dp_ref[:, C_CQ:C_CQ + Q_RANK] = dcq.astype(BF16)
        dckv, dkvn = _norm_bwd_math(p_ref[:, C_CKV:C_CKV + KV_RANK], kvn_ref[...], dckv_ref[...])
        dp_ref[:, C_CKV:C_CKV + KV_RANK] = dckv.astype(BF16)
        dp_ref[:, C_KR:C_KR + HD] = _rope_mla_t(dkr, cm_ref[...], sa_ref[...], sb_ref[...]).astype(BF16)

        @pl.when(pl.program_id(0) == 0)
        def _():
            dqn_ref[...] = jnp.zeros_like(dqn_ref)
            dkvn_ref[...] = jnp.zeros_like(dkvn_ref)

        dqn_ref[...] += dqn
        dkvn_ref[...] += dkvn

    row = lambda w: pl.BlockSpec((tr, w), lambda i: (i, 0))
    vec = lambda w: pl.BlockSpec((1, w), lambda i: (0, 0))
    return pl.pallas_call(
        body, name="mix_prep_bwd", grid=(L // tr,),
        out_shape=[jax.ShapeDtypeStruct((L, D_INP), BF16), jax.ShapeDtypeStruct((1, Q_RANK), F32),
                   jax.ShapeDtypeStruct((1, KV_RANK), F32)],
        in_specs=[row(D_INP)] + [row(W)] * 4 + [row(Q_RANK), row(KV_RANK), row(W)] + [row(HD)] * 5
                 + [vec(Q_RANK), vec(KV_RANK)],
        out_specs=[row(D_INP), vec(Q_RANK), vec(KV_RANK)],
        compiler_params=_params(("arbitrary",)),
    )(proj, dq, dk, dv, drg, dcqn, dckvn, dkr8, *tabs, qn, kvn)


def _post(o_ret, proj, gn):
    L, W = o_ret.shape
    tr = _tile(L, 384)

    def body(o_ref, rg_ref, gn_ref, out_ref):
        for h in range(HEADS):
            sl = slice(h * HD, (h + 1) * HD)
            o = o_ref[:, sl]
            rg = rg_ref[:, sl]
            n = o * lax.rsqrt(jnp.mean(o * o, axis=-1, keepdims=True) + EPS)
            out_ref[:, sl] = (n * gn_ref[:, sl] * (rg * _sigmoid(rg))).astype(BF16)

    row = pl.BlockSpec((tr, W), lambda i: (i, 0))
    return pl.pallas_call(
        body, name="ret_post", grid=(L // tr,), out_shape=jax.ShapeDtypeStruct((L, W), BF16),
        in_specs=[row, pl.BlockSpec((tr, W), lambda i: (i, C_RG // W)), pl.BlockSpec((1, W), lambda i: (0, 0))],
        out_specs=row, compiler_params=_params(("parallel",)),
    )(o_ret, proj, gn)


def _post_bwd(o_ret, proj, gn, dcat):
    L, W = o_ret.shape
    tr = _tile(L, 384)

    def body(o_ref, rg_ref, gn_ref, d_ref, do_ref, drg_ref, dgn_ref):
        @pl.when(pl.program_id(0) == 0)
        def _():
            dgn_ref[...] = jnp.zeros_like(dgn_ref)

        for h in range(HEADS):
            sl = slice(h * HD, (h + 1) * HD)
            o = o_ref[:, sl]
            rg = rg_ref[:, sl]
            d = d_ref[:, sl].astype(F32)
            gw = gn_ref[:, sl]
            r = lax.rsqrt(jnp.mean(o * o, axis=-1, keepdims=True) + EPS)
            n = o * r
            sg = _sigmoid(rg)
            si = rg * sg
            dn = d * gw * si
            dgn_ref[:, sl] += jnp.sum(d * n * si, axis=0, keepdims=True)
            drg_ref[:, sl] = d * n * gw * sg * (1.0 + rg * (1.0 - sg))
            do_ref[:, sl] = (r * (dn - o * (r * r) * jnp.mean(dn * o, axis=-1, keepdims=True))).astype(BF16)

    row = pl.BlockSpec((tr, W), lambda i: (i, 0))
    vec = pl.BlockSpec((1, W), lambda i: (0, 0))
    return pl.pallas_call(
        body, name="ret_post_bwd", grid=(L // tr,),
        out_shape=[jax.ShapeDtypeStruct((L, W), BF16), jax.ShapeDtypeStruct((L, W), F32),
                   jax.ShapeDtypeStruct((1, W), F32)],
        in_specs=[row, pl.BlockSpec((tr, W), lambda i: (i, C_RG // W)), vec, row],
        out_specs=[row, row, vec], compiler_params=_params(("arbitrary",)),
    )(o_ret, proj, gn, dcat)


def _lin_attn(name, q, k, v, lg, reverse):
    L, W = q.shape
    nc = L // BLK - 1

    def body(q_ref, k_ref, v_ref, lg_ref, o_ref, s_ref):
        lgv = lg_ref[0:1, :]
        n = lax.broadcasted_iota(jnp.int32, (BLK, BLK), 0).astype(F32)
        m = lax.broadcasted_iota(jnp.int32, (BLK, BLK), 1).astype(F32)
        dist = (m - n) if reverse else (n - m)
        dmask = jnp.where(dist >= 0, jnp.exp(lgv * jnp.maximum(dist, 0.0)), 0.0)
        dmask0 = jnp.where((n < N_META) & (m < N_META), dmask, 0.0)
        gl = jnp.exp(lgv * float(BLK))
        if reverse:
            inter = jnp.exp(lgv * (float(BLK) - n))
            inter0 = jnp.where(n < N_META, jnp.exp(lgv * jnp.maximum(float(N_META) - n, 0.0)), 0.0)
            upd = jnp.exp(lgv * n)
        else:
            inter = jnp.exp(lgv * (n + 1.0))
            upd = jnp.exp(lgv * (float(BLK) - 1.0 - n))
            upd0 = jnp.where(n < N_META, jnp.exp(lgv * jnp.maximum(float(N_META) - 1.0 - n, 0.0)), 0.0)

        def chunk(c):
            rows = pl.ds(pl.multiple_of(c * BLK, BLK), BLK)
            qc, kc, vc = q_ref[rows, :], k_ref[rows, :], v_ref[rows, :]
            a = _dot(qc, kc, 'nt') * dmask
            o = _dot(a.astype(BF16), vc, 'nn') + _dot(qc, s_ref[...].astype(BF16), 'nn') * inter
            o_ref[rows, :] = o
            s_ref[...] = s_ref[...] * gl + _dot((kc.astype(F32) * upd).astype(BF16), vc, 'tn')

        q0, k0, v0 = q_ref[0:BLK, :], k_ref[0:BLK, :], v_ref[0:BLK, :]
        a0 = _dot(q0, k0, 'nt') * dmask0
        if reverse:
            s_ref[...] = jnp.zeros_like(s_ref)

            def step(t, carry):
                chunk(nc - t)
                return carry

            lax.fori_loop(0, nc, step, 0)
            o_ref[0:BLK, :] = _dot(a0.astype(BF16), v0, 'nn') + _dot(q0, s_ref[...].astype(BF16), 'nn') * inter0
        else:
            o_ref[0:BLK, :] = _dot(a0.astype(BF16), v0, 'nn')
            s_ref[...] = _dot((k0.astype(F32) * upd0).astype(BF16), v0, 'tn')

            def step(t, carry):
                chunk(t + 1)
                return carry

            lax.fori_loop(0, nc, step, 0)

    col = pl.BlockSpec((L, HD), lambda h: (0, h))
    return pl.pallas_call(
        body, name=name, grid=(HEADS,), out_shape=jax.ShapeDtypeStruct((L, W), F32),
        in_specs=[col, col, col, pl.BlockSpec((None, 8, HD), lambda h: (h, 0, 0))], out_specs=col,
        scratch_shapes=[pltpu.VMEM((HD, HD), F32)], compiler_params=_params(("parallel",)),
    )(q, k, v, lg)


ATT_SCALE = (HD + ROPE) ** -0.5
NEG = -1e30


ATT_TILE = 384


def _att_valid(T, row0, col0):
    r = lax.broadcasted_iota(jnp.int32, (T, T), 0) + row0
    c = lax.broadcasted_iota(jnp.int32, (T, T), 1) + col0
    return (c <= r) & ((c < N_META) | (c >= BLK))


def _attn_fwd(qm, kn, krr, vm, comm=None):
    L = qm.shape[0]
    W = HEADS * HD
    T = _tile(L, ATT_TILE, BLK)
    nb = L // T
    n_cm = comm.n if comm is not None else 0

    def body(*refs):
        q_ref, kn_ref, kr_ref, v_ref = refs[:4]
        o_ref, lse_ref = refs[4 + n_cm:6 + n_cm]
        m_sc, l_sc, acc_sc = refs[6 + 2 * n_cm:9 + 2 * n_cm]
        if comm is not None:
            cm_refs = (refs[4:4 + n_cm], refs[6 + n_cm:6 + 2 * n_cm], refs[9 + 2 * n_cm:])
            first, last = _grid_edges((HEADS, nb))

            @pl.when(first)
            def _():
                comm.start(*cm_refs)

        i = pl.program_id(1)
        q = q_ref[...]
        m_sc[...] = jnp.full_like(m_sc, NEG)
        l_sc[...] = jnp.zeros_like(l_sc)
        acc_sc[...] = jnp.zeros_like(acc_sc)

        def tile(j, masked):
            rows = pl.ds(pl.multiple_of(j * T, T), T)
            k = jnp.concatenate([kn_ref[rows, :], kr_ref[rows, :]], axis=1)
            s = _dot(q, k, 'nt') * ATT_SCALE
            if masked:
                s = jnp.where(_att_valid(T, i * T, j * T), s, NEG)
            m_prev = m_sc[...]
            m_new = jnp.maximum(m_prev, jnp.max(s, axis=-1, keepdims=True))
            p = jnp.exp(s - m_new)
            alpha = jnp.exp(m_prev - m_new)
            l_sc[...] = alpha * l_sc[...] + jnp.sum(p, axis=-1, keepdims=True)
            acc_sc[...] = alpha * acc_sc[...] + _dot(p.astype(BF16), v_ref[rows, :], 'nn')
            m_sc[...] = m_new

        tile(0, True)

        def mid(j, carry):
            tile(j, False)
            return carry

        lax.fori_loop(1, i, mid, 0)

        @pl.when(i > 0)
        def _():
            tile(i, True)

        l = l_sc[...]
        o_ref[...] = (acc_sc[...] / l).astype(o_ref.dtype)
        lse_ref[...] = jnp.broadcast_to(m_sc[...] + jnp.log(l), (T, HD))

        if comm is not None:
            @pl.when(last)
            def _():
                comm.finish(*cm_refs)

    cm_specs = comm.specs if comm is not None else []
    res = pl.pallas_call(
        body, name="attn_fwd", grid=(HEADS, nb),
        out_shape=[jax.ShapeDtypeStruct((L, W), BF16), jax.ShapeDtypeStruct((HEADS, L, HD), F32)]
        + (comm.out_shapes if comm is not None else []),
        in_specs=[pl.BlockSpec((T, QH), lambda h, i: (i, h)), pl.BlockSpec((L, HD), lambda h, i: (0, h)),
                  pl.BlockSpec((L, HD), lambda h, i: (0, 0)), pl.BlockSpec((L, HD), lambda h, i: (0, h))] + cm_specs,
        out_specs=[pl.BlockSpec((T, HD), lambda h, i: (i, h)),
                   pl.BlockSpec((None, T, HD), lambda h, i: (h, i, 0))] + cm_specs,
        scratch_shapes=[pltpu.VMEM((T, 1), F32), pltpu.VMEM((T, 1), F32), pltpu.VMEM((T, HD), F32)]
        + (comm.scratch if comm is not None else []),
        compiler_params=_params(("arbitrary", "arbitrary")),
    )(qm, kn, krr, vm, *(comm.arrays if comm is not None else []))
    return res[:2], res[2:]


def _attn_bwd(qm, kn, krr, vm, o, dcat, lse, comm=None):
    L = qm.shape[0]
    W = HEADS * HD
    T = _tile(L, ATT_TILE, BLK)
    nb = L // T
    n_cm = comm.n if comm is not None else 0

    def body(*refs):
        q_ref, kn_ref, kr_ref, v_ref, o_ref, do_ref, lse_ref = refs[:7]
        dq_ref, dkn_ref, dkr_ref, dv_ref = refs[7 + n_cm:11 + n_cm]
        dl_sc, dk_sc, dv_sc = refs[11 + 2 * n_cm:14 + 2 * n_cm]
        if comm is not None:
            cm_refs = (refs[7:7 + n_cm], refs[11 + n_cm:11 + 2 * n_cm], refs[14 + 2 * n_cm:])
            first, last = _grid_edges((HEADS, nb))

            @pl.when(first)
            def _():
                comm.start(*cm_refs)

        j = pl.program_id(1)

        @pl.when(j == 0)
        def _():
            dq_ref[...] = jnp.zeros_like(dq_ref)

            def rowsum(t, carry):
                rows = pl.ds(pl.multiple_of(t * T, T), T)
                dl_sc[rows, :] = jnp.sum(do_ref[rows, :].astype(F32) * o_ref[rows, :].astype(F32), axis=-1,
                                         keepdims=True)
                return carry

            lax.fori_loop(0, nb, rowsum, 0)

        k = jnp.concatenate([kn_ref[...], kr_ref[...]], axis=1)
        v = v_ref[...]
        dk_sc[...] = jnp.zeros_like(dk_sc)
        dv_sc[...] = jnp.zeros_like(dv_sc)

        def tile(i, masked):
            rows = pl.ds(pl.multiple_of(i * T, T), T)
            q = q_ref[rows, :]
            do = do_ref[rows, :]
            s = _dot(q, k, 'nt') * ATT_SCALE
            if masked:
                s = jnp.where(_att_valid(T, i * T, j * T), s, NEG)
            p = jnp.exp(s - lse_ref[rows, 0:1])
            dv_sc[...] += _dot(p.astype(BF16), do, 'tn')
            ds = (p * (_dot(do, v, 'nt') - dl_sc[rows, :]) * ATT_SCALE).astype(BF16)
            dk_sc[...] += _dot(ds, q, 'tn')
            dq_ref[rows, :] += _dot(ds, k, 'nn')

        tile(j, True)

        def rest(masked):
            def step(i, carry):
                tile(i, masked)
                return carry
            lax.fori_loop(j + 1, nb, step, 0)

        @pl.when(j == 0)
        def _():
            rest(True)

        @pl.when(j > 0)
        def _():
            rest(False)

        dk = dk_sc[...]
        dkn_ref[...] = dk[:, 0:HD].astype(BF16)
        dkr_ref[...] = dk[:, HD:QH]
        dv_ref[...] = dv_sc[...].astype(BF16)

        if comm is not None:
            @pl.when(last)
            def _():
                comm.finish(*cm_refs)

    blk = pl.BlockSpec((T, HD), lambda h, j: (j, h))
    cm_specs = comm.specs if comm is not None else []
    res = pl.pallas_call(
        body, name="attn_bwd", grid=(HEADS, nb),
        out_shape=[jax.ShapeDtypeStruct((L, HEADS * QH), F32), jax.ShapeDtypeStruct((L, W), BF16),
                   jax.ShapeDtypeStruct((L, W), F32), jax.ShapeDtypeStruct((L, W), BF16)]
        + (comm.out_shapes if comm is not None else []),
        in_specs=[pl.BlockSpec((L, QH), lambda h, j: (0, h)), blk, pl.BlockSpec((T, HD), lambda h, j: (j, 0)), blk,
                  pl.BlockSpec((L, HD), lambda h, j: (0, h)), pl.BlockSpec((L, HD), lambda h, j: (0, HEADS + h)),
                  pl.BlockSpec((None, L, HD), lambda h, j: (h, 0, 0))] + cm_specs,
        out_specs=[pl.BlockSpec((L, QH), lambda h, j: (0, h)), blk, blk, blk] + cm_specs,
        scratch_shapes=[pltpu.VMEM((L, 1), F32), pltpu.VMEM((T, QH), F32), pltpu.VMEM((T, HD), F32)]
        + (comm.scratch if comm is not None else []),
        compiler_params=_params(("arbitrary", "arbitrary")),
    )(qm, kn, krr, vm, o, dcat, lse, *(comm.arrays if comm is not None else []))
    return res[:4], res[4:]


def _unrope_q(dqm, tabs_m):
    L, W = dqm.shape
    tr = _tile(L, 384)

    def body(d_ref, cm_ref, sa_ref, sb_ref, out_ref):
        cm, sa, sb = cm_ref[...], sa_ref[...], sb_ref[...]
        for h in range(HEADS):
            out_ref[:, h * QH:h * QH + HD] = d_ref[:, h * QH:h * QH + HD].astype(BF16)
            out_ref[:, h * QH + HD:(h + 1) * QH] = _rope_mla_t(d_ref[:, h * QH + HD:(h + 1) * QH], cm, sa,
                                                               sb).astype(BF16)

    row = pl.BlockSpec((tr, W), lambda i: (i, 0))
    tab = pl.BlockSpec((tr, HD), lambda i: (i, 0))
    return pl.pallas_call(
        body, name="unrope_q", grid=(L // tr,), out_shape=jax.ShapeDtypeStruct((L, W), BF16),
        in_specs=[row, tab, tab, tab], out_specs=row, compiler_params=_params(("parallel",)),
    )(dqm, *tabs_m)


def _q_up(cqn, wuq_p, tabs_m):
    L = cqn.shape[0]
    tm = _tile(L, 704)

    def ep(acc, cm, sa, sb):
        parts = []
        for h in range(HEADS):
            parts.append(acc[:, h * QH:h * QH + HD])
            parts.append(_rope_mla(acc[:, h * QH + HD:(h + 1) * QH], cm, sa, sb))
        return (jnp.concatenate(parts, axis=1),)

    tab = pl.BlockSpec((tm, HD), lambda i, j: (i, 0))
    return _mm("mla_q_up", (L // tm, 1), ("parallel", "parallel"), None,
               [cqn, wuq_p], [pl.BlockSpec((tm, Q_RANK), lambda i, j: (i, 0)),
                              pl.BlockSpec((Q_RANK, HEADS * QH), lambda i, j: (0, 0))],
               [(0, 1, 'nn', 0)], [(tm, HEADS * QH)], list(tabs_m), [tab] * 3, ep,
               [jax.ShapeDtypeStruct((L, HEADS * QH), BF16)], [pl.BlockSpec((tm, HEADS * QH), lambda i, j: (i, 0))])[0]


def _mix_out(cat, w_out, h_in, post):
    L, K = cat.shape
    D = w_out.shape[1]
    tm, tk = _tile(L, 384), _tile(K, 512, 128)
    row = pl.BlockSpec((tm, D), lambda i, k: (i, 0))
    return _mm("mix_out", (L // tm, K // tk), ("parallel", "arbitrary"), 1,
               [cat, w_out], [pl.BlockSpec((tm, tk), lambda i, k: (i, k)), pl.BlockSpec((tk, D), lambda i, k: (k, 0))],
               [(0, 1, 'nn', 0)], [(tm, D)], [h_in, post], [row, pl.BlockSpec((1, D), lambda i, k: (0, 0))],
               _resnorm_epilogue(1.0), [jax.ShapeDtypeStruct((L, D), F32)] * 2, [row, row])


def _adam_math(w, g, m, v):
    m = ADAM_B1 * m + (1.0 - ADAM_B1) * g
    v = ADAM_B2 * v + (1.0 - ADAM_B2) * (g * g)
    m_hat = m / (1.0 - ADAM_B1 ** ADAM_STEP)
    v_hat = v / (1.0 - ADAM_B2 ** ADAM_STEP)
    delta = -ADAM_LR * (m_hat / (jnp.sqrt(v_hat) + ADAM_EPS) + ADAM_WD * w)
    return delta, m, v


def _adam(name, w, m, v, g_slots=None, g=None):
    R, C = w.shape
    tr = _tile(R, 256, 8)
    from_slots = g_slots is not None

    def body(w_ref, m_ref, v_ref, g_ref, go_ref, d_ref, mo_ref, vo_ref):
        if from_slots:
            grad = g_ref[0].astype(F32)
            for s in range(1, N_DEV):
                grad = grad + g_ref[s].astype(F32)
        else:
            grad = g_ref[...]
        delta, mn, vn = _adam_math(w_ref[...], grad, m_ref[...], v_ref[...])
        go_ref[...] = grad
        d_ref[...] = delta
        mo_ref[...] = mn
        vo_ref[...] = vn

    row = pl.BlockSpec((tr, C), lambda i: (i, 0))
    gspec = pl.BlockSpec((N_DEV, tr, C), lambda i: (0, i, 0)) if from_slots else row
    return pl.pallas_call(
        body, name=name, grid=(R // tr,), out_shape=[jax.ShapeDtypeStruct((R, C), F32)] * 4,
        in_specs=[row, row, row, gspec], out_specs=[row] * 4, compiler_params=_params(("parallel",)),
    )(w, m, v, g_slots if from_slots else g)


def _unblock(gathered):
    n, r, c = gathered.shape
    return jnp.transpose(gathered, (1, 0, 2)).reshape(r, n * c)


def _reblock(full, c):
    r = full.shape[0]
    return jnp.transpose(full[:, :N_DEV * c].reshape(r, N_DEV, c), (1, 0, 2))


def _step(x, target, w, mom, vel):
    S, D = x.shape[1], x.shape[2]
    L = S + BLK
    sq = lambda a: a.reshape(a.shape[1:]) if a.ndim == 3 else a
    p = {n: sq(w[n]) for n in WEIGHTS if n != 'meta_tokens'}
    gather = lambda names: _Exchange([p[n].astype(BF16) for n in names], False)
    scatter = lambda blocks: _Exchange(blocks, True)
    in_s, uq_s = p['w_in'].shape[1], p['mla_w_uq'].shape[1]
    tabs = _rope_tables(L)
    tabs_m = tabs[2:]
    lg = jnp.broadcast_to(jnp.log(1.0 - 2.0 ** (-5.0 - jnp.arange(HEADS, dtype=F32)))[:, None, None], (HEADS, 8, HD))
    R = {}

    wg1, wu1, meta = _exchange("gather_first", [p['ffn1_w_gate'].astype(BF16), p['ffn1_w_up'].astype(BF16),
                                                w['meta_tokens']], False)
    h0 = jnp.concatenate([_unblock(meta), jnp.zeros((BLK - N_META, D), F32), x[0]], axis=0)
    a1 = _norm_fwd(h0, p['ffn1_pre_norm'])
    (g1, u1, hid1), (wd1,) = _ffn_up(a1, wg1, wu1, comm=gather(['ffn1_w_down']))
    (f1, h1), (w_in_g,) = _ffn_down(hid1, wd1, h0, p['ffn1_post_norm'], comm=gather(['w_in']))

    w_in_full = _unblock(w_in_g)
    w_in = jnp.pad(w_in_full, ((0, 0), (0, D_INP - w_in_full.shape[1])))
    um = _norm_fwd(h1, p['mix_pre_norm'])
    proj, (uq_g, uk_g, uv_g, wout_g) = _mm_nn("mix_in", um, w_in, F32,
                                              comm=gather(['mla_w_uq', 'mla_w_uk', 'mla_w_uv', 'w_out']))
    wuq = jnp.pad(_unblock(uq_g).reshape(Q_RANK, HEADS, HD + ROPE),
                  ((0, 0), (0, 0), (0, QH - HD - ROPE))).reshape(Q_RANK, HEADS * QH)
    wuk, wuv, w_out = _unblock(uk_g), _unblock(uv_g), wout_g.reshape(-1, D)
    qr, kr, vr, cqn, ckvn, krr = _prep(proj, tabs, p['mla_q_norm'], p['mla_kv_norm'])
    qm = _q_up(cqn, wuq, tabs_m)
    kn = _mm_nn("mla_k_up", ckvn, wuk, BF16)
    vm = _mm_nn("mla_v_up", ckvn, wuv, BF16)
    (o_mla, lse), (wg2, wu2) = _attn_fwd(qm, kn, krr, vm, comm=gather(['ffn2_w_gate', 'ffn2_w_up']))
    o_ret = _lin_attn("ret_fwd", qr, kr, vr, lg, False)
    ret = _post(o_ret, proj, p['ret_group_norm'])
    cat = jnp.concatenate([ret, o_mla], axis=1)
    m, h2 = _mix_out(cat, w_out, h1, p['mix_post_norm'])

    a2 = _norm_fwd(h2, p['ffn2_pre_norm'])
    (g2, u2, hid2), (wd2,) = _ffn_up(a2, wg2, wu2, comm=gather(['ffn2_w_down']))
    f2, h3 = _ffn_down(hid2, wd2, h2, p['ffn2_post_norm'])
    dh3, loss_blk = _loss(h3, target[0])

    dsmall = {}
    df2, dsmall['ffn2_post_norm'] = _norm_bwd(f2, p['ffn2_post_norm'], dh3, None, 0.5, BF16)
    dg2, du2 = _ffn_dhid(df2, wd2, g2, u2)
    dwd2 = _ffn_dwd(hid2, df2)
    (dwg2, dwu2), (R['ffn2_w_down'],) = _ffn_dwgu(a2, dg2, du2, comm=scatter([dwd2]))
    da2, (R['ffn2_w_gate'],) = _ffn_da(dg2, du2, wg2, wu2, comm=scatter([dwg2]))
    dh2, dsmall['ffn2_pre_norm'] = _norm_bwd(h2, p['ffn2_pre_norm'], da2, dh3, 1.0, F32)

    dm, dsmall['mix_post_norm'] = _norm_bwd(m, p['mix_post_norm'], dh2, None, 1.0, BF16)
    dcat = _mm_nt("mix_dcat", [(dm, w_out)], BF16)
    dwout = _mm_tn("mix_dwout", cat, [dm])[0]
    do_ret, drg, dsmall['ret_group_norm'] = _post_bwd(o_ret, proj, p['ret_group_norm'], dcat)
    dqr = _lin_attn("ret_dq", do_ret, vr, kr, lg, False)
    dkr = _lin_attn("ret_dk", vr, do_ret, qr, lg, True)
    dvr = _lin_attn("ret_dv", kr, qr, do_ret, lg, True)
    (dqm, dkn, dkr8, dvm), (R['ffn2_w_up'], R['w_out']) = _attn_bwd(
        qm, kn, krr, vm, o_mla, dcat, lse, comm=scatter([dwu2, dwout.reshape(N_DEV, -1, D)]))
    dqp = _unrope_q(dqm, tabs_m)
    dwuq = _mm_tn("mla_dwuq", cqn, [dqp])[0]
    dcqn = _mm_nt("mla_dcq", [(dqp, wuq)], F32)
    dwuk, dwuv = _mm_tn("mla_dwukv", ckvn, [dkn, dvm])
    dckvn = _mm_nt("mla_dckv", [(dkn, wuk), (dvm, wuv)], F32)
    dproj, dsmall['mla_q_norm'], dsmall['mla_kv_norm'] = _prep_bwd(
        proj, dqr, dkr, dvr, drg, dcqn, dckvn, dkr8, tabs, p['mla_q_norm'], p['mla_kv_norm'])
    dwuq_b = _reblock(dwuq.reshape(Q_RANK, HEADS, QH)[:, :, :HD + ROPE].reshape(Q_RANK, HEADS * (HD + ROPE)), uq_s)
    (dwin,), (R['mla_w_uq'], R['mla_w_uk'], R['mla_w_uv']) = _mm_tn(
        "mix_dwin", um, [dproj], comm=scatter([dwuq_b, _reblock(dwuk, p['mla_w_uk'].shape[1]),
                                               _reblock(dwuv, p['mla_w_uv'].shape[1])]))
    dum = _mm_nt("mix_du", [(dproj, w_in)], F32)
    dh1, dsmall['mix_pre_norm'] = _norm_bwd(h1, p['mix_pre_norm'], dum, dh2, 1.0, F32)

    df1, dsmall['ffn1_post_norm'] = _norm_bwd(f1, p['ffn1_post_norm'], dh1, None, 0.5, BF16)
    (dg1, du1), (R['w_in'],) = _ffn_dhid(df1, wd1, g1, u1, comm=scatter([_reblock(dwin, in_s)]))
    dwg1, dwu1 = _ffn_dwgu(a1, dg1, du1)
    dwd1, (R['ffn1_w_gate'],) = _ffn_dwd(hid1, df1, comm=scatter([dwg1]))
    da1, (R['ffn1_w_up'],) = _ffn_da(dg1, du1, wg1, wu1, comm=scatter([dwu1]))
    dh0, dsmall['ffn1_pre_norm'] = _norm_bwd(h0, p['ffn1_pre_norm'], da1, dh1, 1.0, F32)
    R['ffn1_w_down'], = _exchange("scatter_last", [dwd1], True)

    def slab(a):
        a = a.reshape(-1, 128)
        return jnp.pad(a, ((0, (-a.shape[0]) % 8), (0, 0)))

    slab_rows = lambda n: -(-(p[n].shape[-1] // 128) // 8) * 8
    packed = jnp.concatenate([slab(dsmall[n]) for n in SMALL] + [slab(dh0[:N_META]), loss_blk], axis=0)
    red = _allreduce_small(packed)
    offs = sum(slab_rows(n) for n in SMALL)
    n_small = offs
    gmeta_full = red[offs:offs + N_META * D // 128].reshape(N_META, D)
    offs += N_META * D // 128
    loss = red[offs, 0]

    grad, delta, new_m, new_v = {}, {}, {}, {}
    for n in BIG:
        outs = _adam("adam_" + n, p[n], sq(mom[n]), sq(vel[n]), g_slots=R[n])
        grad[n], delta[n], new_m[n], new_v[n] = [o.reshape(w[n].shape) for o in outs]
    pack = lambda d: jnp.concatenate([slab(d[n]) for n in SMALL], axis=0)
    outs = _adam("adam_small", pack(w), pack(mom), pack(vel), g=red[:n_small])
    offs = 0
    for n in SMALL:
        r = p[n].shape[-1] // 128
        grad[n], delta[n], new_m[n], new_v[n] = [o[offs:offs + r].reshape(w[n].shape) for o in outs]
        offs += slab_rows(n)
    dev = 4 * lax.axis_index("x") + 2 * lax.axis_index("y") + lax.axis_index("c")
    mcols = w['meta_tokens'].shape[1]
    gmeta = lax.dynamic_slice(gmeta_full, (0, dev * mcols), (N_META, mcols))
    outs = _adam("adam_meta", w['meta_tokens'], mom['meta_tokens'], vel['meta_tokens'], g=gmeta)
    grad['meta_tokens'], delta['meta_tokens'], new_m['meta_tokens'], new_v['meta_tokens'] = outs

    return (loss, dh0[BLK:][None], *[grad[n] for n in WEIGHTS], *[delta[n] for n in WEIGHTS],
            *[new_m[n] for n in WEIGHTS], *[new_v[n] for n in WEIGHTS])


def kernel(x, meta_tokens, ffn1_pre_norm, ffn1_w_gate, ffn1_w_up, ffn1_w_down, ffn1_post_norm, mix_pre_norm, w_in, ret_group_norm, mla_q_norm, mla_w_uq, mla_kv_norm, mla_w_uk, mla_w_uv, w_out, mix_post_norm, ffn2_pre_norm, ffn2_w_gate, ffn2_w_up, ffn2_w_down, ffn2_post_norm, loss_target, m_meta_tokens, m_ffn1_pre_norm, m_ffn1_w_gate, m_ffn1_w_up, m_ffn1_w_down, m_ffn1_post_norm, m_mix_pre_norm, m_w_in, m_ret_group_norm, m_mla_q_norm, m_mla_w_uq, m_mla_kv_norm, m_mla_w_uk, m_mla_w_uv, m_w_out, m_mix_post_norm, m_ffn2_pre_norm, m_ffn2_w_gate, m_ffn2_w_up, m_ffn2_w_down, m_ffn2_post_norm, v_meta_tokens, v_ffn1_pre_norm, v_ffn1_w_gate, v_ffn1_w_up, v_ffn1_w_down, v_ffn1_post_norm, v_mix_pre_norm, v_w_in, v_ret_group_norm, v_mla_q_norm, v_mla_w_uq, v_mla_kv_norm, v_mla_w_uk, v_mla_w_uv, v_w_out, v_mix_post_norm, v_ffn2_pre_norm, v_ffn2_w_gate, v_ffn2_w_up, v_ffn2_w_down, v_ffn2_post_norm):
    w = dict(zip(WEIGHTS, (meta_tokens, ffn1_pre_norm, ffn1_w_gate, ffn1_w_up, ffn1_w_down, ffn1_post_norm,
                           mix_pre_norm, w_in, ret_group_norm, mla_q_norm, mla_w_uq, mla_kv_norm, mla_w_uk, mla_w_uv,
                           w_out, mix_post_norm, ffn2_pre_norm, ffn2_w_gate, ffn2_w_up, ffn2_w_down, ffn2_post_norm)))
    mom = dict(zip(WEIGHTS, (m_meta_tokens, m_ffn1_pre_norm, m_ffn1_w_gate, m_ffn1_w_up, m_ffn1_w_down,
                             m_ffn1_post_norm, m_mix_pre_norm, m_w_in, m_ret_group_norm, m_mla_q_norm, m_mla_w_uq,
                             m_mla_kv_norm, m_mla_w_uk, m_mla_w_uv, m_w_out, m_mix_post_norm, m_ffn2_pre_norm,
                             m_ffn2_w_gate, m_ffn2_w_up, m_ffn2_w_down, m_ffn2_post_norm)))
    vel = dict(zip(WEIGHTS, (v_meta_tokens, v_ffn1_pre_norm, v_ffn1_w_gate, v_ffn1_w_up, v_ffn1_w_down,
                             v_ffn1_post_norm, v_mix_pre_norm, v_w_in, v_ret_group_norm, v_mla_q_norm, v_mla_w_uq,
                             v_mla_kv_norm, v_mla_w_uk, v_mla_w_uv, v_w_out, v_mix_post_norm, v_ffn2_pre_norm,
                             v_ffn2_w_gate, v_ffn2_w_up, v_ffn2_w_down, v_ffn2_post_norm)))
    return _step(x, loss_target, w, mom, vel)
```

```python
import functools
import math

import jax
import jax.numpy as jnp
from jax import lax
from jax.experimental import pallas as pl
from jax.experimental.pallas import tpu as pltpu

N_DEV = 8
N_META = 16
BLK = 128
HEADS = 8
HD = 128
ROPE = 64
Q_RANK = 512
KV_RANK = 256
QH = 2 * HD
D_INP = 4 * HEADS * HD + Q_RANK + KV_RANK + BLK
ROPE_THETA = 10000.0
EPS = 1e-6
ADAM_LR = 0.001
ADAM_B1 = 0.9
ADAM_B2 = 0.999
ADAM_EPS = 1e-08
ADAM_WD = 0.01
ADAM_STEP = 10
V7X_VMEM_LIMIT = 48 * 1024 * 1024
MESH = pl.DeviceIdType.MESH
F32 = jnp.float32
BF16 = jnp.bfloat16

WEIGHTS = ['meta_tokens', 'ffn1_pre_norm', 'ffn1_w_gate', 'ffn1_w_up', 'ffn1_w_down', 'ffn1_post_norm',
           'mix_pre_norm', 'w_in', 'ret_group_norm', 'mla_q_norm', 'mla_w_uq', 'mla_kv_norm', 'mla_w_uk',
           'mla_w_uv', 'w_out', 'mix_post_norm', 'ffn2_pre_norm', 'ffn2_w_gate', 'ffn2_w_up', 'ffn2_w_down',
           'ffn2_post_norm']
SMALL = ['ffn1_pre_norm', 'ffn1_post_norm', 'mix_pre_norm', 'ret_group_norm', 'mla_q_norm', 'mla_kv_norm',
         'mix_post_norm', 'ffn2_pre_norm', 'ffn2_post_norm']
BIG = ['ffn1_w_gate', 'ffn1_w_up', 'ffn1_w_down', 'w_in', 'mla_w_uq', 'mla_w_uk', 'mla_w_uv', 'w_out',
       'ffn2_w_gate', 'ffn2_w_up', 'ffn2_w_down']

_DIMS = {'nn': (((1,), (0,)), ((), ())), 'nt': (((1,), (1,)), ((), ())), 'tn': (((0,), (0,)), ((), ()))}


def _tile(n, target, mult=16):
    best = None
    for t in range(mult, min(n, target) + 1, mult):
        if n % t == 0:
            best = t
    return best if best is not None else n


def _params(sem):
    return pltpu.CompilerParams(dimension_semantics=sem, vmem_limit_bytes=V7X_VMEM_LIMIT)


def _dot(a, b, dims):
    return lax.dot_general(a, b, _DIMS[dims], preferred_element_type=F32)


def _sigmoid(x):
    return 1.0 / (1.0 + jnp.exp(-x))


def _me_and_peers():
    x, y, c = lax.axis_index("x"), lax.axis_index("y"), lax.axis_index("c")

    def peer(j):
        px = 1 - x if (j >> 2) & 1 else x
        py = 1 - y if (j >> 1) & 1 else y
        pc = 1 - c if j & 1 else c
        return (px, py, pc), 4 * px + 2 * py + pc

    return 4 * x + 2 * y + c, peer


class _Exchange:
    def __init__(self, arrays, per_peer):
        self.arrays = list(arrays)
        self.per_peer = per_peer
        self.n = len(self.arrays)
        self.out_shapes = [jax.ShapeDtypeStruct((N_DEV,) + tuple(a.shape[1:] if per_peer else a.shape), a.dtype)
                           for a in self.arrays]
        self.specs = [pl.BlockSpec(memory_space=pl.ANY)] * self.n
        self.scratch = [pltpu.SemaphoreType.DMA((7 * self.n,)), pltpu.SemaphoreType.DMA((7 * self.n,)),
                        pltpu.SemaphoreType.DMA((self.n,))]

    def _copies(self, src, dst, sems):
        send_sems, recv_sems, local_sems = sems
        me, peer = _me_and_peers()
        sib, _ = peer(1)
        local, sends, recvs, passes = [], {}, {}, {}
        for k in range(self.n):
            own = src[k].at[me] if self.per_peer else src[k]
            local.append(pltpu.make_async_copy(own, dst[k].at[me], local_sems.at[k]))
            for j in range(1, N_DEV):
                pid, pidx = peer(j)
                out = src[k].at[pidx] if self.per_peer else src[k]
                sem = dict(send_sem=send_sems.at[k * 7 + j - 1], recv_sem=recv_sems.at[k * 7 + j - 1])
                recvs[k, j] = pltpu.make_async_remote_copy(src_ref=out, dst_ref=dst[k].at[pidx], device_id=pid,
                                                           device_id_type=MESH, **sem)
                if self.per_peer or j in (1, 2, 4, 6):
                    sends[k, j] = pltpu.make_async_remote_copy(src_ref=out, dst_ref=dst[k].at[me], device_id=pid,
                                                               device_id_type=MESH, **sem)
                else:
                    _, origin = peer(j ^ 1)
                    passes[k, j ^ 1] = pltpu.make_async_remote_copy(
                        src_ref=dst[k].at[origin], dst_ref=dst[k].at[origin], device_id=sib, device_id_type=MESH, **sem)
        return local, sends, recvs, passes

    def start(self, src, dst, sems):
        local, sends, _, _ = self._copies(src, dst, sems)
        for cp in local + list(sends.values()):
            cp.start()

    def finish(self, src, dst, sems):
        local, sends, recvs, passes = self._copies(src, dst, sems)
        for key, cp in passes.items():
            recvs[key].wait_recv()
            cp.start()
        for key, cp in recvs.items():
            if key not in passes:
                cp.wait_recv()
        for cp in list(sends.values()) + list(passes.values()):
            cp.wait_send()
        for cp in local:
            cp.wait()


def _grid_edges(grid):
    first, last = None, None
    for a, n in enumerate(grid):
        f, l = pl.program_id(a) == 0, pl.program_id(a) == n - 1
        first = f if first is None else first & f
        last = l if last is None else last & l
    return first, last


def _exchange(name, arrays, per_peer):
    ex = _Exchange(arrays, per_peer)
    n = ex.n

    def body(*refs):
        ex.start(refs[:n], refs[n:2 * n], refs[2 * n:])
        ex.finish(refs[:n], refs[n:2 * n], refs[2 * n:])

    return pl.pallas_call(body, name=name, out_shape=ex.out_shapes, in_specs=ex.specs, out_specs=ex.specs,
                          scratch_shapes=ex.scratch)(*arrays)


def _allreduce_small(v):
    rows = v.shape[0]

    def body(v_ref, out_ref, buf, send_sems, recv_sems):
        me, peer = _me_and_peers()
        buf[pl.ds(me, 1)] = v_ref[...][None]
        sends = []
        for j in range(1, N_DEV):
            pid, _ = peer(j)
            cp = pltpu.make_async_remote_copy(src_ref=v_ref, dst_ref=buf.at[me], send_sem=send_sems.at[j - 1],
                                              recv_sem=recv_sems.at[j - 1], device_id=pid, device_id_type=MESH)
            cp.start()
            sends.append(cp)
        for j in range(1, N_DEV):
            pid, pidx = peer(j)
            pltpu.make_async_remote_copy(src_ref=v_ref, dst_ref=buf.at[pidx], send_sem=send_sems.at[j - 1],
                                         recv_sem=recv_sems.at[j - 1], device_id=pid,
                                         device_id_type=MESH).wait_recv()
        for cp in sends:
            cp.wait_send()
        acc = buf[0]
        for s in range(1, N_DEV):
            acc = acc + buf[s]
        out_ref[...] = acc

    vm = pl.BlockSpec(memory_space=pltpu.VMEM)
    return pl.pallas_call(
        body, name="allreduce_small", out_shape=jax.ShapeDtypeStruct(v.shape, F32),
        in_specs=[vm], out_specs=vm,
        scratch_shapes=[pltpu.VMEM((N_DEV, rows, 128), F32), pltpu.SemaphoreType.DMA((7,)),
                        pltpu.SemaphoreType.DMA((7,))],
    )(v)


def _mm(name, grid, sem, k_axis, ops, op_specs, pairs, acc_shapes, extras, extra_specs, epilogue, outs, out_specs,
        comm=None):
    n_op, n_ex, n_out = len(ops), len(extras), len(outs)
    nk = grid[k_axis] if k_axis is not None else 1
    n_acc = len(acc_shapes) if nk > 1 else 0
    n_cm = comm.n if comm is not None else 0

    def body(*refs):
        op_refs = refs[:n_op]
        ex_refs = refs[n_op:n_op + n_ex]
        n_in = n_op + n_ex + n_cm
        out_refs = refs[n_in:n_in + n_out]
        acc_refs = refs[n_in + n_out + n_cm:n_in + n_out + n_cm + n_acc]
        if comm is not None:
            cm_refs = (refs[n_op + n_ex:n_in], refs[n_in + n_out:n_in + n_out + n_cm],
                       refs[n_in + n_out + n_cm + n_acc:])
            first, last = _grid_edges(grid)

            @pl.when(first)
            def _():
                comm.start(*cm_refs)

        def finish(vals):
            res = epilogue(*vals, *[e[...] for e in ex_refs])
            for o, r in zip(out_refs, res):
                o[...] = r.astype(o.dtype)

        if nk == 1:
            parts = [None] * len(acc_shapes)
            for li, ri, dims, ai in pairs:
                d = _dot(op_refs[li][...], op_refs[ri][...], dims)
                parts[ai] = d if parts[ai] is None else parts[ai] + d
            finish(parts)
        else:
            k = pl.program_id(k_axis)

            @pl.when(k == 0)
            def _():
                for a in acc_refs:
                    a[...] = jnp.zeros_like(a)

            for li, ri, dims, ai in pairs:
                acc_refs[ai][...] += _dot(op_refs[li][...], op_refs[ri][...], dims)

            @pl.when(k == nk - 1)
            def _():
                finish([a[...] for a in acc_refs])

        if comm is not None:
            @pl.when(last)
            def _():
                comm.finish(*cm_refs)

    scratch = [pltpu.VMEM(s, F32) for s in acc_shapes] if nk > 1 else []
    if comm is None:
        return pl.pallas_call(
            body, name=name, grid=grid, out_shape=outs,
            in_specs=list(op_specs) + list(extra_specs), out_specs=list(out_specs),
            scratch_shapes=scratch, compiler_params=_params(sem),
        )(*ops, *extras)
    res = pl.pallas_call(
        body, name=name, grid=grid, out_shape=list(outs) + comm.out_shapes,
        in_specs=list(op_specs) + list(extra_specs) + comm.specs, out_specs=list(out_specs) + comm.specs,
        scratch_shapes=scratch + comm.scratch, compiler_params=_params(("arbitrary",) * len(grid)),
    )(*ops, *extras, *comm.arrays)
    return res[:n_out], res[n_out:]


def _with_comm(res, comm, pick):
    if comm is None:
        return pick(res)
    return pick(res[0]), res[1]


def _mm_nn(name, a, w, out_dtype, tm_target=704, tn_target=1664, epilogue=None, extras=(), extra_specs=(), comm=None):
    L, K = a.shape
    N = w.shape[1]
    tm, tn = _tile(L, tm_target), _tile(N, tn_target, 128)
    ep = epilogue if epilogue is not None else (lambda acc: (acc,))
    res = _mm(name, (L // tm, N // tn), ("parallel", "parallel"), None,
              [a, w], [pl.BlockSpec((tm, K), lambda i, j: (i, 0)), pl.BlockSpec((K, tn), lambda i, j: (0, j))],
              [(0, 1, 'nn', 0)], [(tm, tn)], list(extras), list(extra_specs), ep,
              [jax.ShapeDtypeStruct((L, N), out_dtype)], [pl.BlockSpec((tm, tn), lambda i, j: (i, j))], comm=comm)
    return _with_comm(res, comm, lambda o: o[0])


def _mm_nt(name, pairs_aw, out_dtype, tm_target=704, tn_target=512, comm=None):
    L = pairs_aw[0][0].shape[0]
    N = pairs_aw[0][1].shape[0]
    tm, tn = _tile(L, tm_target), _tile(N, tn_target, 128)
    ops, specs, pairs = [], [], []
    for t, (a, w) in enumerate(pairs_aw):
        K = a.shape[1]
        ops += [a, w]
        specs += [pl.BlockSpec((tm, K), lambda i, j: (i, 0)), pl.BlockSpec((tn, K), lambda i, j: (j, 0))]
        pairs.append((2 * t, 2 * t + 1, 'nt', 0))
    res = _mm(name, (L // tm, N // tn), ("parallel", "parallel"), None, ops, specs, pairs, [(tm, tn)], [], [],
              lambda acc: (acc,), [jax.ShapeDtypeStruct((L, N), out_dtype)],
              [pl.BlockSpec((tm, tn), lambda i, j: (i, j))], comm=comm)
    return _with_comm(res, comm, lambda o: o[0])


def _mm_tn(name, a, bs, out_dtype=BF16, tk_target=704, tn_target=1664, comm=None):
    L, M = a.shape
    N = bs[0].shape[1]
    tk, tn = _tile(L, tk_target), _tile(N, tn_target, 128)
    nb = len(bs)
    ops = [a] + list(bs)
    specs = [pl.BlockSpec((tk, M), lambda j, k: (k, 0))] + [pl.BlockSpec((tk, tn), lambda j, k: (k, j))] * nb
    res = _mm(name, (N // tn, L // tk), ("parallel", "arbitrary"), 1, ops, specs,
              [(0, 1 + t, 'tn', t) for t in range(nb)], [(M, tn)] * nb, [], [], lambda *acc: acc,
              [jax.ShapeDtypeStruct((M, N), out_dtype)] * nb, [pl.BlockSpec((M, tn), lambda j, k: (0, j))] * nb,
              comm=comm)
    return _with_comm(res, comm, lambda o: o)


def _norm_fwd(x, w):
    L, D = x.shape
    tr = _tile(L, 512)

    def body(x_ref, w_ref, y_ref):
        v = x_ref[...]
        r = lax.rsqrt(jnp.mean(v * v, axis=-1, keepdims=True) + EPS)
        y_ref[...] = (v * r * w_ref[...]).astype(y_ref.dtype)

    return pl.pallas_call(
        body, name="norm_fwd", grid=(L // tr,), out_shape=jax.ShapeDtypeStruct((L, D), BF16),
        in_specs=[pl.BlockSpec((tr, D), lambda i: (i, 0)), pl.BlockSpec((1, D), lambda i: (0, 0))],
        out_specs=pl.BlockSpec((tr, D), lambda i: (i, 0)), compiler_params=_params(("parallel",)),
    )(x, w)


def _norm_bwd_math(x, w, dy):
    r = lax.rsqrt(jnp.mean(x * x, axis=-1, keepdims=True) + EPS)
    gy = dy * w
    dx = r * (gy - x * (r * r) * jnp.mean(gy * x, axis=-1, keepdims=True))
    dw = jnp.sum(dy * x * r, axis=0, keepdims=True)
    return dx, dw


def _norm_bwd(x, w, dy, res, scale, out_dtype):
    L, D = x.shape
    tr = _tile(L, 384)
    has_res = res is not None

    def body(*refs):
        x_ref, w_ref, dy_ref = refs[:3]
        res_ref = refs[3] if has_res else None
        dx_ref, dw_ref = refs[-2:]
        dx, dw = _norm_bwd_math(x_ref[...], w_ref[...], dy_ref[...].astype(F32))
        dx = scale * dx
        if has_res:
            dx = dx + res_ref[...]
        dx_ref[...] = dx.astype(dx_ref.dtype)

        @pl.when(pl.program_id(0) == 0)
        def _():
            dw_ref[...] = jnp.zeros_like(dw_ref)

        dw_ref[...] += scale * dw

    row = pl.BlockSpec((tr, D), lambda i: (i, 0))
    vec = pl.BlockSpec((1, D), lambda i: (0, 0))
    return pl.pallas_call(
        body, name="norm_bwd", grid=(L // tr,),
        out_shape=[jax.ShapeDtypeStruct((L, D), out_dtype), jax.ShapeDtypeStruct((1, D), F32)],
        in_specs=[row, vec, row] + ([row] if has_res else []), out_specs=[row, vec],
        compiler_params=_params(("arbitrary",)),
    )(*([x, w, dy] + ([res] if has_res else [])))


def _loss(h, target):
    L, D = h.shape

    def body(h_ref, t_ref, dh_ref, loss_ref):
        i = pl.program_id(0)

        @pl.when(i == 0)
        def _():
            dh_ref[...] = jnp.zeros_like(dh_ref)
            loss_ref[...] = jnp.zeros_like(loss_ref)

        @pl.when(i > 0)
        def _():
            diff = h_ref[...] - t_ref[...]
            dh_ref[...] = diff * (1.0 / D)
            loss_ref[...] += 0.5 * jnp.sum(diff * diff) * (1.0 / D)

    return pl.pallas_call(
        body, name="loss", grid=(L // BLK,),
        out_shape=[jax.ShapeDtypeStruct((L, D), F32), jax.ShapeDtypeStruct((8, 128), F32)],
        in_specs=[pl.BlockSpec((BLK, D), lambda i: (i, 0)),
                  pl.BlockSpec((BLK, D), lambda i: (jnp.maximum(i - 1, 0), 0))],
        out_specs=[pl.BlockSpec((BLK, D), lambda i: (i, 0)), pl.BlockSpec((8, 128), lambda i: (0, 0))],
        compiler_params=_params(("arbitrary",)),
    )(h, target)


def _ffn_up(a, wg, wu, comm=None):
    L, D = a.shape
    F = wg.shape[2]
    tm = _tile(L, 704)

    def ep(g, u):
        return g, u, g * _sigmoid(g) * u

    hspec = pl.BlockSpec((None, tm, F), lambda i, j: (j, i, 0))
    wspec = pl.BlockSpec((None, D, F), lambda i, j: (j, 0, 0))
    res = _mm("ffn_up", (L // tm, N_DEV), ("parallel", "parallel"), None,
              [a, wg, wu], [pl.BlockSpec((tm, D), lambda i, j: (i, 0)), wspec, wspec],
              [(0, 1, 'nn', 0), (0, 2, 'nn', 1)], [(tm, F)] * 2, [], [], ep,
              [jax.ShapeDtypeStruct((N_DEV, L, F), BF16)] * 3, [hspec] * 3, comm=comm)
    return _with_comm(res, comm, lambda o: o)


def _resnorm_epilogue(scale):
    def ep(acc, h, w):
        r = lax.rsqrt(jnp.mean(acc * acc, axis=-1, keepdims=True) + EPS)
        return acc, h + scale * (acc * r * w)
    return ep


def _ffn_down(hid, wd, h_in, post, comm=None):
    _, L, F = hid.shape
    D = wd.shape[2]
    tm = _tile(L, 384)
    row = pl.BlockSpec((tm, D), lambda i, j: (i, 0))
    res = _mm("ffn_down", (L // tm, N_DEV), ("parallel", "arbitrary"), 1,
              [hid, wd], [pl.BlockSpec((None, tm, F), lambda i, j: (j, i, 0)),
                          pl.BlockSpec((None, F, D), lambda i, j: (j, 0, 0))],
              [(0, 1, 'nn', 0)], [(tm, D)], [h_in, post], [row, pl.BlockSpec((1, D), lambda i, j: (0, 0))],
              _resnorm_epilogue(0.5), [jax.ShapeDtypeStruct((L, D), F32)] * 2, [row, row], comm=comm)
    return _with_comm(res, comm, lambda o: o)


def _ffn_dhid(df, wd, g, u, comm=None):
    L, D = df.shape
    F = wd.shape[1]
    tm = _tile(L, 704)

    def ep(dhid, g_, u_):
        g32, u32 = g_.astype(F32), u_.astype(F32)
        sg = _sigmoid(g32)
        return dhid * u32 * sg * (1.0 + g32 * (1.0 - sg)), dhid * g32 * sg

    hspec = pl.BlockSpec((None, tm, F), lambda i, j: (j, i, 0))
    res = _mm("ffn_dhid", (L // tm, N_DEV), ("parallel", "parallel"), None,
              [df, wd], [pl.BlockSpec((tm, D), lambda i, j: (i, 0)),
                         pl.BlockSpec((None, F, D), lambda i, j: (j, 0, 0))],
              [(0, 1, 'nt', 0)], [(tm, F)], [g, u], [hspec, hspec], ep,
              [jax.ShapeDtypeStruct((N_DEV, L, F), BF16)] * 2, [hspec, hspec], comm=comm)
    return _with_comm(res, comm, lambda o: o)


def _ffn_dwd(hid, df, comm=None):
    _, L, F = hid.shape
    D = df.shape[1]
    tk = _tile(L, 1408)
    res = _mm("ffn_dwd", (N_DEV, L // tk), ("parallel", "arbitrary"), 1,
              [hid, df], [pl.BlockSpec((None, tk, F), lambda j, k: (j, k, 0)),
                          pl.BlockSpec((tk, D), lambda j, k: (k, 0))],
              [(0, 1, 'tn', 0)], [(F, D)], [], [], lambda acc: (acc,),
              [jax.ShapeDtypeStruct((N_DEV, F, D), BF16)], [pl.BlockSpec((None, F, D), lambda j, k: (j, 0, 0))],
              comm=comm)
    return _with_comm(res, comm, lambda o: o[0])


def _ffn_dwgu(a, dg, du, comm=None):
    L, D = a.shape
    F = dg.shape[2]
    tk = _tile(L, 704)
    hspec = pl.BlockSpec((None, tk, F), lambda j, k: (j, k, 0))
    wspec = pl.BlockSpec((None, D, F), lambda j, k: (j, 0, 0))
    res = _mm("ffn_dwgu", (N_DEV, L // tk), ("parallel", "arbitrary"), 1,
              [a, dg, du], [pl.BlockSpec((tk, D), lambda j, k: (k, 0)), hspec, hspec],
              [(0, 1, 'tn', 0), (0, 2, 'tn', 1)], [(D, F)] * 2, [], [], lambda *acc: acc,
              [jax.ShapeDtypeStruct((N_DEV, D, F), BF16)] * 2, [wspec, wspec], comm=comm)
    return _with_comm(res, comm, lambda o: o)


def _ffn_da(dg, du, wg, wu, comm=None):
    _, L, F = dg.shape
    D = wg.shape[1]
    tm = _tile(L, 384)
    hspec = pl.BlockSpec((None, tm, F), lambda i, j: (j, i, 0))
    wspec = pl.BlockSpec((None, D, F), lambda i, j: (j, 0, 0))
    row = pl.BlockSpec((tm, D), lambda i, j: (i, 0))
    res = _mm("ffn_da", (L // tm, N_DEV), ("parallel", "arbitrary"), 1,
              [dg, du, wg, wu], [hspec, hspec, wspec, wspec],
              [(0, 2, 'nt', 0), (1, 3, 'nt', 0)], [(tm, D)], [], [], lambda acc: (acc,),
              [jax.ShapeDtypeStruct((L, D), F32)], [row], comm=comm)
    return _with_comm(res, comm, lambda o: o[0])


def _rope_tables(L):
    rows = jnp.arange(L, dtype=F32)
    pos = jnp.where(rows < BLK, rows, rows - (BLK - N_META))
    inv_r = ROPE_THETA ** (-jnp.arange(0, HD, 2, dtype=F32) / HD)
    ang_r = pos[:, None] * inv_r[None, :]
    cr = jnp.concatenate([jnp.cos(ang_r), jnp.cos(ang_r)], axis=1)
    sr = jnp.concatenate([-jnp.sin(ang_r), jnp.sin(ang_r)], axis=1)
    inv_m = ROPE_THETA ** (-jnp.arange(0, ROPE, 2, dtype=F32) / ROPE)
    ang_m = pos[:, None] * inv_m[None, :]
    z32 = jnp.zeros((L, ROPE // 2), F32)
    z64 = jnp.zeros((L, HD - ROPE), F32)
    cm = jnp.concatenate([jnp.cos(ang_m), jnp.cos(ang_m), z64], axis=1)
    sa = jnp.concatenate([-jnp.sin(ang_m), z32, z64], axis=1)
    sb = jnp.concatenate([z32, jnp.sin(ang_m), z64], axis=1)
    return cr, sr, cm, sa, sb


def _rope_ret(x, cr, sr):
    return x * cr + pltpu.roll(x, HD // 2, 1) * sr


def _rope_ret_t(d, cr, sr):
    return d * cr + pltpu.roll(d * sr, HD // 2, 1)


def _rope_mla(x, cm, sa, sb):
    return x * cm + pltpu.roll(x, HD - ROPE // 2, 1) * sa + pltpu.roll(x, ROPE // 2, 1) * sb


def _rope_mla_t(d, cm, sa, sb):
    return d * cm + pltpu.roll(d * sa, ROPE // 2, 1) + pltpu.roll(d * sb, HD - ROPE // 2, 1)


C_RQ, C_RK, C_RV, C_RG = 0, HEADS * HD, 2 * HEADS * HD, 3 * HEADS * HD
C_CQ = 4 * HEADS * HD
C_CKV = C_CQ + Q_RANK
C_KR = C_CKV + KV_RANK
RET_K_SCALE = HD ** -0.5


def _prep(proj, tabs, qn, kvn):
    L = proj.shape[0]
    tr = _tile(L, 256)
    W = HEADS * HD

    def body(p_ref, cr_ref, sr_ref, cm_ref, sa_ref, sb_ref, qn_ref, kvn_ref, q_ref, k_ref, v_ref, cq_ref, ckv_ref,
             kr_ref):
        cr, sr = cr_ref[...], sr_ref[...]
        for h in range(HEADS):
            sl = slice(h * HD, (h + 1) * HD)
            q_ref[:, sl] = _rope_ret(p_ref[:, C_RQ + h * HD:C_RQ + (h + 1) * HD], cr, sr).astype(BF16)
            k_ref[:, sl] = (_rope_ret(p_ref[:, C_RK + h * HD:C_RK + (h + 1) * HD], cr, sr)
                            * RET_K_SCALE).astype(BF16)
        v_ref[...] = p_ref[:, C_RV:C_RV + W].astype(BF16)
        cq = p_ref[:, C_CQ:C_CQ + Q_RANK]
        cq_ref[...] = (cq * lax.rsqrt(jnp.mean(cq * cq, axis=-1, keepdims=True) + EPS) * qn_ref[...]).astype(BF16)
        ckv = p_ref[:, C_CKV:C_CKV + KV_RANK]
        ckv_ref[...] = (ckv * lax.rsqrt(jnp.mean(ckv * ckv, axis=-1, keepdims=True) + EPS)
                        * kvn_ref[...]).astype(BF16)
        kr_ref[...] = _rope_mla(p_ref[:, C_KR:C_KR + HD], cm_ref[...], sa_ref[...], sb_ref[...]).astype(BF16)

    row = lambda w: pl.BlockSpec((tr, w), lambda i: (i, 0))
    vec = lambda w: pl.BlockSpec((1, w), lambda i: (0, 0))
    return pl.pallas_call(
        body, name="mix_prep", grid=(L // tr,),
        out_shape=[jax.ShapeDtypeStruct((L, W), BF16)] * 3 + [jax.ShapeDtypeStruct((L, Q_RANK), BF16),
                                                              jax.ShapeDtypeStruct((L, KV_RANK), BF16),
                                                              jax.ShapeDtypeStruct((L, HD), BF16)],
        in_specs=[row(D_INP)] + [row(HD)] * 5 + [vec(Q_RANK), vec(KV_RANK)],
        out_specs=[row(W)] * 3 + [row(Q_RANK), row(KV_RANK), row(HD)],
        compiler_params=_params(("parallel",)),
    )(proj, *tabs, qn, kvn)


def _prep_bwd(proj, dq, dk, dv, drg, dcqn, dckvn, dkr8, tabs, qn, kvn):
    L = proj.shape[0]
    tr = _tile(L, 192)
    W = HEADS * HD

    def body(p_ref, dq_ref, dk_ref, dv_ref, drg_ref, dcq_ref, dckv_ref, dkr_ref, cr_ref, sr_ref, cm_ref, sa_ref,
             sb_ref, qn_ref, kvn_ref, dp_ref, dqn_ref, dkvn_ref):
        cr, sr = cr_ref[...], sr_ref[...]
        dkr = None
        for h in range(HEADS):
            sl = slice(h * HD, (h + 1) * HD)
            dp_ref[:, C_RQ + h * HD:C_RQ + (h + 1) * HD] = _rope_ret_t(dq_ref[:, sl], cr, sr).astype(BF16)
            dp_ref[:, C_RK + h * HD:C_RK + (h + 1) * HD] = (_rope_ret_t(dk_ref[:, sl], cr, sr)
                                                            * RET_K_SCALE).astype(BF16)
            part = dkr_ref[:, sl]
            dkr = part if dkr is None else dkr + part
        dp_ref[:, C_RV:C_RV + W] = dv_ref[...].astype(BF16)
        dp_ref[:, C_RG:C_RG + W] = drg_ref[...].astype(BF16)
        dcq, dqn = _norm_bwd_math(p_ref[:, C_CQ:C_CQ + Q_RANK], qn_ref[...], dcq_ref[...])
        dp_ref[:, C_CQ:C_CQ + Q_RANK] = dcq.astype(BF16)
        dckv, dkvn = _norm_bwd_math(p_ref[:, C_CKV:C_CKV + KV_RANK], kvn_ref[...], dckv_ref[...])
        dp_ref[:, C_CKV:C_CKV + KV_RANK] = dckv.astype(BF16)
        dp_ref[:, C_KR:C_KR + HD] = _rope_mla_t(dkr, cm_ref[...], sa_ref[...], sb_ref[...]).astype(BF16)

        @pl.when(pl.program_id(0) == 0)
        def _():
            dqn_ref[...] = jnp.zeros_like(dqn_ref)
            dkvn_ref[...] = jnp.zeros_like(dkvn_ref)

        dqn_ref[...] += dqn
        dkvn_ref[...] += dkvn

    row = lambda w: pl.BlockSpec((tr, w), lambda i: (i, 0))
    vec = lambda w: pl.BlockSpec((1, w), lambda i: (0, 0))
    return pl.pallas_call(
        body, name="mix_prep_bwd", grid=(L // tr,),
        out_shape=[jax.ShapeDtypeStruct((L, D_INP), BF16), jax.ShapeDtypeStruct((1, Q_RANK), F32),
                   jax.ShapeDtypeStruct((1, KV_RANK), F32)],
        in_specs=[row(D_INP)] + [row(W)] * 4 + [row(Q_RANK), row(KV_RANK), row(W)] + [row(HD)] * 5
                 + [vec(Q_RANK), vec(KV_RANK)],
        out_specs=[row(D_INP), vec(Q_RANK), vec(KV_RANK)],
        compiler_params=_params(("arbitrary",)),
    )(proj, dq, dk, dv, drg, dcqn, dckvn, dkr8, *tabs, qn, kvn)


def _post(o_ret, proj, gn):
    L, W = o_ret.shape
    tr = _tile(L, 384)

    def body(o_ref, rg_ref, gn_ref, out_ref):
        for h in range(HEADS):
            sl = slice(h * HD, (h + 1) * HD)
            o = o_ref[:, sl]
            rg = rg_ref[:, sl]
            n = o * lax.rsqrt(jnp.mean(o * o, axis=-1, keepdims=True) + EPS)
            out_ref[:, sl] = (n * gn_ref[:, sl] * (rg * _sigmoid(rg))).astype(BF16)

    row = pl.BlockSpec((tr, W), lambda i: (i, 0))
    return pl.pallas_call(
        body, name="ret_post", grid=(L // tr,), out_shape=jax.ShapeDtypeStruct((L, W), BF16),
        in_specs=[row, pl.BlockSpec((tr, W), lambda i: (i, C_RG // W)), pl.BlockSpec((1, W), lambda i: (0, 0))],
        out_specs=row, compiler_params=_params(("parallel",)),
    )(o_ret, proj, gn)


def _post_bwd(o_ret, proj, gn, dcat):
    L, W = o_ret.shape
    tr = _tile(L, 384)

    def body(o_ref, rg_ref, gn_ref, d_ref, do_ref, drg_ref, dgn_ref):
        @pl.when(pl.program_id(0) == 0)
        def _():
            dgn_ref[...] = jnp.zeros_like(dgn_ref)

        for h in range(HEADS):
            sl = slice(h * HD, (h + 1) * HD)
            o = o_ref[:, sl]
            rg = rg_ref[:, sl]
            d = d_ref[:, sl].astype(F32)
            gw = gn_ref[:, sl]
            r = lax.rsqrt(jnp.mean(o * o, axis=-1, keepdims=True) + EPS)
            n = o * r
            sg = _sigmoid(rg)
            si = rg * sg
            dn = d * gw * si
            dgn_ref[:, sl] += jnp.sum(d * n * si, axis=0, keepdims=True)
            drg_ref[:, sl] = d * n * gw * sg * (1.0 + rg * (1.0 - sg))
            do_ref[:, sl] = (r * (dn - o * (r * r) * jnp.mean(dn * o, axis=-1, keepdims=True))).astype(BF16)

    row = pl.BlockSpec((tr, W), lambda i: (i, 0))
    vec = pl.BlockSpec((1, W), lambda i: (0, 0))
    return pl.pallas_call(
        body, name="ret_post_bwd", grid=(L // tr,),
        out_shape=[jax.ShapeDtypeStruct((L, W), BF16), jax.ShapeDtypeStruct((L, W), F32),
                   jax.ShapeDtypeStruct((1, W), F32)],
        in_specs=[row, pl.BlockSpec((tr, W), lambda i: (i, C_RG // W)), vec, row],
        out_specs=[row, row, vec], compiler_params=_params(("arbitrary",)),
    )(o_ret, proj, gn, dcat)


RET_HEADS_PER_STEP = 4


def _lin_attn(name, q, k, v, lg, reverse):
    L, W = q.shape
    nc = L // BLK - 1
    G = RET_HEADS_PER_STEP

    def body(q_ref, k_ref, v_ref, lg_ref, o_ref, s_ref):
        n = lax.broadcasted_iota(jnp.int32, (BLK, BLK), 0).astype(F32)
        m = lax.broadcasted_iota(jnp.int32, (BLK, BLK), 1).astype(F32)
        dist = (m - n) if reverse else (n - m)
        consts = []
        for g in range(G):
            lgv = lg_ref[g, 0:1, :]
            dmask = jnp.where(dist >= 0, jnp.exp(lgv * jnp.maximum(dist, 0.0)), 0.0)
            c = dict(dmask=dmask, dmask0=jnp.where((n < N_META) & (m < N_META), dmask, 0.0),
                     gl=jnp.exp(lgv * float(BLK)))
            if reverse:
                c.update(inter=jnp.exp(lgv * (float(BLK) - n)), upd=jnp.exp(lgv * n),
                         inter0=jnp.where(n < N_META, jnp.exp(lgv * jnp.maximum(float(N_META) - n, 0.0)), 0.0))
            else:
                c.update(inter=jnp.exp(lgv * (n + 1.0)), upd=jnp.exp(lgv * (float(BLK) - 1.0 - n)),
                         upd0=jnp.where(n < N_META, jnp.exp(lgv * jnp.maximum(float(N_META) - 1.0 - n, 0.0)), 0.0))
            consts.append(c)

        def chunk(c):
            rows = pl.ds(pl.multiple_of(c * BLK, BLK), BLK)
            for g in range(G):
                cols = slice(g * HD, (g + 1) * HD)
                cg = consts[g]
                qc, kc, vc = q_ref[rows, cols], k_ref[rows, cols], v_ref[rows, cols]
                a = _dot(qc, kc, 'nt') * cg['dmask']
                o_ref[rows, cols] = (_dot(a.astype(BF16), vc, 'nn')
                                     + _dot(qc, s_ref[g].astype(BF16), 'nn') * cg['inter'])
                s_ref[g] = s_ref[g] * cg['gl'] + _dot((kc.astype(F32) * cg['upd']).astype(BF16), vc, 'tn')

        def first_chunk(with_state):
            for g in range(G):
                cols = slice(g * HD, (g + 1) * HD)
                cg = consts[g]
                q0, k0, v0 = q_ref[0:BLK, cols], k_ref[0:BLK, cols], v_ref[0:BLK, cols]
                o0 = _dot((_dot(q0, k0, 'nt') * cg['dmask0']).astype(BF16), v0, 'nn')
                if with_state:
                    o0 = o0 + _dot(q0, s_ref[g].astype(BF16), 'nn') * cg['inter0']
                else:
                    s_ref[g] = _dot((k0.astype(F32) * cg['upd0']).astype(BF16), v0, 'tn')
                o_ref[0:BLK, cols] = o0

        if reverse:
            s_ref[...] = jnp.zeros_like(s_ref)

            def step(t, carry):
                chunk(nc - t)
                return carry

            lax.fori_loop(0, nc, step, 0)
            first_chunk(True)
        else:
            first_chunk(False)

            def step(t, carry):
                chunk(t + 1)
                return carry

            lax.fori_loop(0, nc, step, 0)

    col = pl.BlockSpec((L, G * HD), lambda h: (0, h))
    return pl.pallas_call(
        body, name=name, grid=(HEADS // G,), out_shape=jax.ShapeDtypeStruct((L, W), F32),
        in_specs=[col, col, col, pl.BlockSpec((G, 8, HD), lambda h: (h, 0, 0))], out_specs=col,
        scratch_shapes=[pltpu.VMEM((G, HD, HD), F32)], compiler_params=_params(("parallel",)),
    )(q, k, v, lg)


ATT_SCALE = (HD + ROPE) ** -0.5
NEG = -1e30


ATT_TILE = 384
ATT_HEADS_PER_STEP = 2


def _att_valid(T, row0, col0):
    r = lax.broadcasted_iota(jnp.int32, (T, T), 0) + row0
    c = lax.broadcasted_iota(jnp.int32, (T, T), 1) + col0
    return (c <= r) & ((c < N_META) | (c >= BLK))


def _attn_fwd(qm, kn, krr, vm, comm=None):
    L = qm.shape[0]
    W = HEADS * HD
    T = _tile(L, ATT_TILE, BLK)
    nb = L // T
    G = ATT_HEADS_PER_STEP
    n_cm = comm.n if comm is not None else 0

    def body(*refs):
        q_ref, kn_ref, kr_ref, v_ref = refs[:4]
        o_ref, lse_ref = refs[4 + n_cm:6 + n_cm]
        m_sc, l_sc, acc_sc = refs[6 + 2 * n_cm:9 + 2 * n_cm]
        if comm is not None:
            cm_refs = (refs[4:4 + n_cm], refs[6 + n_cm:6 + 2 * n_cm], refs[9 + 2 * n_cm:])
            first, last = _grid_edges((HEADS // G, nb))

            @pl.when(first)
            def _():
                comm.start(*cm_refs)

        i = pl.program_id(1)
        m_sc[...] = jnp.full_like(m_sc, NEG)
        l_sc[...] = jnp.zeros_like(l_sc)
        acc_sc[...] = jnp.zeros_like(acc_sc)

        def tile(j, masked):
            rows = pl.ds(pl.multiple_of(j * T, T), T)
            kr = kr_ref[rows, :]
            valid = _att_valid(T, i * T, j * T) if masked else None
            for g in range(G):
                k = jnp.concatenate([kn_ref[rows, g * HD:(g + 1) * HD], kr], axis=1)
                s = _dot(q_ref[:, g * QH:(g + 1) * QH], k, 'nt') * ATT_SCALE
                if masked:
                    s = jnp.where(valid, s, NEG)
                m_prev = m_sc[g]
                m_new = jnp.maximum(m_prev, jnp.max(s, axis=-1, keepdims=True))
                p = jnp.exp(s - m_new)
                alpha = jnp.exp(m_prev - m_new)
                l_sc[g] = alpha * l_sc[g] + jnp.sum(p, axis=-1, keepdims=True)
                acc_sc[g] = alpha * acc_sc[g] + _dot(p.astype(BF16), v_ref[rows, g * HD:(g + 1) * HD], 'nn')
                m_sc[g] = m_new

        tile(0, True)

        def mid(j, carry):
            tile(j, False)
            return carry

        lax.fori_loop(1, i, mid, 0)

        @pl.when(i > 0)
        def _():
            tile(i, True)

        for g in range(G):
            l = l_sc[g]
            o_ref[:, g * HD:(g + 1) * HD] = (acc_sc[g] / l).astype(o_ref.dtype)
            lse_ref[g] = jnp.broadcast_to(m_sc[g] + jnp.log(l), (T, HD))

        if comm is not None:
            @pl.when(last)
            def _():
                comm.finish(*cm_refs)

    cm_specs = comm.specs if comm is not None else []
    res = pl.pallas_call(
        body, name="attn_fwd", grid=(HEADS // G, nb),
        out_shape=[jax.ShapeDtypeStruct((L, W), BF16), jax.ShapeDtypeStruct((HEADS, L, HD), F32)]
        + (comm.out_shapes if comm is not None else []),
        in_specs=[pl.BlockSpec((T, G * QH), lambda h, i: (i, h)), pl.BlockSpec((L, G * HD), lambda h, i: (0, h)),
                  pl.BlockSpec((L, HD), lambda h, i: (0, 0)), pl.BlockSpec((L, G * HD), lambda h, i: (0, h))]
        + cm_specs,
        out_specs=[pl.BlockSpec((T, G * HD), lambda h, i: (i, h)),
                   pl.BlockSpec((G, T, HD), lambda h, i: (h, i, 0))] + cm_specs,
        scratch_shapes=[pltpu.VMEM((G, T, 1), F32), pltpu.VMEM((G, T, 1), F32), pltpu.VMEM((G, T, HD), F32)]
        + (comm.scratch if comm is not None else []),
        compiler_params=_params(("arbitrary", "arbitrary")),
    )(qm, kn, krr, vm, *(comm.arrays if comm is not None else []))
    return res[:2], res[2:]


def _attn_bwd(qm, kn, krr, vm, o, dcat, lse, comm=None):
    L = qm.shape[0]
    W = HEADS * HD
    T = _tile(L, ATT_TILE, BLK)
    nb = L // T
    n_cm = comm.n if comm is not None else 0

    def body(*refs):
        q_ref, kn_ref, kr_ref, v_ref, o_ref, do_ref, lse_ref = refs[:7]
        dq_ref, dkn_ref, dkr_ref, dv_ref = refs[7 + n_cm:11 + n_cm]
        dl_sc, dk_sc, dv_sc = refs[11 + 2 * n_cm:14 + 2 * n_cm]
        if comm is not None:
            cm_refs = (refs[7:7 + n_cm], refs[11 + n_cm:11 + 2 * n_cm], refs[14 + 2 * n_cm:])
            first, last = _grid_edges((HEADS, nb))

            @pl.when(first)
            def _():
                comm.start(*cm_refs)

        j = pl.program_id(1)

        @pl.when(j == 0)
        def _():
            dq_ref[...] = jnp.zeros_like(dq_ref)

            def rowsum(t, carry):
                rows = pl.ds(pl.multiple_of(t * T, T), T)
                dl_sc[rows, :] = jnp.sum(do_ref[rows, :].astype(F32) * o_ref[rows, :].astype(F32), axis=-1,
                                         keepdims=True)
                return carry

            lax.fori_loop(0, nb, rowsum, 0)

        k = jnp.concatenate([kn_ref[...], kr_ref[...]], axis=1)
        v = v_ref[...]
        dk_sc[...] = jnp.zeros_like(dk_sc)
        dv_sc[...] = jnp.zeros_like(dv_sc)

        def tile(i, masked):
            rows = pl.ds(pl.multiple_of(i * T, T), T)
            q = q_ref[rows, :]
            do = do_ref[rows, :]
            s = _dot(q, k, 'nt') * ATT_SCALE
            if masked:
                s = jnp.where(_att_valid(T, i * T, j * T), s, NEG)
            p = jnp.exp(s - lse_ref[rows, 0:1])
            dv_sc[...] += _dot(p.astype(BF16), do, 'tn')
            ds = (p * (_dot(do, v, 'nt') - dl_sc[rows, :]) * ATT_SCALE).astype(BF16)
            dk_sc[...] += _dot(ds, q, 'tn')
            dq_ref[rows, :] += _dot(ds, k, 'nn')

        tile(j, True)

        def rest(masked):
            def step(i, carry):
                tile(i, masked)
                return carry
            lax.fori_loop(j + 1, nb, step, 0)

        @pl.when(j == 0)
        def _():
            rest(True)

        @pl.when(j > 0)
        def _():
            rest(False)

        dk = dk_sc[...]
        dkn_ref[...] = dk[:, 0:HD].astype(BF16)
        dkr_ref[...] = dk[:, HD:QH]
        dv_ref[...] = dv_sc[...].astype(BF16)

        if comm is not None:
            @pl.when(last)
            def _():
                comm.finish(*cm_refs)

    blk = pl.BlockSpec((T, HD), lambda h, j: (j, h))
    cm_specs = comm.specs if comm is not None else []
    res = pl.pallas_call(
        body, name="attn_bwd", grid=(HEADS, nb),
        out_shape=[jax.ShapeDtypeStruct((L, HEADS * QH), F32), jax.ShapeDtypeStruct((L, W), BF16),
                   jax.ShapeDtypeStruct((L, W), F32), jax.ShapeDtypeStruct((L, W), BF16)]
        + (comm.out_shapes if comm is not None else []),
        in_specs=[pl.BlockSpec((L, QH), lambda h, j: (0, h)), blk, pl.BlockSpec((T, HD), lambda h, j: (j, 0)), blk,
                  pl.BlockSpec((L, HD), lambda h, j: (0, h)), pl.BlockSpec((L, HD), lambda h, j: (0, HEADS + h)),
                  pl.BlockSpec((None, L, HD), lambda h, j: (h, 0, 0))] + cm_specs,
        out_specs=[pl.BlockSpec((L, QH), lambda h, j: (0, h)), blk, blk, blk] + cm_specs,
        scratch_shapes=[pltpu.VMEM((L, 1), F32), pltpu.VMEM((T, QH), F32), pltpu.VMEM((T, HD), F32)]
        + (comm.scratch if comm is not None else []),
        compiler_params=_params(("arbitrary", "arbitrary")),
    )(qm, kn, krr, vm, o, dcat, lse, *(comm.arrays if comm is not None else []))
    return res[:4], res[4:]


def _unrope_q(dqm, tabs_m):
    L, W = dqm.shape
    tr = _tile(L, 384)

    def body(d_ref, cm_ref, sa_ref, sb_ref, out_ref):
        cm, sa, sb = cm_ref[...], sa_ref[...], sb_ref[...]
        for h in range(HEADS):
            out_ref[:, h * QH:h * QH + HD] = d_ref[:, h * QH:h * QH + HD].astype(BF16)
            out_ref[:, h * QH + HD:(h + 1) * QH] = _rope_mla_t(d_ref[:, h * QH + HD:(h + 1) * QH], cm, sa,
                                                               sb).astype(BF16)

    row = pl.BlockSpec((tr, W), lambda i: (i, 0))
    tab = pl.BlockSpec((tr, HD), lambda i: (i, 0))
    return pl.pallas_call(
        body, name="unrope_q", grid=(L // tr,), out_shape=jax.ShapeDtypeStruct((L, W), BF16),
        in_specs=[row, tab, tab, tab], out_specs=row, compiler_params=_params(("parallel",)),
    )(dqm, *tabs_m)


def _q_up(cqn, wuq_p, tabs_m):
    L = cqn.shape[0]
    tm = _tile(L, 704)

    def ep(acc, cm, sa, sb):
        parts = []
        for h in range(HEADS):
            parts.append(acc[:, h * QH:h * QH + HD])
            parts.append(_rope_mla(acc[:, h * QH + HD:(h + 1) * QH], cm, sa, sb))
        return (jnp.concatenate(parts, axis=1),)

    tab = pl.BlockSpec((tm, HD), lambda i, j: (i, 0))
    return _mm("mla_q_up", (L // tm, 1), ("parallel", "parallel"), None,
               [cqn, wuq_p], [pl.BlockSpec((tm, Q_RANK), lambda i, j: (i, 0)),
                              pl.BlockSpec((Q_RANK, HEADS * QH), lambda i, j: (0, 0))],
               [(0, 1, 'nn', 0)], [(tm, HEADS * QH)], list(tabs_m), [tab] * 3, ep,
               [jax.ShapeDtypeStruct((L, HEADS * QH), BF16)], [pl.BlockSpec((tm, HEADS * QH), lambda i, j: (i, 0))])[0]


def _mix_out(cat, w_out, h_in, post):
    L, K = cat.shape
    D = w_out.shape[1]
    tm, tk = _tile(L, 384), _tile(K, 512, 128)
    row = pl.BlockSpec((tm, D), lambda i, k: (i, 0))
    return _mm("mix_out", (L // tm, K // tk), ("parallel", "arbitrary"), 1,
               [cat, w_out], [pl.BlockSpec((tm, tk), lambda i, k: (i, k)), pl.BlockSpec((tk, D), lambda i, k: (k, 0))],
               [(0, 1, 'nn', 0)], [(tm, D)], [h_in, post], [row, pl.BlockSpec((1, D), lambda i, k: (0, 0))],
               _resnorm_epilogue(1.0), [jax.ShapeDtypeStruct((L, D), F32)] * 2, [row, row])


def _adam_math(w, g, m, v):
    m = ADAM_B1 * m + (1.0 - ADAM_B1) * g
    v = ADAM_B2 * v + (1.0 - ADAM_B2) * (g * g)
    m_hat = m / (1.0 - ADAM_B1 ** ADAM_STEP)
    v_hat = v / (1.0 - ADAM_B2 ** ADAM_STEP)
    delta = -ADAM_LR * (m_hat / (jnp.sqrt(v_hat) + ADAM_EPS) + ADAM_WD * w)
    return delta, m, v


def _adam(name, w, m, v, g_slots=None, g=None):
    R, C = w.shape
    tr = _tile(R, 256, 8)
    from_slots = g_slots is not None

    def body(w_ref, m_ref, v_ref, g_ref, go_ref, d_ref, mo_ref, vo_ref):
        if from_slots:
            grad = g_ref[0].astype(F32)
            for s in range(1, N_DEV):
                grad = grad + g_ref[s].astype(F32)
        else:
            grad = g_ref[...]
        delta, mn, vn = _adam_math(w_ref[...], grad, m_ref[...], v_ref[...])
        go_ref[...] = grad
        d_ref[...] = delta
        mo_ref[...] = mn
        vo_ref[...] = vn

    row = pl.BlockSpec((tr, C), lambda i: (i, 0))
    gspec = pl.BlockSpec((N_DEV, tr, C), lambda i: (0, i, 0)) if from_slots else row
    return pl.pallas_call(
        body, name=name, grid=(R // tr,), out_shape=[jax.ShapeDtypeStruct((R, C), F32)] * 4,
        in_specs=[row, row, row, gspec], out_specs=[row] * 4, compiler_params=_params(("parallel",)),
    )(w, m, v, g_slots if from_slots else g)


def _unblock(gathered):
    n, r, c = gathered.shape
    return jnp.transpose(gathered, (1, 0, 2)).reshape(r, n * c)


def _reblock(full, c):
    r = full.shape[0]
    return jnp.transpose(full[:, :N_DEV * c].reshape(r, N_DEV, c), (1, 0, 2))


def _step(x, target, w, mom, vel):
    S, D = x.shape[1], x.shape[2]
    L = S + BLK
    sq = lambda a: a.reshape(a.shape[1:]) if a.ndim == 3 else a
    p = {n: sq(w[n]) for n in WEIGHTS if n != 'meta_tokens'}
    gather = lambda names: _Exchange([p[n].astype(BF16) for n in names], False)
    scatter = lambda blocks: _Exchange(blocks, True)
    in_s, uq_s = p['w_in'].shape[1], p['mla_w_uq'].shape[1]
    tabs = _rope_tables(L)
    tabs_m = tabs[2:]
    lg = jnp.broadcast_to(jnp.log(1.0 - 2.0 ** (-5.0 - jnp.arange(HEADS, dtype=F32)))[:, None, None], (HEADS, 8, HD))
    R = {}

    wg1, wu1, meta = _exchange("gather_first", [p['ffn1_w_gate'].astype(BF16), p['ffn1_w_up'].astype(BF16),
                                                w['meta_tokens']], False)
    h0 = jnp.concatenate([_unblock(meta), jnp.zeros((BLK - N_META, D), F32), x[0]], axis=0)
    a1 = _norm_fwd(h0, p['ffn1_pre_norm'])
    (g1, u1, hid1), (wd1,) = _ffn_up(a1, wg1, wu1, comm=gather(['ffn1_w_down']))
    (f1, h1), (w_in_g,) = _ffn_down(hid1, wd1, h0, p['ffn1_post_norm'], comm=gather(['w_in']))

    w_in_full = _unblock(w_in_g)
    w_in = jnp.pad(w_in_full, ((0, 0), (0, D_INP - w_in_full.shape[1])))
    um = _norm_fwd(h1, p['mix_pre_norm'])
    proj, (uq_g, uk_g, uv_g, wout_g) = _mm_nn("mix_in", um, w_in, F32,
                                              comm=gather(['mla_w_uq', 'mla_w_uk', 'mla_w_uv', 'w_out']))
    wuq = jnp.pad(_unblock(uq_g).reshape(Q_RANK, HEADS, HD + ROPE),
                  ((0, 0), (0, 0), (0, QH - HD - ROPE))).reshape(Q_RANK, HEADS * QH)
    wuk, wuv, w_out = _unblock(uk_g), _unblock(uv_g), wout_g.reshape(-1, D)
    qr, kr, vr, cqn, ckvn, krr = _prep(proj, tabs, p['mla_q_norm'], p['mla_kv_norm'])
    qm = _q_up(cqn, wuq, tabs_m)
    kn = _mm_nn("mla_k_up", ckvn, wuk, BF16)
    vm = _mm_nn("mla_v_up", ckvn, wuv, BF16)
    (o_mla, lse), (wg2, wu2) = _attn_fwd(qm, kn, krr, vm, comm=gather(['ffn2_w_gate', 'ffn2_w_up']))
    o_ret = _lin_attn("ret_fwd", qr, kr, vr, lg, False)
    ret = _post(o_ret, proj, p['ret_group_norm'])
    cat = jnp.concatenate([ret, o_mla], axis=1)
    m, h2 = _mix_out(cat, w_out, h1, p['mix_post_norm'])

    a2 = _norm_fwd(h2, p['ffn2_pre_norm'])
    (g2, u2, hid2), (wd2,) = _ffn_up(a2, wg2, wu2, comm=gather(['ffn2_w_down']))
    f2, h3 = _ffn_down(hid2, wd2, h2, p['ffn2_post_norm'])
    dh3, loss_blk = _loss(h3, target[0])

    dsmall = {}
    df2, dsmall['ffn2_post_norm'] = _norm_bwd(f2, p['ffn2_post_norm'], dh3, None, 0.5, BF16)
    dg2, du2 = _ffn_dhid(df2, wd2, g2, u2)
    dwd2 = _ffn_dwd(hid2, df2)
    (dwg2, dwu2), (R['ffn2_w_down'],) = _ffn_dwgu(a2, dg2, du2, comm=scatter([dwd2]))
    da2, (R['ffn2_w_gate'],) = _ffn_da(dg2, du2, wg2, wu2, comm=scatter([dwg2]))
    dh2, dsmall['ffn2_pre_norm'] = _norm_bwd(h2, p['ffn2_pre_norm'], da2, dh3, 1.0, F32)

    dm, dsmall['mix_post_norm'] = _norm_bwd(m, p['mix_post_norm'], dh2, None, 1.0, BF16)
    dcat = _mm_nt("mix_dcat", [(dm, w_out)], BF16)
    dwout = _mm_tn("mix_dwout", cat, [dm])[0]
    do_ret, drg, dsmall['ret_group_norm'] = _post_bwd(o_ret, proj, p['ret_group_norm'], dcat)
    dqr = _lin_attn("ret_dq", do_ret, vr, kr, lg, False)
    dkr = _lin_attn("ret_dk", vr, do_ret, qr, lg, True)
    dvr = _lin_attn("ret_dv", kr, qr, do_ret, lg, True)
    (dqm, dkn, dkr8, dvm), (R['ffn2_w_up'], R['w_out']) = _attn_bwd(
        qm, kn, krr, vm, o_mla, dcat, lse, comm=scatter([dwu2, dwout.reshape(N_DEV, -1, D)]))
    dqp = _unrope_q(dqm, tabs_m)
    dwuq = _mm_tn("mla_dwuq", cqn, [dqp])[0]
    dcqn = _mm_nt("mla_dcq", [(dqp, wuq)], F32)
    dwuk, dwuv = _mm_tn("mla_dwukv", ckvn, [dkn, dvm])
    dckvn = _mm_nt("mla_dckv", [(dkn, wuk), (dvm, wuv)], F32)
    dproj, dsmall['mla_q_norm'], dsmall['mla_kv_norm'] = _prep_bwd(
        proj, dqr, dkr, dvr, drg, dcqn, dckvn, dkr8, tabs, p['mla_q_norm'], p['mla_kv_norm'])
    dwuq_b = _reblock(dwuq.reshape(Q_RANK, HEADS, QH)[:, :, :HD + ROPE].reshape(Q_RANK, HEADS * (HD + ROPE)), uq_s)
    (dwin,), (R['mla_w_uq'], R['mla_w_uk'], R['mla_w_uv']) = _mm_tn(
        "mix_dwin", um, [dproj], comm=scatter([dwuq_b, _reblock(dwuk, p['mla_w_uk'].shape[1]),
                                               _reblock(dwuv, p['mla_w_uv'].shape[1])]))
    dwin_b = _reblock(dwin, in_s)
    half = D // 2
    dum, (r_win_a,) = _mm_nt("mix_du", [(dproj, w_in)], F32, comm=scatter([dwin_b[:, :half]]))
    dh1, dsmall['mix_pre_norm'] = _norm_bwd(h1, p['mix_pre_norm'], dum, dh2, 1.0, F32)

    df1, dsmall['ffn1_post_norm'] = _norm_bwd(f1, p['ffn1_post_norm'], dh1, None, 0.5, BF16)
    (dg1, du1), (r_win_b,) = _ffn_dhid(df1, wd1, g1, u1, comm=scatter([dwin_b[:, half:]]))
    R['w_in'] = jnp.concatenate([r_win_a, r_win_b], axis=1)
    dwd1 = _ffn_dwd(hid1, df1)
    (dwg1, dwu1), (R['ffn1_w_down'],) = _ffn_dwgu(a1, dg1, du1, comm=scatter([dwd1]))
    da1, (R['ffn1_w_gate'],) = _ffn_da(dg1, du1, wg1, wu1, comm=scatter([dwg1]))
    dh0, dsmall['ffn1_pre_norm'] = _norm_bwd(h0, p['ffn1_pre_norm'], da1, dh1, 1.0, F32)
    R['ffn1_w_up'], = _exchange("scatter_last", [dwu1], True)

    def slab(a):
        a = a.reshape(-1, 128)
        return jnp.pad(a, ((0, (-a.shape[0]) % 8), (0, 0)))

    slab_rows = lambda n: -(-(p[n].shape[-1] // 128) // 8) * 8
    packed = jnp.concatenate([slab(dsmall[n]) for n in SMALL] + [slab(dh0[:N_META]), loss_blk], axis=0)
    red = _allreduce_small(packed)
    offs = sum(slab_rows(n) for n in SMALL)
    n_small = offs
    gmeta_full = red[offs:offs + N_META * D // 128].reshape(N_META, D)
    offs += N_META * D // 128
    loss = red[offs, 0]

    grad, delta, new_m, new_v = {}, {}, {}, {}
    for n in BIG:
        outs = _adam("adam_" + n, p[n], sq(mom[n]), sq(vel[n]), g_slots=R[n])
        grad[n], delta[n], new_m[n], new_v[n] = [o.reshape(w[n].shape) for o in outs]
    pack = lambda d: jnp.concatenate([slab(d[n]) for n in SMALL], axis=0)
    outs = _adam("adam_small", pack(w), pack(mom), pack(vel), g=red[:n_small])
    offs = 0
    for n in SMALL:
        r = p[n].shape[-1] // 128
        grad[n], delta[n], new_m[n], new_v[n] = [o[offs:offs + r].reshape(w[n].shape) for o in outs]
        offs += slab_rows(n)
    dev = 4 * lax.axis_index("x") + 2 * lax.axis_index("y") + lax.axis_index("c")
    mcols = w['meta_tokens'].shape[1]
    gmeta = lax.dynamic_slice(gmeta_full, (0, dev * mcols), (N_META, mcols))
    outs = _adam("adam_meta", w['meta_tokens'], mom['meta_tokens'], vel['meta_tokens'], g=gmeta)
    grad['meta_tokens'], delta['meta_tokens'], new_m['meta_tokens'], new_v['meta_tokens'] = outs

    return (loss, dh0[BLK:][None], *[grad[n] for n in WEIGHTS], *[delta[n] for n in WEIGHTS],
            *[new_m[n] for n in WEIGHTS], *[new_v[n] for n in WEIGHTS])


def kernel(x, meta_tokens, ffn1_pre_norm, ffn1_w_gate, ffn1_w_up, ffn1_w_down, ffn1_post_norm, mix_pre_norm, w_in, ret_group_norm, mla_q_norm, mla_w_uq, mla_kv_norm, mla_w_uk, mla_w_uv, w_out, mix_post_norm, ffn2_pre_norm, ffn2_w_gate, ffn2_w_up, ffn2_w_down, ffn2_post_norm, loss_target, m_meta_tokens, m_ffn1_pre_norm, m_ffn1_w_gate, m_ffn1_w_up, m_ffn1_w_down, m_ffn1_post_norm, m_mix_pre_norm, m_w_in, m_ret_group_norm, m_mla_q_norm, m_mla_w_uq, m_mla_kv_norm, m_mla_w_uk, m_mla_w_uv, m_w_out, m_mix_post_norm, m_ffn2_pre_norm, m_ffn2_w_gate, m_ffn2_w_up, m_ffn2_w_down, m_ffn2_post_norm, v_meta_tokens, v_ffn1_pre_norm, v_ffn1_w_gate, v_ffn1_w_up, v_ffn1_w_down, v_ffn1_post_norm, v_mix_pre_norm, v_w_in, v_ret_group_norm, v_mla_q_norm, v_mla_w_uq, v_mla_kv_norm, v_mla_w_uk, v_mla_w_uv, v_w_out, v_mix_post_norm, v_ffn2_pre_norm, v_ffn2_w_gate, v_ffn2_w_up, v_ffn2_w_down, v_ffn2_post_norm):
    w = dict(zip(WEIGHTS, (meta_tokens, ffn1_pre_norm, ffn1_w_gate, ffn1_w_up, ffn1_w_down, ffn1_post_norm,
                           mix_pre_norm, w_in, ret_group_norm, mla_q_norm, mla_w_uq, mla_kv_norm, mla_w_uk, mla_w_uv,
                           w_out, mix_post_norm, ffn2_pre_norm, ffn2_w_gate, ffn2_w_up, ffn2_w_down, ffn2_post_norm)))
    mom = dict(zip(WEIGHTS, (m_meta_tokens, m_ffn1_pre_norm, m_ffn1_w_gate, m_ffn1_w_up, m_ffn1_w_down,
                             m_ffn1_post_norm, m_mix_pre_norm, m_w_in, m_ret_group_norm, m_mla_q_norm, m_mla_w_uq,
                             m_mla_kv_norm, m_mla_w_uk, m_mla_w_uv, m_w_out, m_mix_post_norm, m_ffn2_pre_norm,
                             m_ffn2_w_gate, m_ffn2_w_up, m_ffn2_w_down, m_ffn2_post_norm)))
    vel = dict(zip(WEIGHTS, (v_meta_tokens, v_ffn1_pre_norm, v_ffn1_w_gate, v_ffn1_w_up, v_ffn1_w_down,
                             v_ffn1_post_norm, v_mix_pre_norm, v_w_in, v_ret_group_norm, v_mla_q_norm, v_mla_w_uq,
                             v_mla_kv_norm, v_mla_w_uk, v_mla_w_uv, v_w_out, v_mix_post_norm, v_ffn2_pre_norm,
                             v_ffn2_w_gate, v_ffn2_w_up, v_ffn2_w_down, v_ffn2_post_norm)))
    return _step(x, loss_target, w, mom, vel)
```

```python
import functools
import math

import jax
import jax.numpy as jnp
from jax import lax
from jax.experimental import pallas as pl
from jax.experimental.pallas import tpu as pltpu

N_DEV = 8
N_META = 16
BLK = 128
HEADS = 8
HD = 128
ROPE = 64
Q_RANK = 512
KV_RANK = 256
QH = 2 * HD
D_INP = 4 * HEADS * HD + Q_RANK + KV_RANK + BLK
ROPE_THETA = 10000.0
EPS = 1e-6
ADAM_LR = 0.001
ADAM_B1 = 0.9
ADAM_B2 = 0.999
ADAM_EPS = 1e-08
ADAM_WD = 0.01
ADAM_STEP = 10
V7X_VMEM_LIMIT = 48 * 1024 * 1024
MESH = pl.DeviceIdType.MESH
F32 = jnp.float32
BF16 = jnp.bfloat16

WEIGHTS = ['meta_tokens', 'ffn1_pre_norm', 'ffn1_w_gate', 'ffn1_w_up', 'ffn1_w_down', 'ffn1_post_norm',
           'mix_pre_norm', 'w_in', 'ret_group_norm', 'mla_q_norm', 'mla_w_uq', 'mla_kv_norm', 'mla_w_uk',
           'mla_w_uv', 'w_out', 'mix_post_norm', 'ffn2_pre_norm', 'ffn2_w_gate', 'ffn2_w_up', 'ffn2_w_down',
           'ffn2_post_norm']
SMALL = ['ffn1_pre_norm', 'ffn1_post_norm', 'mix_pre_norm', 'ret_group_norm', 'mla_q_norm', 'mla_kv_norm',
         'mix_post_norm', 'ffn2_pre_norm', 'ffn2_post_norm']
BIG = ['ffn1_w_gate', 'ffn1_w_up', 'ffn1_w_down', 'w_in', 'mla_w_uq', 'mla_w_uk', 'mla_w_uv', 'w_out',
       'ffn2_w_gate', 'ffn2_w_up', 'ffn2_w_down']

_DIMS = {'nn': (((1,), (0,)), ((), ())), 'nt': (((1,), (1,)), ((), ())), 'tn': (((0,), (0,)), ((), ()))}


def _tile(n, target, mult=16):
    best = None
    for t in range(mult, min(n, target) + 1, mult):
        if n % t == 0:
            best = t
    return best if best is not None else n


def _params(sem):
    return pltpu.CompilerParams(dimension_semantics=sem, vmem_limit_bytes=V7X_VMEM_LIMIT)


def _dot(a, b, dims):
    return lax.dot_general(a, b, _DIMS[dims], preferred_element_type=F32)


def _sigmoid(x):
    return 1.0 / (1.0 + jnp.exp(-x))


def _me_and_peers():
    x, y, c = lax.axis_index("x"), lax.axis_index("y"), lax.axis_index("c")

    def peer(j):
        px = 1 - x if (j >> 2) & 1 else x
        py = 1 - y if (j >> 1) & 1 else y
        pc = 1 - c if j & 1 else c
        return (px, py, pc), 4 * px + 2 * py + pc

    return 4 * x + 2 * y + c, peer


class _Exchange:
    def __init__(self, arrays, per_peer):
        self.arrays = list(arrays)
        self.per_peer = per_peer
        self.n = len(self.arrays)
        self.out_shapes = [jax.ShapeDtypeStruct((N_DEV,) + tuple(a.shape[1:] if per_peer else a.shape), a.dtype)
                           for a in self.arrays]
        self.specs = [pl.BlockSpec(memory_space=pl.ANY)] * self.n
        self.scratch = [pltpu.SemaphoreType.DMA((7 * self.n,)), pltpu.SemaphoreType.DMA((7 * self.n,)),
                        pltpu.SemaphoreType.DMA((self.n,))]

    def _copies(self, src, dst, sems):
        send_sems, recv_sems, local_sems = sems
        me, peer = _me_and_peers()
        sib, _ = peer(1)
        local, sends, recvs, passes = [], {}, {}, {}
        for k in range(self.n):
            own = src[k].at[me] if self.per_peer else src[k]
            local.append(pltpu.make_async_copy(own, dst[k].at[me], local_sems.at[k]))
            for j in range(1, N_DEV):
                pid, pidx = peer(j)
                out = src[k].at[pidx] if self.per_peer else src[k]
                sem = dict(send_sem=send_sems.at[k * 7 + j - 1], recv_sem=recv_sems.at[k * 7 + j - 1])
                recvs[k, j] = pltpu.make_async_remote_copy(src_ref=out, dst_ref=dst[k].at[pidx], device_id=pid,
                                                           device_id_type=MESH, **sem)
                if self.per_peer or j in (1, 2, 4, 6):
                    sends[k, j] = pltpu.make_async_remote_copy(src_ref=out, dst_ref=dst[k].at[me], device_id=pid,
                                                               device_id_type=MESH, **sem)
                else:
                    _, origin = peer(j ^ 1)
                    passes[k, j ^ 1] = pltpu.make_async_remote_copy(
                        src_ref=dst[k].at[origin], dst_ref=dst[k].at[origin], device_id=sib, device_id_type=MESH, **sem)
        return local, sends, recvs, passes

    def start(self, src, dst, sems):
        local, sends, _, _ = self._copies(src, dst, sems)
        for cp in local + list(sends.values()):
            cp.start()

    def finish(self, src, dst, sems):
        local, sends, recvs, passes = self._copies(src, dst, sems)
        for key, cp in passes.items():
            recvs[key].wait_recv()
            cp.start()
        for key, cp in recvs.items():
            if key not in passes:
                cp.wait_recv()
        for cp in list(sends.values()) + list(passes.values()):
            cp.wait_send()
        for cp in local:
            cp.wait()


def _grid_edges(grid):
    first, last = None, None
    for a, n in enumerate(grid):
        f, l = pl.program_id(a) == 0, pl.program_id(a) == n - 1
        first = f if first is None else first & f
        last = l if last is None else last & l
    return first, last


def _exchange(name, arrays, per_peer):
    ex = _Exchange(arrays, per_peer)
    n = ex.n

    def body(*refs):
        ex.start(refs[:n], refs[n:2 * n], refs[2 * n:])
        ex.finish(refs[:n], refs[n:2 * n], refs[2 * n:])

    return pl.pallas_call(body, name=name, out_shape=ex.out_shapes, in_specs=ex.specs, out_specs=ex.specs,
                          scratch_shapes=ex.scratch)(*arrays)


def _allreduce_small(v):
    rows = v.shape[0]

    def body(v_ref, out_ref, buf, send_sems, recv_sems):
        me, peer = _me_and_peers()
        buf[pl.ds(me, 1)] = v_ref[...][None]
        sends = []
        for j in range(1, N_DEV):
            pid, _ = peer(j)
            cp = pltpu.make_async_remote_copy(src_ref=v_ref, dst_ref=buf.at[me], send_sem=send_sems.at[j - 1],
                                              recv_sem=recv_sems.at[j - 1], device_id=pid, device_id_type=MESH)
            cp.start()
            sends.append(cp)
        for j in range(1, N_DEV):
            pid, pidx = peer(j)
            pltpu.make_async_remote_copy(src_ref=v_ref, dst_ref=buf.at[pidx], send_sem=send_sems.at[j - 1],
                                         recv_sem=recv_sems.at[j - 1], device_id=pid,
                                         device_id_type=MESH).wait_recv()
        for cp in sends:
            cp.wait_send()
        acc = buf[0]
        for s in range(1, N_DEV):
            acc = acc + buf[s]
        out_ref[...] = acc

    vm = pl.BlockSpec(memory_space=pltpu.VMEM)
    return pl.pallas_call(
        body, name="allreduce_small", out_shape=jax.ShapeDtypeStruct(v.shape, F32),
        in_specs=[vm], out_specs=vm,
        scratch_shapes=[pltpu.VMEM((N_DEV, rows, 128), F32), pltpu.SemaphoreType.DMA((7,)),
                        pltpu.SemaphoreType.DMA((7,))],
    )(v)


def _mm(name, grid, sem, k_axis, ops, op_specs, pairs, acc_shapes, extras, extra_specs, epilogue, outs, out_specs,
        comm=None):
    n_op, n_ex, n_out = len(ops), len(extras), len(outs)
    nk = grid[k_axis] if k_axis is not None else 1
    n_acc = len(acc_shapes) if nk > 1 else 0
    n_cm = comm.n if comm is not None else 0

    def body(*refs):
        op_refs = refs[:n_op]
        ex_refs = refs[n_op:n_op + n_ex]
        n_in = n_op + n_ex + n_cm
        out_refs = refs[n_in:n_in + n_out]
        acc_refs = refs[n_in + n_out + n_cm:n_in + n_out + n_cm + n_acc]
        if comm is not None:
            cm_refs = (refs[n_op + n_ex:n_in], refs[n_in + n_out:n_in + n_out + n_cm],
                       refs[n_in + n_out + n_cm + n_acc:])
            first, last = _grid_edges(grid)

            @pl.when(first)
            def _():
                comm.start(*cm_refs)

        def finish(vals):
            res = epilogue(*vals, *[e[...] for e in ex_refs])
            for o, r in zip(out_refs, res):
                o[...] = r.astype(o.dtype)

        if nk == 1:
            parts = [None] * len(acc_shapes)
            for li, ri, dims, ai in pairs:
                d = _dot(op_refs[li][...], op_refs[ri][...], dims)
                parts[ai] = d if parts[ai] is None else parts[ai] + d
            finish(parts)
        else:
            k = pl.program_id(k_axis)

            @pl.when(k == 0)
            def _():
                for a in acc_refs:
                    a[...] = jnp.zeros_like(a)

            for li, ri, dims, ai in pairs:
                acc_refs[ai][...] += _dot(op_refs[li][...], op_refs[ri][...], dims)

            @pl.when(k == nk - 1)
            def _():
                finish([a[...] for a in acc_refs])

        if comm is not None:
            @pl.when(last)
            def _():
                comm.finish(*cm_refs)

    scratch = [pltpu.VMEM(s, F32) for s in acc_shapes] if nk > 1 else []
    if comm is None:
        return pl.pallas_call(
            body, name=name, grid=grid, out_shape=outs,
            in_specs=list(op_specs) + list(extra_specs), out_specs=list(out_specs),
            scratch_shapes=scratch, compiler_params=_params(sem),
        )(*ops, *extras)
    res = pl.pallas_call(
        body, name=name, grid=grid, out_shape=list(outs) + comm.out_shapes,
        in_specs=list(op_specs) + list(extra_specs) + comm.specs, out_specs=list(out_specs) + comm.specs,
        scratch_shapes=scratch + comm.scratch, compiler_params=_params(("arbitrary",) * len(grid)),
    )(*ops, *extras, *comm.arrays)
    return res[:n_out], res[n_out:]


def _with_comm(res, comm, pick):
    if comm is None:
        return pick(res)
    return pick(res[0]), res[1]


def _mm_nn(name, a, w, out_dtype, tm_target=704, tn_target=1664, epilogue=None, extras=(), extra_specs=(), comm=None):
    L, K = a.shape
    N = w.shape[1]
    tm, tn = _tile(L, tm_target), _tile(N, tn_target, 128)
    ep = epilogue if epilogue is not None else (lambda acc: (acc,))
    res = _mm(name, (L // tm, N // tn), ("parallel", "parallel"), None,
              [a, w], [pl.BlockSpec((tm, K), lambda i, j: (i, 0)), pl.BlockSpec((K, tn), lambda i, j: (0, j))],
              [(0, 1, 'nn', 0)], [(tm, tn)], list(extras), list(extra_specs), ep,
              [jax.ShapeDtypeStruct((L, N), out_dtype)], [pl.BlockSpec((tm, tn), lambda i, j: (i, j))], comm=comm)
    return _with_comm(res, comm, lambda o: o[0])


def _mm_nt(name, pairs_aw, out_dtype, tm_target=704, tn_target=512, comm=None):
    L = pairs_aw[0][0].shape[0]
    N = pairs_aw[0][1].shape[0]
    tm, tn = _tile(L, tm_target), _tile(N, tn_target, 128)
    ops, specs, pairs = [], [], []
    for t, (a, w) in enumerate(pairs_aw):
        K = a.shape[1]
        ops += [a, w]
        specs += [pl.BlockSpec((tm, K), lambda i, j: (i, 0)), pl.BlockSpec((tn, K), lambda i, j: (j, 0))]
        pairs.append((2 * t, 2 * t + 1, 'nt', 0))
    res = _mm(name, (L // tm, N // tn), ("parallel", "parallel"), None, ops, specs, pairs, [(tm, tn)], [], [],
              lambda acc: (acc,), [jax.ShapeDtypeStruct((L, N), out_dtype)],
              [pl.BlockSpec((tm, tn), lambda i, j: (i, j))], comm=comm)
    return _with_comm(res, comm, lambda o: o[0])


def _mm_tn(name, a, bs, out_dtype=BF16, tk_target=704, tn_target=1664, comm=None):
    L, M = a.shape
    N = bs[0].shape[1]
    tk, tn = _tile(L, tk_target), _tile(N, tn_target, 128)
    nb = len(bs)
    ops = [a] + list(bs)
    specs = [pl.BlockSpec((tk, M), lambda j, k: (k, 0))] + [pl.BlockSpec((tk, tn), lambda j, k: (k, j))] * nb
    res = _mm(name, (N // tn, L // tk), ("parallel", "arbitrary"), 1, ops, specs,
              [(0, 1 + t, 'tn', t) for t in range(nb)], [(M, tn)] * nb, [], [], lambda *acc: acc,
              [jax.ShapeDtypeStruct((M, N), out_dtype)] * nb, [pl.BlockSpec((M, tn), lambda j, k: (0, j))] * nb,
              comm=comm)
    return _with_comm(res, comm, lambda o: o)


def _norm_fwd(x, w):
    L, D = x.shape
    tr = _tile(L, 512)

    def body(x_ref, w_ref, y_ref):
        v = x_ref[...]
        r = lax.rsqrt(jnp.mean(v * v, axis=-1, keepdims=True) + EPS)
        y_ref[...] = (v * r * w_ref[...]).astype(y_ref.dtype)

    return pl.pallas_call(
        body, name="norm_fwd", grid=(L // tr,), out_shape=jax.ShapeDtypeStruct((L, D), BF16),
        in_specs=[pl.BlockSpec((tr, D), lambda i: (i, 0)), pl.BlockSpec((1, D), lambda i: (0, 0))],
        out_specs=pl.BlockSpec((tr, D), lambda i: (i, 0)), compiler_params=_params(("parallel",)),
    )(x, w)


def _norm_bwd_math(x, w, dy):
    r = lax.rsqrt(jnp.mean(x * x, axis=-1, keepdims=True) + EPS)
    gy = dy * w
    dx = r * (gy - x * (r * r) * jnp.mean(gy * x, axis=-1, keepdims=True))
    dw = jnp.sum(dy * x * r, axis=0, keepdims=True)
    return dx, dw


def _norm_bwd(x, w, dy, res, scale, out_dtype):
    L, D = x.shape
    tr = _tile(L, 384)
    has_res = res is not None

    def body(*refs):
        x_ref, w_ref, dy_ref = refs[:3]
        res_ref = refs[3] if has_res else None
        dx_ref, dw_ref = refs[-2:]
        dx, dw = _norm_bwd_math(x_ref[...], w_ref[...], dy_ref[...].astype(F32))
        dx = scale * dx
        if has_res:
            dx = dx + res_ref[...]
        dx_ref[...] = dx.astype(dx_ref.dtype)

        @pl.when(pl.program_id(0) == 0)
        def _():
            dw_ref[...] = jnp.zeros_like(dw_ref)

        dw_ref[...] += scale * dw

    row = pl.BlockSpec((tr, D), lambda i: (i, 0))
    vec = pl.BlockSpec((1, D), lambda i: (0, 0))
    return pl.pallas_call(
        body, name="norm_bwd", grid=(L // tr,),
        out_shape=[jax.ShapeDtypeStruct((L, D), out_dtype), jax.ShapeDtypeStruct((1, D), F32)],
        in_specs=[row, vec, row] + ([row] if has_res else []), out_specs=[row, vec],
        compiler_params=_params(("arbitrary",)),
    )(*([x, w, dy] + ([res] if has_res else [])))


def _loss(h, target):
    L, D = h.shape

    def body(h_ref, t_ref, dh_ref, loss_ref):
        i = pl.program_id(0)

        @pl.when(i == 0)
        def _():
            dh_ref[...] = jnp.zeros_like(dh_ref)
            loss_ref[...] = jnp.zeros_like(loss_ref)

        @pl.when(i > 0)
        def _():
            diff = h_ref[...] - t_ref[...]
            dh_ref[...] = diff * (1.0 / D)
            loss_ref[...] += 0.5 * jnp.sum(diff * diff) * (1.0 / D)

    return pl.pallas_call(
        body, name="loss", grid=(L // BLK,),
        out_shape=[jax.ShapeDtypeStruct((L, D), F32), jax.ShapeDtypeStruct((8, 128), F32)],
        in_specs=[pl.BlockSpec((BLK, D), lambda i: (i, 0)),
                  pl.BlockSpec((BLK, D), lambda i: (jnp.maximum(i - 1, 0), 0))],
        out_specs=[pl.BlockSpec((BLK, D), lambda i: (i, 0)), pl.BlockSpec((8, 128), lambda i: (0, 0))],
        compiler_params=_params(("arbitrary",)),
    )(h, target)


def _ffn_up(a, wg, wu, comm=None):
    L, D = a.shape
    F = wg.shape[2]
    tm = _tile(L, 704)

    def ep(g, u):
        return g, u, g * _sigmoid(g) * u

    hspec = pl.BlockSpec((None, tm, F), lambda i, j: (j, i, 0))
    wspec = pl.BlockSpec((None, D, F), lambda i, j: (j, 0, 0))
    res = _mm("ffn_up", (L // tm, N_DEV), ("parallel", "parallel"), None,
              [a, wg, wu], [pl.BlockSpec((tm, D), lambda i, j: (i, 0)), wspec, wspec],
              [(0, 1, 'nn', 0), (0, 2, 'nn', 1)], [(tm, F)] * 2, [], [], ep,
              [jax.ShapeDtypeStruct((N_DEV, L, F), BF16)] * 3, [hspec] * 3, comm=comm)
    return _with_comm(res, comm, lambda o: o)


def _ffn_gate(a, wg, comm=None):
    L, D = a.shape
    F = wg.shape[2]
    tm = _tile(L, 704)
    res = _mm("ffn_gate", (L // tm, N_DEV), ("parallel", "parallel"), None,
              [a, wg], [pl.BlockSpec((tm, D), lambda i, j: (i, 0)), pl.BlockSpec((None, D, F), lambda i, j: (j, 0, 0))],
              [(0, 1, 'nn', 0)], [(tm, F)], [], [], lambda g: (g,),
              [jax.ShapeDtypeStruct((N_DEV, L, F), BF16)], [pl.BlockSpec((None, tm, F), lambda i, j: (j, i, 0))],
              comm=comm)
    return _with_comm(res, comm, lambda o: o[0])


def _ffn_up_gated(a, wu, g, comm=None):
    L, D = a.shape
    F = wu.shape[2]
    tm = _tile(L, 704)

    def ep(u, g_):
        g32 = g_.astype(F32)
        return u, g32 * _sigmoid(g32) * u

    hspec = pl.BlockSpec((None, tm, F), lambda i, j: (j, i, 0))
    res = _mm("ffn_up_gated", (L // tm, N_DEV), ("parallel", "parallel"), None,
              [a, wu], [pl.BlockSpec((tm, D), lambda i, j: (i, 0)), pl.BlockSpec((None, D, F), lambda i, j: (j, 0, 0))],
              [(0, 1, 'nn', 0)], [(tm, F)], [g], [hspec], ep,
              [jax.ShapeDtypeStruct((N_DEV, L, F), BF16)] * 2, [hspec, hspec], comm=comm)
    return _with_comm(res, comm, lambda o: o)


def _resnorm_epilogue(scale):
    def ep(acc, h, w):
        r = lax.rsqrt(jnp.mean(acc * acc, axis=-1, keepdims=True) + EPS)
        return acc, h + scale * (acc * r * w)
    return ep


def _ffn_down(hid, wd, h_in, post, comm=None):
    _, L, F = hid.shape
    D = wd.shape[2]
    tm = _tile(L, 528)
    row = pl.BlockSpec((tm, D), lambda i, j: (i, 0))
    res = _mm("ffn_down", (L // tm, N_DEV), ("parallel", "arbitrary"), 1,
              [hid, wd], [pl.BlockSpec((None, tm, F), lambda i, j: (j, i, 0)),
                          pl.BlockSpec((None, F, D), lambda i, j: (j, 0, 0))],
              [(0, 1, 'nn', 0)], [(tm, D)], [h_in, post], [row, pl.BlockSpec((1, D), lambda i, j: (0, 0))],
              _resnorm_epilogue(0.5), [jax.ShapeDtypeStruct((L, D), F32)] * 2, [row, row], comm=comm)
    return _with_comm(res, comm, lambda o: o)


def _ffn_dhid(df, wd, g, u, comm=None):
    L, D = df.shape
    F = wd.shape[1]
    tm = _tile(L, 704)

    def ep(dhid, g_, u_):
        g32, u32 = g_.astype(F32), u_.astype(F32)
        sg = _sigmoid(g32)
        return dhid * u32 * sg * (1.0 + g32 * (1.0 - sg)), dhid * g32 * sg

    hspec = pl.BlockSpec((None, tm, F), lambda i, j: (j, i, 0))
    res = _mm("ffn_dhid", (L // tm, N_DEV), ("parallel", "parallel"), None,
              [df, wd], [pl.BlockSpec((tm, D), lambda i, j: (i, 0)),
                         pl.BlockSpec((None, F, D), lambda i, j: (j, 0, 0))],
              [(0, 1, 'nt', 0)], [(tm, F)], [g, u], [hspec, hspec], ep,
              [jax.ShapeDtypeStruct((N_DEV, L, F), BF16)] * 2, [hspec, hspec], comm=comm)
    return _with_comm(res, comm, lambda o: o)


def _ffn_dwd(hid, df, comm=None):
    _, L, F = hid.shape
    D = df.shape[1]
    tk = _tile(L, 1408)
    res = _mm("ffn_dwd", (N_DEV, L // tk), ("parallel", "arbitrary"), 1,
              [hid, df], [pl.BlockSpec((None, tk, F), lambda j, k: (j, k, 0)),
                          pl.BlockSpec((tk, D), lambda j, k: (k, 0))],
              [(0, 1, 'tn', 0)], [(F, D)], [], [], lambda acc: (acc,),
              [jax.ShapeDtypeStruct((N_DEV, F, D), BF16)], [pl.BlockSpec((None, F, D), lambda j, k: (j, 0, 0))],
              comm=comm)
    return _with_comm(res, comm, lambda o: o[0])


def _ffn_dwgu(a, dg, du, comm=None):
    L, D = a.shape
    F = dg.shape[2]
    tk = _tile(L, 704)
    hspec = pl.BlockSpec((None, tk, F), lambda j, k: (j, k, 0))
    wspec = pl.BlockSpec((None, D, F), lambda j, k: (j, 0, 0))
    res = _mm("ffn_dwgu", (N_DEV, L // tk), ("parallel", "arbitrary"), 1,
              [a, dg, du], [pl.BlockSpec((tk, D), lambda j, k: (k, 0)), hspec, hspec],
              [(0, 1, 'tn', 0), (0, 2, 'tn', 1)], [(D, F)] * 2, [], [], lambda *acc: acc,
              [jax.ShapeDtypeStruct((N_DEV, D, F), BF16)] * 2, [wspec, wspec], comm=comm)
    return _with_comm(res, comm, lambda o: o)


def _ffn_da(dg, du, wg, wu, comm=None):
    _, L, F = dg.shape
    D = wg.shape[1]
    tm = _tile(L, 704)
    hspec = pl.BlockSpec((None, tm, F), lambda i, j: (j, i, 0))
    wspec = pl.BlockSpec((None, D, F), lambda i, j: (j, 0, 0))
    row = pl.BlockSpec((tm, D), lambda i, j: (i, 0))
    res = _mm("ffn_da", (L // tm, N_DEV), ("parallel", "arbitrary"), 1,
              [dg, du, wg, wu], [hspec, hspec, wspec, wspec],
              [(0, 2, 'nt', 0), (1, 3, 'nt', 0)], [(tm, D)], [], [], lambda acc: (acc,),
              [jax.ShapeDtypeStruct((L, D), F32)], [row], comm=comm)
    return _with_comm(res, comm, lambda o: o[0])


def _rope_tables(L):
    rows = jnp.arange(L, dtype=F32)
    pos = jnp.where(rows < BLK, rows, rows - (BLK - N_META))
    inv_r = ROPE_THETA ** (-jnp.arange(0, HD, 2, dtype=F32) / HD)
    ang_r = pos[:, None] * inv_r[None, :]
    cr = jnp.concatenate([jnp.cos(ang_r), jnp.cos(ang_r)], axis=1)
    sr = jnp.concatenate([-jnp.sin(ang_r), jnp.sin(ang_r)], axis=1)
    inv_m = ROPE_THETA ** (-jnp.arange(0, ROPE, 2, dtype=F32) / ROPE)
    ang_m = pos[:, None] * inv_m[None, :]
    z32 = jnp.zeros((L, ROPE // 2), F32)
    z64 = jnp.zeros((L, HD - ROPE), F32)
    cm = jnp.concatenate([jnp.cos(ang_m), jnp.cos(ang_m), z64], axis=1)
    sa = jnp.concatenate([-jnp.sin(ang_m), z32, z64], axis=1)
    sb = jnp.concatenate([z32, jnp.sin(ang_m), z64], axis=1)
    return cr, sr, cm, sa, sb


def _rope_ret(x, cr, sr):
    return x * cr + pltpu.roll(x, HD // 2, 1) * sr


def _rope_ret_t(d, cr, sr):
    return d * cr + pltpu.roll(d * sr, HD // 2, 1)


def _rope_mla(x, cm, sa, sb):
    return x * cm + pltpu.roll(x, HD - ROPE // 2, 1) * sa + pltpu.roll(x, ROPE // 2, 1) * sb


def _rope_mla_t(d, cm, sa, sb):
    return d * cm + pltpu.roll(d * sa, ROPE // 2, 1) + pltpu.roll(d * sb, HD - ROPE // 2, 1)


C_RQ, C_RK, C_RV, C_RG = 0, HEADS * HD, 2 * HEADS * HD, 3 * HEADS * HD
C_CQ = 4 * HEADS * HD
C_CKV = C_CQ + Q_RANK
C_KR = C_CKV + KV_RANK
RET_K_SCALE = HD ** -0.5


def _prep(proj, tabs, qn, kvn):
    L = proj.shape[0]
    tr = _tile(L, 256)
    W = HEADS * HD

    def body(p_ref, cr_ref, sr_ref, cm_ref, sa_ref, sb_ref, qn_ref, kvn_ref, q_ref, k_ref, v_ref, cq_ref, ckv_ref,
             kr_ref):
        cr, sr = cr_ref[...], sr_ref[...]
        for h in range(HEADS):
            sl = slice(h * HD, (h + 1) * HD)
            q_ref[:, sl] = _rope_ret(p_ref[:, C_RQ + h * HD:C_RQ + (h + 1) * HD], cr, sr).astype(BF16)
            k_ref[:, sl] = (_rope_ret(p_ref[:, C_RK + h * HD:C_RK + (h + 1) * HD], cr, sr)
                            * RET_K_SCALE).astype(BF16)
        v_ref[...] = p_ref[:, C_RV:C_RV + W].astype(BF16)
        cq = p_ref[:, C_CQ:C_CQ + Q_RANK]
        cq_ref[...] = (cq * lax.rsqrt(jnp.mean(cq * cq, axis=-1, keepdims=True) + EPS) * qn_ref[...]).astype(BF16)
        ckv = p_ref[:, C_CKV:C_CKV + KV_RANK]
        ckv_ref[...] = (ckv * lax.rsqrt(jnp.mean(ckv * ckv, axis=-1, keepdims=True) + EPS)
                        * kvn_ref[...]).astype(BF16)
        kr_ref[...] = _rope_mla(p_ref[:, C_KR:C_KR + HD], cm_ref[...], sa_ref[...], sb_ref[...]).astype(BF16)

    row = lambda w: pl.BlockSpec((tr, w), lambda i: (i, 0))
    vec = lambda w: pl.BlockSpec((1, w), lambda i: (0, 0))
    return pl.pallas_call(
        body, name="mix_prep", grid=(L // tr,),
        out_shape=[jax.ShapeDtypeStruct((L, W), BF16)] * 3 + [jax.ShapeDtypeStruct((L, Q_RANK), BF16),
                                                              jax.ShapeDtypeStruct((L, KV_RANK), BF16),
                                                              jax.ShapeDtypeStruct((L, HD), BF16)],
        in_specs=[row(D_INP)] + [row(HD)] * 5 + [vec(Q_RANK), vec(KV_RANK)],
        out_specs=[row(W)] * 3 + [row(Q_RANK), row(KV_RANK), row(HD)],
        compiler_params=_params(("parallel",)),
    )(proj, *tabs, qn, kvn)


def _prep_bwd(proj, dq, dk, dv, drg, dcqn, dckvn, dkr8, tabs, qn, kvn):
    L = proj.shape[0]
    tr = _tile(L, 192)
    W = HEADS * HD

    def body(p_ref, dq_ref, dk_ref, dv_ref, drg_ref, dcq_ref, dckv_ref, dkr_ref, cr_ref, sr_ref, cm_ref, sa_ref,
             sb_ref, qn_ref, kvn_ref, dp_ref, dqn_ref, dkvn_ref):
        cr, sr = cr_ref[...], sr_ref[...]
        dkr = None
        for h in range(HEADS):
            sl = slice(h * HD, (h + 1) * HD)
            dp_ref[:, C_RQ + h * HD:C_RQ + (h + 1) * HD] = _rope_ret_t(dq_ref[:, sl], cr, sr).astype(BF16)
            dp_ref[:, C_RK + h * HD:C_RK + (h + 1) * HD] = (_rope_ret_t(dk_ref[:, sl], cr, sr)
                                                            * RET_K_SCALE).astype(BF16)
            part = dkr_ref[:, sl]
            dkr = part if dkr is None else dkr + part
        dp_ref[:, C_RV:C_RV + W] = dv_ref[...].astype(BF16)
        dp_ref[:, C_RG:C_RG + W] = drg_ref[...].astype(BF16)
        dcq, dqn = _norm_bwd_math(p_ref[:, C_CQ:C_CQ + Q_RANK], qn_ref[...], dcq_ref[...])
        dp_ref[:, C_CQ:C_CQ + Q_RANK] = dcq.astype(BF16)
        dckv, dkvn = _norm_bwd_math(p_ref[:, C_CKV:C_CKV + KV_RANK], kvn_ref[...], dckv_ref[...])
        dp_ref[:, C_CKV:C_CKV + KV_RANK] = dckv.astype(BF16)
        dp_ref[:, C_KR:C_KR + HD] = _rope_mla_t(dkr, cm_ref[...], sa_ref[...], sb_ref[...]).astype(BF16)

        @pl.when(pl.program_id(0) == 0)
        def _():
            dqn_ref[...] = jnp.zeros_like(dqn_ref)
            dkvn_ref[...] = jnp.zeros_like(dkvn_ref)

        dqn_ref[...] += dqn
        dkvn_ref[...] += dkvn

    row = lambda w: pl.BlockSpec((tr, w), lambda i: (i, 0))
    vec = lambda w: pl.BlockSpec((1, w), lambda i: (0, 0))
    return pl.pallas_call(
        body, name="mix_prep_bwd", grid=(L // tr,),
        out_shape=[jax.ShapeDtypeStruct((L, D_INP), BF16), jax.ShapeDtypeStruct((1, Q_RANK), F32),
                   jax.ShapeDtypeStruct((1, KV_RANK), F32)],
        in_specs=[row(D_INP)] + [row(W)] * 4 + [row(Q_RANK), row(KV_RANK), row(W)] + [row(HD)] * 5
                 + [vec(Q_RANK), vec(KV_RANK)],
        out_specs=[row(D_INP), vec(Q_RANK), vec(KV_RANK)],
        compiler_params=_params(("arbitrary",)),
    )(proj, dq, dk, dv, drg, dcqn, dckvn, dkr8, *tabs, qn, kvn)


def _post(o_ret, proj, gn):
    L, W = o_ret.shape
    tr = _tile(L, 384)

    def body(o_ref, rg_ref, gn_ref, out_ref):
        for h in range(HEADS):
            sl = slice(h * HD, (h + 1) * HD)
            o = o_ref[:, sl]
            rg = rg_ref[:, sl]
            n = o * lax.rsqrt(jnp.mean(o * o, axis=-1, keepdims=True) + EPS)
            out_ref[:, sl] = (n * gn_ref[:, sl] * (rg * _sigmoid(rg))).astype(BF16)

    row = pl.BlockSpec((tr, W), lambda i: (i, 0))
    return pl.pallas_call(
        body, name="ret_post", grid=(L // tr,), out_shape=jax.ShapeDtypeStruct((L, W), BF16),
        in_specs=[row, pl.BlockSpec((tr, W), lambda i: (i, C_RG // W)), pl.BlockSpec((1, W), lambda i: (0, 0))],
        out_specs=row, compiler_params=_params(("parallel",)),
    )(o_ret, proj, gn)


def _post_bwd(o_ret, proj, gn, dcat):
    L, W = o_ret.shape
    tr = _tile(L, 384)

    def body(o_ref, rg_ref, gn_ref, d_ref, do_ref, drg_ref, dgn_ref):
        @pl.when(pl.program_id(0) == 0)
        def _():
            dgn_ref[...] = jnp.zeros_like(dgn_ref)

        for h in range(HEADS):
            sl = slice(h * HD, (h + 1) * HD)
            o = o_ref[:, sl]
            rg = rg_ref[:, sl]
            d = d_ref[:, sl].astype(F32)
            gw = gn_ref[:, sl]
            r = lax.rsqrt(jnp.mean(o * o, axis=-1, keepdims=True) + EPS)
            n = o * r
            sg = _sigmoid(rg)
            si = rg * sg
            dn = d * gw * si
            dgn_ref[:, sl] += jnp.sum(d * n * si, axis=0, keepdims=True)
            drg_ref[:, sl] = d * n * gw * sg * (1.0 + rg * (1.0 - sg))
            do_ref[:, sl] = (r * (dn - o * (r * r) * jnp.mean(dn * o, axis=-1, keepdims=True))).astype(BF16)

    row = pl.BlockSpec((tr, W), lambda i: (i, 0))
    vec = pl.BlockSpec((1, W), lambda i: (0, 0))
    return pl.pallas_call(
        body, name="ret_post_bwd", grid=(L // tr,),
        out_shape=[jax.ShapeDtypeStruct((L, W), BF16), jax.ShapeDtypeStruct((L, W), F32),
                   jax.ShapeDtypeStruct((1, W), F32)],
        in_specs=[row, pl.BlockSpec((tr, W), lambda i: (i, C_RG // W)), vec, row],
        out_specs=[row, row, vec], compiler_params=_params(("arbitrary",)),
    )(o_ret, proj, gn, dcat)


RET_HEADS_PER_STEP = 4


def _lin_attn(name, q, k, v, lg, reverse):
    L, W = q.shape
    nc = L // BLK - 1
    G = RET_HEADS_PER_STEP

    def body(q_ref, k_ref, v_ref, lg_ref, o_ref, s_ref):
        n = lax.broadcasted_iota(jnp.int32, (BLK, BLK), 0).astype(F32)
        m = lax.broadcasted_iota(jnp.int32, (BLK, BLK), 1).astype(F32)
        dist = (m - n) if reverse else (n - m)
        consts = []
        for g in range(G):
            lgv = lg_ref[g, 0:1, :]
            dmask = jnp.where(dist >= 0, jnp.exp(lgv * jnp.maximum(dist, 0.0)), 0.0)
            c = dict(dmask=dmask, dmask0=jnp.where((n < N_META) & (m < N_META), dmask, 0.0),
                     gl=jnp.exp(lgv * float(BLK)))
            if reverse:
                c.update(inter=jnp.exp(lgv * (float(BLK) - n)), upd=jnp.exp(lgv * n),
                         inter0=jnp.where(n < N_META, jnp.exp(lgv * jnp.maximum(float(N_META) - n, 0.0)), 0.0))
            else:
                c.update(inter=jnp.exp(lgv * (n + 1.0)), upd=jnp.exp(lgv * (float(BLK) - 1.0 - n)),
                         upd0=jnp.where(n < N_META, jnp.exp(lgv * jnp.maximum(float(N_META) - 1.0 - n, 0.0)), 0.0))
            consts.append(c)

        def chunk(c):
            rows = pl.ds(pl.multiple_of(c * BLK, BLK), BLK)
            state = [s_ref[g] for g in range(G)]
            outs, new_state = [], []
            for g in range(G):
                cols = slice(g * HD, (g + 1) * HD)
                cg = consts[g]
                qc, kc, vc = q_ref[rows, cols], k_ref[rows, cols], v_ref[rows, cols]
                a = _dot(qc, kc, 'nt') * cg['dmask']
                outs.append(_dot(a.astype(BF16), vc, 'nn') + _dot(qc, state[g].astype(BF16), 'nn') * cg['inter'])
                new_state.append(state[g] * cg['gl'] + _dot((kc.astype(F32) * cg['upd']).astype(BF16), vc, 'tn'))
            for g in range(G):
                o_ref[rows, g * HD:(g + 1) * HD] = outs[g]
                s_ref[g] = new_state[g]

        def first_chunk(with_state):
            for g in range(G):
                cols = slice(g * HD, (g + 1) * HD)
                cg = consts[g]
                q0, k0, v0 = q_ref[0:BLK, cols], k_ref[0:BLK, cols], v_ref[0:BLK, cols]
                o0 = _dot((_dot(q0, k0, 'nt') * cg['dmask0']).astype(BF16), v0, 'nn')
                if with_state:
                    o0 = o0 + _dot(q0, s_ref[g].astype(BF16), 'nn') * cg['inter0']
                else:
                    s_ref[g] = _dot((k0.astype(F32) * cg['upd0']).astype(BF16), v0, 'tn')
                o_ref[0:BLK, cols] = o0

        if reverse:
            s_ref[...] = jnp.zeros_like(s_ref)

            def step(t, carry):
                chunk(nc - t)
                return carry

            lax.fori_loop(0, nc, step, 0)
            first_chunk(True)
        else:
            first_chunk(False)

            def step(t, carry):
                chunk(t + 1)
                return carry

            lax.fori_loop(0, nc, step, 0)

    col = pl.BlockSpec((L, G * HD), lambda h: (0, h))
    return pl.pallas_call(
        body, name=name, grid=(HEADS // G,), out_shape=jax.ShapeDtypeStruct((L, W), F32),
        in_specs=[col, col, col, pl.BlockSpec((G, 8, HD), lambda h: (h, 0, 0))], out_specs=col,
        scratch_shapes=[pltpu.VMEM((G, HD, HD), F32)], compiler_params=_params(("parallel",)),
    )(q, k, v, lg)


ATT_SCALE = (HD + ROPE) ** -0.5
NEG = -1e30


ATT_TILE = 384
ATT_HEADS_PER_STEP = 2


def _att_valid(T, row0, col0):
    r = lax.broadcasted_iota(jnp.int32, (T, T), 0) + row0
    c = lax.broadcasted_iota(jnp.int32, (T, T), 1) + col0
    return (c <= r) & ((c < N_META) | (c >= BLK))


def _attn_fwd(qm, kn, krr, vm, comm=None):
    L = qm.shape[0]
    W = HEADS * HD
    T = _tile(L, ATT_TILE, BLK)
    nb = L // T
    G = ATT_HEADS_PER_STEP
    n_cm = comm.n if comm is not None else 0

    def body(*refs):
        q_ref, kn_ref, kr_ref, v_ref = refs[:4]
        o_ref, lse_ref = refs[4 + n_cm:6 + n_cm]
        m_sc, l_sc, acc_sc = refs[6 + 2 * n_cm:9 + 2 * n_cm]
        if comm is not None:
            cm_refs = (refs[4:4 + n_cm], refs[6 + n_cm:6 + 2 * n_cm], refs[9 + 2 * n_cm:])
            first, last = _grid_edges((HEADS // G, nb))

            @pl.when(first)
            def _():
                comm.start(*cm_refs)

        i = pl.program_id(1)
        m_sc[...] = jnp.full_like(m_sc, NEG)
        l_sc[...] = jnp.zeros_like(l_sc)
        acc_sc[...] = jnp.zeros_like(acc_sc)

        def tile(j, masked):
            rows = pl.ds(pl.multiple_of(j * T, T), T)
            kr = kr_ref[rows, :]
            valid = _att_valid(T, i * T, j * T) if masked else None
            m_prev = [m_sc[g] for g in range(G)]
            l_prev = [l_sc[g] for g in range(G)]
            acc_prev = [acc_sc[g] for g in range(G)]
            m_new, l_new, acc_new = [], [], []
            for g in range(G):
                k = jnp.concatenate([kn_ref[rows, g * HD:(g + 1) * HD], kr], axis=1)
                s = _dot(q_ref[:, g * QH:(g + 1) * QH], k, 'nt') * ATT_SCALE
                if masked:
                    s = jnp.where(valid, s, NEG)
                m_new.append(jnp.maximum(m_prev[g], jnp.max(s, axis=-1, keepdims=True)))
                p = jnp.exp(s - m_new[g])
                alpha = jnp.exp(m_prev[g] - m_new[g])
                l_new.append(alpha * l_prev[g] + jnp.sum(p, axis=-1, keepdims=True))
                acc_new.append(alpha * acc_prev[g] + _dot(p.astype(BF16), v_ref[rows, g * HD:(g + 1) * HD], 'nn'))
            for g in range(G):
                m_sc[g] = m_new[g]
                l_sc[g] = l_new[g]
                acc_sc[g] = acc_new[g]

        tile(0, True)

        def mid(j, carry):
            tile(j, False)
            return carry

        lax.fori_loop(1, i, mid, 0)

        @pl.when(i > 0)
        def _():
            tile(i, True)

        for g in range(G):
            l = l_sc[g]
            o_ref[:, g * HD:(g + 1) * HD] = (acc_sc[g] / l).astype(o_ref.dtype)
            lse_ref[g] = jnp.broadcast_to(m_sc[g] + jnp.log(l), (T, HD))

        if comm is not None:
            @pl.when(last)
            def _():
                comm.finish(*cm_refs)

    cm_specs = comm.specs if comm is not None else []
    res = pl.pallas_call(
        body, name="attn_fwd", grid=(HEADS // G, nb),
        out_shape=[jax.ShapeDtypeStruct((L, W), BF16), jax.ShapeDtypeStruct((HEADS, L, HD), F32)]
        + (comm.out_shapes if comm is not None else []),
        in_specs=[pl.BlockSpec((T, G * QH), lambda h, i: (i, h)), pl.BlockSpec((L, G * HD), lambda h, i: (0, h)),
                  pl.BlockSpec((L, HD), lambda h, i: (0, 0)), pl.BlockSpec((L, G * HD), lambda h, i: (0, h))]
        + cm_specs,
        out_specs=[pl.BlockSpec((T, G * HD), lambda h, i: (i, h)),
                   pl.BlockSpec((G, T, HD), lambda h, i: (h, i, 0))] + cm_specs,
        scratch_shapes=[pltpu.VMEM((G, T, 1), F32), pltpu.VMEM((G, T, 1), F32), pltpu.VMEM((G, T, HD), F32)]
        + (comm.scratch if comm is not None else []),
        compiler_params=_params(("arbitrary", "arbitrary")),
    )(qm, kn, krr, vm, *(comm.arrays if comm is not None else []))
    return res[:2], res[2:]


def _attn_bwd(qm, kn, krr, vm, o, dcat, lse, comm=None):
    L = qm.shape[0]
    W = HEADS * HD
    T = _tile(L, ATT_TILE, BLK)
    nb = L // T
    n_cm = comm.n if comm is not None else 0

    def body(*refs):
        q_ref, kn_ref, kr_ref, v_ref, o_ref, do_ref, lse_ref = refs[:7]
        dq_ref, dkn_ref, dkr_ref, dv_ref = refs[7 + n_cm:11 + n_cm]
        dl_sc, dk_sc, dv_sc = refs[11 + 2 * n_cm:14 + 2 * n_cm]
        if comm is not None:
            cm_refs = (refs[7:7 + n_cm], refs[11 + n_cm:11 + 2 * n_cm], refs[14 + 2 * n_cm:])
            first, last = _grid_edges((HEADS, nb))

            @pl.when(first)
            def _():
                comm.start(*cm_refs)

        j = pl.program_id(1)

        @pl.when(j == 0)
        def _():
            dq_ref[...] = jnp.zeros_like(dq_ref)

            def rowsum(t, carry):
                rows = pl.ds(pl.multiple_of(t * T, T), T)
                dl_sc[rows, :] = jnp.sum(do_ref[rows, :].astype(F32) * o_ref[rows, :].astype(F32), axis=-1,
                                         keepdims=True)
                return carry

            lax.fori_loop(0, nb, rowsum, 0)

        k = jnp.concatenate([kn_ref[...], kr_ref[...]], axis=1)
        v = v_ref[...]
        dk_sc[...] = jnp.zeros_like(dk_sc)
        dv_sc[...] = jnp.zeros_like(dv_sc)

        def tile(i, masked):
            rows = pl.ds(pl.multiple_of(i * T, T), T)
            q = q_ref[rows, :]
            do = do_ref[rows, :]
            s = _dot(q, k, 'nt') * ATT_SCALE
            if masked:
                s = jnp.where(_att_valid(T, i * T, j * T), s, NEG)
            p = jnp.exp(s - lse_ref[rows, 0:1])
            dv_sc[...] += _dot(p.astype(BF16), do, 'tn')
            ds = (p * (_dot(do, v, 'nt') - dl_sc[rows, :]) * ATT_SCALE).astype(BF16)
            dk_sc[...] += _dot(ds, q, 'tn')
            dq_ref[rows, :] += _dot(ds, k, 'nn')

        tile(j, True)

        def rest(masked):
            def step(i, carry):
                tile(i, masked)
                return carry
            lax.fori_loop(j + 1, nb, step, 0)

        @pl.when(j == 0)
        def _():
            rest(True)

        @pl.when(j > 0)
        def _():
            rest(False)

        dk = dk_sc[...]
        dkn_ref[...] = dk[:, 0:HD].astype(BF16)
        dkr_ref[...] = dk[:, HD:QH]
        dv_ref[...] = dv_sc[...].astype(BF16)

        if comm is not None:
            @pl.when(last)
            def _():
                comm.finish(*cm_refs)

    blk = pl.BlockSpec((T, HD), lambda h, j: (j, h))
    cm_specs = comm.specs if comm is not None else []
    res = pl.pallas_call(
        body, name="attn_bwd", grid=(HEADS, nb),
        out_shape=[jax.ShapeDtypeStruct((L, HEADS * QH), F32), jax.ShapeDtypeStruct((L, W), BF16),
                   jax.ShapeDtypeStruct((L, W), F32), jax.ShapeDtypeStruct((L, W), BF16)]
        + (comm.out_shapes if comm is not None else []),
        in_specs=[pl.BlockSpec((L, QH), lambda h, j: (0, h)), blk, pl.BlockSpec((T, HD), lambda h, j: (j, 0)), blk,
                  pl.BlockSpec((L, HD), lambda h, j: (0, h)), pl.BlockSpec((L, HD), lambda h, j: (0, HEADS + h)),
                  pl.BlockSpec((None, L, HD), lambda h, j: (h, 0, 0))] + cm_specs,
        out_specs=[pl.BlockSpec((L, QH), lambda h, j: (0, h)), blk, blk, blk] + cm_specs,
        scratch_shapes=[pltpu.VMEM((L, 1), F32), pltpu.VMEM((T, QH), F32), pltpu.VMEM((T, HD), F32)]
        + (comm.scratch if comm is not None else []),
        compiler_params=_params(("arbitrary", "arbitrary")),
    )(qm, kn, krr, vm, o, dcat, lse, *(comm.arrays if comm is not None else []))
    return res[:4], res[4:]


def _unrope_q(dqm, tabs_m):
    L, W = dqm.shape
    tr = _tile(L, 384)

    def body(d_ref, cm_ref, sa_ref, sb_ref, out_ref):
        cm, sa, sb = cm_ref[...], sa_ref[...], sb_ref[...]
        for h in range(HEADS):
            out_ref[:, h * QH:h * QH + HD] = d_ref[:, h * QH:h * QH + HD].astype(BF16)
            out_ref[:, h * QH + HD:(h + 1) * QH] = _rope_mla_t(d_ref[:, h * QH + HD:(h + 1) * QH], cm, sa,
                                                               sb).astype(BF16)

    row = pl.BlockSpec((tr, W), lambda i: (i, 0))
    tab = pl.BlockSpec((tr, HD), lambda i: (i, 0))
    return pl.pallas_call(
        body, name="unrope_q", grid=(L // tr,), out_shape=jax.ShapeDtypeStruct((L, W), BF16),
        in_specs=[row, tab, tab, tab], out_specs=row, compiler_params=_params(("parallel",)),
    )(dqm, *tabs_m)


def _q_up(cqn, wuq_p, tabs_m):
    L = cqn.shape[0]
    tm = _tile(L, 704)

    def ep(acc, cm, sa, sb):
        parts = []
        for h in range(HEADS):
            parts.append(acc[:, h * QH:h * QH + HD])
            parts.append(_rope_mla(acc[:, h * QH + HD:(h + 1) * QH], cm, sa, sb))
        return (jnp.concatenate(parts, axis=1),)

    tab = pl.BlockSpec((tm, HD), lambda i, j: (i, 0))
    return _mm("mla_q_up", (L // tm, 1), ("parallel", "parallel"), None,
               [cqn, wuq_p], [pl.BlockSpec((tm, Q_RANK), lambda i, j: (i, 0)),
                              pl.BlockSpec((Q_RANK, HEADS * QH), lambda i, j: (0, 0))],
               [(0, 1, 'nn', 0)], [(tm, HEADS * QH)], list(tabs_m), [tab] * 3, ep,
               [jax.ShapeDtypeStruct((L, HEADS * QH), BF16)], [pl.BlockSpec((tm, HEADS * QH), lambda i, j: (i, 0))])[0]


def _mix_out(cat, w_out, h_in, post):
    L, K = cat.shape
    D = w_out.shape[1]
    tm, tk = _tile(L, 384), _tile(K, 512, 128)
    row = pl.BlockSpec((tm, D), lambda i, k: (i, 0))
    return _mm("mix_out", (L // tm, K // tk), ("parallel", "arbitrary"), 1,
               [cat, w_out], [pl.BlockSpec((tm, tk), lambda i, k: (i, k)), pl.BlockSpec((tk, D), lambda i, k: (k, 0))],
               [(0, 1, 'nn', 0)], [(tm, D)], [h_in, post], [row, pl.BlockSpec((1, D), lambda i, k: (0, 0))],
               _resnorm_epilogue(1.0), [jax.ShapeDtypeStruct((L, D), F32)] * 2, [row, row])


ADAM_BLOCK_ELEMS = 512 * 704


def _adam_math(w, g, m, v):
    m = ADAM_B1 * m + (1.0 - ADAM_B1) * g
    v = ADAM_B2 * v + (1.0 - ADAM_B2) * (g * g)
    m_hat = m / (1.0 - ADAM_B1 ** ADAM_STEP)
    v_hat = v / (1.0 - ADAM_B2 ** ADAM_STEP)
    delta = -ADAM_LR * (m_hat / (jnp.sqrt(v_hat) + ADAM_EPS) + ADAM_WD * w)
    return delta, m, v


def _adam(name, w, m, v, g_slots=None, g=None):
    R, C = w.shape
    tr = _tile(R, max(16, ADAM_BLOCK_ELEMS // C // 16 * 16), 16)
    from_slots = g_slots is not None

    def body(w_ref, m_ref, v_ref, g_ref, go_ref, d_ref, mo_ref, vo_ref):
        if from_slots:
            grad = g_ref[0].astype(F32)
            for s in range(1, N_DEV):
                grad = grad + g_ref[s].astype(F32)
        else:
            grad = g_ref[...]
        delta, mn, vn = _adam_math(w_ref[...], grad, m_ref[...], v_ref[...])
        go_ref[...] = grad
        d_ref[...] = delta
        mo_ref[...] = mn
        vo_ref[...] = vn

    row = pl.BlockSpec((tr, C), lambda i: (i, 0))
    gspec = pl.BlockSpec((N_DEV, tr, C), lambda i: (0, i, 0)) if from_slots else row
    return pl.pallas_call(
        body, name=name, grid=(R // tr,), out_shape=[jax.ShapeDtypeStruct((R, C), F32)] * 4,
        in_specs=[row, row, row, gspec], out_specs=[row] * 4, compiler_params=_params(("parallel",)),
    )(w, m, v, g_slots if from_slots else g)


def _unblock(gathered):
    n, r, c = gathered.shape
    return jnp.transpose(gathered, (1, 0, 2)).reshape(r, n * c)


def _reblock(full, c):
    r = full.shape[0]
    return jnp.transpose(full[:, :N_DEV * c].reshape(r, N_DEV, c), (1, 0, 2))


def _step(x, target, w, mom, vel):
    S, D = x.shape[1], x.shape[2]
    L = S + BLK
    sq = lambda a: a.reshape(a.shape[1:]) if a.ndim == 3 else a
    p = {n: sq(w[n]) for n in WEIGHTS if n != 'meta_tokens'}
    gather = lambda names: _Exchange([p[n].astype(BF16) for n in names], False)
    scatter = lambda blocks: _Exchange(blocks, True)
    in_s, uq_s = p['w_in'].shape[1], p['mla_w_uq'].shape[1]
    tabs = _rope_tables(L)
    tabs_m = tabs[2:]
    lg = jnp.broadcast_to(jnp.log(1.0 - 2.0 ** (-5.0 - jnp.arange(HEADS, dtype=F32)))[:, None, None], (HEADS, 8, HD))
    R = {}

    wg1, meta = _exchange("gather_first", [p['ffn1_w_gate'].astype(BF16), w['meta_tokens']], False)
    h0 = jnp.concatenate([_unblock(meta), jnp.zeros((BLK - N_META, D), F32), x[0]], axis=0)
    a1 = _norm_fwd(h0, p['ffn1_pre_norm'])
    g1, (wu1,) = _ffn_gate(a1, wg1, comm=gather(['ffn1_w_up']))
    (u1, hid1), (wd1,) = _ffn_up_gated(a1, wu1, g1, comm=gather(['ffn1_w_down']))
    (f1, h1), (w_in_g,) = _ffn_down(hid1, wd1, h0, p['ffn1_post_norm'], comm=gather(['w_in']))

    w_in_full = _unblock(w_in_g)
    w_in = jnp.pad(w_in_full, ((0, 0), (0, D_INP - w_in_full.shape[1])))
    um = _norm_fwd(h1, p['mix_pre_norm'])
    proj, (uq_g, uk_g, uv_g, wout_g) = _mm_nn("mix_in", um, w_in, F32,
                                              comm=gather(['mla_w_uq', 'mla_w_uk', 'mla_w_uv', 'w_out']))
    wuq = jnp.pad(_unblock(uq_g).reshape(Q_RANK, HEADS, HD + ROPE),
                  ((0, 0), (0, 0), (0, QH - HD - ROPE))).reshape(Q_RANK, HEADS * QH)
    wuk, wuv, w_out = _unblock(uk_g), _unblock(uv_g), wout_g.reshape(-1, D)
    qr, kr, vr, cqn, ckvn, krr = _prep(proj, tabs, p['mla_q_norm'], p['mla_kv_norm'])
    qm = _q_up(cqn, wuq, tabs_m)
    kn = _mm_nn("mla_k_up", ckvn, wuk, BF16)
    vm = _mm_nn("mla_v_up", ckvn, wuv, BF16)
    (o_mla, lse), (wg2, wu2) = _attn_fwd(qm, kn, krr, vm, comm=gather(['ffn2_w_gate', 'ffn2_w_up']))
    o_ret = _lin_attn("ret_fwd", qr, kr, vr, lg, False)
    ret = _post(o_ret, proj, p['ret_group_norm'])
    cat = jnp.concatenate([ret, o_mla], axis=1)
    m, h2 = _mix_out(cat, w_out, h1, p['mix_post_norm'])

    a2 = _norm_fwd(h2, p['ffn2_pre_norm'])
    (g2, u2, hid2), (wd2,) = _ffn_up(a2, wg2, wu2, comm=gather(['ffn2_w_down']))
    f2, h3 = _ffn_down(hid2, wd2, h2, p['ffn2_post_norm'])
    dh3, loss_blk = _loss(h3, target[0])

    dsmall = {}
    df2, dsmall['ffn2_post_norm'] = _norm_bwd(f2, p['ffn2_post_norm'], dh3, None, 0.5, BF16)
    dg2, du2 = _ffn_dhid(df2, wd2, g2, u2)
    dwd2 = _ffn_dwd(hid2, df2)
    (dwg2, dwu2), (R['ffn2_w_down'],) = _ffn_dwgu(a2, dg2, du2, comm=scatter([dwd2]))
    da2, (R['ffn2_w_gate'],) = _ffn_da(dg2, du2, wg2, wu2, comm=scatter([dwg2]))
    dh2, dsmall['ffn2_pre_norm'] = _norm_bwd(h2, p['ffn2_pre_norm'], da2, dh3, 1.0, F32)

    dm, dsmall['mix_post_norm'] = _norm_bwd(m, p['mix_post_norm'], dh2, None, 1.0, BF16)
    dcat = _mm_nt("mix_dcat", [(dm, w_out)], BF16)
    dwout = _mm_tn("mix_dwout", cat, [dm])[0]
    do_ret, drg, dsmall['ret_group_norm'] = _post_bwd(o_ret, proj, p['ret_group_norm'], dcat)
    dqr = _lin_attn("ret_dq", do_ret, vr, kr, lg, False)
    dkr = _lin_attn("ret_dk", vr, do_ret, qr, lg, True)
    dvr = _lin_attn("ret_dv", kr, qr, do_ret, lg, True)
    (dqm, dkn, dkr8, dvm), (R['ffn2_w_up'], R['w_out']) = _attn_bwd(
        qm, kn, krr, vm, o_mla, dcat, lse, comm=scatter([dwu2, dwout.reshape(N_DEV, -1, D)]))
    dqp = _unrope_q(dqm, tabs_m)
    dwuq = _mm_tn("mla_dwuq", cqn, [dqp])[0]
    dcqn = _mm_nt("mla_dcq", [(dqp, wuq)], F32)
    dwuk, dwuv = _mm_tn("mla_dwukv", ckvn, [dkn, dvm])
    dckvn = _mm_nt("mla_dckv", [(dkn, wuk), (dvm, wuv)], F32)
    dproj, dsmall['mla_q_norm'], dsmall['mla_kv_norm'] = _prep_bwd(
        proj, dqr, dkr, dvr, drg, dcqn, dckvn, dkr8, tabs, p['mla_q_norm'], p['mla_kv_norm'])
    dwuq_b = _reblock(dwuq.reshape(Q_RANK, HEADS, QH)[:, :, :HD + ROPE].reshape(Q_RANK, HEADS * (HD + ROPE)), uq_s)
    (dwin,), (R['mla_w_uq'], R['mla_w_uk'], R['mla_w_uv']) = _mm_tn(
        "mix_dwin", um, [dproj], comm=scatter([dwuq_b, _reblock(dwuk, p['mla_w_uk'].shape[1]),
                                               _reblock(dwuv, p['mla_w_uv'].shape[1])]))
    dwin_b = _reblock(dwin, in_s)
    half = D // 2
    dum, (r_win_a,) = _mm_nt("mix_du", [(dproj, w_in)], F32, comm=scatter([dwin_b[:, :half]]))
    dh1, dsmall['mix_pre_norm'] = _norm_bwd(h1, p['mix_pre_norm'], dum, dh2, 1.0, F32)

    df1, dsmall['ffn1_post_norm'] = _norm_bwd(f1, p['ffn1_post_norm'], dh1, None, 0.5, BF16)
    (dg1, du1), (r_win_b,) = _ffn_dhid(df1, wd1, g1, u1, comm=scatter([dwin_b[:, half:]]))
    R['w_in'] = jnp.concatenate([r_win_a, r_win_b], axis=1)
    dwd1 = _ffn_dwd(hid1, df1)
    (dwg1, dwu1), (R['ffn1_w_down'],) = _ffn_dwgu(a1, dg1, du1, comm=scatter([dwd1]))
    da1, (R['ffn1_w_gate'],) = _ffn_da(dg1, du1, wg1, wu1, comm=scatter([dwg1]))
    dh0, dsmall['ffn1_pre_norm'] = _norm_bwd(h0, p['ffn1_pre_norm'], da1, dh1, 1.0, F32)
    R['ffn1_w_up'], = _exchange("scatter_last", [dwu1], True)

    def slab(a):
        a = a.reshape(-1, 128)
        return jnp.pad(a, ((0, (-a.shape[0]) % 8), (0, 0)))

    slab_rows = lambda n: -(-(p[n].shape[-1] // 128) // 8) * 8
    packed = jnp.concatenate([slab(dsmall[n]) for n in SMALL] + [slab(dh0[:N_META]), loss_blk], axis=0)
    red = _allreduce_small(packed)
    offs = sum(slab_rows(n) for n in SMALL)
    n_small = offs
    gmeta_full = red[offs:offs + N_META * D // 128].reshape(N_META, D)
    offs += N_META * D // 128
    loss = red[offs, 0]

    grad, delta, new_m, new_v = {}, {}, {}, {}
    for n in BIG:
        outs = _adam("adam_" + n, p[n], sq(mom[n]), sq(vel[n]), g_slots=R[n])
        grad[n], delta[n], new_m[n], new_v[n] = [o.reshape(w[n].shape) for o in outs]
    pack = lambda d: jnp.concatenate([slab(d[n]) for n in SMALL], axis=0)
    outs = _adam("adam_small", pack(w), pack(mom), pack(vel), g=red[:n_small])
    offs = 0
    for n in SMALL:
        r = p[n].shape[-1] // 128
        grad[n], delta[n], new_m[n], new_v[n] = [o[offs:offs + r].reshape(w[n].shape) for o in outs]
        offs += slab_rows(n)
    dev = 4 * lax.axis_index("x") + 2 * lax.axis_index("y") + lax.axis_index("c")
    mcols = w['meta_tokens'].shape[1]
    gmeta = lax.dynamic_slice(gmeta_full, (0, dev * mcols), (N_META, mcols))
    outs = _adam("adam_meta", w['meta_tokens'], mom['meta_tokens'], vel['meta_tokens'], g=gmeta)
    grad['meta_tokens'], delta['meta_tokens'], new_m['meta_tokens'], new_v['meta_tokens'] = outs

    return (loss, dh0[BLK:][None], *[grad[n] for n in WEIGHTS], *[delta[n] for n in WEIGHTS],
            *[new_m[n] for n in WEIGHTS], *[new_v[n] for n in WEIGHTS])


def kernel(x, meta_tokens, ffn1_pre_norm, ffn1_w_gate, ffn1_w_up, ffn1_w_down, ffn1_post_norm, mix_pre_norm, w_in, ret_group_norm, mla_q_norm, mla_w_uq, mla_kv_norm, mla_w_uk, mla_w_uv, w_out, mix_post_norm, ffn2_pre_norm, ffn2_w_gate, ffn2_w_up, ffn2_w_down, ffn2_post_norm, loss_target, m_meta_tokens, m_ffn1_pre_norm, m_ffn1_w_gate, m_ffn1_w_up, m_ffn1_w_down, m_ffn1_post_norm, m_mix_pre_norm, m_w_in, m_ret_group_norm, m_mla_q_norm, m_mla_w_uq, m_mla_kv_norm, m_mla_w_uk, m_mla_w_uv, m_w_out, m_mix_post_norm, m_ffn2_pre_norm, m_ffn2_w_gate, m_ffn2_w_up, m_ffn2_w_down, m_ffn2_post_norm, v_meta_tokens, v_ffn1_pre_norm, v_ffn1_w_gate, v_ffn1_w_up, v_ffn1_w_down, v_ffn1_post_norm, v_mix_pre_norm, v_w_in, v_ret_group_norm, v_mla_q_norm, v_mla_w_uq, v_mla_kv_norm, v_mla_w_uk, v_mla_w_uv, v_w_out, v_mix_post_norm, v_ffn2_pre_norm, v_ffn2_w_gate, v_ffn2_w_up, v_ffn2_w_down, v_ffn2_post_norm):
    w = dict(zip(WEIGHTS, (meta_tokens, ffn1_pre_norm, ffn1_w_gate, ffn1_w_up, ffn1_w_down, ffn1_post_norm,
                           mix_pre_norm, w_in, ret_group_norm, mla_q_norm, mla_w_uq, mla_kv_norm, mla_w_uk, mla_w_uv,
                           w_out, mix_post_norm, ffn2_pre_norm, ffn2_w_gate, ffn2_w_up, ffn2_w_down, ffn2_post_norm)))
    mom = dict(zip(WEIGHTS, (m_meta_tokens, m_ffn1_pre_norm, m_ffn1_w_gate, m_ffn1_w_up, m_ffn1_w_down,
                             m_ffn1_post_norm, m_mix_pre_norm, m_w_in, m_ret_group_norm, m_mla_q_norm, m_mla_w_uq,
                             m_mla_kv_norm, m_mla_w_uk, m_mla_w_uv, m_w_out, m_mix_post_norm, m_ffn2_pre_norm,
                             m_ffn2_w_gate, m_ffn2_w_up, m_ffn2_w_down, m_ffn2_post_norm)))
    vel = dict(zip(WEIGHTS, (v_meta_tokens, v_ffn1_pre_norm, v_ffn1_w_gate, v_ffn1_w_up, v_ffn1_w_down,
                             v_ffn1_post_norm, v_mix_pre_norm, v_w_in, v_ret_group_norm, v_mla_q_norm, v_mla_w_uq,
                             v_mla_kv_norm, v_mla_w_uk, v_mla_w_uv, v_w_out, v_mix_post_norm, v_ffn2_pre_norm,
                             v_ffn2_w_gate, v_ffn2_w_up, v_ffn2_w_down, v_ffn2_post_norm)))
    return _step(x, loss_target, w, mom, vel)
```

```python
import functools
import math

import jax
import jax.numpy as jnp
from jax import lax
from jax.experimental import pallas as pl
from jax.experimental.pallas import tpu as pltpu

N_DEV = 8
N_META = 16
BLK = 128
HEADS = 8
HD = 128
ROPE = 64
Q_RANK = 512
KV_RANK = 256
QH = 2 * HD
D_INP = 4 * HEADS * HD + Q_RANK + KV_RANK + BLK
ROPE_THETA = 10000.0
EPS = 1e-6
ADAM_LR = 0.001
ADAM_B1 = 0.9
ADAM_B2 = 0.999
ADAM_EPS = 1e-08
ADAM_WD = 0.01
ADAM_STEP = 10
V7X_VMEM_LIMIT = 48 * 1024 * 1024
MESH = pl.DeviceIdType.MESH
F32 = jnp.float32
BF16 = jnp.bfloat16

WEIGHTS = ['meta_tokens', 'ffn1_pre_norm', 'ffn1_w_gate', 'ffn1_w_up', 'ffn1_w_down', 'ffn1_post_norm',
           'mix_pre_norm', 'w_in', 'ret_group_norm', 'mla_q_norm', 'mla_w_uq', 'mla_kv_norm', 'mla_w_uk',
           'mla_w_uv', 'w_out', 'mix_post_norm', 'ffn2_pre_norm', 'ffn2_w_gate', 'ffn2_w_up', 'ffn2_w_down',
           'ffn2_post_norm']
SMALL = ['ffn1_pre_norm', 'ffn1_post_norm', 'mix_pre_norm', 'ret_group_norm', 'mla_q_norm', 'mla_kv_norm',
         'mix_post_norm', 'ffn2_pre_norm', 'ffn2_post_norm']
BIG = ['ffn1_w_gate', 'ffn1_w_up', 'ffn1_w_down', 'w_in', 'mla_w_uq', 'mla_w_uk', 'mla_w_uv', 'w_out',
       'ffn2_w_gate', 'ffn2_w_up', 'ffn2_w_down']

_DIMS = {'nn': (((1,), (0,)), ((), ())), 'nt': (((1,), (1,)), ((), ())), 'tn': (((0,), (0,)), ((), ()))}


def _tile(n, target, mult=16):
    best = None
    for t in range(mult, min(n, target) + 1, mult):
        if n % t == 0:
            best = t
    return best if best is not None else n


def _params(sem):
    return pltpu.CompilerParams(dimension_semantics=sem, vmem_limit_bytes=V7X_VMEM_LIMIT)


def _dot(a, b, dims):
    return lax.dot_general(a, b, _DIMS[dims], preferred_element_type=F32)


def _sigmoid(x):
    return 0.5 * jnp.tanh(0.5 * x) + 0.5


def _me_and_peers():
    x, y, c = lax.axis_index("x"), lax.axis_index("y"), lax.axis_index("c")

    def peer(j):
        px = 1 - x if (j >> 2) & 1 else x
        py = 1 - y if (j >> 1) & 1 else y
        pc = 1 - c if j & 1 else c
        return (px, py, pc), 4 * px + 2 * py + pc

    return 4 * x + 2 * y + c, peer


class _Exchange:
    def __init__(self, arrays, per_peer):
        self.arrays = list(arrays)
        self.per_peer = per_peer
        self.n = len(self.arrays)
        self.out_shapes = [jax.ShapeDtypeStruct((N_DEV,) + tuple(a.shape[1:] if per_peer else a.shape), a.dtype)
                           for a in self.arrays]
        self.specs = [pl.BlockSpec(memory_space=pl.ANY)] * self.n
        self.scratch = [pltpu.SemaphoreType.DMA((7 * self.n,)), pltpu.SemaphoreType.DMA((7 * self.n,)),
                        pltpu.SemaphoreType.DMA((self.n,))]

    def _copies(self, src, dst, sems):
        send_sems, recv_sems, local_sems = sems
        me, peer = _me_and_peers()
        sib, _ = peer(1)
        local, sends, recvs, passes = [], {}, {}, {}
        for k in range(self.n):
            own = src[k].at[me] if self.per_peer else src[k]
            local.append(pltpu.make_async_copy(own, dst[k].at[me], local_sems.at[k]))
            for j in range(1, N_DEV):
                pid, pidx = peer(j)
                out = src[k].at[pidx] if self.per_peer else src[k]
                sem = dict(send_sem=send_sems.at[k * 7 + j - 1], recv_sem=recv_sems.at[k * 7 + j - 1])
                recvs[k, j] = pltpu.make_async_remote_copy(src_ref=out, dst_ref=dst[k].at[pidx], device_id=pid,
                                                           device_id_type=MESH, **sem)
                if self.per_peer or j in (1, 2, 4, 6):
                    sends[k, j] = pltpu.make_async_remote_copy(src_ref=out, dst_ref=dst[k].at[me], device_id=pid,
                                                               device_id_type=MESH, **sem)
                else:
                    _, origin = peer(j ^ 1)
                    passes[k, j ^ 1] = pltpu.make_async_remote_copy(
                        src_ref=dst[k].at[origin], dst_ref=dst[k].at[origin], device_id=sib, device_id_type=MESH, **sem)
        return local, sends, recvs, passes

    def start(self, src, dst, sems):
        local, sends, _, _ = self._copies(src, dst, sems)
        for cp in local + list(sends.values()):
            cp.start()

    def finish(self, src, dst, sems):
        local, sends, recvs, passes = self._copies(src, dst, sems)
        for key, cp in passes.items():
            recvs[key].wait_recv()
            cp.start()
        for key, cp in recvs.items():
            if key not in passes:
                cp.wait_recv()
        for cp in list(sends.values()) + list(passes.values()):
            cp.wait_send()
        for cp in local:
            cp.wait()


def _grid_edges(grid):
    first, last = None, None
    for a, n in enumerate(grid):
        f, l = pl.program_id(a) == 0, pl.program_id(a) == n - 1
        first = f if first is None else first & f
        last = l if last is None else last & l
    return first, last


def _exchange(name, arrays, per_peer):
    ex = _Exchange(arrays, per_peer)
    n = ex.n

    def body(*refs):
        ex.start(refs[:n], refs[n:2 * n], refs[2 * n:])
        ex.finish(refs[:n], refs[n:2 * n], refs[2 * n:])

    return pl.pallas_call(body, name=name, out_shape=ex.out_shapes, in_specs=ex.specs, out_specs=ex.specs,
                          scratch_shapes=ex.scratch)(*arrays)


def _allreduce_small(v):
    rows = v.shape[0]

    def body(v_ref, out_ref, buf, send_sems, recv_sems):
        me, peer = _me_and_peers()
        buf[pl.ds(me, 1)] = v_ref[...][None]
        sends = []
        for j in range(1, N_DEV):
            pid, _ = peer(j)
            cp = pltpu.make_async_remote_copy(src_ref=v_ref, dst_ref=buf.at[me], send_sem=send_sems.at[j - 1],
                                              recv_sem=recv_sems.at[j - 1], device_id=pid, device_id_type=MESH)
            cp.start()
            sends.append(cp)
        for j in range(1, N_DEV):
            pid, pidx = peer(j)
            pltpu.make_async_remote_copy(src_ref=v_ref, dst_ref=buf.at[pidx], send_sem=send_sems.at[j - 1],
                                         recv_sem=recv_sems.at[j - 1], device_id=pid,
                                         device_id_type=MESH).wait_recv()
        for cp in sends:
            cp.wait_send()
        acc = buf[0]
        for s in range(1, N_DEV):
            acc = acc + buf[s]
        out_ref[...] = acc

    vm = pl.BlockSpec(memory_space=pltpu.VMEM)
    return pl.pallas_call(
        body, name="allreduce_small", out_shape=jax.ShapeDtypeStruct(v.shape, F32),
        in_specs=[vm], out_specs=vm,
        scratch_shapes=[pltpu.VMEM((N_DEV, rows, 128), F32), pltpu.SemaphoreType.DMA((7,)),
                        pltpu.SemaphoreType.DMA((7,))],
    )(v)


def _mm(name, grid, sem, k_axis, ops, op_specs, pairs, acc_shapes, extras, extra_specs, epilogue, outs, out_specs,
        comm=None):
    n_op, n_ex, n_out = len(ops), len(extras), len(outs)
    nk = grid[k_axis] if k_axis is not None else 1
    n_acc = len(acc_shapes) if nk > 1 else 0
    n_cm = comm.n if comm is not None else 0

    def body(*refs):
        op_refs = refs[:n_op]
        ex_refs = refs[n_op:n_op + n_ex]
        n_in = n_op + n_ex + n_cm
        out_refs = refs[n_in:n_in + n_out]
        acc_refs = refs[n_in + n_out + n_cm:n_in + n_out + n_cm + n_acc]
        if comm is not None:
            cm_refs = (refs[n_op + n_ex:n_in], refs[n_in + n_out:n_in + n_out + n_cm],
                       refs[n_in + n_out + n_cm + n_acc:])
            first, last = _grid_edges(grid)

            @pl.when(first)
            def _():
                comm.start(*cm_refs)

        def finish(vals):
            res = epilogue(*vals, *[e[...] for e in ex_refs])
            for o, r in zip(out_refs, res):
                o[...] = r.astype(o.dtype)

        if nk == 1:
            parts = [None] * len(acc_shapes)
            for li, ri, dims, ai in pairs:
                d = _dot(op_refs[li][...], op_refs[ri][...], dims)
                parts[ai] = d if parts[ai] is None else parts[ai] + d
            finish(parts)
        else:
            k = pl.program_id(k_axis)

            @pl.when(k == 0)
            def _():
                for a in acc_refs:
                    a[...] = jnp.zeros_like(a)

            for li, ri, dims, ai in pairs:
                acc_refs[ai][...] += _dot(op_refs[li][...], op_refs[ri][...], dims)

            @pl.when(k == nk - 1)
            def _():
                finish([a[...] for a in acc_refs])

        if comm is not None:
            @pl.when(last)
            def _():
                comm.finish(*cm_refs)

    scratch = [pltpu.VMEM(s, F32) for s in acc_shapes] if nk > 1 else []
    if comm is None:
        return pl.pallas_call(
            body, name=name, grid=grid, out_shape=outs,
            in_specs=list(op_specs) + list(extra_specs), out_specs=list(out_specs),
            scratch_shapes=scratch, compiler_params=_params(sem),
        )(*ops, *extras)
    res = pl.pallas_call(
        body, name=name, grid=grid, out_shape=list(outs) + comm.out_shapes,
        in_specs=list(op_specs) + list(extra_specs) + comm.specs, out_specs=list(out_specs) + comm.specs,
        scratch_shapes=scratch + comm.scratch, compiler_params=_params(("arbitrary",) * len(grid)),
    )(*ops, *extras, *comm.arrays)
    return res[:n_out], res[n_out:]


def _with_comm(res, comm, pick):
    if comm is None:
        return pick(res)
    return pick(res[0]), res[1]


def _mm_nn(name, a, w, out_dtype, tm_target=704, tn_target=1664, epilogue=None, extras=(), extra_specs=(), comm=None):
    L, K = a.shape
    N = w.shape[1]
    tm, tn = _tile(L, tm_target), _tile(N, tn_target, 128)
    ep = epilogue if epilogue is not None else (lambda acc: (acc,))
    res = _mm(name, (L // tm, N // tn), ("parallel", "parallel"), None,
              [a, w], [pl.BlockSpec((tm, K), lambda i, j: (i, 0)), pl.BlockSpec((K, tn), lambda i, j: (0, j))],
              [(0, 1, 'nn', 0)], [(tm, tn)], list(extras), list(extra_specs), ep,
              [jax.ShapeDtypeStruct((L, N), out_dtype)], [pl.BlockSpec((tm, tn), lambda i, j: (i, j))], comm=comm)
    return _with_comm(res, comm, lambda o: o[0])


def _mm_nt(name, pairs_aw, out_dtype, tm_target=704, tn_target=512, comm=None):
    L = pairs_aw[0][0].shape[0]
    N = pairs_aw[0][1].shape[0]
    tm, tn = _tile(L, tm_target), _tile(N, tn_target, 128)
    ops, specs, pairs = [], [], []
    for t, (a, w) in enumerate(pairs_aw):
        K = a.shape[1]
        ops += [a, w]
        specs += [pl.BlockSpec((tm, K), lambda i, j: (i, 0)), pl.BlockSpec((tn, K), lambda i, j: (j, 0))]
        pairs.append((2 * t, 2 * t + 1, 'nt', 0))
    res = _mm(name, (L // tm, N // tn), ("parallel", "parallel"), None, ops, specs, pairs, [(tm, tn)], [], [],
              lambda acc: (acc,), [jax.ShapeDtypeStruct((L, N), out_dtype)],
              [pl.BlockSpec((tm, tn), lambda i, j: (i, j))], comm=comm)
    return _with_comm(res, comm, lambda o: o[0])


def _mm_tn(name, a, bs, out_dtype=BF16, tk_target=704, tn_target=1664, comm=None):
    L, M = a.shape
    N = bs[0].shape[1]
    tk, tn = _tile(L, tk_target), _tile(N, tn_target, 128)
    nb = len(bs)
    ops = [a] + list(bs)
    specs = [pl.BlockSpec((tk, M), lambda j, k: (k, 0))] + [pl.BlockSpec((tk, tn), lambda j, k: (k, j))] * nb
    res = _mm(name, (N // tn, L // tk), ("parallel", "arbitrary"), 1, ops, specs,
              [(0, 1 + t, 'tn', t) for t in range(nb)], [(M, tn)] * nb, [], [], lambda *acc: acc,
              [jax.ShapeDtypeStruct((M, N), out_dtype)] * nb, [pl.BlockSpec((M, tn), lambda j, k: (0, j))] * nb,
              comm=comm)
    return _with_comm(res, comm, lambda o: o)


def _norm_fwd(x, w):
    L, D = x.shape
    tr = _tile(L, 512)

    def body(x_ref, w_ref, y_ref):
        v = x_ref[...]
        r = lax.rsqrt(jnp.mean(v * v, axis=-1, keepdims=True) + EPS)
        y_ref[...] = (v * r * w_ref[...]).astype(y_ref.dtype)

    return pl.pallas_call(
        body, name="norm_fwd", grid=(L // tr,), out_shape=jax.ShapeDtypeStruct((L, D), BF16),
        in_specs=[pl.BlockSpec((tr, D), lambda i: (i, 0)), pl.BlockSpec((1, D), lambda i: (0, 0))],
        out_specs=pl.BlockSpec((tr, D), lambda i: (i, 0)), compiler_params=_params(("parallel",)),
    )(x, w)


def _norm_bwd_math(x, w, dy):
    r = lax.rsqrt(jnp.mean(x * x, axis=-1, keepdims=True) + EPS)
    gy = dy * w
    dx = r * (gy - x * (r * r) * jnp.mean(gy * x, axis=-1, keepdims=True))
    dw = jnp.sum(dy * x * r, axis=0, keepdims=True)
    return dx, dw


def _norm_bwd(x, w, dy, res, scale, out_dtype):
    L, D = x.shape
    tr = _tile(L, 384)
    has_res = res is not None

    def body(*refs):
        x_ref, w_ref, dy_ref = refs[:3]
        res_ref = refs[3] if has_res else None
        dx_ref, dw_ref = refs[-2:]
        dx, dw = _norm_bwd_math(x_ref[...], w_ref[...], dy_ref[...].astype(F32))
        dx = scale * dx
        if has_res:
            dx = dx + res_ref[...]
        dx_ref[...] = dx.astype(dx_ref.dtype)

        @pl.when(pl.program_id(0) == 0)
        def _():
            dw_ref[...] = jnp.zeros_like(dw_ref)

        dw_ref[...] += scale * dw

    row = pl.BlockSpec((tr, D), lambda i: (i, 0))
    vec = pl.BlockSpec((1, D), lambda i: (0, 0))
    return pl.pallas_call(
        body, name="norm_bwd", grid=(L // tr,),
        out_shape=[jax.ShapeDtypeStruct((L, D), out_dtype), jax.ShapeDtypeStruct((1, D), F32)],
        in_specs=[row, vec, row] + ([row] if has_res else []), out_specs=[row, vec],
        compiler_params=_params(("arbitrary",)),
    )(*([x, w, dy] + ([res] if has_res else [])))


def _loss(h, target):
    L, D = h.shape

    def body(h_ref, t_ref, dh_ref, loss_ref):
        i = pl.program_id(0)

        @pl.when(i == 0)
        def _():
            dh_ref[...] = jnp.zeros_like(dh_ref)
            loss_ref[...] = jnp.zeros_like(loss_ref)

        @pl.when(i > 0)
        def _():
            diff = h_ref[...] - t_ref[...]
            dh_ref[...] = diff * (1.0 / D)
            loss_ref[...] += 0.5 * jnp.sum(diff * diff) * (1.0 / D)

    return pl.pallas_call(
        body, name="loss", grid=(L // BLK,),
        out_shape=[jax.ShapeDtypeStruct((L, D), F32), jax.ShapeDtypeStruct((8, 128), F32)],
        in_specs=[pl.BlockSpec((BLK, D), lambda i: (i, 0)),
                  pl.BlockSpec((BLK, D), lambda i: (jnp.maximum(i - 1, 0), 0))],
        out_specs=[pl.BlockSpec((BLK, D), lambda i: (i, 0)), pl.BlockSpec((8, 128), lambda i: (0, 0))],
        compiler_params=_params(("arbitrary",)),
    )(h, target)


def _ffn_up(a, wg, wu, comm=None):
    L, D = a.shape
    F = wg.shape[2]
    tm = _tile(L, 704)

    def ep(g, u):
        return g, u, g * _sigmoid(g) * u

    hspec = pl.BlockSpec((None, tm, F), lambda i, j: (j, i, 0))
    wspec = pl.BlockSpec((None, D, F), lambda i, j: (j, 0, 0))
    res = _mm("ffn_up", (L // tm, N_DEV), ("parallel", "parallel"), None,
              [a, wg, wu], [pl.BlockSpec((tm, D), lambda i, j: (i, 0)), wspec, wspec],
              [(0, 1, 'nn', 0), (0, 2, 'nn', 1)], [(tm, F)] * 2, [], [], ep,
              [jax.ShapeDtypeStruct((N_DEV, L, F), BF16)] * 3, [hspec] * 3, comm=comm)
    return _with_comm(res, comm, lambda o: o)


def _ffn_gate(a, wg, comm=None):
    L, D = a.shape
    F = wg.shape[2]
    tm = _tile(L, 704)
    res = _mm("ffn_gate", (L // tm, N_DEV), ("parallel", "parallel"), None,
              [a, wg], [pl.BlockSpec((tm, D), lambda i, j: (i, 0)), pl.BlockSpec((None, D, F), lambda i, j: (j, 0, 0))],
              [(0, 1, 'nn', 0)], [(tm, F)], [], [], lambda g: (g,),
              [jax.ShapeDtypeStruct((N_DEV, L, F), BF16)], [pl.BlockSpec((None, tm, F), lambda i, j: (j, i, 0))],
              comm=comm)
    return _with_comm(res, comm, lambda o: o[0])


def _ffn_up_gated(a, wu, g, comm=None):
    L, D = a.shape
    F = wu.shape[2]
    tm = _tile(L, 704)

    def ep(u, g_):
        g32 = g_.astype(F32)
        return u, g32 * _sigmoid(g32) * u

    hspec = pl.BlockSpec((None, tm, F), lambda i, j: (j, i, 0))
    res = _mm("ffn_up_gated", (L // tm, N_DEV), ("parallel", "parallel"), None,
              [a, wu], [pl.BlockSpec((tm, D), lambda i, j: (i, 0)), pl.BlockSpec((None, D, F), lambda i, j: (j, 0, 0))],
              [(0, 1, 'nn', 0)], [(tm, F)], [g], [hspec], ep,
              [jax.ShapeDtypeStruct((N_DEV, L, F), BF16)] * 2, [hspec, hspec], comm=comm)
    return _with_comm(res, comm, lambda o: o)


def _resnorm_epilogue(scale):
    def ep(acc, h, w):
        r = lax.rsqrt(jnp.mean(acc * acc, axis=-1, keepdims=True) + EPS)
        return acc, h + scale * (acc * r * w)
    return ep


def _ffn_down(hid, wd, h_in, post, comm=None):
    _, L, F = hid.shape
    D = wd.shape[2]
    tm = _tile(L, 528)
    row = pl.BlockSpec((tm, D), lambda i, j: (i, 0))
    res = _mm("ffn_down", (L // tm, N_DEV), ("parallel", "arbitrary"), 1,
              [hid, wd], [pl.BlockSpec((None, tm, F), lambda i, j: (j, i, 0)),
                          pl.BlockSpec((None, F, D), lambda i, j: (j, 0, 0))],
              [(0, 1, 'nn', 0)], [(tm, D)], [h_in, post], [row, pl.BlockSpec((1, D), lambda i, j: (0, 0))],
              _resnorm_epilogue(0.5), [jax.ShapeDtypeStruct((L, D), F32)] * 2, [row, row], comm=comm)
    return _with_comm(res, comm, lambda o: o)


def _ffn_dhid(df, wd, g, u, comm=None):
    L, D = df.shape
    F = wd.shape[1]
    tm = _tile(L, 704)

    def ep(dhid, g_, u_):
        g32, u32 = g_.astype(F32), u_.astype(F32)
        sg = _sigmoid(g32)
        return dhid * u32 * sg * (1.0 + g32 * (1.0 - sg)), dhid * g32 * sg

    hspec = pl.BlockSpec((None, tm, F), lambda i, j: (j, i, 0))
    res = _mm("ffn_dhid", (L // tm, N_DEV), ("parallel", "parallel"), None,
              [df, wd], [pl.BlockSpec((tm, D), lambda i, j: (i, 0)),
                         pl.BlockSpec((None, F, D), lambda i, j: (j, 0, 0))],
              [(0, 1, 'nt', 0)], [(tm, F)], [g, u], [hspec, hspec], ep,
              [jax.ShapeDtypeStruct((N_DEV, L, F), BF16)] * 2, [hspec, hspec], comm=comm)
    return _with_comm(res, comm, lambda o: o)


def _ffn_dwd(hid, df, comm=None):
    _, L, F = hid.shape
    D = df.shape[1]
    tk = _tile(L, 1408)
    res = _mm("ffn_dwd", (N_DEV, L // tk), ("parallel", "arbitrary"), 1,
              [hid, df], [pl.BlockSpec((None, tk, F), lambda j, k: (j, k, 0)),
                          pl.BlockSpec((tk, D), lambda j, k: (k, 0))],
              [(0, 1, 'tn', 0)], [(F, D)], [], [], lambda acc: (acc,),
              [jax.ShapeDtypeStruct((N_DEV, F, D), BF16)], [pl.BlockSpec((None, F, D), lambda j, k: (j, 0, 0))],
              comm=comm)
    return _with_comm(res, comm, lambda o: o[0])


def _ffn_dwgu(a, dg, du, comm=None):
    L, D = a.shape
    F = dg.shape[2]
    tk = _tile(L, 704)
    hspec = pl.BlockSpec((None, tk, F), lambda j, k: (j, k, 0))
    wspec = pl.BlockSpec((None, D, F), lambda j, k: (j, 0, 0))
    res = _mm("ffn_dwgu", (N_DEV, L // tk), ("parallel", "arbitrary"), 1,
              [a, dg, du], [pl.BlockSpec((tk, D), lambda j, k: (k, 0)), hspec, hspec],
              [(0, 1, 'tn', 0), (0, 2, 'tn', 1)], [(D, F)] * 2, [], [], lambda *acc: acc,
              [jax.ShapeDtypeStruct((N_DEV, D, F), BF16)] * 2, [wspec, wspec], comm=comm)
    return _with_comm(res, comm, lambda o: o)


def _ffn_da(dg, du, wg, wu, comm=None):
    _, L, F = dg.shape
    D = wg.shape[1]
    tm = _tile(L, 704)
    hspec = pl.BlockSpec((None, tm, F), lambda i, j: (j, i, 0))
    wspec = pl.BlockSpec((None, D, F), lambda i, j: (j, 0, 0))
    row = pl.BlockSpec((tm, D), lambda i, j: (i, 0))
    res = _mm("ffn_da", (L // tm, N_DEV), ("parallel", "arbitrary"), 1,
              [dg, du, wg, wu], [hspec, hspec, wspec, wspec],
              [(0, 2, 'nt', 0), (1, 3, 'nt', 0)], [(tm, D)], [], [], lambda acc: (acc,),
              [jax.ShapeDtypeStruct((L, D), F32)], [row], comm=comm)
    return _with_comm(res, comm, lambda o: o[0])


def _rope_tables(L):
    rows = jnp.arange(L, dtype=F32)
    pos = jnp.where(rows < BLK, rows, rows - (BLK - N_META))
    inv_r = ROPE_THETA ** (-jnp.arange(0, HD, 2, dtype=F32) / HD)
    ang_r = pos[:, None] * inv_r[None, :]
    cr = jnp.concatenate([jnp.cos(ang_r), jnp.cos(ang_r)], axis=1)
    sr = jnp.concatenate([-jnp.sin(ang_r), jnp.sin(ang_r)], axis=1)
    inv_m = ROPE_THETA ** (-jnp.arange(0, ROPE, 2, dtype=F32) / ROPE)
    ang_m = pos[:, None] * inv_m[None, :]
    z32 = jnp.zeros((L, ROPE // 2), F32)
    z64 = jnp.zeros((L, HD - ROPE), F32)
    cm = jnp.concatenate([jnp.cos(ang_m), jnp.cos(ang_m), z64], axis=1)
    sa = jnp.concatenate([-jnp.sin(ang_m), z32, z64], axis=1)
    sb = jnp.concatenate([z32, jnp.sin(ang_m), z64], axis=1)
    return cr, sr, cm, sa, sb


def _rope_ret(x, cr, sr):
    return x * cr + pltpu.roll(x, HD // 2, 1) * sr


def _rope_ret_t(d, cr, sr):
    return d * cr + pltpu.roll(d * sr, HD // 2, 1)


def _rope_mla(x, cm, sa, sb):
    return x * cm + pltpu.roll(x, HD - ROPE // 2, 1) * sa + pltpu.roll(x, ROPE // 2, 1) * sb


def _rope_mla_t(d, cm, sa, sb):
    return d * cm + pltpu.roll(d * sa, ROPE // 2, 1) + pltpu.roll(d * sb, HD - ROPE // 2, 1)


C_RQ, C_RK, C_RV, C_RG = 0, HEADS * HD, 2 * HEADS * HD, 3 * HEADS * HD
C_CQ = 4 * HEADS * HD
C_CKV = C_CQ + Q_RANK
C_KR = C_CKV + KV_RANK
RET_K_SCALE = HD ** -0.5


def _prep(proj, tabs, qn, kvn):
    L = proj.shape[0]
    tr = _tile(L, 256)
    W = HEADS * HD

    def body(p_ref, cr_ref, sr_ref, cm_ref, sa_ref, sb_ref, qn_ref, kvn_ref, q_ref, k_ref, v_ref, cq_ref, ckv_ref,
             kr_ref):
        cr, sr = cr_ref[...], sr_ref[...]
        for h in range(HEADS):
            sl = slice(h * HD, (h + 1) * HD)
            q_ref[:, sl] = _rope_ret(p_ref[:, C_RQ + h * HD:C_RQ + (h + 1) * HD], cr, sr).astype(BF16)
            k_ref[:, sl] = (_rope_ret(p_ref[:, C_RK + h * HD:C_RK + (h + 1) * HD], cr, sr)
                            * RET_K_SCALE).astype(BF16)
        v_ref[...] = p_ref[:, C_RV:C_RV + W].astype(BF16)
        cq = p_ref[:, C_CQ:C_CQ + Q_RANK]
        cq_ref[...] = (cq * lax.rsqrt(jnp.mean(cq * cq, axis=-1, keepdims=True) + EPS) * qn_ref[...]).astype(BF16)
        ckv = p_ref[:, C_CKV:C_CKV + KV_RANK]
        ckv_ref[...] = (ckv * lax.rsqrt(jnp.mean(ckv * ckv, axis=-1, keepdims=True) + EPS)
                        * kvn_ref[...]).astype(BF16)
        kr_ref[...] = _rope_mla(p_ref[:, C_KR:C_KR + HD], cm_ref[...], sa_ref[...], sb_ref[...]).astype(BF16)

    row = lambda w: pl.BlockSpec((tr, w), lambda i: (i, 0))
    vec = lambda w: pl.BlockSpec((1, w), lambda i: (0, 0))
    return pl.pallas_call(
        body, name="mix_prep", grid=(L // tr,),
        out_shape=[jax.ShapeDtypeStruct((L, W), BF16)] * 3 + [jax.ShapeDtypeStruct((L, Q_RANK), BF16),
                                                              jax.ShapeDtypeStruct((L, KV_RANK), BF16),
                                                              jax.ShapeDtypeStruct((L, HD), BF16)],
        in_specs=[row(D_INP)] + [row(HD)] * 5 + [vec(Q_RANK), vec(KV_RANK)],
        out_specs=[row(W)] * 3 + [row(Q_RANK), row(KV_RANK), row(HD)],
        compiler_params=_params(("parallel",)),
    )(proj, *tabs, qn, kvn)


def _prep_bwd(proj, dq, dk, dv, drg, dcqn, dckvn, dkr8, tabs, qn, kvn):
    L = proj.shape[0]
    tr = _tile(L, 192)
    W = HEADS * HD

    def body(p_ref, dq_ref, dk_ref, dv_ref, drg_ref, dcq_ref, dckv_ref, dkr_ref, cr_ref, sr_ref, cm_ref, sa_ref,
             sb_ref, qn_ref, kvn_ref, dp_ref, dqn_ref, dkvn_ref):
        cr, sr = cr_ref[...], sr_ref[...]
        dkr = None
        for h in range(HEADS):
            sl = slice(h * HD, (h + 1) * HD)
            dp_ref[:, C_RQ + h * HD:C_RQ + (h + 1) * HD] = _rope_ret_t(dq_ref[:, sl], cr, sr).astype(BF16)
            dp_ref[:, C_RK + h * HD:C_RK + (h + 1) * HD] = (_rope_ret_t(dk_ref[:, sl], cr, sr)
                                                            * RET_K_SCALE).astype(BF16)
            part = dkr_ref[:, sl]
            dkr = part if dkr is None else dkr + part
        dp_ref[:, C_RV:C_RV + W] = dv_ref[...].astype(BF16)
        dp_ref[:, C_RG:C_RG + W] = drg_ref[...].astype(BF16)
        dcq, dqn = _norm_bwd_math(p_ref[:, C_CQ:C_CQ + Q_RANK], qn_ref[...], dcq_ref[...])
        dp_ref[:, C_CQ:C_CQ + Q_RANK] = dcq.astype(BF16)
        dckv, dkvn = _norm_bwd_math(p_ref[:, C_CKV:C_CKV + KV_RANK], kvn_ref[...], dckv_ref[...])
        dp_ref[:, C_CKV:C_CKV + KV_RANK] = dckv.astype(BF16)
        dp_ref[:, C_KR:C_KR + HD] = _rope_mla_t(dkr, cm_ref[...], sa_ref[...], sb_ref[...]).astype(BF16)

        @pl.when(pl.program_id(0) == 0)
        def _():
            dqn_ref[...] = jnp.zeros_like(dqn_ref)
            dkvn_ref[...] = jnp.zeros_like(dkvn_ref)

        dqn_ref[...] += dqn
        dkvn_ref[...] += dkvn

    row = lambda w: pl.BlockSpec((tr, w), lambda i: (i, 0))
    vec = lambda w: pl.BlockSpec((1, w), lambda i: (0, 0))
    return pl.pallas_call(
        body, name="mix_prep_bwd", grid=(L // tr,),
        out_shape=[jax.ShapeDtypeStruct((L, D_INP), BF16), jax.ShapeDtypeStruct((1, Q_RANK), F32),
                   jax.ShapeDtypeStruct((1, KV_RANK), F32)],
        in_specs=[row(D_INP)] + [row(W)] * 4 + [row(Q_RANK), row(KV_RANK), row(W)] + [row(HD)] * 5
                 + [vec(Q_RANK), vec(KV_RANK)],
        out_specs=[row(D_INP), vec(Q_RANK), vec(KV_RANK)],
        compiler_params=_params(("arbitrary",)),
    )(proj, dq, dk, dv, drg, dcqn, dckvn, dkr8, *tabs, qn, kvn)


def _post(o_ret, proj, gn):
    L, W = o_ret.shape
    tr = _tile(L, 384)

    def body(o_ref, rg_ref, gn_ref, out_ref):
        for h in range(HEADS):
            sl = slice(h * HD, (h + 1) * HD)
            o = o_ref[:, sl]
            rg = rg_ref[:, sl]
            n = o * lax.rsqrt(jnp.mean(o * o, axis=-1, keepdims=True) + EPS)
            out_ref[:, sl] = (n * gn_ref[:, sl] * (rg * _sigmoid(rg))).astype(BF16)

    row = pl.BlockSpec((tr, W), lambda i: (i, 0))
    return pl.pallas_call(
        body, name="ret_post", grid=(L // tr,), out_shape=jax.ShapeDtypeStruct((L, W), BF16),
        in_specs=[row, pl.BlockSpec((tr, W), lambda i: (i, C_RG // W)), pl.BlockSpec((1, W), lambda i: (0, 0))],
        out_specs=row, compiler_params=_params(("parallel",)),
    )(o_ret, proj, gn)


def _post_bwd(o_ret, proj, gn, dcat):
    L, W = o_ret.shape
    tr = _tile(L, 384)

    def body(o_ref, rg_ref, gn_ref, d_ref, do_ref, drg_ref, dgn_ref):
        @pl.when(pl.program_id(0) == 0)
        def _():
            dgn_ref[...] = jnp.zeros_like(dgn_ref)

        for h in range(HEADS):
            sl = slice(h * HD, (h + 1) * HD)
            o = o_ref[:, sl]
            rg = rg_ref[:, sl]
            d = d_ref[:, sl].astype(F32)
            gw = gn_ref[:, sl]
            r = lax.rsqrt(jnp.mean(o * o, axis=-1, keepdims=True) + EPS)
            n = o * r
            sg = _sigmoid(rg)
            si = rg * sg
            dn = d * gw * si
            dgn_ref[:, sl] += jnp.sum(d * n * si, axis=0, keepdims=True)
            drg_ref[:, sl] = d * n * gw * sg * (1.0 + rg * (1.0 - sg))
            do_ref[:, sl] = (r * (dn - o * (r * r) * jnp.mean(dn * o, axis=-1, keepdims=True))).astype(BF16)

    row = pl.BlockSpec((tr, W), lambda i: (i, 0))
    vec = pl.BlockSpec((1, W), lambda i: (0, 0))
    return pl.pallas_call(
        body, name="ret_post_bwd", grid=(L // tr,),
        out_shape=[jax.ShapeDtypeStruct((L, W), BF16), jax.ShapeDtypeStruct((L, W), F32),
                   jax.ShapeDtypeStruct((1, W), F32)],
        in_specs=[row, pl.BlockSpec((tr, W), lambda i: (i, C_RG // W)), vec, row],
        out_specs=[row, row, vec], compiler_params=_params(("arbitrary",)),
    )(o_ret, proj, gn, dcat)


RET_HEADS_PER_STEP = 4


def _lin_attn(name, q, k, v, lg, reverse):
    L, W = q.shape
    nc = L // BLK - 1
    G = RET_HEADS_PER_STEP

    def body(q_ref, k_ref, v_ref, lg_ref, o_ref, s_ref):
        n = lax.broadcasted_iota(jnp.int32, (BLK, BLK), 0).astype(F32)
        m = lax.broadcasted_iota(jnp.int32, (BLK, BLK), 1).astype(F32)
        dist = (m - n) if reverse else (n - m)
        consts = []
        for g in range(G):
            lgv = lg_ref[g, 0:1, :]
            dmask = jnp.where(dist >= 0, jnp.exp(lgv * jnp.maximum(dist, 0.0)), 0.0)
            c = dict(dmask=dmask, dmask0=jnp.where((n < N_META) & (m < N_META), dmask, 0.0),
                     gl=jnp.exp(lgv * float(BLK)))
            if reverse:
                c.update(inter=jnp.exp(lgv * (float(BLK) - n)), upd=jnp.exp(lgv * n),
                         inter0=jnp.where(n < N_META, jnp.exp(lgv * jnp.maximum(float(N_META) - n, 0.0)), 0.0))
            else:
                c.update(inter=jnp.exp(lgv * (n + 1.0)), upd=jnp.exp(lgv * (float(BLK) - 1.0 - n)),
                         upd0=jnp.where(n < N_META, jnp.exp(lgv * jnp.maximum(float(N_META) - 1.0 - n, 0.0)), 0.0))
            consts.append(c)

        def chunk(c):
            rows = pl.ds(pl.multiple_of(c * BLK, BLK), BLK)
            state = [s_ref[g] for g in range(G)]
            outs, new_state = [], []
            for g in range(G):
                cols = slice(g * HD, (g + 1) * HD)
                cg = consts[g]
                qc, kc, vc = q_ref[rows, cols], k_ref[rows, cols], v_ref[rows, cols]
                a = _dot(qc, kc, 'nt') * cg['dmask']
                outs.append(_dot(a.astype(BF16), vc, 'nn') + _dot(qc, state[g].astype(BF16), 'nn') * cg['inter'])
                new_state.append(state[g] * cg['gl'] + _dot((kc.astype(F32) * cg['upd']).astype(BF16), vc, 'tn'))
            for g in range(G):
                o_ref[rows, g * HD:(g + 1) * HD] = outs[g]
                s_ref[g] = new_state[g]

        def first_chunk(with_state):
            for g in range(G):
                cols = slice(g * HD, (g + 1) * HD)
                cg = consts[g]
                q0, k0, v0 = q_ref[0:BLK, cols], k_ref[0:BLK, cols], v_ref[0:BLK, cols]
                o0 = _dot((_dot(q0, k0, 'nt') * cg['dmask0']).astype(BF16), v0, 'nn')
                if with_state:
                    o0 = o0 + _dot(q0, s_ref[g].astype(BF16), 'nn') * cg['inter0']
                else:
                    s_ref[g] = _dot((k0.astype(F32) * cg['upd0']).astype(BF16), v0, 'tn')
                o_ref[0:BLK, cols] = o0

        if reverse:
            s_ref[...] = jnp.zeros_like(s_ref)

            def step(t, carry):
                chunk(nc - t)
                return carry

            lax.fori_loop(0, nc, step, 0)
            first_chunk(True)
        else:
            first_chunk(False)

            def step(t, carry):
                chunk(t + 1)
                return carry

            lax.fori_loop(0, nc, step, 0)

    col = pl.BlockSpec((L, G * HD), lambda h: (0, h))
    return pl.pallas_call(
        body, name=name, grid=(HEADS // G,), out_shape=jax.ShapeDtypeStruct((L, W), F32),
        in_specs=[col, col, col, pl.BlockSpec((G, 8, HD), lambda h: (h, 0, 0))], out_specs=col,
        scratch_shapes=[pltpu.VMEM((G, HD, HD), F32)], compiler_params=_params(("parallel",)),
    )(q, k, v, lg)


ATT_SCALE = (HD + ROPE) ** -0.5
LOG2E = 1.4426950408889634
Q_PRESCALE = ATT_SCALE * LOG2E
NEG = -1e30


ATT_TILE = 384
ATT_HEADS_PER_STEP = 2


def _att_valid(nq, nk, row0, col0):
    r = lax.broadcasted_iota(jnp.int32, (nq, nk), 0) + row0
    c = lax.broadcasted_iota(jnp.int32, (nq, nk), 1) + col0
    return (c <= r) & ((c < N_META) | (c >= BLK))


def _attn_fwd(qm, kn, krr, vm, comm=None):
    L = qm.shape[0]
    W = HEADS * HD
    T = _tile(L, ATT_TILE, BLK)
    nb = L // T
    G = ATT_HEADS_PER_STEP
    n_cm = comm.n if comm is not None else 0

    def body(*refs):
        q_ref, kn_ref, kr_ref, v_ref = refs[:4]
        o_ref, lse_ref = refs[4 + n_cm:6 + n_cm]
        m_sc, l_sc, acc_sc = refs[6 + 2 * n_cm:9 + 2 * n_cm]
        if comm is not None:
            cm_refs = (refs[4:4 + n_cm], refs[6 + n_cm:6 + 2 * n_cm], refs[9 + 2 * n_cm:])
            first, last = _grid_edges((HEADS // G, nb))

            @pl.when(first)
            def _():
                comm.start(*cm_refs)

        i = pl.program_id(1)
        m_sc[...] = jnp.full_like(m_sc, NEG)
        l_sc[...] = jnp.zeros_like(l_sc)
        acc_sc[...] = jnp.zeros_like(acc_sc)

        def tile(j, masked):
            rows = pl.ds(pl.multiple_of(j * T, T), T)
            kr = kr_ref[rows, :]
            valid = _att_valid(T, T, i * T, j * T) if masked else None
            m_prev = [m_sc[g] for g in range(G)]
            l_prev = [l_sc[g] for g in range(G)]
            acc_prev = [acc_sc[g] for g in range(G)]
            m_new, l_new, acc_new = [], [], []
            for g in range(G):
                k = jnp.concatenate([kn_ref[rows, g * HD:(g + 1) * HD], kr], axis=1)
                s = _dot(q_ref[:, g * QH:(g + 1) * QH], k, 'nt')
                if masked:
                    s = jnp.where(valid, s, NEG)
                m_new.append(jnp.maximum(m_prev[g], jnp.max(s, axis=-1, keepdims=True)))
                p = jnp.exp2(s - m_new[g])
                alpha = jnp.exp2(m_prev[g] - m_new[g])
                l_new.append(alpha * l_prev[g] + jnp.sum(p, axis=-1, keepdims=True))
                acc_new.append(alpha * acc_prev[g] + _dot(p.astype(BF16), v_ref[rows, g * HD:(g + 1) * HD], 'nn'))
            for g in range(G):
                m_sc[g] = m_new[g]
                l_sc[g] = l_new[g]
                acc_sc[g] = acc_new[g]

        tile(0, True)

        def mid(j, carry):
            tile(j, False)
            return carry

        lax.fori_loop(1, i, mid, 0)

        @pl.when(i > 0)
        def _():
            tile(i, True)

        for g in range(G):
            l = l_sc[g]
            o_ref[:, g * HD:(g + 1) * HD] = (acc_sc[g] / l).astype(o_ref.dtype)
            lse_ref[g] = jnp.broadcast_to(m_sc[g] + jnp.log(l) * LOG2E, (T, HD))

        if comm is not None:
            @pl.when(last)
            def _():
                comm.finish(*cm_refs)

    cm_specs = comm.specs if comm is not None else []
    res = pl.pallas_call(
        body, name="attn_fwd", grid=(HEADS // G, nb),
        out_shape=[jax.ShapeDtypeStruct((L, W), BF16), jax.ShapeDtypeStruct((HEADS, L, HD), F32)]
        + (comm.out_shapes if comm is not None else []),
        in_specs=[pl.BlockSpec((T, G * QH), lambda h, i: (i, h)), pl.BlockSpec((L, G * HD), lambda h, i: (0, h)),
                  pl.BlockSpec((L, HD), lambda h, i: (0, 0)), pl.BlockSpec((L, G * HD), lambda h, i: (0, h))]
        + cm_specs,
        out_specs=[pl.BlockSpec((T, G * HD), lambda h, i: (i, h)),
                   pl.BlockSpec((G, T, HD), lambda h, i: (h, i, 0))] + cm_specs,
        scratch_shapes=[pltpu.VMEM((G, T, 1), F32), pltpu.VMEM((G, T, 1), F32), pltpu.VMEM((G, T, HD), F32)]
        + (comm.scratch if comm is not None else []),
        compiler_params=_params(("arbitrary", "arbitrary")),
    )(qm, kn, krr, vm, *(comm.arrays if comm is not None else []))
    return res[:2], res[2:]


def _attn_bwd(qm, kn, krr, vm, o, dcat, lse, comm=None):
    L = qm.shape[0]
    W = HEADS * HD
    T = _tile(L, ATT_TILE, BLK)
    nb = L // T
    n_cm = comm.n if comm is not None else 0

    def body(*refs):
        q_ref, kn_ref, kr_ref, v_ref, o_ref, do_ref, lse_ref = refs[:7]
        dq_ref, dkn_ref, dkr_ref, dv_ref = refs[7 + n_cm:11 + n_cm]
        dl_sc, dk_sc, dv_sc = refs[11 + 2 * n_cm:14 + 2 * n_cm]
        if comm is not None:
            cm_refs = (refs[7:7 + n_cm], refs[11 + n_cm:11 + 2 * n_cm], refs[14 + 2 * n_cm:])
            first, last = _grid_edges((HEADS, nb))

            @pl.when(first)
            def _():
                comm.start(*cm_refs)

        j = pl.program_id(1)

        @pl.when(j == 0)
        def _():
            dq_ref[...] = jnp.zeros_like(dq_ref)

            def rowsum(t, carry):
                rows = pl.ds(pl.multiple_of(t * T, T), T)
                dl_sc[rows, :] = jnp.sum(do_ref[rows, :].astype(F32) * o_ref[rows, :].astype(F32), axis=-1,
                                         keepdims=True)
                return carry

            lax.fori_loop(0, nb, rowsum, 0)

        k = jnp.concatenate([kn_ref[...], kr_ref[...]], axis=1)
        v = v_ref[...]
        dk_sc[...] = jnp.zeros_like(dk_sc)
        dv_sc[...] = jnp.zeros_like(dv_sc)

        def tile(i, masked):
            rows = pl.ds(pl.multiple_of(i * T, T), T)
            q = q_ref[rows, :]
            do = do_ref[rows, :]
            s = _dot(q, k, 'nt')
            if masked:
                s = jnp.where(_att_valid(T, T, i * T, j * T), s, NEG)
            p = jnp.exp2(s - lse_ref[rows, 0:1])
            dv_sc[...] += _dot(p.astype(BF16), do, 'tn')
            ds = (p * (_dot(do, v, 'nt') - dl_sc[rows, :])).astype(BF16)
            dk_sc[...] += _dot(ds, q, 'tn')
            dq_ref[rows, :] += _dot(ds, k, 'nn')

        tile(j, True)

        def rest(masked):
            def step(i, carry):
                tile(i, masked)
                return carry
            lax.fori_loop(j + 1, nb, step, 0)

        @pl.when(j == 0)
        def _():
            rest(True)

        @pl.when(j > 0)
        def _():
            rest(False)

        dk = dk_sc[...] * (1.0 / LOG2E)
        dkn_ref[...] = dk[:, 0:HD].astype(BF16)
        dkr_ref[...] = dk[:, HD:QH]
        dv_ref[...] = dv_sc[...].astype(BF16)

        if comm is not None:
            @pl.when(last)
            def _():
                comm.finish(*cm_refs)

    blk = pl.BlockSpec((T, HD), lambda h, j: (j, h))
    cm_specs = comm.specs if comm is not None else []
    res = pl.pallas_call(
        body, name="attn_bwd", grid=(HEADS, nb),
        out_shape=[jax.ShapeDtypeStruct((L, HEADS * QH), F32), jax.ShapeDtypeStruct((L, W), BF16),
                   jax.ShapeDtypeStruct((L, W), F32), jax.ShapeDtypeStruct((L, W), BF16)]
        + (comm.out_shapes if comm is not None else []),
        in_specs=[pl.BlockSpec((L, QH), lambda h, j: (0, h)), blk, pl.BlockSpec((T, HD), lambda h, j: (j, 0)), blk,
                  pl.BlockSpec((L, HD), lambda h, j: (0, h)), pl.BlockSpec((L, HD), lambda h, j: (0, HEADS + h)),
                  pl.BlockSpec((None, L, HD), lambda h, j: (h, 0, 0))] + cm_specs,
        out_specs=[pl.BlockSpec((L, QH), lambda h, j: (0, h)), blk, blk, blk] + cm_specs,
        scratch_shapes=[pltpu.VMEM((L, 1), F32), pltpu.VMEM((T, QH), F32), pltpu.VMEM((T, HD), F32)]
        + (comm.scratch if comm is not None else []),
        compiler_params=_params(("arbitrary", "arbitrary")),
    )(qm, kn, krr, vm, o, dcat, lse, *(comm.arrays if comm is not None else []))
    return res[:4], res[4:]


def _unrope_q(dqm, tabs_m):
    L, W = dqm.shape
    tr = _tile(L, 384)

    def body(d_ref, cm_ref, sa_ref, sb_ref, out_ref):
        cm, sa, sb = cm_ref[...], sa_ref[...], sb_ref[...]
        for h in range(HEADS):
            out_ref[:, h * QH:h * QH + HD] = (d_ref[:, h * QH:h * QH + HD] * ATT_SCALE).astype(BF16)
            out_ref[:, h * QH + HD:(h + 1) * QH] = _rope_mla_t(d_ref[:, h * QH + HD:(h + 1) * QH] * ATT_SCALE, cm, sa,
                                                               sb).astype(BF16)

    row = pl.BlockSpec((tr, W), lambda i: (i, 0))
    tab = pl.BlockSpec((tr, HD), lambda i: (i, 0))
    return pl.pallas_call(
        body, name="unrope_q", grid=(L // tr,), out_shape=jax.ShapeDtypeStruct((L, W), BF16),
        in_specs=[row, tab, tab, tab], out_specs=row, compiler_params=_params(("parallel",)),
    )(dqm, *tabs_m)


def _q_up(cqn, wuq_p, tabs_m):
    L = cqn.shape[0]
    tm = _tile(L, 704)

    def ep(acc, cm, sa, sb):
        acc = acc * Q_PRESCALE
        parts = []
        for h in range(HEADS):
            parts.append(acc[:, h * QH:h * QH + HD])
            parts.append(_rope_mla(acc[:, h * QH + HD:(h + 1) * QH], cm, sa, sb))
        return (jnp.concatenate(parts, axis=1),)

    tab = pl.BlockSpec((tm, HD), lambda i, j: (i, 0))
    return _mm("mla_q_up", (L // tm, 1), ("parallel", "parallel"), None,
               [cqn, wuq_p], [pl.BlockSpec((tm, Q_RANK), lambda i, j: (i, 0)),
                              pl.BlockSpec((Q_RANK, HEADS * QH), lambda i, j: (0, 0))],
               [(0, 1, 'nn', 0)], [(tm, HEADS * QH)], list(tabs_m), [tab] * 3, ep,
               [jax.ShapeDtypeStruct((L, HEADS * QH), BF16)], [pl.BlockSpec((tm, HEADS * QH), lambda i, j: (i, 0))])[0]


def _mix_out(cat, w_out, h_in, post):
    L, K = cat.shape
    D = w_out.shape[1]
    tm, tk = _tile(L, 384), _tile(K, 512, 128)
    row = pl.BlockSpec((tm, D), lambda i, k: (i, 0))
    return _mm("mix_out", (L // tm, K // tk), ("parallel", "arbitrary"), 1,
               [cat, w_out], [pl.BlockSpec((tm, tk), lambda i, k: (i, k)), pl.BlockSpec((tk, D), lambda i, k: (k, 0))],
               [(0, 1, 'nn', 0)], [(tm, D)], [h_in, post], [row, pl.BlockSpec((1, D), lambda i, k: (0, 0))],
               _resnorm_epilogue(1.0), [jax.ShapeDtypeStruct((L, D), F32)] * 2, [row, row])


ADAM_BLOCK_ELEMS = 512 * 704


def _adam_math(w, g, m, v):
    m = ADAM_B1 * m + (1.0 - ADAM_B1) * g
    v = ADAM_B2 * v + (1.0 - ADAM_B2) * (g * g)
    m_hat = m / (1.0 - ADAM_B1 ** ADAM_STEP)
    v_hat = v / (1.0 - ADAM_B2 ** ADAM_STEP)
    delta = -ADAM_LR * (m_hat / (jnp.sqrt(v_hat) + ADAM_EPS) + ADAM_WD * w)
    return delta, m, v


def _adam(name, w, m, v, g_slots=None, g=None):
    R, C = w.shape
    tr = _tile(R, max(16, ADAM_BLOCK_ELEMS // C // 16 * 16), 16)
    from_slots = g_slots is not None

    def body(w_ref, m_ref, v_ref, g_ref, go_ref, d_ref, mo_ref, vo_ref):
        if from_slots:
            grad = g_ref[0].astype(F32)
            for s in range(1, N_DEV):
                grad = grad + g_ref[s].astype(F32)
        else:
            grad = g_ref[...]
        delta, mn, vn = _adam_math(w_ref[...], grad, m_ref[...], v_ref[...])
        go_ref[...] = grad
        d_ref[...] = delta
        mo_ref[...] = mn
        vo_ref[...] = vn

    row = pl.BlockSpec((tr, C), lambda i: (i, 0))
    gspec = pl.BlockSpec((N_DEV, tr, C), lambda i: (0, i, 0)) if from_slots else row
    return pl.pallas_call(
        body, name=name, grid=(R // tr,), out_shape=[jax.ShapeDtypeStruct((R, C), F32)] * 4,
        in_specs=[row, row, row, gspec], out_specs=[row] * 4, compiler_params=_params(("parallel",)),
    )(w, m, v, g_slots if from_slots else g)


def _unblock(gathered):
    n, r, c = gathered.shape
    return jnp.transpose(gathered, (1, 0, 2)).reshape(r, n * c)


def _reblock(full, c):
    r = full.shape[0]
    return jnp.transpose(full[:, :N_DEV * c].reshape(r, N_DEV, c), (1, 0, 2))


def _step(x, target, w, mom, vel):
    S, D = x.shape[1], x.shape[2]
    L = S + BLK
    sq = lambda a: a.reshape(a.shape[1:]) if a.ndim == 3 else a
    p = {n: sq(w[n]) for n in WEIGHTS if n != 'meta_tokens'}
    gather = lambda names: _Exchange([p[n].astype(BF16) for n in names], False)
    scatter = lambda blocks: _Exchange(blocks, True)
    in_s, uq_s = p['w_in'].shape[1], p['mla_w_uq'].shape[1]
    tabs = _rope_tables(L)
    tabs_m = tabs[2:]
    lg = jnp.broadcast_to(jnp.log(1.0 - 2.0 ** (-5.0 - jnp.arange(HEADS, dtype=F32)))[:, None, None], (HEADS, 8, HD))
    R = {}

    wg1, meta = _exchange("gather_first", [p['ffn1_w_gate'].astype(BF16), w['meta_tokens']], False)
    h0 = jnp.concatenate([_unblock(meta), jnp.zeros((BLK - N_META, D), F32), x[0]], axis=0)
    a1 = _norm_fwd(h0, p['ffn1_pre_norm'])
    g1, (wu1,) = _ffn_gate(a1, wg1, comm=gather(['ffn1_w_up']))
    (u1, hid1), (wd1,) = _ffn_up_gated(a1, wu1, g1, comm=gather(['ffn1_w_down']))
    (f1, h1), (w_in_g,) = _ffn_down(hid1, wd1, h0, p['ffn1_post_norm'], comm=gather(['w_in']))

    w_in_full = _unblock(w_in_g)
    w_in = jnp.pad(w_in_full, ((0, 0), (0, D_INP - w_in_full.shape[1])))
    um = _norm_fwd(h1, p['mix_pre_norm'])
    proj, (uq_g, uk_g, uv_g, wout_g) = _mm_nn("mix_in", um, w_in, F32,
                                              comm=gather(['mla_w_uq', 'mla_w_uk', 'mla_w_uv', 'w_out']))
    wuq = jnp.pad(_unblock(uq_g).reshape(Q_RANK, HEADS, HD + ROPE),
                  ((0, 0), (0, 0), (0, QH - HD - ROPE))).reshape(Q_RANK, HEADS * QH)
    wuk, wuv, w_out = _unblock(uk_g), _unblock(uv_g), wout_g.reshape(-1, D)
    qr, kr, vr, cqn, ckvn, krr = _prep(proj, tabs, p['mla_q_norm'], p['mla_kv_norm'])
    qm = _q_up(cqn, wuq, tabs_m)
    kn = _mm_nn("mla_k_up", ckvn, wuk, BF16)
    vm = _mm_nn("mla_v_up", ckvn, wuv, BF16)
    (o_mla, lse), (wg2, wu2) = _attn_fwd(qm, kn, krr, vm, comm=gather(['ffn2_w_gate', 'ffn2_w_up']))
    o_ret = _lin_attn("ret_fwd", qr, kr, vr, lg, False)
    ret = _post(o_ret, proj, p['ret_group_norm'])
    cat = jnp.concatenate([ret, o_mla], axis=1)
    m, h2 = _mix_out(cat, w_out, h1, p['mix_post_norm'])

    a2 = _norm_fwd(h2, p['ffn2_pre_norm'])
    (g2, u2, hid2), (wd2,) = _ffn_up(a2, wg2, wu2, comm=gather(['ffn2_w_down']))
    f2, h3 = _ffn_down(hid2, wd2, h2, p['ffn2_post_norm'])
    dh3, loss_blk = _loss(h3, target[0])

    dsmall = {}
    df2, dsmall['ffn2_post_norm'] = _norm_bwd(f2, p['ffn2_post_norm'], dh3, None, 0.5, BF16)
    dg2, du2 = _ffn_dhid(df2, wd2, g2, u2)
    dwd2 = _ffn_dwd(hid2, df2)
    (dwg2, dwu2), (R['ffn2_w_down'],) = _ffn_dwgu(a2, dg2, du2, comm=scatter([dwd2]))
    da2, (R['ffn2_w_gate'],) = _ffn_da(dg2, du2, wg2, wu2, comm=scatter([dwg2]))
    dh2, dsmall['ffn2_pre_norm'] = _norm_bwd(h2, p['ffn2_pre_norm'], da2, dh3, 1.0, F32)

    dm, dsmall['mix_post_norm'] = _norm_bwd(m, p['mix_post_norm'], dh2, None, 1.0, BF16)
    dcat = _mm_nt("mix_dcat", [(dm, w_out)], BF16)
    dwout = _mm_tn("mix_dwout", cat, [dm])[0]
    do_ret, drg, dsmall['ret_group_norm'] = _post_bwd(o_ret, proj, p['ret_group_norm'], dcat)
    dqr = _lin_attn("ret_dq", do_ret, vr, kr, lg, False)
    dkr = _lin_attn("ret_dk", vr, do_ret, qr, lg, True)
    dvr = _lin_attn("ret_dv", kr, qr, do_ret, lg, True)
    (dqm, dkn, dkr8, dvm), (R['ffn2_w_up'], R['w_out']) = _attn_bwd(
        qm, kn, krr, vm, o_mla, dcat, lse, comm=scatter([dwu2, dwout.reshape(N_DEV, -1, D)]))
    dqp = _unrope_q(dqm, tabs_m)
    dwuq = _mm_tn("mla_dwuq", cqn, [dqp])[0]
    dcqn = _mm_nt("mla_dcq", [(dqp, wuq)], F32)
    dwuk, dwuv = _mm_tn("mla_dwukv", ckvn, [dkn, dvm])
    dckvn = _mm_nt("mla_dckv", [(dkn, wuk), (dvm, wuv)], F32)
    dproj, dsmall['mla_q_norm'], dsmall['mla_kv_norm'] = _prep_bwd(
        proj, dqr, dkr, dvr, drg, dcqn, dckvn, dkr8, tabs, p['mla_q_norm'], p['mla_kv_norm'])
    dwuq_b = _reblock(dwuq.reshape(Q_RANK, HEADS, QH)[:, :, :HD + ROPE].reshape(Q_RANK, HEADS * (HD + ROPE)), uq_s)
    (dwin,), (R['mla_w_uq'], R['mla_w_uk'], R['mla_w_uv']) = _mm_tn(
        "mix_dwin", um, [dproj], comm=scatter([dwuq_b, _reblock(dwuk, p['mla_w_uk'].shape[1]),
                                               _reblock(dwuv, p['mla_w_uv'].shape[1])]))
    dwin_b = _reblock(dwin, in_s)
    half = D // 2
    dum, (r_win_a,) = _mm_nt("mix_du", [(dproj, w_in)], F32, comm=scatter([dwin_b[:, :half]]))
    dh1, dsmall['mix_pre_norm'] = _norm_bwd(h1, p['mix_pre_norm'], dum, dh2, 1.0, F32)

    df1, dsmall['ffn1_post_norm'] = _norm_bwd(f1, p['ffn1_post_norm'], dh1, None, 0.5, BF16)
    (dg1, du1), (r_win_b,) = _ffn_dhid(df1, wd1, g1, u1, comm=scatter([dwin_b[:, half:]]))
    R['w_in'] = jnp.concatenate([r_win_a, r_win_b], axis=1)
    dwd1 = _ffn_dwd(hid1, df1)
    (dwg1, dwu1), (R['ffn1_w_down'],) = _ffn_dwgu(a1, dg1, du1, comm=scatter([dwd1]))
    da1, (R['ffn1_w_gate'],) = _ffn_da(dg1, du1, wg1, wu1, comm=scatter([dwg1]))
    dh0, dsmall['ffn1_pre_norm'] = _norm_bwd(h0, p['ffn1_pre_norm'], da1, dh1, 1.0, F32)
    R['ffn1_w_up'], = _exchange("scatter_last", [dwu1], True)

    def slab(a):
        a = a.reshape(-1, 128)
        return jnp.pad(a, ((0, (-a.shape[0]) % 8), (0, 0)))

    slab_rows = lambda n: -(-(p[n].shape[-1] // 128) // 8) * 8
    packed = jnp.concatenate([slab(dsmall[n]) for n in SMALL] + [slab(dh0[:N_META]), loss_blk], axis=0)
    red = _allreduce_small(packed)
    offs = sum(slab_rows(n) for n in SMALL)
    n_small = offs
    gmeta_full = red[offs:offs + N_META * D // 128].reshape(N_META, D)
    offs += N_META * D // 128
    loss = red[offs, 0]

    grad, delta, new_m, new_v = {}, {}, {}, {}
    for n in BIG:
        outs = _adam("adam_" + n, p[n], sq(mom[n]), sq(vel[n]), g_slots=R[n])
        grad[n], delta[n], new_m[n], new_v[n] = [o.reshape(w[n].shape) for o in outs]
    pack = lambda d: jnp.concatenate([slab(d[n]) for n in SMALL], axis=0)
    outs = _adam("adam_small", pack(w), pack(mom), pack(vel), g=red[:n_small])
    offs = 0
    for n in SMALL:
        r = p[n].shape[-1] // 128
        grad[n], delta[n], new_m[n], new_v[n] = [o[offs:offs + r].reshape(w[n].shape) for o in outs]
        offs += slab_rows(n)
    dev = 4 * lax.axis_index("x") + 2 * lax.axis_index("y") + lax.axis_index("c")
    mcols = w['meta_tokens'].shape[1]
    gmeta = lax.dynamic_slice(gmeta_full, (0, dev * mcols), (N_META, mcols))
    outs = _adam("adam_meta", w['meta_tokens'], mom['meta_tokens'], vel['meta_tokens'], g=gmeta)
    grad['meta_tokens'], delta['meta_tokens'], new_m['meta_tokens'], new_v['meta_tokens'] = outs

    return (loss, dh0[BLK:][None], *[grad[n] for n in WEIGHTS], *[delta[n] for n in WEIGHTS],
            *[new_m[n] for n in WEIGHTS], *[new_v[n] for n in WEIGHTS])


def kernel(x, meta_tokens, ffn1_pre_norm, ffn1_w_gate, ffn1_w_up, ffn1_w_down, ffn1_post_norm, mix_pre_norm, w_in, ret_group_norm, mla_q_norm, mla_w_uq, mla_kv_norm, mla_w_uk, mla_w_uv, w_out, mix_post_norm, ffn2_pre_norm, ffn2_w_gate, ffn2_w_up, ffn2_w_down, ffn2_post_norm, loss_target, m_meta_tokens, m_ffn1_pre_norm, m_ffn1_w_gate, m_ffn1_w_up, m_ffn1_w_down, m_ffn1_post_norm, m_mix_pre_norm, m_w_in, m_ret_group_norm, m_mla_q_norm, m_mla_w_uq, m_mla_kv_norm, m_mla_w_uk, m_mla_w_uv, m_w_out, m_mix_post_norm, m_ffn2_pre_norm, m_ffn2_w_gate, m_ffn2_w_up, m_ffn2_w_down, m_ffn2_post_norm, v_meta_tokens, v_ffn1_pre_norm, v_ffn1_w_gate, v_ffn1_w_up, v_ffn1_w_down, v_ffn1_post_norm, v_mix_pre_norm, v_w_in, v_ret_group_norm, v_mla_q_norm, v_mla_w_uq, v_mla_kv_norm, v_mla_w_uk, v_mla_w_uv, v_w_out, v_mix_post_norm, v_ffn2_pre_norm, v_ffn2_w_gate, v_ffn2_w_up, v_ffn2_w_down, v_ffn2_post_norm):
    w = dict(zip(WEIGHTS, (meta_tokens, ffn1_pre_norm, ffn1_w_gate, ffn1_w_up, ffn1_w_down, ffn1_post_norm,
                           mix_pre_norm, w_in, ret_group_norm, mla_q_norm, mla_w_uq, mla_kv_norm, mla_w_uk, mla_w_uv,
                           w_out, mix_post_norm, ffn2_pre_norm, ffn2_w_gate, ffn2_w_up, ffn2_w_down, ffn2_post_norm)))
    mom = dict(zip(WEIGHTS, (m_meta_tokens, m_ffn1_pre_norm, m_ffn1_w_gate, m_ffn1_w_up, m_ffn1_w_down,
                             m_ffn1_post_norm, m_mix_pre_norm, m_w_in, m_ret_group_norm, m_mla_q_norm, m_mla_w_uq,
                             m_mla_kv_norm, m_mla_w_uk, m_mla_w_uv, m_w_out, m_mix_post_norm, m_ffn2_pre_norm,
                             m_ffn2_w_gate, m_ffn2_w_up, m_ffn2_w_down, m_ffn2_post_norm)))
    vel = dict(zip(WEIGHTS, (v_meta_tokens, v_ffn1_pre_norm, v_ffn1_w_gate, v_ffn1_w_up, v_ffn1_w_down,
                             v_ffn1_post_norm, v_mix_pre_norm, v_w_in, v_ret_group_norm, v_mla_q_norm, v_mla_w_uq,
                             v_mla_kv_norm, v_mla_w_uk, v_mla_w_uv, v_w_out, v_mix_post_norm, v_ffn2_pre_norm,
                             v_ffn2_w_gate, v_ffn2_w_up, v_ffn2_w_down, v_ffn2_post_norm)))
    return _step(x, loss_target, w, mom, vel)
```

```python
import functools
import math

import jax
import jax.numpy as jnp
from jax import lax
from jax.experimental import pallas as pl
from jax.experimental.pallas import tpu as pltpu

N_DEV = 8
N_META = 16
BLK = 128
HEADS = 8
HD = 128
ROPE = 64
Q_RANK = 512
KV_RANK = 256
QH = 2 * HD
D_INP = 4 * HEADS * HD + Q_RANK + KV_RANK + BLK
ROPE_THETA = 10000.0
EPS = 1e-6
ADAM_LR = 0.001
ADAM_B1 = 0.9
ADAM_B2 = 0.999
ADAM_EPS = 1e-08
ADAM_WD = 0.01
ADAM_STEP = 10
V7X_VMEM_LIMIT = 48 * 1024 * 1024
MESH = pl.DeviceIdType.MESH
F32 = jnp.float32
BF16 = jnp.bfloat16

WEIGHTS = ['meta_tokens', 'ffn1_pre_norm', 'ffn1_w_gate', 'ffn1_w_up', 'ffn1_w_down', 'ffn1_post_norm',
           'mix_pre_norm', 'w_in', 'ret_group_norm', 'mla_q_norm', 'mla_w_uq', 'mla_kv_norm', 'mla_w_uk',
           'mla_w_uv', 'w_out', 'mix_post_norm', 'ffn2_pre_norm', 'ffn2_w_gate', 'ffn2_w_up', 'ffn2_w_down',
           'ffn2_post_norm']
SMALL = ['ffn1_pre_norm', 'ffn1_post_norm', 'mix_pre_norm', 'ret_group_norm', 'mla_q_norm', 'mla_kv_norm',
         'mix_post_norm', 'ffn2_pre_norm', 'ffn2_post_norm']
TRANSPOSED = ('ffn1_w_gate', 'ffn1_w_up', 'ffn2_w_gate', 'ffn2_w_up', 'w_in', 'mla_w_uq')
BIG = ['ffn1_w_gate', 'ffn1_w_up', 'ffn1_w_down', 'w_in', 'mla_w_uq', 'mla_w_uk', 'mla_w_uv', 'w_out',
       'ffn2_w_gate', 'ffn2_w_up', 'ffn2_w_down']

_DIMS = {'nn': (((1,), (0,)), ((), ())), 'nt': (((1,), (1,)), ((), ())), 'tn': (((0,), (0,)), ((), ()))}


def _tile(n, target, mult=16):
    best = None
    for t in range(mult, min(n, target) + 1, mult):
        if n % t == 0:
            best = t
    return best if best is not None else n


def _params(sem):
    return pltpu.CompilerParams(dimension_semantics=sem, vmem_limit_bytes=V7X_VMEM_LIMIT)


def _dot(a, b, dims):
    return lax.dot_general(a, b, _DIMS[dims], preferred_element_type=F32)


def _sigmoid(x):
    return 0.5 * jnp.tanh(0.5 * x) + 0.5


def _me_and_peers():
    x, y, c = lax.axis_index("x"), lax.axis_index("y"), lax.axis_index("c")

    def peer(j):
        px = 1 - x if (j >> 2) & 1 else x
        py = 1 - y if (j >> 1) & 1 else y
        pc = 1 - c if j & 1 else c
        return (px, py, pc), 4 * px + 2 * py + pc

    return 4 * x + 2 * y + c, peer


class _Exchange:
    def __init__(self, arrays, per_peer):
        self.arrays = list(arrays)
        self.per_peer = per_peer
        self.n = len(self.arrays)
        self.out_shapes = [jax.ShapeDtypeStruct((N_DEV,) + tuple(a.shape[1:] if per_peer else a.shape), a.dtype)
                           for a in self.arrays]
        self.specs = [pl.BlockSpec(memory_space=pl.ANY)] * self.n
        self.scratch = [pltpu.SemaphoreType.DMA((7 * self.n,)), pltpu.SemaphoreType.DMA((7 * self.n,)),
                        pltpu.SemaphoreType.DMA((self.n,))]

    def _copies(self, src, dst, sems):
        send_sems, recv_sems, local_sems = sems
        me, peer = _me_and_peers()
        sib, _ = peer(1)
        local, sends, recvs, passes = [], {}, {}, {}
        for k in range(self.n):
            own = src[k].at[me] if self.per_peer else src[k]
            local.append(pltpu.make_async_copy(own, dst[k].at[me], local_sems.at[k]))
            for j in range(1, N_DEV):
                pid, pidx = peer(j)
                out = src[k].at[pidx] if self.per_peer else src[k]
                sem = dict(send_sem=send_sems.at[k * 7 + j - 1], recv_sem=recv_sems.at[k * 7 + j - 1])
                recvs[k, j] = pltpu.make_async_remote_copy(src_ref=out, dst_ref=dst[k].at[pidx], device_id=pid,
                                                           device_id_type=MESH, **sem)
                if self.per_peer or j in (1, 2, 4, 6):
                    sends[k, j] = pltpu.make_async_remote_copy(src_ref=out, dst_ref=dst[k].at[me], device_id=pid,
                                                               device_id_type=MESH, **sem)
                else:
                    _, origin = peer(j ^ 1)
                    passes[k, j ^ 1] = pltpu.make_async_remote_copy(
                        src_ref=dst[k].at[origin], dst_ref=dst[k].at[origin], device_id=sib, device_id_type=MESH, **sem)
        return local, sends, recvs, passes

    def start(self, src, dst, sems):
        local, sends, _, _ = self._copies(src, dst, sems)
        for cp in local + list(sends.values()):
            cp.start()

    def finish(self, src, dst, sems):
        local, sends, recvs, passes = self._copies(src, dst, sems)
        for key, cp in passes.items():
            recvs[key].wait_recv()
            cp.start()
        for key, cp in recvs.items():
            if key not in passes:
                cp.wait_recv()
        for cp in list(sends.values()) + list(passes.values()):
            cp.wait_send()
        for cp in local:
            cp.wait()


def _grid_edges(grid):
    first, last = None, None
    for a, n in enumerate(grid):
        f, l = pl.program_id(a) == 0, pl.program_id(a) == n - 1
        first = f if first is None else first & f
        last = l if last is None else last & l
    return first, last


def _exchange(name, arrays, per_peer):
    ex = _Exchange(arrays, per_peer)
    n = ex.n

    def body(*refs):
        ex.start(refs[:n], refs[n:2 * n], refs[2 * n:])
        ex.finish(refs[:n], refs[n:2 * n], refs[2 * n:])

    return pl.pallas_call(body, name=name, out_shape=ex.out_shapes, in_specs=ex.specs, out_specs=ex.specs,
                          scratch_shapes=ex.scratch)(*arrays)


def _allreduce_small(v):
    rows = v.shape[0]

    def body(v_ref, out_ref, buf, send_sems, recv_sems):
        me, peer = _me_and_peers()
        buf[pl.ds(me, 1)] = v_ref[...][None]
        sends = []
        for j in range(1, N_DEV):
            pid, _ = peer(j)
            cp = pltpu.make_async_remote_copy(src_ref=v_ref, dst_ref=buf.at[me], send_sem=send_sems.at[j - 1],
                                              recv_sem=recv_sems.at[j - 1], device_id=pid, device_id_type=MESH)
            cp.start()
            sends.append(cp)
        for j in range(1, N_DEV):
            pid, pidx = peer(j)
            pltpu.make_async_remote_copy(src_ref=v_ref, dst_ref=buf.at[pidx], send_sem=send_sems.at[j - 1],
                                         recv_sem=recv_sems.at[j - 1], device_id=pid,
                                         device_id_type=MESH).wait_recv()
        for cp in sends:
            cp.wait_send()
        acc = buf[0]
        for s in range(1, N_DEV):
            acc = acc + buf[s]
        out_ref[...] = acc

    vm = pl.BlockSpec(memory_space=pltpu.VMEM)
    return pl.pallas_call(
        body, name="allreduce_small", out_shape=jax.ShapeDtypeStruct(v.shape, F32),
        in_specs=[vm], out_specs=vm,
        scratch_shapes=[pltpu.VMEM((N_DEV, rows, 128), F32), pltpu.SemaphoreType.DMA((7,)),
                        pltpu.SemaphoreType.DMA((7,))],
    )(v)


def _mm(name, grid, sem, k_axis, ops, op_specs, pairs, acc_shapes, extras, extra_specs, epilogue, outs, out_specs,
        comm=None):
    n_op, n_ex, n_out = len(ops), len(extras), len(outs)
    nk = grid[k_axis] if k_axis is not None else 1
    n_acc = len(acc_shapes) if nk > 1 else 0
    n_cm = comm.n if comm is not None else 0

    def body(*refs):
        op_refs = refs[:n_op]
        ex_refs = refs[n_op:n_op + n_ex]
        n_in = n_op + n_ex + n_cm
        out_refs = refs[n_in:n_in + n_out]
        acc_refs = refs[n_in + n_out + n_cm:n_in + n_out + n_cm + n_acc]
        if comm is not None:
            cm_refs = (refs[n_op + n_ex:n_in], refs[n_in + n_out:n_in + n_out + n_cm],
                       refs[n_in + n_out + n_cm + n_acc:])
            first, last = _grid_edges(grid)

            @pl.when(first)
            def _():
                comm.start(*cm_refs)

        def finish(vals):
            res = epilogue(*vals, *[e[...] for e in ex_refs])
            for o, r in zip(out_refs, res):
                o[...] = r.astype(o.dtype)

        if nk == 1:
            parts = [None] * len(acc_shapes)
            for li, ri, dims, ai in pairs:
                d = _dot(op_refs[li][...], op_refs[ri][...], dims)
                parts[ai] = d if parts[ai] is None else parts[ai] + d
            finish(parts)
        else:
            k = pl.program_id(k_axis)

            @pl.when(k == 0)
            def _():
                for a in acc_refs:
                    a[...] = jnp.zeros_like(a)

            for li, ri, dims, ai in pairs:
                acc_refs[ai][...] += _dot(op_refs[li][...], op_refs[ri][...], dims)

            @pl.when(k == nk - 1)
            def _():
                finish([a[...] for a in acc_refs])

        if comm is not None:
            @pl.when(last)
            def _():
                comm.finish(*cm_refs)

    scratch = [pltpu.VMEM(s, F32) for s in acc_shapes] if nk > 1 else []
    if comm is None:
        return pl.pallas_call(
            body, name=name, grid=grid, out_shape=outs,
            in_specs=list(op_specs) + list(extra_specs), out_specs=list(out_specs),
            scratch_shapes=scratch, compiler_params=_params(sem),
        )(*ops, *extras)
    res = pl.pallas_call(
        body, name=name, grid=grid, out_shape=list(outs) + comm.out_shapes,
        in_specs=list(op_specs) + list(extra_specs) + comm.specs, out_specs=list(out_specs) + comm.specs,
        scratch_shapes=scratch + comm.scratch, compiler_params=_params(("arbitrary",) * len(grid)),
    )(*ops, *extras, *comm.arrays)
    return res[:n_out], res[n_out:]


def _with_comm(res, comm, pick):
    if comm is None:
        return pick(res)
    return pick(res[0]), res[1]


def _mm_nn(name, a, w, out_dtype, tm_target=704, tn_target=1664, epilogue=None, extras=(), extra_specs=(), comm=None):
    L, K = a.shape
    N = w.shape[1]
    tm, tn = _tile(L, tm_target), _tile(N, tn_target, 128)
    ep = epilogue if epilogue is not None else (lambda acc: (acc,))
    res = _mm(name, (L // tm, N // tn), ("parallel", "parallel"), None,
              [a, w], [pl.BlockSpec((tm, K), lambda i, j: (i, 0)), pl.BlockSpec((K, tn), lambda i, j: (0, j))],
              [(0, 1, 'nn', 0)], [(tm, tn)], list(extras), list(extra_specs), ep,
              [jax.ShapeDtypeStruct((L, N), out_dtype)], [pl.BlockSpec((tm, tn), lambda i, j: (i, j))], comm=comm)
    return _with_comm(res, comm, lambda o: o[0])


def _mm_nt(name, pairs_aw, out_dtype, tm_target=704, tn_target=512, comm=None):
    L = pairs_aw[0][0].shape[0]
    N = pairs_aw[0][1].shape[0]
    tm, tn = _tile(L, tm_target), _tile(N, tn_target, 128)
    ops, specs, pairs = [], [], []
    for t, (a, w) in enumerate(pairs_aw):
        K = a.shape[1]
        ops += [a, w]
        specs += [pl.BlockSpec((tm, K), lambda i, j: (i, 0)), pl.BlockSpec((tn, K), lambda i, j: (j, 0))]
        pairs.append((2 * t, 2 * t + 1, 'nt', 0))
    res = _mm(name, (L // tm, N // tn), ("parallel", "parallel"), None, ops, specs, pairs, [(tm, tn)], [], [],
              lambda acc: (acc,), [jax.ShapeDtypeStruct((L, N), out_dtype)],
              [pl.BlockSpec((tm, tn), lambda i, j: (i, j))], comm=comm)
    return _with_comm(res, comm, lambda o: o[0])


def _mm_tn(name, a, bs, out_dtype=BF16, tk_target=704, tn_target=1664, tm_target=2048, comm=None):
    L, M = a.shape
    N = bs[0].shape[1]
    tk, tn, tm = _tile(L, tk_target), _tile(N, tn_target, 128), _tile(M, tm_target, 128)
    nb = len(bs)
    ops = [a] + list(bs)
    specs = [pl.BlockSpec((tk, tm), lambda i, j, k: (k, i))] + [pl.BlockSpec((tk, tn), lambda i, j, k: (k, j))] * nb
    res = _mm(name, (M // tm, N // tn, L // tk), ("parallel", "parallel", "arbitrary"), 2, ops, specs,
              [(0, 1 + t, 'tn', t) for t in range(nb)], [(tm, tn)] * nb, [], [], lambda *acc: acc,
              [jax.ShapeDtypeStruct((M, N), out_dtype)] * nb,
              [pl.BlockSpec((tm, tn), lambda i, j, k: (i, j))] * nb, comm=comm)
    return _with_comm(res, comm, lambda o: o)


def _norm_fwd(x, w):
    L, D = x.shape
    tr = _tile(L, 512)

    def body(x_ref, w_ref, y_ref):
        v = x_ref[...]
        r = lax.rsqrt(jnp.mean(v * v, axis=-1, keepdims=True) + EPS)
        y_ref[...] = (v * r * w_ref[...]).astype(y_ref.dtype)

    return pl.pallas_call(
        body, name="norm_fwd", grid=(L // tr,), out_shape=jax.ShapeDtypeStruct((L, D), BF16),
        in_specs=[pl.BlockSpec((tr, D), lambda i: (i, 0)), pl.BlockSpec((1, D), lambda i: (0, 0))],
        out_specs=pl.BlockSpec((tr, D), lambda i: (i, 0)), compiler_params=_params(("parallel",)),
    )(x, w)


def _norm_bwd_math(x, w, dy):
    r = lax.rsqrt(jnp.mean(x * x, axis=-1, keepdims=True) + EPS)
    gy = dy * w
    dx = r * (gy - x * (r * r) * jnp.mean(gy * x, axis=-1, keepdims=True))
    dw = jnp.sum(dy * x * r, axis=0, keepdims=True)
    return dx, dw


def _norm_bwd(x, w, dy, res, scale, out_dtype):
    L, D = x.shape
    tr = _tile(L, 384)
    has_res = res is not None

    def body(*refs):
        x_ref, w_ref, dy_ref = refs[:3]
        res_ref = refs[3] if has_res else None
        dx_ref, dw_ref = refs[-2:]
        dx, dw = _norm_bwd_math(x_ref[...], w_ref[...], dy_ref[...].astype(F32))
        dx = scale * dx
        if has_res:
            dx = dx + res_ref[...]
        dx_ref[...] = dx.astype(dx_ref.dtype)

        @pl.when(pl.program_id(0) == 0)
        def _():
            dw_ref[...] = jnp.zeros_like(dw_ref)

        dw_ref[...] += scale * dw

    row = pl.BlockSpec((tr, D), lambda i: (i, 0))
    vec = pl.BlockSpec((1, D), lambda i: (0, 0))
    return pl.pallas_call(
        body, name="norm_bwd", grid=(L // tr,),
        out_shape=[jax.ShapeDtypeStruct((L, D), out_dtype), jax.ShapeDtypeStruct((1, D), F32)],
        in_specs=[row, vec, row] + ([row] if has_res else []), out_specs=[row, vec],
        compiler_params=_params(("arbitrary",)),
    )(*([x, w, dy] + ([res] if has_res else [])))


def _loss(h, target):
    L, D = h.shape

    def body(h_ref, t_ref, dh_ref, loss_ref):
        i = pl.program_id(0)

        @pl.when(i == 0)
        def _():
            dh_ref[...] = jnp.zeros_like(dh_ref)
            loss_ref[...] = jnp.zeros_like(loss_ref)

        @pl.when(i > 0)
        def _():
            diff = h_ref[...] - t_ref[...]
            dh_ref[...] = diff * (1.0 / D)
            loss_ref[...] += 0.5 * jnp.sum(diff * diff) * (1.0 / D)

    return pl.pallas_call(
        body, name="loss", grid=(L // BLK,),
        out_shape=[jax.ShapeDtypeStruct((L, D), F32), jax.ShapeDtypeStruct((8, 128), F32)],
        in_specs=[pl.BlockSpec((BLK, D), lambda i: (i, 0)),
                  pl.BlockSpec((BLK, D), lambda i: (jnp.maximum(i - 1, 0), 0))],
        out_specs=[pl.BlockSpec((BLK, D), lambda i: (i, 0)), pl.BlockSpec((8, 128), lambda i: (0, 0))],
        compiler_params=_params(("arbitrary",)),
    )(h, target)


def _ffn_up(a, wg, wu, comm=None):
    L, D = a.shape
    F = wg.shape[1]
    tm = _tile(L, 704)

    def ep(g, u):
        return g, u, g * _sigmoid(g) * u

    hspec = pl.BlockSpec((None, tm, F), lambda i, j: (j, i, 0))
    wspec = pl.BlockSpec((None, F, D), lambda i, j: (j, 0, 0))
    res = _mm("ffn_up", (L // tm, N_DEV), ("parallel", "parallel"), None,
              [a, wg, wu], [pl.BlockSpec((tm, D), lambda i, j: (i, 0)), wspec, wspec],
              [(0, 1, 'nt', 0), (0, 2, 'nt', 1)], [(tm, F)] * 2, [], [], ep,
              [jax.ShapeDtypeStruct((N_DEV, L, F), BF16)] * 3, [hspec] * 3, comm=comm)
    return _with_comm(res, comm, lambda o: o)


def _ffn_gate(a, wg, comm=None):
    L, D = a.shape
    F = wg.shape[1]
    tm = _tile(L, 704)
    res = _mm("ffn_gate", (L // tm, N_DEV), ("parallel", "parallel"), None,
              [a, wg], [pl.BlockSpec((tm, D), lambda i, j: (i, 0)), pl.BlockSpec((None, F, D), lambda i, j: (j, 0, 0))],
              [(0, 1, 'nt', 0)], [(tm, F)], [], [], lambda g: (g,),
              [jax.ShapeDtypeStruct((N_DEV, L, F), BF16)], [pl.BlockSpec((None, tm, F), lambda i, j: (j, i, 0))],
              comm=comm)
    return _with_comm(res, comm, lambda o: o[0])


def _ffn_up_gated(a, wu, g, comm=None):
    L, D = a.shape
    F = wu.shape[1]
    tm = _tile(L, 704)

    def ep(u, g_):
        g32 = g_.astype(F32)
        return u, g32 * _sigmoid(g32) * u

    hspec = pl.BlockSpec((None, tm, F), lambda i, j: (j, i, 0))
    res = _mm("ffn_up_gated", (L // tm, N_DEV), ("parallel", "parallel"), None,
              [a, wu], [pl.BlockSpec((tm, D), lambda i, j: (i, 0)), pl.BlockSpec((None, F, D), lambda i, j: (j, 0, 0))],
              [(0, 1, 'nt', 0)], [(tm, F)], [g], [hspec], ep,
              [jax.ShapeDtypeStruct((N_DEV, L, F), BF16)] * 2, [hspec, hspec], comm=comm)
    return _with_comm(res, comm, lambda o: o)


def _resnorm_epilogue(scale):
    def ep(acc, h, w):
        r = lax.rsqrt(jnp.mean(acc * acc, axis=-1, keepdims=True) + EPS)
        return acc, h + scale * (acc * r * w)
    return ep


def _ffn_down(hid, wd, h_in, post, comm=None):
    _, L, F = hid.shape
    D = wd.shape[2]
    tm = _tile(L, 528)
    row = pl.BlockSpec((tm, D), lambda i, j: (i, 0))
    res = _mm("ffn_down", (L // tm, N_DEV), ("parallel", "arbitrary"), 1,
              [hid, wd], [pl.BlockSpec((None, tm, F), lambda i, j: (j, i, 0)),
                          pl.BlockSpec((None, F, D), lambda i, j: (j, 0, 0))],
              [(0, 1, 'nn', 0)], [(tm, D)], [h_in, post], [row, pl.BlockSpec((1, D), lambda i, j: (0, 0))],
              _resnorm_epilogue(0.5), [jax.ShapeDtypeStruct((L, D), F32)] * 2, [row, row], comm=comm)
    return _with_comm(res, comm, lambda o: o)


def _ffn_dhid(df, wd, g, u, comm=None):
    L, D = df.shape
    F = wd.shape[1]
    tm = _tile(L, 704)

    def ep(dhid, g_, u_):
        g32, u32 = g_.astype(F32), u_.astype(F32)
        sg = _sigmoid(g32)
        return dhid * u32 * sg * (1.0 + g32 * (1.0 - sg)), dhid * g32 * sg

    hspec = pl.BlockSpec((None, tm, F), lambda i, j: (j, i, 0))
    res = _mm("ffn_dhid", (L // tm, N_DEV), ("parallel", "parallel"), None,
              [df, wd], [pl.BlockSpec((tm, D), lambda i, j: (i, 0)),
                         pl.BlockSpec((None, F, D), lambda i, j: (j, 0, 0))],
              [(0, 1, 'nt', 0)], [(tm, F)], [g, u], [hspec, hspec], ep,
              [jax.ShapeDtypeStruct((N_DEV, L, F), BF16)] * 2, [hspec, hspec], comm=comm)
    return _with_comm(res, comm, lambda o: o)


def _ffn_dwd(hid, df, comm=None):
    _, L, F = hid.shape
    D = df.shape[1]
    tk = _tile(L, 1408)
    res = _mm("ffn_dwd", (N_DEV, L // tk), ("parallel", "arbitrary"), 1,
              [hid, df], [pl.BlockSpec((None, tk, F), lambda j, k: (j, k, 0)),
                          pl.BlockSpec((tk, D), lambda j, k: (k, 0))],
              [(0, 1, 'tn', 0)], [(F, D)], [], [], lambda acc: (acc,),
              [jax.ShapeDtypeStruct((N_DEV, F, D), BF16)], [pl.BlockSpec((None, F, D), lambda j, k: (j, 0, 0))],
              comm=comm)
    return _with_comm(res, comm, lambda o: o[0])


def _ffn_dwgu(a, dg, du, comm=None):
    L, D = a.shape
    F = dg.shape[2]
    tk = _tile(L, 704)
    hspec = pl.BlockSpec((None, tk, F), lambda j, k: (j, k, 0))
    wspec = pl.BlockSpec((None, F, D), lambda j, k: (j, 0, 0))
    res = _mm("ffn_dwgu", (N_DEV, L // tk), ("parallel", "arbitrary"), 1,
              [a, dg, du], [pl.BlockSpec((tk, D), lambda j, k: (k, 0)), hspec, hspec],
              [(1, 0, 'tn', 0), (2, 0, 'tn', 1)], [(F, D)] * 2, [], [], lambda *acc: acc,
              [jax.ShapeDtypeStruct((N_DEV, F, D), BF16)] * 2, [wspec, wspec], comm=comm)
    return _with_comm(res, comm, lambda o: o)


def _ffn_da(dg, du, wg, wu, comm=None):
    _, L, F = dg.shape
    D = wg.shape[2]
    tm = _tile(L, 704)
    hspec = pl.BlockSpec((None, tm, F), lambda i, j: (j, i, 0))
    wspec = pl.BlockSpec((None, F, D), lambda i, j: (j, 0, 0))
    row = pl.BlockSpec((tm, D), lambda i, j: (i, 0))
    res = _mm("ffn_da", (L // tm, N_DEV), ("parallel", "arbitrary"), 1,
              [dg, du, wg, wu], [hspec, hspec, wspec, wspec],
              [(0, 2, 'nn', 0), (1, 3, 'nn', 0)], [(tm, D)], [], [], lambda acc: (acc,),
              [jax.ShapeDtypeStruct((L, D), F32)], [row], comm=comm)
    return _with_comm(res, comm, lambda o: o[0])


def _rope_tables(L):
    rows = jnp.arange(L, dtype=F32)
    pos = jnp.where(rows < BLK, rows, rows - (BLK - N_META))
    inv_r = ROPE_THETA ** (-jnp.arange(0, HD, 2, dtype=F32) / HD)
    ang_r = pos[:, None] * inv_r[None, :]
    cr = jnp.concatenate([jnp.cos(ang_r), jnp.cos(ang_r)], axis=1)
    sr = jnp.concatenate([-jnp.sin(ang_r), jnp.sin(ang_r)], axis=1)
    inv_m = ROPE_THETA ** (-jnp.arange(0, ROPE, 2, dtype=F32) / ROPE)
    ang_m = pos[:, None] * inv_m[None, :]
    z32 = jnp.zeros((L, ROPE // 2), F32)
    z64 = jnp.zeros((L, HD - ROPE), F32)
    cm = jnp.concatenate([jnp.cos(ang_m), jnp.cos(ang_m), z64], axis=1)
    sa = jnp.concatenate([-jnp.sin(ang_m), z32, z64], axis=1)
    sb = jnp.concatenate([z32, jnp.sin(ang_m), z64], axis=1)
    return cr, sr, cm, sa, sb


def _rope_ret(x, cr, sr):
    return x * cr + pltpu.roll(x, HD // 2, 1) * sr


def _rope_ret_t(d, cr, sr):
    return d * cr + pltpu.roll(d * sr, HD // 2, 1)


def _rope_mla(x, cm, sa, sb):
    return x * cm + pltpu.roll(x, HD - ROPE // 2, 1) * sa + pltpu.roll(x, ROPE // 2, 1) * sb


def _rope_mla_t(d, cm, sa, sb):
    return d * cm + pltpu.roll(d * sa, ROPE // 2, 1) + pltpu.roll(d * sb, HD - ROPE // 2, 1)


C_RQ, C_RK, C_RV, C_RG = 0, HEADS * HD, 2 * HEADS * HD, 3 * HEADS * HD
C_CQ = 4 * HEADS * HD
C_CKV = C_CQ + Q_RANK
C_KR = C_CKV + KV_RANK
RET_K_SCALE = HD ** -0.5


def _prep(proj, tabs, qn, kvn):
    L = proj.shape[0]
    tr = _tile(L, 256)
    W = HEADS * HD

    def body(p_ref, cr_ref, sr_ref, cm_ref, sa_ref, sb_ref, qn_ref, kvn_ref, q_ref, k_ref, v_ref, cq_ref, ckv_ref,
             kr_ref):
        cr, sr = cr_ref[...], sr_ref[...]
        for h in range(HEADS):
            sl = slice(h * HD, (h + 1) * HD)
            q_ref[:, sl] = _rope_ret(p_ref[:, C_RQ + h * HD:C_RQ + (h + 1) * HD], cr, sr).astype(BF16)
            k_ref[:, sl] = (_rope_ret(p_ref[:, C_RK + h * HD:C_RK + (h + 1) * HD], cr, sr)
                            * RET_K_SCALE).astype(BF16)
        v_ref[...] = p_ref[:, C_RV:C_RV + W].astype(BF16)
        cq = p_ref[:, C_CQ:C_CQ + Q_RANK]
        cq_ref[...] = (cq * lax.rsqrt(jnp.mean(cq * cq, axis=-1, keepdims=True) + EPS) * qn_ref[...]).astype(BF16)
        ckv = p_ref[:, C_CKV:C_CKV + KV_RANK]
        ckv_ref[...] = (ckv * lax.rsqrt(jnp.mean(ckv * ckv, axis=-1, keepdims=True) + EPS)
                        * kvn_ref[...]).astype(BF16)
        kr_ref[...] = _rope_mla(p_ref[:, C_KR:C_KR + HD], cm_ref[...], sa_ref[...], sb_ref[...]).astype(BF16)

    row = lambda w: pl.BlockSpec((tr, w), lambda i: (i, 0))
    vec = lambda w: pl.BlockSpec((1, w), lambda i: (0, 0))
    return pl.pallas_call(
        body, name="mix_prep", grid=(L // tr,),
        out_shape=[jax.ShapeDtypeStruct((L, W), BF16)] * 3 + [jax.ShapeDtypeStruct((L, Q_RANK), BF16),
                                                              jax.ShapeDtypeStruct((L, KV_RANK), BF16),
                                                              jax.ShapeDtypeStruct((L, HD), BF16)],
        in_specs=[row(D_INP)] + [row(HD)] * 5 + [vec(Q_RANK), vec(KV_RANK)],
        out_specs=[row(W)] * 3 + [row(Q_RANK), row(KV_RANK), row(HD)],
        compiler_params=_params(("parallel",)),
    )(proj, *tabs, qn, kvn)


def _prep_bwd(proj, dq, dk, dv, drg, dcqn, dckvn, dkr8, tabs, qn, kvn):
    L = proj.shape[0]
    tr = _tile(L, 192)
    W = HEADS * HD

    def body(p_ref, dq_ref, dk_ref, dv_ref, drg_ref, dcq_ref, dckv_ref, dkr_ref, cr_ref, sr_ref, cm_ref, sa_ref,
             sb_ref, qn_ref, kvn_ref, dp_ref, dqn_ref, dkvn_ref):
        cr, sr = cr_ref[...], sr_ref[...]
        dkr = None
        for h in range(HEADS):
            sl = slice(h * HD, (h + 1) * HD)
            dp_ref[:, C_RQ + h * HD:C_RQ + (h + 1) * HD] = _rope_ret_t(dq_ref[:, sl], cr, sr).astype(BF16)
            dp_ref[:, C_RK + h * HD:C_RK + (h + 1) * HD] = (_rope_ret_t(dk_ref[:, sl], cr, sr)
                                                            * RET_K_SCALE).astype(BF16)
            part = dkr_ref[:, sl]
            dkr = part if dkr is None else dkr + part
        dp_ref[:, C_RV:C_RV + W] = dv_ref[...].astype(BF16)
        dp_ref[:, C_RG:C_RG + W] = drg_ref[...].astype(BF16)
        dcq, dqn = _norm_bwd_math(p_ref[:, C_CQ:C_CQ + Q_RANK], qn_ref[...], dcq_ref[...])
        dp_ref[:, C_CQ:C_CQ + Q_RANK] = dcq.astype(BF16)
        dckv, dkvn = _norm_bwd_math(p_ref[:, C_CKV:C_CKV + KV_RANK], kvn_ref[...], dckv_ref[...])
        dp_ref[:, C_CKV:C_CKV + KV_RANK] = dckv.astype(BF16)
        dp_ref[:, C_KR:C_KR + HD] = _rope_mla_t(dkr, cm_ref[...], sa_ref[...], sb_ref[...]).astype(BF16)

        @pl.when(pl.program_id(0) == 0)
        def _():
            dqn_ref[...] = jnp.zeros_like(dqn_ref)
            dkvn_ref[...] = jnp.zeros_like(dkvn_ref)

        dqn_ref[...] += dqn
        dkvn_ref[...] += dkvn

    row = lambda w: pl.BlockSpec((tr, w), lambda i: (i, 0))
    vec = lambda w: pl.BlockSpec((1, w), lambda i: (0, 0))
    return pl.pallas_call(
        body, name="mix_prep_bwd", grid=(L // tr,),
        out_shape=[jax.ShapeDtypeStruct((L, D_INP), BF16), jax.ShapeDtypeStruct((1, Q_RANK), F32),
                   jax.ShapeDtypeStruct((1, KV_RANK), F32)],
        in_specs=[row(D_INP)] + [row(W)] * 4 + [row(Q_RANK), row(KV_RANK), row(W)] + [row(HD)] * 5
                 + [vec(Q_RANK), vec(KV_RANK)],
        out_specs=[row(D_INP), vec(Q_RANK), vec(KV_RANK)],
        compiler_params=_params(("arbitrary",)),
    )(proj, dq, dk, dv, drg, dcqn, dckvn, dkr8, *tabs, qn, kvn)


def _post(o_ret, proj, gn):
    L, W = o_ret.shape
    tr = _tile(L, 384)

    def body(o_ref, rg_ref, gn_ref, out_ref):
        for h in range(HEADS):
            sl = slice(h * HD, (h + 1) * HD)
            o = o_ref[:, sl]
            rg = rg_ref[:, sl]
            n = o * lax.rsqrt(jnp.mean(o * o, axis=-1, keepdims=True) + EPS)
            out_ref[:, sl] = (n * gn_ref[:, sl] * (rg * _sigmoid(rg))).astype(BF16)

    row = pl.BlockSpec((tr, W), lambda i: (i, 0))
    return pl.pallas_call(
        body, name="ret_post", grid=(L // tr,), out_shape=jax.ShapeDtypeStruct((L, W), BF16),
        in_specs=[row, pl.BlockSpec((tr, W), lambda i: (i, C_RG // W)), pl.BlockSpec((1, W), lambda i: (0, 0))],
        out_specs=row, compiler_params=_params(("parallel",)),
    )(o_ret, proj, gn)


def _post_bwd(o_ret, proj, gn, dcat):
    L, W = o_ret.shape
    tr = _tile(L, 384)

    def body(o_ref, rg_ref, gn_ref, d_ref, do_ref, drg_ref, dgn_ref):
        @pl.when(pl.program_id(0) == 0)
        def _():
            dgn_ref[...] = jnp.zeros_like(dgn_ref)

        for h in range(HEADS):
            sl = slice(h * HD, (h + 1) * HD)
            o = o_ref[:, sl]
            rg = rg_ref[:, sl]
            d = d_ref[:, sl].astype(F32)
            gw = gn_ref[:, sl]
            r = lax.rsqrt(jnp.mean(o * o, axis=-1, keepdims=True) + EPS)
            n = o * r
            sg = _sigmoid(rg)
            si = rg * sg
            dn = d * gw * si
            dgn_ref[:, sl] += jnp.sum(d * n * si, axis=0, keepdims=True)
            drg_ref[:, sl] = d * n * gw * sg * (1.0 + rg * (1.0 - sg))
            do_ref[:, sl] = (r * (dn - o * (r * r) * jnp.mean(dn * o, axis=-1, keepdims=True))).astype(BF16)

    row = pl.BlockSpec((tr, W), lambda i: (i, 0))
    vec = pl.BlockSpec((1, W), lambda i: (0, 0))
    return pl.pallas_call(
        body, name="ret_post_bwd", grid=(L // tr,),
        out_shape=[jax.ShapeDtypeStruct((L, W), BF16), jax.ShapeDtypeStruct((L, W), F32),
                   jax.ShapeDtypeStruct((1, W), F32)],
        in_specs=[row, pl.BlockSpec((tr, W), lambda i: (i, C_RG // W)), vec, row],
        out_specs=[row, row, vec], compiler_params=_params(("arbitrary",)),
    )(o_ret, proj, gn, dcat)


RET_HEADS_PER_STEP = 4


def _lin_attn(name, q, k, v, lg, reverse):
    L, W = q.shape
    nc = L // BLK - 1
    G = RET_HEADS_PER_STEP

    def body(q_ref, k_ref, v_ref, lg_ref, o_ref, s_ref):
        n = lax.broadcasted_iota(jnp.int32, (BLK, BLK), 0).astype(F32)
        m = lax.broadcasted_iota(jnp.int32, (BLK, BLK), 1).astype(F32)
        dist = (m - n) if reverse else (n - m)
        consts = []
        for g in range(G):
            lgv = lg_ref[g, 0:1, :]
            dmask = jnp.where(dist >= 0, jnp.exp(lgv * jnp.maximum(dist, 0.0)), 0.0)
            c = dict(dmask=dmask, dmask0=jnp.where((n < N_META) & (m < N_META), dmask, 0.0),
                     gl=jnp.exp(lgv * float(BLK)))
            if reverse:
                c.update(inter=jnp.exp(lgv * (float(BLK) - n)), upd=jnp.exp(lgv * n),
                         inter0=jnp.where(n < N_META, jnp.exp(lgv * jnp.maximum(float(N_META) - n, 0.0)), 0.0))
            else:
                c.update(inter=jnp.exp(lgv * (n + 1.0)), upd=jnp.exp(lgv * (float(BLK) - 1.0 - n)),
                         upd0=jnp.where(n < N_META, jnp.exp(lgv * jnp.maximum(float(N_META) - 1.0 - n, 0.0)), 0.0))
            consts.append(c)

        def chunk(c):
            rows = pl.ds(pl.multiple_of(c * BLK, BLK), BLK)
            state = [s_ref[g] for g in range(G)]
            outs, new_state = [], []
            for g in range(G):
                cols = slice(g * HD, (g + 1) * HD)
                cg = consts[g]
                qc, kc, vc = q_ref[rows, cols], k_ref[rows, cols], v_ref[rows, cols]
                a = _dot(qc, kc, 'nt') * cg['dmask']
                outs.append(_dot(a.astype(BF16), vc, 'nn') + _dot(qc, state[g].astype(BF16), 'nn') * cg['inter'])
                new_state.append(state[g] * cg['gl'] + _dot((kc.astype(F32) * cg['upd']).astype(BF16), vc, 'tn'))
            for g in range(G):
                o_ref[rows, g * HD:(g + 1) * HD] = outs[g]
                s_ref[g] = new_state[g]

        def first_chunk(with_state):
            for g in range(G):
                cols = slice(g * HD, (g + 1) * HD)
                cg = consts[g]
                q0, k0, v0 = q_ref[0:BLK, cols], k_ref[0:BLK, cols], v_ref[0:BLK, cols]
                o0 = _dot((_dot(q0, k0, 'nt') * cg['dmask0']).astype(BF16), v0, 'nn')
                if with_state:
                    o0 = o0 + _dot(q0, s_ref[g].astype(BF16), 'nn') * cg['inter0']
                else:
                    s_ref[g] = _dot((k0.astype(F32) * cg['upd0']).astype(BF16), v0, 'tn')
                o_ref[0:BLK, cols] = o0

        if reverse:
            s_ref[...] = jnp.zeros_like(s_ref)

            def step(t, carry):
                chunk(nc - t)
                return carry

            lax.fori_loop(0, nc, step, 0)
            first_chunk(True)
        else:
            first_chunk(False)

            def step(t, carry):
                chunk(t + 1)
                return carry

            lax.fori_loop(0, nc, step, 0)

    col = pl.BlockSpec((L, G * HD), lambda h: (0, h))
    return pl.pallas_call(
        body, name=name, grid=(HEADS // G,), out_shape=jax.ShapeDtypeStruct((L, W), F32),
        in_specs=[col, col, col, pl.BlockSpec((G, 8, HD), lambda h: (h, 0, 0))], out_specs=col,
        scratch_shapes=[pltpu.VMEM((G, HD, HD), F32)], compiler_params=_params(("parallel",)),
    )(q, k, v, lg)


ATT_SCALE = (HD + ROPE) ** -0.5
LOG2E = 1.4426950408889634
Q_PRESCALE = ATT_SCALE * LOG2E
NEG = -1e30


ATT_TILE = 384
ATT_HEADS_PER_STEP = 2


def _att_valid(nq, nk, row0, col0):
    r = lax.broadcasted_iota(jnp.int32, (nq, nk), 0) + row0
    c = lax.broadcasted_iota(jnp.int32, (nq, nk), 1) + col0
    return (c <= r) & ((c < N_META) | (c >= BLK))


def _attn_fwd(qm, kn, krr, vm, comm=None):
    L = qm.shape[0]
    W = HEADS * HD
    T = _tile(L, ATT_TILE, BLK)
    nb = L // T
    G = ATT_HEADS_PER_STEP
    n_cm = comm.n if comm is not None else 0

    def body(*refs):
        q_ref, kn_ref, kr_ref, v_ref = refs[:4]
        o_ref, lse_ref = refs[4 + n_cm:6 + n_cm]
        m_sc, l_sc, acc_sc = refs[6 + 2 * n_cm:9 + 2 * n_cm]
        if comm is not None:
            cm_refs = (refs[4:4 + n_cm], refs[6 + n_cm:6 + 2 * n_cm], refs[9 + 2 * n_cm:])
            first, last = _grid_edges((HEADS // G, nb))

            @pl.when(first)
            def _():
                comm.start(*cm_refs)

        i = pl.program_id(1)
        m_sc[...] = jnp.full_like(m_sc, NEG)
        l_sc[...] = jnp.zeros_like(l_sc)
        acc_sc[...] = jnp.zeros_like(acc_sc)

        def tile(j, masked):
            rows = pl.ds(pl.multiple_of(j * T, T), T)
            kr = kr_ref[rows, :]
            valid = _att_valid(T, T, i * T, j * T) if masked else None
            m_prev = [m_sc[g] for g in range(G)]
            l_prev = [l_sc[g] for g in range(G)]
            acc_prev = [acc_sc[g] for g in range(G)]
            m_new, l_new, acc_new = [], [], []
            for g in range(G):
                k = jnp.concatenate([kn_ref[rows, g * HD:(g + 1) * HD], kr], axis=1)
                s = _dot(q_ref[:, g * QH:(g + 1) * QH], k, 'nt')
                if masked:
                    s = jnp.where(valid, s, NEG)
                m_new.append(jnp.maximum(m_prev[g], jnp.max(s, axis=-1, keepdims=True)))
                p = jnp.exp2(s - m_new[g])
                alpha = jnp.exp2(m_prev[g] - m_new[g])
                l_new.append(alpha * l_prev[g] + jnp.sum(p, axis=-1, keepdims=True))
                acc_new.append(alpha * acc_prev[g] + _dot(p.astype(BF16), v_ref[rows, g * HD:(g + 1) * HD], 'nn'))
            for g in range(G):
                m_sc[g] = m_new[g]
                l_sc[g] = l_new[g]
                acc_sc[g] = acc_new[g]

        tile(0, True)

        def mid(j, carry):
            tile(j, False)
            return carry

        lax.fori_loop(1, i, mid, 0)

        @pl.when(i > 0)
        def _():
            tile(i, True)

        for g in range(G):
            l = l_sc[g]
            o_ref[:, g * HD:(g + 1) * HD] = (acc_sc[g] / l).astype(o_ref.dtype)
            lse_ref[g] = jnp.broadcast_to(m_sc[g] + jnp.log(l) * LOG2E, (T, HD))

        if comm is not None:
            @pl.when(last)
            def _():
                comm.finish(*cm_refs)

    cm_specs = comm.specs if comm is not None else []
    res = pl.pallas_call(
        body, name="attn_fwd", grid=(HEADS // G, nb),
        out_shape=[jax.ShapeDtypeStruct((L, W), BF16), jax.ShapeDtypeStruct((HEADS, L, HD), F32)]
        + (comm.out_shapes if comm is not None else []),
        in_specs=[pl.BlockSpec((T, G * QH), lambda h, i: (i, h)), pl.BlockSpec((L, G * HD), lambda h, i: (0, h)),
                  pl.BlockSpec((L, HD), lambda h, i: (0, 0)), pl.BlockSpec((L, G * HD), lambda h, i: (0, h))]
        + cm_specs,
        out_specs=[pl.BlockSpec((T, G * HD), lambda h, i: (i, h)),
                   pl.BlockSpec((G, T, HD), lambda h, i: (h, i, 0))] + cm_specs,
        scratch_shapes=[pltpu.VMEM((G, T, 1), F32), pltpu.VMEM((G, T, 1), F32), pltpu.VMEM((G, T, HD), F32)]
        + (comm.scratch if comm is not None else []),
        compiler_params=_params(("arbitrary", "arbitrary")),
    )(qm, kn, krr, vm, *(comm.arrays if comm is not None else []))
    return res[:2], res[2:]


def _attn_bwd(qm, kn, krr, vm, o, dcat, lse, comm=None):
    L = qm.shape[0]
    W = HEADS * HD
    T = _tile(L, ATT_TILE, BLK)
    nb = L // T
    n_cm = comm.n if comm is not None else 0

    def body(*refs):
        q_ref, kn_ref, kr_ref, v_ref, o_ref, do_ref, lse_ref = refs[:7]
        dq_ref, dkn_ref, dkr_ref, dv_ref = refs[7 + n_cm:11 + n_cm]
        dl_sc, dk_sc, dv_sc = refs[11 + 2 * n_cm:14 + 2 * n_cm]
        if comm is not None:
            cm_refs = (refs[7:7 + n_cm], refs[11 + n_cm:11 + 2 * n_cm], refs[14 + 2 * n_cm:])
            first, last = _grid_edges((HEADS, nb))

            @pl.when(first)
            def _():
                comm.start(*cm_refs)

        j = pl.program_id(1)

        @pl.when(j == 0)
        def _():
            dq_ref[...] = jnp.zeros_like(dq_ref)

            def rowsum(t, carry):
                rows = pl.ds(pl.multiple_of(t * T, T), T)
                dl_sc[rows, :] = jnp.sum(do_ref[rows, :].astype(F32) * o_ref[rows, :].astype(F32), axis=-1,
                                         keepdims=True)
                return carry

            lax.fori_loop(0, nb, rowsum, 0)

        k = jnp.concatenate([kn_ref[...], kr_ref[...]], axis=1)
        v = v_ref[...]
        dk_sc[...] = jnp.zeros_like(dk_sc)
        dv_sc[...] = jnp.zeros_like(dv_sc)

        def tile(i, masked):
            rows = pl.ds(pl.multiple_of(i * T, T), T)
            q = q_ref[rows, :]
            do = do_ref[rows, :]
            s = _dot(q, k, 'nt')
            if masked:
                s = jnp.where(_att_valid(T, T, i * T, j * T), s, NEG)
            p = jnp.exp2(s - lse_ref[rows, 0:1])
            dv_sc[...] += _dot(p.astype(BF16), do, 'tn')
            ds = (p * (_dot(do, v, 'nt') - dl_sc[rows, :])).astype(BF16)
            dk_sc[...] += _dot(ds, q, 'tn')
            dq_ref[rows, :] += _dot(ds, k, 'nn')

        tile(j, True)

        def rest(masked):
            def step(i, carry):
                tile(i, masked)
                return carry
            lax.fori_loop(j + 1, nb, step, 0)

        @pl.when(j == 0)
        def _():
            rest(True)

        @pl.when(j > 0)
        def _():
            rest(False)

        dk = dk_sc[...] * (1.0 / LOG2E)
        dkn_ref[...] = dk[:, 0:HD].astype(BF16)
        dkr_ref[...] = dk[:, HD:QH]
        dv_ref[...] = dv_sc[...].astype(BF16)

        if comm is not None:
            @pl.when(last)
            def _():
                comm.finish(*cm_refs)

    blk = pl.BlockSpec((T, HD), lambda h, j: (j, h))
    cm_specs = comm.specs if comm is not None else []
    res = pl.pallas_call(
        body, name="attn_bwd", grid=(HEADS, nb),
        out_shape=[jax.ShapeDtypeStruct((L, HEADS * QH), F32), jax.ShapeDtypeStruct((L, W), BF16),
                   jax.ShapeDtypeStruct((L, W), F32), jax.ShapeDtypeStruct((L, W), BF16)]
        + (comm.out_shapes if comm is not None else []),
        in_specs=[pl.BlockSpec((L, QH), lambda h, j: (0, h)), blk, pl.BlockSpec((T, HD), lambda h, j: (j, 0)), blk,
                  pl.BlockSpec((L, HD), lambda h, j: (0, h)), pl.BlockSpec((L, HD), lambda h, j: (0, HEADS + h)),
                  pl.BlockSpec((None, L, HD), lambda h, j: (h, 0, 0))] + cm_specs,
        out_specs=[pl.BlockSpec((L, QH), lambda h, j: (0, h)), blk, blk, blk] + cm_specs,
        scratch_shapes=[pltpu.VMEM((L, 1), F32), pltpu.VMEM((T, QH), F32), pltpu.VMEM((T, HD), F32)]
        + (comm.scratch if comm is not None else []),
        compiler_params=_params(("arbitrary", "arbitrary")),
    )(qm, kn, krr, vm, o, dcat, lse, *(comm.arrays if comm is not None else []))
    return res[:4], res[4:]


def _unrope_q(dqm, tabs_m):
    L, W = dqm.shape
    tr = _tile(L, 384)

    def body(d_ref, cm_ref, sa_ref, sb_ref, out_ref):
        cm, sa, sb = cm_ref[...], sa_ref[...], sb_ref[...]
        for h in range(HEADS):
            out_ref[:, h * QH:h * QH + HD] = (d_ref[:, h * QH:h * QH + HD] * ATT_SCALE).astype(BF16)
            out_ref[:, h * QH + HD:(h + 1) * QH] = _rope_mla_t(d_ref[:, h * QH + HD:(h + 1) * QH] * ATT_SCALE, cm, sa,
                                                               sb).astype(BF16)

    row = pl.BlockSpec((tr, W), lambda i: (i, 0))
    tab = pl.BlockSpec((tr, HD), lambda i: (i, 0))
    return pl.pallas_call(
        body, name="unrope_q", grid=(L // tr,), out_shape=jax.ShapeDtypeStruct((L, W), BF16),
        in_specs=[row, tab, tab, tab], out_specs=row, compiler_params=_params(("parallel",)),
    )(dqm, *tabs_m)


def _q_up(cqn, wuq_p, tabs_m):
    L = cqn.shape[0]
    tm = _tile(L, 704)

    def ep(acc, cm, sa, sb):
        acc = acc * Q_PRESCALE
        parts = []
        for h in range(HEADS):
            parts.append(acc[:, h * QH:h * QH + HD])
            parts.append(_rope_mla(acc[:, h * QH + HD:(h + 1) * QH], cm, sa, sb))
        return (jnp.concatenate(parts, axis=1),)

    tab = pl.BlockSpec((tm, HD), lambda i, j: (i, 0))
    return _mm("mla_q_up", (L // tm, 1), ("parallel", "parallel"), None,
               [cqn, wuq_p], [pl.BlockSpec((tm, Q_RANK), lambda i, j: (i, 0)),
                              pl.BlockSpec((HEADS * QH, Q_RANK), lambda i, j: (0, 0))],
               [(0, 1, 'nt', 0)], [(tm, HEADS * QH)], list(tabs_m), [tab] * 3, ep,
               [jax.ShapeDtypeStruct((L, HEADS * QH), BF16)], [pl.BlockSpec((tm, HEADS * QH), lambda i, j: (i, 0))])[0]


def _mix_out(cat, w_out, h_in, post):
    L, K = cat.shape
    D = w_out.shape[1]
    tm, tk = _tile(L, 384), _tile(K, 512, 128)
    row = pl.BlockSpec((tm, D), lambda i, k: (i, 0))
    return _mm("mix_out", (L // tm, K // tk), ("parallel", "arbitrary"), 1,
               [cat, w_out], [pl.BlockSpec((tm, tk), lambda i, k: (i, k)), pl.BlockSpec((tk, D), lambda i, k: (k, 0))],
               [(0, 1, 'nn', 0)], [(tm, D)], [h_in, post], [row, pl.BlockSpec((1, D), lambda i, k: (0, 0))],
               _resnorm_epilogue(1.0), [jax.ShapeDtypeStruct((L, D), F32)] * 2, [row, row])


ADAM_BLOCK_ELEMS = 512 * 704


def _adam_math(w, g, m, v):
    m = ADAM_B1 * m + (1.0 - ADAM_B1) * g
    v = ADAM_B2 * v + (1.0 - ADAM_B2) * (g * g)
    m_hat = m / (1.0 - ADAM_B1 ** ADAM_STEP)
    v_hat = v / (1.0 - ADAM_B2 ** ADAM_STEP)
    delta = -ADAM_LR * (m_hat / (jnp.sqrt(v_hat) + ADAM_EPS) + ADAM_WD * w)
    return delta, m, v


def _adam(name, w, m, v, g_slots=None, g=None):
    R, C = w.shape
    tr, tc = _tile(R, max(16, ADAM_BLOCK_ELEMS // C // 16 * 16), 16), C
    if tr * tc > ADAM_BLOCK_ELEMS:
        tr, tc = R, _tile(C, max(128, ADAM_BLOCK_ELEMS // R // 128 * 128), 128)
    from_slots = g_slots is not None

    def body(w_ref, m_ref, v_ref, g_ref, go_ref, d_ref, mo_ref, vo_ref):
        if from_slots:
            grad = g_ref[0].astype(F32)
            for s in range(1, N_DEV):
                grad = grad + g_ref[s].astype(F32)
        else:
            grad = g_ref[...]
        delta, mn, vn = _adam_math(w_ref[...], grad, m_ref[...], v_ref[...])
        go_ref[...] = grad
        d_ref[...] = delta
        mo_ref[...] = mn
        vo_ref[...] = vn

    row = pl.BlockSpec((tr, tc), lambda i, j: (i, j))
    gspec = pl.BlockSpec((N_DEV, tr, tc), lambda i, j: (0, i, j)) if from_slots else row
    return pl.pallas_call(
        body, name=name, grid=(R // tr, C // tc), out_shape=[jax.ShapeDtypeStruct((R, C), F32)] * 4,
        in_specs=[row, row, row, gspec], out_specs=[row] * 4, compiler_params=_params(("parallel", "parallel")),
    )(w, m, v, g_slots if from_slots else g)


def _unblock(gathered):
    n, r, c = gathered.shape
    return jnp.transpose(gathered, (1, 0, 2)).reshape(r, n * c)


def _reblock(full, c):
    r = full.shape[0]
    return jnp.transpose(full[:, :N_DEV * c].reshape(r, N_DEV, c), (1, 0, 2))


def _step(x, target, w, mom, vel):
    S, D = x.shape[1], x.shape[2]
    L = S + BLK
    def sq(a, n):
        if a.ndim == 2:
            return a
        if n in TRANSPOSED:
            a = jnp.swapaxes(a, 1, 2)
        return a.reshape(a.shape[1:])

    def unsq(o, n):
        o = o.reshape((1,) + o.shape)
        return jnp.swapaxes(o, 1, 2) if n in TRANSPOSED else o

    p = {n: sq(w[n], n) for n in WEIGHTS if n != 'meta_tokens'}
    gather = lambda names: _Exchange([p[n].astype(BF16) for n in names], False)
    scatter = lambda blocks: _Exchange(blocks, True)
    in_s, uq_s = p['w_in'].shape[0], p['mla_w_uq'].shape[0]
    assert uq_s == HD + ROPE and N_DEV == HEADS, "a w_uq shard is one head's columns"
    tabs = _rope_tables(L)
    tabs_m = tabs[2:]
    lg = jnp.broadcast_to(jnp.log(1.0 - 2.0 ** (-5.0 - jnp.arange(HEADS, dtype=F32)))[:, None, None], (HEADS, 8, HD))
    R = {}

    wg1, meta = _exchange("gather_first", [p['ffn1_w_gate'].astype(BF16), w['meta_tokens']], False)
    h0 = jnp.concatenate([_unblock(meta), jnp.zeros((BLK - N_META, D), F32), x[0]], axis=0)
    a1 = _norm_fwd(h0, p['ffn1_pre_norm'])
    g1, (wu1,) = _ffn_gate(a1, wg1, comm=gather(['ffn1_w_up']))
    (u1, hid1), (wd1,) = _ffn_up_gated(a1, wu1, g1, comm=gather(['ffn1_w_down']))
    (f1, h1), (w_in_g,) = _ffn_down(hid1, wd1, h0, p['ffn1_post_norm'], comm=gather(['w_in']))

    w_in = jnp.pad(w_in_g.reshape(N_DEV * in_s, D), ((0, D_INP - N_DEV * in_s), (0, 0)))
    um = _norm_fwd(h1, p['mix_pre_norm'])
    proj, (uq_g, uk_g, uv_g, wout_g) = _mm_nt("mix_in", [(um, w_in)], F32, tn_target=1664,
                                              comm=gather(['mla_w_uq', 'mla_w_uk', 'mla_w_uv', 'w_out']))
    wuq = jnp.pad(uq_g, ((0, 0), (0, QH - uq_s), (0, 0))).reshape(HEADS * QH, Q_RANK)
    wuk, wuv, w_out = _unblock(uk_g), _unblock(uv_g), wout_g.reshape(-1, D)
    qr, kr, vr, cqn, ckvn, krr = _prep(proj, tabs, p['mla_q_norm'], p['mla_kv_norm'])
    qm = _q_up(cqn, wuq, tabs_m)
    kn = _mm_nn("mla_k_up", ckvn, wuk, BF16)
    vm = _mm_nn("mla_v_up", ckvn, wuv, BF16)
    (o_mla, lse), (wg2, wu2) = _attn_fwd(qm, kn, krr, vm, comm=gather(['ffn2_w_gate', 'ffn2_w_up']))
    o_ret = _lin_attn("ret_fwd", qr, kr, vr, lg, False)
    ret = _post(o_ret, proj, p['ret_group_norm'])
    cat = jnp.concatenate([ret, o_mla], axis=1)
    m, h2 = _mix_out(cat, w_out, h1, p['mix_post_norm'])

    a2 = _norm_fwd(h2, p['ffn2_pre_norm'])
    (g2, u2, hid2), (wd2,) = _ffn_up(a2, wg2, wu2, comm=gather(['ffn2_w_down']))
    f2, h3 = _ffn_down(hid2, wd2, h2, p['ffn2_post_norm'])
    dh3, loss_blk = _loss(h3, target[0])

    dsmall = {}
    df2, dsmall['ffn2_post_norm'] = _norm_bwd(f2, p['ffn2_post_norm'], dh3, None, 0.5, BF16)
    dg2, du2 = _ffn_dhid(df2, wd2, g2, u2)
    dwd2 = _ffn_dwd(hid2, df2)
    (dwg2, dwu2), (R['ffn2_w_down'],) = _ffn_dwgu(a2, dg2, du2, comm=scatter([dwd2]))
    da2, (R['ffn2_w_gate'],) = _ffn_da(dg2, du2, wg2, wu2, comm=scatter([dwg2]))
    dh2, dsmall['ffn2_pre_norm'] = _norm_bwd(h2, p['ffn2_pre_norm'], da2, dh3, 1.0, F32)

    dm, dsmall['mix_post_norm'] = _norm_bwd(m, p['mix_post_norm'], dh2, None, 1.0, BF16)
    dcat = _mm_nt("mix_dcat", [(dm, w_out)], BF16)
    dwout = _mm_tn("mix_dwout", cat, [dm])[0]
    do_ret, drg, dsmall['ret_group_norm'] = _post_bwd(o_ret, proj, p['ret_group_norm'], dcat)
    dqr = _lin_attn("ret_dq", do_ret, vr, kr, lg, False)
    dkr = _lin_attn("ret_dk", vr, do_ret, qr, lg, True)
    dvr = _lin_attn("ret_dv", kr, qr, do_ret, lg, True)
    (dqm, dkn, dkr8, dvm), (R['ffn2_w_up'], R['w_out']) = _attn_bwd(
        qm, kn, krr, vm, o_mla, dcat, lse, comm=scatter([dwu2, dwout.reshape(N_DEV, -1, D)]))
    dqp = _unrope_q(dqm, tabs_m)
    dwuq = _mm_tn("mla_dwuq", dqp, [cqn])[0]
    dcqn = _mm_nn("mla_dcq", dqp, wuq, F32)
    dwuk, dwuv = _mm_tn("mla_dwukv", ckvn, [dkn, dvm])
    dckvn = _mm_nt("mla_dckv", [(dkn, wuk), (dvm, wuv)], F32)
    dproj, dsmall['mla_q_norm'], dsmall['mla_kv_norm'] = _prep_bwd(
        proj, dqr, dkr, dvr, drg, dcqn, dckvn, dkr8, tabs, p['mla_q_norm'], p['mla_kv_norm'])
    dwuq_b = dwuq.reshape(HEADS, QH, Q_RANK)[:, :uq_s]
    (dwin,), (R['mla_w_uq'], R['mla_w_uk'], R['mla_w_uv']) = _mm_tn(
        "mix_dwin", dproj, [um], comm=scatter([dwuq_b, _reblock(dwuk, p['mla_w_uk'].shape[1]),
                                               _reblock(dwuv, p['mla_w_uv'].shape[1])]))
    dwin_b = dwin[:N_DEV * in_s].reshape(N_DEV, in_s, D)
    half = D // 2
    dum, (r_win_a,) = _mm_nn("mix_du", dproj, w_in, F32, tn_target=512, comm=scatter([dwin_b[:, :, :half]]))
    dh1, dsmall['mix_pre_norm'] = _norm_bwd(h1, p['mix_pre_norm'], dum, dh2, 1.0, F32)

    df1, dsmall['ffn1_post_norm'] = _norm_bwd(f1, p['ffn1_post_norm'], dh1, None, 0.5, BF16)
    (dg1, du1), (r_win_b,) = _ffn_dhid(df1, wd1, g1, u1, comm=scatter([dwin_b[:, :, half:]]))
    R['w_in'] = jnp.concatenate([r_win_a, r_win_b], axis=2)
    dwd1 = _ffn_dwd(hid1, df1)
    (dwg1, dwu1), (R['ffn1_w_down'],) = _ffn_dwgu(a1, dg1, du1, comm=scatter([dwd1]))
    da1, (R['ffn1_w_gate'],) = _ffn_da(dg1, du1, wg1, wu1, comm=scatter([dwg1]))
    dh0, dsmall['ffn1_pre_norm'] = _norm_bwd(h0, p['ffn1_pre_norm'], da1, dh1, 1.0, F32)
    R['ffn1_w_up'], = _exchange("scatter_last", [dwu1], True)

    def slab(a):
        a = a.reshape(-1, 128)
        return jnp.pad(a, ((0, (-a.shape[0]) % 8), (0, 0)))

    slab_rows = lambda n: -(-(p[n].shape[-1] // 128) // 8) * 8
    packed = jnp.concatenate([slab(dsmall[n]) for n in SMALL] + [slab(dh0[:N_META]), loss_blk], axis=0)
    red = _allreduce_small(packed)
    offs = sum(slab_rows(n) for n in SMALL)
    n_small = offs
    gmeta_full = red[offs:offs + N_META * D // 128].reshape(N_META, D)
    offs += N_META * D // 128
    loss = red[offs, 0]

    grad, delta, new_m, new_v = {}, {}, {}, {}
    for n in BIG:
        outs = _adam("adam_" + n, p[n], sq(mom[n], n), sq(vel[n], n), g_slots=R[n])
        grad[n], delta[n], new_m[n], new_v[n] = [unsq(o, n) for o in outs]
    pack = lambda d: jnp.concatenate([slab(d[n]) for n in SMALL], axis=0)
    outs = _adam("adam_small", pack(w), pack(mom), pack(vel), g=red[:n_small])
    offs = 0
    for n in SMALL:
        r = p[n].shape[-1] // 128
        grad[n], delta[n], new_m[n], new_v[n] = [o[offs:offs + r].reshape(w[n].shape) for o in outs]
        offs += slab_rows(n)
    dev = 4 * lax.axis_index("x") + 2 * lax.axis_index("y") + lax.axis_index("c")
    mcols = w['meta_tokens'].shape[1]
    gmeta = lax.dynamic_slice(gmeta_full, (0, dev * mcols), (N_META, mcols))
    outs = _adam("adam_meta", w['meta_tokens'], mom['meta_tokens'], vel['meta_tokens'], g=gmeta)
    grad['meta_tokens'], delta['meta_tokens'], new_m['meta_tokens'], new_v['meta_tokens'] = outs

    return (loss, dh0[BLK:][None], *[grad[n] for n in WEIGHTS], *[delta[n] for n in WEIGHTS],
            *[new_m[n] for n in WEIGHTS], *[new_v[n] for n in WEIGHTS])


def kernel(x, meta_tokens, ffn1_pre_norm, ffn1_w_gate, ffn1_w_up, ffn1_w_down, ffn1_post_norm, mix_pre_norm, w_in, ret_group_norm, mla_q_norm, mla_w_uq, mla_kv_norm, mla_w_uk, mla_w_uv, w_out, mix_post_norm, ffn2_pre_norm, ffn2_w_gate, ffn2_w_up, ffn2_w_down, ffn2_post_norm, loss_target, m_meta_tokens, m_ffn1_pre_norm, m_ffn1_w_gate, m_ffn1_w_up, m_ffn1_w_down, m_ffn1_post_norm, m_mix_pre_norm, m_w_in, m_ret_group_norm, m_mla_q_norm, m_mla_w_uq, m_mla_kv_norm, m_mla_w_uk, m_mla_w_uv, m_w_out, m_mix_post_norm, m_ffn2_pre_norm, m_ffn2_w_gate, m_ffn2_w_up, m_ffn2_w_down, m_ffn2_post_norm, v_meta_tokens, v_ffn1_pre_norm, v_ffn1_w_gate, v_ffn1_w_up, v_ffn1_w_down, v_ffn1_post_norm, v_mix_pre_norm, v_w_in, v_ret_group_norm, v_mla_q_norm, v_mla_w_uq, v_mla_kv_norm, v_mla_w_uk, v_mla_w_uv, v_w_out, v_mix_post_norm, v_ffn2_pre_norm, v_ffn2_w_gate, v_ffn2_w_up, v_ffn2_w_down, v_ffn2_post_norm):
    w = dict(zip(WEIGHTS, (meta_tokens, ffn1_pre_norm, ffn1_w_gate, ffn1_w_up, ffn1_w_down, ffn1_post_norm,
                           mix_pre_norm, w_in, ret_group_norm, mla_q_norm, mla_w_uq, mla_kv_norm, mla_w_uk, mla_w_uv,
                           w_out, mix_post_norm, ffn2_pre_norm, ffn2_w_gate, ffn2_w_up, ffn2_w_down, ffn2_post_norm)))
    mom = dict(zip(WEIGHTS, (m_meta_tokens, m_ffn1_pre_norm, m_ffn1_w_gate, m_ffn1_w_up, m_ffn1_w_down,
                             m_ffn1_post_norm, m_mix_pre_norm, m_w_in, m_ret_group_norm, m_mla_q_norm, m_mla_w_uq,
                             m_mla_kv_norm, m_mla_w_uk, m_mla_w_uv, m_w_out, m_mix_post_norm, m_ffn2_pre_norm,
                             m_ffn2_w_gate, m_ffn2_w_up, m_ffn2_w_down, m_ffn2_post_norm)))
    vel = dict(zip(WEIGHTS, (v_meta_tokens, v_ffn1_pre_norm, v_ffn1_w_gate, v_ffn1_w_up, v_ffn1_w_down,
                             v_ffn1_post_norm, v_mix_pre_norm, v_w_in, v_ret_group_norm, v_mla_q_norm, v_mla_w_uq,
                             v_mla_kv_norm, v_mla_w_uk, v_mla_w_uv, v_w_out, v_mix_post_norm, v_ffn2_pre_norm,
                             v_ffn2_w_gate, v_ffn2_w_up, v_ffn2_w_down, v_ffn2_post_norm)))
    return _step(x, loss_target, w, mom, vel)
```

```python
import functools
import math

import jax
import jax.numpy as jnp
from jax import lax
from jax.experimental import pallas as pl
from jax.experimental.pallas import tpu as pltpu

N_DEV = 8
N_META = 16
BLK = 128
HEADS = 8
HD = 128
ROPE = 64
Q_RANK = 512
KV_RANK = 256
QH = 2 * HD
D_INP = 4 * HEADS * HD + Q_RANK + KV_RANK + BLK
ROPE_THETA = 10000.0
EPS = 1e-6
ADAM_LR = 0.001
ADAM_B1 = 0.9
ADAM_B2 = 0.999
ADAM_EPS = 1e-08
ADAM_WD = 0.01
ADAM_STEP = 10
V7X_VMEM_LIMIT = 48 * 1024 * 1024
MESH = pl.DeviceIdType.MESH
F32 = jnp.float32
BF16 = jnp.bfloat16

WEIGHTS = ['meta_tokens', 'ffn1_pre_norm', 'ffn1_w_gate', 'ffn1_w_up', 'ffn1_w_down', 'ffn1_post_norm',
           'mix_pre_norm', 'w_in', 'ret_group_norm', 'mla_q_norm', 'mla_w_uq', 'mla_kv_norm', 'mla_w_uk',
           'mla_w_uv', 'w_out', 'mix_post_norm', 'ffn2_pre_norm', 'ffn2_w_gate', 'ffn2_w_up', 'ffn2_w_down',
           'ffn2_post_norm']
SMALL = ['ffn1_pre_norm', 'ffn1_post_norm', 'mix_pre_norm', 'ret_group_norm', 'mla_q_norm', 'mla_kv_norm',
         'mix_post_norm', 'ffn2_pre_norm', 'ffn2_post_norm']
TRANSPOSED = ('ffn1_w_gate', 'ffn1_w_up', 'ffn2_w_gate', 'ffn2_w_up', 'w_in', 'mla_w_uq')
BIG = ['ffn1_w_gate', 'ffn1_w_up', 'ffn1_w_down', 'w_in', 'mla_w_uq', 'mla_w_uk', 'mla_w_uv', 'w_out',
       'ffn2_w_gate', 'ffn2_w_up', 'ffn2_w_down']

_DIMS = {'nn': (((1,), (0,)), ((), ())), 'nt': (((1,), (1,)), ((), ())), 'tn': (((0,), (0,)), ((), ()))}


def _tile(n, target, mult=16):
    best = None
    for t in range(mult, min(n, target) + 1, mult):
        if n % t == 0:
            best = t
    return best if best is not None else n


def _params(sem):
    return pltpu.CompilerParams(dimension_semantics=sem, vmem_limit_bytes=V7X_VMEM_LIMIT)


def _dot(a, b, dims):
    return lax.dot_general(a, b, _DIMS[dims], preferred_element_type=F32)


def _sigmoid(x):
    return 0.5 * jnp.tanh(0.5 * x) + 0.5


def _me_and_peers():
    x, y, c = lax.axis_index("x"), lax.axis_index("y"), lax.axis_index("c")

    def peer(j):
        px = 1 - x if (j >> 2) & 1 else x
        py = 1 - y if (j >> 1) & 1 else y
        pc = 1 - c if j & 1 else c
        return (px, py, pc), 4 * px + 2 * py + pc

    return 4 * x + 2 * y + c, peer


class _Exchange:
    def __init__(self, arrays, per_peer):
        self.arrays = list(arrays)
        self.per_peer = per_peer
        self.n = len(self.arrays)
        self.out_shapes = [jax.ShapeDtypeStruct((N_DEV,) + tuple(a.shape[1:] if per_peer else a.shape), a.dtype)
                           for a in self.arrays]
        self.specs = [pl.BlockSpec(memory_space=pl.ANY)] * self.n
        self.scratch = [pltpu.SemaphoreType.DMA((7 * self.n,)), pltpu.SemaphoreType.DMA((7 * self.n,)),
                        pltpu.SemaphoreType.DMA((self.n,))]

    def _copies(self, src, dst, sems):
        send_sems, recv_sems, local_sems = sems
        me, peer = _me_and_peers()
        sib, _ = peer(1)
        local, sends, recvs, passes = [], {}, {}, {}
        for k in range(self.n):
            own = src[k].at[me] if self.per_peer else src[k]
            local.append(pltpu.make_async_copy(own, dst[k].at[me], local_sems.at[k]))
            for j in range(1, N_DEV):
                pid, pidx = peer(j)
                out = src[k].at[pidx] if self.per_peer else src[k]
                sem = dict(send_sem=send_sems.at[k * 7 + j - 1], recv_sem=recv_sems.at[k * 7 + j - 1])
                recvs[k, j] = pltpu.make_async_remote_copy(src_ref=out, dst_ref=dst[k].at[pidx], device_id=pid,
                                                           device_id_type=MESH, **sem)
                if self.per_peer or j in (1, 2, 4, 6):
                    sends[k, j] = pltpu.make_async_remote_copy(src_ref=out, dst_ref=dst[k].at[me], device_id=pid,
                                                               device_id_type=MESH, **sem)
                else:
                    _, origin = peer(j ^ 1)
                    passes[k, j ^ 1] = pltpu.make_async_remote_copy(
                        src_ref=dst[k].at[origin], dst_ref=dst[k].at[origin], device_id=sib, device_id_type=MESH, **sem)
        return local, sends, recvs, passes

    def start(self, src, dst, sems):
        local, sends, _, _ = self._copies(src, dst, sems)
        for cp in local + list(sends.values()):
            cp.start()

    def finish(self, src, dst, sems):
        local, sends, recvs, passes = self._copies(src, dst, sems)
        for key, cp in passes.items():
            recvs[key].wait_recv()
            cp.start()
        for key, cp in recvs.items():
            if key not in passes:
                cp.wait_recv()
        for cp in list(sends.values()) + list(passes.values()):
            cp.wait_send()
        for cp in local:
            cp.wait()


def _grid_edges(grid):
    first, last = None, None
    for a, n in enumerate(grid):
        f, l = pl.program_id(a) == 0, pl.program_id(a) == n - 1
        first = f if first is None else first & f
        last = l if last is None else last & l
    return first, last


def _exchange(name, arrays, per_peer):
    ex = _Exchange(arrays, per_peer)
    n = ex.n

    def body(*refs):
        ex.start(refs[:n], refs[n:2 * n], refs[2 * n:])
        ex.finish(refs[:n], refs[n:2 * n], refs[2 * n:])

    return pl.pallas_call(body, name=name, out_shape=ex.out_shapes, in_specs=ex.specs, out_specs=ex.specs,
                          scratch_shapes=ex.scratch)(*arrays)


def _scatter_start(blocks):
    def body(src_ref, land_ref, send_sems, recv_sems, src_thru, land_thru, token, local_sem):
        me, peer = _me_and_peers()
        local = pltpu.make_async_copy(src_ref.at[me], land_ref.at[me], local_sem)
        local.start()
        for j in range(1, N_DEV):
            pid, pidx = peer(j)
            pltpu.make_async_remote_copy(src_ref=src_ref.at[pidx], dst_ref=land_ref.at[me],
                                         send_sem=send_sems.at[j - 1], recv_sem=recv_sems.at[j - 1],
                                         device_id=pid, device_id_type=MESH).start()
        local.wait()
        token[...] = jnp.zeros_like(token)

    hbm = pl.BlockSpec(memory_space=pltpu.HBM)
    sem = pl.BlockSpec(memory_space=pltpu.SEMAPHORE)
    return pl.pallas_call(
        body, name="scatter_tail_start",
        out_shape=(pltpu.SemaphoreType.DMA((7,)), pltpu.SemaphoreType.DMA((7,)), pltpu.HBM(blocks.shape, blocks.dtype),
                   pltpu.HBM(blocks.shape, blocks.dtype), jax.ShapeDtypeStruct((8, 128), F32)),
        in_specs=(hbm, hbm), out_specs=(sem, sem, hbm, hbm, pl.BlockSpec(memory_space=pltpu.VMEM)),
        input_output_aliases={0: 2, 1: 3}, scratch_shapes=[pltpu.SemaphoreType.DMA],
        compiler_params=pltpu.CompilerParams(has_side_effects=pltpu.SideEffectType.DATAFLOW_SIDE_EFFECTING),
    )(pltpu.with_memory_space_constraint(blocks, pltpu.HBM),
      pltpu.with_memory_space_constraint(lax.empty(blocks.shape, blocks.dtype), pltpu.HBM))


def _scatter_wait(send_sems, recv_sems, src_thru, land_thru, after):
    n_after = len(after)

    def body(src_ref, land_ref, send_sems, recv_sems, *rest):
        me, peer = _me_and_peers()
        for j in range(1, N_DEV):
            pid, pidx = peer(j)
            cp = pltpu.make_async_remote_copy(src_ref=src_ref.at[pidx], dst_ref=land_ref.at[pidx],
                                              send_sem=send_sems.at[j - 1], recv_sem=recv_sems.at[j - 1],
                                              device_id=pid, device_id_type=MESH)
            cp.wait_send()
            cp.wait_recv()

    hbm = pl.BlockSpec(memory_space=pltpu.HBM)
    sem = pl.BlockSpec(memory_space=pltpu.SEMAPHORE)
    return pl.pallas_call(
        body, name="scatter_tail_wait",
        out_shape=(pltpu.HBM(src_thru.shape, src_thru.dtype), pltpu.HBM(land_thru.shape, land_thru.dtype)),
        in_specs=(hbm, hbm, sem, sem) + (pl.BlockSpec(memory_space=pl.ANY),) * n_after, out_specs=(hbm, hbm),
        input_output_aliases={0: 0, 1: 1},
        compiler_params=pltpu.CompilerParams(has_side_effects=pltpu.SideEffectType.DATAFLOW_SIDE_EFFECTING),
    )(src_thru, land_thru, send_sems, recv_sems, *after)[1]


def _allreduce_small(v):
    rows = v.shape[0]

    def body(v_ref, out_ref, buf, send_sems, recv_sems):
        me, peer = _me_and_peers()
        buf[pl.ds(me, 1)] = v_ref[...][None]
        sends = []
        for j in range(1, N_DEV):
            pid, _ = peer(j)
            cp = pltpu.make_async_remote_copy(src_ref=v_ref, dst_ref=buf.at[me], send_sem=send_sems.at[j - 1],
                                              recv_sem=recv_sems.at[j - 1], device_id=pid, device_id_type=MESH)
            cp.start()
            sends.append(cp)
        for j in range(1, N_DEV):
            pid, pidx = peer(j)
            pltpu.make_async_remote_copy(src_ref=v_ref, dst_ref=buf.at[pidx], send_sem=send_sems.at[j - 1],
                                         recv_sem=recv_sems.at[j - 1], device_id=pid,
                                         device_id_type=MESH).wait_recv()
        for cp in sends:
            cp.wait_send()
        acc = buf[0]
        for s in range(1, N_DEV):
            acc = acc + buf[s]
        out_ref[...] = acc

    vm = pl.BlockSpec(memory_space=pltpu.VMEM)
    return pl.pallas_call(
        body, name="allreduce_small", out_shape=jax.ShapeDtypeStruct(v.shape, F32),
        in_specs=[vm], out_specs=vm,
        scratch_shapes=[pltpu.VMEM((N_DEV, rows, 128), F32), pltpu.SemaphoreType.DMA((7,)),
                        pltpu.SemaphoreType.DMA((7,))],
    )(v)


def _mm(name, grid, sem, k_axis, ops, op_specs, pairs, acc_shapes, extras, extra_specs, epilogue, outs, out_specs,
        comm=None):
    n_op, n_ex, n_out = len(ops), len(extras), len(outs)
    nk = grid[k_axis] if k_axis is not None else 1
    n_acc = len(acc_shapes) if nk > 1 else 0
    n_cm = comm.n if comm is not None else 0

    def body(*refs):
        op_refs = refs[:n_op]
        ex_refs = refs[n_op:n_op + n_ex]
        n_in = n_op + n_ex + n_cm
        out_refs = refs[n_in:n_in + n_out]
        acc_refs = refs[n_in + n_out + n_cm:n_in + n_out + n_cm + n_acc]
        if comm is not None:
            cm_refs = (refs[n_op + n_ex:n_in], refs[n_in + n_out:n_in + n_out + n_cm],
                       refs[n_in + n_out + n_cm + n_acc:])
            first, last = _grid_edges(grid)

            @pl.when(first)
            def _():
                comm.start(*cm_refs)

        def finish(vals):
            res = epilogue(*vals, *[e[...] for e in ex_refs])
            for o, r in zip(out_refs, res):
                o[...] = r.astype(o.dtype)

        if nk == 1:
            parts = [None] * len(acc_shapes)
            for li, ri, dims, ai in pairs:
                d = _dot(op_refs[li][...], op_refs[ri][...], dims)
                parts[ai] = d if parts[ai] is None else parts[ai] + d
            finish(parts)
        else:
            k = pl.program_id(k_axis)

            @pl.when(k == 0)
            def _():
                for a in acc_refs:
                    a[...] = jnp.zeros_like(a)

            for li, ri, dims, ai in pairs:
                acc_refs[ai][...] += _dot(op_refs[li][...], op_refs[ri][...], dims)

            @pl.when(k == nk - 1)
            def _():
                finish([a[...] for a in acc_refs])

        if comm is not None:
            @pl.when(last)
            def _():
                comm.finish(*cm_refs)

    scratch = [pltpu.VMEM(s, F32) for s in acc_shapes] if nk > 1 else []
    if comm is None:
        return pl.pallas_call(
            body, name=name, grid=grid, out_shape=outs,
            in_specs=list(op_specs) + list(extra_specs), out_specs=list(out_specs),
            scratch_shapes=scratch, compiler_params=_params(sem),
        )(*ops, *extras)
    res = pl.pallas_call(
        body, name=name, grid=grid, out_shape=list(outs) + comm.out_shapes,
        in_specs=list(op_specs) + list(extra_specs) + comm.specs, out_specs=list(out_specs) + comm.specs,
        scratch_shapes=scratch + comm.scratch, compiler_params=_params(("arbitrary",) * len(grid)),
    )(*ops, *extras, *comm.arrays)
    return res[:n_out], res[n_out:]


def _with_comm(res, comm, pick):
    if comm is None:
        return pick(res)
    return pick(res[0]), res[1]


def _mm_nn(name, a, w, out_dtype, tm_target=704, tn_target=1664, epilogue=None, extras=(), extra_specs=(), comm=None):
    L, K = a.shape
    N = w.shape[1]
    tm, tn = _tile(L, tm_target), _tile(N, tn_target, 128)
    ep = epilogue if epilogue is not None else (lambda acc: (acc,))
    res = _mm(name, (L // tm, N // tn), ("parallel", "parallel"), None,
              [a, w], [pl.BlockSpec((tm, K), lambda i, j: (i, 0)), pl.BlockSpec((K, tn), lambda i, j: (0, j))],
              [(0, 1, 'nn', 0)], [(tm, tn)], list(extras), list(extra_specs), ep,
              [jax.ShapeDtypeStruct((L, N), out_dtype)], [pl.BlockSpec((tm, tn), lambda i, j: (i, j))], comm=comm)
    return _with_comm(res, comm, lambda o: o[0])


def _mm_nt(name, pairs_aw, out_dtype, tm_target=704, tn_target=512, comm=None):
    L = pairs_aw[0][0].shape[0]
    N = pairs_aw[0][1].shape[0]
    tm, tn = _tile(L, tm_target), _tile(N, tn_target, 128)
    ops, specs, pairs = [], [], []
    for t, (a, w) in enumerate(pairs_aw):
        K = a.shape[1]
        ops += [a, w]
        specs += [pl.BlockSpec((tm, K), lambda i, j: (i, 0)), pl.BlockSpec((tn, K), lambda i, j: (j, 0))]
        pairs.append((2 * t, 2 * t + 1, 'nt', 0))
    res = _mm(name, (L // tm, N // tn), ("parallel", "parallel"), None, ops, specs, pairs, [(tm, tn)], [], [],
              lambda acc: (acc,), [jax.ShapeDtypeStruct((L, N), out_dtype)],
              [pl.BlockSpec((tm, tn), lambda i, j: (i, j))], comm=comm)
    return _with_comm(res, comm, lambda o: o[0])


def _mm_tn(name, a, bs, out_dtype=BF16, tk_target=704, tn_target=1664, tm_target=2048, comm=None):
    L, M = a.shape
    N = bs[0].shape[1]
    tk, tn, tm = _tile(L, tk_target), _tile(N, tn_target, 128), _tile(M, tm_target, 128)
    nb = len(bs)
    ops = [a] + list(bs)
    specs = [pl.BlockSpec((tk, tm), lambda i, j, k: (k, i))] + [pl.BlockSpec((tk, tn), lambda i, j, k: (k, j))] * nb
    res = _mm(name, (M // tm, N // tn, L // tk), ("parallel", "parallel", "arbitrary"), 2, ops, specs,
              [(0, 1 + t, 'tn', t) for t in range(nb)], [(tm, tn)] * nb, [], [], lambda *acc: acc,
              [jax.ShapeDtypeStruct((M, N), out_dtype)] * nb,
              [pl.BlockSpec((tm, tn), lambda i, j, k: (i, j))] * nb, comm=comm)
    return _with_comm(res, comm, lambda o: o)


def _norm_fwd(x, w):
    L, D = x.shape
    tr = _tile(L, 512)

    def body(x_ref, w_ref, y_ref):
        v = x_ref[...]
        r = lax.rsqrt(jnp.mean(v * v, axis=-1, keepdims=True) + EPS)
        y_ref[...] = (v * r * w_ref[...]).astype(y_ref.dtype)

    return pl.pallas_call(
        body, name="norm_fwd", grid=(L // tr,), out_shape=jax.ShapeDtypeStruct((L, D), BF16),
        in_specs=[pl.BlockSpec((tr, D), lambda i: (i, 0)), pl.BlockSpec((1, D), lambda i: (0, 0))],
        out_specs=pl.BlockSpec((tr, D), lambda i: (i, 0)), compiler_params=_params(("parallel",)),
    )(x, w)


def _norm_bwd_math(x, w, dy):
    r = lax.rsqrt(jnp.mean(x * x, axis=-1, keepdims=True) + EPS)
    gy = dy * w
    dx = r * (gy - x * (r * r) * jnp.mean(gy * x, axis=-1, keepdims=True))
    dw = jnp.sum(dy * x * r, axis=0, keepdims=True)
    return dx, dw


def _norm_bwd(x, w, dy, res, scale, out_dtype):
    L, D = x.shape
    tr = _tile(L, 384)
    has_res = res is not None

    def body(*refs):
        x_ref, w_ref, dy_ref = refs[:3]
        res_ref = refs[3] if has_res else None
        dx_ref, dw_ref = refs[-2:]
        dx, dw = _norm_bwd_math(x_ref[...], w_ref[...], dy_ref[...].astype(F32))
        dx = scale * dx
        if has_res:
            dx = dx + res_ref[...]
        dx_ref[...] = dx.astype(dx_ref.dtype)

        @pl.when(pl.program_id(0) == 0)
        def _():
            dw_ref[...] = jnp.zeros_like(dw_ref)

        dw_ref[...] += scale * dw

    row = pl.BlockSpec((tr, D), lambda i: (i, 0))
    vec = pl.BlockSpec((1, D), lambda i: (0, 0))
    return pl.pallas_call(
        body, name="norm_bwd", grid=(L // tr,),
        out_shape=[jax.ShapeDtypeStruct((L, D), out_dtype), jax.ShapeDtypeStruct((1, D), F32)],
        in_specs=[row, vec, row] + ([row] if has_res else []), out_specs=[row, vec],
        compiler_params=_params(("arbitrary",)),
    )(*([x, w, dy] + ([res] if has_res else [])))


def _loss(h, target):
    L, D = h.shape

    def body(h_ref, t_ref, dh_ref, loss_ref):
        i = pl.program_id(0)

        @pl.when(i == 0)
        def _():
            dh_ref[...] = jnp.zeros_like(dh_ref)
            loss_ref[...] = jnp.zeros_like(loss_ref)

        @pl.when(i > 0)
        def _():
            diff = h_ref[...] - t_ref[...]
            dh_ref[...] = diff * (1.0 / D)
            loss_ref[...] += 0.5 * jnp.sum(diff * diff) * (1.0 / D)

    return pl.pallas_call(
        body, name="loss", grid=(L // BLK,),
        out_shape=[jax.ShapeDtypeStruct((L, D), F32), jax.ShapeDtypeStruct((8, 128), F32)],
        in_specs=[pl.BlockSpec((BLK, D), lambda i: (i, 0)),
                  pl.BlockSpec((BLK, D), lambda i: (jnp.maximum(i - 1, 0), 0))],
        out_specs=[pl.BlockSpec((BLK, D), lambda i: (i, 0)), pl.BlockSpec((8, 128), lambda i: (0, 0))],
        compiler_params=_params(("arbitrary",)),
    )(h, target)


def _ffn_up(a, wg, wu, comm=None):
    L, D = a.shape
    F = wg.shape[1]
    tm = _tile(L, 704)

    def ep(g, u):
        return g, u, g * _sigmoid(g) * u

    hspec = pl.BlockSpec((None, tm, F), lambda i, j: (j, i, 0))
    wspec = pl.BlockSpec((None, F, D), lambda i, j: (j, 0, 0))
    res = _mm("ffn_up", (L // tm, N_DEV), ("parallel", "parallel"), None,
              [a, wg, wu], [pl.BlockSpec((tm, D), lambda i, j: (i, 0)), wspec, wspec],
              [(0, 1, 'nt', 0), (0, 2, 'nt', 1)], [(tm, F)] * 2, [], [], ep,
              [jax.ShapeDtypeStruct((N_DEV, L, F), BF16)] * 3, [hspec] * 3, comm=comm)
    return _with_comm(res, comm, lambda o: o)


def _ffn_gate(a, wg, comm=None):
    L, D = a.shape
    F = wg.shape[1]
    tm = _tile(L, 704)
    res = _mm("ffn_gate", (L // tm, N_DEV), ("parallel", "parallel"), None,
              [a, wg], [pl.BlockSpec((tm, D), lambda i, j: (i, 0)), pl.BlockSpec((None, F, D), lambda i, j: (j, 0, 0))],
              [(0, 1, 'nt', 0)], [(tm, F)], [], [], lambda g: (g,),
              [jax.ShapeDtypeStruct((N_DEV, L, F), BF16)], [pl.BlockSpec((None, tm, F), lambda i, j: (j, i, 0))],
              comm=comm)
    return _with_comm(res, comm, lambda o: o[0])


def _ffn_up_gated(a, wu, g, comm=None):
    L, D = a.shape
    F = wu.shape[1]
    tm = _tile(L, 704)

    def ep(u, g_):
        g32 = g_.astype(F32)
        return u, g32 * _sigmoid(g32) * u

    hspec = pl.BlockSpec((None, tm, F), lambda i, j: (j, i, 0))
    res = _mm("ffn_up_gated", (L // tm, N_DEV), ("parallel", "parallel"), None,
              [a, wu], [pl.BlockSpec((tm, D), lambda i, j: (i, 0)), pl.BlockSpec((None, F, D), lambda i, j: (j, 0, 0))],
              [(0, 1, 'nt', 0)], [(tm, F)], [g], [hspec], ep,
              [jax.ShapeDtypeStruct((N_DEV, L, F), BF16)] * 2, [hspec, hspec], comm=comm)
    return _with_comm(res, comm, lambda o: o)


def _resnorm_epilogue(scale):
    def ep(acc, h, w):
        r = lax.rsqrt(jnp.mean(acc * acc, axis=-1, keepdims=True) + EPS)
        return acc, h + scale * (acc * r * w)
    return ep


def _ffn_down(hid, wd, h_in, post, comm=None):
    _, L, F = hid.shape
    D = wd.shape[2]
    tm = _tile(L, 528)
    row = pl.BlockSpec((tm, D), lambda i, j: (i, 0))
    res = _mm("ffn_down", (L // tm, N_DEV), ("parallel", "arbitrary"), 1,
              [hid, wd], [pl.BlockSpec((None, tm, F), lambda i, j: (j, i, 0)),
                          pl.BlockSpec((None, F, D), lambda i, j: (j, 0, 0))],
              [(0, 1, 'nn', 0)], [(tm, D)], [h_in, post], [row, pl.BlockSpec((1, D), lambda i, j: (0, 0))],
              _resnorm_epilogue(0.5), [jax.ShapeDtypeStruct((L, D), F32)] * 2, [row, row], comm=comm)
    return _with_comm(res, comm, lambda o: o)


def _ffn_dhid(df, wd, g, u, comm=None):
    L, D = df.shape
    F = wd.shape[1]
    tm = _tile(L, 704)

    def ep(dhid, g_, u_):
        g32, u32 = g_.astype(F32), u_.astype(F32)
        sg = _sigmoid(g32)
        return dhid * u32 * sg * (1.0 + g32 * (1.0 - sg)), dhid * g32 * sg

    hspec = pl.BlockSpec((None, tm, F), lambda i, j: (j, i, 0))
    res = _mm("ffn_dhid", (L // tm, N_DEV), ("parallel", "parallel"), None,
              [df, wd], [pl.BlockSpec((tm, D), lambda i, j: (i, 0)),
                         pl.BlockSpec((None, F, D), lambda i, j: (j, 0, 0))],
              [(0, 1, 'nt', 0)], [(tm, F)], [g, u], [hspec, hspec], ep,
              [jax.ShapeDtypeStruct((N_DEV, L, F), BF16)] * 2, [hspec, hspec], comm=comm)
    return _with_comm(res, comm, lambda o: o)


def _ffn_dwd(hid, df, comm=None):
    _, L, F = hid.shape
    D = df.shape[1]
    tk = _tile(L, 1408)
    res = _mm("ffn_dwd", (N_DEV, L // tk), ("parallel", "arbitrary"), 1,
              [hid, df], [pl.BlockSpec((None, tk, F), lambda j, k: (j, k, 0)),
                          pl.BlockSpec((tk, D), lambda j, k: (k, 0))],
              [(0, 1, 'tn', 0)], [(F, D)], [], [], lambda acc: (acc,),
              [jax.ShapeDtypeStruct((N_DEV, F, D), BF16)], [pl.BlockSpec((None, F, D), lambda j, k: (j, 0, 0))],
              comm=comm)
    return _with_comm(res, comm, lambda o: o[0])


def _ffn_dwgu(a, dg, du, comm=None):
    L, D = a.shape
    F = dg.shape[2]
    tk = _tile(L, 704)
    hspec = pl.BlockSpec((None, tk, F), lambda j, k: (j, k, 0))
    wspec = pl.BlockSpec((None, F, D), lambda j, k: (j, 0, 0))
    res = _mm("ffn_dwgu", (N_DEV, L // tk), ("parallel", "arbitrary"), 1,
              [a, dg, du], [pl.BlockSpec((tk, D), lambda j, k: (k, 0)), hspec, hspec],
              [(1, 0, 'tn', 0), (2, 0, 'tn', 1)], [(F, D)] * 2, [], [], lambda *acc: acc,
              [jax.ShapeDtypeStruct((N_DEV, F, D), BF16)] * 2, [wspec, wspec], comm=comm)
    return _with_comm(res, comm, lambda o: o)


def _ffn_da(dg, du, wg, wu, comm=None):
    _, L, F = dg.shape
    D = wg.shape[2]
    tm = _tile(L, 704)
    hspec = pl.BlockSpec((None, tm, F), lambda i, j: (j, i, 0))
    wspec = pl.BlockSpec((None, F, D), lambda i, j: (j, 0, 0))
    row = pl.BlockSpec((tm, D), lambda i, j: (i, 0))
    res = _mm("ffn_da", (L // tm, N_DEV), ("parallel", "arbitrary"), 1,
              [dg, du, wg, wu], [hspec, hspec, wspec, wspec],
              [(0, 2, 'nn', 0), (1, 3, 'nn', 0)], [(tm, D)], [], [], lambda acc: (acc,),
              [jax.ShapeDtypeStruct((L, D), F32)], [row], comm=comm)
    return _with_comm(res, comm, lambda o: o[0])


def _rope_tables(L):
    rows = jnp.arange(L, dtype=F32)
    pos = jnp.where(rows < BLK, rows, rows - (BLK - N_META))
    inv_r = ROPE_THETA ** (-jnp.arange(0, HD, 2, dtype=F32) / HD)
    ang_r = pos[:, None] * inv_r[None, :]
    cr = jnp.concatenate([jnp.cos(ang_r), jnp.cos(ang_r)], axis=1)
    sr = jnp.concatenate([-jnp.sin(ang_r), jnp.sin(ang_r)], axis=1)
    inv_m = ROPE_THETA ** (-jnp.arange(0, ROPE, 2, dtype=F32) / ROPE)
    ang_m = pos[:, None] * inv_m[None, :]
    z32 = jnp.zeros((L, ROPE // 2), F32)
    z64 = jnp.zeros((L, HD - ROPE), F32)
    cm = jnp.concatenate([jnp.cos(ang_m), jnp.cos(ang_m), z64], axis=1)
    sa = jnp.concatenate([-jnp.sin(ang_m), z32, z64], axis=1)
    sb = jnp.concatenate([z32, jnp.sin(ang_m), z64], axis=1)
    return cr, sr, cm, sa, sb


def _rope_ret(x, cr, sr):
    return x * cr + pltpu.roll(x, HD // 2, 1) * sr


def _rope_ret_t(d, cr, sr):
    return d * cr + pltpu.roll(d * sr, HD // 2, 1)


def _rope_mla(x, cm, sa, sb):
    return x * cm + pltpu.roll(x, HD - ROPE // 2, 1) * sa + pltpu.roll(x, ROPE // 2, 1) * sb


def _rope_mla_t(d, cm, sa, sb):
    return d * cm + pltpu.roll(d * sa, ROPE // 2, 1) + pltpu.roll(d * sb, HD - ROPE // 2, 1)


C_RQ, C_RK, C_RV, C_RG = 0, HEADS * HD, 2 * HEADS * HD, 3 * HEADS * HD
C_CQ = 4 * HEADS * HD
C_CKV = C_CQ + Q_RANK
C_KR = C_CKV + KV_RANK
RET_K_SCALE = HD ** -0.5


def _prep(proj, tabs, qn, kvn):
    L = proj.shape[0]
    tr = _tile(L, 256)
    W = HEADS * HD

    def body(p_ref, cr_ref, sr_ref, cm_ref, sa_ref, sb_ref, qn_ref, kvn_ref, q_ref, k_ref, v_ref, cq_ref, ckv_ref,
             kr_ref):
        cr, sr = cr_ref[...], sr_ref[...]
        for h in range(HEADS):
            sl = slice(h * HD, (h + 1) * HD)
            q_ref[:, sl] = _rope_ret(p_ref[:, C_RQ + h * HD:C_RQ + (h + 1) * HD], cr, sr).astype(BF16)
            k_ref[:, sl] = (_rope_ret(p_ref[:, C_RK + h * HD:C_RK + (h + 1) * HD], cr, sr)
                            * RET_K_SCALE).astype(BF16)
        v_ref[...] = p_ref[:, C_RV:C_RV + W].astype(BF16)
        cq = p_ref[:, C_CQ:C_CQ + Q_RANK]
        cq_ref[...] = (cq * lax.rsqrt(jnp.mean(cq * cq, axis=-1, keepdims=True) + EPS) * qn_ref[...]).astype(BF16)
        ckv = p_ref[:, C_CKV:C_CKV + KV_RANK]
        ckv_ref[...] = (ckv * lax.rsqrt(jnp.mean(ckv * ckv, axis=-1, keepdims=True) + EPS)
                        * kvn_ref[...]).astype(BF16)
        kr_ref[...] = _rope_mla(p_ref[:, C_KR:C_KR + HD], cm_ref[...], sa_ref[...], sb_ref[...]).astype(BF16)

    row = lambda w: pl.BlockSpec((tr, w), lambda i: (i, 0))
    vec = lambda w: pl.BlockSpec((1, w), lambda i: (0, 0))
    return pl.pallas_call(
        body, name="mix_prep", grid=(L // tr,),
        out_shape=[jax.ShapeDtypeStruct((L, W), BF16)] * 3 + [jax.ShapeDtypeStruct((L, Q_RANK), BF16),
                                                              jax.ShapeDtypeStruct((L, KV_RANK), BF16),
                                                              jax.ShapeDtypeStruct((L, HD), BF16)],
        in_specs=[row(D_INP)] + [row(HD)] * 5 + [vec(Q_RANK), vec(KV_RANK)],
        out_specs=[row(W)] * 3 + [row(Q_RANK), row(KV_RANK), row(HD)],
        compiler_params=_params(("parallel",)),
    )(proj, *tabs, qn, kvn)


def _prep_bwd(proj, dq, dk, dv, drg, dcqn, dckvn, dkr8, tabs, qn, kvn):
    L = proj.shape[0]
    tr = _tile(L, 192)
    W = HEADS * HD

    def body(p_ref, dq_ref, dk_ref, dv_ref, drg_ref, dcq_ref, dckv_ref, dkr_ref, cr_ref, sr_ref, cm_ref, sa_ref,
             sb_ref, qn_ref, kvn_ref, dp_ref, dqn_ref, dkvn_ref):
        cr, sr = cr_ref[...], sr_ref[...]
        dkr = None
        for h in range(HEADS):
            sl = slice(h * HD, (h + 1) * HD)
            dp_ref[:, C_RQ + h * HD:C_RQ + (h + 1) * HD] = _rope_ret_t(dq_ref[:, sl], cr, sr).astype(BF16)
            dp_ref[:, C_RK + h * HD:C_RK + (h + 1) * HD] = (_rope_ret_t(dk_ref[:, sl], cr, sr)
                                                            * RET_K_SCALE).astype(BF16)
            part = dkr_ref[:, sl]
            dkr = part if dkr is None else dkr + part
        dp_ref[:, C_RV:C_RV + W] = dv_ref[...].astype(BF16)
        dp_ref[:, C_RG:C_RG + W] = drg_ref[...].astype(BF16)
        dcq, dqn = _norm_bwd_math(p_ref[:, C_CQ:C_CQ + Q_RANK], qn_ref[...], dcq_ref[...])
        dp_ref[:, C_CQ:C_CQ + Q_RANK] = dcq.astype(BF16)
        dckv, dkvn = _norm_bwd_math(p_ref[:, C_CKV:C_CKV + KV_RANK], kvn_ref[...], dckv_ref[...])
        dp_ref[:, C_CKV:C_CKV + KV_RANK] = dckv.astype(BF16)
        dp_ref[:, C_KR:C_KR + HD] = _rope_mla_t(dkr, cm_ref[...], sa_ref[...], sb_ref[...]).astype(BF16)

        @pl.when(pl.program_id(0) == 0)
        def _():
            dqn_ref[...] = jnp.zeros_like(dqn_ref)
            dkvn_ref[...] = jnp.zeros_like(dkvn_ref)

        dqn_ref[...] += dqn
        dkvn_ref[...] += dkvn

    row = lambda w: pl.BlockSpec((tr, w), lambda i: (i, 0))
    vec = lambda w: pl.BlockSpec((1, w), lambda i: (0, 0))
    return pl.pallas_call(
        body, name="mix_prep_bwd", grid=(L // tr,),
        out_shape=[jax.ShapeDtypeStruct((L, D_INP), BF16), jax.ShapeDtypeStruct((1, Q_RANK), F32),
                   jax.ShapeDtypeStruct((1, KV_RANK), F32)],
        in_specs=[row(D_INP)] + [row(W)] * 4 + [row(Q_RANK), row(KV_RANK), row(W)] + [row(HD)] * 5
                 + [vec(Q_RANK), vec(KV_RANK)],
        out_specs=[row(D_INP), vec(Q_RANK), vec(KV_RANK)],
        compiler_params=_params(("arbitrary",)),
    )(proj, dq, dk, dv, drg, dcqn, dckvn, dkr8, *tabs, qn, kvn)


def _post(o_ret, proj, gn):
    L, W = o_ret.shape
    tr = _tile(L, 384)

    def body(o_ref, rg_ref, gn_ref, out_ref):
        for h in range(HEADS):
            sl = slice(h * HD, (h + 1) * HD)
            o = o_ref[:, sl]
            rg = rg_ref[:, sl]
            n = o * lax.rsqrt(jnp.mean(o * o, axis=-1, keepdims=True) + EPS)
            out_ref[:, sl] = (n * gn_ref[:, sl] * (rg * _sigmoid(rg))).astype(BF16)

    row = pl.BlockSpec((tr, W), lambda i: (i, 0))
    return pl.pallas_call(
        body, name="ret_post", grid=(L // tr,), out_shape=jax.ShapeDtypeStruct((L, W), BF16),
        in_specs=[row, pl.BlockSpec((tr, W), lambda i: (i, C_RG // W)), pl.BlockSpec((1, W), lambda i: (0, 0))],
        out_specs=row, compiler_params=_params(("parallel",)),
    )(o_ret, proj, gn)


def _post_bwd(o_ret, proj, gn, dcat):
    L, W = o_ret.shape
    tr = _tile(L, 384)

    def body(o_ref, rg_ref, gn_ref, d_ref, do_ref, drg_ref, dgn_ref):
        @pl.when(pl.program_id(0) == 0)
        def _():
            dgn_ref[...] = jnp.zeros_like(dgn_ref)

        for h in range(HEADS):
            sl = slice(h * HD, (h + 1) * HD)
            o = o_ref[:, sl]
            rg = rg_ref[:, sl]
            d = d_ref[:, sl].astype(F32)
            gw = gn_ref[:, sl]
            r = lax.rsqrt(jnp.mean(o * o, axis=-1, keepdims=True) + EPS)
            n = o * r
            sg = _sigmoid(rg)
            si = rg * sg
            dn = d * gw * si
            dgn_ref[:, sl] += jnp.sum(d * n * si, axis=0, keepdims=True)
            drg_ref[:, sl] = d * n * gw * sg * (1.0 + rg * (1.0 - sg))
            do_ref[:, sl] = (r * (dn - o * (r * r) * jnp.mean(dn * o, axis=-1, keepdims=True))).astype(BF16)

    row = pl.BlockSpec((tr, W), lambda i: (i, 0))
    vec = pl.BlockSpec((1, W), lambda i: (0, 0))
    return pl.pallas_call(
        body, name="ret_post_bwd", grid=(L // tr,),
        out_shape=[jax.ShapeDtypeStruct((L, W), BF16), jax.ShapeDtypeStruct((L, W), F32),
                   jax.ShapeDtypeStruct((1, W), F32)],
        in_specs=[row, pl.BlockSpec((tr, W), lambda i: (i, C_RG // W)), vec, row],
        out_specs=[row, row, vec], compiler_params=_params(("arbitrary",)),
    )(o_ret, proj, gn, dcat)


RET_HEADS_PER_STEP = 4


def _lin_attn(name, q, k, v, lg, reverse):
    L, W = q.shape
    nc = L // BLK - 1
    G = RET_HEADS_PER_STEP

    def body(q_ref, k_ref, v_ref, lg_ref, o_ref, s_ref):
        n = lax.broadcasted_iota(jnp.int32, (BLK, BLK), 0).astype(F32)
        m = lax.broadcasted_iota(jnp.int32, (BLK, BLK), 1).astype(F32)
        dist = (m - n) if reverse else (n - m)
        consts = []
        for g in range(G):
            lgv = lg_ref[g, 0:1, :]
            dmask = jnp.where(dist >= 0, jnp.exp(lgv * jnp.maximum(dist, 0.0)), 0.0)
            c = dict(dmask=dmask, dmask0=jnp.where((n < N_META) & (m < N_META), dmask, 0.0),
                     gl=jnp.exp(lgv * float(BLK)))
            if reverse:
                c.update(inter=jnp.exp(lgv * (float(BLK) - n)), upd=jnp.exp(lgv * n),
                         inter0=jnp.where(n < N_META, jnp.exp(lgv * jnp.maximum(float(N_META) - n, 0.0)), 0.0))
            else:
                c.update(inter=jnp.exp(lgv * (n + 1.0)), upd=jnp.exp(lgv * (float(BLK) - 1.0 - n)),
                         upd0=jnp.where(n < N_META, jnp.exp(lgv * jnp.maximum(float(N_META) - 1.0 - n, 0.0)), 0.0))
            consts.append(c)

        def chunk(c):
            rows = pl.ds(pl.multiple_of(c * BLK, BLK), BLK)
            state = [s_ref[g] for g in range(G)]
            outs, new_state = [], []
            for g in range(G):
                cols = slice(g * HD, (g + 1) * HD)
                cg = consts[g]
                qc, kc, vc = q_ref[rows, cols], k_ref[rows, cols], v_ref[rows, cols]
                a = _dot(qc, kc, 'nt') * cg['dmask']
                outs.append(_dot(a.astype(BF16), vc, 'nn') + _dot(qc, state[g].astype(BF16), 'nn') * cg['inter'])
                new_state.append(state[g] * cg['gl'] + _dot((kc.astype(F32) * cg['upd']).astype(BF16), vc, 'tn'))
            for g in range(G):
                o_ref[rows, g * HD:(g + 1) * HD] = outs[g]
                s_ref[g] = new_state[g]

        def first_chunk(with_state):
            for g in range(G):
                cols = slice(g * HD, (g + 1) * HD)
                cg = consts[g]
                q0, k0, v0 = q_ref[0:BLK, cols], k_ref[0:BLK, cols], v_ref[0:BLK, cols]
                o0 = _dot((_dot(q0, k0, 'nt') * cg['dmask0']).astype(BF16), v0, 'nn')
                if with_state:
                    o0 = o0 + _dot(q0, s_ref[g].astype(BF16), 'nn') * cg['inter0']
                else:
                    s_ref[g] = _dot((k0.astype(F32) * cg['upd0']).astype(BF16), v0, 'tn')
                o_ref[0:BLK, cols] = o0

        if reverse:
            s_ref[...] = jnp.zeros_like(s_ref)

            def step(t, carry):
                chunk(nc - t)
                return carry

            lax.fori_loop(0, nc, step, 0)
            first_chunk(True)
        else:
            first_chunk(False)

            def step(t, carry):
                chunk(t + 1)
                return carry

            lax.fori_loop(0, nc, step, 0)

    col = pl.BlockSpec((L, G * HD), lambda h: (0, h))
    return pl.pallas_call(
        body, name=name, grid=(HEADS // G,), out_shape=jax.ShapeDtypeStruct((L, W), F32),
        in_specs=[col, col, col, pl.BlockSpec((G, 8, HD), lambda h: (h, 0, 0))], out_specs=col,
        scratch_shapes=[pltpu.VMEM((G, HD, HD), F32)], compiler_params=_params(("parallel",)),
    )(q, k, v, lg)


ATT_SCALE = (HD + ROPE) ** -0.5
LOG2E = 1.4426950408889634
Q_PRESCALE = ATT_SCALE * LOG2E
NEG = -1e30


ATT_TILE = 384
ATT_HEADS_PER_STEP = 2


def _att_valid(nq, nk, row0, col0):
    r = lax.broadcasted_iota(jnp.int32, (nq, nk), 0) + row0
    c = lax.broadcasted_iota(jnp.int32, (nq, nk), 1) + col0
    return (c <= r) & ((c < N_META) | (c >= BLK))


def _attn_fwd(qm, kn, krr, vm, comm=None):
    L = qm.shape[0]
    W = HEADS * HD
    T = _tile(L, ATT_TILE, BLK)
    nb = L // T
    G = ATT_HEADS_PER_STEP
    n_cm = comm.n if comm is not None else 0

    def body(*refs):
        q_ref, kn_ref, kr_ref, v_ref = refs[:4]
        o_ref, lse_ref = refs[4 + n_cm:6 + n_cm]
        m_sc, l_sc, acc_sc = refs[6 + 2 * n_cm:9 + 2 * n_cm]
        if comm is not None:
            cm_refs = (refs[4:4 + n_cm], refs[6 + n_cm:6 + 2 * n_cm], refs[9 + 2 * n_cm:])
            first, last = _grid_edges((HEADS // G, nb))

            @pl.when(first)
            def _():
                comm.start(*cm_refs)

        i = pl.program_id(1)
        m_sc[...] = jnp.full_like(m_sc, NEG)
        l_sc[...] = jnp.zeros_like(l_sc)
        acc_sc[...] = jnp.zeros_like(acc_sc)

        def tile(j, masked):
            rows = pl.ds(pl.multiple_of(j * T, T), T)
            kr = kr_ref[rows, :]
            valid = _att_valid(T, T, i * T, j * T) if masked else None
            m_prev = [m_sc[g] for g in range(G)]
            l_prev = [l_sc[g] for g in range(G)]
            acc_prev = [acc_sc[g] for g in range(G)]
            m_new, l_new, acc_new = [], [], []
            for g in range(G):
                k = jnp.concatenate([kn_ref[rows, g * HD:(g + 1) * HD], kr], axis=1)
                s = _dot(q_ref[:, g * QH:(g + 1) * QH], k, 'nt')
                if masked:
                    s = jnp.where(valid, s, NEG)
                m_new.append(jnp.maximum(m_prev[g], jnp.max(s, axis=-1, keepdims=True)))
                p = jnp.exp2(s - m_new[g])
                alpha = jnp.exp2(m_prev[g] - m_new[g])
                l_new.append(alpha * l_prev[g] + jnp.sum(p, axis=-1, keepdims=True))
                acc_new.append(alpha * acc_prev[g] + _dot(p.astype(BF16), v_ref[rows, g * HD:(g + 1) * HD], 'nn'))
            for g in range(G):
                m_sc[g] = m_new[g]
                l_sc[g] = l_new[g]
                acc_sc[g] = acc_new[g]

        tile(0, True)

        def mid(j, carry):
            tile(j, False)
            return carry

        lax.fori_loop(1, i, mid, 0)

        @pl.when(i > 0)
        def _():
            tile(i, True)

        for g in range(G):
            l = l_sc[g]
            o_ref[:, g * HD:(g + 1) * HD] = (acc_sc[g] / l).astype(o_ref.dtype)
            lse_ref[g] = jnp.broadcast_to(m_sc[g] + jnp.log(l) * LOG2E, (T, HD))

        if comm is not None:
            @pl.when(last)
            def _():
                comm.finish(*cm_refs)

    cm_specs = comm.specs if comm is not None else []
    res = pl.pallas_call(
        body, name="attn_fwd", grid=(HEADS // G, nb),
        out_shape=[jax.ShapeDtypeStruct((L, W), BF16), jax.ShapeDtypeStruct((HEADS, L, HD), F32)]
        + (comm.out_shapes if comm is not None else []),
        in_specs=[pl.BlockSpec((T, G * QH), lambda h, i: (i, h)), pl.BlockSpec((L, G * HD), lambda h, i: (0, h)),
                  pl.BlockSpec((L, HD), lambda h, i: (0, 0)), pl.BlockSpec((L, G * HD), lambda h, i: (0, h))]
        + cm_specs,
        out_specs=[pl.BlockSpec((T, G * HD), lambda h, i: (i, h)),
                   pl.BlockSpec((G, T, HD), lambda h, i: (h, i, 0))] + cm_specs,
        scratch_shapes=[pltpu.VMEM((G, T, 1), F32), pltpu.VMEM((G, T, 1), F32), pltpu.VMEM((G, T, HD), F32)]
        + (comm.scratch if comm is not None else []),
        compiler_params=_params(("arbitrary", "arbitrary")),
    )(qm, kn, krr, vm, *(comm.arrays if comm is not None else []))
    return res[:2], res[2:]


def _attn_bwd(qm, kn, krr, vm, o, dcat, lse, comm=None):
    L = qm.shape[0]
    W = HEADS * HD
    T = _tile(L, ATT_TILE, BLK)
    nb = L // T
    n_cm = comm.n if comm is not None else 0

    def body(*refs):
        q_ref, kn_ref, kr_ref, v_ref, o_ref, do_ref, lse_ref = refs[:7]
        dq_ref, dkn_ref, dkr_ref, dv_ref = refs[7 + n_cm:11 + n_cm]
        dl_sc, dk_sc, dv_sc = refs[11 + 2 * n_cm:14 + 2 * n_cm]
        if comm is not None:
            cm_refs = (refs[7:7 + n_cm], refs[11 + n_cm:11 + 2 * n_cm], refs[14 + 2 * n_cm:])
            first, last = _grid_edges((HEADS, nb))

            @pl.when(first)
            def _():
                comm.start(*cm_refs)

        j = pl.program_id(1)

        @pl.when(j == 0)
        def _():
            dq_ref[...] = jnp.zeros_like(dq_ref)

            def rowsum(t, carry):
                rows = pl.ds(pl.multiple_of(t * T, T), T)
                dl_sc[rows, :] = jnp.sum(do_ref[rows, :].astype(F32) * o_ref[rows, :].astype(F32), axis=-1,
                                         keepdims=True)
                return carry

            lax.fori_loop(0, nb, rowsum, 0)

        k = jnp.concatenate([kn_ref[...], kr_ref[...]], axis=1)
        v = v_ref[...]
        dk_sc[...] = jnp.zeros_like(dk_sc)
        dv_sc[...] = jnp.zeros_like(dv_sc)

        def tile(i, masked):
            rows = pl.ds(pl.multiple_of(i * T, T), T)
            q = q_ref[rows, :]
            do = do_ref[rows, :]
            s = _dot(q, k, 'nt')
            if masked:
                s = jnp.where(_att_valid(T, T, i * T, j * T), s, NEG)
            p = jnp.exp2(s - lse_ref[rows, 0:1])
            dv_sc[...] += _dot(p.astype(BF16), do, 'tn')
            ds = (p * (_dot(do, v, 'nt') - dl_sc[rows, :])).astype(BF16)
            dk_sc[...] += _dot(ds, q, 'tn')
            dq_ref[rows, :] += _dot(ds, k, 'nn')

        tile(j, True)

        def rest(masked):
            def step(i, carry):
                tile(i, masked)
                return carry
            lax.fori_loop(j + 1, nb, step, 0)

        @pl.when(j == 0)
        def _():
            rest(True)

        @pl.when(j > 0)
        def _():
            rest(False)

        dk = dk_sc[...] * (1.0 / LOG2E)
        dkn_ref[...] = dk[:, 0:HD].astype(BF16)
        dkr_ref[...] = dk[:, HD:QH]
        dv_ref[...] = dv_sc[...].astype(BF16)

        if comm is not None:
            @pl.when(last)
            def _():
                comm.finish(*cm_refs)

    blk = pl.BlockSpec((T, HD), lambda h, j: (j, h))
    cm_specs = comm.specs if comm is not None else []
    res = pl.pallas_call(
        body, name="attn_bwd", grid=(HEADS, nb),
        out_shape=[jax.ShapeDtypeStruct((L, HEADS * QH), F32), jax.ShapeDtypeStruct((L, W), BF16),
                   jax.ShapeDtypeStruct((L, W), F32), jax.ShapeDtypeStruct((L, W), BF16)]
        + (comm.out_shapes if comm is not None else []),
        in_specs=[pl.BlockSpec((L, QH), lambda h, j: (0, h)), blk, pl.BlockSpec((T, HD), lambda h, j: (j, 0)), blk,
                  pl.BlockSpec((L, HD), lambda h, j: (0, h)), pl.BlockSpec((L, HD), lambda h, j: (0, HEADS + h)),
                  pl.BlockSpec((None, L, HD), lambda h, j: (h, 0, 0))] + cm_specs,
        out_specs=[pl.BlockSpec((L, QH), lambda h, j: (0, h)), blk, blk, blk] + cm_specs,
        scratch_shapes=[pltpu.VMEM((L, 1), F32), pltpu.VMEM((T, QH), F32), pltpu.VMEM((T, HD), F32)]
        + (comm.scratch if comm is not None else []),
        compiler_params=_params(("arbitrary", "arbitrary")),
    )(qm, kn, krr, vm, o, dcat, lse, *(comm.arrays if comm is not None else []))
    return res[:4], res[4:]


def _unrope_q(dqm, tabs_m):
    L, W = dqm.shape
    tr = _tile(L, 384)

    def body(d_ref, cm_ref, sa_ref, sb_ref, out_ref):
        cm, sa, sb = cm_ref[...], sa_ref[...], sb_ref[...]
        for h in range(HEADS):
            out_ref[:, h * QH:h * QH + HD] = (d_ref[:, h * QH:h * QH + HD] * ATT_SCALE).astype(BF16)
            out_ref[:, h * QH + HD:(h + 1) * QH] = _rope_mla_t(d_ref[:, h * QH + HD:(h + 1) * QH] * ATT_SCALE, cm, sa,
                                                               sb).astype(BF16)

    row = pl.BlockSpec((tr, W), lambda i: (i, 0))
    tab = pl.BlockSpec((tr, HD), lambda i: (i, 0))
    return pl.pallas_call(
        body, name="unrope_q", grid=(L // tr,), out_shape=jax.ShapeDtypeStruct((L, W), BF16),
        in_specs=[row, tab, tab, tab], out_specs=row, compiler_params=_params(("parallel",)),
    )(dqm, *tabs_m)


def _q_up(cqn, wuq_p, tabs_m):
    L = cqn.shape[0]
    tm = _tile(L, 704)

    def ep(acc, cm, sa, sb):
        acc = acc * Q_PRESCALE
        parts = []
        for h in range(HEADS):
            parts.append(acc[:, h * QH:h * QH + HD])
            parts.append(_rope_mla(acc[:, h * QH + HD:(h + 1) * QH], cm, sa, sb))
        return (jnp.concatenate(parts, axis=1),)

    tab = pl.BlockSpec((tm, HD), lambda i, j: (i, 0))
    return _mm("mla_q_up", (L // tm, 1), ("parallel", "parallel"), None,
               [cqn, wuq_p], [pl.BlockSpec((tm, Q_RANK), lambda i, j: (i, 0)),
                              pl.BlockSpec((HEADS * QH, Q_RANK), lambda i, j: (0, 0))],
               [(0, 1, 'nt', 0)], [(tm, HEADS * QH)], list(tabs_m), [tab] * 3, ep,
               [jax.ShapeDtypeStruct((L, HEADS * QH), BF16)], [pl.BlockSpec((tm, HEADS * QH), lambda i, j: (i, 0))])[0]


def _mix_out(cat, w_out, h_in, post):
    L, K = cat.shape
    D = w_out.shape[1]
    tm, tk = _tile(L, 384), _tile(K, 512, 128)
    row = pl.BlockSpec((tm, D), lambda i, k: (i, 0))
    return _mm("mix_out", (L // tm, K // tk), ("parallel", "arbitrary"), 1,
               [cat, w_out], [pl.BlockSpec((tm, tk), lambda i, k: (i, k)), pl.BlockSpec((tk, D), lambda i, k: (k, 0))],
               [(0, 1, 'nn', 0)], [(tm, D)], [h_in, post], [row, pl.BlockSpec((1, D), lambda i, k: (0, 0))],
               _resnorm_epilogue(1.0), [jax.ShapeDtypeStruct((L, D), F32)] * 2, [row, row])


ADAM_BLOCK_ELEMS = 512 * 704


def _adam_math(w, g, m, v):
    m = ADAM_B1 * m + (1.0 - ADAM_B1) * g
    v = ADAM_B2 * v + (1.0 - ADAM_B2) * (g * g)
    m_hat = m / (1.0 - ADAM_B1 ** ADAM_STEP)
    v_hat = v / (1.0 - ADAM_B2 ** ADAM_STEP)
    delta = -ADAM_LR * (m_hat / (jnp.sqrt(v_hat) + ADAM_EPS) + ADAM_WD * w)
    return delta, m, v


def _adam(name, w, m, v, g_slots=None, g=None, after=None):
    R, C = w.shape
    tr, tc = _tile(R, max(16, ADAM_BLOCK_ELEMS // C // 16 * 16), 16), C
    if tr * tc > ADAM_BLOCK_ELEMS:
        tr, tc = R, _tile(C, max(128, ADAM_BLOCK_ELEMS // R // 128 * 128), 128)
    from_slots = g_slots is not None

    def body(w_ref, m_ref, v_ref, g_ref, *rest):
        go_ref, d_ref, mo_ref, vo_ref = rest[-4:]
        if from_slots:
            grad = g_ref[0].astype(F32)
            for s in range(1, N_DEV):
                grad = grad + g_ref[s].astype(F32)
        else:
            grad = g_ref[...]
        delta, mn, vn = _adam_math(w_ref[...], grad, m_ref[...], v_ref[...])
        go_ref[...] = grad
        d_ref[...] = delta
        mo_ref[...] = mn
        vo_ref[...] = vn

    row = pl.BlockSpec((tr, tc), lambda i, j: (i, j))
    gspec = pl.BlockSpec((N_DEV, tr, tc), lambda i, j: (0, i, j)) if from_slots else row
    order = [] if after is None else [after]
    return pl.pallas_call(
        body, name=name, grid=(R // tr, C // tc), out_shape=[jax.ShapeDtypeStruct((R, C), F32)] * 4,
        in_specs=[row, row, row, gspec] + [pl.BlockSpec(memory_space=pl.ANY)] * len(order), out_specs=[row] * 4,
        compiler_params=_params(("parallel", "parallel")),
    )(w, m, v, g_slots if from_slots else g, *order)


def _unblock(gathered):
    n, r, c = gathered.shape
    return jnp.transpose(gathered, (1, 0, 2)).reshape(r, n * c)


def _reblock(full, c):
    r = full.shape[0]
    return jnp.transpose(full[:, :N_DEV * c].reshape(r, N_DEV, c), (1, 0, 2))


def _step(x, target, w, mom, vel):
    S, D = x.shape[1], x.shape[2]
    L = S + BLK
    def sq(a, n):
        if a.ndim == 2:
            return a
        if n in TRANSPOSED:
            a = jnp.swapaxes(a, 1, 2)
        return a.reshape(a.shape[1:])

    def unsq(o, n):
        o = o.reshape((1,) + o.shape)
        return jnp.swapaxes(o, 1, 2) if n in TRANSPOSED else o

    p = {n: sq(w[n], n) for n in WEIGHTS if n != 'meta_tokens'}
    gather = lambda names: _Exchange([p[n].astype(BF16) for n in names], False)
    scatter = lambda blocks: _Exchange(blocks, True)
    in_s, uq_s = p['w_in'].shape[0], p['mla_w_uq'].shape[0]
    assert uq_s == HD + ROPE and N_DEV == HEADS, "a w_uq shard is one head's columns"
    tabs = _rope_tables(L)
    tabs_m = tabs[2:]
    lg = jnp.broadcast_to(jnp.log(1.0 - 2.0 ** (-5.0 - jnp.arange(HEADS, dtype=F32)))[:, None, None], (HEADS, 8, HD))
    R = {}

    wg1, meta = _exchange("gather_first", [p['ffn1_w_gate'].astype(BF16), w['meta_tokens']], False)
    h0 = jnp.concatenate([_unblock(meta), jnp.zeros((BLK - N_META, D), F32), x[0]], axis=0)
    a1 = _norm_fwd(h0, p['ffn1_pre_norm'])
    g1, (wu1,) = _ffn_gate(a1, wg1, comm=gather(['ffn1_w_up']))
    (u1, hid1), (wd1,) = _ffn_up_gated(a1, wu1, g1, comm=gather(['ffn1_w_down']))
    (f1, h1), (w_in_g,) = _ffn_down(hid1, wd1, h0, p['ffn1_post_norm'], comm=gather(['w_in']))

    w_in = jnp.pad(w_in_g.reshape(N_DEV * in_s, D), ((0, D_INP - N_DEV * in_s), (0, 0)))
    um = _norm_fwd(h1, p['mix_pre_norm'])
    proj, (uq_g, uk_g, uv_g, wout_g) = _mm_nt("mix_in", [(um, w_in)], F32, tn_target=1664,
                                              comm=gather(['mla_w_uq', 'mla_w_uk', 'mla_w_uv', 'w_out']))
    wuq = jnp.pad(uq_g, ((0, 0), (0, QH - uq_s), (0, 0))).reshape(HEADS * QH, Q_RANK)
    wuk, wuv, w_out = _unblock(uk_g), _unblock(uv_g), wout_g.reshape(-1, D)
    qr, kr, vr, cqn, ckvn, krr = _prep(proj, tabs, p['mla_q_norm'], p['mla_kv_norm'])
    qm = _q_up(cqn, wuq, tabs_m)
    kn = _mm_nn("mla_k_up", ckvn, wuk, BF16)
    vm = _mm_nn("mla_v_up", ckvn, wuv, BF16)
    (o_mla, lse), (wg2, wu2) = _attn_fwd(qm, kn, krr, vm, comm=gather(['ffn2_w_gate', 'ffn2_w_up']))
    o_ret = _lin_attn("ret_fwd", qr, kr, vr, lg, False)
    ret = _post(o_ret, proj, p['ret_group_norm'])
    cat = jnp.concatenate([ret, o_mla], axis=1)
    m, h2 = _mix_out(cat, w_out, h1, p['mix_post_norm'])

    a2 = _norm_fwd(h2, p['ffn2_pre_norm'])
    (g2, u2, hid2), (wd2,) = _ffn_up(a2, wg2, wu2, comm=gather(['ffn2_w_down']))
    f2, h3 = _ffn_down(hid2, wd2, h2, p['ffn2_post_norm'])
    dh3, loss_blk = _loss(h3, target[0])

    dsmall = {}
    df2, dsmall['ffn2_post_norm'] = _norm_bwd(f2, p['ffn2_post_norm'], dh3, None, 0.5, BF16)
    dg2, du2 = _ffn_dhid(df2, wd2, g2, u2)
    dwd2 = _ffn_dwd(hid2, df2)
    (dwg2, dwu2), (R['ffn2_w_down'],) = _ffn_dwgu(a2, dg2, du2, comm=scatter([dwd2]))
    da2, (R['ffn2_w_gate'],) = _ffn_da(dg2, du2, wg2, wu2, comm=scatter([dwg2]))
    dh2, dsmall['ffn2_pre_norm'] = _norm_bwd(h2, p['ffn2_pre_norm'], da2, dh3, 1.0, F32)

    dm, dsmall['mix_post_norm'] = _norm_bwd(m, p['mix_post_norm'], dh2, None, 1.0, BF16)
    dcat = _mm_nt("mix_dcat", [(dm, w_out)], BF16)
    dwout = _mm_tn("mix_dwout", cat, [dm])[0]
    do_ret, drg, dsmall['ret_group_norm'] = _post_bwd(o_ret, proj, p['ret_group_norm'], dcat)
    dqr = _lin_attn("ret_dq", do_ret, vr, kr, lg, False)
    dkr = _lin_attn("ret_dk", vr, do_ret, qr, lg, True)
    dvr = _lin_attn("ret_dv", kr, qr, do_ret, lg, True)
    (dqm, dkn, dkr8, dvm), (R['ffn2_w_up'], R['w_out']) = _attn_bwd(
        qm, kn, krr, vm, o_mla, dcat, lse, comm=scatter([dwu2, dwout.reshape(N_DEV, -1, D)]))
    dqp = _unrope_q(dqm, tabs_m)
    dwuq = _mm_tn("mla_dwuq", dqp, [cqn])[0]
    dcqn = _mm_nn("mla_dcq", dqp, wuq, F32)
    dwuk, dwuv = _mm_tn("mla_dwukv", ckvn, [dkn, dvm])
    dckvn = _mm_nt("mla_dckv", [(dkn, wuk), (dvm, wuv)], F32)
    dproj, dsmall['mla_q_norm'], dsmall['mla_kv_norm'] = _prep_bwd(
        proj, dqr, dkr, dvr, drg, dcqn, dckvn, dkr8, tabs, p['mla_q_norm'], p['mla_kv_norm'])
    dwuq_b = dwuq.reshape(HEADS, QH, Q_RANK)[:, :uq_s]
    (dwin,), (R['mla_w_uq'], R['mla_w_uk'], R['mla_w_uv']) = _mm_tn(
        "mix_dwin", dproj, [um], comm=scatter([dwuq_b, _reblock(dwuk, p['mla_w_uk'].shape[1]),
                                               _reblock(dwuv, p['mla_w_uv'].shape[1])]))
    dwin_b = dwin[:N_DEV * in_s].reshape(N_DEV, in_s, D)
    half = D // 2
    dum, (r_win_a,) = _mm_nn("mix_du", dproj, w_in, F32, tn_target=512, comm=scatter([dwin_b[:, :, :half]]))
    dh1, dsmall['mix_pre_norm'] = _norm_bwd(h1, p['mix_pre_norm'], dum, dh2, 1.0, F32)

    df1, dsmall['ffn1_post_norm'] = _norm_bwd(f1, p['ffn1_post_norm'], dh1, None, 0.5, BF16)
    (dg1, du1), (r_win_b,) = _ffn_dhid(df1, wd1, g1, u1, comm=scatter([dwin_b[:, :, half:]]))
    R['w_in'] = jnp.concatenate([r_win_a, r_win_b], axis=2)
    dwd1 = _ffn_dwd(hid1, df1)
    (dwg1, dwu1), (R['ffn1_w_down'],) = _ffn_dwgu(a1, dg1, du1, comm=scatter([dwd1]))
    da1, (R['ffn1_w_gate'],) = _ffn_da(dg1, du1, wg1, wu1, comm=scatter([dwg1]))
    dh0, dsmall['ffn1_pre_norm'] = _norm_bwd(h0, p['ffn1_pre_norm'], da1, dh1, 1.0, F32)
    tail_sems_s, tail_sems_r, tail_src, tail_land, token = _scatter_start(dwu1)

    def slab(a):
        a = a.reshape(-1, 128)
        return jnp.pad(a, ((0, (-a.shape[0]) % 8), (0, 0)))

    slab_rows = lambda n: -(-(p[n].shape[-1] // 128) // 8) * 8
    packed = jnp.concatenate([slab(dsmall[n]) for n in SMALL] + [slab(dh0[:N_META]), loss_blk], axis=0)
    red = _allreduce_small(packed + token[0, 0])
    offs = sum(slab_rows(n) for n in SMALL)
    n_small = offs
    gmeta_full = red[offs:offs + N_META * D // 128].reshape(N_META, D)
    offs += N_META * D // 128
    loss = red[offs, 0]

    grad, delta, new_m, new_v = {}, {}, {}, {}
    meanwhile = []
    for n in BIG:
        if n == 'ffn1_w_up':
            continue
        outs = _adam("adam_" + n, p[n], sq(mom[n], n), sq(vel[n], n), g_slots=R[n], after=token)
        meanwhile.append(outs[0])
        grad[n], delta[n], new_m[n], new_v[n] = [unsq(o, n) for o in outs]
    pack = lambda d: jnp.concatenate([slab(d[n]) for n in SMALL], axis=0)
    outs = _adam("adam_small", pack(w), pack(mom), pack(vel), g=red[:n_small])
    meanwhile.append(outs[0])
    offs = 0
    for n in SMALL:
        r = p[n].shape[-1] // 128
        grad[n], delta[n], new_m[n], new_v[n] = [o[offs:offs + r].reshape(w[n].shape) for o in outs]
        offs += slab_rows(n)
    dev = 4 * lax.axis_index("x") + 2 * lax.axis_index("y") + lax.axis_index("c")
    mcols = w['meta_tokens'].shape[1]
    gmeta = lax.dynamic_slice(gmeta_full, (0, dev * mcols), (N_META, mcols))
    outs = _adam("adam_meta", w['meta_tokens'], mom['meta_tokens'], vel['meta_tokens'], g=gmeta)
    grad['meta_tokens'], delta['meta_tokens'], new_m['meta_tokens'], new_v['meta_tokens'] = outs
    meanwhile.append(outs[0])
    n = 'ffn1_w_up'
    slots = _scatter_wait(tail_sems_s, tail_sems_r, tail_src, tail_land, meanwhile)
    outs = _adam("adam_" + n, p[n], sq(mom[n], n), sq(vel[n], n), g_slots=slots)
    grad[n], delta[n], new_m[n], new_v[n] = [unsq(o, n) for o in outs]

    return (loss, dh0[BLK:][None], *[grad[n] for n in WEIGHTS], *[delta[n] for n in WEIGHTS],
            *[new_m[n] for n in WEIGHTS], *[new_v[n] for n in WEIGHTS])


def kernel(x, meta_tokens, ffn1_pre_norm, ffn1_w_gate, ffn1_w_up, ffn1_w_down, ffn1_post_norm, mix_pre_norm, w_in, ret_group_norm, mla_q_norm, mla_w_uq, mla_kv_norm, mla_w_uk, mla_w_uv, w_out, mix_post_norm, ffn2_pre_norm, ffn2_w_gate, ffn2_w_up, ffn2_w_down, ffn2_post_norm, loss_target, m_meta_tokens, m_ffn1_pre_norm, m_ffn1_w_gate, m_ffn1_w_up, m_ffn1_w_down, m_ffn1_post_norm, m_mix_pre_norm, m_w_in, m_ret_group_norm, m_mla_q_norm, m_mla_w_uq, m_mla_kv_norm, m_mla_w_uk, m_mla_w_uv, m_w_out, m_mix_post_norm, m_ffn2_pre_norm, m_ffn2_w_gate, m_ffn2_w_up, m_ffn2_w_down, m_ffn2_post_norm, v_meta_tokens, v_ffn1_pre_norm, v_ffn1_w_gate, v_ffn1_w_up, v_ffn1_w_down, v_ffn1_post_norm, v_mix_pre_norm, v_w_in, v_ret_group_norm, v_mla_q_norm, v_mla_w_uq, v_mla_kv_norm, v_mla_w_uk, v_mla_w_uv, v_w_out, v_mix_post_norm, v_ffn2_pre_norm, v_ffn2_w_gate, v_ffn2_w_up, v_ffn2_w_down, v_ffn2_post_norm):
    w = dict(zip(WEIGHTS, (meta_tokens, ffn1_pre_norm, ffn1_w_gate, ffn1_w_up, ffn1_w_down, ffn1_post_norm,
                           mix_pre_norm, w_in, ret_group_norm, mla_q_norm, mla_w_uq, mla_kv_norm, mla_w_uk, mla_w_uv,
                           w_out, mix_post_norm, ffn2_pre_norm, ffn2_w_gate, ffn2_w_up, ffn2_w_down, ffn2_post_norm)))
    mom = dict(zip(WEIGHTS, (m_meta_tokens, m_ffn1_pre_norm, m_ffn1_w_gate, m_ffn1_w_up, m_ffn1_w_down,
                             m_ffn1_post_norm, m_mix_pre_norm, m_w_in, m_ret_group_norm, m_mla_q_norm, m_mla_w_uq,
                             m_mla_kv_norm, m_mla_w_uk, m_mla_w_uv, m_w_out, m_mix_post_norm, m_ffn2_pre_norm,
                             m_ffn2_w_gate, m_ffn2_w_up, m_ffn2_w_down, m_ffn2_post_norm)))
    vel = dict(zip(WEIGHTS, (v_meta_tokens, v_ffn1_pre_norm, v_ffn1_w_gate, v_ffn1_w_up, v_ffn1_w_down,
                             v_ffn1_post_norm, v_mix_pre_norm, v_w_in, v_ret_group_norm, v_mla_q_norm, v_mla_w_uq,
                             v_mla_kv_norm, v_mla_w_uk, v_mla_w_uv, v_w_out, v_mix_post_norm, v_ffn2_pre_norm,
                             v_ffn2_w_gate, v_ffn2_w_up, v_ffn2_w_down, v_ffn2_post_norm)))
    return _step(x, loss_target, w, mom, vel)
```

```python
import functools
import math

import jax
import jax.numpy as jnp
from jax import lax
from jax.experimental import pallas as pl
from jax.experimental.pallas import tpu as pltpu

N_DEV = 8
N_META = 16
BLK = 128
HEADS = 8
HD = 128
ROPE = 64
Q_RANK = 512
KV_RANK = 256
QH = 2 * HD
D_INP = 4 * HEADS * HD + Q_RANK + KV_RANK + BLK
ROPE_THETA = 10000.0
EPS = 1e-6
ADAM_LR = 0.001
ADAM_B1 = 0.9
ADAM_B2 = 0.999
ADAM_EPS = 1e-08
ADAM_WD = 0.01
ADAM_STEP = 10
V7X_VMEM_LIMIT = 48 * 1024 * 1024
MESH = pl.DeviceIdType.MESH
F32 = jnp.float32
BF16 = jnp.bfloat16

WEIGHTS = ['meta_tokens', 'ffn1_pre_norm', 'ffn1_w_gate', 'ffn1_w_up', 'ffn1_w_down', 'ffn1_post_norm',
           'mix_pre_norm', 'w_in', 'ret_group_norm', 'mla_q_norm', 'mla_w_uq', 'mla_kv_norm', 'mla_w_uk',
           'mla_w_uv', 'w_out', 'mix_post_norm', 'ffn2_pre_norm', 'ffn2_w_gate', 'ffn2_w_up', 'ffn2_w_down',
           'ffn2_post_norm']
SMALL = ['ffn1_pre_norm', 'ffn1_post_norm', 'mix_pre_norm', 'ret_group_norm', 'mla_q_norm', 'mla_kv_norm',
         'mix_post_norm', 'ffn2_pre_norm', 'ffn2_post_norm']
TRANSPOSED = ('ffn1_w_gate', 'ffn1_w_up', 'ffn2_w_gate', 'ffn2_w_up', 'w_in', 'mla_w_uq')
BIG = ['ffn1_w_gate', 'ffn1_w_up', 'ffn1_w_down', 'w_in', 'mla_w_uq', 'mla_w_uk', 'mla_w_uv', 'w_out',
       'ffn2_w_gate', 'ffn2_w_up', 'ffn2_w_down']

_DIMS = {'nn': (((1,), (0,)), ((), ())), 'nt': (((1,), (1,)), ((), ())), 'tn': (((0,), (0,)), ((), ()))}


def _tile(n, target, mult=16):
    best = None
    for t in range(mult, min(n, target) + 1, mult):
        if n % t == 0:
            best = t
    return best if best is not None else n


def _params(sem):
    return pltpu.CompilerParams(dimension_semantics=sem, vmem_limit_bytes=V7X_VMEM_LIMIT)


def _dot(a, b, dims):
    return lax.dot_general(a, b, _DIMS[dims], preferred_element_type=F32)


def _sigmoid(x):
    return 0.5 * jnp.tanh(0.5 * x) + 0.5


def _me_and_peers():
    x, y, c = lax.axis_index("x"), lax.axis_index("y"), lax.axis_index("c")

    def peer(j):
        px = 1 - x if (j >> 2) & 1 else x
        py = 1 - y if (j >> 1) & 1 else y
        pc = 1 - c if j & 1 else c
        return (px, py, pc), 4 * px + 2 * py + pc

    return 4 * x + 2 * y + c, peer


class _Exchange:
    def __init__(self, arrays, per_peer):
        self.arrays = list(arrays)
        self.per_peer = per_peer
        self.n = len(self.arrays)
        self.out_shapes = [jax.ShapeDtypeStruct((N_DEV,) + tuple(a.shape[1:] if per_peer else a.shape), a.dtype)
                           for a in self.arrays]
        self.specs = [pl.BlockSpec(memory_space=pl.ANY)] * self.n
        self.scratch = [pltpu.SemaphoreType.DMA((7 * self.n,)), pltpu.SemaphoreType.DMA((7 * self.n,)),
                        pltpu.SemaphoreType.DMA((self.n,))]

    def _copies(self, src, dst, sems):
        send_sems, recv_sems, local_sems = sems
        me, peer = _me_and_peers()
        sib, _ = peer(1)
        local, sends, recvs, passes = [], {}, {}, {}
        for k in range(self.n):
            own = src[k].at[me] if self.per_peer else src[k]
            local.append(pltpu.make_async_copy(own, dst[k].at[me], local_sems.at[k]))
            for j in range(1, N_DEV):
                pid, pidx = peer(j)
                out = src[k].at[pidx] if self.per_peer else src[k]
                sem = dict(send_sem=send_sems.at[k * 7 + j - 1], recv_sem=recv_sems.at[k * 7 + j - 1])
                recvs[k, j] = pltpu.make_async_remote_copy(src_ref=out, dst_ref=dst[k].at[pidx], device_id=pid,
                                                           device_id_type=MESH, **sem)
                if self.per_peer or j in (1, 2, 4, 6):
                    sends[k, j] = pltpu.make_async_remote_copy(src_ref=out, dst_ref=dst[k].at[me], device_id=pid,
                                                               device_id_type=MESH, **sem)
                else:
                    _, origin = peer(j ^ 1)
                    passes[k, j ^ 1] = pltpu.make_async_remote_copy(
                        src_ref=dst[k].at[origin], dst_ref=dst[k].at[origin], device_id=sib, device_id_type=MESH, **sem)
        return local, sends, recvs, passes

    def start(self, src, dst, sems):
        local, sends, _, _ = self._copies(src, dst, sems)
        for cp in local + list(sends.values()):
            cp.start()

    def finish(self, src, dst, sems):
        local, sends, recvs, passes = self._copies(src, dst, sems)
        for key, cp in passes.items():
            recvs[key].wait_recv()
            cp.start()
        for key, cp in recvs.items():
            if key not in passes:
                cp.wait_recv()
        for cp in list(sends.values()) + list(passes.values()):
            cp.wait_send()
        for cp in local:
            cp.wait()


def _grid_edges(grid):
    first, last = None, None
    for a, n in enumerate(grid):
        f, l = pl.program_id(a) == 0, pl.program_id(a) == n - 1
        first = f if first is None else first & f
        last = l if last is None else last & l
    return first, last


def _exchange(name, arrays, per_peer):
    ex = _Exchange(arrays, per_peer)
    n = ex.n

    def body(*refs):
        ex.start(refs[:n], refs[n:2 * n], refs[2 * n:])
        ex.finish(refs[:n], refs[n:2 * n], refs[2 * n:])

    return pl.pallas_call(body, name=name, out_shape=ex.out_shapes, in_specs=ex.specs, out_specs=ex.specs,
                          scratch_shapes=ex.scratch)(*arrays)


def _scatter_start(blocks):
    def body(src_ref, land_ref, send_sems, recv_sems, src_thru, land_thru, token, local_sem):
        me, peer = _me_and_peers()
        local = pltpu.make_async_copy(src_ref.at[me], land_ref.at[me], local_sem)
        local.start()
        for j in range(1, N_DEV):
            pid, pidx = peer(j)
            pltpu.make_async_remote_copy(src_ref=src_ref.at[pidx], dst_ref=land_ref.at[me],
                                         send_sem=send_sems.at[j - 1], recv_sem=recv_sems.at[j - 1],
                                         device_id=pid, device_id_type=MESH).start()
        local.wait()
        token[...] = jnp.zeros_like(token)

    hbm = pl.BlockSpec(memory_space=pltpu.HBM)
    sem = pl.BlockSpec(memory_space=pltpu.SEMAPHORE)
    return pl.pallas_call(
        body, name="scatter_tail_start",
        out_shape=(pltpu.SemaphoreType.DMA((7,)), pltpu.SemaphoreType.DMA((7,)), pltpu.HBM(blocks.shape, blocks.dtype),
                   pltpu.HBM(blocks.shape, blocks.dtype), jax.ShapeDtypeStruct((8, 128), F32)),
        in_specs=(hbm, hbm), out_specs=(sem, sem, hbm, hbm, pl.BlockSpec(memory_space=pltpu.VMEM)),
        input_output_aliases={0: 2, 1: 3}, scratch_shapes=[pltpu.SemaphoreType.DMA],
        compiler_params=pltpu.CompilerParams(has_side_effects=pltpu.SideEffectType.DATAFLOW_SIDE_EFFECTING),
    )(pltpu.with_memory_space_constraint(blocks, pltpu.HBM),
      pltpu.with_memory_space_constraint(lax.empty(blocks.shape, blocks.dtype), pltpu.HBM))


def _scatter_wait(send_sems, recv_sems, src_thru, land_thru, after):
    n_after = len(after)

    def body(src_ref, land_ref, send_sems, recv_sems, *rest):
        me, peer = _me_and_peers()
        for j in range(1, N_DEV):
            pid, pidx = peer(j)
            cp = pltpu.make_async_remote_copy(src_ref=src_ref.at[pidx], dst_ref=land_ref.at[pidx],
                                              send_sem=send_sems.at[j - 1], recv_sem=recv_sems.at[j - 1],
                                              device_id=pid, device_id_type=MESH)
            cp.wait_send()
            cp.wait_recv()

    hbm = pl.BlockSpec(memory_space=pltpu.HBM)
    sem = pl.BlockSpec(memory_space=pltpu.SEMAPHORE)
    return pl.pallas_call(
        body, name="scatter_tail_wait",
        out_shape=(pltpu.HBM(src_thru.shape, src_thru.dtype), pltpu.HBM(land_thru.shape, land_thru.dtype)),
        in_specs=(hbm, hbm, sem, sem) + (pl.BlockSpec(memory_space=pl.ANY),) * n_after, out_specs=(hbm, hbm),
        input_output_aliases={0: 0, 1: 1},
        compiler_params=pltpu.CompilerParams(has_side_effects=pltpu.SideEffectType.DATAFLOW_SIDE_EFFECTING),
    )(src_thru, land_thru, send_sems, recv_sems, *after)[1]


def _allreduce_small(v):
    rows = v.shape[0]

    def body(v_ref, out_ref, buf, send_sems, recv_sems):
        me, peer = _me_and_peers()
        buf[pl.ds(me, 1)] = v_ref[...][None]
        sends = []
        for j in range(1, N_DEV):
            pid, _ = peer(j)
            cp = pltpu.make_async_remote_copy(src_ref=v_ref, dst_ref=buf.at[me], send_sem=send_sems.at[j - 1],
                                              recv_sem=recv_sems.at[j - 1], device_id=pid, device_id_type=MESH)
            cp.start()
            sends.append(cp)
        for j in range(1, N_DEV):
            pid, pidx = peer(j)
            pltpu.make_async_remote_copy(src_ref=v_ref, dst_ref=buf.at[pidx], send_sem=send_sems.at[j - 1],
                                         recv_sem=recv_sems.at[j - 1], device_id=pid,
                                         device_id_type=MESH).wait_recv()
        for cp in sends:
            cp.wait_send()
        acc = buf[0]
        for s in range(1, N_DEV):
            acc = acc + buf[s]
        out_ref[...] = acc

    vm = pl.BlockSpec(memory_space=pltpu.VMEM)
    return pl.pallas_call(
        body, name="allreduce_small", out_shape=jax.ShapeDtypeStruct(v.shape, F32),
        in_specs=[vm], out_specs=vm,
        scratch_shapes=[pltpu.VMEM((N_DEV, rows, 128), F32), pltpu.SemaphoreType.DMA((7,)),
                        pltpu.SemaphoreType.DMA((7,))],
    )(v)


def _mm(name, grid, sem, k_axis, ops, op_specs, pairs, acc_shapes, extras, extra_specs, epilogue, outs, out_specs,
        comm=None):
    n_op, n_ex, n_out = len(ops), len(extras), len(outs)
    nk = grid[k_axis] if k_axis is not None else 1
    n_acc = len(acc_shapes) if nk > 1 else 0
    n_cm = comm.n if comm is not None else 0

    def body(*refs):
        op_refs = refs[:n_op]
        ex_refs = refs[n_op:n_op + n_ex]
        n_in = n_op + n_ex + n_cm
        out_refs = refs[n_in:n_in + n_out]
        acc_refs = refs[n_in + n_out + n_cm:n_in + n_out + n_cm + n_acc]
        if comm is not None:
            cm_refs = (refs[n_op + n_ex:n_in], refs[n_in + n_out:n_in + n_out + n_cm],
                       refs[n_in + n_out + n_cm + n_acc:])
            first, last = _grid_edges(grid)

            @pl.when(first)
            def _():
                comm.start(*cm_refs)

        def finish(vals):
            res = epilogue(*vals, *[e[...] for e in ex_refs])
            for o, r in zip(out_refs, res):
                o[...] = r.astype(o.dtype)

        if nk == 1:
            parts = [None] * len(acc_shapes)
            for li, ri, dims, ai in pairs:
                d = _dot(op_refs[li][...], op_refs[ri][...], dims)
                parts[ai] = d if parts[ai] is None else parts[ai] + d
            finish(parts)
        else:
            k = pl.program_id(k_axis)

            @pl.when(k == 0)
            def _():
                for a in acc_refs:
                    a[...] = jnp.zeros_like(a)

            for li, ri, dims, ai in pairs:
                acc_refs[ai][...] += _dot(op_refs[li][...], op_refs[ri][...], dims)

            @pl.when(k == nk - 1)
            def _():
                finish([a[...] for a in acc_refs])

        if comm is not None:
            @pl.when(last)
            def _():
                comm.finish(*cm_refs)

    scratch = [pltpu.VMEM(s, F32) for s in acc_shapes] if nk > 1 else []
    if comm is None:
        return pl.pallas_call(
            body, name=name, grid=grid, out_shape=outs,
            in_specs=list(op_specs) + list(extra_specs), out_specs=list(out_specs),
            scratch_shapes=scratch, compiler_params=_params(sem),
        )(*ops, *extras)
    res = pl.pallas_call(
        body, name=name, grid=grid, out_shape=list(outs) + comm.out_shapes,
        in_specs=list(op_specs) + list(extra_specs) + comm.specs, out_specs=list(out_specs) + comm.specs,
        scratch_shapes=scratch + comm.scratch, compiler_params=_params(("arbitrary",) * len(grid)),
    )(*ops, *extras, *comm.arrays)
    return res[:n_out], res[n_out:]


def _with_comm(res, comm, pick):
    if comm is None:
        return pick(res)
    return pick(res[0]), res[1]


def _mm_nn(name, a, w, out_dtype, tm_target=704, tn_target=1664, epilogue=None, extras=(), extra_specs=(), comm=None):
    L, K = a.shape
    N = w.shape[1]
    tm, tn = _tile(L, tm_target), _tile(N, tn_target, 128)
    ep = epilogue if epilogue is not None else (lambda acc: (acc,))
    res = _mm(name, (L // tm, N // tn), ("parallel", "parallel"), None,
              [a, w], [pl.BlockSpec((tm, K), lambda i, j: (i, 0)), pl.BlockSpec((K, tn), lambda i, j: (0, j))],
              [(0, 1, 'nn', 0)], [(tm, tn)], list(extras), list(extra_specs), ep,
              [jax.ShapeDtypeStruct((L, N), out_dtype)], [pl.BlockSpec((tm, tn), lambda i, j: (i, j))], comm=comm)
    return _with_comm(res, comm, lambda o: o[0])


def _mm_nt(name, pairs_aw, out_dtype, tm_target=704, tn_target=512, comm=None):
    L = pairs_aw[0][0].shape[0]
    N = pairs_aw[0][1].shape[0]
    tm, tn = _tile(L, tm_target), _tile(N, tn_target, 128)
    ops, specs, pairs = [], [], []
    for t, (a, w) in enumerate(pairs_aw):
        K = a.shape[1]
        ops += [a, w]
        specs += [pl.BlockSpec((tm, K), lambda i, j: (i, 0)), pl.BlockSpec((tn, K), lambda i, j: (j, 0))]
        pairs.append((2 * t, 2 * t + 1, 'nt', 0))
    res = _mm(name, (L // tm, N // tn), ("parallel", "parallel"), None, ops, specs, pairs, [(tm, tn)], [], [],
              lambda acc: (acc,), [jax.ShapeDtypeStruct((L, N), out_dtype)],
              [pl.BlockSpec((tm, tn), lambda i, j: (i, j))], comm=comm)
    return _with_comm(res, comm, lambda o: o[0])


def _mm_tn(name, a, bs, out_dtype=BF16, tk_target=704, tn_target=1664, tm_target=2048, comm=None):
    L, M = a.shape
    N = bs[0].shape[1]
    tk, tn, tm = _tile(L, tk_target), _tile(N, tn_target, 128), _tile(M, tm_target, 128)
    nb = len(bs)
    ops = [a] + list(bs)
    specs = [pl.BlockSpec((tk, tm), lambda i, j, k: (k, i))] + [pl.BlockSpec((tk, tn), lambda i, j, k: (k, j))] * nb
    res = _mm(name, (M // tm, N // tn, L // tk), ("parallel", "parallel", "arbitrary"), 2, ops, specs,
              [(0, 1 + t, 'tn', t) for t in range(nb)], [(tm, tn)] * nb, [], [], lambda *acc: acc,
              [jax.ShapeDtypeStruct((M, N), out_dtype)] * nb,
              [pl.BlockSpec((tm, tn), lambda i, j, k: (i, j))] * nb, comm=comm)
    return _with_comm(res, comm, lambda o: o)


def _norm_fwd(x, w):
    L, D = x.shape
    tr = _tile(L, 512)

    def body(x_ref, w_ref, y_ref):
        v = x_ref[...]
        r = lax.rsqrt(jnp.mean(v * v, axis=-1, keepdims=True) + EPS)
        y_ref[...] = (v * r * w_ref[...]).astype(y_ref.dtype)

    return pl.pallas_call(
        body, name="norm_fwd", grid=(L // tr,), out_shape=jax.ShapeDtypeStruct((L, D), BF16),
        in_specs=[pl.BlockSpec((tr, D), lambda i: (i, 0)), pl.BlockSpec((1, D), lambda i: (0, 0))],
        out_specs=pl.BlockSpec((tr, D), lambda i: (i, 0)), compiler_params=_params(("parallel",)),
    )(x, w)


def _norm_bwd_math(x, w, dy):
    r = lax.rsqrt(jnp.mean(x * x, axis=-1, keepdims=True) + EPS)
    gy = dy * w
    dx = r * (gy - x * (r * r) * jnp.mean(gy * x, axis=-1, keepdims=True))
    dw = jnp.sum(dy * x * r, axis=0, keepdims=True)
    return dx, dw


def _norm_bwd(x, w, dy, res, scale, out_dtype):
    L, D = x.shape
    tr = _tile(L, 384)
    has_res = res is not None

    def body(*refs):
        x_ref, w_ref, dy_ref = refs[:3]
        res_ref = refs[3] if has_res else None
        dx_ref, dw_ref = refs[-2:]
        dx, dw = _norm_bwd_math(x_ref[...], w_ref[...], dy_ref[...].astype(F32))
        dx = scale * dx
        if has_res:
            dx = dx + res_ref[...]
        dx_ref[...] = dx.astype(dx_ref.dtype)

        @pl.when(pl.program_id(0) == 0)
        def _():
            dw_ref[...] = jnp.zeros_like(dw_ref)

        dw_ref[...] += scale * dw

    row = pl.BlockSpec((tr, D), lambda i: (i, 0))
    vec = pl.BlockSpec((1, D), lambda i: (0, 0))
    return pl.pallas_call(
        body, name="norm_bwd", grid=(L // tr,),
        out_shape=[jax.ShapeDtypeStruct((L, D), out_dtype), jax.ShapeDtypeStruct((1, D), F32)],
        in_specs=[row, vec, row] + ([row] if has_res else []), out_specs=[row, vec],
        compiler_params=_params(("arbitrary",)),
    )(*([x, w, dy] + ([res] if has_res else [])))


def _loss(h, target):
    L, D = h.shape

    def body(h_ref, t_ref, dh_ref, loss_ref):
        i = pl.program_id(0)

        @pl.when(i == 0)
        def _():
            dh_ref[...] = jnp.zeros_like(dh_ref)
            loss_ref[...] = jnp.zeros_like(loss_ref)

        @pl.when(i > 0)
        def _():
            diff = h_ref[...] - t_ref[...]
            dh_ref[...] = diff * (1.0 / D)
            loss_ref[...] += 0.5 * jnp.sum(diff * diff) * (1.0 / D)

    return pl.pallas_call(
        body, name="loss", grid=(L // BLK,),
        out_shape=[jax.ShapeDtypeStruct((L, D), F32), jax.ShapeDtypeStruct((8, 128), F32)],
        in_specs=[pl.BlockSpec((BLK, D), lambda i: (i, 0)),
                  pl.BlockSpec((BLK, D), lambda i: (jnp.maximum(i - 1, 0), 0))],
        out_specs=[pl.BlockSpec((BLK, D), lambda i: (i, 0)), pl.BlockSpec((8, 128), lambda i: (0, 0))],
        compiler_params=_params(("arbitrary",)),
    )(h, target)


def _ffn_up(a, wg, wu, comm=None):
    L, D = a.shape
    F = wg.shape[1]
    tm = _tile(L, 704)

    def ep(g, u):
        return g, u, g * _sigmoid(g) * u

    hspec = pl.BlockSpec((None, tm, F), lambda i, j: (j, i, 0))
    wspec = pl.BlockSpec((None, F, D), lambda i, j: (j, 0, 0))
    res = _mm("ffn_up", (L // tm, N_DEV), ("parallel", "parallel"), None,
              [a, wg, wu], [pl.BlockSpec((tm, D), lambda i, j: (i, 0)), wspec, wspec],
              [(0, 1, 'nt', 0), (0, 2, 'nt', 1)], [(tm, F)] * 2, [], [], ep,
              [jax.ShapeDtypeStruct((N_DEV, L, F), BF16)] * 3, [hspec] * 3, comm=comm)
    return _with_comm(res, comm, lambda o: o)


def _ffn_gate(a, wg, comm=None):
    L, D = a.shape
    F = wg.shape[1]
    tm = _tile(L, 704)
    res = _mm("ffn_gate", (L // tm, N_DEV), ("parallel", "parallel"), None,
              [a, wg], [pl.BlockSpec((tm, D), lambda i, j: (i, 0)), pl.BlockSpec((None, F, D), lambda i, j: (j, 0, 0))],
              [(0, 1, 'nt', 0)], [(tm, F)], [], [], lambda g: (g,),
              [jax.ShapeDtypeStruct((N_DEV, L, F), BF16)], [pl.BlockSpec((None, tm, F), lambda i, j: (j, i, 0))],
              comm=comm)
    return _with_comm(res, comm, lambda o: o[0])


def _ffn_up_gated(a, wu, g, comm=None):
    L, D = a.shape
    F = wu.shape[1]
    tm = _tile(L, 704)

    def ep(u, g_):
        g32 = g_.astype(F32)
        return u, g32 * _sigmoid(g32) * u

    hspec = pl.BlockSpec((None, tm, F), lambda i, j: (j, i, 0))
    res = _mm("ffn_up_gated", (L // tm, N_DEV), ("parallel", "parallel"), None,
              [a, wu], [pl.BlockSpec((tm, D), lambda i, j: (i, 0)), pl.BlockSpec((None, F, D), lambda i, j: (j, 0, 0))],
              [(0, 1, 'nt', 0)], [(tm, F)], [g], [hspec], ep,
              [jax.ShapeDtypeStruct((N_DEV, L, F), BF16)] * 2, [hspec, hspec], comm=comm)
    return _with_comm(res, comm, lambda o: o)


def _resnorm_epilogue(scale, with_next):
    def ep(acc, h, w, *w_next):
        r = lax.rsqrt(jnp.mean(acc * acc, axis=-1, keepdims=True) + EPS)
        h_out = h + scale * (acc * r * w)
        if not with_next:
            return acc, h_out
        r_next = lax.rsqrt(jnp.mean(h_out * h_out, axis=-1, keepdims=True) + EPS)
        return acc, h_out, h_out * r_next * w_next[0]
    return ep


def _ffn_down(hid, wd, h_in, post, next_norm=None, comm=None):
    _, L, F = hid.shape
    D = wd.shape[2]
    tm = _tile(L, 528)
    row = pl.BlockSpec((tm, D), lambda i, j: (i, 0))
    vec = pl.BlockSpec((1, D), lambda i, j: (0, 0))
    nxt = [] if next_norm is None else [next_norm]
    res = _mm("ffn_down", (L // tm, N_DEV), ("parallel", "arbitrary"), 1,
              [hid, wd], [pl.BlockSpec((None, tm, F), lambda i, j: (j, i, 0)),
                          pl.BlockSpec((None, F, D), lambda i, j: (j, 0, 0))],
              [(0, 1, 'nn', 0)], [(tm, D)], [h_in, post] + nxt, [row, vec] + [vec] * len(nxt),
              _resnorm_epilogue(0.5, bool(nxt)),
              [jax.ShapeDtypeStruct((L, D), F32)] * 2 + [jax.ShapeDtypeStruct((L, D), BF16)] * len(nxt),
              [row] * (2 + len(nxt)), comm=comm)
    return _with_comm(res, comm, lambda o: o)


def _ffn_dhid(df, wd, g, u, comm=None):
    L, D = df.shape
    F = wd.shape[1]
    tm = _tile(L, 704)

    def ep(dhid, g_, u_):
        g32, u32 = g_.astype(F32), u_.astype(F32)
        sg = _sigmoid(g32)
        return dhid * u32 * sg * (1.0 + g32 * (1.0 - sg)), dhid * g32 * sg

    hspec = pl.BlockSpec((None, tm, F), lambda i, j: (j, i, 0))
    res = _mm("ffn_dhid", (L // tm, N_DEV), ("parallel", "parallel"), None,
              [df, wd], [pl.BlockSpec((tm, D), lambda i, j: (i, 0)),
                         pl.BlockSpec((None, F, D), lambda i, j: (j, 0, 0))],
              [(0, 1, 'nt', 0)], [(tm, F)], [g, u], [hspec, hspec], ep,
              [jax.ShapeDtypeStruct((N_DEV, L, F), BF16)] * 2, [hspec, hspec], comm=comm)
    return _with_comm(res, comm, lambda o: o)


def _ffn_dwd(hid, df, comm=None):
    _, L, F = hid.shape
    D = df.shape[1]
    tk = _tile(L, 1408)
    res = _mm("ffn_dwd", (N_DEV, L // tk), ("parallel", "arbitrary"), 1,
              [hid, df], [pl.BlockSpec((None, tk, F), lambda j, k: (j, k, 0)),
                          pl.BlockSpec((tk, D), lambda j, k: (k, 0))],
              [(0, 1, 'tn', 0)], [(F, D)], [], [], lambda acc: (acc,),
              [jax.ShapeDtypeStruct((N_DEV, F, D), BF16)], [pl.BlockSpec((None, F, D), lambda j, k: (j, 0, 0))],
              comm=comm)
    return _with_comm(res, comm, lambda o: o[0])


def _ffn_dwgu(a, dg, du, comm=None):
    L, D = a.shape
    F = dg.shape[2]
    tk = _tile(L, 704)
    hspec = pl.BlockSpec((None, tk, F), lambda j, k: (j, k, 0))
    wspec = pl.BlockSpec((None, F, D), lambda j, k: (j, 0, 0))
    res = _mm("ffn_dwgu", (N_DEV, L // tk), ("parallel", "arbitrary"), 1,
              [a, dg, du], [pl.BlockSpec((tk, D), lambda j, k: (k, 0)), hspec, hspec],
              [(1, 0, 'tn', 0), (2, 0, 'tn', 1)], [(F, D)] * 2, [], [], lambda *acc: acc,
              [jax.ShapeDtypeStruct((N_DEV, F, D), BF16)] * 2, [wspec, wspec], comm=comm)
    return _with_comm(res, comm, lambda o: o)


def _ffn_da(dg, du, wg, wu, comm=None):
    _, L, F = dg.shape
    D = wg.shape[2]
    tm = _tile(L, 704)
    hspec = pl.BlockSpec((None, tm, F), lambda i, j: (j, i, 0))
    wspec = pl.BlockSpec((None, F, D), lambda i, j: (j, 0, 0))
    row = pl.BlockSpec((tm, D), lambda i, j: (i, 0))
    res = _mm("ffn_da", (L // tm, N_DEV), ("parallel", "arbitrary"), 1,
              [dg, du, wg, wu], [hspec, hspec, wspec, wspec],
              [(0, 2, 'nn', 0), (1, 3, 'nn', 0)], [(tm, D)], [], [], lambda acc: (acc,),
              [jax.ShapeDtypeStruct((L, D), F32)], [row], comm=comm)
    return _with_comm(res, comm, lambda o: o[0])


def _rope_tables(L):
    rows = jnp.arange(L, dtype=F32)
    pos = jnp.where(rows < BLK, rows, rows - (BLK - N_META))
    inv_r = ROPE_THETA ** (-jnp.arange(0, HD, 2, dtype=F32) / HD)
    ang_r = pos[:, None] * inv_r[None, :]
    cr = jnp.concatenate([jnp.cos(ang_r), jnp.cos(ang_r)], axis=1)
    sr = jnp.concatenate([-jnp.sin(ang_r), jnp.sin(ang_r)], axis=1)
    inv_m = ROPE_THETA ** (-jnp.arange(0, ROPE, 2, dtype=F32) / ROPE)
    ang_m = pos[:, None] * inv_m[None, :]
    z32 = jnp.zeros((L, ROPE // 2), F32)
    z64 = jnp.zeros((L, HD - ROPE), F32)
    cm = jnp.concatenate([jnp.cos(ang_m), jnp.cos(ang_m), z64], axis=1)
    sa = jnp.concatenate([-jnp.sin(ang_m), z32, z64], axis=1)
    sb = jnp.concatenate([z32, jnp.sin(ang_m), z64], axis=1)
    return cr, sr, cm, sa, sb


def _rope_ret(x, cr, sr):
    return x * cr + pltpu.roll(x, HD // 2, 1) * sr


def _rope_ret_t(d, cr, sr):
    return d * cr + pltpu.roll(d * sr, HD // 2, 1)


def _rope_mla(x, cm, sa, sb):
    return x * cm + pltpu.roll(x, HD - ROPE // 2, 1) * sa + pltpu.roll(x, ROPE // 2, 1) * sb


def _rope_mla_t(d, cm, sa, sb):
    return d * cm + pltpu.roll(d * sa, ROPE // 2, 1) + pltpu.roll(d * sb, HD - ROPE // 2, 1)


C_RQ, C_RK, C_RV, C_RG = 0, HEADS * HD, 2 * HEADS * HD, 3 * HEADS * HD
C_CQ = 4 * HEADS * HD
C_CKV = C_CQ + Q_RANK
C_KR = C_CKV + KV_RANK
RET_K_SCALE = HD ** -0.5


def _prep(proj, tabs, qn, kvn):
    L = proj.shape[0]
    tr = _tile(L, 256)
    W = HEADS * HD

    def body(p_ref, cr_ref, sr_ref, cm_ref, sa_ref, sb_ref, qn_ref, kvn_ref, q_ref, k_ref, v_ref, cq_ref, ckv_ref,
             kr_ref):
        cr, sr = cr_ref[...], sr_ref[...]
        for h in range(HEADS):
            sl = slice(h * HD, (h + 1) * HD)
            q_ref[:, sl] = _rope_ret(p_ref[:, C_RQ + h * HD:C_RQ + (h + 1) * HD].astype(F32), cr, sr).astype(BF16)
            k_ref[:, sl] = (_rope_ret(p_ref[:, C_RK + h * HD:C_RK + (h + 1) * HD].astype(F32), cr, sr)
                            * RET_K_SCALE).astype(BF16)
        v_ref[...] = p_ref[:, C_RV:C_RV + W].astype(BF16)
        cq = p_ref[:, C_CQ:C_CQ + Q_RANK].astype(F32)
        cq_ref[...] = (cq * lax.rsqrt(jnp.mean(cq * cq, axis=-1, keepdims=True) + EPS) * qn_ref[...]).astype(BF16)
        ckv = p_ref[:, C_CKV:C_CKV + KV_RANK].astype(F32)
        ckv_ref[...] = (ckv * lax.rsqrt(jnp.mean(ckv * ckv, axis=-1, keepdims=True) + EPS)
                        * kvn_ref[...]).astype(BF16)
        kr_ref[...] = _rope_mla(p_ref[:, C_KR:C_KR + HD].astype(F32), cm_ref[...], sa_ref[...], sb_ref[...]).astype(BF16)

    row = lambda w: pl.BlockSpec((tr, w), lambda i: (i, 0))
    vec = lambda w: pl.BlockSpec((1, w), lambda i: (0, 0))
    return pl.pallas_call(
        body, name="mix_prep", grid=(L // tr,),
        out_shape=[jax.ShapeDtypeStruct((L, W), BF16)] * 3 + [jax.ShapeDtypeStruct((L, Q_RANK), BF16),
                                                              jax.ShapeDtypeStruct((L, KV_RANK), BF16),
                                                              jax.ShapeDtypeStruct((L, HD), BF16)],
        in_specs=[row(D_INP)] + [row(HD)] * 5 + [vec(Q_RANK), vec(KV_RANK)],
        out_specs=[row(W)] * 3 + [row(Q_RANK), row(KV_RANK), row(HD)],
        compiler_params=_params(("parallel",)),
    )(proj, *tabs, qn, kvn)


def _prep_bwd(proj, dq, dk, dv, drg, dcqn, dckvn, dkr8, tabs, qn, kvn):
    L = proj.shape[0]
    tr = _tile(L, 192)
    W = HEADS * HD

    def body(p_ref, dq_ref, dk_ref, dv_ref, drg_ref, dcq_ref, dckv_ref, dkr_ref, cr_ref, sr_ref, cm_ref, sa_ref,
             sb_ref, qn_ref, kvn_ref, dp_ref, dqn_ref, dkvn_ref):
        cr, sr = cr_ref[...], sr_ref[...]
        dkr = None
        for h in range(HEADS):
            sl = slice(h * HD, (h + 1) * HD)
            dp_ref[:, C_RQ + h * HD:C_RQ + (h + 1) * HD] = _rope_ret_t(dq_ref[:, sl].astype(F32), cr, sr).astype(BF16)
            dp_ref[:, C_RK + h * HD:C_RK + (h + 1) * HD] = (_rope_ret_t(dk_ref[:, sl].astype(F32), cr, sr)
                                                            * RET_K_SCALE).astype(BF16)
            part = dkr_ref[:, sl].astype(F32)
            dkr = part if dkr is None else dkr + part
        dp_ref[:, C_RV:C_RV + W] = dv_ref[...].astype(BF16)
        dp_ref[:, C_RG:C_RG + W] = drg_ref[...].astype(BF16)
        dcq, dqn = _norm_bwd_math(p_ref[:, C_CQ:C_CQ + Q_RANK].astype(F32), qn_ref[...], dcq_ref[...])
        dp_ref[:, C_CQ:C_CQ + Q_RANK] = dcq.astype(BF16)
        dckv, dkvn = _norm_bwd_math(p_ref[:, C_CKV:C_CKV + KV_RANK].astype(F32), kvn_ref[...], dckv_ref[...])
        dp_ref[:, C_CKV:C_CKV + KV_RANK] = dckv.astype(BF16)
        dp_ref[:, C_KR:C_KR + HD] = _rope_mla_t(dkr, cm_ref[...], sa_ref[...], sb_ref[...]).astype(BF16)

        @pl.when(pl.program_id(0) == 0)
        def _():
            dqn_ref[...] = jnp.zeros_like(dqn_ref)
            dkvn_ref[...] = jnp.zeros_like(dkvn_ref)

        dqn_ref[...] += dqn
        dkvn_ref[...] += dkvn

    row = lambda w: pl.BlockSpec((tr, w), lambda i: (i, 0))
    vec = lambda w: pl.BlockSpec((1, w), lambda i: (0, 0))
    return pl.pallas_call(
        body, name="mix_prep_bwd", grid=(L // tr,),
        out_shape=[jax.ShapeDtypeStruct((L, D_INP), BF16), jax.ShapeDtypeStruct((1, Q_RANK), F32),
                   jax.ShapeDtypeStruct((1, KV_RANK), F32)],
        in_specs=[row(D_INP)] + [row(W)] * 4 + [row(Q_RANK), row(KV_RANK), row(W)] + [row(HD)] * 5
                 + [vec(Q_RANK), vec(KV_RANK)],
        out_specs=[row(D_INP), vec(Q_RANK), vec(KV_RANK)],
        compiler_params=_params(("arbitrary",)),
    )(proj, dq, dk, dv, drg, dcqn, dckvn, dkr8, *tabs, qn, kvn)


def _post(o_ret, proj, gn):
    L, W = o_ret.shape
    tr = _tile(L, 384)

    def body(o_ref, rg_ref, gn_ref, out_ref):
        for h in range(HEADS):
            sl = slice(h * HD, (h + 1) * HD)
            o = o_ref[:, sl]
            rg = rg_ref[:, sl].astype(F32)
            n = o * lax.rsqrt(jnp.mean(o * o, axis=-1, keepdims=True) + EPS)
            out_ref[:, sl] = (n * gn_ref[:, sl] * (rg * _sigmoid(rg))).astype(BF16)

    row = pl.BlockSpec((tr, W), lambda i: (i, 0))
    return pl.pallas_call(
        body, name="ret_post", grid=(L // tr,), out_shape=jax.ShapeDtypeStruct((L, W), BF16),
        in_specs=[row, pl.BlockSpec((tr, W), lambda i: (i, C_RG // W)), pl.BlockSpec((1, W), lambda i: (0, 0))],
        out_specs=row, compiler_params=_params(("parallel",)),
    )(o_ret, proj, gn)


def _post_bwd(o_ret, proj, gn, dcat):
    L, W = o_ret.shape
    tr = _tile(L, 384)

    def body(o_ref, rg_ref, gn_ref, d_ref, do_ref, drg_ref, dgn_ref):
        @pl.when(pl.program_id(0) == 0)
        def _():
            dgn_ref[...] = jnp.zeros_like(dgn_ref)

        for h in range(HEADS):
            sl = slice(h * HD, (h + 1) * HD)
            o = o_ref[:, sl]
            rg = rg_ref[:, sl].astype(F32)
            d = d_ref[:, sl].astype(F32)
            gw = gn_ref[:, sl]
            r = lax.rsqrt(jnp.mean(o * o, axis=-1, keepdims=True) + EPS)
            n = o * r
            sg = _sigmoid(rg)
            si = rg * sg
            dn = d * gw * si
            dgn_ref[:, sl] += jnp.sum(d * n * si, axis=0, keepdims=True)
            drg_ref[:, sl] = (d * n * gw * sg * (1.0 + rg * (1.0 - sg))).astype(drg_ref.dtype)
            do_ref[:, sl] = (r * (dn - o * (r * r) * jnp.mean(dn * o, axis=-1, keepdims=True))).astype(BF16)

    row = pl.BlockSpec((tr, W), lambda i: (i, 0))
    vec = pl.BlockSpec((1, W), lambda i: (0, 0))
    return pl.pallas_call(
        body, name="ret_post_bwd", grid=(L // tr,),
        out_shape=[jax.ShapeDtypeStruct((L, W), BF16), jax.ShapeDtypeStruct((L, W), BF16),
                   jax.ShapeDtypeStruct((1, W), F32)],
        in_specs=[row, pl.BlockSpec((tr, W), lambda i: (i, C_RG // W)), vec, row],
        out_specs=[row, row, vec], compiler_params=_params(("arbitrary",)),
    )(o_ret, proj, gn, dcat)


RET_HEADS_PER_STEP = 4


def _lin_attn(name, q, k, v, lg, reverse, out_dtype=F32):
    L, W = q.shape
    nc = L // BLK - 1
    G = RET_HEADS_PER_STEP

    def body(q_ref, k_ref, v_ref, lg_ref, o_ref, s_ref):
        n = lax.broadcasted_iota(jnp.int32, (BLK, BLK), 0).astype(F32)
        m = lax.broadcasted_iota(jnp.int32, (BLK, BLK), 1).astype(F32)
        dist = (m - n) if reverse else (n - m)
        consts = []
        for g in range(G):
            lgv = lg_ref[g, 0:1, :]
            dmask = jnp.where(dist >= 0, jnp.exp(lgv * jnp.maximum(dist, 0.0)), 0.0)
            c = dict(dmask=dmask, dmask0=jnp.where((n < N_META) & (m < N_META), dmask, 0.0),
                     gl=jnp.exp(lgv * float(BLK)))
            if reverse:
                c.update(inter=jnp.exp(lgv * (float(BLK) - n)), upd=jnp.exp(lgv * n),
                         inter0=jnp.where(n < N_META, jnp.exp(lgv * jnp.maximum(float(N_META) - n, 0.0)), 0.0))
            else:
                c.update(inter=jnp.exp(lgv * (n + 1.0)), upd=jnp.exp(lgv * (float(BLK) - 1.0 - n)),
                         upd0=jnp.where(n < N_META, jnp.exp(lgv * jnp.maximum(float(N_META) - 1.0 - n, 0.0)), 0.0))
            consts.append(c)

        def chunk(c):
            rows = pl.ds(pl.multiple_of(c * BLK, BLK), BLK)
            state = [s_ref[g] for g in range(G)]
            outs, new_state = [], []
            for g in range(G):
                cols = slice(g * HD, (g + 1) * HD)
                cg = consts[g]
                qc, kc, vc = q_ref[rows, cols], k_ref[rows, cols], v_ref[rows, cols]
                a = _dot(qc, kc, 'nt') * cg['dmask']
                outs.append(_dot(a.astype(BF16), vc, 'nn') + _dot(qc, state[g].astype(BF16), 'nn') * cg['inter'])
                new_state.append(state[g] * cg['gl'] + _dot((kc.astype(F32) * cg['upd']).astype(BF16), vc, 'tn'))
            for g in range(G):
                o_ref[rows, g * HD:(g + 1) * HD] = outs[g].astype(o_ref.dtype)
                s_ref[g] = new_state[g]

        def first_chunk(with_state):
            for g in range(G):
                cols = slice(g * HD, (g + 1) * HD)
                cg = consts[g]
                q0, k0, v0 = q_ref[0:BLK, cols], k_ref[0:BLK, cols], v_ref[0:BLK, cols]
                o0 = _dot((_dot(q0, k0, 'nt') * cg['dmask0']).astype(BF16), v0, 'nn')
                if with_state:
                    o0 = o0 + _dot(q0, s_ref[g].astype(BF16), 'nn') * cg['inter0']
                else:
                    s_ref[g] = _dot((k0.astype(F32) * cg['upd0']).astype(BF16), v0, 'tn')
                o_ref[0:BLK, cols] = o0.astype(o_ref.dtype)

        if reverse:
            s_ref[...] = jnp.zeros_like(s_ref)

            def step(t, carry):
                chunk(nc - t)
                return carry

            lax.fori_loop(0, nc, step, 0)
            first_chunk(True)
        else:
            first_chunk(False)

            def step(t, carry):
                chunk(t + 1)
                return carry

            lax.fori_loop(0, nc, step, 0)

    col = pl.BlockSpec((L, G * HD), lambda h: (0, h))
    return pl.pallas_call(
        body, name=name, grid=(HEADS // G,), out_shape=jax.ShapeDtypeStruct((L, W), out_dtype),
        in_specs=[col, col, col, pl.BlockSpec((G, 8, HD), lambda h: (h, 0, 0))], out_specs=col,
        scratch_shapes=[pltpu.VMEM((G, HD, HD), F32)], compiler_params=_params(("parallel",)),
    )(q, k, v, lg)


ATT_SCALE = (HD + ROPE) ** -0.5
LOG2E = 1.4426950408889634
Q_PRESCALE = ATT_SCALE * LOG2E
NEG = -1e30


ATT_TILE = 384
ATT_HEADS_PER_STEP = 2


def _att_valid(nq, nk, row0, col0):
    r = lax.broadcasted_iota(jnp.int32, (nq, nk), 0) + row0
    c = lax.broadcasted_iota(jnp.int32, (nq, nk), 1) + col0
    return (c <= r) & ((c < N_META) | (c >= BLK))


def _attn_fwd(qm, kn, krr, vm, comm=None):
    L = qm.shape[0]
    W = HEADS * HD
    T = _tile(L, ATT_TILE, BLK)
    nb = L // T
    G = ATT_HEADS_PER_STEP
    n_cm = comm.n if comm is not None else 0

    def body(*refs):
        q_ref, kn_ref, kr_ref, v_ref = refs[:4]
        o_ref, lse_ref = refs[4 + n_cm:6 + n_cm]
        m_sc, l_sc, acc_sc = refs[6 + 2 * n_cm:9 + 2 * n_cm]
        if comm is not None:
            cm_refs = (refs[4:4 + n_cm], refs[6 + n_cm:6 + 2 * n_cm], refs[9 + 2 * n_cm:])
            first, last = _grid_edges((HEADS // G, nb))

            @pl.when(first)
            def _():
                comm.start(*cm_refs)

        i = pl.program_id(1)
        m_sc[...] = jnp.full_like(m_sc, NEG)
        l_sc[...] = jnp.zeros_like(l_sc)
        acc_sc[...] = jnp.zeros_like(acc_sc)

        def tile(j, masked):
            rows = pl.ds(pl.multiple_of(j * T, T), T)
            kr = kr_ref[rows, :]
            valid = _att_valid(T, T, i * T, j * T) if masked else None
            m_prev = [m_sc[g] for g in range(G)]
            l_prev = [l_sc[g] for g in range(G)]
            acc_prev = [acc_sc[g] for g in range(G)]
            m_new, l_new, acc_new = [], [], []
            for g in range(G):
                k = jnp.concatenate([kn_ref[rows, g * HD:(g + 1) * HD], kr], axis=1)
                s = _dot(q_ref[:, g * QH:(g + 1) * QH], k, 'nt')
                if masked:
                    s = jnp.where(valid, s, NEG)
                m_new.append(jnp.maximum(m_prev[g], jnp.max(s, axis=-1, keepdims=True)))
                p = jnp.exp2(s - m_new[g])
                alpha = jnp.exp2(m_prev[g] - m_new[g])
                l_new.append(alpha * l_prev[g] + jnp.sum(p, axis=-1, keepdims=True))
                acc_new.append(alpha * acc_prev[g] + _dot(p.astype(BF16), v_ref[rows, g * HD:(g + 1) * HD], 'nn'))
            for g in range(G):
                m_sc[g] = m_new[g]
                l_sc[g] = l_new[g]
                acc_sc[g] = acc_new[g]

        tile(0, True)

        def mid(j, carry):
            tile(j, False)
            return carry

        lax.fori_loop(1, i, mid, 0)

        @pl.when(i > 0)
        def _():
            tile(i, True)

        for g in range(G):
            l = l_sc[g]
            o_ref[:, g * HD:(g + 1) * HD] = (acc_sc[g] / l).astype(o_ref.dtype)
            lse_ref[g] = jnp.broadcast_to(m_sc[g] + jnp.log(l) * LOG2E, (T, HD))

        if comm is not None:
            @pl.when(last)
            def _():
                comm.finish(*cm_refs)

    cm_specs = comm.specs if comm is not None else []
    res = pl.pallas_call(
        body, name="attn_fwd", grid=(HEADS // G, nb),
        out_shape=[jax.ShapeDtypeStruct((L, W), BF16), jax.ShapeDtypeStruct((HEADS, L, HD), F32)]
        + (comm.out_shapes if comm is not None else []),
        in_specs=[pl.BlockSpec((T, G * QH), lambda h, i: (i, h)), pl.BlockSpec((L, G * HD), lambda h, i: (0, h)),
                  pl.BlockSpec((L, HD), lambda h, i: (0, 0)), pl.BlockSpec((L, G * HD), lambda h, i: (0, h))]
        + cm_specs,
        out_specs=[pl.BlockSpec((T, G * HD), lambda h, i: (i, h)),
                   pl.BlockSpec((G, T, HD), lambda h, i: (h, i, 0))] + cm_specs,
        scratch_shapes=[pltpu.VMEM((G, T, 1), F32), pltpu.VMEM((G, T, 1), F32), pltpu.VMEM((G, T, HD), F32)]
        + (comm.scratch if comm is not None else []),
        compiler_params=_params(("arbitrary", "arbitrary")),
    )(qm, kn, krr, vm, *(comm.arrays if comm is not None else []))
    return res[:2], res[2:]


def _attn_bwd(qm, kn, krr, vm, o, dcat, lse, comm=None):
    L = qm.shape[0]
    W = HEADS * HD
    T = _tile(L, ATT_TILE, BLK)
    nb = L // T
    n_cm = comm.n if comm is not None else 0

    def body(*refs):
        q_ref, kn_ref, kr_ref, v_ref, o_ref, do_ref, lse_ref = refs[:7]
        dq_ref, dkn_ref, dkr_ref, dv_ref = refs[7 + n_cm:11 + n_cm]
        dl_sc, dk_sc, dv_sc = refs[11 + 2 * n_cm:14 + 2 * n_cm]
        if comm is not None:
            cm_refs = (refs[7:7 + n_cm], refs[11 + n_cm:11 + 2 * n_cm], refs[14 + 2 * n_cm:])
            first, last = _grid_edges((HEADS, nb))

            @pl.when(first)
            def _():
                comm.start(*cm_refs)

        j = pl.program_id(1)

        @pl.when(j == 0)
        def _():
            dq_ref[...] = jnp.zeros_like(dq_ref)

            def rowsum(t, carry):
                rows = pl.ds(pl.multiple_of(t * T, T), T)
                dl_sc[rows, :] = jnp.sum(do_ref[rows, :].astype(F32) * o_ref[rows, :].astype(F32), axis=-1,
                                         keepdims=True)
                return carry

            lax.fori_loop(0, nb, rowsum, 0)

        k = jnp.concatenate([kn_ref[...], kr_ref[...]], axis=1)
        v = v_ref[...]
        dk_sc[...] = jnp.zeros_like(dk_sc)
        dv_sc[...] = jnp.zeros_like(dv_sc)

        def tile(i, masked):
            rows = pl.ds(pl.multiple_of(i * T, T), T)
            q = q_ref[rows, :]
            do = do_ref[rows, :]
            s = _dot(q, k, 'nt')
            if masked:
                s = jnp.where(_att_valid(T, T, i * T, j * T), s, NEG)
            p = jnp.exp2(s - lse_ref[rows, 0:1])
            dv_sc[...] += _dot(p.astype(BF16), do, 'tn')
            ds = (p * (_dot(do, v, 'nt') - dl_sc[rows, :])).astype(BF16)
            dk_sc[...] += _dot(ds, q, 'tn')
            dq_ref[rows, :] += _dot(ds, k, 'nn')

        tile(j, True)

        def rest(masked):
            def step(i, carry):
                tile(i, masked)
                return carry
            lax.fori_loop(j + 1, nb, step, 0)

        @pl.when(j == 0)
        def _():
            rest(True)

        @pl.when(j > 0)
        def _():
            rest(False)

        dk = dk_sc[...] * (1.0 / LOG2E)
        dkn_ref[...] = dk[:, 0:HD].astype(BF16)
        dkr_ref[...] = dk[:, HD:QH].astype(dkr_ref.dtype)
        dv_ref[...] = dv_sc[...].astype(BF16)

        if comm is not None:
            @pl.when(last)
            def _():
                comm.finish(*cm_refs)

    blk = pl.BlockSpec((T, HD), lambda h, j: (j, h))
    cm_specs = comm.specs if comm is not None else []
    res = pl.pallas_call(
        body, name="attn_bwd", grid=(HEADS, nb),
        out_shape=[jax.ShapeDtypeStruct((L, HEADS * QH), F32), jax.ShapeDtypeStruct((L, W), BF16),
                   jax.ShapeDtypeStruct((L, W), BF16), jax.ShapeDtypeStruct((L, W), BF16)]
        + (comm.out_shapes if comm is not None else []),
        in_specs=[pl.BlockSpec((L, QH), lambda h, j: (0, h)), blk, pl.BlockSpec((T, HD), lambda h, j: (j, 0)), blk,
                  pl.BlockSpec((L, HD), lambda h, j: (0, h)), pl.BlockSpec((L, HD), lambda h, j: (0, HEADS + h)),
                  pl.BlockSpec((None, L, HD), lambda h, j: (h, 0, 0))] + cm_specs,
        out_specs=[pl.BlockSpec((L, QH), lambda h, j: (0, h)), blk, blk, blk] + cm_specs,
        scratch_shapes=[pltpu.VMEM((L, 1), F32), pltpu.VMEM((T, QH), F32), pltpu.VMEM((T, HD), F32)]
        + (comm.scratch if comm is not None else []),
        compiler_params=_params(("arbitrary", "arbitrary")),
    )(qm, kn, krr, vm, o, dcat, lse, *(comm.arrays if comm is not None else []))
    return res[:4], res[4:]


def _unrope_q(dqm, tabs_m):
    L, W = dqm.shape
    tr = _tile(L, 384)

    def body(d_ref, cm_ref, sa_ref, sb_ref, out_ref):
        cm, sa, sb = cm_ref[...], sa_ref[...], sb_ref[...]
        for h in range(HEADS):
            out_ref[:, h * QH:h * QH + HD] = (d_ref[:, h * QH:h * QH + HD] * ATT_SCALE).astype(BF16)
            out_ref[:, h * QH + HD:(h + 1) * QH] = _rope_mla_t(d_ref[:, h * QH + HD:(h + 1) * QH] * ATT_SCALE, cm, sa,
                                                               sb).astype(BF16)

    row = pl.BlockSpec((tr, W), lambda i: (i, 0))
    tab = pl.BlockSpec((tr, HD), lambda i: (i, 0))
    return pl.pallas_call(
        body, name="unrope_q", grid=(L // tr,), out_shape=jax.ShapeDtypeStruct((L, W), BF16),
        in_specs=[row, tab, tab, tab], out_specs=row, compiler_params=_params(("parallel",)),
    )(dqm, *tabs_m)


def _q_up(cqn, wuq_p, tabs_m):
    L = cqn.shape[0]
    tm = _tile(L, 704)

    def ep(acc, cm, sa, sb):
        acc = acc * Q_PRESCALE
        parts = []
        for h in range(HEADS):
            parts.append(acc[:, h * QH:h * QH + HD])
            parts.append(_rope_mla(acc[:, h * QH + HD:(h + 1) * QH], cm, sa, sb))
        return (jnp.concatenate(parts, axis=1),)

    tab = pl.BlockSpec((tm, HD), lambda i, j: (i, 0))
    return _mm("mla_q_up", (L // tm, 1), ("parallel", "parallel"), None,
               [cqn, wuq_p], [pl.BlockSpec((tm, Q_RANK), lambda i, j: (i, 0)),
                              pl.BlockSpec((HEADS * QH, Q_RANK), lambda i, j: (0, 0))],
               [(0, 1, 'nt', 0)], [(tm, HEADS * QH)], list(tabs_m), [tab] * 3, ep,
               [jax.ShapeDtypeStruct((L, HEADS * QH), BF16)], [pl.BlockSpec((tm, HEADS * QH), lambda i, j: (i, 0))])[0]


def _mix_out(cat, w_out, h_in, post, next_norm):
    L, K = cat.shape
    D = w_out.shape[1]
    tm, tk = _tile(L, 384), _tile(K, 512, 128)
    row = pl.BlockSpec((tm, D), lambda i, k: (i, 0))
    vec = pl.BlockSpec((1, D), lambda i, k: (0, 0))
    return _mm("mix_out", (L // tm, K // tk), ("parallel", "arbitrary"), 1,
               [cat, w_out], [pl.BlockSpec((tm, tk), lambda i, k: (i, k)), pl.BlockSpec((tk, D), lambda i, k: (k, 0))],
               [(0, 1, 'nn', 0)], [(tm, D)], [h_in, post, next_norm], [row, vec, vec], _resnorm_epilogue(1.0, True),
               [jax.ShapeDtypeStruct((L, D), F32)] * 2 + [jax.ShapeDtypeStruct((L, D), BF16)], [row, row, row])


ADAM_BLOCK_ELEMS = 512 * 704


def _adam_math(w, g, m, v):
    m = ADAM_B1 * m + (1.0 - ADAM_B1) * g
    v = ADAM_B2 * v + (1.0 - ADAM_B2) * (g * g)
    m_hat = m / (1.0 - ADAM_B1 ** ADAM_STEP)
    v_hat = v / (1.0 - ADAM_B2 ** ADAM_STEP)
    delta = -ADAM_LR * (m_hat / (jnp.sqrt(v_hat) + ADAM_EPS) + ADAM_WD * w)
    return delta, m, v


def _adam(name, w, m, v, g_slots=None, g=None, after=None):
    R, C = w.shape
    tr, tc = _tile(R, max(16, ADAM_BLOCK_ELEMS // C // 16 * 16), 16), C
    if tr * tc > ADAM_BLOCK_ELEMS:
        tr, tc = R, _tile(C, max(128, ADAM_BLOCK_ELEMS // R // 128 * 128), 128)
    from_slots = g_slots is not None

    def body(w_ref, m_ref, v_ref, g_ref, *rest):
        go_ref, d_ref, mo_ref, vo_ref = rest[-4:]
        if from_slots:
            grad = g_ref[0].astype(F32)
            for s in range(1, N_DEV):
                grad = grad + g_ref[s].astype(F32)
        else:
            grad = g_ref[...]
        delta, mn, vn = _adam_math(w_ref[...], grad, m_ref[...], v_ref[...])
        go_ref[...] = grad
        d_ref[...] = delta
        mo_ref[...] = mn
        vo_ref[...] = vn

    row = pl.BlockSpec((tr, tc), lambda i, j: (i, j))
    gspec = pl.BlockSpec((N_DEV, tr, tc), lambda i, j: (0, i, j)) if from_slots else row
    order = [] if after is None else [after]
    return pl.pallas_call(
        body, name=name, grid=(R // tr, C // tc), out_shape=[jax.ShapeDtypeStruct((R, C), F32)] * 4,
        in_specs=[row, row, row, gspec] + [pl.BlockSpec(memory_space=pl.ANY)] * len(order), out_specs=[row] * 4,
        compiler_params=_params(("parallel", "parallel")),
    )(w, m, v, g_slots if from_slots else g, *order)


def _unblock(gathered):
    n, r, c = gathered.shape
    return jnp.transpose(gathered, (1, 0, 2)).reshape(r, n * c)


def _reblock(full, c):
    r = full.shape[0]
    return jnp.transpose(full[:, :N_DEV * c].reshape(r, N_DEV, c), (1, 0, 2))


def _step(x, target, w, mom, vel):
    S, D = x.shape[1], x.shape[2]
    L = S + BLK
    def sq(a, n):
        if a.ndim == 2:
            return a
        if n in TRANSPOSED:
            a = jnp.swapaxes(a, 1, 2)
        return a.reshape(a.shape[1:])

    def unsq(o, n):
        o = o.reshape((1,) + o.shape)
        return jnp.swapaxes(o, 1, 2) if n in TRANSPOSED else o

    p = {n: sq(w[n], n) for n in WEIGHTS if n != 'meta_tokens'}
    gather = lambda names: _Exchange([p[n].astype(BF16) for n in names], False)
    scatter = lambda blocks: _Exchange(blocks, True)
    in_s, uq_s = p['w_in'].shape[0], p['mla_w_uq'].shape[0]
    assert uq_s == HD + ROPE and N_DEV == HEADS, "a w_uq shard is one head's columns"
    tabs = _rope_tables(L)
    tabs_m = tabs[2:]
    lg = jnp.broadcast_to(jnp.log(1.0 - 2.0 ** (-5.0 - jnp.arange(HEADS, dtype=F32)))[:, None, None], (HEADS, 8, HD))
    R = {}

    wg1, meta = _exchange("gather_first", [p['ffn1_w_gate'].astype(BF16), w['meta_tokens']], False)
    h0 = jnp.concatenate([_unblock(meta), jnp.zeros((BLK - N_META, D), F32), x[0]], axis=0)
    a1 = _norm_fwd(h0, p['ffn1_pre_norm'])
    g1, (wu1,) = _ffn_gate(a1, wg1, comm=gather(['ffn1_w_up']))
    (u1, hid1), (wd1,) = _ffn_up_gated(a1, wu1, g1, comm=gather(['ffn1_w_down']))
    (f1, h1, um), (w_in_g,) = _ffn_down(hid1, wd1, h0, p['ffn1_post_norm'], next_norm=p['mix_pre_norm'],
                                        comm=gather(['w_in']))

    w_in = jnp.pad(w_in_g.reshape(N_DEV * in_s, D), ((0, D_INP - N_DEV * in_s), (0, 0)))
    proj, (uq_g, uk_g, uv_g, wout_g) = _mm_nt("mix_in", [(um, w_in)], BF16, tn_target=1664,
                                              comm=gather(['mla_w_uq', 'mla_w_uk', 'mla_w_uv', 'w_out']))
    wuq = jnp.pad(uq_g, ((0, 0), (0, QH - uq_s), (0, 0))).reshape(HEADS * QH, Q_RANK)
    wuk, wuv, w_out = _unblock(uk_g), _unblock(uv_g), wout_g.reshape(-1, D)
    qr, kr, vr, cqn, ckvn, krr = _prep(proj, tabs, p['mla_q_norm'], p['mla_kv_norm'])
    qm = _q_up(cqn, wuq, tabs_m)
    kn = _mm_nn("mla_k_up", ckvn, wuk, BF16)
    vm = _mm_nn("mla_v_up", ckvn, wuv, BF16)
    (o_mla, lse), (wg2, wu2) = _attn_fwd(qm, kn, krr, vm, comm=gather(['ffn2_w_gate', 'ffn2_w_up']))
    o_ret = _lin_attn("ret_fwd", qr, kr, vr, lg, False)
    ret = _post(o_ret, proj, p['ret_group_norm'])
    cat = jnp.concatenate([ret, o_mla], axis=1)
    m, h2, a2 = _mix_out(cat, w_out, h1, p['mix_post_norm'], p['ffn2_pre_norm'])

    (g2, u2, hid2), (wd2,) = _ffn_up(a2, wg2, wu2, comm=gather(['ffn2_w_down']))
    f2, h3 = _ffn_down(hid2, wd2, h2, p['ffn2_post_norm'])
    dh3, loss_blk = _loss(h3, target[0])

    dsmall = {}
    df2, dsmall['ffn2_post_norm'] = _norm_bwd(f2, p['ffn2_post_norm'], dh3, None, 0.5, BF16)
    dg2, du2 = _ffn_dhid(df2, wd2, g2, u2)
    dwd2 = _ffn_dwd(hid2, df2)
    (dwg2, dwu2), (R['ffn2_w_down'],) = _ffn_dwgu(a2, dg2, du2, comm=scatter([dwd2]))
    da2, (R['ffn2_w_gate'],) = _ffn_da(dg2, du2, wg2, wu2, comm=scatter([dwg2]))
    dh2, dsmall['ffn2_pre_norm'] = _norm_bwd(h2, p['ffn2_pre_norm'], da2, dh3, 1.0, F32)

    dm, dsmall['mix_post_norm'] = _norm_bwd(m, p['mix_post_norm'], dh2, None, 1.0, BF16)
    dcat = _mm_nt("mix_dcat", [(dm, w_out)], BF16)
    dwout = _mm_tn("mix_dwout", cat, [dm])[0]
    do_ret, drg, dsmall['ret_group_norm'] = _post_bwd(o_ret, proj, p['ret_group_norm'], dcat)
    dqr = _lin_attn("ret_dq", do_ret, vr, kr, lg, False, BF16)
    dkr = _lin_attn("ret_dk", vr, do_ret, qr, lg, True, BF16)
    dvr = _lin_attn("ret_dv", kr, qr, do_ret, lg, True, BF16)
    (dqm, dkn, dkr8, dvm), (R['ffn2_w_up'], R['w_out']) = _attn_bwd(
        qm, kn, krr, vm, o_mla, dcat, lse, comm=scatter([dwu2, dwout.reshape(N_DEV, -1, D)]))
    dqp = _unrope_q(dqm, tabs_m)
    dwuq = _mm_tn("mla_dwuq", dqp, [cqn])[0]
    dcqn = _mm_nn("mla_dcq", dqp, wuq, F32)
    dwuk, dwuv = _mm_tn("mla_dwukv", ckvn, [dkn, dvm])
    dckvn = _mm_nt("mla_dckv", [(dkn, wuk), (dvm, wuv)], F32)
    dproj, dsmall['mla_q_norm'], dsmall['mla_kv_norm'] = _prep_bwd(
        proj, dqr, dkr, dvr, drg, dcqn, dckvn, dkr8, tabs, p['mla_q_norm'], p['mla_kv_norm'])
    dwuq_b = dwuq.reshape(HEADS, QH, Q_RANK)[:, :uq_s]
    (dwin,), (R['mla_w_uq'], R['mla_w_uk'], R['mla_w_uv']) = _mm_tn(
        "mix_dwin", dproj, [um], comm=scatter([dwuq_b, _reblock(dwuk, p['mla_w_uk'].shape[1]),
                                               _reblock(dwuv, p['mla_w_uv'].shape[1])]))
    dwin_b = dwin[:N_DEV * in_s].reshape(N_DEV, in_s, D)
    half = D // 2
    dum, (r_win_a,) = _mm_nn("mix_du", dproj, w_in, F32, tn_target=512, comm=scatter([dwin_b[:, :, :half]]))
    dh1, dsmall['mix_pre_norm'] = _norm_bwd(h1, p['mix_pre_norm'], dum, dh2, 1.0, F32)

    df1, dsmall['ffn1_post_norm'] = _norm_bwd(f1, p['ffn1_post_norm'], dh1, None, 0.5, BF16)
    (dg1, du1), (r_win_b,) = _ffn_dhid(df1, wd1, g1, u1, comm=scatter([dwin_b[:, :, half:]]))
    R['w_in'] = jnp.concatenate([r_win_a, r_win_b], axis=2)
    dwd1 = _ffn_dwd(hid1, df1)
    (dwg1, dwu1), (R['ffn1_w_down'],) = _ffn_dwgu(a1, dg1, du1, comm=scatter([dwd1]))
    da1, (R['ffn1_w_gate'],) = _ffn_da(dg1, du1, wg1, wu1, comm=scatter([dwg1]))
    dh0, dsmall['ffn1_pre_norm'] = _norm_bwd(h0, p['ffn1_pre_norm'], da1, dh1, 1.0, F32)
    tail_sems_s, tail_sems_r, tail_src, tail_land, token = _scatter_start(dwu1)

    def slab(a):
        a = a.reshape(-1, 128)
        return jnp.pad(a, ((0, (-a.shape[0]) % 8), (0, 0)))

    slab_rows = lambda n: -(-(p[n].shape[-1] // 128) // 8) * 8
    packed = jnp.concatenate([slab(dsmall[n]) for n in SMALL] + [slab(dh0[:N_META]), loss_blk], axis=0)
    red = _allreduce_small(packed + token[0, 0])
    offs = sum(slab_rows(n) for n in SMALL)
    n_small = offs
    gmeta_full = red[offs:offs + N_META * D // 128].reshape(N_META, D)
    offs += N_META * D // 128
    loss = red[offs, 0]

    grad, delta, new_m, new_v = {}, {}, {}, {}
    meanwhile = []
    for n in BIG:
        if n == 'ffn1_w_up':
            continue
        outs = _adam("adam_" + n, p[n], sq(mom[n], n), sq(vel[n], n), g_slots=R[n], after=token)
        meanwhile.append(outs[0])
        grad[n], delta[n], new_m[n], new_v[n] = [unsq(o, n) for o in outs]
    pack = lambda d: jnp.concatenate([slab(d[n]) for n in SMALL], axis=0)
    outs = _adam("adam_small", pack(w), pack(mom), pack(vel), g=red[:n_small])
    meanwhile.append(outs[0])
    offs = 0
    for n in SMALL:
        r = p[n].shape[-1] // 128
        grad[n], delta[n], new_m[n], new_v[n] = [o[offs:offs + r].reshape(w[n].shape) for o in outs]
        offs += slab_rows(n)
    dev = 4 * lax.axis_index("x") + 2 * lax.axis_index("y") + lax.axis_index("c")
    mcols = w['meta_tokens'].shape[1]
    gmeta = lax.dynamic_slice(gmeta_full, (0, dev * mcols), (N_META, mcols))
    outs = _adam("adam_meta", w['meta_tokens'], mom['meta_tokens'], vel['meta_tokens'], g=gmeta)
    grad['meta_tokens'], delta['meta_tokens'], new_m['meta_tokens'], new_v['meta_tokens'] = outs
    meanwhile.append(outs[0])
    n = 'ffn1_w_up'
    slots = _scatter_wait(tail_sems_s, tail_sems_r, tail_src, tail_land, meanwhile)
    outs = _adam("adam_" + n, p[n], sq(mom[n], n), sq(vel[n], n), g_slots=slots)
    grad[n], delta[n], new_m[n], new_v[n] = [unsq(o, n) for o in outs]

    return (loss, dh0[BLK:][None], *[grad[n] for n in WEIGHTS], *[delta[n] for n in WEIGHTS],
            *[new_m[n] for n in WEIGHTS], *[new_v[n] for n in WEIGHTS])


def kernel(x, meta_tokens, ffn1_pre_norm, ffn1_w_gate, ffn1_w_up, ffn1_w_down, ffn1_post_norm, mix_pre_norm, w_in, ret_group_norm, mla_q_norm, mla_w_uq, mla_kv_norm, mla_w_uk, mla_w_uv, w_out, mix_post_norm, ffn2_pre_norm, ffn2_w_gate, ffn2_w_up, ffn2_w_down, ffn2_post_norm, loss_target, m_meta_tokens, m_ffn1_pre_norm, m_ffn1_w_gate, m_ffn1_w_up, m_ffn1_w_down, m_ffn1_post_norm, m_mix_pre_norm, m_w_in, m_ret_group_norm, m_mla_q_norm, m_mla_w_uq, m_mla_kv_norm, m_mla_w_uk, m_mla_w_uv, m_w_out, m_mix_post_norm, m_ffn2_pre_norm, m_ffn2_w_gate, m_ffn2_w_up, m_ffn2_w_down, m_ffn2_post_norm, v_meta_tokens, v_ffn1_pre_norm, v_ffn1_w_gate, v_ffn1_w_up, v_ffn1_w_down, v_ffn1_post_norm, v_mix_pre_norm, v_w_in, v_ret_group_norm, v_mla_q_norm, v_mla_w_uq, v_mla_kv_norm, v_mla_w_uk, v_mla_w_uv, v_w_out, v_mix_post_norm, v_ffn2_pre_norm, v_ffn2_w_gate, v_ffn2_w_up, v_ffn2_w_down, v_ffn2_post_norm):
    w = dict(zip(WEIGHTS, (meta_tokens, ffn1_pre_norm, ffn1_w_gate, ffn1_w_up, ffn1_w_down, ffn1_post_norm,
                           mix_pre_norm, w_in, ret_group_norm, mla_q_norm, mla_w_uq, mla_kv_norm, mla_w_uk, mla_w_uv,
                           w_out, mix_post_norm, ffn2_pre_norm, ffn2_w_gate, ffn2_w_up, ffn2_w_down, ffn2_post_norm)))
    mom = dict(zip(WEIGHTS, (m_meta_tokens, m_ffn1_pre_norm, m_ffn1_w_gate, m_ffn1_w_up, m_ffn1_w_down,
                             m_ffn1_post_norm, m_mix_pre_norm, m_w_in, m_ret_group_norm, m_mla_q_norm, m_mla_w_uq,
                             m_mla_kv_norm, m_mla_w_uk, m_mla_w_uv, m_w_out, m_mix_post_norm, m_ffn2_pre_norm,
                             m_ffn2_w_gate, m_ffn2_w_up, m_ffn2_w_down, m_ffn2_post_norm)))
    vel = dict(zip(WEIGHTS, (v_meta_tokens, v_ffn1_pre_norm, v_ffn1_w_gate, v_ffn1_w_up, v_ffn1_w_down,
                             v_ffn1_post_norm, v_mix_pre_norm, v_w_in, v_ret_group_norm, v_mla_q_norm, v_mla_w_uq,
                             v_mla_kv_norm, v_mla_w_uk, v_mla_w_uv, v_w_out, v_mix_post_norm, v_ffn2_pre_norm,
                             v_ffn2_w_gate, v_ffn2_w_up, v_ffn2_w_down, v_ffn2_post_norm)))
    return _step(x, loss_target, w, mom, vel)
```

```python
import functools
import math

import jax
import jax.numpy as jnp
from jax import lax
from jax.experimental import pallas as pl
from jax.experimental.pallas import tpu as pltpu

N_DEV = 8
N_META = 16
BLK = 128
HEADS = 8
HD = 128
ROPE = 64
Q_RANK = 512
KV_RANK = 256
QH = 2 * HD
D_INP = 4 * HEADS * HD + Q_RANK + KV_RANK + BLK
ROPE_THETA = 10000.0
EPS = 1e-6
ADAM_LR = 0.001
ADAM_B1 = 0.9
ADAM_B2 = 0.999
ADAM_EPS = 1e-08
ADAM_WD = 0.01
ADAM_STEP = 10
V7X_VMEM_LIMIT = 48 * 1024 * 1024
MESH = pl.DeviceIdType.MESH
F32 = jnp.float32
BF16 = jnp.bfloat16

WEIGHTS = ['meta_tokens', 'ffn1_pre_norm', 'ffn1_w_gate', 'ffn1_w_up', 'ffn1_w_down', 'ffn1_post_norm',
           'mix_pre_norm', 'w_in', 'ret_group_norm', 'mla_q_norm', 'mla_w_uq', 'mla_kv_norm', 'mla_w_uk',
           'mla_w_uv', 'w_out', 'mix_post_norm', 'ffn2_pre_norm', 'ffn2_w_gate', 'ffn2_w_up', 'ffn2_w_down',
           'ffn2_post_norm']
SMALL = ['ffn1_pre_norm', 'ffn1_post_norm', 'mix_pre_norm', 'ret_group_norm', 'mla_q_norm', 'mla_kv_norm',
         'mix_post_norm', 'ffn2_pre_norm', 'ffn2_post_norm']
TRANSPOSED = ('ffn1_w_gate', 'ffn1_w_up', 'ffn2_w_gate', 'ffn2_w_up', 'w_in', 'mla_w_uq')
BIG = ['ffn1_w_gate', 'ffn1_w_up', 'ffn1_w_down', 'w_in', 'mla_w_uq', 'mla_w_uk', 'mla_w_uv', 'w_out',
       'ffn2_w_gate', 'ffn2_w_up', 'ffn2_w_down']

_DIMS = {'nn': (((1,), (0,)), ((), ())), 'nt': (((1,), (1,)), ((), ())), 'tn': (((0,), (0,)), ((), ()))}


def _tile(n, target, mult=16):
    best = None
    for t in range(mult, min(n, target) + 1, mult):
        if n % t == 0:
            best = t
    return best if best is not None else n


def _params(sem):
    return pltpu.CompilerParams(dimension_semantics=sem, vmem_limit_bytes=V7X_VMEM_LIMIT)


def _dot(a, b, dims):
    return lax.dot_general(a, b, _DIMS[dims], preferred_element_type=F32)


def _sigmoid(x):
    return 0.5 * jnp.tanh(0.5 * x) + 0.5


def _me_and_peers():
    x, y, c = lax.axis_index("x"), lax.axis_index("y"), lax.axis_index("c")

    def peer(j):
        px = 1 - x if (j >> 2) & 1 else x
        py = 1 - y if (j >> 1) & 1 else y
        pc = 1 - c if j & 1 else c
        return (px, py, pc), 4 * px + 2 * py + pc

    return 4 * x + 2 * y + c, peer


class _Exchange:
    def __init__(self, arrays, per_peer):
        self.arrays = list(arrays)
        self.per_peer = per_peer
        self.n = len(self.arrays)
        self.out_shapes = [jax.ShapeDtypeStruct((N_DEV,) + tuple(a.shape[1:] if per_peer else a.shape), a.dtype)
                           for a in self.arrays]
        self.specs = [pl.BlockSpec(memory_space=pl.ANY)] * self.n
        self.scratch = [pltpu.SemaphoreType.DMA((7 * self.n,)), pltpu.SemaphoreType.DMA((7 * self.n,)),
                        pltpu.SemaphoreType.DMA((self.n,))]

    def _copies(self, src, dst, sems):
        send_sems, recv_sems, local_sems = sems
        me, peer = _me_and_peers()
        sib, _ = peer(1)
        local, sends, recvs, passes = [], {}, {}, {}
        for k in range(self.n):
            own = src[k].at[me] if self.per_peer else src[k]
            local.append(pltpu.make_async_copy(own, dst[k].at[me], local_sems.at[k]))
            for j in range(1, N_DEV):
                pid, pidx = peer(j)
                out = src[k].at[pidx] if self.per_peer else src[k]
                sem = dict(send_sem=send_sems.at[k * 7 + j - 1], recv_sem=recv_sems.at[k * 7 + j - 1])
                recvs[k, j] = pltpu.make_async_remote_copy(src_ref=out, dst_ref=dst[k].at[pidx], device_id=pid,
                                                           device_id_type=MESH, **sem)
                if self.per_peer or j in (1, 2, 4, 6):
                    sends[k, j] = pltpu.make_async_remote_copy(src_ref=out, dst_ref=dst[k].at[me], device_id=pid,
                                                               device_id_type=MESH, **sem)
                else:
                    _, origin = peer(j ^ 1)
                    passes[k, j ^ 1] = pltpu.make_async_remote_copy(
                        src_ref=dst[k].at[origin], dst_ref=dst[k].at[origin], device_id=sib, device_id_type=MESH, **sem)
        return local, sends, recvs, passes

    def start(self, src, dst, sems):
        local, sends, _, _ = self._copies(src, dst, sems)
        for cp in local + list(sends.values()):
            cp.start()

    def finish(self, src, dst, sems):
        local, sends, recvs, passes = self._copies(src, dst, sems)
        for key, cp in passes.items():
            recvs[key].wait_recv()
            cp.start()
        for key, cp in recvs.items():
            if key not in passes:
                cp.wait_recv()
        for cp in list(sends.values()) + list(passes.values()):
            cp.wait_send()
        for cp in local:
            cp.wait()


def _grid_edges(grid):
    first, last = None, None
    for a, n in enumerate(grid):
        f, l = pl.program_id(a) == 0, pl.program_id(a) == n - 1
        first = f if first is None else first & f
        last = l if last is None else last & l
    return first, last


def _exchange(name, arrays, per_peer):
    ex = _Exchange(arrays, per_peer)
    n = ex.n

    def body(*refs):
        ex.start(refs[:n], refs[n:2 * n], refs[2 * n:])
        ex.finish(refs[:n], refs[n:2 * n], refs[2 * n:])

    return pl.pallas_call(body, name=name, out_shape=ex.out_shapes, in_specs=ex.specs, out_specs=ex.specs,
                          scratch_shapes=ex.scratch)(*arrays)


def _scatter_start(blocks):
    def body(src_ref, land_ref, send_sems, recv_sems, src_thru, land_thru, token, local_sem):
        me, peer = _me_and_peers()
        local = pltpu.make_async_copy(src_ref.at[me], land_ref.at[me], local_sem)
        local.start()
        for j in range(1, N_DEV):
            pid, pidx = peer(j)
            pltpu.make_async_remote_copy(src_ref=src_ref.at[pidx], dst_ref=land_ref.at[me],
                                         send_sem=send_sems.at[j - 1], recv_sem=recv_sems.at[j - 1],
                                         device_id=pid, device_id_type=MESH).start()
        local.wait()
        token[...] = jnp.zeros_like(token)

    hbm = pl.BlockSpec(memory_space=pltpu.HBM)
    sem = pl.BlockSpec(memory_space=pltpu.SEMAPHORE)
    return pl.pallas_call(
        body, name="scatter_tail_start",
        out_shape=(pltpu.SemaphoreType.DMA((7,)), pltpu.SemaphoreType.DMA((7,)), pltpu.HBM(blocks.shape, blocks.dtype),
                   pltpu.HBM(blocks.shape, blocks.dtype), jax.ShapeDtypeStruct((8, 128), F32)),
        in_specs=(hbm, hbm), out_specs=(sem, sem, hbm, hbm, pl.BlockSpec(memory_space=pltpu.VMEM)),
        input_output_aliases={0: 2, 1: 3}, scratch_shapes=[pltpu.SemaphoreType.DMA],
        compiler_params=pltpu.CompilerParams(has_side_effects=pltpu.SideEffectType.DATAFLOW_SIDE_EFFECTING),
    )(pltpu.with_memory_space_constraint(blocks, pltpu.HBM),
      pltpu.with_memory_space_constraint(lax.empty(blocks.shape, blocks.dtype), pltpu.HBM))


def _scatter_wait(send_sems, recv_sems, src_thru, land_thru, after):
    n_after = len(after)

    def body(src_ref, land_ref, send_sems, recv_sems, *rest):
        me, peer = _me_and_peers()
        for j in range(1, N_DEV):
            pid, pidx = peer(j)
            cp = pltpu.make_async_remote_copy(src_ref=src_ref.at[pidx], dst_ref=land_ref.at[pidx],
                                              send_sem=send_sems.at[j - 1], recv_sem=recv_sems.at[j - 1],
                                              device_id=pid, device_id_type=MESH)
            cp.wait_send()
            cp.wait_recv()

    hbm = pl.BlockSpec(memory_space=pltpu.HBM)
    sem = pl.BlockSpec(memory_space=pltpu.SEMAPHORE)
    return pl.pallas_call(
        body, name="scatter_tail_wait",
        out_shape=(pltpu.HBM(src_thru.shape, src_thru.dtype), pltpu.HBM(land_thru.shape, land_thru.dtype)),
        in_specs=(hbm, hbm, sem, sem) + (pl.BlockSpec(memory_space=pl.ANY),) * n_after, out_specs=(hbm, hbm),
        input_output_aliases={0: 0, 1: 1},
        compiler_params=pltpu.CompilerParams(has_side_effects=pltpu.SideEffectType.DATAFLOW_SIDE_EFFECTING),
    )(src_thru, land_thru, send_sems, recv_sems, *after)[1]


def _allreduce_small(v):
    rows = v.shape[0]

    def body(v_ref, out_ref, buf, send_sems, recv_sems):
        me, peer = _me_and_peers()
        buf[pl.ds(me, 1)] = v_ref[...][None]
        sends = []
        for j in range(1, N_DEV):
            pid, _ = peer(j)
            cp = pltpu.make_async_remote_copy(src_ref=v_ref, dst_ref=buf.at[me], send_sem=send_sems.at[j - 1],
                                              recv_sem=recv_sems.at[j - 1], device_id=pid, device_id_type=MESH)
            cp.start()
            sends.append(cp)
        for j in range(1, N_DEV):
            pid, pidx = peer(j)
            pltpu.make_async_remote_copy(src_ref=v_ref, dst_ref=buf.at[pidx], send_sem=send_sems.at[j - 1],
                                         recv_sem=recv_sems.at[j - 1], device_id=pid,
                                         device_id_type=MESH).wait_recv()
        for cp in sends:
            cp.wait_send()
        acc = buf[0]
        for s in range(1, N_DEV):
            acc = acc + buf[s]
        out_ref[...] = acc

    vm = pl.BlockSpec(memory_space=pltpu.VMEM)
    return pl.pallas_call(
        body, name="allreduce_small", out_shape=jax.ShapeDtypeStruct(v.shape, F32),
        in_specs=[vm], out_specs=vm,
        scratch_shapes=[pltpu.VMEM((N_DEV, rows, 128), F32), pltpu.SemaphoreType.DMA((7,)),
                        pltpu.SemaphoreType.DMA((7,))],
    )(v)


def _mm(name, grid, sem, k_axis, ops, op_specs, pairs, acc_shapes, extras, extra_specs, epilogue, outs, out_specs,
        comm=None):
    n_op, n_ex, n_out = len(ops), len(extras), len(outs)
    nk = grid[k_axis] if k_axis is not None else 1
    n_acc = len(acc_shapes) if nk > 1 else 0
    n_cm = comm.n if comm is not None else 0

    def body(*refs):
        op_refs = refs[:n_op]
        ex_refs = refs[n_op:n_op + n_ex]
        n_in = n_op + n_ex + n_cm
        out_refs = refs[n_in:n_in + n_out]
        acc_refs = refs[n_in + n_out + n_cm:n_in + n_out + n_cm + n_acc]
        if comm is not None:
            cm_refs = (refs[n_op + n_ex:n_in], refs[n_in + n_out:n_in + n_out + n_cm],
                       refs[n_in + n_out + n_cm + n_acc:])
            first, last = _grid_edges(grid)

            @pl.when(first)
            def _():
                comm.start(*cm_refs)

        def finish(vals):
            res = epilogue(*vals, *[e[...] for e in ex_refs])
            for o, r in zip(out_refs, res):
                o[...] = r.astype(o.dtype)

        if nk == 1:
            parts = [None] * len(acc_shapes)
            for li, ri, dims, ai in pairs:
                d = _dot(op_refs[li][...], op_refs[ri][...], dims)
                parts[ai] = d if parts[ai] is None else parts[ai] + d
            finish(parts)
        else:
            k = pl.program_id(k_axis)

            @pl.when(k == 0)
            def _():
                for a in acc_refs:
                    a[...] = jnp.zeros_like(a)

            for li, ri, dims, ai in pairs:
                acc_refs[ai][...] += _dot(op_refs[li][...], op_refs[ri][...], dims)

            @pl.when(k == nk - 1)
            def _():
                finish([a[...] for a in acc_refs])

        if comm is not None:
            @pl.when(last)
            def _():
                comm.finish(*cm_refs)

    scratch = [pltpu.VMEM(s, F32) for s in acc_shapes] if nk > 1 else []
    if comm is None:
        return pl.pallas_call(
            body, name=name, grid=grid, out_shape=outs,
            in_specs=list(op_specs) + list(extra_specs), out_specs=list(out_specs),
            scratch_shapes=scratch, compiler_params=_params(sem),
        )(*ops, *extras)
    res = pl.pallas_call(
        body, name=name, grid=grid, out_shape=list(outs) + comm.out_shapes,
        in_specs=list(op_specs) + list(extra_specs) + comm.specs, out_specs=list(out_specs) + comm.specs,
        scratch_shapes=scratch + comm.scratch, compiler_params=_params(("arbitrary",) * len(grid)),
    )(*ops, *extras, *comm.arrays)
    return res[:n_out], res[n_out:]


def _with_comm(res, comm, pick):
    if comm is None:
        return pick(res)
    return pick(res[0]), res[1]


def _mm_nn(name, a, w, out_dtype, tm_target=704, tn_target=1664, epilogue=None, extras=(), extra_specs=(), comm=None):
    L, K = a.shape
    N = w.shape[1]
    tm, tn = _tile(L, tm_target), _tile(N, tn_target, 128)
    ep = epilogue if epilogue is not None else (lambda acc: (acc,))
    res = _mm(name, (L // tm, N // tn), ("parallel", "parallel"), None,
              [a, w], [pl.BlockSpec((tm, K), lambda i, j: (i, 0)), pl.BlockSpec((K, tn), lambda i, j: (0, j))],
              [(0, 1, 'nn', 0)], [(tm, tn)], list(extras), list(extra_specs), ep,
              [jax.ShapeDtypeStruct((L, N), out_dtype)], [pl.BlockSpec((tm, tn), lambda i, j: (i, j))], comm=comm)
    return _with_comm(res, comm, lambda o: o[0])


def _mm_nt(name, pairs_aw, out_dtype, tm_target=704, tn_target=512, comm=None):
    L = pairs_aw[0][0].shape[0]
    N = pairs_aw[0][1].shape[0]
    tm, tn = _tile(L, tm_target), _tile(N, tn_target, 128)
    ops, specs, pairs = [], [], []
    for t, (a, w) in enumerate(pairs_aw):
        K = a.shape[1]
        ops += [a, w]
        specs += [pl.BlockSpec((tm, K), lambda i, j: (i, 0)), pl.BlockSpec((tn, K), lambda i, j: (j, 0))]
        pairs.append((2 * t, 2 * t + 1, 'nt', 0))
    res = _mm(name, (L // tm, N // tn), ("parallel", "parallel"), None, ops, specs, pairs, [(tm, tn)], [], [],
              lambda acc: (acc,), [jax.ShapeDtypeStruct((L, N), out_dtype)],
              [pl.BlockSpec((tm, tn), lambda i, j: (i, j))], comm=comm)
    return _with_comm(res, comm, lambda o: o[0])


def _mm_tn(name, a, bs, out_dtype=BF16, tk_target=1408, tn_target=1664, tm_target=2048, comm=None):
    L, M = a.shape
    N = bs[0].shape[1]
    tk, tn, tm = _tile(L, tk_target), _tile(N, tn_target, 128), _tile(M, tm_target, 128)
    nb = len(bs)
    ops = [a] + list(bs)
    specs = [pl.BlockSpec((tk, tm), lambda i, j, k: (k, i))] + [pl.BlockSpec((tk, tn), lambda i, j, k: (k, j))] * nb
    res = _mm(name, (M // tm, N // tn, L // tk), ("parallel", "parallel", "arbitrary"), 2, ops, specs,
              [(0, 1 + t, 'tn', t) for t in range(nb)], [(tm, tn)] * nb, [], [], lambda *acc: acc,
              [jax.ShapeDtypeStruct((M, N), out_dtype)] * nb,
              [pl.BlockSpec((tm, tn), lambda i, j, k: (i, j))] * nb, comm=comm)
    return _with_comm(res, comm, lambda o: o)


def _norm_fwd(x, w):
    L, D = x.shape
    tr = _tile(L, 512)

    def body(x_ref, w_ref, y_ref):
        v = x_ref[...]
        r = lax.rsqrt(jnp.mean(v * v, axis=-1, keepdims=True) + EPS)
        y_ref[...] = (v * r * w_ref[...]).astype(y_ref.dtype)

    return pl.pallas_call(
        body, name="norm_fwd", grid=(L // tr,), out_shape=jax.ShapeDtypeStruct((L, D), BF16),
        in_specs=[pl.BlockSpec((tr, D), lambda i: (i, 0)), pl.BlockSpec((1, D), lambda i: (0, 0))],
        out_specs=pl.BlockSpec((tr, D), lambda i: (i, 0)), compiler_params=_params(("parallel",)),
    )(x, w)


def _norm_bwd_math(x, w, dy):
    r = lax.rsqrt(jnp.mean(x * x, axis=-1, keepdims=True) + EPS)
    gy = dy * w
    dx = r * (gy - x * (r * r) * jnp.mean(gy * x, axis=-1, keepdims=True))
    dw = jnp.sum(dy * x * r, axis=0, keepdims=True)
    return dx, dw


def _norm_bwd(x, w, dy, res, scale, out_dtype):
    L, D = x.shape
    tr = _tile(L, 384)
    has_res = res is not None

    def body(*refs):
        x_ref, w_ref, dy_ref = refs[:3]
        res_ref = refs[3] if has_res else None
        dx_ref, dw_ref = refs[-2:]
        dx, dw = _norm_bwd_math(x_ref[...], w_ref[...], dy_ref[...].astype(F32))
        dx = scale * dx
        if has_res:
            dx = dx + res_ref[...]
        dx_ref[...] = dx.astype(dx_ref.dtype)

        @pl.when(pl.program_id(0) == 0)
        def _():
            dw_ref[...] = jnp.zeros_like(dw_ref)

        dw_ref[...] += scale * dw

    row = pl.BlockSpec((tr, D), lambda i: (i, 0))
    vec = pl.BlockSpec((1, D), lambda i: (0, 0))
    return pl.pallas_call(
        body, name="norm_bwd", grid=(L // tr,),
        out_shape=[jax.ShapeDtypeStruct((L, D), out_dtype), jax.ShapeDtypeStruct((1, D), F32)],
        in_specs=[row, vec, row] + ([row] if has_res else []), out_specs=[row, vec],
        compiler_params=_params(("arbitrary",)),
    )(*([x, w, dy] + ([res] if has_res else [])))


def _loss(h, target):
    L, D = h.shape

    def body(h_ref, t_ref, dh_ref, loss_ref):
        i = pl.program_id(0)

        @pl.when(i == 0)
        def _():
            dh_ref[...] = jnp.zeros_like(dh_ref)
            loss_ref[...] = jnp.zeros_like(loss_ref)

        @pl.when(i > 0)
        def _():
            diff = h_ref[...] - t_ref[...]
            dh_ref[...] = diff * (1.0 / D)
            loss_ref[...] += 0.5 * jnp.sum(diff * diff) * (1.0 / D)

    return pl.pallas_call(
        body, name="loss", grid=(L // BLK,),
        out_shape=[jax.ShapeDtypeStruct((L, D), F32), jax.ShapeDtypeStruct((8, 128), F32)],
        in_specs=[pl.BlockSpec((BLK, D), lambda i: (i, 0)),
                  pl.BlockSpec((BLK, D), lambda i: (jnp.maximum(i - 1, 0), 0))],
        out_specs=[pl.BlockSpec((BLK, D), lambda i: (i, 0)), pl.BlockSpec((8, 128), lambda i: (0, 0))],
        compiler_params=_params(("arbitrary",)),
    )(h, target)


def _ffn_up(a, wg, wu, comm=None):
    L, D = a.shape
    F = wg.shape[1]
    tm = _tile(L, 704)

    def ep(g, u):
        return g, u, g * _sigmoid(g) * u

    hspec = pl.BlockSpec((None, tm, F), lambda i, j: (j, i, 0))
    wspec = pl.BlockSpec((None, F, D), lambda i, j: (j, 0, 0))
    res = _mm("ffn_up", (L // tm, N_DEV), ("parallel", "parallel"), None,
              [a, wg, wu], [pl.BlockSpec((tm, D), lambda i, j: (i, 0)), wspec, wspec],
              [(0, 1, 'nt', 0), (0, 2, 'nt', 1)], [(tm, F)] * 2, [], [], ep,
              [jax.ShapeDtypeStruct((N_DEV, L, F), BF16)] * 3, [hspec] * 3, comm=comm)
    return _with_comm(res, comm, lambda o: o)


def _ffn_gate(a, wg, comm=None):
    L, D = a.shape
    F = wg.shape[1]
    tm = _tile(L, 704)
    res = _mm("ffn_gate", (L // tm, N_DEV), ("parallel", "parallel"), None,
              [a, wg], [pl.BlockSpec((tm, D), lambda i, j: (i, 0)), pl.BlockSpec((None, F, D), lambda i, j: (j, 0, 0))],
              [(0, 1, 'nt', 0)], [(tm, F)], [], [], lambda g: (g,),
              [jax.ShapeDtypeStruct((N_DEV, L, F), BF16)], [pl.BlockSpec((None, tm, F), lambda i, j: (j, i, 0))],
              comm=comm)
    return _with_comm(res, comm, lambda o: o[0])


def _ffn_up_gated(a, wu, g, comm=None):
    L, D = a.shape
    F = wu.shape[1]
    tm = _tile(L, 704)

    def ep(u, g_):
        g32 = g_.astype(F32)
        return u, g32 * _sigmoid(g32) * u

    hspec = pl.BlockSpec((None, tm, F), lambda i, j: (j, i, 0))
    res = _mm("ffn_up_gated", (L // tm, N_DEV), ("parallel", "parallel"), None,
              [a, wu], [pl.BlockSpec((tm, D), lambda i, j: (i, 0)), pl.BlockSpec((None, F, D), lambda i, j: (j, 0, 0))],
              [(0, 1, 'nt', 0)], [(tm, F)], [g], [hspec], ep,
              [jax.ShapeDtypeStruct((N_DEV, L, F), BF16)] * 2, [hspec, hspec], comm=comm)
    return _with_comm(res, comm, lambda o: o)


def _resnorm_epilogue(scale, with_next):
    def ep(acc, h, w, *w_next):
        r = lax.rsqrt(jnp.mean(acc * acc, axis=-1, keepdims=True) + EPS)
        h_out = h + scale * (acc * r * w)
        if not with_next:
            return acc, h_out
        r_next = lax.rsqrt(jnp.mean(h_out * h_out, axis=-1, keepdims=True) + EPS)
        return acc, h_out, h_out * r_next * w_next[0]
    return ep


def _ffn_down(hid, wd, h_in, post, next_norm=None, comm=None):
    _, L, F = hid.shape
    D = wd.shape[2]
    tm = _tile(L, 528)
    row = pl.BlockSpec((tm, D), lambda i, j: (i, 0))
    vec = pl.BlockSpec((1, D), lambda i, j: (0, 0))
    nxt = [] if next_norm is None else [next_norm]
    res = _mm("ffn_down", (L // tm, N_DEV), ("parallel", "arbitrary"), 1,
              [hid, wd], [pl.BlockSpec((None, tm, F), lambda i, j: (j, i, 0)),
                          pl.BlockSpec((None, F, D), lambda i, j: (j, 0, 0))],
              [(0, 1, 'nn', 0)], [(tm, D)], [h_in, post] + nxt, [row, vec] + [vec] * len(nxt),
              _resnorm_epilogue(0.5, bool(nxt)),
              [jax.ShapeDtypeStruct((L, D), F32)] * 2 + [jax.ShapeDtypeStruct((L, D), BF16)] * len(nxt),
              [row] * (2 + len(nxt)), comm=comm)
    return _with_comm(res, comm, lambda o: o)


def _ffn_dhid(df, wd, g, u, comm=None):
    L, D = df.shape
    F = wd.shape[1]
    tm = _tile(L, 704)

    def ep(dhid, g_, u_):
        g32, u32 = g_.astype(F32), u_.astype(F32)
        sg = _sigmoid(g32)
        return dhid * u32 * sg * (1.0 + g32 * (1.0 - sg)), dhid * g32 * sg

    hspec = pl.BlockSpec((None, tm, F), lambda i, j: (j, i, 0))
    res = _mm("ffn_dhid", (L // tm, N_DEV), ("parallel", "parallel"), None,
              [df, wd], [pl.BlockSpec((tm, D), lambda i, j: (i, 0)),
                         pl.BlockSpec((None, F, D), lambda i, j: (j, 0, 0))],
              [(0, 1, 'nt', 0)], [(tm, F)], [g, u], [hspec, hspec], ep,
              [jax.ShapeDtypeStruct((N_DEV, L, F), BF16)] * 2, [hspec, hspec], comm=comm)
    return _with_comm(res, comm, lambda o: o)


def _ffn_dwd(hid, df, comm=None):
    _, L, F = hid.shape
    D = df.shape[1]
    tk = _tile(L, 1408)
    res = _mm("ffn_dwd", (N_DEV, L // tk), ("parallel", "arbitrary"), 1,
              [hid, df], [pl.BlockSpec((None, tk, F), lambda j, k: (j, k, 0)),
                          pl.BlockSpec((tk, D), lambda j, k: (k, 0))],
              [(0, 1, 'tn', 0)], [(F, D)], [], [], lambda acc: (acc,),
              [jax.ShapeDtypeStruct((N_DEV, F, D), BF16)], [pl.BlockSpec((None, F, D), lambda j, k: (j, 0, 0))],
              comm=comm)
    return _with_comm(res, comm, lambda o: o[0])


def _ffn_dwgu(a, dg, du, comm=None):
    L, D = a.shape
    F = dg.shape[2]
    tk = _tile(L, 1408)
    hspec = pl.BlockSpec((None, tk, F), lambda j, k: (j, k, 0))
    wspec = pl.BlockSpec((None, F, D), lambda j, k: (j, 0, 0))
    res = _mm("ffn_dwgu", (N_DEV, L // tk), ("parallel", "arbitrary"), 1,
              [a, dg, du], [pl.BlockSpec((tk, D), lambda j, k: (k, 0)), hspec, hspec],
              [(1, 0, 'tn', 0), (2, 0, 'tn', 1)], [(F, D)] * 2, [], [], lambda *acc: acc,
              [jax.ShapeDtypeStruct((N_DEV, F, D), BF16)] * 2, [wspec, wspec], comm=comm)
    return _with_comm(res, comm, lambda o: o)


def _ffn_da(dg, du, wg, wu, comm=None):
    _, L, F = dg.shape
    D = wg.shape[2]
    tm = _tile(L, 704)
    hspec = pl.BlockSpec((None, tm, F), lambda i, j: (j, i, 0))
    wspec = pl.BlockSpec((None, F, D), lambda i, j: (j, 0, 0))
    row = pl.BlockSpec((tm, D), lambda i, j: (i, 0))
    res = _mm("ffn_da", (L // tm, N_DEV), ("parallel", "arbitrary"), 1,
              [dg, du, wg, wu], [hspec, hspec, wspec, wspec],
              [(0, 2, 'nn', 0), (1, 3, 'nn', 0)], [(tm, D)], [], [], lambda acc: (acc,),
              [jax.ShapeDtypeStruct((L, D), F32)], [row], comm=comm)
    return _with_comm(res, comm, lambda o: o[0])


def _rope_tables(L):
    rows = jnp.arange(L, dtype=F32)
    pos = jnp.where(rows < BLK, rows, rows - (BLK - N_META))
    inv_r = ROPE_THETA ** (-jnp.arange(0, HD, 2, dtype=F32) / HD)
    ang_r = pos[:, None] * inv_r[None, :]
    cr = jnp.concatenate([jnp.cos(ang_r), jnp.cos(ang_r)], axis=1)
    sr = jnp.concatenate([-jnp.sin(ang_r), jnp.sin(ang_r)], axis=1)
    inv_m = ROPE_THETA ** (-jnp.arange(0, ROPE, 2, dtype=F32) / ROPE)
    ang_m = pos[:, None] * inv_m[None, :]
    z32 = jnp.zeros((L, ROPE // 2), F32)
    z64 = jnp.zeros((L, HD - ROPE), F32)
    cm = jnp.concatenate([jnp.cos(ang_m), jnp.cos(ang_m), z64], axis=1)
    sa = jnp.concatenate([-jnp.sin(ang_m), z32, z64], axis=1)
    sb = jnp.concatenate([z32, jnp.sin(ang_m), z64], axis=1)
    return cr, sr, cm, sa, sb


def _rope_ret(x, cr, sr):
    return x * cr + pltpu.roll(x, HD // 2, 1) * sr


def _rope_ret_t(d, cr, sr):
    return d * cr + pltpu.roll(d * sr, HD // 2, 1)


def _rope_mla(x, cm, sa, sb):
    return x * cm + pltpu.roll(x, HD - ROPE // 2, 1) * sa + pltpu.roll(x, ROPE // 2, 1) * sb


def _rope_mla_t(d, cm, sa, sb):
    return d * cm + pltpu.roll(d * sa, ROPE // 2, 1) + pltpu.roll(d * sb, HD - ROPE // 2, 1)


C_RQ, C_RK, C_RV, C_RG = 0, HEADS * HD, 2 * HEADS * HD, 3 * HEADS * HD
C_CQ = 4 * HEADS * HD
C_CKV = C_CQ + Q_RANK
C_KR = C_CKV + KV_RANK
RET_K_SCALE = HD ** -0.5


def _prep(proj, tabs, qn, kvn):
    L = proj.shape[0]
    tr = _tile(L, 256)
    W = HEADS * HD

    def body(p_ref, cr_ref, sr_ref, cm_ref, sa_ref, sb_ref, qn_ref, kvn_ref, q_ref, k_ref, v_ref, cq_ref, ckv_ref,
             kr_ref):
        cr, sr = cr_ref[...], sr_ref[...]
        for h in range(HEADS):
            sl = slice(h * HD, (h + 1) * HD)
            q_ref[:, sl] = _rope_ret(p_ref[:, C_RQ + h * HD:C_RQ + (h + 1) * HD].astype(F32), cr, sr).astype(BF16)
            k_ref[:, sl] = (_rope_ret(p_ref[:, C_RK + h * HD:C_RK + (h + 1) * HD].astype(F32), cr, sr)
                            * RET_K_SCALE).astype(BF16)
        v_ref[...] = p_ref[:, C_RV:C_RV + W].astype(BF16)
        cq = p_ref[:, C_CQ:C_CQ + Q_RANK].astype(F32)
        cq_ref[...] = (cq * lax.rsqrt(jnp.mean(cq * cq, axis=-1, keepdims=True) + EPS) * qn_ref[...]).astype(BF16)
        ckv = p_ref[:, C_CKV:C_CKV + KV_RANK].astype(F32)
        ckv_ref[...] = (ckv * lax.rsqrt(jnp.mean(ckv * ckv, axis=-1, keepdims=True) + EPS)
                        * kvn_ref[...]).astype(BF16)
        kr_ref[...] = _rope_mla(p_ref[:, C_KR:C_KR + HD].astype(F32), cm_ref[...], sa_ref[...], sb_ref[...]).astype(BF16)

    row = lambda w: pl.BlockSpec((tr, w), lambda i: (i, 0))
    vec = lambda w: pl.BlockSpec((1, w), lambda i: (0, 0))
    return pl.pallas_call(
        body, name="mix_prep", grid=(L // tr,),
        out_shape=[jax.ShapeDtypeStruct((L, W), BF16)] * 3 + [jax.ShapeDtypeStruct((L, Q_RANK), BF16),
                                                              jax.ShapeDtypeStruct((L, KV_RANK), BF16),
                                                              jax.ShapeDtypeStruct((L, HD), BF16)],
        in_specs=[row(D_INP)] + [row(HD)] * 5 + [vec(Q_RANK), vec(KV_RANK)],
        out_specs=[row(W)] * 3 + [row(Q_RANK), row(KV_RANK), row(HD)],
        compiler_params=_params(("parallel",)),
    )(proj, *tabs, qn, kvn)


def _prep_bwd(proj, dq, dk, dv, drg, dcqn, dckvn, dkr8, tabs, qn, kvn):
    L = proj.shape[0]
    tr = _tile(L, 192)
    W = HEADS * HD

    def body(p_ref, dq_ref, dk_ref, dv_ref, drg_ref, dcq_ref, dckv_ref, dkr_ref, cr_ref, sr_ref, cm_ref, sa_ref,
             sb_ref, qn_ref, kvn_ref, dp_ref, dqn_ref, dkvn_ref):
        cr, sr = cr_ref[...], sr_ref[...]
        dkr = None
        for h in range(HEADS):
            sl = slice(h * HD, (h + 1) * HD)
            dp_ref[:, C_RQ + h * HD:C_RQ + (h + 1) * HD] = _rope_ret_t(dq_ref[:, sl].astype(F32), cr, sr).astype(BF16)
            dp_ref[:, C_RK + h * HD:C_RK + (h + 1) * HD] = (_rope_ret_t(dk_ref[:, sl].astype(F32), cr, sr)
                                                            * RET_K_SCALE).astype(BF16)
            part = dkr_ref[:, sl].astype(F32)
            dkr = part if dkr is None else dkr + part
        dp_ref[:, C_RV:C_RV + W] = dv_ref[...].astype(BF16)
        dp_ref[:, C_RG:C_RG + W] = drg_ref[...].astype(BF16)
        dcq, dqn = _norm_bwd_math(p_ref[:, C_CQ:C_CQ + Q_RANK].astype(F32), qn_ref[...], dcq_ref[...])
        dp_ref[:, C_CQ:C_CQ + Q_RANK] = dcq.astype(BF16)
        dckv, dkvn = _norm_bwd_math(p_ref[:, C_CKV:C_CKV + KV_RANK].astype(F32), kvn_ref[...], dckv_ref[...])
        dp_ref[:, C_CKV:C_CKV + KV_RANK] = dckv.astype(BF16)
        dp_ref[:, C_KR:C_KR + HD] = _rope_mla_t(dkr, cm_ref[...], sa_ref[...], sb_ref[...]).astype(BF16)

        @pl.when(pl.program_id(0) == 0)
        def _():
            dqn_ref[...] = jnp.zeros_like(dqn_ref)
            dkvn_ref[...] = jnp.zeros_like(dkvn_ref)

        dqn_ref[...] += dqn
        dkvn_ref[...] += dkvn

    row = lambda w: pl.BlockSpec((tr, w), lambda i: (i, 0))
    vec = lambda w: pl.BlockSpec((1, w), lambda i: (0, 0))
    return pl.pallas_call(
        body, name="mix_prep_bwd", grid=(L // tr,),
        out_shape=[jax.ShapeDtypeStruct((L, D_INP), BF16), jax.ShapeDtypeStruct((1, Q_RANK), F32),
                   jax.ShapeDtypeStruct((1, KV_RANK), F32)],
        in_specs=[row(D_INP)] + [row(W)] * 4 + [row(Q_RANK), row(KV_RANK), row(W)] + [row(HD)] * 5
                 + [vec(Q_RANK), vec(KV_RANK)],
        out_specs=[row(D_INP), vec(Q_RANK), vec(KV_RANK)],
        compiler_params=_params(("arbitrary",)),
    )(proj, dq, dk, dv, drg, dcqn, dckvn, dkr8, *tabs, qn, kvn)


def _post(o_ret, proj, gn):
    L, W = o_ret.shape
    tr = _tile(L, 384)

    def body(o_ref, rg_ref, gn_ref, out_ref):
        for h in range(HEADS):
            sl = slice(h * HD, (h + 1) * HD)
            o = o_ref[:, sl]
            rg = rg_ref[:, sl].astype(F32)
            n = o * lax.rsqrt(jnp.mean(o * o, axis=-1, keepdims=True) + EPS)
            out_ref[:, sl] = (n * gn_ref[:, sl] * (rg * _sigmoid(rg))).astype(BF16)

    row = pl.BlockSpec((tr, W), lambda i: (i, 0))
    return pl.pallas_call(
        body, name="ret_post", grid=(L // tr,), out_shape=jax.ShapeDtypeStruct((L, W), BF16),
        in_specs=[row, pl.BlockSpec((tr, W), lambda i: (i, C_RG // W)), pl.BlockSpec((1, W), lambda i: (0, 0))],
        out_specs=row, compiler_params=_params(("parallel",)),
    )(o_ret, proj, gn)


def _post_bwd(o_ret, proj, gn, dcat):
    L, W = o_ret.shape
    tr = _tile(L, 384)

    def body(o_ref, rg_ref, gn_ref, d_ref, do_ref, drg_ref, dgn_ref):
        @pl.when(pl.program_id(0) == 0)
        def _():
            dgn_ref[...] = jnp.zeros_like(dgn_ref)

        for h in range(HEADS):
            sl = slice(h * HD, (h + 1) * HD)
            o = o_ref[:, sl]
            rg = rg_ref[:, sl].astype(F32)
            d = d_ref[:, sl].astype(F32)
            gw = gn_ref[:, sl]
            r = lax.rsqrt(jnp.mean(o * o, axis=-1, keepdims=True) + EPS)
            n = o * r
            sg = _sigmoid(rg)
            si = rg * sg
            dn = d * gw * si
            dgn_ref[:, sl] += jnp.sum(d * n * si, axis=0, keepdims=True)
            drg_ref[:, sl] = (d * n * gw * sg * (1.0 + rg * (1.0 - sg))).astype(drg_ref.dtype)
            do_ref[:, sl] = (r * (dn - o * (r * r) * jnp.mean(dn * o, axis=-1, keepdims=True))).astype(BF16)

    row = pl.BlockSpec((tr, W), lambda i: (i, 0))
    vec = pl.BlockSpec((1, W), lambda i: (0, 0))
    return pl.pallas_call(
        body, name="ret_post_bwd", grid=(L // tr,),
        out_shape=[jax.ShapeDtypeStruct((L, W), BF16), jax.ShapeDtypeStruct((L, W), BF16),
                   jax.ShapeDtypeStruct((1, W), F32)],
        in_specs=[row, pl.BlockSpec((tr, W), lambda i: (i, C_RG // W)), vec, row],
        out_specs=[row, row, vec], compiler_params=_params(("arbitrary",)),
    )(o_ret, proj, gn, dcat)


RET_HEADS_PER_STEP = 4


def _lin_attn(name, q, k, v, lg, reverse, out_dtype=F32):
    L, W = q.shape
    nc = L // BLK - 1
    G = RET_HEADS_PER_STEP

    def body(q_ref, k_ref, v_ref, lg_ref, o_ref, s_ref):
        n = lax.broadcasted_iota(jnp.int32, (BLK, BLK), 0).astype(F32)
        m = lax.broadcasted_iota(jnp.int32, (BLK, BLK), 1).astype(F32)
        dist = (m - n) if reverse else (n - m)
        consts = []
        for g in range(G):
            lgv = lg_ref[g, 0:1, :]
            dmask = jnp.where(dist >= 0, jnp.exp(lgv * jnp.maximum(dist, 0.0)), 0.0)
            c = dict(dmask=dmask, dmask0=jnp.where((n < N_META) & (m < N_META), dmask, 0.0),
                     gl=jnp.exp(lgv * float(BLK)))
            if reverse:
                c.update(inter=jnp.exp(lgv * (float(BLK) - n)), upd=jnp.exp(lgv * n),
                         inter0=jnp.where(n < N_META, jnp.exp(lgv * jnp.maximum(float(N_META) - n, 0.0)), 0.0))
            else:
                c.update(inter=jnp.exp(lgv * (n + 1.0)), upd=jnp.exp(lgv * (float(BLK) - 1.0 - n)),
                         upd0=jnp.where(n < N_META, jnp.exp(lgv * jnp.maximum(float(N_META) - 1.0 - n, 0.0)), 0.0))
            consts.append(c)

        def chunk(c):
            rows = pl.ds(pl.multiple_of(c * BLK, BLK), BLK)
            state = [s_ref[g] for g in range(G)]
            outs, new_state = [], []
            for g in range(G):
                cols = slice(g * HD, (g + 1) * HD)
                cg = consts[g]
                qc, kc, vc = q_ref[rows, cols], k_ref[rows, cols], v_ref[rows, cols]
                a = _dot(qc, kc, 'nt') * cg['dmask']
                outs.append(_dot(a.astype(BF16), vc, 'nn') + _dot(qc, state[g].astype(BF16), 'nn') * cg['inter'])
                new_state.append(state[g] * cg['gl'] + _dot((kc.astype(F32) * cg['upd']).astype(BF16), vc, 'tn'))
            for g in range(G):
                o_ref[rows, g * HD:(g + 1) * HD] = outs[g].astype(o_ref.dtype)
                s_ref[g] = new_state[g]

        def first_chunk(with_state):
            for g in range(G):
                cols = slice(g * HD, (g + 1) * HD)
                cg = consts[g]
                q0, k0, v0 = q_ref[0:BLK, cols], k_ref[0:BLK, cols], v_ref[0:BLK, cols]
                o0 = _dot((_dot(q0, k0, 'nt') * cg['dmask0']).astype(BF16), v0, 'nn')
                if with_state:
                    o0 = o0 + _dot(q0, s_ref[g].astype(BF16), 'nn') * cg['inter0']
                else:
                    s_ref[g] = _dot((k0.astype(F32) * cg['upd0']).astype(BF16), v0, 'tn')
                o_ref[0:BLK, cols] = o0.astype(o_ref.dtype)

        if reverse:
            s_ref[...] = jnp.zeros_like(s_ref)

            def step(t, carry):
                chunk(nc - t)
                return carry

            lax.fori_loop(0, nc, step, 0)
            first_chunk(True)
        else:
            first_chunk(False)

            def step(t, carry):
                chunk(t + 1)
                return carry

            lax.fori_loop(0, nc, step, 0)

    col = pl.BlockSpec((L, G * HD), lambda h: (0, h))
    return pl.pallas_call(
        body, name=name, grid=(HEADS // G,), out_shape=jax.ShapeDtypeStruct((L, W), out_dtype),
        in_specs=[col, col, col, pl.BlockSpec((G, 8, HD), lambda h: (h, 0, 0))], out_specs=col,
        scratch_shapes=[pltpu.VMEM((G, HD, HD), F32)], compiler_params=_params(("parallel",)),
    )(q, k, v, lg)


ATT_SCALE = (HD + ROPE) ** -0.5
LOG2E = 1.4426950408889634
Q_PRESCALE = ATT_SCALE * LOG2E
NEG = -1e30


ATT_TILE = 384
ATT_HEADS_PER_STEP = 2


def _att_valid(nq, nk, row0, col0):
    r = lax.broadcasted_iota(jnp.int32, (nq, nk), 0) + row0
    c = lax.broadcasted_iota(jnp.int32, (nq, nk), 1) + col0
    return (c <= r) & ((c < N_META) | (c >= BLK))


def _attn_fwd(qm, kn, krr, vm, comm=None):
    L = qm.shape[0]
    W = HEADS * HD
    T = _tile(L, ATT_TILE, BLK)
    nb = L // T
    G = ATT_HEADS_PER_STEP
    n_cm = comm.n if comm is not None else 0

    def body(*refs):
        q_ref, kn_ref, kr_ref, v_ref = refs[:4]
        o_ref, lse_ref = refs[4 + n_cm:6 + n_cm]
        m_sc, l_sc, acc_sc = refs[6 + 2 * n_cm:9 + 2 * n_cm]
        if comm is not None:
            cm_refs = (refs[4:4 + n_cm], refs[6 + n_cm:6 + 2 * n_cm], refs[9 + 2 * n_cm:])
            first, last = _grid_edges((HEADS // G, nb))

            @pl.when(first)
            def _():
                comm.start(*cm_refs)

        i = pl.program_id(1)
        m_sc[...] = jnp.full_like(m_sc, NEG)
        l_sc[...] = jnp.zeros_like(l_sc)
        acc_sc[...] = jnp.zeros_like(acc_sc)

        def tile(j, masked):
            rows = pl.ds(pl.multiple_of(j * T, T), T)
            kr = kr_ref[rows, :]
            valid = _att_valid(T, T, i * T, j * T) if masked else None
            m_prev = [m_sc[g] for g in range(G)]
            l_prev = [l_sc[g] for g in range(G)]
            acc_prev = [acc_sc[g] for g in range(G)]
            m_new, l_new, acc_new = [], [], []
            for g in range(G):
                k = jnp.concatenate([kn_ref[rows, g * HD:(g + 1) * HD], kr], axis=1)
                s = _dot(q_ref[:, g * QH:(g + 1) * QH], k, 'nt')
                if masked:
                    s = jnp.where(valid, s, NEG)
                m_new.append(jnp.maximum(m_prev[g], jnp.max(s, axis=-1, keepdims=True)))
                p = jnp.exp2(s - m_new[g])
                alpha = jnp.exp2(m_prev[g] - m_new[g])
                l_new.append(alpha * l_prev[g] + jnp.sum(p, axis=-1, keepdims=True))
                acc_new.append(alpha * acc_prev[g] + _dot(p.astype(BF16), v_ref[rows, g * HD:(g + 1) * HD], 'nn'))
            for g in range(G):
                m_sc[g] = m_new[g]
                l_sc[g] = l_new[g]
                acc_sc[g] = acc_new[g]

        tile(0, True)

        def mid(j, carry):
            tile(j, False)
            return carry

        lax.fori_loop(1, i, mid, 0)

        @pl.when(i > 0)
        def _():
            tile(i, True)

        for g in range(G):
            l = l_sc[g]
            o_ref[:, g * HD:(g + 1) * HD] = (acc_sc[g] / l).astype(o_ref.dtype)
            lse_ref[g] = jnp.broadcast_to(m_sc[g] + jnp.log(l) * LOG2E, (T, HD))

        if comm is not None:
            @pl.when(last)
            def _():
                comm.finish(*cm_refs)

    cm_specs = comm.specs if comm is not None else []
    res = pl.pallas_call(
        body, name="attn_fwd", grid=(HEADS // G, nb),
        out_shape=[jax.ShapeDtypeStruct((L, W), BF16), jax.ShapeDtypeStruct((HEADS, L, HD), F32)]
        + (comm.out_shapes if comm is not None else []),
        in_specs=[pl.BlockSpec((T, G * QH), lambda h, i: (i, h)), pl.BlockSpec((L, G * HD), lambda h, i: (0, h)),
                  pl.BlockSpec((L, HD), lambda h, i: (0, 0)), pl.BlockSpec((L, G * HD), lambda h, i: (0, h))]
        + cm_specs,
        out_specs=[pl.BlockSpec((T, G * HD), lambda h, i: (i, h)),
                   pl.BlockSpec((G, T, HD), lambda h, i: (h, i, 0))] + cm_specs,
        scratch_shapes=[pltpu.VMEM((G, T, 1), F32), pltpu.VMEM((G, T, 1), F32), pltpu.VMEM((G, T, HD), F32)]
        + (comm.scratch if comm is not None else []),
        compiler_params=_params(("arbitrary", "arbitrary")),
    )(qm, kn, krr, vm, *(comm.arrays if comm is not None else []))
    return res[:2], res[2:]


def _attn_bwd(qm, kn, krr, vm, o, dcat, lse, comm=None):
    L = qm.shape[0]
    W = HEADS * HD
    T = _tile(L, ATT_TILE, BLK)
    nb = L // T
    n_cm = comm.n if comm is not None else 0

    def body(*refs):
        q_ref, kn_ref, kr_ref, v_ref, o_ref, do_ref, lse_ref = refs[:7]
        dq_ref, dkn_ref, dkr_ref, dv_ref = refs[7 + n_cm:11 + n_cm]
        dl_sc, dk_sc, dv_sc = refs[11 + 2 * n_cm:14 + 2 * n_cm]
        if comm is not None:
            cm_refs = (refs[7:7 + n_cm], refs[11 + n_cm:11 + 2 * n_cm], refs[14 + 2 * n_cm:])
            first, last = _grid_edges((HEADS, nb))

            @pl.when(first)
            def _():
                comm.start(*cm_refs)

        j = pl.program_id(1)

        @pl.when(j == 0)
        def _():
            dq_ref[...] = jnp.zeros_like(dq_ref)

            def rowsum(t, carry):
                rows = pl.ds(pl.multiple_of(t * T, T), T)
                dl_sc[rows, :] = jnp.sum(do_ref[rows, :].astype(F32) * o_ref[rows, :].astype(F32), axis=-1,
                                         keepdims=True)
                return carry

            lax.fori_loop(0, nb, rowsum, 0)

        k = jnp.concatenate([kn_ref[...], kr_ref[...]], axis=1)
        v = v_ref[...]
        dk_sc[...] = jnp.zeros_like(dk_sc)
        dv_sc[...] = jnp.zeros_like(dv_sc)

        def tile(i, masked):
            rows = pl.ds(pl.multiple_of(i * T, T), T)
            q = q_ref[rows, :]
            do = do_ref[rows, :]
            s = _dot(q, k, 'nt')
            if masked:
                s = jnp.where(_att_valid(T, T, i * T, j * T), s, NEG)
            p = jnp.exp2(s - lse_ref[rows, 0:1])
            dv_sc[...] += _dot(p.astype(BF16), do, 'tn')
            ds = (p * (_dot(do, v, 'nt') - dl_sc[rows, :])).astype(BF16)
            dk_sc[...] += _dot(ds, q, 'tn')
            dq_ref[rows, :] += _dot(ds, k, 'nn')

        tile(j, True)

        def rest(masked):
            def step(i, carry):
                tile(i, masked)
                return carry
            lax.fori_loop(j + 1, nb, step, 0)

        @pl.when(j == 0)
        def _():
            rest(True)

        @pl.when(j > 0)
        def _():
            rest(False)

        dk = dk_sc[...] * (1.0 / LOG2E)
        dkn_ref[...] = dk[:, 0:HD].astype(BF16)
        dkr_ref[...] = dk[:, HD:QH].astype(dkr_ref.dtype)
        dv_ref[...] = dv_sc[...].astype(BF16)

        if comm is not None:
            @pl.when(last)
            def _():
                comm.finish(*cm_refs)

    blk = pl.BlockSpec((T, HD), lambda h, j: (j, h))
    cm_specs = comm.specs if comm is not None else []
    res = pl.pallas_call(
        body, name="attn_bwd", grid=(HEADS, nb),
        out_shape=[jax.ShapeDtypeStruct((L, HEADS * QH), F32), jax.ShapeDtypeStruct((L, W), BF16),
                   jax.ShapeDtypeStruct((L, W), BF16), jax.ShapeDtypeStruct((L, W), BF16)]
        + (comm.out_shapes if comm is not None else []),
        in_specs=[pl.BlockSpec((L, QH), lambda h, j: (0, h)), blk, pl.BlockSpec((T, HD), lambda h, j: (j, 0)), blk,
                  pl.BlockSpec((L, HD), lambda h, j: (0, h)), pl.BlockSpec((L, HD), lambda h, j: (0, HEADS + h)),
                  pl.BlockSpec((None, L, HD), lambda h, j: (h, 0, 0))] + cm_specs,
        out_specs=[pl.BlockSpec((L, QH), lambda h, j: (0, h)), blk, blk, blk] + cm_specs,
        scratch_shapes=[pltpu.VMEM((L, 1), F32), pltpu.VMEM((T, QH), F32), pltpu.VMEM((T, HD), F32)]
        + (comm.scratch if comm is not None else []),
        compiler_params=_params(("arbitrary", "arbitrary")),
    )(qm, kn, krr, vm, o, dcat, lse, *(comm.arrays if comm is not None else []))
    return res[:4], res[4:]


def _unrope_q(dqm, tabs_m):
    L, W = dqm.shape
    tr = _tile(L, 384)

    def body(d_ref, cm_ref, sa_ref, sb_ref, out_ref):
        cm, sa, sb = cm_ref[...], sa_ref[...], sb_ref[...]
        for h in range(HEADS):
            out_ref[:, h * QH:h * QH + HD] = (d_ref[:, h * QH:h * QH + HD] * ATT_SCALE).astype(BF16)
            out_ref[:, h * QH + HD:(h + 1) * QH] = _rope_mla_t(d_ref[:, h * QH + HD:(h + 1) * QH] * ATT_SCALE, cm, sa,
                                                               sb).astype(BF16)

    row = pl.BlockSpec((tr, W), lambda i: (i, 0))
    tab = pl.BlockSpec((tr, HD), lambda i: (i, 0))
    return pl.pallas_call(
        body, name="unrope_q", grid=(L // tr,), out_shape=jax.ShapeDtypeStruct((L, W), BF16),
        in_specs=[row, tab, tab, tab], out_specs=row, compiler_params=_params(("parallel",)),
    )(dqm, *tabs_m)


def _q_up(cqn, wuq_p, tabs_m):
    L = cqn.shape[0]
    tm = _tile(L, 704)

    def ep(acc, cm, sa, sb):
        acc = acc * Q_PRESCALE
        parts = []
        for h in range(HEADS):
            parts.append(acc[:, h * QH:h * QH + HD])
            parts.append(_rope_mla(acc[:, h * QH + HD:(h + 1) * QH], cm, sa, sb))
        return (jnp.concatenate(parts, axis=1),)

    tab = pl.BlockSpec((tm, HD), lambda i, j: (i, 0))
    return _mm("mla_q_up", (L // tm, 1), ("parallel", "parallel"), None,
               [cqn, wuq_p], [pl.BlockSpec((tm, Q_RANK), lambda i, j: (i, 0)),
                              pl.BlockSpec((HEADS * QH, Q_RANK), lambda i, j: (0, 0))],
               [(0, 1, 'nt', 0)], [(tm, HEADS * QH)], list(tabs_m), [tab] * 3, ep,
               [jax.ShapeDtypeStruct((L, HEADS * QH), BF16)], [pl.BlockSpec((tm, HEADS * QH), lambda i, j: (i, 0))])[0]


def _mix_out(cat, w_out, h_in, post, next_norm):
    L, K = cat.shape
    D = w_out.shape[1]
    tm, tk = _tile(L, 384), K
    row = pl.BlockSpec((tm, D), lambda i, k: (i, 0))
    vec = pl.BlockSpec((1, D), lambda i, k: (0, 0))
    return _mm("mix_out", (L // tm, K // tk), ("parallel", "arbitrary"), 1,
               [cat, w_out], [pl.BlockSpec((tm, tk), lambda i, k: (i, k)), pl.BlockSpec((tk, D), lambda i, k: (k, 0))],
               [(0, 1, 'nn', 0)], [(tm, D)], [h_in, post, next_norm], [row, vec, vec], _resnorm_epilogue(1.0, True),
               [jax.ShapeDtypeStruct((L, D), F32)] * 2 + [jax.ShapeDtypeStruct((L, D), BF16)], [row, row, row])


ADAM_BLOCK_ELEMS = 512 * 704


def _adam_math(w, g, m, v):
    m = ADAM_B1 * m + (1.0 - ADAM_B1) * g
    v = ADAM_B2 * v + (1.0 - ADAM_B2) * (g * g)
    m_hat = m / (1.0 - ADAM_B1 ** ADAM_STEP)
    v_hat = v / (1.0 - ADAM_B2 ** ADAM_STEP)
    delta = -ADAM_LR * (m_hat / (jnp.sqrt(v_hat) + ADAM_EPS) + ADAM_WD * w)
    return delta, m, v


def _adam(name, w, m, v, g_slots=None, g=None, after=None):
    R, C = w.shape
    tr, tc = _tile(R, max(16, ADAM_BLOCK_ELEMS // C // 16 * 16), 16), C
    if tr * tc > ADAM_BLOCK_ELEMS:
        tr, tc = R, _tile(C, max(128, ADAM_BLOCK_ELEMS // R // 128 * 128), 128)
    from_slots = g_slots is not None

    def body(w_ref, m_ref, v_ref, g_ref, *rest):
        go_ref, d_ref, mo_ref, vo_ref = rest[-4:]
        if from_slots:
            grad = g_ref[0].astype(F32)
            for s in range(1, N_DEV):
                grad = grad + g_ref[s].astype(F32)
        else:
            grad = g_ref[...]
        delta, mn, vn = _adam_math(w_ref[...], grad, m_ref[...], v_ref[...])
        go_ref[...] = grad
        d_ref[...] = delta
        mo_ref[...] = mn
        vo_ref[...] = vn

    row = pl.BlockSpec((tr, tc), lambda i, j: (i, j))
    gspec = pl.BlockSpec((N_DEV, tr, tc), lambda i, j: (0, i, j)) if from_slots else row
    order = [] if after is None else [after]
    return pl.pallas_call(
        body, name=name, grid=(R // tr, C // tc), out_shape=[jax.ShapeDtypeStruct((R, C), F32)] * 4,
        in_specs=[row, row, row, gspec] + [pl.BlockSpec(memory_space=pl.ANY)] * len(order), out_specs=[row] * 4,
        compiler_params=_params(("parallel", "parallel")),
    )(w, m, v, g_slots if from_slots else g, *order)


def _unblock(gathered):
    n, r, c = gathered.shape
    return jnp.transpose(gathered, (1, 0, 2)).reshape(r, n * c)


def _reblock(full, c):
    r = full.shape[0]
    return jnp.transpose(full[:, :N_DEV * c].reshape(r, N_DEV, c), (1, 0, 2))


def _step(x, target, w, mom, vel):
    S, D = x.shape[1], x.shape[2]
    L = S + BLK
    def sq(a, n):
        if a.ndim == 2:
            return a
        if n in TRANSPOSED:
            a = jnp.swapaxes(a, 1, 2)
        return a.reshape(a.shape[1:])

    def unsq(o, n):
        o = o.reshape((1,) + o.shape)
        return jnp.swapaxes(o, 1, 2) if n in TRANSPOSED else o

    p = {n: sq(w[n], n) for n in WEIGHTS if n != 'meta_tokens'}
    gather = lambda names: _Exchange([p[n].astype(BF16) for n in names], False)
    scatter = lambda blocks: _Exchange(blocks, True)
    in_s, uq_s = p['w_in'].shape[0], p['mla_w_uq'].shape[0]
    assert uq_s == HD + ROPE and N_DEV == HEADS, "a w_uq shard is one head's columns"
    tabs = _rope_tables(L)
    tabs_m = tabs[2:]
    lg = jnp.broadcast_to(jnp.log(1.0 - 2.0 ** (-5.0 - jnp.arange(HEADS, dtype=F32)))[:, None, None], (HEADS, 8, HD))
    R = {}

    wg1, meta = _exchange("gather_first", [p['ffn1_w_gate'].astype(BF16), w['meta_tokens']], False)
    h0 = jnp.concatenate([_unblock(meta), jnp.zeros((BLK - N_META, D), F32), x[0]], axis=0)
    a1 = _norm_fwd(h0, p['ffn1_pre_norm'])
    g1, (wu1,) = _ffn_gate(a1, wg1, comm=gather(['ffn1_w_up']))
    (u1, hid1), (wd1,) = _ffn_up_gated(a1, wu1, g1, comm=gather(['ffn1_w_down']))
    (f1, h1, um), (w_in_g,) = _ffn_down(hid1, wd1, h0, p['ffn1_post_norm'], next_norm=p['mix_pre_norm'],
                                        comm=gather(['w_in']))

    w_in = jnp.pad(w_in_g.reshape(N_DEV * in_s, D), ((0, D_INP - N_DEV * in_s), (0, 0)))
    proj, (uq_g, uk_g, uv_g, wout_g) = _mm_nt("mix_in", [(um, w_in)], BF16, tn_target=1664,
                                              comm=gather(['mla_w_uq', 'mla_w_uk', 'mla_w_uv', 'w_out']))
    wuq = jnp.pad(uq_g, ((0, 0), (0, QH - uq_s), (0, 0))).reshape(HEADS * QH, Q_RANK)
    wuk, wuv, w_out = _unblock(uk_g), _unblock(uv_g), wout_g.reshape(-1, D)
    qr, kr, vr, cqn, ckvn, krr = _prep(proj, tabs, p['mla_q_norm'], p['mla_kv_norm'])
    qm = _q_up(cqn, wuq, tabs_m)
    kn = _mm_nn("mla_k_up", ckvn, wuk, BF16)
    vm = _mm_nn("mla_v_up", ckvn, wuv, BF16)
    (o_mla, lse), (wg2, wu2) = _attn_fwd(qm, kn, krr, vm, comm=gather(['ffn2_w_gate', 'ffn2_w_up']))
    o_ret = _lin_attn("ret_fwd", qr, kr, vr, lg, False)
    ret = _post(o_ret, proj, p['ret_group_norm'])
    cat = jnp.concatenate([ret, o_mla], axis=1)
    m, h2, a2 = _mix_out(cat, w_out, h1, p['mix_post_norm'], p['ffn2_pre_norm'])

    (g2, u2, hid2), (wd2,) = _ffn_up(a2, wg2, wu2, comm=gather(['ffn2_w_down']))
    f2, h3 = _ffn_down(hid2, wd2, h2, p['ffn2_post_norm'])
    dh3, loss_blk = _loss(h3, target[0])

    dsmall = {}
    df2, dsmall['ffn2_post_norm'] = _norm_bwd(f2, p['ffn2_post_norm'], dh3, None, 0.5, BF16)
    dg2, du2 = _ffn_dhid(df2, wd2, g2, u2)
    dwd2 = _ffn_dwd(hid2, df2)
    (dwg2, dwu2), (R['ffn2_w_down'],) = _ffn_dwgu(a2, dg2, du2, comm=scatter([dwd2]))
    da2, (R['ffn2_w_gate'],) = _ffn_da(dg2, du2, wg2, wu2, comm=scatter([dwg2]))
    dh2, dsmall['ffn2_pre_norm'] = _norm_bwd(h2, p['ffn2_pre_norm'], da2, dh3, 1.0, F32)

    dm, dsmall['mix_post_norm'] = _norm_bwd(m, p['mix_post_norm'], dh2, None, 1.0, BF16)
    dcat = _mm_nt("mix_dcat", [(dm, w_out)], BF16)
    dwout = _mm_tn("mix_dwout", cat, [dm])[0]
    do_ret, drg, dsmall['ret_group_norm'] = _post_bwd(o_ret, proj, p['ret_group_norm'], dcat)
    dqr = _lin_attn("ret_dq", do_ret, vr, kr, lg, False, BF16)
    dkr = _lin_attn("ret_dk", vr, do_ret, qr, lg, True, BF16)
    dvr = _lin_attn("ret_dv", kr, qr, do_ret, lg, True, BF16)
    (dqm, dkn, dkr8, dvm), (R['ffn2_w_up'], R['w_out']) = _attn_bwd(
        qm, kn, krr, vm, o_mla, dcat, lse, comm=scatter([dwu2, dwout.reshape(N_DEV, -1, D)]))
    dqp = _unrope_q(dqm, tabs_m)
    dwuq = _mm_tn("mla_dwuq", dqp, [cqn])[0]
    dcqn = _mm_nn("mla_dcq", dqp, wuq, F32)
    dwuk, dwuv = _mm_tn("mla_dwukv", ckvn, [dkn, dvm])
    dckvn = _mm_nt("mla_dckv", [(dkn, wuk), (dvm, wuv)], F32)
    dproj, dsmall['mla_q_norm'], dsmall['mla_kv_norm'] = _prep_bwd(
        proj, dqr, dkr, dvr, drg, dcqn, dckvn, dkr8, tabs, p['mla_q_norm'], p['mla_kv_norm'])
    dwuq_b = dwuq.reshape(HEADS, QH, Q_RANK)[:, :uq_s]
    (dwin,), (R['mla_w_uq'], R['mla_w_uk'], R['mla_w_uv']) = _mm_tn(
        "mix_dwin", dproj, [um], comm=scatter([dwuq_b, _reblock(dwuk, p['mla_w_uk'].shape[1]),
                                               _reblock(dwuv, p['mla_w_uv'].shape[1])]))
    dwin_b = dwin[:N_DEV * in_s].reshape(N_DEV, in_s, D)
    half = D // 2
    dum, (r_win_a,) = _mm_nn("mix_du", dproj, w_in, F32, tn_target=512, comm=scatter([dwin_b[:, :, :half]]))
    dh1, dsmall['mix_pre_norm'] = _norm_bwd(h1, p['mix_pre_norm'], dum, dh2, 1.0, F32)

    df1, dsmall['ffn1_post_norm'] = _norm_bwd(f1, p['ffn1_post_norm'], dh1, None, 0.5, BF16)
    (dg1, du1), (r_win_b,) = _ffn_dhid(df1, wd1, g1, u1, comm=scatter([dwin_b[:, :, half:]]))
    R['w_in'] = jnp.concatenate([r_win_a, r_win_b], axis=2)
    dwd1 = _ffn_dwd(hid1, df1)
    (dwg1, dwu1), (R['ffn1_w_down'],) = _ffn_dwgu(a1, dg1, du1, comm=scatter([dwd1]))
    da1, (R['ffn1_w_gate'],) = _ffn_da(dg1, du1, wg1, wu1, comm=scatter([dwg1]))
    dh0, dsmall['ffn1_pre_norm'] = _norm_bwd(h0, p['ffn1_pre_norm'], da1, dh1, 1.0, F32)
    tail_sems_s, tail_sems_r, tail_src, tail_land, token = _scatter_start(dwu1)

    def slab(a):
        a = a.reshape(-1, 128)
        return jnp.pad(a, ((0, (-a.shape[0]) % 8), (0, 0)))

    slab_rows = lambda n: -(-(p[n].shape[-1] // 128) // 8) * 8
    packed = jnp.concatenate([slab(dsmall[n]) for n in SMALL] + [slab(dh0[:N_META]), loss_blk], axis=0)
    red = _allreduce_small(packed + token[0, 0])
    offs = sum(slab_rows(n) for n in SMALL)
    n_small = offs
    gmeta_full = red[offs:offs + N_META * D // 128].reshape(N_META, D)
    offs += N_META * D // 128
    loss = red[offs, 0]

    grad, delta, new_m, new_v = {}, {}, {}, {}
    meanwhile = []
    for n in BIG:
        if n == 'ffn1_w_up':
            continue
        outs = _adam("adam_" + n, p[n], sq(mom[n], n), sq(vel[n], n), g_slots=R[n], after=token)
        meanwhile.append(outs[0])
        grad[n], delta[n], new_m[n], new_v[n] = [unsq(o, n) for o in outs]
    pack = lambda d: jnp.concatenate([slab(d[n]) for n in SMALL], axis=0)
    outs = _adam("adam_small", pack(w), pack(mom), pack(vel), g=red[:n_small])
    meanwhile.append(outs[0])
    offs = 0
    for n in SMALL:
        r = p[n].shape[-1] // 128
        grad[n], delta[n], new_m[n], new_v[n] = [o[offs:offs + r].reshape(w[n].shape) for o in outs]
        offs += slab_rows(n)
    dev = 4 * lax.axis_index("x") + 2 * lax.axis_index("y") + lax.axis_index("c")
    mcols = w['meta_tokens'].shape[1]
    gmeta = lax.dynamic_slice(gmeta_full, (0, dev * mcols), (N_META, mcols))
    outs = _adam("adam_meta", w['meta_tokens'], mom['meta_tokens'], vel['meta_tokens'], g=gmeta)
    grad['meta_tokens'], delta['meta_tokens'], new_m['meta_tokens'], new_v['meta_tokens'] = outs
    meanwhile.append(outs[0])
    n = 'ffn1_w_up'
    slots = _scatter_wait(tail_sems_s, tail_sems_r, tail_src, tail_land, meanwhile)
    outs = _adam("adam_" + n, p[n], sq(mom[n], n), sq(vel[n], n), g_slots=slots)
    grad[n], delta[n], new_m[n], new_v[n] = [unsq(o, n) for o in outs]

    return (loss, dh0[BLK:][None], *[grad[n] for n in WEIGHTS], *[delta[n] for n in WEIGHTS],
            *[new_m[n] for n in WEIGHTS], *[new_v[n] for n in WEIGHTS])


def kernel(x, meta_tokens, ffn1_pre_norm, ffn1_w_gate, ffn1_w_up, ffn1_w_down, ffn1_post_norm, mix_pre_norm, w_in, ret_group_norm, mla_q_norm, mla_w_uq, mla_kv_norm, mla_w_uk, mla_w_uv, w_out, mix_post_norm, ffn2_pre_norm, ffn2_w_gate, ffn2_w_up, ffn2_w_down, ffn2_post_norm, loss_target, m_meta_tokens, m_ffn1_pre_norm, m_ffn1_w_gate, m_ffn1_w_up, m_ffn1_w_down, m_ffn1_post_norm, m_mix_pre_norm, m_w_in, m_ret_group_norm, m_mla_q_norm, m_mla_w_uq, m_mla_kv_norm, m_mla_w_uk, m_mla_w_uv, m_w_out, m_mix_post_norm, m_ffn2_pre_norm, m_ffn2_w_gate, m_ffn2_w_up, m_ffn2_w_down, m_ffn2_post_norm, v_meta_tokens, v_ffn1_pre_norm, v_ffn1_w_gate, v_ffn1_w_up, v_ffn1_w_down, v_ffn1_post_norm, v_mix_pre_norm, v_w_in, v_ret_group_norm, v_mla_q_norm, v_mla_w_uq, v_mla_kv_norm, v_mla_w_uk, v_mla_w_uv, v_w_out, v_mix_post_norm, v_ffn2_pre_norm, v_ffn2_w_gate, v_ffn2_w_up, v_ffn2_w_down, v_ffn2_post_norm):
    w = dict(zip(WEIGHTS, (meta_tokens, ffn1_pre_norm, ffn1_w_gate, ffn1_w_up, ffn1_w_down, ffn1_post_norm,
                           mix_pre_norm, w_in, ret_group_norm, mla_q_norm, mla_w_uq, mla_kv_norm, mla_w_uk, mla_w_uv,
                           w_out, mix_post_norm, ffn2_pre_norm, ffn2_w_gate, ffn2_w_up, ffn2_w_down, ffn2_post_norm)))
    mom = dict(zip(WEIGHTS, (m_meta_tokens, m_ffn1_pre_norm, m_ffn1_w_gate, m_ffn1_w_up, m_ffn1_w_down,
                             m_ffn1_post_norm, m_mix_pre_norm, m_w_in, m_ret_group_norm, m_mla_q_norm, m_mla_w_uq,
                             m_mla_kv_norm, m_mla_w_uk, m_mla_w_uv, m_w_out, m_mix_post_norm, m_ffn2_pre_norm,
                             m_ffn2_w_gate, m_ffn2_w_up, m_ffn2_w_down, m_ffn2_post_norm)))
    vel = dict(zip(WEIGHTS, (v_meta_tokens, v_ffn1_pre_norm, v_ffn1_w_gate, v_ffn1_w_up, v_ffn1_w_down,
                             v_ffn1_post_norm, v_mix_pre_norm, v_w_in, v_ret_group_norm, v_mla_q_norm, v_mla_w_uq,
                             v_mla_kv_norm, v_mla_w_uk, v_mla_w_uv, v_w_out, v_mix_post_norm, v_ffn2_pre_norm,
                             v_ffn2_w_gate, v_ffn2_w_up, v_ffn2_w_down, v_ffn2_post_norm)))
    return _step(x, loss_target, w, mom, vel)
```

```python
import functools
import math

import jax
import jax.numpy as jnp
from jax import lax
from jax.experimental import pallas as pl
from jax.experimental.pallas import tpu as pltpu

N_DEV = 8
N_META = 16
BLK = 128
HEADS = 8
HD = 128
ROPE = 64
Q_RANK = 512
KV_RANK = 256
QH = 2 * HD
D_INP = 4 * HEADS * HD + Q_RANK + KV_RANK + BLK
ROPE_THETA = 10000.0
EPS = 1e-6
ADAM_LR = 0.001
ADAM_B1 = 0.9
ADAM_B2 = 0.999
ADAM_EPS = 1e-08
ADAM_WD = 0.01
ADAM_STEP = 10
V7X_VMEM_LIMIT = 48 * 1024 * 1024
MESH = pl.DeviceIdType.MESH
F32 = jnp.float32
BF16 = jnp.bfloat16

WEIGHTS = ['meta_tokens', 'ffn1_pre_norm', 'ffn1_w_gate', 'ffn1_w_up', 'ffn1_w_down', 'ffn1_post_norm',
           'mix_pre_norm', 'w_in', 'ret_group_norm', 'mla_q_norm', 'mla_w_uq', 'mla_kv_norm', 'mla_w_uk',
           'mla_w_uv', 'w_out', 'mix_post_norm', 'ffn2_pre_norm', 'ffn2_w_gate', 'ffn2_w_up', 'ffn2_w_down',
           'ffn2_post_norm']
SMALL = ['ffn1_pre_norm', 'ffn1_post_norm', 'mix_pre_norm', 'ret_group_norm', 'mla_q_norm', 'mla_kv_norm',
         'mix_post_norm', 'ffn2_pre_norm', 'ffn2_post_norm']
TRANSPOSED = ('ffn1_w_gate', 'ffn1_w_up', 'ffn2_w_gate', 'ffn2_w_up', 'w_in', 'mla_w_uq')
BIG = ['ffn1_w_gate', 'ffn1_w_up', 'ffn1_w_down', 'w_in', 'mla_w_uq', 'mla_w_uk', 'mla_w_uv', 'w_out',
       'ffn2_w_gate', 'ffn2_w_up', 'ffn2_w_down']

_DIMS = {'nn': (((1,), (0,)), ((), ())), 'nt': (((1,), (1,)), ((), ())), 'tn': (((0,), (0,)), ((), ()))}


def _tile(n, target, mult=16):
    best = None
    for t in range(mult, min(n, target) + 1, mult):
        if n % t == 0:
            best = t
    return best if best is not None else n


def _params(sem):
    return pltpu.CompilerParams(dimension_semantics=sem, vmem_limit_bytes=V7X_VMEM_LIMIT)


def _dot(a, b, dims):
    return lax.dot_general(a, b, _DIMS[dims], preferred_element_type=F32)


def _sigmoid(x):
    return 0.5 * jnp.tanh(0.5 * x) + 0.5


def _me_and_peers():
    x, y, c = lax.axis_index("x"), lax.axis_index("y"), lax.axis_index("c")

    def peer(j):
        px = 1 - x if (j >> 2) & 1 else x
        py = 1 - y if (j >> 1) & 1 else y
        pc = 1 - c if j & 1 else c
        return (px, py, pc), 4 * px + 2 * py + pc

    return 4 * x + 2 * y + c, peer


class _Exchange:
    def __init__(self, arrays, per_peer):
        self.arrays = list(arrays)
        self.per_peer = per_peer
        self.n = len(self.arrays)
        self.out_shapes = [jax.ShapeDtypeStruct((N_DEV,) + tuple(a.shape[1:] if per_peer else a.shape), a.dtype)
                           for a in self.arrays]
        self.specs = [pl.BlockSpec(memory_space=pl.ANY)] * self.n
        self.scratch = [pltpu.SemaphoreType.DMA((7 * self.n,)), pltpu.SemaphoreType.DMA((7 * self.n,)),
                        pltpu.SemaphoreType.DMA((self.n,))]

    def _copies(self, src, dst, sems):
        send_sems, recv_sems, local_sems = sems
        me, peer = _me_and_peers()
        sib, _ = peer(1)
        local, sends, recvs, passes = [], {}, {}, {}
        for k in range(self.n):
            own = src[k].at[me] if self.per_peer else src[k]
            local.append(pltpu.make_async_copy(own, dst[k].at[me], local_sems.at[k]))
            for j in range(1, N_DEV):
                pid, pidx = peer(j)
                out = src[k].at[pidx] if self.per_peer else src[k]
                sem = dict(send_sem=send_sems.at[k * 7 + j - 1], recv_sem=recv_sems.at[k * 7 + j - 1])
                recvs[k, j] = pltpu.make_async_remote_copy(src_ref=out, dst_ref=dst[k].at[pidx], device_id=pid,
                                                           device_id_type=MESH, **sem)
                if self.per_peer or j in (1, 2, 4, 6):
                    sends[k, j] = pltpu.make_async_remote_copy(src_ref=out, dst_ref=dst[k].at[me], device_id=pid,
                                                               device_id_type=MESH, **sem)
                else:
                    _, origin = peer(j ^ 1)
                    passes[k, j ^ 1] = pltpu.make_async_remote_copy(
                        src_ref=dst[k].at[origin], dst_ref=dst[k].at[origin], device_id=sib, device_id_type=MESH, **sem)
        return local, sends, recvs, passes

    def start(self, src, dst, sems):
        local, sends, _, _ = self._copies(src, dst, sems)
        for cp in local + list(sends.values()):
            cp.start()

    def finish(self, src, dst, sems):
        local, sends, recvs, passes = self._copies(src, dst, sems)
        for key, cp in passes.items():
            recvs[key].wait_recv()
            cp.start()
        for key, cp in recvs.items():
            if key not in passes:
                cp.wait_recv()
        for cp in list(sends.values()) + list(passes.values()):
            cp.wait_send()
        for cp in local:
            cp.wait()


def _grid_edges(grid):
    first, last = None, None
    for a, n in enumerate(grid):
        f, l = pl.program_id(a) == 0, pl.program_id(a) == n - 1
        first = f if first is None else first & f
        last = l if last is None else last & l
    return first, last


def _exchange(name, arrays, per_peer):
    ex = _Exchange(arrays, per_peer)
    n = ex.n

    def body(*refs):
        ex.start(refs[:n], refs[n:2 * n], refs[2 * n:])
        ex.finish(refs[:n], refs[n:2 * n], refs[2 * n:])

    return pl.pallas_call(body, name=name, out_shape=ex.out_shapes, in_specs=ex.specs, out_specs=ex.specs,
                          scratch_shapes=ex.scratch)(*arrays)


def _scatter_start(name, blocks):
    n = len(blocks)

    def body(*refs):
        src, land = refs[:n], refs[n:2 * n]
        send_sems, recv_sems = refs[2 * n], refs[2 * n + 1]
        token, local_sems = refs[4 * n + 2], refs[4 * n + 3]
        me, peer = _me_and_peers()
        local = [pltpu.make_async_copy(src[k].at[me], land[k].at[me], local_sems.at[k]) for k in range(n)]
        for cp in local:
            cp.start()
        for k in range(n):
            for j in range(1, N_DEV):
                pid, pidx = peer(j)
                pltpu.make_async_remote_copy(src_ref=src[k].at[pidx], dst_ref=land[k].at[me],
                                             send_sem=send_sems.at[k * 7 + j - 1], recv_sem=recv_sems.at[k * 7 + j - 1],
                                             device_id=pid, device_id_type=MESH).start()
        for cp in local:
            cp.wait()
        token[...] = jnp.zeros_like(token)

    hbm = pl.BlockSpec(memory_space=pltpu.HBM)
    sem = pl.BlockSpec(memory_space=pltpu.SEMAPHORE)
    thru = [pltpu.HBM(b.shape, b.dtype) for b in blocks]
    res = pl.pallas_call(
        body, name=name,
        out_shape=(pltpu.SemaphoreType.DMA((7 * n,)), pltpu.SemaphoreType.DMA((7 * n,)), *thru, *thru,
                   jax.ShapeDtypeStruct((8, 128), F32)),
        in_specs=(hbm,) * (2 * n), out_specs=(sem, sem) + (hbm,) * (2 * n) + (pl.BlockSpec(memory_space=pltpu.VMEM),),
        input_output_aliases={k: 2 + k for k in range(2 * n)}, scratch_shapes=[pltpu.SemaphoreType.DMA((n,))],
        compiler_params=pltpu.CompilerParams(has_side_effects=pltpu.SideEffectType.DATAFLOW_SIDE_EFFECTING),
    )(*[pltpu.with_memory_space_constraint(b, pltpu.HBM) for b in blocks],
      *[pltpu.with_memory_space_constraint(lax.empty(b.shape, b.dtype), pltpu.HBM) for b in blocks])
    return dict(send=res[0], recv=res[1], src=list(res[2:2 + n]), land=list(res[2 + n:2 + 2 * n]), token=res[-1])


def _scatter_wait(name, started, after):
    sizes = [len(st['src']) for st in started]
    n_all = sum(sizes)

    def body(*refs):
        me, peer = _me_and_peers()
        at = 0
        for g, n in enumerate(sizes):
            src, land = refs[at:at + n], refs[at + n:at + 2 * n]
            send_sems, recv_sems = refs[at + 2 * n], refs[at + 2 * n + 1]
            at += 2 * n + 2
            for k in range(n):
                for j in range(1, N_DEV):
                    pid, pidx = peer(j)
                    cp = pltpu.make_async_remote_copy(
                        src_ref=src[k].at[pidx], dst_ref=land[k].at[pidx], send_sem=send_sems.at[k * 7 + j - 1],
                        recv_sem=recv_sems.at[k * 7 + j - 1], device_id=pid, device_id_type=MESH)
                    cp.wait_send()
                    cp.wait_recv()

    hbm = pl.BlockSpec(memory_space=pltpu.HBM)
    sem = pl.BlockSpec(memory_space=pltpu.SEMAPHORE)
    ops, specs, outs, alias = [], [], [], {}
    for st in started:
        for a in st['src'] + st['land']:
            alias[len(ops)] = len(outs)
            ops.append(a)
            specs.append(hbm)
            outs.append(pltpu.HBM(a.shape, a.dtype))
        ops += [st['send'], st['recv']]
        specs += [sem, sem]
    res = pl.pallas_call(
        body, name=name, out_shape=tuple(outs),
        in_specs=tuple(specs) + (pl.BlockSpec(memory_space=pl.ANY),) * len(after), out_specs=(hbm,) * len(outs),
        input_output_aliases=alias,
        compiler_params=pltpu.CompilerParams(has_side_effects=pltpu.SideEffectType.DATAFLOW_SIDE_EFFECTING),
    )(*ops, *after)
    landed, at = [], 0
    for n in sizes:
        landed += list(res[at + n:at + 2 * n])
        at += 2 * n
    return landed


def _allreduce_small(v):
    rows = v.shape[0]

    def body(v_ref, out_ref, buf, send_sems, recv_sems):
        me, peer = _me_and_peers()
        buf[pl.ds(me, 1)] = v_ref[...][None]
        sends = []
        for j in range(1, N_DEV):
            pid, _ = peer(j)
            cp = pltpu.make_async_remote_copy(src_ref=v_ref, dst_ref=buf.at[me], send_sem=send_sems.at[j - 1],
                                              recv_sem=recv_sems.at[j - 1], device_id=pid, device_id_type=MESH)
            cp.start()
            sends.append(cp)
        for j in range(1, N_DEV):
            pid, pidx = peer(j)
            pltpu.make_async_remote_copy(src_ref=v_ref, dst_ref=buf.at[pidx], send_sem=send_sems.at[j - 1],
                                         recv_sem=recv_sems.at[j - 1], device_id=pid,
                                         device_id_type=MESH).wait_recv()
        for cp in sends:
            cp.wait_send()
        acc = buf[0]
        for s in range(1, N_DEV):
            acc = acc + buf[s]
        out_ref[...] = acc

    vm = pl.BlockSpec(memory_space=pltpu.VMEM)
    return pl.pallas_call(
        body, name="allreduce_small", out_shape=jax.ShapeDtypeStruct(v.shape, F32),
        in_specs=[vm], out_specs=vm,
        scratch_shapes=[pltpu.VMEM((N_DEV, rows, 128), F32), pltpu.SemaphoreType.DMA((7,)),
                        pltpu.SemaphoreType.DMA((7,))],
    )(v)


def _mm(name, grid, sem, k_axis, ops, op_specs, pairs, acc_shapes, extras, extra_specs, epilogue, outs, out_specs,
        comm=None, after=()):
    n_op, n_ex, n_out = len(ops), len(extras), len(outs)
    nk = grid[k_axis] if k_axis is not None else 1
    n_acc = len(acc_shapes) if nk > 1 else 0
    n_cm = comm.n if comm is not None else 0
    after = list(after)
    assert not (after and comm is not None)

    def body(*refs):
        op_refs = refs[:n_op]
        ex_refs = refs[n_op:n_op + n_ex]
        n_in = n_op + n_ex + n_cm + len(after)
        out_refs = refs[n_in:n_in + n_out]
        acc_refs = refs[n_in + n_out + n_cm:n_in + n_out + n_cm + n_acc]
        if comm is not None:
            cm_refs = (refs[n_op + n_ex:n_in], refs[n_in + n_out:n_in + n_out + n_cm],
                       refs[n_in + n_out + n_cm + n_acc:])
            first, last = _grid_edges(grid)

            @pl.when(first)
            def _():
                comm.start(*cm_refs)

        def finish(vals):
            res = epilogue(*vals, *[e[...] for e in ex_refs])
            for o, r in zip(out_refs, res):
                o[...] = r.astype(o.dtype)

        if nk == 1:
            parts = [None] * len(acc_shapes)
            for li, ri, dims, ai in pairs:
                d = _dot(op_refs[li][...], op_refs[ri][...], dims)
                parts[ai] = d if parts[ai] is None else parts[ai] + d
            finish(parts)
        else:
            k = pl.program_id(k_axis)

            @pl.when(k == 0)
            def _():
                for a in acc_refs:
                    a[...] = jnp.zeros_like(a)

            for li, ri, dims, ai in pairs:
                acc_refs[ai][...] += _dot(op_refs[li][...], op_refs[ri][...], dims)

            @pl.when(k == nk - 1)
            def _():
                finish([a[...] for a in acc_refs])

        if comm is not None:
            @pl.when(last)
            def _():
                comm.finish(*cm_refs)

    scratch = [pltpu.VMEM(s, F32) for s in acc_shapes] if nk > 1 else []
    if comm is None:
        return pl.pallas_call(
            body, name=name, grid=grid, out_shape=outs,
            in_specs=list(op_specs) + list(extra_specs) + [pl.BlockSpec(memory_space=pl.ANY)] * len(after),
            out_specs=list(out_specs), scratch_shapes=scratch, compiler_params=_params(sem),
        )(*ops, *extras, *after)
    res = pl.pallas_call(
        body, name=name, grid=grid, out_shape=list(outs) + comm.out_shapes,
        in_specs=list(op_specs) + list(extra_specs) + comm.specs, out_specs=list(out_specs) + comm.specs,
        scratch_shapes=scratch + comm.scratch, compiler_params=_params(("arbitrary",) * len(grid)),
    )(*ops, *extras, *comm.arrays)
    return res[:n_out], res[n_out:]


def _with_comm(res, comm, pick):
    if comm is None:
        return pick(res)
    return pick(res[0]), res[1]


def _mm_nn(name, a, w, out_dtype, tm_target=704, tn_target=1664, epilogue=None, extras=(), extra_specs=(), comm=None,
           after=()):
    L, K = a.shape
    N = w.shape[1]
    tm, tn = _tile(L, tm_target), _tile(N, tn_target, 128)
    ep = epilogue if epilogue is not None else (lambda acc: (acc,))
    res = _mm(name, (L // tm, N // tn), ("parallel", "parallel"), None,
              [a, w], [pl.BlockSpec((tm, K), lambda i, j: (i, 0)), pl.BlockSpec((K, tn), lambda i, j: (0, j))],
              [(0, 1, 'nn', 0)], [(tm, tn)], list(extras), list(extra_specs), ep,
              [jax.ShapeDtypeStruct((L, N), out_dtype)], [pl.BlockSpec((tm, tn), lambda i, j: (i, j))], comm=comm,
              after=after)
    return _with_comm(res, comm, lambda o: o[0])


def _mm_nt(name, pairs_aw, out_dtype, tm_target=704, tn_target=512, comm=None, after=()):
    L = pairs_aw[0][0].shape[0]
    N = pairs_aw[0][1].shape[0]
    tm, tn = _tile(L, tm_target), _tile(N, tn_target, 128)
    ops, specs, pairs = [], [], []
    for t, (a, w) in enumerate(pairs_aw):
        K = a.shape[1]
        ops += [a, w]
        specs += [pl.BlockSpec((tm, K), lambda i, j: (i, 0)), pl.BlockSpec((tn, K), lambda i, j: (j, 0))]
        pairs.append((2 * t, 2 * t + 1, 'nt', 0))
    res = _mm(name, (L // tm, N // tn), ("parallel", "parallel"), None, ops, specs, pairs, [(tm, tn)], [], [],
              lambda acc: (acc,), [jax.ShapeDtypeStruct((L, N), out_dtype)],
              [pl.BlockSpec((tm, tn), lambda i, j: (i, j))], comm=comm, after=after)
    return _with_comm(res, comm, lambda o: o[0])


def _mm_tn(name, a, bs, out_dtype=BF16, tk_target=1408, tn_target=1664, tm_target=2048, comm=None):
    L, M = a.shape
    N = bs[0].shape[1]
    tk, tn, tm = _tile(L, tk_target), _tile(N, tn_target, 128), _tile(M, tm_target, 128)
    nb = len(bs)
    ops = [a] + list(bs)
    specs = [pl.BlockSpec((tk, tm), lambda i, j, k: (k, i))] + [pl.BlockSpec((tk, tn), lambda i, j, k: (k, j))] * nb
    res = _mm(name, (M // tm, N // tn, L // tk), ("parallel", "parallel", "arbitrary"), 2, ops, specs,
              [(0, 1 + t, 'tn', t) for t in range(nb)], [(tm, tn)] * nb, [], [], lambda *acc: acc,
              [jax.ShapeDtypeStruct((M, N), out_dtype)] * nb,
              [pl.BlockSpec((tm, tn), lambda i, j, k: (i, j))] * nb, comm=comm)
    return _with_comm(res, comm, lambda o: o)


def _norm_fwd(x, w):
    L, D = x.shape
    tr = _tile(L, 512)

    def body(x_ref, w_ref, y_ref):
        v = x_ref[...]
        r = lax.rsqrt(jnp.mean(v * v, axis=-1, keepdims=True) + EPS)
        y_ref[...] = (v * r * w_ref[...]).astype(y_ref.dtype)

    return pl.pallas_call(
        body, name="norm_fwd", grid=(L // tr,), out_shape=jax.ShapeDtypeStruct((L, D), BF16),
        in_specs=[pl.BlockSpec((tr, D), lambda i: (i, 0)), pl.BlockSpec((1, D), lambda i: (0, 0))],
        out_specs=pl.BlockSpec((tr, D), lambda i: (i, 0)), compiler_params=_params(("parallel",)),
    )(x, w)


def _norm_bwd_math(x, w, dy):
    r = lax.rsqrt(jnp.mean(x * x, axis=-1, keepdims=True) + EPS)
    gy = dy * w
    dx = r * (gy - x * (r * r) * jnp.mean(gy * x, axis=-1, keepdims=True))
    dw = jnp.sum(dy * x * r, axis=0, keepdims=True)
    return dx, dw


def _norm_bwd(x, w, dy, res, scale, out_dtype):
    L, D = x.shape
    tr = _tile(L, 384)
    has_res = res is not None

    def body(*refs):
        x_ref, w_ref, dy_ref = refs[:3]
        res_ref = refs[3] if has_res else None
        dx_ref, dw_ref = refs[-2:]
        dx, dw = _norm_bwd_math(x_ref[...], w_ref[...], dy_ref[...].astype(F32))
        dx = scale * dx
        if has_res:
            dx = dx + res_ref[...]
        dx_ref[...] = dx.astype(dx_ref.dtype)

        @pl.when(pl.program_id(0) == 0)
        def _():
            dw_ref[...] = jnp.zeros_like(dw_ref)

        dw_ref[...] += scale * dw

    row = pl.BlockSpec((tr, D), lambda i: (i, 0))
    vec = pl.BlockSpec((1, D), lambda i: (0, 0))
    return pl.pallas_call(
        body, name="norm_bwd", grid=(L // tr,),
        out_shape=[jax.ShapeDtypeStruct((L, D), out_dtype), jax.ShapeDtypeStruct((1, D), F32)],
        in_specs=[row, vec, row] + ([row] if has_res else []), out_specs=[row, vec],
        compiler_params=_params(("arbitrary",)),
    )(*([x, w, dy] + ([res] if has_res else [])))


def _loss(h, target):
    L, D = h.shape

    def body(h_ref, t_ref, dh_ref, loss_ref):
        i = pl.program_id(0)

        @pl.when(i == 0)
        def _():
            dh_ref[...] = jnp.zeros_like(dh_ref)
            loss_ref[...] = jnp.zeros_like(loss_ref)

        @pl.when(i > 0)
        def _():
            diff = h_ref[...] - t_ref[...]
            dh_ref[...] = diff * (1.0 / D)
            loss_ref[...] += 0.5 * jnp.sum(diff * diff) * (1.0 / D)

    return pl.pallas_call(
        body, name="loss", grid=(L // BLK,),
        out_shape=[jax.ShapeDtypeStruct((L, D), F32), jax.ShapeDtypeStruct((8, 128), F32)],
        in_specs=[pl.BlockSpec((BLK, D), lambda i: (i, 0)),
                  pl.BlockSpec((BLK, D), lambda i: (jnp.maximum(i - 1, 0), 0))],
        out_specs=[pl.BlockSpec((BLK, D), lambda i: (i, 0)), pl.BlockSpec((8, 128), lambda i: (0, 0))],
        compiler_params=_params(("arbitrary",)),
    )(h, target)


def _ffn_up(a, wg, wu, comm=None):
    L, D = a.shape
    F = wg.shape[1]
    tm = _tile(L, 704)

    def ep(g, u):
        return g, u, g * _sigmoid(g) * u

    hspec = pl.BlockSpec((None, tm, F), lambda i, j: (j, i, 0))
    wspec = pl.BlockSpec((None, F, D), lambda i, j: (j, 0, 0))
    res = _mm("ffn_up", (L // tm, N_DEV), ("parallel", "parallel"), None,
              [a, wg, wu], [pl.BlockSpec((tm, D), lambda i, j: (i, 0)), wspec, wspec],
              [(0, 1, 'nt', 0), (0, 2, 'nt', 1)], [(tm, F)] * 2, [], [], ep,
              [jax.ShapeDtypeStruct((N_DEV, L, F), BF16)] * 3, [hspec] * 3, comm=comm)
    return _with_comm(res, comm, lambda o: o)


def _ffn_gate(a, wg, comm=None):
    L, D = a.shape
    F = wg.shape[1]
    tm = _tile(L, 704)
    res = _mm("ffn_gate", (L // tm, N_DEV), ("parallel", "parallel"), None,
              [a, wg], [pl.BlockSpec((tm, D), lambda i, j: (i, 0)), pl.BlockSpec((None, F, D), lambda i, j: (j, 0, 0))],
              [(0, 1, 'nt', 0)], [(tm, F)], [], [], lambda g: (g,),
              [jax.ShapeDtypeStruct((N_DEV, L, F), BF16)], [pl.BlockSpec((None, tm, F), lambda i, j: (j, i, 0))],
              comm=comm)
    return _with_comm(res, comm, lambda o: o[0])


def _ffn_up_gated(a, wu, g, comm=None):
    L, D = a.shape
    F = wu.shape[1]
    tm = _tile(L, 704)

    def ep(u, g_):
        g32 = g_.astype(F32)
        return u, g32 * _sigmoid(g32) * u

    hspec = pl.BlockSpec((None, tm, F), lambda i, j: (j, i, 0))
    res = _mm("ffn_up_gated", (L // tm, N_DEV), ("parallel", "parallel"), None,
              [a, wu], [pl.BlockSpec((tm, D), lambda i, j: (i, 0)), pl.BlockSpec((None, F, D), lambda i, j: (j, 0, 0))],
              [(0, 1, 'nt', 0)], [(tm, F)], [g], [hspec], ep,
              [jax.ShapeDtypeStruct((N_DEV, L, F), BF16)] * 2, [hspec, hspec], comm=comm)
    return _with_comm(res, comm, lambda o: o)


def _resnorm_epilogue(scale, with_next):
    def ep(acc, h, w, *w_next):
        r = lax.rsqrt(jnp.mean(acc * acc, axis=-1, keepdims=True) + EPS)
        h_out = h + scale * (acc * r * w)
        if not with_next:
            return acc, h_out
        r_next = lax.rsqrt(jnp.mean(h_out * h_out, axis=-1, keepdims=True) + EPS)
        return acc, h_out, h_out * r_next * w_next[0]
    return ep


def _ffn_down(hid, wd, h_in, post, next_norm=None, comm=None):
    _, L, F = hid.shape
    D = wd.shape[2]
    tm = _tile(L, 528)
    row = pl.BlockSpec((tm, D), lambda i, j: (i, 0))
    vec = pl.BlockSpec((1, D), lambda i, j: (0, 0))
    nxt = [] if next_norm is None else [next_norm]
    res = _mm("ffn_down", (L // tm, N_DEV), ("parallel", "arbitrary"), 1,
              [hid, wd], [pl.BlockSpec((None, tm, F), lambda i, j: (j, i, 0)),
                          pl.BlockSpec((None, F, D), lambda i, j: (j, 0, 0))],
              [(0, 1, 'nn', 0)], [(tm, D)], [h_in, post] + nxt, [row, vec] + [vec] * len(nxt),
              _resnorm_epilogue(0.5, bool(nxt)),
              [jax.ShapeDtypeStruct((L, D), F32)] * 2 + [jax.ShapeDtypeStruct((L, D), BF16)] * len(nxt),
              [row] * (2 + len(nxt)), comm=comm)
    return _with_comm(res, comm, lambda o: o)


def _ffn_dhid(df, wd, g, u, comm=None):
    L, D = df.shape
    F = wd.shape[1]
    tm = _tile(L, 704)

    def ep(dhid, g_, u_):
        g32, u32 = g_.astype(F32), u_.astype(F32)
        sg = _sigmoid(g32)
        return dhid * u32 * sg * (1.0 + g32 * (1.0 - sg)), dhid * g32 * sg

    hspec = pl.BlockSpec((None, tm, F), lambda i, j: (j, i, 0))
    res = _mm("ffn_dhid", (L // tm, N_DEV), ("parallel", "parallel"), None,
              [df, wd], [pl.BlockSpec((tm, D), lambda i, j: (i, 0)),
                         pl.BlockSpec((None, F, D), lambda i, j: (j, 0, 0))],
              [(0, 1, 'nt', 0)], [(tm, F)], [g, u], [hspec, hspec], ep,
              [jax.ShapeDtypeStruct((N_DEV, L, F), BF16)] * 2, [hspec, hspec], comm=comm)
    return _with_comm(res, comm, lambda o: o)


def _ffn_dwd(hid, df, comm=None):
    _, L, F = hid.shape
    D = df.shape[1]
    tk = _tile(L, 1408)
    res = _mm("ffn_dwd", (N_DEV, L // tk), ("parallel", "arbitrary"), 1,
              [hid, df], [pl.BlockSpec((None, tk, F), lambda j, k: (j, k, 0)),
                          pl.BlockSpec((tk, D), lambda j, k: (k, 0))],
              [(0, 1, 'tn', 0)], [(F, D)], [], [], lambda acc: (acc,),
              [jax.ShapeDtypeStruct((N_DEV, F, D), BF16)], [pl.BlockSpec((None, F, D), lambda j, k: (j, 0, 0))],
              comm=comm)
    return _with_comm(res, comm, lambda o: o[0])


def _ffn_dwgu(a, dg, du, comm=None, after=()):
    L, D = a.shape
    F = dg.shape[2]
    tk = _tile(L, 1408)
    hspec = pl.BlockSpec((None, tk, F), lambda j, k: (j, k, 0))
    wspec = pl.BlockSpec((None, F, D), lambda j, k: (j, 0, 0))
    res = _mm("ffn_dwgu", (N_DEV, L // tk), ("parallel", "arbitrary"), 1,
              [a, dg, du], [pl.BlockSpec((tk, D), lambda j, k: (k, 0)), hspec, hspec],
              [(1, 0, 'tn', 0), (2, 0, 'tn', 1)], [(F, D)] * 2, [], [], lambda *acc: acc,
              [jax.ShapeDtypeStruct((N_DEV, F, D), BF16)] * 2, [wspec, wspec], comm=comm, after=after)
    return _with_comm(res, comm, lambda o: o)


def _ffn_da(dg, du, wg, wu, comm=None, after=()):
    _, L, F = dg.shape
    D = wg.shape[2]
    tm = _tile(L, 704)
    hspec = pl.BlockSpec((None, tm, F), lambda i, j: (j, i, 0))
    wspec = pl.BlockSpec((None, F, D), lambda i, j: (j, 0, 0))
    row = pl.BlockSpec((tm, D), lambda i, j: (i, 0))
    res = _mm("ffn_da", (L // tm, N_DEV), ("parallel", "arbitrary"), 1,
              [dg, du, wg, wu], [hspec, hspec, wspec, wspec],
              [(0, 2, 'nn', 0), (1, 3, 'nn', 0)], [(tm, D)], [], [], lambda acc: (acc,),
              [jax.ShapeDtypeStruct((L, D), F32)], [row], comm=comm, after=after)
    return _with_comm(res, comm, lambda o: o[0])


def _rope_tables(L):
    rows = jnp.arange(L, dtype=F32)
    pos = jnp.where(rows < BLK, rows, rows - (BLK - N_META))
    inv_r = ROPE_THETA ** (-jnp.arange(0, HD, 2, dtype=F32) / HD)
    ang_r = pos[:, None] * inv_r[None, :]
    cr = jnp.concatenate([jnp.cos(ang_r), jnp.cos(ang_r)], axis=1)
    sr = jnp.concatenate([-jnp.sin(ang_r), jnp.sin(ang_r)], axis=1)
    inv_m = ROPE_THETA ** (-jnp.arange(0, ROPE, 2, dtype=F32) / ROPE)
    ang_m = pos[:, None] * inv_m[None, :]
    z32 = jnp.zeros((L, ROPE // 2), F32)
    z64 = jnp.zeros((L, HD - ROPE), F32)
    cm = jnp.concatenate([jnp.cos(ang_m), jnp.cos(ang_m), z64], axis=1)
    sa = jnp.concatenate([-jnp.sin(ang_m), z32, z64], axis=1)
    sb = jnp.concatenate([z32, jnp.sin(ang_m), z64], axis=1)
    return cr, sr, cm, sa, sb


def _rope_ret(x, cr, sr):
    return x * cr + pltpu.roll(x, HD // 2, 1) * sr


def _rope_ret_t(d, cr, sr):
    return d * cr + pltpu.roll(d * sr, HD // 2, 1)


def _rope_mla(x, cm, sa, sb):
    return x * cm + pltpu.roll(x, HD - ROPE // 2, 1) * sa + pltpu.roll(x, ROPE // 2, 1) * sb


def _rope_mla_t(d, cm, sa, sb):
    return d * cm + pltpu.roll(d * sa, ROPE // 2, 1) + pltpu.roll(d * sb, HD - ROPE // 2, 1)


C_RQ, C_RK, C_RV, C_RG = 0, HEADS * HD, 2 * HEADS * HD, 3 * HEADS * HD
C_CQ = 4 * HEADS * HD
C_CKV = C_CQ + Q_RANK
C_KR = C_CKV + KV_RANK
RET_K_SCALE = HD ** -0.5


def _prep(proj, tabs, qn, kvn):
    L = proj.shape[0]
    tr = _tile(L, 256)
    W = HEADS * HD

    def body(p_ref, cr_ref, sr_ref, cm_ref, sa_ref, sb_ref, qn_ref, kvn_ref, q_ref, k_ref, v_ref, cq_ref, ckv_ref,
             kr_ref):
        cr, sr = cr_ref[...], sr_ref[...]
        for h in range(HEADS):
            sl = slice(h * HD, (h + 1) * HD)
            q_ref[:, sl] = _rope_ret(p_ref[:, C_RQ + h * HD:C_RQ + (h + 1) * HD].astype(F32), cr, sr).astype(BF16)
            k_ref[:, sl] = (_rope_ret(p_ref[:, C_RK + h * HD:C_RK + (h + 1) * HD].astype(F32), cr, sr)
                            * RET_K_SCALE).astype(BF16)
        v_ref[...] = p_ref[:, C_RV:C_RV + W].astype(BF16)
        cq = p_ref[:, C_CQ:C_CQ + Q_RANK].astype(F32)
        cq_ref[...] = (cq * lax.rsqrt(jnp.mean(cq * cq, axis=-1, keepdims=True) + EPS) * qn_ref[...]).astype(BF16)
        ckv = p_ref[:, C_CKV:C_CKV + KV_RANK].astype(F32)
        ckv_ref[...] = (ckv * lax.rsqrt(jnp.mean(ckv * ckv, axis=-1, keepdims=True) + EPS)
                        * kvn_ref[...]).astype(BF16)
        kr_ref[...] = _rope_mla(p_ref[:, C_KR:C_KR + HD].astype(F32), cm_ref[...], sa_ref[...], sb_ref[...]).astype(BF16)

    row = lambda w: pl.BlockSpec((tr, w), lambda i: (i, 0))
    vec = lambda w: pl.BlockSpec((1, w), lambda i: (0, 0))
    return pl.pallas_call(
        body, name="mix_prep", grid=(L // tr,),
        out_shape=[jax.ShapeDtypeStruct((L, W), BF16)] * 3 + [jax.ShapeDtypeStruct((L, Q_RANK), BF16),
                                                              jax.ShapeDtypeStruct((L, KV_RANK), BF16),
                                                              jax.ShapeDtypeStruct((L, HD), BF16)],
        in_specs=[row(D_INP)] + [row(HD)] * 5 + [vec(Q_RANK), vec(KV_RANK)],
        out_specs=[row(W)] * 3 + [row(Q_RANK), row(KV_RANK), row(HD)],
        compiler_params=_params(("parallel",)),
    )(proj, *tabs, qn, kvn)


def _prep_bwd(proj, dq, dk, dv, drg, dcqn, dckvn, dkr8, tabs, qn, kvn):
    L = proj.shape[0]
    tr = _tile(L, 192)
    W = HEADS * HD

    def body(p_ref, dq_ref, dk_ref, dv_ref, drg_ref, dcq_ref, dckv_ref, dkr_ref, cr_ref, sr_ref, cm_ref, sa_ref,
             sb_ref, qn_ref, kvn_ref, dp_ref, dqn_ref, dkvn_ref):
        cr, sr = cr_ref[...], sr_ref[...]
        dkr = None
        for h in range(HEADS):
            sl = slice(h * HD, (h + 1) * HD)
            dp_ref[:, C_RQ + h * HD:C_RQ + (h + 1) * HD] = _rope_ret_t(dq_ref[:, sl].astype(F32), cr, sr).astype(BF16)
            dp_ref[:, C_RK + h * HD:C_RK + (h + 1) * HD] = (_rope_ret_t(dk_ref[:, sl].astype(F32), cr, sr)
                                                            * RET_K_SCALE).astype(BF16)
            part = dkr_ref[:, sl].astype(F32)
            dkr = part if dkr is None else dkr + part
        dp_ref[:, C_RV:C_RV + W] = dv_ref[...].astype(BF16)
        dp_ref[:, C_RG:C_RG + W] = drg_ref[...].astype(BF16)
        dcq, dqn = _norm_bwd_math(p_ref[:, C_CQ:C_CQ + Q_RANK].astype(F32), qn_ref[...], dcq_ref[...])
        dp_ref[:, C_CQ:C_CQ + Q_RANK] = dcq.astype(BF16)
        dckv, dkvn = _norm_bwd_math(p_ref[:, C_CKV:C_CKV + KV_RANK].astype(F32), kvn_ref[...], dckv_ref[...])
        dp_ref[:, C_CKV:C_CKV + KV_RANK] = dckv.astype(BF16)
        dp_ref[:, C_KR:C_KR + HD] = _rope_mla_t(dkr, cm_ref[...], sa_ref[...], sb_ref[...]).astype(BF16)

        @pl.when(pl.program_id(0) == 0)
        def _():
            dqn_ref[...] = jnp.zeros_like(dqn_ref)
            dkvn_ref[...] = jnp.zeros_like(dkvn_ref)

        dqn_ref[...] += dqn
        dkvn_ref[...] += dkvn

    row = lambda w: pl.BlockSpec((tr, w), lambda i: (i, 0))
    vec = lambda w: pl.BlockSpec((1, w), lambda i: (0, 0))
    return pl.pallas_call(
        body, name="mix_prep_bwd", grid=(L // tr,),
        out_shape=[jax.ShapeDtypeStruct((L, D_INP), BF16), jax.ShapeDtypeStruct((1, Q_RANK), F32),
                   jax.ShapeDtypeStruct((1, KV_RANK), F32)],
        in_specs=[row(D_INP)] + [row(W)] * 4 + [row(Q_RANK), row(KV_RANK), row(W)] + [row(HD)] * 5
                 + [vec(Q_RANK), vec(KV_RANK)],
        out_specs=[row(D_INP), vec(Q_RANK), vec(KV_RANK)],
        compiler_params=_params(("arbitrary",)),
    )(proj, dq, dk, dv, drg, dcqn, dckvn, dkr8, *tabs, qn, kvn)


def _post(o_ret, proj, gn):
    L, W = o_ret.shape
    tr = _tile(L, 384)

    def body(o_ref, rg_ref, gn_ref, out_ref):
        for h in range(HEADS):
            sl = slice(h * HD, (h + 1) * HD)
            o = o_ref[:, sl]
            rg = rg_ref[:, sl].astype(F32)
            n = o * lax.rsqrt(jnp.mean(o * o, axis=-1, keepdims=True) + EPS)
            out_ref[:, sl] = (n * gn_ref[:, sl] * (rg * _sigmoid(rg))).astype(BF16)

    row = pl.BlockSpec((tr, W), lambda i: (i, 0))
    return pl.pallas_call(
        body, name="ret_post", grid=(L // tr,), out_shape=jax.ShapeDtypeStruct((L, W), BF16),
        in_specs=[row, pl.BlockSpec((tr, W), lambda i: (i, C_RG // W)), pl.BlockSpec((1, W), lambda i: (0, 0))],
        out_specs=row, compiler_params=_params(("parallel",)),
    )(o_ret, proj, gn)


def _post_bwd(o_ret, proj, gn, dcat):
    L, W = o_ret.shape
    tr = _tile(L, 384)

    def body(o_ref, rg_ref, gn_ref, d_ref, do_ref, drg_ref, dgn_ref):
        @pl.when(pl.program_id(0) == 0)
        def _():
            dgn_ref[...] = jnp.zeros_like(dgn_ref)

        for h in range(HEADS):
            sl = slice(h * HD, (h + 1) * HD)
            o = o_ref[:, sl]
            rg = rg_ref[:, sl].astype(F32)
            d = d_ref[:, sl].astype(F32)
            gw = gn_ref[:, sl]
            r = lax.rsqrt(jnp.mean(o * o, axis=-1, keepdims=True) + EPS)
            n = o * r
            sg = _sigmoid(rg)
            si = rg * sg
            dn = d * gw * si
            dgn_ref[:, sl] += jnp.sum(d * n * si, axis=0, keepdims=True)
            drg_ref[:, sl] = (d * n * gw * sg * (1.0 + rg * (1.0 - sg))).astype(drg_ref.dtype)
            do_ref[:, sl] = (r * (dn - o * (r * r) * jnp.mean(dn * o, axis=-1, keepdims=True))).astype(BF16)

    row = pl.BlockSpec((tr, W), lambda i: (i, 0))
    vec = pl.BlockSpec((1, W), lambda i: (0, 0))
    return pl.pallas_call(
        body, name="ret_post_bwd", grid=(L // tr,),
        out_shape=[jax.ShapeDtypeStruct((L, W), BF16), jax.ShapeDtypeStruct((L, W), BF16),
                   jax.ShapeDtypeStruct((1, W), F32)],
        in_specs=[row, pl.BlockSpec((tr, W), lambda i: (i, C_RG // W)), vec, row],
        out_specs=[row, row, vec], compiler_params=_params(("arbitrary",)),
    )(o_ret, proj, gn, dcat)


RET_HEADS_PER_STEP = 4


def _lin_attn(name, q, k, v, lg, reverse, out_dtype=F32):
    L, W = q.shape
    nc = L // BLK - 1
    G = RET_HEADS_PER_STEP

    def body(q_ref, k_ref, v_ref, lg_ref, o_ref, s_ref):
        n = lax.broadcasted_iota(jnp.int32, (BLK, BLK), 0).astype(F32)
        m = lax.broadcasted_iota(jnp.int32, (BLK, BLK), 1).astype(F32)
        dist = (m - n) if reverse else (n - m)
        consts = []
        for g in range(G):
            lgv = lg_ref[g, 0:1, :]
            dmask = jnp.where(dist >= 0, jnp.exp(lgv * jnp.maximum(dist, 0.0)), 0.0)
            c = dict(dmask=dmask, dmask0=jnp.where((n < N_META) & (m < N_META), dmask, 0.0),
                     gl=jnp.exp(lgv * float(BLK)))
            if reverse:
                c.update(inter=jnp.exp(lgv * (float(BLK) - n)), upd=jnp.exp(lgv * n),
                         inter0=jnp.where(n < N_META, jnp.exp(lgv * jnp.maximum(float(N_META) - n, 0.0)), 0.0))
            else:
                c.update(inter=jnp.exp(lgv * (n + 1.0)), upd=jnp.exp(lgv * (float(BLK) - 1.0 - n)),
                         upd0=jnp.where(n < N_META, jnp.exp(lgv * jnp.maximum(float(N_META) - 1.0 - n, 0.0)), 0.0))
            consts.append(c)

        def chunk(c):
            rows = pl.ds(pl.multiple_of(c * BLK, BLK), BLK)
            state = [s_ref[g] for g in range(G)]
            outs, new_state = [], []
            for g in range(G):
                cols = slice(g * HD, (g + 1) * HD)
                cg = consts[g]
                qc, kc, vc = q_ref[rows, cols], k_ref[rows, cols], v_ref[rows, cols]
                a = _dot(qc, kc, 'nt') * cg['dmask']
                outs.append(_dot(a.astype(BF16), vc, 'nn') + _dot(qc, state[g].astype(BF16), 'nn') * cg['inter'])
                new_state.append(state[g] * cg['gl'] + _dot((kc.astype(F32) * cg['upd']).astype(BF16), vc, 'tn'))
            for g in range(G):
                o_ref[rows, g * HD:(g + 1) * HD] = outs[g].astype(o_ref.dtype)
                s_ref[g] = new_state[g]

        def first_chunk(with_state):
            for g in range(G):
                cols = slice(g * HD, (g + 1) * HD)
                cg = consts[g]
                q0, k0, v0 = q_ref[0:BLK, cols], k_ref[0:BLK, cols], v_ref[0:BLK, cols]
                o0 = _dot((_dot(q0, k0, 'nt') * cg['dmask0']).astype(BF16), v0, 'nn')
                if with_state:
                    o0 = o0 + _dot(q0, s_ref[g].astype(BF16), 'nn') * cg['inter0']
                else:
                    s_ref[g] = _dot((k0.astype(F32) * cg['upd0']).astype(BF16), v0, 'tn')
                o_ref[0:BLK, cols] = o0.astype(o_ref.dtype)

        if reverse:
            s_ref[...] = jnp.zeros_like(s_ref)

            def step(t, carry):
                chunk(nc - t)
                return carry

            lax.fori_loop(0, nc, step, 0)
            first_chunk(True)
        else:
            first_chunk(False)

            def step(t, carry):
                chunk(t + 1)
                return carry

            lax.fori_loop(0, nc, step, 0)

    col = pl.BlockSpec((L, G * HD), lambda h: (0, h))
    return pl.pallas_call(
        body, name=name, grid=(HEADS // G,), out_shape=jax.ShapeDtypeStruct((L, W), out_dtype),
        in_specs=[col, col, col, pl.BlockSpec((G, 8, HD), lambda h: (h, 0, 0))], out_specs=col,
        scratch_shapes=[pltpu.VMEM((G, HD, HD), F32)], compiler_params=_params(("parallel",)),
    )(q, k, v, lg)


ATT_SCALE = (HD + ROPE) ** -0.5
LOG2E = 1.4426950408889634
Q_PRESCALE = ATT_SCALE * LOG2E
NEG = -1e30


ATT_TILE = 384
ATT_HEADS_PER_STEP = 2


def _att_valid(nq, nk, row0, col0):
    r = lax.broadcasted_iota(jnp.int32, (nq, nk), 0) + row0
    c = lax.broadcasted_iota(jnp.int32, (nq, nk), 1) + col0
    return (c <= r) & ((c < N_META) | (c >= BLK))


def _attn_fwd(qm, kn, krr, vm, comm=None):
    L = qm.shape[0]
    W = HEADS * HD
    T = _tile(L, ATT_TILE, BLK)
    nb = L // T
    G = ATT_HEADS_PER_STEP
    n_cm = comm.n if comm is not None else 0

    def body(*refs):
        q_ref, kn_ref, kr_ref, v_ref = refs[:4]
        o_ref, lse_ref = refs[4 + n_cm:6 + n_cm]
        m_sc, l_sc, acc_sc = refs[6 + 2 * n_cm:9 + 2 * n_cm]
        if comm is not None:
            cm_refs = (refs[4:4 + n_cm], refs[6 + n_cm:6 + 2 * n_cm], refs[9 + 2 * n_cm:])
            first, last = _grid_edges((HEADS // G, nb))

            @pl.when(first)
            def _():
                comm.start(*cm_refs)

        i = pl.program_id(1)
        m_sc[...] = jnp.full_like(m_sc, NEG)
        l_sc[...] = jnp.zeros_like(l_sc)
        acc_sc[...] = jnp.zeros_like(acc_sc)

        def tile(j, masked):
            rows = pl.ds(pl.multiple_of(j * T, T), T)
            kr = kr_ref[rows, :]
            valid = _att_valid(T, T, i * T, j * T) if masked else None
            m_prev = [m_sc[g] for g in range(G)]
            l_prev = [l_sc[g] for g in range(G)]
            acc_prev = [acc_sc[g] for g in range(G)]
            m_new, l_new, acc_new = [], [], []
            for g in range(G):
                k = jnp.concatenate([kn_ref[rows, g * HD:(g + 1) * HD], kr], axis=1)
                s = _dot(q_ref[:, g * QH:(g + 1) * QH], k, 'nt')
                if masked:
                    s = jnp.where(valid, s, NEG)
                m_new.append(jnp.maximum(m_prev[g], jnp.max(s, axis=-1, keepdims=True)))
                p = jnp.exp2(s - m_new[g])
                alpha = jnp.exp2(m_prev[g] - m_new[g])
                l_new.append(alpha * l_prev[g] + jnp.sum(p, axis=-1, keepdims=True))
                acc_new.append(alpha * acc_prev[g] + _dot(p.astype(BF16), v_ref[rows, g * HD:(g + 1) * HD], 'nn'))
            for g in range(G):
                m_sc[g] = m_new[g]
                l_sc[g] = l_new[g]
                acc_sc[g] = acc_new[g]

        tile(0, True)

        def mid(j, carry):
            tile(j, False)
            return carry

        lax.fori_loop(1, i, mid, 0)

        @pl.when(i > 0)
        def _():
            tile(i, True)

        for g in range(G):
            l = l_sc[g]
            o_ref[:, g * HD:(g + 1) * HD] = (acc_sc[g] / l).astype(o_ref.dtype)
            lse_ref[g] = jnp.broadcast_to(m_sc[g] + jnp.log(l) * LOG2E, (T, HD))

        if comm is not None:
            @pl.when(last)
            def _():
                comm.finish(*cm_refs)

    cm_specs = comm.specs if comm is not None else []
    res = pl.pallas_call(
        body, name="attn_fwd", grid=(HEADS // G, nb),
        out_shape=[jax.ShapeDtypeStruct((L, W), BF16), jax.ShapeDtypeStruct((HEADS, L, HD), F32)]
        + (comm.out_shapes if comm is not None else []),
        in_specs=[pl.BlockSpec((T, G * QH), lambda h, i: (i, h)), pl.BlockSpec((L, G * HD), lambda h, i: (0, h)),
                  pl.BlockSpec((L, HD), lambda h, i: (0, 0)), pl.BlockSpec((L, G * HD), lambda h, i: (0, h))]
        + cm_specs,
        out_specs=[pl.BlockSpec((T, G * HD), lambda h, i: (i, h)),
                   pl.BlockSpec((G, T, HD), lambda h, i: (h, i, 0))] + cm_specs,
        scratch_shapes=[pltpu.VMEM((G, T, 1), F32), pltpu.VMEM((G, T, 1), F32), pltpu.VMEM((G, T, HD), F32)]
        + (comm.scratch if comm is not None else []),
        compiler_params=_params(("arbitrary", "arbitrary")),
    )(qm, kn, krr, vm, *(comm.arrays if comm is not None else []))
    return res[:2], res[2:]


def _attn_bwd(qm, kn, krr, vm, o, dcat, lse, comm=None):
    L = qm.shape[0]
    W = HEADS * HD
    T = _tile(L, ATT_TILE, BLK)
    nb = L // T
    n_cm = comm.n if comm is not None else 0

    def body(*refs):
        q_ref, kn_ref, kr_ref, v_ref, o_ref, do_ref, lse_ref = refs[:7]
        dq_ref, dkn_ref, dkr_ref, dv_ref = refs[7 + n_cm:11 + n_cm]
        dl_sc, dk_sc, dv_sc = refs[11 + 2 * n_cm:14 + 2 * n_cm]
        if comm is not None:
            cm_refs = (refs[7:7 + n_cm], refs[11 + n_cm:11 + 2 * n_cm], refs[14 + 2 * n_cm:])
            first, last = _grid_edges((HEADS, nb))

            @pl.when(first)
            def _():
                comm.start(*cm_refs)

        j = pl.program_id(1)

        @pl.when(j == 0)
        def _():
            dq_ref[...] = jnp.zeros_like(dq_ref)

            def rowsum(t, carry):
                rows = pl.ds(pl.multiple_of(t * T, T), T)
                dl_sc[rows, :] = jnp.sum(do_ref[rows, :].astype(F32) * o_ref[rows, :].astype(F32), axis=-1,
                                         keepdims=True)
                return carry

            lax.fori_loop(0, nb, rowsum, 0)

        k = jnp.concatenate([kn_ref[...], kr_ref[...]], axis=1)
        v = v_ref[...]
        dk_sc[...] = jnp.zeros_like(dk_sc)
        dv_sc[...] = jnp.zeros_like(dv_sc)

        def tile(i, masked):
            rows = pl.ds(pl.multiple_of(i * T, T), T)
            q = q_ref[rows, :]
            do = do_ref[rows, :]
            s = _dot(q, k, 'nt')
            if masked:
                s = jnp.where(_att_valid(T, T, i * T, j * T), s, NEG)
            p = jnp.exp2(s - lse_ref[rows, 0:1])
            dv_sc[...] += _dot(p.astype(BF16), do, 'tn')
            ds = (p * (_dot(do, v, 'nt') - dl_sc[rows, :])).astype(BF16)
            dk_sc[...] += _dot(ds, q, 'tn')
            dq_ref[rows, :] += _dot(ds, k, 'nn')

        tile(j, True)

        def rest(masked):
            def step(i, carry):
                tile(i, masked)
                return carry
            lax.fori_loop(j + 1, nb, step, 0)

        @pl.when(j == 0)
        def _():
            rest(True)

        @pl.when(j > 0)
        def _():
            rest(False)

        dk = dk_sc[...] * (1.0 / LOG2E)
        dkn_ref[...] = dk[:, 0:HD].astype(BF16)
        dkr_ref[...] = dk[:, HD:QH].astype(dkr_ref.dtype)
        dv_ref[...] = dv_sc[...].astype(BF16)

        if comm is not None:
            @pl.when(last)
            def _():
                comm.finish(*cm_refs)

    blk = pl.BlockSpec((T, HD), lambda h, j: (j, h))
    cm_specs = comm.specs if comm is not None else []
    res = pl.pallas_call(
        body, name="attn_bwd", grid=(HEADS, nb),
        out_shape=[jax.ShapeDtypeStruct((L, HEADS * QH), F32), jax.ShapeDtypeStruct((L, W), BF16),
                   jax.ShapeDtypeStruct((L, W), BF16), jax.ShapeDtypeStruct((L, W), BF16)]
        + (comm.out_shapes if comm is not None else []),
        in_specs=[pl.BlockSpec((L, QH), lambda h, j: (0, h)), blk, pl.BlockSpec((T, HD), lambda h, j: (j, 0)), blk,
                  pl.BlockSpec((L, HD), lambda h, j: (0, h)), pl.BlockSpec((L, HD), lambda h, j: (0, HEADS + h)),
                  pl.BlockSpec((None, L, HD), lambda h, j: (h, 0, 0))] + cm_specs,
        out_specs=[pl.BlockSpec((L, QH), lambda h, j: (0, h)), blk, blk, blk] + cm_specs,
        scratch_shapes=[pltpu.VMEM((L, 1), F32), pltpu.VMEM((T, QH), F32), pltpu.VMEM((T, HD), F32)]
        + (comm.scratch if comm is not None else []),
        compiler_params=_params(("arbitrary", "arbitrary")),
    )(qm, kn, krr, vm, o, dcat, lse, *(comm.arrays if comm is not None else []))
    return res[:4], res[4:]


def _unrope_q(dqm, tabs_m):
    L, W = dqm.shape
    tr = _tile(L, 384)

    def body(d_ref, cm_ref, sa_ref, sb_ref, out_ref):
        cm, sa, sb = cm_ref[...], sa_ref[...], sb_ref[...]
        for h in range(HEADS):
            out_ref[:, h * QH:h * QH + HD] = (d_ref[:, h * QH:h * QH + HD] * ATT_SCALE).astype(BF16)
            out_ref[:, h * QH + HD:(h + 1) * QH] = _rope_mla_t(d_ref[:, h * QH + HD:(h + 1) * QH] * ATT_SCALE, cm, sa,
                                                               sb).astype(BF16)

    row = pl.BlockSpec((tr, W), lambda i: (i, 0))
    tab = pl.BlockSpec((tr, HD), lambda i: (i, 0))
    return pl.pallas_call(
        body, name="unrope_q", grid=(L // tr,), out_shape=jax.ShapeDtypeStruct((L, W), BF16),
        in_specs=[row, tab, tab, tab], out_specs=row, compiler_params=_params(("parallel",)),
    )(dqm, *tabs_m)


def _q_up(cqn, wuq_p, tabs_m):
    L = cqn.shape[0]
    tm = _tile(L, 704)

    def ep(acc, cm, sa, sb):
        acc = acc * Q_PRESCALE
        parts = []
        for h in range(HEADS):
            parts.append(acc[:, h * QH:h * QH + HD])
            parts.append(_rope_mla(acc[:, h * QH + HD:(h + 1) * QH], cm, sa, sb))
        return (jnp.concatenate(parts, axis=1),)

    tab = pl.BlockSpec((tm, HD), lambda i, j: (i, 0))
    return _mm("mla_q_up", (L // tm, 1), ("parallel", "parallel"), None,
               [cqn, wuq_p], [pl.BlockSpec((tm, Q_RANK), lambda i, j: (i, 0)),
                              pl.BlockSpec((HEADS * QH, Q_RANK), lambda i, j: (0, 0))],
               [(0, 1, 'nt', 0)], [(tm, HEADS * QH)], list(tabs_m), [tab] * 3, ep,
               [jax.ShapeDtypeStruct((L, HEADS * QH), BF16)], [pl.BlockSpec((tm, HEADS * QH), lambda i, j: (i, 0))])[0]


def _mix_out(cat, w_out, h_in, post, next_norm):
    L, K = cat.shape
    D = w_out.shape[1]
    tm, tk = _tile(L, 384), K
    row = pl.BlockSpec((tm, D), lambda i, k: (i, 0))
    vec = pl.BlockSpec((1, D), lambda i, k: (0, 0))
    return _mm("mix_out", (L // tm, K // tk), ("parallel", "arbitrary"), 1,
               [cat, w_out], [pl.BlockSpec((tm, tk), lambda i, k: (i, k)), pl.BlockSpec((tk, D), lambda i, k: (k, 0))],
               [(0, 1, 'nn', 0)], [(tm, D)], [h_in, post, next_norm], [row, vec, vec], _resnorm_epilogue(1.0, True),
               [jax.ShapeDtypeStruct((L, D), F32)] * 2 + [jax.ShapeDtypeStruct((L, D), BF16)], [row, row, row])


ADAM_BLOCK_ELEMS = 512 * 704


def _adam_math(w, g, m, v):
    m = ADAM_B1 * m + (1.0 - ADAM_B1) * g
    v = ADAM_B2 * v + (1.0 - ADAM_B2) * (g * g)
    m_hat = m / (1.0 - ADAM_B1 ** ADAM_STEP)
    v_hat = v / (1.0 - ADAM_B2 ** ADAM_STEP)
    delta = -ADAM_LR * (m_hat / (jnp.sqrt(v_hat) + ADAM_EPS) + ADAM_WD * w)
    return delta, m, v


def _adam(name, w, m, v, g_slots=None, g=None):
    R, C = w.shape
    tr, tc = _tile(R, max(16, ADAM_BLOCK_ELEMS // C // 16 * 16), 16), C
    if tr * tc > ADAM_BLOCK_ELEMS:
        tr, tc = R, _tile(C, max(128, ADAM_BLOCK_ELEMS // R // 128 * 128), 128)
    from_slots = g_slots is not None

    def body(w_ref, m_ref, v_ref, g_ref, go_ref, d_ref, mo_ref, vo_ref):
        if from_slots:
            grad = g_ref[0].astype(F32)
            for s in range(1, N_DEV):
                grad = grad + g_ref[s].astype(F32)
        else:
            grad = g_ref[...]
        delta, mn, vn = _adam_math(w_ref[...], grad, m_ref[...], v_ref[...])
        go_ref[...] = grad
        d_ref[...] = delta
        mo_ref[...] = mn
        vo_ref[...] = vn

    row = pl.BlockSpec((tr, tc), lambda i, j: (i, j))
    gspec = pl.BlockSpec((N_DEV, tr, tc), lambda i, j: (0, i, j)) if from_slots else row
    return pl.pallas_call(
        body, name=name, grid=(R // tr, C // tc), out_shape=[jax.ShapeDtypeStruct((R, C), F32)] * 4,
        in_specs=[row, row, row, gspec], out_specs=[row] * 4, compiler_params=_params(("parallel", "parallel")),
    )(w, m, v, g_slots if from_slots else g)


def _unblock(gathered):
    n, r, c = gathered.shape
    return jnp.transpose(gathered, (1, 0, 2)).reshape(r, n * c)


def _reblock(full, c):
    r = full.shape[0]
    return jnp.transpose(full[:, :N_DEV * c].reshape(r, N_DEV, c), (1, 0, 2))


def _step(x, target, w, mom, vel):
    S, D = x.shape[1], x.shape[2]
    L = S + BLK
    def sq(a, n):
        if a.ndim == 2:
            return a
        if n in TRANSPOSED:
            a = jnp.swapaxes(a, 1, 2)
        return a.reshape(a.shape[1:])

    def unsq(o, n):
        o = o.reshape((1,) + o.shape)
        return jnp.swapaxes(o, 1, 2) if n in TRANSPOSED else o

    p = {n: sq(w[n], n) for n in WEIGHTS if n != 'meta_tokens'}
    gather = lambda names: _Exchange([p[n].astype(BF16) for n in names], False)
    in_s, uq_s = p['w_in'].shape[0], p['mla_w_uq'].shape[0]
    assert uq_s == HD + ROPE and N_DEV == HEADS, "a w_uq shard is one head's columns"
    tabs = _rope_tables(L)
    tabs_m = tabs[2:]
    lg = jnp.broadcast_to(jnp.log(1.0 - 2.0 ** (-5.0 - jnp.arange(HEADS, dtype=F32)))[:, None, None], (HEADS, 8, HD))

    wg1, meta = _exchange("gather_first", [p['ffn1_w_gate'].astype(BF16), w['meta_tokens']], False)
    h0 = jnp.concatenate([_unblock(meta), jnp.zeros((BLK - N_META, D), F32), x[0]], axis=0)
    a1 = _norm_fwd(h0, p['ffn1_pre_norm'])
    g1, (wu1,) = _ffn_gate(a1, wg1, comm=gather(['ffn1_w_up']))
    (u1, hid1), (wd1,) = _ffn_up_gated(a1, wu1, g1, comm=gather(['ffn1_w_down']))
    (f1, h1, um), (w_in_g,) = _ffn_down(hid1, wd1, h0, p['ffn1_post_norm'], next_norm=p['mix_pre_norm'],
                                        comm=gather(['w_in']))

    w_in = jnp.pad(w_in_g.reshape(N_DEV * in_s, D), ((0, D_INP - N_DEV * in_s), (0, 0)))
    proj, (uq_g, uk_g, uv_g, wout_g) = _mm_nt("mix_in", [(um, w_in)], BF16, tn_target=1664,
                                              comm=gather(['mla_w_uq', 'mla_w_uk', 'mla_w_uv', 'w_out']))
    wuq = jnp.pad(uq_g, ((0, 0), (0, QH - uq_s), (0, 0))).reshape(HEADS * QH, Q_RANK)
    wuk, wuv, w_out = _unblock(uk_g), _unblock(uv_g), wout_g.reshape(-1, D)
    qr, kr, vr, cqn, ckvn, krr = _prep(proj, tabs, p['mla_q_norm'], p['mla_kv_norm'])
    qm = _q_up(cqn, wuq, tabs_m)
    kn = _mm_nn("mla_k_up", ckvn, wuk, BF16)
    vm = _mm_nn("mla_v_up", ckvn, wuv, BF16)
    (o_mla, lse), (wg2, wu2) = _attn_fwd(qm, kn, krr, vm, comm=gather(['ffn2_w_gate', 'ffn2_w_up']))
    o_ret = _lin_attn("ret_fwd", qr, kr, vr, lg, False)
    ret = _post(o_ret, proj, p['ret_group_norm'])
    cat = jnp.concatenate([ret, o_mla], axis=1)
    m, h2, a2 = _mix_out(cat, w_out, h1, p['mix_post_norm'], p['ffn2_pre_norm'])

    (g2, u2, hid2), (wd2,) = _ffn_up(a2, wg2, wu2, comm=gather(['ffn2_w_down']))
    f2, h3 = _ffn_down(hid2, wd2, h2, p['ffn2_post_norm'])
    dh3, loss_blk = _loss(h3, target[0])

    dsmall = {}
    df2, dsmall['ffn2_post_norm'] = _norm_bwd(f2, p['ffn2_post_norm'], dh3, None, 0.5, BF16)
    dg2, du2 = _ffn_dhid(df2, wd2, g2, u2)
    dwd2 = _ffn_dwd(hid2, df2)
    s1 = _scatter_start("scatter_start_ffn2_down", [dwd2])
    dwg2, dwu2 = _ffn_dwgu(a2, dg2, du2, after=[s1['token']])
    s2 = _scatter_start("scatter_start_ffn2_gate_up", [dwg2, dwu2])
    da2 = _ffn_da(dg2, du2, wg2, wu2, after=[s2['token']])
    dh2, dsmall['ffn2_pre_norm'] = _norm_bwd(h2, p['ffn2_pre_norm'], da2, dh3, 1.0, F32)

    dm, dsmall['mix_post_norm'] = _norm_bwd(m, p['mix_post_norm'], dh2, None, 1.0, BF16)
    dcat = _mm_nt("mix_dcat", [(dm, w_out)], BF16)
    dwout = _mm_tn("mix_dwout", cat, [dm])[0]
    do_ret, drg, dsmall['ret_group_norm'] = _post_bwd(o_ret, proj, p['ret_group_norm'], dcat)
    dqr = _lin_attn("ret_dq", do_ret, vr, kr, lg, False, BF16)
    dkr = _lin_attn("ret_dk", vr, do_ret, qr, lg, True, BF16)
    dvr = _lin_attn("ret_dv", kr, qr, do_ret, lg, True, BF16)
    (dqm, dkn, dkr8, dvm), _ = _attn_bwd(qm, kn, krr, vm, o_mla, dcat, lse)
    dqp = _unrope_q(dqm, tabs_m)
    dwuq = _mm_tn("mla_dwuq", dqp, [cqn])[0]
    dcqn = _mm_nn("mla_dcq", dqp, wuq, F32)
    dwuk, dwuv = _mm_tn("mla_dwukv", ckvn, [dkn, dvm])
    s3 = _scatter_start("scatter_start_mixer", [
        dwout.reshape(N_DEV, -1, D), dwuq.reshape(HEADS, QH, Q_RANK)[:, :uq_s],
        _reblock(dwuk, p['mla_w_uk'].shape[1]), _reblock(dwuv, p['mla_w_uv'].shape[1])])
    dckvn = _mm_nt("mla_dckv", [(dkn, wuk), (dvm, wuv)], F32, after=[s3['token']])
    dproj, dsmall['mla_q_norm'], dsmall['mla_kv_norm'] = _prep_bwd(
        proj, dqr, dkr, dvr, drg, dcqn, dckvn, dkr8, tabs, p['mla_q_norm'], p['mla_kv_norm'])
    dwin = _mm_tn("mix_dwin", dproj, [um])[0]
    s4 = _scatter_start("scatter_start_w_in", [dwin[:N_DEV * in_s].reshape(N_DEV, in_s, D)])
    dum = _mm_nn("mix_du", dproj, w_in, F32, tn_target=512, after=[s4['token']])
    dh1, dsmall['mix_pre_norm'] = _norm_bwd(h1, p['mix_pre_norm'], dum, dh2, 1.0, F32)

    df1, dsmall['ffn1_post_norm'] = _norm_bwd(f1, p['ffn1_post_norm'], dh1, None, 0.5, BF16)
    dg1, du1 = _ffn_dhid(df1, wd1, g1, u1)
    dwd1 = _ffn_dwd(hid1, df1)
    s5 = _scatter_start("scatter_start_ffn1_down", [dwd1])
    dwg1, dwu1 = _ffn_dwgu(a1, dg1, du1, after=[s5['token']])
    s6 = _scatter_start("scatter_start_ffn1_gate_up", [dwg1, dwu1])
    da1 = _ffn_da(dg1, du1, wg1, wu1, after=[s6['token']])
    dh0, dsmall['ffn1_pre_norm'] = _norm_bwd(h0, p['ffn1_pre_norm'], da1, dh1, 1.0, F32)

    def slab(a):
        a = a.reshape(-1, 128)
        return jnp.pad(a, ((0, (-a.shape[0]) % 8), (0, 0)))

    slab_rows = lambda n: -(-(p[n].shape[-1] // 128) // 8) * 8
    packed = jnp.concatenate([slab(dsmall[n]) for n in SMALL] + [slab(dh0[:N_META]), loss_blk], axis=0)
    red = _allreduce_small(packed)
    offs = sum(slab_rows(n) for n in SMALL)
    n_small = offs
    gmeta_full = red[offs:offs + N_META * D // 128].reshape(N_META, D)
    offs += N_META * D // 128
    loss = red[offs, 0]

    early = ['ffn2_w_down', 'ffn2_w_gate', 'ffn2_w_up', 'w_out', 'mla_w_uq', 'mla_w_uk', 'mla_w_uv', 'w_in']
    late = ['ffn1_w_down', 'ffn1_w_gate', 'ffn1_w_up']
    grad, delta, new_m, new_v = {}, {}, {}, {}
    meanwhile = []

    def update_big(names, slots):
        for n, g_slots in zip(names, slots):
            outs = _adam("adam_" + n, p[n], sq(mom[n], n), sq(vel[n], n), g_slots=g_slots)
            meanwhile.append(outs[0])
            grad[n], delta[n], new_m[n], new_v[n] = [unsq(o, n) for o in outs]

    update_big(early, _scatter_wait("scatter_wait_early", [s1, s2, s3, s4], [red]))
    pack = lambda d: jnp.concatenate([slab(d[n]) for n in SMALL], axis=0)
    outs = _adam("adam_small", pack(w), pack(mom), pack(vel), g=red[:n_small])
    meanwhile.append(outs[0])
    offs = 0
    for n in SMALL:
        r = p[n].shape[-1] // 128
        grad[n], delta[n], new_m[n], new_v[n] = [o[offs:offs + r].reshape(w[n].shape) for o in outs]
        offs += slab_rows(n)
    dev = 4 * lax.axis_index("x") + 2 * lax.axis_index("y") + lax.axis_index("c")
    mcols = w['meta_tokens'].shape[1]
    gmeta = lax.dynamic_slice(gmeta_full, (0, dev * mcols), (N_META, mcols))
    outs = _adam("adam_meta", w['meta_tokens'], mom['meta_tokens'], vel['meta_tokens'], g=gmeta)
    grad['meta_tokens'], delta['meta_tokens'], new_m['meta_tokens'], new_v['meta_tokens'] = outs
    meanwhile.append(outs[0])
    update_big(late, _scatter_wait("scatter_wait_late", [s5, s6], list(meanwhile)))

    return (loss, dh0[BLK:][None], *[grad[n] for n in WEIGHTS], *[delta[n] for n in WEIGHTS],
            *[new_m[n] for n in WEIGHTS], *[new_v[n] for n in WEIGHTS])


def kernel(x, meta_tokens, ffn1_pre_norm, ffn1_w_gate, ffn1_w_up, ffn1_w_down, ffn1_post_norm, mix_pre_norm, w_in, ret_group_norm, mla_q_norm, mla_w_uq, mla_kv_norm, mla_w_uk, mla_w_uv, w_out, mix_post_norm, ffn2_pre_norm, ffn2_w_gate, ffn2_w_up, ffn2_w_down, ffn2_post_norm, loss_target, m_meta_tokens, m_ffn1_pre_norm, m_ffn1_w_gate, m_ffn1_w_up, m_ffn1_w_down, m_ffn1_post_norm, m_mix_pre_norm, m_w_in, m_ret_group_norm, m_mla_q_norm, m_mla_w_uq, m_mla_kv_norm, m_mla_w_uk, m_mla_w_uv, m_w_out, m_mix_post_norm, m_ffn2_pre_norm, m_ffn2_w_gate, m_ffn2_w_up, m_ffn2_w_down, m_ffn2_post_norm, v_meta_tokens, v_ffn1_pre_norm, v_ffn1_w_gate, v_ffn1_w_up, v_ffn1_w_down, v_ffn1_post_norm, v_mix_pre_norm, v_w_in, v_ret_group_norm, v_mla_q_norm, v_mla_w_uq, v_mla_kv_norm, v_mla_w_uk, v_mla_w_uv, v_w_out, v_mix_post_norm, v_ffn2_pre_norm, v_ffn2_w_gate, v_ffn2_w_up, v_ffn2_w_down, v_ffn2_post_norm):
    w = dict(zip(WEIGHTS, (meta_tokens, ffn1_pre_norm, ffn1_w_gate, ffn1_w_up, ffn1_w_down, ffn1_post_norm,
                           mix_pre_norm, w_in, ret_group_norm, mla_q_norm, mla_w_uq, mla_kv_norm, mla_w_uk, mla_w_uv,
                           w_out, mix_post_norm, ffn2_pre_norm, ffn2_w_gate, ffn2_w_up, ffn2_w_down, ffn2_post_norm)))
    mom = dict(zip(WEIGHTS, (m_meta_tokens, m_ffn1_pre_norm, m_ffn1_w_gate, m_ffn1_w_up, m_ffn1_w_down,
                             m_ffn1_post_norm, m_mix_pre_norm, m_w_in, m_ret_group_norm, m_mla_q_norm, m_mla_w_uq,
                             m_mla_kv_norm, m_mla_w_uk, m_mla_w_uv, m_w_out, m_mix_post_norm, m_ffn2_pre_norm,
                             m_ffn2_w_gate, m_ffn2_w_up, m_ffn2_w_down, m_ffn2_post_norm)))
    vel = dict(zip(WEIGHTS, (v_meta_tokens, v_ffn1_pre_norm, v_ffn1_w_gate, v_ffn1_w_up, v_ffn1_w_down,
                             v_ffn1_post_norm, v_mix_pre_norm, v_w_in, v_ret_group_norm, v_mla_q_norm, v_mla_w_uq,
                             v_mla_kv_norm, v_mla_w_uk, v_mla_w_uv, v_w_out, v_mix_post_norm, v_ffn2_pre_norm,
                             v_ffn2_w_gate, v_ffn2_w_up, v_ffn2_w_down, v_ffn2_post_norm)))
    return _step(x, loss_target, w, mom, vel)
```

```python
import functools
import math

import jax
import jax.numpy as jnp
from jax import lax
from jax.experimental import pallas as pl
from jax.experimental.pallas import tpu as pltpu

N_DEV = 8
N_META = 16
BLK = 128
HEADS = 8
HD = 128
ROPE = 64
Q_RANK = 512
KV_RANK = 256
QH = 2 * HD
D_INP = 4 * HEADS * HD + Q_RANK + KV_RANK + BLK
ROPE_THETA = 10000.0
EPS = 1e-6
ADAM_LR = 0.001
ADAM_B1 = 0.9
ADAM_B2 = 0.999
ADAM_EPS = 1e-08
ADAM_WD = 0.01
ADAM_STEP = 10
V7X_VMEM_LIMIT = 48 * 1024 * 1024
MESH = pl.DeviceIdType.MESH
F32 = jnp.float32
BF16 = jnp.bfloat16

WEIGHTS = ['meta_tokens', 'ffn1_pre_norm', 'ffn1_w_gate', 'ffn1_w_up', 'ffn1_w_down', 'ffn1_post_norm',
           'mix_pre_norm', 'w_in', 'ret_group_norm', 'mla_q_norm', 'mla_w_uq', 'mla_kv_norm', 'mla_w_uk',
           'mla_w_uv', 'w_out', 'mix_post_norm', 'ffn2_pre_norm', 'ffn2_w_gate', 'ffn2_w_up', 'ffn2_w_down',
           'ffn2_post_norm']
SMALL = ['ffn1_pre_norm', 'ffn1_post_norm', 'mix_pre_norm', 'ret_group_norm', 'mla_q_norm', 'mla_kv_norm',
         'mix_post_norm', 'ffn2_pre_norm', 'ffn2_post_norm']
TRANSPOSED = ('ffn1_w_gate', 'ffn1_w_up', 'ffn2_w_gate', 'ffn2_w_up', 'w_in', 'mla_w_uq')
BIG = ['ffn1_w_gate', 'ffn1_w_up', 'ffn1_w_down', 'w_in', 'mla_w_uq', 'mla_w_uk', 'mla_w_uv', 'w_out',
       'ffn2_w_gate', 'ffn2_w_up', 'ffn2_w_down']

_DIMS = {'nn': (((1,), (0,)), ((), ())), 'nt': (((1,), (1,)), ((), ())), 'tn': (((0,), (0,)), ((), ()))}


def _tile(n, target, mult=16):
    best = None
    for t in range(mult, min(n, target) + 1, mult):
        if n % t == 0:
            best = t
    return best if best is not None else n


def _params(sem):
    return pltpu.CompilerParams(dimension_semantics=sem, vmem_limit_bytes=V7X_VMEM_LIMIT)


def _dot(a, b, dims):
    return lax.dot_general(a, b, _DIMS[dims], preferred_element_type=F32)


def _sigmoid(x):
    return 0.5 * jnp.tanh(0.5 * x) + 0.5


def _me_and_peers():
    x, y, c = lax.axis_index("x"), lax.axis_index("y"), lax.axis_index("c")

    def peer(j):
        px = 1 - x if (j >> 2) & 1 else x
        py = 1 - y if (j >> 1) & 1 else y
        pc = 1 - c if j & 1 else c
        return (px, py, pc), 4 * px + 2 * py + pc

    return 4 * x + 2 * y + c, peer


class _Exchange:
    def __init__(self, arrays, per_peer):
        self.arrays = list(arrays)
        self.per_peer = per_peer
        self.n = len(self.arrays)
        self.out_shapes = [jax.ShapeDtypeStruct((N_DEV,) + tuple(a.shape[1:] if per_peer else a.shape), a.dtype)
                           for a in self.arrays]
        self.specs = [pl.BlockSpec(memory_space=pl.ANY)] * self.n
        self.scratch = [pltpu.SemaphoreType.DMA((7 * self.n,)), pltpu.SemaphoreType.DMA((7 * self.n,)),
                        pltpu.SemaphoreType.DMA((self.n,))]

    def _copies(self, src, dst, sems):
        send_sems, recv_sems, local_sems = sems
        me, peer = _me_and_peers()
        sib, _ = peer(1)
        local, sends, recvs, passes = [], {}, {}, {}
        for k in range(self.n):
            own = src[k].at[me] if self.per_peer else src[k]
            local.append(pltpu.make_async_copy(own, dst[k].at[me], local_sems.at[k]))
            for j in range(1, N_DEV):
                pid, pidx = peer(j)
                out = src[k].at[pidx] if self.per_peer else src[k]
                sem = dict(send_sem=send_sems.at[k * 7 + j - 1], recv_sem=recv_sems.at[k * 7 + j - 1])
                recvs[k, j] = pltpu.make_async_remote_copy(src_ref=out, dst_ref=dst[k].at[pidx], device_id=pid,
                                                           device_id_type=MESH, **sem)
                if self.per_peer or j in (1, 2, 4, 6):
                    sends[k, j] = pltpu.make_async_remote_copy(src_ref=out, dst_ref=dst[k].at[me], device_id=pid,
                                                               device_id_type=MESH, **sem)
                else:
                    _, origin = peer(j ^ 1)
                    passes[k, j ^ 1] = pltpu.make_async_remote_copy(
                        src_ref=dst[k].at[origin], dst_ref=dst[k].at[origin], device_id=sib, device_id_type=MESH, **sem)
        return local, sends, recvs, passes

    def start(self, src, dst, sems):
        local, sends, _, _ = self._copies(src, dst, sems)
        for cp in local + list(sends.values()):
            cp.start()

    def finish(self, src, dst, sems):
        local, sends, recvs, passes = self._copies(src, dst, sems)
        for key, cp in passes.items():
            recvs[key].wait_recv()
            cp.start()
        for key, cp in recvs.items():
            if key not in passes:
                cp.wait_recv()
        for cp in list(sends.values()) + list(passes.values()):
            cp.wait_send()
        for cp in local:
            cp.wait()


def _grid_edges(grid):
    first, last = None, None
    for a, n in enumerate(grid):
        f, l = pl.program_id(a) == 0, pl.program_id(a) == n - 1
        first = f if first is None else first & f
        last = l if last is None else last & l
    return first, last


def _exchange(name, arrays, per_peer):
    ex = _Exchange(arrays, per_peer)
    n = ex.n

    def body(*refs):
        ex.start(refs[:n], refs[n:2 * n], refs[2 * n:])
        ex.finish(refs[:n], refs[n:2 * n], refs[2 * n:])

    return pl.pallas_call(body, name=name, out_shape=ex.out_shapes, in_specs=ex.specs, out_specs=ex.specs,
                          scratch_shapes=ex.scratch)(*arrays)


def _scatter_start(name, blocks):
    n = len(blocks)

    def body(*refs):
        src, land = refs[:n], refs[n:2 * n]
        send_sems, recv_sems = refs[2 * n], refs[2 * n + 1]
        token, local_sems = refs[4 * n + 2], refs[4 * n + 3]
        me, peer = _me_and_peers()
        local = [pltpu.make_async_copy(src[k].at[me], land[k].at[me], local_sems.at[k]) for k in range(n)]
        for cp in local:
            cp.start()
        for cp in local:
            cp.wait()
        for k in range(n):
            for j in range(1, N_DEV):
                pid, pidx = peer(j)
                pltpu.make_async_remote_copy(src_ref=src[k].at[pidx], dst_ref=land[k].at[me],
                                             send_sem=send_sems.at[k * 7 + j - 1], recv_sem=recv_sems.at[k * 7 + j - 1],
                                             device_id=pid, device_id_type=MESH).start()
        token[...] = jnp.zeros_like(token)

    hbm = pl.BlockSpec(memory_space=pltpu.HBM)
    sem = pl.BlockSpec(memory_space=pltpu.SEMAPHORE)
    thru = [pltpu.HBM(b.shape, b.dtype) for b in blocks]
    res = pl.pallas_call(
        body, name=name,
        out_shape=(pltpu.SemaphoreType.DMA((7 * n,)), pltpu.SemaphoreType.DMA((7 * n,)), *thru, *thru,
                   jax.ShapeDtypeStruct((8, 128), F32)),
        in_specs=(hbm,) * (2 * n), out_specs=(sem, sem) + (hbm,) * (2 * n) + (pl.BlockSpec(memory_space=pltpu.VMEM),),
        input_output_aliases={k: 2 + k for k in range(2 * n)}, scratch_shapes=[pltpu.SemaphoreType.DMA((n,))],
        compiler_params=pltpu.CompilerParams(has_side_effects=pltpu.SideEffectType.DATAFLOW_SIDE_EFFECTING),
    )(*[pltpu.with_memory_space_constraint(b, pltpu.HBM) for b in blocks],
      *[pltpu.with_memory_space_constraint(lax.empty(b.shape, b.dtype), pltpu.HBM) for b in blocks])
    return dict(send=res[0], recv=res[1], src=list(res[2:2 + n]), land=list(res[2 + n:2 + 2 * n]), token=res[-1])


def _scatter_wait(name, started, after):
    sizes = [len(st['src']) for st in started]
    n_all = sum(sizes)

    def body(*refs):
        me, peer = _me_and_peers()
        at = 0
        for g, n in enumerate(sizes):
            src, land = refs[at:at + n], refs[at + n:at + 2 * n]
            send_sems, recv_sems = refs[at + 2 * n], refs[at + 2 * n + 1]
            at += 2 * n + 2
            for k in range(n):
                for j in range(1, N_DEV):
                    pid, pidx = peer(j)
                    cp = pltpu.make_async_remote_copy(
                        src_ref=src[k].at[pidx], dst_ref=land[k].at[pidx], send_sem=send_sems.at[k * 7 + j - 1],
                        recv_sem=recv_sems.at[k * 7 + j - 1], device_id=pid, device_id_type=MESH)
                    cp.wait_send()
                    cp.wait_recv()

    hbm = pl.BlockSpec(memory_space=pltpu.HBM)
    sem = pl.BlockSpec(memory_space=pltpu.SEMAPHORE)
    ops, specs, outs, alias = [], [], [], {}
    for st in started:
        for a in st['src'] + st['land']:
            alias[len(ops)] = len(outs)
            ops.append(a)
            specs.append(hbm)
            outs.append(pltpu.HBM(a.shape, a.dtype))
        ops += [st['send'], st['recv']]
        specs += [sem, sem]
    res = pl.pallas_call(
        body, name=name, out_shape=tuple(outs),
        in_specs=tuple(specs) + (pl.BlockSpec(memory_space=pl.ANY),) * len(after), out_specs=(hbm,) * len(outs),
        input_output_aliases=alias,
        compiler_params=pltpu.CompilerParams(has_side_effects=pltpu.SideEffectType.DATAFLOW_SIDE_EFFECTING),
    )(*ops, *after)
    landed, at = [], 0
    for n in sizes:
        landed += list(res[at + n:at + 2 * n])
        at += 2 * n
    return landed


def _allreduce_small(v):
    rows = v.shape[0]

    def body(v_ref, out_ref, buf, send_sems, recv_sems):
        me, peer = _me_and_peers()
        buf[pl.ds(me, 1)] = v_ref[...][None]
        sends = []
        for j in range(1, N_DEV):
            pid, _ = peer(j)
            cp = pltpu.make_async_remote_copy(src_ref=v_ref, dst_ref=buf.at[me], send_sem=send_sems.at[j - 1],
                                              recv_sem=recv_sems.at[j - 1], device_id=pid, device_id_type=MESH)
            cp.start()
            sends.append(cp)
        for j in range(1, N_DEV):
            pid, pidx = peer(j)
            pltpu.make_async_remote_copy(src_ref=v_ref, dst_ref=buf.at[pidx], send_sem=send_sems.at[j - 1],
                                         recv_sem=recv_sems.at[j - 1], device_id=pid,
                                         device_id_type=MESH).wait_recv()
        for cp in sends:
            cp.wait_send()
        acc = buf[0]
        for s in range(1, N_DEV):
            acc = acc + buf[s]
        out_ref[...] = acc

    vm = pl.BlockSpec(memory_space=pltpu.VMEM)
    return pl.pallas_call(
        body, name="allreduce_small", out_shape=jax.ShapeDtypeStruct(v.shape, F32),
        in_specs=[vm], out_specs=vm,
        scratch_shapes=[pltpu.VMEM((N_DEV, rows, 128), F32), pltpu.SemaphoreType.DMA((7,)),
                        pltpu.SemaphoreType.DMA((7,))],
    )(v)


def _mm(name, grid, sem, k_axis, ops, op_specs, pairs, acc_shapes, extras, extra_specs, epilogue, outs, out_specs,
        comm=None, after=()):
    n_op, n_ex, n_out = len(ops), len(extras), len(outs)
    nk = grid[k_axis] if k_axis is not None else 1
    n_acc = len(acc_shapes) if nk > 1 else 0
    n_cm = comm.n if comm is not None else 0
    after = list(after)
    assert not (after and comm is not None)

    def body(*refs):
        op_refs = refs[:n_op]
        ex_refs = refs[n_op:n_op + n_ex]
        n_in = n_op + n_ex + n_cm + len(after)
        out_refs = refs[n_in:n_in + n_out]
        acc_refs = refs[n_in + n_out + n_cm:n_in + n_out + n_cm + n_acc]
        if comm is not None:
            cm_refs = (refs[n_op + n_ex:n_in], refs[n_in + n_out:n_in + n_out + n_cm],
                       refs[n_in + n_out + n_cm + n_acc:])
            first, last = _grid_edges(grid)

            @pl.when(first)
            def _():
                comm.start(*cm_refs)

        def finish(vals):
            res = epilogue(*vals, *[e[...] for e in ex_refs])
            for o, r in zip(out_refs, res):
                o[...] = r.astype(o.dtype)

        if nk == 1:
            parts = [None] * len(acc_shapes)
            for li, ri, dims, ai in pairs:
                d = _dot(op_refs[li][...], op_refs[ri][...], dims)
                parts[ai] = d if parts[ai] is None else parts[ai] + d
            finish(parts)
        else:
            k = pl.program_id(k_axis)

            @pl.when(k == 0)
            def _():
                for a in acc_refs:
                    a[...] = jnp.zeros_like(a)

            for li, ri, dims, ai in pairs:
                acc_refs[ai][...] += _dot(op_refs[li][...], op_refs[ri][...], dims)

            @pl.when(k == nk - 1)
            def _():
                finish([a[...] for a in acc_refs])

        if comm is not None:
            @pl.when(last)
            def _():
                comm.finish(*cm_refs)

    scratch = [pltpu.VMEM(s, F32) for s in acc_shapes] if nk > 1 else []
    if comm is None:
        return pl.pallas_call(
            body, name=name, grid=grid, out_shape=outs,
            in_specs=list(op_specs) + list(extra_specs) + [pl.BlockSpec(memory_space=pl.ANY)] * len(after),
            out_specs=list(out_specs), scratch_shapes=scratch, compiler_params=_params(sem),
        )(*ops, *extras, *after)
    res = pl.pallas_call(
        body, name=name, grid=grid, out_shape=list(outs) + comm.out_shapes,
        in_specs=list(op_specs) + list(extra_specs) + comm.specs, out_specs=list(out_specs) + comm.specs,
        scratch_shapes=scratch + comm.scratch, compiler_params=_params(("arbitrary",) * len(grid)),
    )(*ops, *extras, *comm.arrays)
    return res[:n_out], res[n_out:]


def _with_comm(res, comm, pick):
    if comm is None:
        return pick(res)
    return pick(res[0]), res[1]


def _mm_nn(name, a, w, out_dtype, tm_target=704, tn_target=1664, epilogue=None, extras=(), extra_specs=(), comm=None,
           after=()):
    L, K = a.shape
    N = w.shape[1]
    tm, tn = _tile(L, tm_target), _tile(N, tn_target, 128)
    ep = epilogue if epilogue is not None else (lambda acc: (acc,))
    res = _mm(name, (L // tm, N // tn), ("parallel", "parallel"), None,
              [a, w], [pl.BlockSpec((tm, K), lambda i, j: (i, 0)), pl.BlockSpec((K, tn), lambda i, j: (0, j))],
              [(0, 1, 'nn', 0)], [(tm, tn)], list(extras), list(extra_specs), ep,
              [jax.ShapeDtypeStruct((L, N), out_dtype)], [pl.BlockSpec((tm, tn), lambda i, j: (i, j))], comm=comm,
              after=after)
    return _with_comm(res, comm, lambda o: o[0])


def _mm_nt(name, pairs_aw, out_dtype, tm_target=704, tn_target=512, comm=None, after=()):
    L = pairs_aw[0][0].shape[0]
    N = pairs_aw[0][1].shape[0]
    tm, tn = _tile(L, tm_target), _tile(N, tn_target, 128)
    ops, specs, pairs = [], [], []
    for t, (a, w) in enumerate(pairs_aw):
        K = a.shape[1]
        ops += [a, w]
        specs += [pl.BlockSpec((tm, K), lambda i, j: (i, 0)), pl.BlockSpec((tn, K), lambda i, j: (j, 0))]
        pairs.append((2 * t, 2 * t + 1, 'nt', 0))
    res = _mm(name, (L // tm, N // tn), ("parallel", "parallel"), None, ops, specs, pairs, [(tm, tn)], [], [],
              lambda acc: (acc,), [jax.ShapeDtypeStruct((L, N), out_dtype)],
              [pl.BlockSpec((tm, tn), lambda i, j: (i, j))], comm=comm, after=after)
    return _with_comm(res, comm, lambda o: o[0])


def _mm_tn(name, a, bs, out_dtype=BF16, tk_target=1408, tn_target=1664, tm_target=2048, comm=None):
    L, M = a.shape
    N = bs[0].shape[1]
    tk, tn, tm = _tile(L, tk_target), _tile(N, tn_target, 128), _tile(M, tm_target, 128)
    nb = len(bs)
    ops = [a] + list(bs)
    specs = [pl.BlockSpec((tk, tm), lambda i, j, k: (k, i))] + [pl.BlockSpec((tk, tn), lambda i, j, k: (k, j))] * nb
    res = _mm(name, (M // tm, N // tn, L // tk), ("parallel", "parallel", "arbitrary"), 2, ops, specs,
              [(0, 1 + t, 'tn', t) for t in range(nb)], [(tm, tn)] * nb, [], [], lambda *acc: acc,
              [jax.ShapeDtypeStruct((M, N), out_dtype)] * nb,
              [pl.BlockSpec((tm, tn), lambda i, j, k: (i, j))] * nb, comm=comm)
    return _with_comm(res, comm, lambda o: o)


def _norm_fwd(x, w):
    L, D = x.shape
    tr = _tile(L, 512)

    def body(x_ref, w_ref, y_ref):
        v = x_ref[...]
        r = lax.rsqrt(jnp.mean(v * v, axis=-1, keepdims=True) + EPS)
        y_ref[...] = (v * r * w_ref[...]).astype(y_ref.dtype)

    return pl.pallas_call(
        body, name="norm_fwd", grid=(L // tr,), out_shape=jax.ShapeDtypeStruct((L, D), BF16),
        in_specs=[pl.BlockSpec((tr, D), lambda i: (i, 0)), pl.BlockSpec((1, D), lambda i: (0, 0))],
        out_specs=pl.BlockSpec((tr, D), lambda i: (i, 0)), compiler_params=_params(("parallel",)),
    )(x, w)


def _norm_bwd_math(x, w, dy):
    r = lax.rsqrt(jnp.mean(x * x, axis=-1, keepdims=True) + EPS)
    gy = dy * w
    dx = r * (gy - x * (r * r) * jnp.mean(gy * x, axis=-1, keepdims=True))
    dw = jnp.sum(dy * x * r, axis=0, keepdims=True)
    return dx, dw


def _norm_bwd(x, w, dy, res, scale, out_dtype):
    L, D = x.shape
    tr = _tile(L, 384)
    has_res = res is not None

    def body(*refs):
        x_ref, w_ref, dy_ref = refs[:3]
        res_ref = refs[3] if has_res else None
        dx_ref, dw_ref = refs[-2:]
        dx, dw = _norm_bwd_math(x_ref[...], w_ref[...], dy_ref[...].astype(F32))
        dx = scale * dx
        if has_res:
            dx = dx + res_ref[...]
        dx_ref[...] = dx.astype(dx_ref.dtype)

        @pl.when(pl.program_id(0) == 0)
        def _():
            dw_ref[...] = jnp.zeros_like(dw_ref)

        dw_ref[...] += scale * dw

    row = pl.BlockSpec((tr, D), lambda i: (i, 0))
    vec = pl.BlockSpec((1, D), lambda i: (0, 0))
    return pl.pallas_call(
        body, name="norm_bwd", grid=(L // tr,),
        out_shape=[jax.ShapeDtypeStruct((L, D), out_dtype), jax.ShapeDtypeStruct((1, D), F32)],
        in_specs=[row, vec, row] + ([row] if has_res else []), out_specs=[row, vec],
        compiler_params=_params(("arbitrary",)),
    )(*([x, w, dy] + ([res] if has_res else [])))


def _loss(h, target):
    L, D = h.shape

    def body(h_ref, t_ref, dh_ref, loss_ref):
        i = pl.program_id(0)

        @pl.when(i == 0)
        def _():
            dh_ref[...] = jnp.zeros_like(dh_ref)
            loss_ref[...] = jnp.zeros_like(loss_ref)

        @pl.when(i > 0)
        def _():
            diff = h_ref[...] - t_ref[...]
            dh_ref[...] = diff * (1.0 / D)
            loss_ref[...] += 0.5 * jnp.sum(diff * diff) * (1.0 / D)

    return pl.pallas_call(
        body, name="loss", grid=(L // BLK,),
        out_shape=[jax.ShapeDtypeStruct((L, D), F32), jax.ShapeDtypeStruct((8, 128), F32)],
        in_specs=[pl.BlockSpec((BLK, D), lambda i: (i, 0)),
                  pl.BlockSpec((BLK, D), lambda i: (jnp.maximum(i - 1, 0), 0))],
        out_specs=[pl.BlockSpec((BLK, D), lambda i: (i, 0)), pl.BlockSpec((8, 128), lambda i: (0, 0))],
        compiler_params=_params(("arbitrary",)),
    )(h, target)


def _ffn_up(a, wg, wu, comm=None):
    L, D = a.shape
    F = wg.shape[1]
    tm = _tile(L, 704)

    def ep(g, u):
        return g, u, g * _sigmoid(g) * u

    hspec = pl.BlockSpec((None, tm, F), lambda i, j: (j, i, 0))
    wspec = pl.BlockSpec((None, F, D), lambda i, j: (j, 0, 0))
    res = _mm("ffn_up", (L // tm, N_DEV), ("parallel", "parallel"), None,
              [a, wg, wu], [pl.BlockSpec((tm, D), lambda i, j: (i, 0)), wspec, wspec],
              [(0, 1, 'nt', 0), (0, 2, 'nt', 1)], [(tm, F)] * 2, [], [], ep,
              [jax.ShapeDtypeStruct((N_DEV, L, F), BF16)] * 3, [hspec] * 3, comm=comm)
    return _with_comm(res, comm, lambda o: o)


def _ffn_gate(a, wg, comm=None):
    L, D = a.shape
    F = wg.shape[1]
    tm = _tile(L, 704)
    res = _mm("ffn_gate", (L // tm, N_DEV), ("parallel", "parallel"), None,
              [a, wg], [pl.BlockSpec((tm, D), lambda i, j: (i, 0)), pl.BlockSpec((None, F, D), lambda i, j: (j, 0, 0))],
              [(0, 1, 'nt', 0)], [(tm, F)], [], [], lambda g: (g,),
              [jax.ShapeDtypeStruct((N_DEV, L, F), BF16)], [pl.BlockSpec((None, tm, F), lambda i, j: (j, i, 0))],
              comm=comm)
    return _with_comm(res, comm, lambda o: o[0])


def _ffn_up_gated(a, wu, g, comm=None):
    L, D = a.shape
    F = wu.shape[1]
    tm = _tile(L, 704)

    def ep(u, g_):
        g32 = g_.astype(F32)
        return u, g32 * _sigmoid(g32) * u

    hspec = pl.BlockSpec((None, tm, F), lambda i, j: (j, i, 0))
    res = _mm("ffn_up_gated", (L // tm, N_DEV), ("parallel", "parallel"), None,
              [a, wu], [pl.BlockSpec((tm, D), lambda i, j: (i, 0)), pl.BlockSpec((None, F, D), lambda i, j: (j, 0, 0))],
              [(0, 1, 'nt', 0)], [(tm, F)], [g], [hspec], ep,
              [jax.ShapeDtypeStruct((N_DEV, L, F), BF16)] * 2, [hspec, hspec], comm=comm)
    return _with_comm(res, comm, lambda o: o)


def _resnorm_epilogue(scale, with_next):
    def ep(acc, h, w, *w_next):
        r = lax.rsqrt(jnp.mean(acc * acc, axis=-1, keepdims=True) + EPS)
        h_out = h + scale * (acc * r * w)
        if not with_next:
            return acc, h_out
        r_next = lax.rsqrt(jnp.mean(h_out * h_out, axis=-1, keepdims=True) + EPS)
        return acc, h_out, h_out * r_next * w_next[0]
    return ep


def _ffn_down(hid, wd, h_in, post, next_norm=None, comm=None):
    _, L, F = hid.shape
    D = wd.shape[2]
    tm = _tile(L, 528)
    row = pl.BlockSpec((tm, D), lambda i, j: (i, 0))
    vec = pl.BlockSpec((1, D), lambda i, j: (0, 0))
    nxt = [] if next_norm is None else [next_norm]
    res = _mm("ffn_down", (L // tm, N_DEV), ("parallel", "arbitrary"), 1,
              [hid, wd], [pl.BlockSpec((None, tm, F), lambda i, j: (j, i, 0)),
                          pl.BlockSpec((None, F, D), lambda i, j: (j, 0, 0))],
              [(0, 1, 'nn', 0)], [(tm, D)], [h_in, post] + nxt, [row, vec] + [vec] * len(nxt),
              _resnorm_epilogue(0.5, bool(nxt)),
              [jax.ShapeDtypeStruct((L, D), F32)] * 2 + [jax.ShapeDtypeStruct((L, D), BF16)] * len(nxt),
              [row] * (2 + len(nxt)), comm=comm)
    return _with_comm(res, comm, lambda o: o)


def _ffn_dhid(df, wd, g, u, comm=None):
    L, D = df.shape
    F = wd.shape[1]
    tm = _tile(L, 704)

    def ep(dhid, g_, u_):
        g32, u32 = g_.astype(F32), u_.astype(F32)
        sg = _sigmoid(g32)
        return dhid * u32 * sg * (1.0 + g32 * (1.0 - sg)), dhid * g32 * sg

    hspec = pl.BlockSpec((None, tm, F), lambda i, j: (j, i, 0))
    res = _mm("ffn_dhid", (L // tm, N_DEV), ("parallel", "parallel"), None,
              [df, wd], [pl.BlockSpec((tm, D), lambda i, j: (i, 0)),
                         pl.BlockSpec((None, F, D), lambda i, j: (j, 0, 0))],
              [(0, 1, 'nt', 0)], [(tm, F)], [g, u], [hspec, hspec], ep,
              [jax.ShapeDtypeStruct((N_DEV, L, F), BF16)] * 2, [hspec, hspec], comm=comm)
    return _with_comm(res, comm, lambda o: o)


def _ffn_dwd(hid, df, comm=None):
    _, L, F = hid.shape
    D = df.shape[1]
    tk = _tile(L, 1408)
    res = _mm("ffn_dwd", (N_DEV, L // tk), ("parallel", "arbitrary"), 1,
              [hid, df], [pl.BlockSpec((None, tk, F), lambda j, k: (j, k, 0)),
                          pl.BlockSpec((tk, D), lambda j, k: (k, 0))],
              [(0, 1, 'tn', 0)], [(F, D)], [], [], lambda acc: (acc,),
              [jax.ShapeDtypeStruct((N_DEV, F, D), BF16)], [pl.BlockSpec((None, F, D), lambda j, k: (j, 0, 0))],
              comm=comm)
    return _with_comm(res, comm, lambda o: o[0])


def _ffn_dwgu(a, dg, du, comm=None, after=()):
    L, D = a.shape
    F = dg.shape[2]
    tk = _tile(L, 1408)
    hspec = pl.BlockSpec((None, tk, F), lambda j, k: (j, k, 0))
    wspec = pl.BlockSpec((None, F, D), lambda j, k: (j, 0, 0))
    res = _mm("ffn_dwgu", (N_DEV, L // tk), ("parallel", "arbitrary"), 1,
              [a, dg, du], [pl.BlockSpec((tk, D), lambda j, k: (k, 0)), hspec, hspec],
              [(1, 0, 'tn', 0), (2, 0, 'tn', 1)], [(F, D)] * 2, [], [], lambda *acc: acc,
              [jax.ShapeDtypeStruct((N_DEV, F, D), BF16)] * 2, [wspec, wspec], comm=comm, after=after)
    return _with_comm(res, comm, lambda o: o)


def _ffn_da(dg, du, wg, wu, comm=None, after=()):
    _, L, F = dg.shape
    D = wg.shape[2]
    tm = _tile(L, 704)
    hspec = pl.BlockSpec((None, tm, F), lambda i, j: (j, i, 0))
    wspec = pl.BlockSpec((None, F, D), lambda i, j: (j, 0, 0))
    row = pl.BlockSpec((tm, D), lambda i, j: (i, 0))
    res = _mm("ffn_da", (L // tm, N_DEV), ("parallel", "arbitrary"), 1,
              [dg, du, wg, wu], [hspec, hspec, wspec, wspec],
              [(0, 2, 'nn', 0), (1, 3, 'nn', 0)], [(tm, D)], [], [], lambda acc: (acc,),
              [jax.ShapeDtypeStruct((L, D), F32)], [row], comm=comm, after=after)
    return _with_comm(res, comm, lambda o: o[0])


def _rope_tables(L):
    rows = jnp.arange(L, dtype=F32)
    pos = jnp.where(rows < BLK, rows, rows - (BLK - N_META))
    inv_r = ROPE_THETA ** (-jnp.arange(0, HD, 2, dtype=F32) / HD)
    ang_r = pos[:, None] * inv_r[None, :]
    cr = jnp.concatenate([jnp.cos(ang_r), jnp.cos(ang_r)], axis=1)
    sr = jnp.concatenate([-jnp.sin(ang_r), jnp.sin(ang_r)], axis=1)
    inv_m = ROPE_THETA ** (-jnp.arange(0, ROPE, 2, dtype=F32) / ROPE)
    ang_m = pos[:, None] * inv_m[None, :]
    z32 = jnp.zeros((L, ROPE // 2), F32)
    z64 = jnp.zeros((L, HD - ROPE), F32)
    cm = jnp.concatenate([jnp.cos(ang_m), jnp.cos(ang_m), z64], axis=1)
    sa = jnp.concatenate([-jnp.sin(ang_m), z32, z64], axis=1)
    sb = jnp.concatenate([z32, jnp.sin(ang_m), z64], axis=1)
    return cr, sr, cm, sa, sb


def _rope_ret(x, cr, sr):
    return x * cr + pltpu.roll(x, HD // 2, 1) * sr


def _rope_ret_t(d, cr, sr):
    return d * cr + pltpu.roll(d * sr, HD // 2, 1)


def _rope_mla(x, cm, sa, sb):
    return x * cm + pltpu.roll(x, HD - ROPE // 2, 1) * sa + pltpu.roll(x, ROPE // 2, 1) * sb


def _rope_mla_t(d, cm, sa, sb):
    return d * cm + pltpu.roll(d * sa, ROPE // 2, 1) + pltpu.roll(d * sb, HD - ROPE // 2, 1)


C_RQ, C_RK, C_RV, C_RG = 0, HEADS * HD, 2 * HEADS * HD, 3 * HEADS * HD
C_CQ = 4 * HEADS * HD
C_CKV = C_CQ + Q_RANK
C_KR = C_CKV + KV_RANK
RET_K_SCALE = HD ** -0.5


def _prep(proj, tabs, qn, kvn):
    L = proj.shape[0]
    tr = _tile(L, 256)
    W = HEADS * HD

    def body(p_ref, cr_ref, sr_ref, cm_ref, sa_ref, sb_ref, qn_ref, kvn_ref, q_ref, k_ref, v_ref, cq_ref, ckv_ref,
             kr_ref):
        cr, sr = cr_ref[...], sr_ref[...]
        for h in range(HEADS):
            sl = slice(h * HD, (h + 1) * HD)
            q_ref[:, sl] = _rope_ret(p_ref[:, C_RQ + h * HD:C_RQ + (h + 1) * HD].astype(F32), cr, sr).astype(BF16)
            k_ref[:, sl] = (_rope_ret(p_ref[:, C_RK + h * HD:C_RK + (h + 1) * HD].astype(F32), cr, sr)
                            * RET_K_SCALE).astype(BF16)
        v_ref[...] = p_ref[:, C_RV:C_RV + W].astype(BF16)
        cq = p_ref[:, C_CQ:C_CQ + Q_RANK].astype(F32)
        cq_ref[...] = (cq * lax.rsqrt(jnp.mean(cq * cq, axis=-1, keepdims=True) + EPS) * qn_ref[...]).astype(BF16)
        ckv = p_ref[:, C_CKV:C_CKV + KV_RANK].astype(F32)
        ckv_ref[...] = (ckv * lax.rsqrt(jnp.mean(ckv * ckv, axis=-1, keepdims=True) + EPS)
                        * kvn_ref[...]).astype(BF16)
        kr_ref[...] = _rope_mla(p_ref[:, C_KR:C_KR + HD].astype(F32), cm_ref[...], sa_ref[...], sb_ref[...]).astype(BF16)

    row = lambda w: pl.BlockSpec((tr, w), lambda i: (i, 0))
    vec = lambda w: pl.BlockSpec((1, w), lambda i: (0, 0))
    return pl.pallas_call(
        body, name="mix_prep", grid=(L // tr,),
        out_shape=[jax.ShapeDtypeStruct((L, W), BF16)] * 3 + [jax.ShapeDtypeStruct((L, Q_RANK), BF16),
                                                              jax.ShapeDtypeStruct((L, KV_RANK), BF16),
                                                              jax.ShapeDtypeStruct((L, HD), BF16)],
        in_specs=[row(D_INP)] + [row(HD)] * 5 + [vec(Q_RANK), vec(KV_RANK)],
        out_specs=[row(W)] * 3 + [row(Q_RANK), row(KV_RANK), row(HD)],
        compiler_params=_params(("parallel",)),
    )(proj, *tabs, qn, kvn)


def _prep_bwd(proj, dq, dk, dv, drg, dcqn, dckvn, dkr8, tabs, qn, kvn):
    L = proj.shape[0]
    tr = _tile(L, 192)
    W = HEADS * HD

    def body(p_ref, dq_ref, dk_ref, dv_ref, drg_ref, dcq_ref, dckv_ref, dkr_ref, cr_ref, sr_ref, cm_ref, sa_ref,
             sb_ref, qn_ref, kvn_ref, dp_ref, dqn_ref, dkvn_ref):
        cr, sr = cr_ref[...], sr_ref[...]
        dkr = None
        for h in range(HEADS):
            sl = slice(h * HD, (h + 1) * HD)
            dp_ref[:, C_RQ + h * HD:C_RQ + (h + 1) * HD] = _rope_ret_t(dq_ref[:, sl].astype(F32), cr, sr).astype(BF16)
            dp_ref[:, C_RK + h * HD:C_RK + (h + 1) * HD] = (_rope_ret_t(dk_ref[:, sl].astype(F32), cr, sr)
                                                            * RET_K_SCALE).astype(BF16)
            part = dkr_ref[:, sl].astype(F32)
            dkr = part if dkr is None else dkr + part
        dp_ref[:, C_RV:C_RV + W] = dv_ref[...].astype(BF16)
        dp_ref[:, C_RG:C_RG + W] = drg_ref[...].astype(BF16)
        dcq, dqn = _norm_bwd_math(p_ref[:, C_CQ:C_CQ + Q_RANK].astype(F32), qn_ref[...], dcq_ref[...])
        dp_ref[:, C_CQ:C_CQ + Q_RANK] = dcq.astype(BF16)
        dckv, dkvn = _norm_bwd_math(p_ref[:, C_CKV:C_CKV + KV_RANK].astype(F32), kvn_ref[...], dckv_ref[...])
        dp_ref[:, C_CKV:C_CKV + KV_RANK] = dckv.astype(BF16)
        dp_ref[:, C_KR:C_KR + HD] = _rope_mla_t(dkr, cm_ref[...], sa_ref[...], sb_ref[...]).astype(BF16)

        @pl.when(pl.program_id(0) == 0)
        def _():
            dqn_ref[...] = jnp.zeros_like(dqn_ref)
            dkvn_ref[...] = jnp.zeros_like(dkvn_ref)

        dqn_ref[...] += dqn
        dkvn_ref[...] += dkvn

    row = lambda w: pl.BlockSpec((tr, w), lambda i: (i, 0))
    vec = lambda w: pl.BlockSpec((1, w), lambda i: (0, 0))
    return pl.pallas_call(
        body, name="mix_prep_bwd", grid=(L // tr,),
        out_shape=[jax.ShapeDtypeStruct((L, D_INP), BF16), jax.ShapeDtypeStruct((1, Q_RANK), F32),
                   jax.ShapeDtypeStruct((1, KV_RANK), F32)],
        in_specs=[row(D_INP)] + [row(W)] * 4 + [row(Q_RANK), row(KV_RANK), row(W)] + [row(HD)] * 5
                 + [vec(Q_RANK), vec(KV_RANK)],
        out_specs=[row(D_INP), vec(Q_RANK), vec(KV_RANK)],
        compiler_params=_params(("arbitrary",)),
    )(proj, dq, dk, dv, drg, dcqn, dckvn, dkr8, *tabs, qn, kvn)


def _post(o_ret, proj, gn):
    L, W = o_ret.shape
    tr = _tile(L, 384)

    def body(o_ref, rg_ref, gn_ref, out_ref):
        for h in range(HEADS):
            sl = slice(h * HD, (h + 1) * HD)
            o = o_ref[:, sl]
            rg = rg_ref[:, sl].astype(F32)
            n = o * lax.rsqrt(jnp.mean(o * o, axis=-1, keepdims=True) + EPS)
            out_ref[:, sl] = (n * gn_ref[:, sl] * (rg * _sigmoid(rg))).astype(BF16)

    row = pl.BlockSpec((tr, W), lambda i: (i, 0))
    return pl.pallas_call(
        body, name="ret_post", grid=(L // tr,), out_shape=jax.ShapeDtypeStruct((L, W), BF16),
        in_specs=[row, pl.BlockSpec((tr, W), lambda i: (i, C_RG // W)), pl.BlockSpec((1, W), lambda i: (0, 0))],
        out_specs=row, compiler_params=_params(("parallel",)),
    )(o_ret, proj, gn)


def _post_bwd(o_ret, proj, gn, dcat):
    L, W = o_ret.shape
    tr = _tile(L, 384)

    def body(o_ref, rg_ref, gn_ref, d_ref, do_ref, drg_ref, dgn_ref):
        @pl.when(pl.program_id(0) == 0)
        def _():
            dgn_ref[...] = jnp.zeros_like(dgn_ref)

        for h in range(HEADS):
            sl = slice(h * HD, (h + 1) * HD)
            o = o_ref[:, sl]
            rg = rg_ref[:, sl].astype(F32)
            d = d_ref[:, sl].astype(F32)
            gw = gn_ref[:, sl]
            r = lax.rsqrt(jnp.mean(o * o, axis=-1, keepdims=True) + EPS)
            n = o * r
            sg = _sigmoid(rg)
            si = rg * sg
            dn = d * gw * si
            dgn_ref[:, sl] += jnp.sum(d * n * si, axis=0, keepdims=True)
            drg_ref[:, sl] = (d * n * gw * sg * (1.0 + rg * (1.0 - sg))).astype(drg_ref.dtype)
            do_ref[:, sl] = (r * (dn - o * (r * r) * jnp.mean(dn * o, axis=-1, keepdims=True))).astype(BF16)

    row = pl.BlockSpec((tr, W), lambda i: (i, 0))
    vec = pl.BlockSpec((1, W), lambda i: (0, 0))
    return pl.pallas_call(
        body, name="ret_post_bwd", grid=(L // tr,),
        out_shape=[jax.ShapeDtypeStruct((L, W), BF16), jax.ShapeDtypeStruct((L, W), BF16),
                   jax.ShapeDtypeStruct((1, W), F32)],
        in_specs=[row, pl.BlockSpec((tr, W), lambda i: (i, C_RG // W)), vec, row],
        out_specs=[row, row, vec], compiler_params=_params(("arbitrary",)),
    )(o_ret, proj, gn, dcat)


RET_HEADS_PER_STEP = 4


def _lin_attn(name, q, k, v, lg, reverse, out_dtype=F32):
    L, W = q.shape
    nc = L // BLK - 1
    G = RET_HEADS_PER_STEP

    def body(q_ref, k_ref, v_ref, lg_ref, o_ref, s_ref):
        n = lax.broadcasted_iota(jnp.int32, (BLK, BLK), 0).astype(F32)
        m = lax.broadcasted_iota(jnp.int32, (BLK, BLK), 1).astype(F32)
        dist = (m - n) if reverse else (n - m)
        consts = []
        for g in range(G):
            lgv = lg_ref[g, 0:1, :]
            dmask = jnp.where(dist >= 0, jnp.exp(lgv * jnp.maximum(dist, 0.0)), 0.0)
            c = dict(dmask=dmask, dmask0=jnp.where((n < N_META) & (m < N_META), dmask, 0.0),
                     gl=jnp.exp(lgv * float(BLK)))
            if reverse:
                c.update(inter=jnp.exp(lgv * (float(BLK) - n)), upd=jnp.exp(lgv * n),
                         inter0=jnp.where(n < N_META, jnp.exp(lgv * jnp.maximum(float(N_META) - n, 0.0)), 0.0))
            else:
                c.update(inter=jnp.exp(lgv * (n + 1.0)), upd=jnp.exp(lgv * (float(BLK) - 1.0 - n)),
                         upd0=jnp.where(n < N_META, jnp.exp(lgv * jnp.maximum(float(N_META) - 1.0 - n, 0.0)), 0.0))
            consts.append(c)

        def chunk(c):
            rows = pl.ds(pl.multiple_of(c * BLK, BLK), BLK)
            state = [s_ref[g] for g in range(G)]
            outs, new_state = [], []
            for g in range(G):
                cols = slice(g * HD, (g + 1) * HD)
                cg = consts[g]
                qc, kc, vc = q_ref[rows, cols], k_ref[rows, cols], v_ref[rows, cols]
                a = _dot(qc, kc, 'nt') * cg['dmask']
                outs.append(_dot(a.astype(BF16), vc, 'nn') + _dot(qc, state[g].astype(BF16), 'nn') * cg['inter'])
                new_state.append(state[g] * cg['gl'] + _dot((kc.astype(F32) * cg['upd']).astype(BF16), vc, 'tn'))
            for g in range(G):
                o_ref[rows, g * HD:(g + 1) * HD] = outs[g].astype(o_ref.dtype)
                s_ref[g] = new_state[g]

        def first_chunk(with_state):
            for g in range(G):
                cols = slice(g * HD, (g + 1) * HD)
                cg = consts[g]
                q0, k0, v0 = q_ref[0:BLK, cols], k_ref[0:BLK, cols], v_ref[0:BLK, cols]
                o0 = _dot((_dot(q0, k0, 'nt') * cg['dmask0']).astype(BF16), v0, 'nn')
                if with_state:
                    o0 = o0 + _dot(q0, s_ref[g].astype(BF16), 'nn') * cg['inter0']
                else:
                    s_ref[g] = _dot((k0.astype(F32) * cg['upd0']).astype(BF16), v0, 'tn')
                o_ref[0:BLK, cols] = o0.astype(o_ref.dtype)

        if reverse:
            s_ref[...] = jnp.zeros_like(s_ref)

            def step(t, carry):
                chunk(nc - t)
                return carry

            lax.fori_loop(0, nc, step, 0)
            first_chunk(True)
        else:
            first_chunk(False)

            def step(t, carry):
                chunk(t + 1)
                return carry

            lax.fori_loop(0, nc, step, 0)

    col = pl.BlockSpec((L, G * HD), lambda h: (0, h))
    return pl.pallas_call(
        body, name=name, grid=(HEADS // G,), out_shape=jax.ShapeDtypeStruct((L, W), out_dtype),
        in_specs=[col, col, col, pl.BlockSpec((G, 8, HD), lambda h: (h, 0, 0))], out_specs=col,
        scratch_shapes=[pltpu.VMEM((G, HD, HD), F32)], compiler_params=_params(("parallel",)),
    )(q, k, v, lg)


ATT_SCALE = (HD + ROPE) ** -0.5
LOG2E = 1.4426950408889634
Q_PRESCALE = ATT_SCALE * LOG2E
NEG = -1e30


ATT_TILE = 384
ATT_HEADS_PER_STEP = 2


def _att_valid(nq, nk, row0, col0):
    r = lax.broadcasted_iota(jnp.int32, (nq, nk), 0) + row0
    c = lax.broadcasted_iota(jnp.int32, (nq, nk), 1) + col0
    return (c <= r) & ((c < N_META) | (c >= BLK))


def _attn_fwd(qm, kn, krr, vm, comm=None):
    L = qm.shape[0]
    W = HEADS * HD
    T = _tile(L, ATT_TILE, BLK)
    nb = L // T
    G = ATT_HEADS_PER_STEP
    n_cm = comm.n if comm is not None else 0

    def body(*refs):
        q_ref, kn_ref, kr_ref, v_ref = refs[:4]
        o_ref, lse_ref = refs[4 + n_cm:6 + n_cm]
        m_sc, l_sc, acc_sc = refs[6 + 2 * n_cm:9 + 2 * n_cm]
        if comm is not None:
            cm_refs = (refs[4:4 + n_cm], refs[6 + n_cm:6 + 2 * n_cm], refs[9 + 2 * n_cm:])
            first, last = _grid_edges((HEADS // G, nb))

            @pl.when(first)
            def _():
                comm.start(*cm_refs)

        i = pl.program_id(1)
        m_sc[...] = jnp.full_like(m_sc, NEG)
        l_sc[...] = jnp.zeros_like(l_sc)
        acc_sc[...] = jnp.zeros_like(acc_sc)

        def tile(j, masked):
            rows = pl.ds(pl.multiple_of(j * T, T), T)
            kr = kr_ref[rows, :]
            valid = _att_valid(T, T, i * T, j * T) if masked else None
            m_prev = [m_sc[g] for g in range(G)]
            l_prev = [l_sc[g] for g in range(G)]
            acc_prev = [acc_sc[g] for g in range(G)]
            m_new, l_new, acc_new = [], [], []
            for g in range(G):
                k = jnp.concatenate([kn_ref[rows, g * HD:(g + 1) * HD], kr], axis=1)
                s = _dot(q_ref[:, g * QH:(g + 1) * QH], k, 'nt')
                if masked:
                    s = jnp.where(valid, s, NEG)
                m_new.append(jnp.maximum(m_prev[g], jnp.max(s, axis=-1, keepdims=True)))
                p = jnp.exp2(s - m_new[g])
                alpha = jnp.exp2(m_prev[g] - m_new[g])
                l_new.append(alpha * l_prev[g] + jnp.sum(p, axis=-1, keepdims=True))
                acc_new.append(alpha * acc_prev[g] + _dot(p.astype(BF16), v_ref[rows, g * HD:(g + 1) * HD], 'nn'))
            for g in range(G):
                m_sc[g] = m_new[g]
                l_sc[g] = l_new[g]
                acc_sc[g] = acc_new[g]

        tile(0, True)

        def mid(j, carry):
            tile(j, False)
            return carry

        lax.fori_loop(1, i, mid, 0)

        @pl.when(i > 0)
        def _():
            tile(i, True)

        for g in range(G):
            l = l_sc[g]
            o_ref[:, g * HD:(g + 1) * HD] = (acc_sc[g] / l).astype(o_ref.dtype)
            lse_ref[g] = jnp.broadcast_to(m_sc[g] + jnp.log(l) * LOG2E, (T, HD))

        if comm is not None:
            @pl.when(last)
            def _():
                comm.finish(*cm_refs)

    cm_specs = comm.specs if comm is not None else []
    res = pl.pallas_call(
        body, name="attn_fwd", grid=(HEADS // G, nb),
        out_shape=[jax.ShapeDtypeStruct((L, W), BF16), jax.ShapeDtypeStruct((HEADS, L, HD), F32)]
        + (comm.out_shapes if comm is not None else []),
        in_specs=[pl.BlockSpec((T, G * QH), lambda h, i: (i, h)), pl.BlockSpec((L, G * HD), lambda h, i: (0, h)),
                  pl.BlockSpec((L, HD), lambda h, i: (0, 0)), pl.BlockSpec((L, G * HD), lambda h, i: (0, h))]
        + cm_specs,
        out_specs=[pl.BlockSpec((T, G * HD), lambda h, i: (i, h)),
                   pl.BlockSpec((G, T, HD), lambda h, i: (h, i, 0))] + cm_specs,
        scratch_shapes=[pltpu.VMEM((G, T, 1), F32), pltpu.VMEM((G, T, 1), F32), pltpu.VMEM((G, T, HD), F32)]
        + (comm.scratch if comm is not None else []),
        compiler_params=_params(("arbitrary", "arbitrary")),
    )(qm, kn, krr, vm, *(comm.arrays if comm is not None else []))
    return res[:2], res[2:]


def _attn_bwd(qm, kn, krr, vm, o, dcat, lse, comm=None):
    L = qm.shape[0]
    W = HEADS * HD
    T = _tile(L, ATT_TILE, BLK)
    nb = L // T
    n_cm = comm.n if comm is not None else 0

    def body(*refs):
        q_ref, kn_ref, kr_ref, v_ref, o_ref, do_ref, lse_ref = refs[:7]
        dq_ref, dkn_ref, dkr_ref, dv_ref = refs[7 + n_cm:11 + n_cm]
        dl_sc, dk_sc, dv_sc = refs[11 + 2 * n_cm:14 + 2 * n_cm]
        if comm is not None:
            cm_refs = (refs[7:7 + n_cm], refs[11 + n_cm:11 + 2 * n_cm], refs[14 + 2 * n_cm:])
            first, last = _grid_edges((HEADS, nb))

            @pl.when(first)
            def _():
                comm.start(*cm_refs)

        j = pl.program_id(1)

        @pl.when(j == 0)
        def _():
            dq_ref[...] = jnp.zeros_like(dq_ref)

            def rowsum(t, carry):
                rows = pl.ds(pl.multiple_of(t * T, T), T)
                dl_sc[rows, :] = jnp.sum(do_ref[rows, :].astype(F32) * o_ref[rows, :].astype(F32), axis=-1,
                                         keepdims=True)
                return carry

            lax.fori_loop(0, nb, rowsum, 0)

        k = jnp.concatenate([kn_ref[...], kr_ref[...]], axis=1)
        v = v_ref[...]
        dk_sc[...] = jnp.zeros_like(dk_sc)
        dv_sc[...] = jnp.zeros_like(dv_sc)

        def tile(i, masked):
            rows = pl.ds(pl.multiple_of(i * T, T), T)
            q = q_ref[rows, :]
            do = do_ref[rows, :]
            s = _dot(q, k, 'nt')
            if masked:
                s = jnp.where(_att_valid(T, T, i * T, j * T), s, NEG)
            p = jnp.exp2(s - lse_ref[rows, 0:1])
            dv_sc[...] += _dot(p.astype(BF16), do, 'tn')
            ds = (p * (_dot(do, v, 'nt') - dl_sc[rows, :])).astype(BF16)
            dk_sc[...] += _dot(ds, q, 'tn')
            dq_ref[rows, :] += _dot(ds, k, 'nn')

        tile(j, True)

        def rest(masked):
            def step(i, carry):
                tile(i, masked)
                return carry
            lax.fori_loop(j + 1, nb, step, 0)

        @pl.when(j == 0)
        def _():
            rest(True)

        @pl.when(j > 0)
        def _():
            rest(False)

        dk = dk_sc[...] * (1.0 / LOG2E)
        dkn_ref[...] = dk[:, 0:HD].astype(BF16)
        dkr_ref[...] = dk[:, HD:QH].astype(dkr_ref.dtype)
        dv_ref[...] = dv_sc[...].astype(BF16)

        if comm is not None:
            @pl.when(last)
            def _():
                comm.finish(*cm_refs)

    blk = pl.BlockSpec((T, HD), lambda h, j: (j, h))
    cm_specs = comm.specs if comm is not None else []
    res = pl.pallas_call(
        body, name="attn_bwd", grid=(HEADS, nb),
        out_shape=[jax.ShapeDtypeStruct((L, HEADS * QH), F32), jax.ShapeDtypeStruct((L, W), BF16),
                   jax.ShapeDtypeStruct((L, W), BF16), jax.ShapeDtypeStruct((L, W), BF16)]
        + (comm.out_shapes if comm is not None else []),
        in_specs=[pl.BlockSpec((L, QH), lambda h, j: (0, h)), blk, pl.BlockSpec((T, HD), lambda h, j: (j, 0)), blk,
                  pl.BlockSpec((L, HD), lambda h, j: (0, h)), pl.BlockSpec((L, HD), lambda h, j: (0, HEADS + h)),
                  pl.BlockSpec((None, L, HD), lambda h, j: (h, 0, 0))] + cm_specs,
        out_specs=[pl.BlockSpec((L, QH), lambda h, j: (0, h)), blk, blk, blk] + cm_specs,
        scratch_shapes=[pltpu.VMEM((L, 1), F32), pltpu.VMEM((T, QH), F32), pltpu.VMEM((T, HD), F32)]
        + (comm.scratch if comm is not None else []),
        compiler_params=_params(("arbitrary", "arbitrary")),
    )(qm, kn, krr, vm, o, dcat, lse, *(comm.arrays if comm is not None else []))
    return res[:4], res[4:]


def _unrope_q(dqm, tabs_m):
    L, W = dqm.shape
    tr = _tile(L, 384)

    def body(d_ref, cm_ref, sa_ref, sb_ref, out_ref):
        cm, sa, sb = cm_ref[...], sa_ref[...], sb_ref[...]
        for h in range(HEADS):
            out_ref[:, h * QH:h * QH + HD] = (d_ref[:, h * QH:h * QH + HD] * ATT_SCALE).astype(BF16)
            out_ref[:, h * QH + HD:(h + 1) * QH] = _rope_mla_t(d_ref[:, h * QH + HD:(h + 1) * QH] * ATT_SCALE, cm, sa,
                                                               sb).astype(BF16)

    row = pl.BlockSpec((tr, W), lambda i: (i, 0))
    tab = pl.BlockSpec((tr, HD), lambda i: (i, 0))
    return pl.pallas_call(
        body, name="unrope_q", grid=(L // tr,), out_shape=jax.ShapeDtypeStruct((L, W), BF16),
        in_specs=[row, tab, tab, tab], out_specs=row, compiler_params=_params(("parallel",)),
    )(dqm, *tabs_m)


def _q_up(cqn, wuq_p, tabs_m):
    L = cqn.shape[0]
    tm = _tile(L, 704)

    def ep(acc, cm, sa, sb):
        acc = acc * Q_PRESCALE
        parts = []
        for h in range(HEADS):
            parts.append(acc[:, h * QH:h * QH + HD])
            parts.append(_rope_mla(acc[:, h * QH + HD:(h + 1) * QH], cm, sa, sb))
        return (jnp.concatenate(parts, axis=1),)

    tab = pl.BlockSpec((tm, HD), lambda i, j: (i, 0))
    return _mm("mla_q_up", (L // tm, 1), ("parallel", "parallel"), None,
               [cqn, wuq_p], [pl.BlockSpec((tm, Q_RANK), lambda i, j: (i, 0)),
                              pl.BlockSpec((HEADS * QH, Q_RANK), lambda i, j: (0, 0))],
               [(0, 1, 'nt', 0)], [(tm, HEADS * QH)], list(tabs_m), [tab] * 3, ep,
               [jax.ShapeDtypeStruct((L, HEADS * QH), BF16)], [pl.BlockSpec((tm, HEADS * QH), lambda i, j: (i, 0))])[0]


def _mix_out(cat, w_out, h_in, post, next_norm):
    L, K = cat.shape
    D = w_out.shape[1]
    tm, tk = _tile(L, 384), K
    row = pl.BlockSpec((tm, D), lambda i, k: (i, 0))
    vec = pl.BlockSpec((1, D), lambda i, k: (0, 0))
    return _mm("mix_out", (L // tm, K // tk), ("parallel", "arbitrary"), 1,
               [cat, w_out], [pl.BlockSpec((tm, tk), lambda i, k: (i, k)), pl.BlockSpec((tk, D), lambda i, k: (k, 0))],
               [(0, 1, 'nn', 0)], [(tm, D)], [h_in, post, next_norm], [row, vec, vec], _resnorm_epilogue(1.0, True),
               [jax.ShapeDtypeStruct((L, D), F32)] * 2 + [jax.ShapeDtypeStruct((L, D), BF16)], [row, row, row])


ADAM_BLOCK_ELEMS = 512 * 704


def _adam_math(w, g, m, v):
    m = ADAM_B1 * m + (1.0 - ADAM_B1) * g
    v = ADAM_B2 * v + (1.0 - ADAM_B2) * (g * g)
    m_hat = m / (1.0 - ADAM_B1 ** ADAM_STEP)
    v_hat = v / (1.0 - ADAM_B2 ** ADAM_STEP)
    delta = -ADAM_LR * (m_hat / (jnp.sqrt(v_hat) + ADAM_EPS) + ADAM_WD * w)
    return delta, m, v


def _adam(name, w, m, v, g_slots=None, g=None):
    R, C = w.shape
    tr, tc = _tile(R, max(16, ADAM_BLOCK_ELEMS // C // 16 * 16), 16), C
    if tr * tc > ADAM_BLOCK_ELEMS:
        tr, tc = R, _tile(C, max(128, ADAM_BLOCK_ELEMS // R // 128 * 128), 128)
    from_slots = g_slots is not None

    def body(w_ref, m_ref, v_ref, g_ref, go_ref, d_ref, mo_ref, vo_ref):
        if from_slots:
            grad = g_ref[0].astype(F32)
            for s in range(1, N_DEV):
                grad = grad + g_ref[s].astype(F32)
        else:
            grad = g_ref[...]
        delta, mn, vn = _adam_math(w_ref[...], grad, m_ref[...], v_ref[...])
        go_ref[...] = grad
        d_ref[...] = delta
        mo_ref[...] = mn
        vo_ref[...] = vn

    row = pl.BlockSpec((tr, tc), lambda i, j: (i, j))
    gspec = pl.BlockSpec((N_DEV, tr, tc), lambda i, j: (0, i, j)) if from_slots else row
    return pl.pallas_call(
        body, name=name, grid=(R // tr, C // tc), out_shape=[jax.ShapeDtypeStruct((R, C), F32)] * 4,
        in_specs=[row, row, row, gspec], out_specs=[row] * 4, compiler_params=_params(("parallel", "parallel")),
    )(w, m, v, g_slots if from_slots else g)


def _unblock(gathered):
    n, r, c = gathered.shape
    return jnp.transpose(gathered, (1, 0, 2)).reshape(r, n * c)


def _reblock(full, c):
    r = full.shape[0]
    return jnp.transpose(full[:, :N_DEV * c].reshape(r, N_DEV, c), (1, 0, 2))


def _step(x, target, w, mom, vel):
    S, D = x.shape[1], x.shape[2]
    L = S + BLK
    def sq(a, n):
        if a.ndim == 2:
            return a
        if n in TRANSPOSED:
            a = jnp.swapaxes(a, 1, 2)
        return a.reshape(a.shape[1:])

    def unsq(o, n):
        o = o.reshape((1,) + o.shape)
        return jnp.swapaxes(o, 1, 2) if n in TRANSPOSED else o

    p = {n: sq(w[n], n) for n in WEIGHTS if n != 'meta_tokens'}
    gather = lambda names: _Exchange([p[n].astype(BF16) for n in names], False)
    in_s, uq_s = p['w_in'].shape[0], p['mla_w_uq'].shape[0]
    assert uq_s == HD + ROPE and N_DEV == HEADS, "a w_uq shard is one head's columns"
    tabs = _rope_tables(L)
    tabs_m = tabs[2:]
    lg = jnp.broadcast_to(jnp.log(1.0 - 2.0 ** (-5.0 - jnp.arange(HEADS, dtype=F32)))[:, None, None], (HEADS, 8, HD))

    wg1, meta = _exchange("gather_first", [p['ffn1_w_gate'].astype(BF16), w['meta_tokens']], False)
    h0 = jnp.concatenate([_unblock(meta), jnp.zeros((BLK - N_META, D), F32), x[0]], axis=0)
    a1 = _norm_fwd(h0, p['ffn1_pre_norm'])
    g1, (wu1,) = _ffn_gate(a1, wg1, comm=gather(['ffn1_w_up']))
    (u1, hid1), (wd1,) = _ffn_up_gated(a1, wu1, g1, comm=gather(['ffn1_w_down']))
    (f1, h1, um), (w_in_g,) = _ffn_down(hid1, wd1, h0, p['ffn1_post_norm'], next_norm=p['mix_pre_norm'],
                                        comm=gather(['w_in']))

    w_in = jnp.pad(w_in_g.reshape(N_DEV * in_s, D), ((0, D_INP - N_DEV * in_s), (0, 0)))
    proj, (uq_g, uk_g, uv_g, wout_g) = _mm_nt("mix_in", [(um, w_in)], BF16, tn_target=1664,
                                              comm=gather(['mla_w_uq', 'mla_w_uk', 'mla_w_uv', 'w_out']))
    wuq = jnp.pad(uq_g, ((0, 0), (0, QH - uq_s), (0, 0))).reshape(HEADS * QH, Q_RANK)
    wuk, wuv, w_out = _unblock(uk_g), _unblock(uv_g), wout_g.reshape(-1, D)
    qr, kr, vr, cqn, ckvn, krr = _prep(proj, tabs, p['mla_q_norm'], p['mla_kv_norm'])
    qm = _q_up(cqn, wuq, tabs_m)
    kn = _mm_nn("mla_k_up", ckvn, wuk, BF16)
    vm = _mm_nn("mla_v_up", ckvn, wuv, BF16)
    (o_mla, lse), (wg2, wu2) = _attn_fwd(qm, kn, krr, vm, comm=gather(['ffn2_w_gate', 'ffn2_w_up']))
    o_ret = _lin_attn("ret_fwd", qr, kr, vr, lg, False)
    ret = _post(o_ret, proj, p['ret_group_norm'])
    cat = jnp.concatenate([ret, o_mla], axis=1)
    m, h2, a2 = _mix_out(cat, w_out, h1, p['mix_post_norm'], p['ffn2_pre_norm'])

    (g2, u2, hid2), (wd2,) = _ffn_up(a2, wg2, wu2, comm=gather(['ffn2_w_down']))
    f2, h3 = _ffn_down(hid2, wd2, h2, p['ffn2_post_norm'])
    dh3, loss_blk = _loss(h3, target[0])

    dsmall = {}
    df2, dsmall['ffn2_post_norm'] = _norm_bwd(f2, p['ffn2_post_norm'], dh3, None, 0.5, BF16)
    dg2, du2 = _ffn_dhid(df2, wd2, g2, u2)
    dwd2 = _ffn_dwd(hid2, df2)
    s1 = _scatter_start("scatter_start_ffn2_down", [dwd2])
    dwg2, dwu2 = _ffn_dwgu(a2, dg2, du2, after=[s1['token']])
    s2 = _scatter_start("scatter_start_ffn2_gate_up", [dwg2, dwu2])
    da2 = _ffn_da(dg2, du2, wg2, wu2, after=[s2['token']])
    dh2, dsmall['ffn2_pre_norm'] = _norm_bwd(h2, p['ffn2_pre_norm'], da2, dh3, 1.0, F32)

    dm, dsmall['mix_post_norm'] = _norm_bwd(m, p['mix_post_norm'], dh2, None, 1.0, BF16)
    dcat = _mm_nt("mix_dcat", [(dm, w_out)], BF16)
    dwout = _mm_tn("mix_dwout", cat, [dm])[0]
    do_ret, drg, dsmall['ret_group_norm'] = _post_bwd(o_ret, proj, p['ret_group_norm'], dcat)
    dqr = _lin_attn("ret_dq", do_ret, vr, kr, lg, False, BF16)
    dkr = _lin_attn("ret_dk", vr, do_ret, qr, lg, True, BF16)
    dvr = _lin_attn("ret_dv", kr, qr, do_ret, lg, True, BF16)
    (dqm, dkn, dkr8, dvm), _ = _attn_bwd(qm, kn, krr, vm, o_mla, dcat, lse)
    dqp = _unrope_q(dqm, tabs_m)
    dwuq = _mm_tn("mla_dwuq", dqp, [cqn])[0]
    dcqn = _mm_nn("mla_dcq", dqp, wuq, F32)
    dwuk, dwuv = _mm_tn("mla_dwukv", ckvn, [dkn, dvm])
    s3 = _scatter_start("scatter_start_mixer", [
        dwout.reshape(N_DEV, -1, D), dwuq.reshape(HEADS, QH, Q_RANK)[:, :uq_s],
        _reblock(dwuk, p['mla_w_uk'].shape[1]), _reblock(dwuv, p['mla_w_uv'].shape[1])])
    dckvn = _mm_nt("mla_dckv", [(dkn, wuk), (dvm, wuv)], F32, after=[s3['token']])
    dproj, dsmall['mla_q_norm'], dsmall['mla_kv_norm'] = _prep_bwd(
        proj, dqr, dkr, dvr, drg, dcqn, dckvn, dkr8, tabs, p['mla_q_norm'], p['mla_kv_norm'])
    dwin = _mm_tn("mix_dwin", dproj, [um])[0]
    s4 = _scatter_start("scatter_start_w_in", [dwin[:N_DEV * in_s].reshape(N_DEV, in_s, D)])
    dum = _mm_nn("mix_du", dproj, w_in, F32, tn_target=512, after=[s4['token']])
    dh1, dsmall['mix_pre_norm'] = _norm_bwd(h1, p['mix_pre_norm'], dum, dh2, 1.0, F32)

    df1, dsmall['ffn1_post_norm'] = _norm_bwd(f1, p['ffn1_post_norm'], dh1, None, 0.5, BF16)
    dg1, du1 = _ffn_dhid(df1, wd1, g1, u1)
    dwd1 = _ffn_dwd(hid1, df1)
    s5 = _scatter_start("scatter_start_ffn1_down", [dwd1])
    dwg1, dwu1 = _ffn_dwgu(a1, dg1, du1, after=[s5['token']])
    s6 = _scatter_start("scatter_start_ffn1_gate_up", [dwg1, dwu1])
    da1 = _ffn_da(dg1, du1, wg1, wu1, after=[s6['token']])
    dh0, dsmall['ffn1_pre_norm'] = _norm_bwd(h0, p['ffn1_pre_norm'], da1, dh1, 1.0, F32)

    def slab(a):
        a = a.reshape(-1, 128)
        return jnp.pad(a, ((0, (-a.shape[0]) % 8), (0, 0)))

    slab_rows = lambda n: -(-(p[n].shape[-1] // 128) // 8) * 8
    packed = jnp.concatenate([slab(dsmall[n]) for n in SMALL] + [slab(dh0[:N_META]), loss_blk], axis=0)
    red = _allreduce_small(packed)
    offs = sum(slab_rows(n) for n in SMALL)
    n_small = offs
    gmeta_full = red[offs:offs + N_META * D // 128].reshape(N_META, D)
    offs += N_META * D // 128
    loss = red[offs, 0]

    early = ['ffn2_w_down', 'ffn2_w_gate', 'ffn2_w_up', 'w_out', 'mla_w_uq', 'mla_w_uk', 'mla_w_uv', 'w_in']
    late = ['ffn1_w_down', 'ffn1_w_gate', 'ffn1_w_up']
    grad, delta, new_m, new_v = {}, {}, {}, {}
    meanwhile = []

    def update_big(names, slots):
        for n, g_slots in zip(names, slots):
            outs = _adam("adam_" + n, p[n], sq(mom[n], n), sq(vel[n], n), g_slots=g_slots)
            meanwhile.append(outs[0])
            grad[n], delta[n], new_m[n], new_v[n] = [unsq(o, n) for o in outs]

    update_big(early, _scatter_wait("scatter_wait_early", [s1, s2, s3, s4], [red]))
    pack = lambda d: jnp.concatenate([slab(d[n]) for n in SMALL], axis=0)
    outs = _adam("adam_small", pack(w), pack(mom), pack(vel), g=red[:n_small])
    meanwhile.append(outs[0])
    offs = 0
    for n in SMALL:
        r = p[n].shape[-1] // 128
        grad[n], delta[n], new_m[n], new_v[n] = [o[offs:offs + r].reshape(w[n].shape) for o in outs]
        offs += slab_rows(n)
    dev = 4 * lax.axis_index("x") + 2 * lax.axis_index("y") + lax.axis_index("c")
    mcols = w['meta_tokens'].shape[1]
    gmeta = lax.dynamic_slice(gmeta_full, (0, dev * mcols), (N_META, mcols))
    outs = _adam("adam_meta", w['meta_tokens'], mom['meta_tokens'], vel['meta_tokens'], g=gmeta)
    grad['meta_tokens'], delta['meta_tokens'], new_m['meta_tokens'], new_v['meta_tokens'] = outs
    meanwhile.append(outs[0])
    update_big(late, _scatter_wait("scatter_wait_late", [s5, s6], list(meanwhile)))

    return (loss, dh0[BLK:][None], *[grad[n] for n in WEIGHTS], *[delta[n] for n in WEIGHTS],
            *[new_m[n] for n in WEIGHTS], *[new_v[n] for n in WEIGHTS])


def kernel(x, meta_tokens, ffn1_pre_norm, ffn1_w_gate, ffn1_w_up, ffn1_w_down, ffn1_post_norm, mix_pre_norm, w_in, ret_group_norm, mla_q_norm, mla_w_uq, mla_kv_norm, mla_w_uk, mla_w_uv, w_out, mix_post_norm, ffn2_pre_norm, ffn2_w_gate, ffn2_w_up, ffn2_w_down, ffn2_post_norm, loss_target, m_meta_tokens, m_ffn1_pre_norm, m_ffn1_w_gate, m_ffn1_w_up, m_ffn1_w_down, m_ffn1_post_norm, m_mix_pre_norm, m_w_in, m_ret_group_norm, m_mla_q_norm, m_mla_w_uq, m_mla_kv_norm, m_mla_w_uk, m_mla_w_uv, m_w_out, m_mix_post_norm, m_ffn2_pre_norm, m_ffn2_w_gate, m_ffn2_w_up, m_ffn2_w_down, m_ffn2_post_norm, v_meta_tokens, v_ffn1_pre_norm, v_ffn1_w_gate, v_ffn1_w_up, v_ffn1_w_down, v_ffn1_post_norm, v_mix_pre_norm, v_w_in, v_ret_group_norm, v_mla_q_norm, v_mla_w_uq, v_mla_kv_norm, v_mla_w_uk, v_mla_w_uv, v_w_out, v_mix_post_norm, v_ffn2_pre_norm, v_ffn2_w_gate, v_ffn2_w_up, v_ffn2_w_down, v_ffn2_post_norm):
    w = dict(zip(WEIGHTS, (meta_tokens, ffn1_pre_norm, ffn1_w_gate, ffn1_w_up, ffn1_w_down, ffn1_post_norm,
                           mix_pre_norm, w_in, ret_group_norm, mla_q_norm, mla_w_uq, mla_kv_norm, mla_w_uk, mla_w_uv,
                           w_out, mix_post_norm, ffn2_pre_norm, ffn2_w_gate, ffn2_w_up, ffn2_w_down, ffn2_post_norm)))
    mom = dict(zip(WEIGHTS, (m_meta_tokens, m_ffn1_pre_norm, m_ffn1_w_gate, m_ffn1_w_up, m_ffn1_w_down,
                             m_ffn1_post_norm, m_mix_pre_norm, m_w_in, m_ret_group_norm, m_mla_q_norm, m_mla_w_uq,
                             m_mla_kv_norm, m_mla_w_uk, m_mla_w_uv, m_w_out, m_mix_post_norm, m_ffn2_pre_norm,
                             m_ffn2_w_gate, m_ffn2_w_up, m_ffn2_w_down, m_ffn2_post_norm)))
    vel = dict(zip(WEIGHTS, (v_meta_tokens, v_ffn1_pre_norm, v_ffn1_w_gate, v_ffn1_w_up, v_ffn1_w_down,
                             v_ffn1_post_norm, v_mix_pre_norm, v_w_in, v_ret_group_norm, v_mla_q_norm, v_mla_w_uq,
                             v_mla_kv_norm, v_mla_w_uk, v_mla_w_uv, v_w_out, v_mix_post_norm, v_ffn2_pre_norm,
                             v_ffn2_w_gate, v_ffn2_w_up, v_ffn2_w_down, v_ffn2_post_norm)))
    return _step(x, loss_target, w, mom, vel)
```

```python
import functools
import math

import jax
import jax.numpy as jnp
from jax import lax
from jax.experimental import pallas as pl
from jax.experimental.pallas import tpu as pltpu

N_DEV = 8
N_META = 16
BLK = 128
HEADS = 8
HD = 128
ROPE = 64
Q_RANK = 512
KV_RANK = 256
QH = 2 * HD
D_INP = 4 * HEADS * HD + Q_RANK + KV_RANK + BLK
ROPE_THETA = 10000.0
EPS = 1e-6
ADAM_LR = 0.001
ADAM_B1 = 0.9
ADAM_B2 = 0.999
ADAM_EPS = 1e-08
ADAM_WD = 0.01
ADAM_STEP = 10
V7X_VMEM_LIMIT = 48 * 1024 * 1024
MESH = pl.DeviceIdType.MESH
F32 = jnp.float32
BF16 = jnp.bfloat16

WEIGHTS = ['meta_tokens', 'ffn1_pre_norm', 'ffn1_w_gate', 'ffn1_w_up', 'ffn1_w_down', 'ffn1_post_norm',
           'mix_pre_norm', 'w_in', 'ret_group_norm', 'mla_q_norm', 'mla_w_uq', 'mla_kv_norm', 'mla_w_uk',
           'mla_w_uv', 'w_out', 'mix_post_norm', 'ffn2_pre_norm', 'ffn2_w_gate', 'ffn2_w_up', 'ffn2_w_down',
           'ffn2_post_norm']
SMALL = ['ffn1_pre_norm', 'ffn1_post_norm', 'mix_pre_norm', 'ret_group_norm', 'mla_q_norm', 'mla_kv_norm',
         'mix_post_norm', 'ffn2_pre_norm', 'ffn2_post_norm']
TRANSPOSED = ('ffn1_w_gate', 'ffn1_w_up', 'ffn2_w_gate', 'ffn2_w_up', 'w_in', 'mla_w_uq')
BIG = ['ffn1_w_gate', 'ffn1_w_up', 'ffn1_w_down', 'w_in', 'mla_w_uq', 'mla_w_uk', 'mla_w_uv', 'w_out',
       'ffn2_w_gate', 'ffn2_w_up', 'ffn2_w_down']

_DIMS = {'nn': (((1,), (0,)), ((), ())), 'nt': (((1,), (1,)), ((), ())), 'tn': (((0,), (0,)), ((), ()))}


def _tile(n, target, mult=16):
    best = None
    for t in range(mult, min(n, target) + 1, mult):
        if n % t == 0:
            best = t
    return best if best is not None else n


def _params(sem):
    return pltpu.CompilerParams(dimension_semantics=sem, vmem_limit_bytes=V7X_VMEM_LIMIT)


def _dot(a, b, dims):
    return lax.dot_general(a, b, _DIMS[dims], preferred_element_type=F32)


def _sigmoid(x):
    return 0.5 * jnp.tanh(0.5 * x) + 0.5


def _me_and_peers():
    x, y, c = lax.axis_index("x"), lax.axis_index("y"), lax.axis_index("c")

    def peer(j):
        px = 1 - x if (j >> 2) & 1 else x
        py = 1 - y if (j >> 1) & 1 else y
        pc = 1 - c if j & 1 else c
        return (px, py, pc), 4 * px + 2 * py + pc

    return 4 * x + 2 * y + c, peer


class _Exchange:
    def __init__(self, arrays, per_peer):
        self.arrays = list(arrays)
        self.per_peer = per_peer
        self.n = len(self.arrays)
        self.out_shapes = [jax.ShapeDtypeStruct((N_DEV,) + tuple(a.shape[1:] if per_peer else a.shape), a.dtype)
                           for a in self.arrays]
        self.specs = [pl.BlockSpec(memory_space=pl.ANY)] * self.n
        self.scratch = [pltpu.SemaphoreType.DMA((7 * self.n,)), pltpu.SemaphoreType.DMA((7 * self.n,)),
                        pltpu.SemaphoreType.DMA((self.n,))]

    def _copies(self, src, dst, sems):
        send_sems, recv_sems, local_sems = sems
        me, peer = _me_and_peers()
        sib, _ = peer(1)
        local, sends, recvs, passes = [], {}, {}, {}
        for k in range(self.n):
            own = src[k].at[me] if self.per_peer else src[k]
            local.append(pltpu.make_async_copy(own, dst[k].at[me], local_sems.at[k]))
            for j in range(1, N_DEV):
                pid, pidx = peer(j)
                out = src[k].at[pidx] if self.per_peer else src[k]
                sem = dict(send_sem=send_sems.at[k * 7 + j - 1], recv_sem=recv_sems.at[k * 7 + j - 1])
                recvs[k, j] = pltpu.make_async_remote_copy(src_ref=out, dst_ref=dst[k].at[pidx], device_id=pid,
                                                           device_id_type=MESH, **sem)
                if self.per_peer or j in (1, 2, 4, 6):
                    sends[k, j] = pltpu.make_async_remote_copy(src_ref=out, dst_ref=dst[k].at[me], device_id=pid,
                                                               device_id_type=MESH, **sem)
                else:
                    _, origin = peer(j ^ 1)
                    passes[k, j ^ 1] = pltpu.make_async_remote_copy(
                        src_ref=dst[k].at[origin], dst_ref=dst[k].at[origin], device_id=sib, device_id_type=MESH, **sem)
        return local, sends, recvs, passes

    def start(self, src, dst, sems):
        local, sends, _, _ = self._copies(src, dst, sems)
        for cp in local + list(sends.values()):
            cp.start()

    def finish(self, src, dst, sems):
        local, sends, recvs, passes = self._copies(src, dst, sems)
        for key, cp in passes.items():
            recvs[key].wait_recv()
            cp.start()
        for key, cp in recvs.items():
            if key not in passes:
                cp.wait_recv()
        for cp in list(sends.values()) + list(passes.values()):
            cp.wait_send()
        for cp in local:
            cp.wait()


def _grid_edges(grid):
    first, last = None, None
    for a, n in enumerate(grid):
        f, l = pl.program_id(a) == 0, pl.program_id(a) == n - 1
        first = f if first is None else first & f
        last = l if last is None else last & l
    return first, last


def _exchange(name, arrays, per_peer):
    ex = _Exchange(arrays, per_peer)
    n = ex.n

    def body(*refs):
        ex.start(refs[:n], refs[n:2 * n], refs[2 * n:])
        ex.finish(refs[:n], refs[n:2 * n], refs[2 * n:])

    return pl.pallas_call(body, name=name, out_shape=ex.out_shapes, in_specs=ex.specs, out_specs=ex.specs,
                          scratch_shapes=ex.scratch)(*arrays)


def _scatter_wait(name, started, after):
    sizes = [len(st['src']) for st in started]

    def body(*refs):
        me, peer = _me_and_peers()
        at = 0
        for g, n in enumerate(sizes):
            src, land = refs[at:at + n], refs[at + n:at + 2 * n]
            send_sems, recv_sems = refs[at + 2 * n], refs[at + 2 * n + 1]
            at += 2 * n + 2
            for k in range(n):
                for j in range(1, N_DEV):
                    pid, pidx = peer(j)
                    cp = pltpu.make_async_remote_copy(
                        src_ref=src[k].at[pidx], dst_ref=land[k].at[pidx], send_sem=send_sems.at[k * 7 + j - 1],
                        recv_sem=recv_sems.at[k * 7 + j - 1], device_id=pid, device_id_type=MESH)
                    cp.wait_send()
                    cp.wait_recv()

    hbm = pl.BlockSpec(memory_space=pltpu.HBM)
    sem = pl.BlockSpec(memory_space=pltpu.SEMAPHORE)
    ops, specs, outs, alias = [], [], [], {}
    for st in started:
        for a in st['src'] + st['land']:
            alias[len(ops)] = len(outs)
            ops.append(a)
            specs.append(hbm)
            outs.append(pltpu.HBM(a.shape, a.dtype))
        ops += [st['send'], st['recv']]
        specs += [sem, sem]
    res = pl.pallas_call(
        body, name=name, out_shape=tuple(outs),
        in_specs=tuple(specs) + (pl.BlockSpec(memory_space=pl.ANY),) * len(after), out_specs=(hbm,) * len(outs),
        input_output_aliases=alias,
        compiler_params=pltpu.CompilerParams(has_side_effects=pltpu.SideEffectType.DATAFLOW_SIDE_EFFECTING),
    )(*ops, *after)
    landed, at = [], 0
    for n in sizes:
        landed += list(res[at + n:at + 2 * n])
        at += 2 * n
    return landed


def _allreduce_small(v):
    rows = v.shape[0]

    def body(v_ref, out_ref, buf, send_sems, recv_sems):
        me, peer = _me_and_peers()
        buf[pl.ds(me, 1)] = v_ref[...][None]
        sends = []
        for j in range(1, N_DEV):
            pid, _ = peer(j)
            cp = pltpu.make_async_remote_copy(src_ref=v_ref, dst_ref=buf.at[me], send_sem=send_sems.at[j - 1],
                                              recv_sem=recv_sems.at[j - 1], device_id=pid, device_id_type=MESH)
            cp.start()
            sends.append(cp)
        for j in range(1, N_DEV):
            pid, pidx = peer(j)
            pltpu.make_async_remote_copy(src_ref=v_ref, dst_ref=buf.at[pidx], send_sem=send_sems.at[j - 1],
                                         recv_sem=recv_sems.at[j - 1], device_id=pid,
                                         device_id_type=MESH).wait_recv()
        for cp in sends:
            cp.wait_send()
        acc = buf[0]
        for s in range(1, N_DEV):
            acc = acc + buf[s]
        out_ref[...] = acc

    vm = pl.BlockSpec(memory_space=pltpu.VMEM)
    return pl.pallas_call(
        body, name="allreduce_small", out_shape=jax.ShapeDtypeStruct(v.shape, F32),
        in_specs=[vm], out_specs=vm,
        scratch_shapes=[pltpu.VMEM((N_DEV, rows, 128), F32), pltpu.SemaphoreType.DMA((7,)),
                        pltpu.SemaphoreType.DMA((7,))],
    )(v)


def _mm(name, grid, sem, k_axis, ops, op_specs, pairs, acc_shapes, extras, extra_specs, epilogue, outs, out_specs,
        comm=None, split=()):
    n_op, n_ex, n_out = len(ops), len(extras), len(outs)
    nk = grid[k_axis] if k_axis is not None else 1
    n_acc = len(acc_shapes) if nk > 1 else 0
    n_cm = comm.n if comm is not None else 0
    split = list(split)
    n_sp = len(split)
    assert not (split and comm is not None)

    def body(*refs):
        op_refs = refs[:n_op]
        ex_refs = refs[n_op:n_op + n_ex]
        n_in = n_op + n_ex + n_cm + 2 * n_sp
        out_refs = refs[n_in:n_in + n_out]
        n_res = n_cm + (2 + 2 * n_sp if split else 0)
        acc_refs = refs[n_in + n_out + n_res:n_in + n_out + n_res + n_acc]
        if split:
            src, land = refs[n_op + n_ex:n_op + n_ex + n_sp], refs[n_op + n_ex + n_sp:n_in]
            send_sems, recv_sems, local_sems = refs[n_in + n_out], refs[n_in + n_out + 1], refs[-1]

            @pl.when(_grid_edges(grid)[0])
            def _():
                me, peer = _me_and_peers()
                local = [pltpu.make_async_copy(src[k].at[me], land[k].at[me], local_sems.at[k]) for k in range(n_sp)]
                for cp in local:
                    cp.start()
                for cp in local:
                    cp.wait()
                for k in range(n_sp):
                    for j in range(1, N_DEV):
                        pid, pidx = peer(j)
                        pltpu.make_async_remote_copy(
                            src_ref=src[k].at[pidx], dst_ref=land[k].at[me], send_sem=send_sems.at[k * 7 + j - 1],
                            recv_sem=recv_sems.at[k * 7 + j - 1], device_id=pid, device_id_type=MESH).start()

        if comm is not None:
            cm_refs = (refs[n_op + n_ex:n_in], refs[n_in + n_out:n_in + n_out + n_cm],
                       refs[n_in + n_out + n_cm + n_acc:])
            first, last = _grid_edges(grid)

            @pl.when(first)
            def _():
                comm.start(*cm_refs)

        def finish(vals):
            res = epilogue(*vals, *[e[...] for e in ex_refs])
            for o, r in zip(out_refs, res):
                o[...] = r.astype(o.dtype)

        if nk == 1:
            parts = [None] * len(acc_shapes)
            for li, ri, dims, ai in pairs:
                d = _dot(op_refs[li][...], op_refs[ri][...], dims)
                parts[ai] = d if parts[ai] is None else parts[ai] + d
            finish(parts)
        else:
            k = pl.program_id(k_axis)

            @pl.when(k == 0)
            def _():
                for a in acc_refs:
                    a[...] = jnp.zeros_like(a)

            for li, ri, dims, ai in pairs:
                acc_refs[ai][...] += _dot(op_refs[li][...], op_refs[ri][...], dims)

            @pl.when(k == nk - 1)
            def _():
                finish([a[...] for a in acc_refs])

        if comm is not None:
            @pl.when(last)
            def _():
                comm.finish(*cm_refs)

    scratch = [pltpu.VMEM(s, F32) for s in acc_shapes] if nk > 1 else []
    if split:
        hbm = pl.BlockSpec(memory_space=pltpu.HBM)
        sem_spec = pl.BlockSpec(memory_space=pltpu.SEMAPHORE)
        thru = [pltpu.HBM(b.shape, b.dtype) for b in split]
        res = pl.pallas_call(
            body, name=name, grid=grid,
            out_shape=list(outs) + [pltpu.SemaphoreType.DMA((7 * n_sp,)), pltpu.SemaphoreType.DMA((7 * n_sp,))]
            + thru + thru,
            in_specs=list(op_specs) + list(extra_specs) + [hbm] * (2 * n_sp),
            out_specs=list(out_specs) + [sem_spec, sem_spec] + [hbm] * (2 * n_sp),
            input_output_aliases={n_op + n_ex + k: n_out + 2 + k for k in range(2 * n_sp)},
            scratch_shapes=scratch + [pltpu.SemaphoreType.DMA((n_sp,))],
            compiler_params=pltpu.CompilerParams(
                dimension_semantics=("arbitrary",) * len(grid), vmem_limit_bytes=V7X_VMEM_LIMIT,
                has_side_effects=pltpu.SideEffectType.DATAFLOW_SIDE_EFFECTING),
        )(*ops, *extras, *[pltpu.with_memory_space_constraint(b, pltpu.HBM) for b in split],
          *[pltpu.with_memory_space_constraint(lax.empty(b.shape, b.dtype), pltpu.HBM) for b in split])
        state = dict(send=res[n_out], recv=res[n_out + 1], src=list(res[n_out + 2:n_out + 2 + n_sp]),
                     land=list(res[n_out + 2 + n_sp:]))
        return res[:n_out], state
    if comm is None:
        return pl.pallas_call(
            body, name=name, grid=grid, out_shape=outs,
            in_specs=list(op_specs) + list(extra_specs), out_specs=list(out_specs),
            scratch_shapes=scratch, compiler_params=_params(sem),
        )(*ops, *extras)
    res = pl.pallas_call(
        body, name=name, grid=grid, out_shape=list(outs) + comm.out_shapes,
        in_specs=list(op_specs) + list(extra_specs) + comm.specs, out_specs=list(out_specs) + comm.specs,
        scratch_shapes=scratch + comm.scratch, compiler_params=_params(("arbitrary",) * len(grid)),
    )(*ops, *extras, *comm.arrays)
    return res[:n_out], res[n_out:]


def _with_comm(res, comm, pick):
    if comm is None:
        return pick(res)
    return pick(res[0]), res[1]


def _mm_nn(name, a, w, out_dtype, tm_target=704, tn_target=1664, epilogue=None, extras=(), extra_specs=(), comm=None):
    L, K = a.shape
    N = w.shape[1]
    tm, tn = _tile(L, tm_target), _tile(N, tn_target, 128)
    ep = epilogue if epilogue is not None else (lambda acc: (acc,))
    res = _mm(name, (L // tm, N // tn), ("parallel", "parallel"), None,
              [a, w], [pl.BlockSpec((tm, K), lambda i, j: (i, 0)), pl.BlockSpec((K, tn), lambda i, j: (0, j))],
              [(0, 1, 'nn', 0)], [(tm, tn)], list(extras), list(extra_specs), ep,
              [jax.ShapeDtypeStruct((L, N), out_dtype)], [pl.BlockSpec((tm, tn), lambda i, j: (i, j))], comm=comm)
    return _with_comm(res, comm, lambda o: o[0])


def _mm_nt(name, pairs_aw, out_dtype, tm_target=704, tn_target=512, comm=None):
    L = pairs_aw[0][0].shape[0]
    N = pairs_aw[0][1].shape[0]
    tm, tn = _tile(L, tm_target), _tile(N, tn_target, 128)
    ops, specs, pairs = [], [], []
    for t, (a, w) in enumerate(pairs_aw):
        K = a.shape[1]
        ops += [a, w]
        specs += [pl.BlockSpec((tm, K), lambda i, j: (i, 0)), pl.BlockSpec((tn, K), lambda i, j: (j, 0))]
        pairs.append((2 * t, 2 * t + 1, 'nt', 0))
    res = _mm(name, (L // tm, N // tn), ("parallel", "parallel"), None, ops, specs, pairs, [(tm, tn)], [], [],
              lambda acc: (acc,), [jax.ShapeDtypeStruct((L, N), out_dtype)],
              [pl.BlockSpec((tm, tn), lambda i, j: (i, j))], comm=comm)
    return _with_comm(res, comm, lambda o: o[0])


def _mm_tn(name, a, bs, out_dtype=BF16, tk_target=1408, tn_target=1664, tm_target=2048, comm=None):
    L, M = a.shape
    N = bs[0].shape[1]
    tk, tn, tm = _tile(L, tk_target), _tile(N, tn_target, 128), _tile(M, tm_target, 128)
    nb = len(bs)
    ops = [a] + list(bs)
    specs = [pl.BlockSpec((tk, tm), lambda i, j, k: (k, i))] + [pl.BlockSpec((tk, tn), lambda i, j, k: (k, j))] * nb
    res = _mm(name, (M // tm, N // tn, L // tk), ("parallel", "parallel", "arbitrary"), 2, ops, specs,
              [(0, 1 + t, 'tn', t) for t in range(nb)], [(tm, tn)] * nb, [], [], lambda *acc: acc,
              [jax.ShapeDtypeStruct((M, N), out_dtype)] * nb,
              [pl.BlockSpec((tm, tn), lambda i, j, k: (i, j))] * nb, comm=comm)
    return _with_comm(res, comm, lambda o: o)


def _norm_fwd(x, w):
    L, D = x.shape
    tr = _tile(L, 512)

    def body(x_ref, w_ref, y_ref):
        v = x_ref[...]
        r = lax.rsqrt(jnp.mean(v * v, axis=-1, keepdims=True) + EPS)
        y_ref[...] = (v * r * w_ref[...]).astype(y_ref.dtype)

    return pl.pallas_call(
        body, name="norm_fwd", grid=(L // tr,), out_shape=jax.ShapeDtypeStruct((L, D), BF16),
        in_specs=[pl.BlockSpec((tr, D), lambda i: (i, 0)), pl.BlockSpec((1, D), lambda i: (0, 0))],
        out_specs=pl.BlockSpec((tr, D), lambda i: (i, 0)), compiler_params=_params(("parallel",)),
    )(x, w)


def _norm_bwd_math(x, w, dy):
    r = lax.rsqrt(jnp.mean(x * x, axis=-1, keepdims=True) + EPS)
    gy = dy * w
    dx = r * (gy - x * (r * r) * jnp.mean(gy * x, axis=-1, keepdims=True))
    dw = jnp.sum(dy * x * r, axis=0, keepdims=True)
    return dx, dw


def _norm_bwd(x, w, dy, res, scale, out_dtype):
    L, D = x.shape
    tr = _tile(L, 384)
    has_res = res is not None

    def body(*refs):
        x_ref, w_ref, dy_ref = refs[:3]
        res_ref = refs[3] if has_res else None
        dx_ref, dw_ref = refs[-2:]
        dx, dw = _norm_bwd_math(x_ref[...], w_ref[...], dy_ref[...].astype(F32))
        dx = scale * dx
        if has_res:
            dx = dx + res_ref[...]
        dx_ref[...] = dx.astype(dx_ref.dtype)

        @pl.when(pl.program_id(0) == 0)
        def _():
            dw_ref[...] = jnp.zeros_like(dw_ref)

        dw_ref[...] += scale * dw

    row = pl.BlockSpec((tr, D), lambda i: (i, 0))
    vec = pl.BlockSpec((1, D), lambda i: (0, 0))
    return pl.pallas_call(
        body, name="norm_bwd", grid=(L // tr,),
        out_shape=[jax.ShapeDtypeStruct((L, D), out_dtype), jax.ShapeDtypeStruct((1, D), F32)],
        in_specs=[row, vec, row] + ([row] if has_res else []), out_specs=[row, vec],
        compiler_params=_params(("arbitrary",)),
    )(*([x, w, dy] + ([res] if has_res else [])))


def _loss(h, target):
    L, D = h.shape

    def body(h_ref, t_ref, dh_ref, loss_ref):
        i = pl.program_id(0)

        @pl.when(i == 0)
        def _():
            dh_ref[...] = jnp.zeros_like(dh_ref)
            loss_ref[...] = jnp.zeros_like(loss_ref)

        @pl.when(i > 0)
        def _():
            diff = h_ref[...] - t_ref[...]
            dh_ref[...] = diff * (1.0 / D)
            loss_ref[...] += 0.5 * jnp.sum(diff * diff) * (1.0 / D)

    return pl.pallas_call(
        body, name="loss", grid=(L // BLK,),
        out_shape=[jax.ShapeDtypeStruct((L, D), F32), jax.ShapeDtypeStruct((8, 128), F32)],
        in_specs=[pl.BlockSpec((BLK, D), lambda i: (i, 0)),
                  pl.BlockSpec((BLK, D), lambda i: (jnp.maximum(i - 1, 0), 0))],
        out_specs=[pl.BlockSpec((BLK, D), lambda i: (i, 0)), pl.BlockSpec((8, 128), lambda i: (0, 0))],
        compiler_params=_params(("arbitrary",)),
    )(h, target)


def _ffn_up(a, wg, wu, comm=None):
    L, D = a.shape
    F = wg.shape[1]
    tm = _tile(L, 704)

    def ep(g, u):
        return g, u, g * _sigmoid(g) * u

    hspec = pl.BlockSpec((None, tm, F), lambda i, j: (j, i, 0))
    wspec = pl.BlockSpec((None, F, D), lambda i, j: (j, 0, 0))
    res = _mm("ffn_up", (L // tm, N_DEV), ("parallel", "parallel"), None,
              [a, wg, wu], [pl.BlockSpec((tm, D), lambda i, j: (i, 0)), wspec, wspec],
              [(0, 1, 'nt', 0), (0, 2, 'nt', 1)], [(tm, F)] * 2, [], [], ep,
              [jax.ShapeDtypeStruct((N_DEV, L, F), BF16)] * 3, [hspec] * 3, comm=comm)
    return _with_comm(res, comm, lambda o: o)


def _ffn_gate(a, wg, comm=None):
    L, D = a.shape
    F = wg.shape[1]
    tm = _tile(L, 704)
    res = _mm("ffn_gate", (L // tm, N_DEV), ("parallel", "parallel"), None,
              [a, wg], [pl.BlockSpec((tm, D), lambda i, j: (i, 0)), pl.BlockSpec((None, F, D), lambda i, j: (j, 0, 0))],
              [(0, 1, 'nt', 0)], [(tm, F)], [], [], lambda g: (g,),
              [jax.ShapeDtypeStruct((N_DEV, L, F), BF16)], [pl.BlockSpec((None, tm, F), lambda i, j: (j, i, 0))],
              comm=comm)
    return _with_comm(res, comm, lambda o: o[0])


def _ffn_up_gated(a, wu, g, comm=None):
    L, D = a.shape
    F = wu.shape[1]
    tm = _tile(L, 704)

    def ep(u, g_):
        g32 = g_.astype(F32)
        return u, g32 * _sigmoid(g32) * u

    hspec = pl.BlockSpec((None, tm, F), lambda i, j: (j, i, 0))
    res = _mm("ffn_up_gated", (L // tm, N_DEV), ("parallel", "parallel"), None,
              [a, wu], [pl.BlockSpec((tm, D), lambda i, j: (i, 0)), pl.BlockSpec((None, F, D), lambda i, j: (j, 0, 0))],
              [(0, 1, 'nt', 0)], [(tm, F)], [g], [hspec], ep,
              [jax.ShapeDtypeStruct((N_DEV, L, F), BF16)] * 2, [hspec, hspec], comm=comm)
    return _with_comm(res, comm, lambda o: o)


def _resnorm_epilogue(scale, with_next):
    def ep(acc, h, w, *w_next):
        r = lax.rsqrt(jnp.mean(acc * acc, axis=-1, keepdims=True) + EPS)
        h_out = h + scale * (acc * r * w)
        if not with_next:
            return acc, h_out
        r_next = lax.rsqrt(jnp.mean(h_out * h_out, axis=-1, keepdims=True) + EPS)
        return acc, h_out, h_out * r_next * w_next[0]
    return ep


def _ffn_down(hid, wd, h_in, post, next_norm=None, comm=None):
    _, L, F = hid.shape
    D = wd.shape[2]
    tm = _tile(L, 528)
    row = pl.BlockSpec((tm, D), lambda i, j: (i, 0))
    vec = pl.BlockSpec((1, D), lambda i, j: (0, 0))
    nxt = [] if next_norm is None else [next_norm]
    res = _mm("ffn_down", (L // tm, N_DEV), ("parallel", "arbitrary"), 1,
              [hid, wd], [pl.BlockSpec((None, tm, F), lambda i, j: (j, i, 0)),
                          pl.BlockSpec((None, F, D), lambda i, j: (j, 0, 0))],
              [(0, 1, 'nn', 0)], [(tm, D)], [h_in, post] + nxt, [row, vec] + [vec] * len(nxt),
              _resnorm_epilogue(0.5, bool(nxt)),
              [jax.ShapeDtypeStruct((L, D), F32)] * 2 + [jax.ShapeDtypeStruct((L, D), BF16)] * len(nxt),
              [row] * (2 + len(nxt)), comm=comm)
    return _with_comm(res, comm, lambda o: o)


def _ffn_dhid(df, wd, g, u, comm=None):
    L, D = df.shape
    F = wd.shape[1]
    tm = _tile(L, 704)

    def ep(dhid, g_, u_):
        g32, u32 = g_.astype(F32), u_.astype(F32)
        sg = _sigmoid(g32)
        return dhid * u32 * sg * (1.0 + g32 * (1.0 - sg)), dhid * g32 * sg

    hspec = pl.BlockSpec((None, tm, F), lambda i, j: (j, i, 0))
    res = _mm("ffn_dhid", (L // tm, N_DEV), ("parallel", "parallel"), None,
              [df, wd], [pl.BlockSpec((tm, D), lambda i, j: (i, 0)),
                         pl.BlockSpec((None, F, D), lambda i, j: (j, 0, 0))],
              [(0, 1, 'nt', 0)], [(tm, F)], [g, u], [hspec, hspec], ep,
              [jax.ShapeDtypeStruct((N_DEV, L, F), BF16)] * 2, [hspec, hspec], comm=comm)
    return _with_comm(res, comm, lambda o: o)


def _ffn_dwd(hid, df, comm=None):
    _, L, F = hid.shape
    D = df.shape[1]
    tk = _tile(L, 1408)
    res = _mm("ffn_dwd", (N_DEV, L // tk), ("parallel", "arbitrary"), 1,
              [hid, df], [pl.BlockSpec((None, tk, F), lambda j, k: (j, k, 0)),
                          pl.BlockSpec((tk, D), lambda j, k: (k, 0))],
              [(0, 1, 'tn', 0)], [(F, D)], [], [], lambda acc: (acc,),
              [jax.ShapeDtypeStruct((N_DEV, F, D), BF16)], [pl.BlockSpec((None, F, D), lambda j, k: (j, 0, 0))],
              comm=comm)
    return _with_comm(res, comm, lambda o: o[0])


def _ffn_dwgu(a, dg, du, comm=None):
    L, D = a.shape
    F = dg.shape[2]
    tk = _tile(L, 1408)
    hspec = pl.BlockSpec((None, tk, F), lambda j, k: (j, k, 0))
    wspec = pl.BlockSpec((None, F, D), lambda j, k: (j, 0, 0))
    res = _mm("ffn_dwgu", (N_DEV, L // tk), ("parallel", "arbitrary"), 1,
              [a, dg, du], [pl.BlockSpec((tk, D), lambda j, k: (k, 0)), hspec, hspec],
              [(1, 0, 'tn', 0), (2, 0, 'tn', 1)], [(F, D)] * 2, [], [], lambda *acc: acc,
              [jax.ShapeDtypeStruct((N_DEV, F, D), BF16)] * 2, [wspec, wspec], comm=comm)
    return _with_comm(res, comm, lambda o: o)


def _ffn_da(dg, du, wg, wu, comm=None, split=()):
    _, L, F = dg.shape
    D = wg.shape[2]
    tm = _tile(L, 704)
    hspec = pl.BlockSpec((None, tm, F), lambda i, j: (j, i, 0))
    wspec = pl.BlockSpec((None, F, D), lambda i, j: (j, 0, 0))
    row = pl.BlockSpec((tm, D), lambda i, j: (i, 0))
    res = _mm("ffn_da_scatter_start" if split else "ffn_da", (L // tm, N_DEV), ("parallel", "arbitrary"), 1,
              [dg, du, wg, wu], [hspec, hspec, wspec, wspec],
              [(0, 2, 'nn', 0), (1, 3, 'nn', 0)], [(tm, D)], [], [], lambda acc: (acc,),
              [jax.ShapeDtypeStruct((L, D), F32)], [row], comm=comm, split=split)
    if split:
        return res[0][0], res[1]
    return _with_comm(res, comm, lambda o: o[0])


def _rope_tables(L):
    rows = jnp.arange(L, dtype=F32)
    pos = jnp.where(rows < BLK, rows, rows - (BLK - N_META))
    inv_r = ROPE_THETA ** (-jnp.arange(0, HD, 2, dtype=F32) / HD)
    ang_r = pos[:, None] * inv_r[None, :]
    cr = jnp.concatenate([jnp.cos(ang_r), jnp.cos(ang_r)], axis=1)
    sr = jnp.concatenate([-jnp.sin(ang_r), jnp.sin(ang_r)], axis=1)
    inv_m = ROPE_THETA ** (-jnp.arange(0, ROPE, 2, dtype=F32) / ROPE)
    ang_m = pos[:, None] * inv_m[None, :]
    z32 = jnp.zeros((L, ROPE // 2), F32)
    z64 = jnp.zeros((L, HD - ROPE), F32)
    cm = jnp.concatenate([jnp.cos(ang_m), jnp.cos(ang_m), z64], axis=1)
    sa = jnp.concatenate([-jnp.sin(ang_m), z32, z64], axis=1)
    sb = jnp.concatenate([z32, jnp.sin(ang_m), z64], axis=1)
    return cr, sr, cm, sa, sb


def _rope_ret(x, cr, sr):
    return x * cr + pltpu.roll(x, HD // 2, 1) * sr


def _rope_ret_t(d, cr, sr):
    return d * cr + pltpu.roll(d * sr, HD // 2, 1)


def _rope_mla(x, cm, sa, sb):
    return x * cm + pltpu.roll(x, HD - ROPE // 2, 1) * sa + pltpu.roll(x, ROPE // 2, 1) * sb


def _rope_mla_t(d, cm, sa, sb):
    return d * cm + pltpu.roll(d * sa, ROPE // 2, 1) + pltpu.roll(d * sb, HD - ROPE // 2, 1)


C_RQ, C_RK, C_RV, C_RG = 0, HEADS * HD, 2 * HEADS * HD, 3 * HEADS * HD
C_CQ = 4 * HEADS * HD
C_CKV = C_CQ + Q_RANK
C_KR = C_CKV + KV_RANK
RET_K_SCALE = HD ** -0.5


def _prep(proj, tabs, qn, kvn):
    L = proj.shape[0]
    tr = _tile(L, 256)
    W = HEADS * HD

    def body(p_ref, cr_ref, sr_ref, cm_ref, sa_ref, sb_ref, qn_ref, kvn_ref, q_ref, k_ref, v_ref, cq_ref, ckv_ref,
             kr_ref):
        cr, sr = cr_ref[...], sr_ref[...]
        for h in range(HEADS):
            sl = slice(h * HD, (h + 1) * HD)
            q_ref[:, sl] = _rope_ret(p_ref[:, C_RQ + h * HD:C_RQ + (h + 1) * HD].astype(F32), cr, sr).astype(BF16)
            k_ref[:, sl] = (_rope_ret(p_ref[:, C_RK + h * HD:C_RK + (h + 1) * HD].astype(F32), cr, sr)
                            * RET_K_SCALE).astype(BF16)
        v_ref[...] = p_ref[:, C_RV:C_RV + W].astype(BF16)
        cq = p_ref[:, C_CQ:C_CQ + Q_RANK].astype(F32)
        cq_ref[...] = (cq * lax.rsqrt(jnp.mean(cq * cq, axis=-1, keepdims=True) + EPS) * qn_ref[...]).astype(BF16)
        ckv = p_ref[:, C_CKV:C_CKV + KV_RANK].astype(F32)
        ckv_ref[...] = (ckv * lax.rsqrt(jnp.mean(ckv * ckv, axis=-1, keepdims=True) + EPS)
                        * kvn_ref[...]).astype(BF16)
        kr_ref[...] = _rope_mla(p_ref[:, C_KR:C_KR + HD].astype(F32), cm_ref[...], sa_ref[...], sb_ref[...]).astype(BF16)

    row = lambda w: pl.BlockSpec((tr, w), lambda i: (i, 0))
    vec = lambda w: pl.BlockSpec((1, w), lambda i: (0, 0))
    return pl.pallas_call(
        body, name="mix_prep", grid=(L // tr,),
        out_shape=[jax.ShapeDtypeStruct((L, W), BF16)] * 3 + [jax.ShapeDtypeStruct((L, Q_RANK), BF16),
                                                              jax.ShapeDtypeStruct((L, KV_RANK), BF16),
                                                              jax.ShapeDtypeStruct((L, HD), BF16)],
        in_specs=[row(D_INP)] + [row(HD)] * 5 + [vec(Q_RANK), vec(KV_RANK)],
        out_specs=[row(W)] * 3 + [row(Q_RANK), row(KV_RANK), row(HD)],
        compiler_params=_params(("parallel",)),
    )(proj, *tabs, qn, kvn)


def _prep_bwd(proj, dq, dk, dv, drg, dcqn, dckvn, dkr8, tabs, qn, kvn):
    L = proj.shape[0]
    tr = _tile(L, 192)
    W = HEADS * HD

    def body(p_ref, dq_ref, dk_ref, dv_ref, drg_ref, dcq_ref, dckv_ref, dkr_ref, cr_ref, sr_ref, cm_ref, sa_ref,
             sb_ref, qn_ref, kvn_ref, dp_ref, dqn_ref, dkvn_ref):
        cr, sr = cr_ref[...], sr_ref[...]
        dkr = None
        for h in range(HEADS):
            sl = slice(h * HD, (h + 1) * HD)
            dp_ref[:, C_RQ + h * HD:C_RQ + (h + 1) * HD] = _rope_ret_t(dq_ref[:, sl].astype(F32), cr, sr).astype(BF16)
            dp_ref[:, C_RK + h * HD:C_RK + (h + 1) * HD] = (_rope_ret_t(dk_ref[:, sl].astype(F32), cr, sr)
                                                            * RET_K_SCALE).astype(BF16)
            part = dkr_ref[:, sl].astype(F32)
            dkr = part if dkr is None else dkr + part
        dp_ref[:, C_RV:C_RV + W] = dv_ref[...].astype(BF16)
        dp_ref[:, C_RG:C_RG + W] = drg_ref[...].astype(BF16)
        dcq, dqn = _norm_bwd_math(p_ref[:, C_CQ:C_CQ + Q_RANK].astype(F32), qn_ref[...], dcq_ref[...])
        dp_ref[:, C_CQ:C_CQ + Q_RANK] = dcq.astype(BF16)
        dckv, dkvn = _norm_bwd_math(p_ref[:, C_CKV:C_CKV + KV_RANK].astype(F32), kvn_ref[...], dckv_ref[...])
        dp_ref[:, C_CKV:C_CKV + KV_RANK] = dckv.astype(BF16)
        dp_ref[:, C_KR:C_KR + HD] = _rope_mla_t(dkr, cm_ref[...], sa_ref[...], sb_ref[...]).astype(BF16)

        @pl.when(pl.program_id(0) == 0)
        def _():
            dqn_ref[...] = jnp.zeros_like(dqn_ref)
            dkvn_ref[...] = jnp.zeros_like(dkvn_ref)

        dqn_ref[...] += dqn
        dkvn_ref[...] += dkvn

    row = lambda w: pl.BlockSpec((tr, w), lambda i: (i, 0))
    vec = lambda w: pl.BlockSpec((1, w), lambda i: (0, 0))
    return pl.pallas_call(
        body, name="mix_prep_bwd", grid=(L // tr,),
        out_shape=[jax.ShapeDtypeStruct((L, D_INP), BF16), jax.ShapeDtypeStruct((1, Q_RANK), F32),
                   jax.ShapeDtypeStruct((1, KV_RANK), F32)],
        in_specs=[row(D_INP)] + [row(W)] * 4 + [row(Q_RANK), row(KV_RANK), row(W)] + [row(HD)] * 5
                 + [vec(Q_RANK), vec(KV_RANK)],
        out_specs=[row(D_INP), vec(Q_RANK), vec(KV_RANK)],
        compiler_params=_params(("arbitrary",)),
    )(proj, dq, dk, dv, drg, dcqn, dckvn, dkr8, *tabs, qn, kvn)


def _post(o_ret, proj, gn):
    L, W = o_ret.shape
    tr = _tile(L, 384)

    def body(o_ref, rg_ref, gn_ref, out_ref):
        for h in range(HEADS):
            sl = slice(h * HD, (h + 1) * HD)
            o = o_ref[:, sl]
            rg = rg_ref[:, sl].astype(F32)
            n = o * lax.rsqrt(jnp.mean(o * o, axis=-1, keepdims=True) + EPS)
            out_ref[:, sl] = (n * gn_ref[:, sl] * (rg * _sigmoid(rg))).astype(BF16)

    row = pl.BlockSpec((tr, W), lambda i: (i, 0))
    return pl.pallas_call(
        body, name="ret_post", grid=(L // tr,), out_shape=jax.ShapeDtypeStruct((L, W), BF16),
        in_specs=[row, pl.BlockSpec((tr, W), lambda i: (i, C_RG // W)), pl.BlockSpec((1, W), lambda i: (0, 0))],
        out_specs=row, compiler_params=_params(("parallel",)),
    )(o_ret, proj, gn)


def _post_bwd(o_ret, proj, gn, dcat):
    L, W = o_ret.shape
    tr = _tile(L, 384)

    def body(o_ref, rg_ref, gn_ref, d_ref, do_ref, drg_ref, dgn_ref):
        @pl.when(pl.program_id(0) == 0)
        def _():
            dgn_ref[...] = jnp.zeros_like(dgn_ref)

        for h in range(HEADS):
            sl = slice(h * HD, (h + 1) * HD)
            o = o_ref[:, sl]
            rg = rg_ref[:, sl].astype(F32)
            d = d_ref[:, sl].astype(F32)
            gw = gn_ref[:, sl]
            r = lax.rsqrt(jnp.mean(o * o, axis=-1, keepdims=True) + EPS)
            n = o * r
            sg = _sigmoid(rg)
            si = rg * sg
            dn = d * gw * si
            dgn_ref[:, sl] += jnp.sum(d * n * si, axis=0, keepdims=True)
            drg_ref[:, sl] = (d * n * gw * sg * (1.0 + rg * (1.0 - sg))).astype(drg_ref.dtype)
            do_ref[:, sl] = (r * (dn - o * (r * r) * jnp.mean(dn * o, axis=-1, keepdims=True))).astype(BF16)

    row = pl.BlockSpec((tr, W), lambda i: (i, 0))
    vec = pl.BlockSpec((1, W), lambda i: (0, 0))
    return pl.pallas_call(
        body, name="ret_post_bwd", grid=(L // tr,),
        out_shape=[jax.ShapeDtypeStruct((L, W), BF16), jax.ShapeDtypeStruct((L, W), BF16),
                   jax.ShapeDtypeStruct((1, W), F32)],
        in_specs=[row, pl.BlockSpec((tr, W), lambda i: (i, C_RG // W)), vec, row],
        out_specs=[row, row, vec], compiler_params=_params(("arbitrary",)),
    )(o_ret, proj, gn, dcat)


RET_HEADS_PER_STEP = 4


def _lin_attn(name, q, k, v, lg, reverse, out_dtype=F32):
    L, W = q.shape
    nc = L // BLK - 1
    G = RET_HEADS_PER_STEP

    def body(q_ref, k_ref, v_ref, lg_ref, o_ref, s_ref):
        n = lax.broadcasted_iota(jnp.int32, (BLK, BLK), 0).astype(F32)
        m = lax.broadcasted_iota(jnp.int32, (BLK, BLK), 1).astype(F32)
        dist = (m - n) if reverse else (n - m)
        consts = []
        for g in range(G):
            lgv = lg_ref[g, 0:1, :]
            dmask = jnp.where(dist >= 0, jnp.exp(lgv * jnp.maximum(dist, 0.0)), 0.0)
            c = dict(dmask=dmask, dmask0=jnp.where((n < N_META) & (m < N_META), dmask, 0.0),
                     gl=jnp.exp(lgv * float(BLK)))
            if reverse:
                c.update(inter=jnp.exp(lgv * (float(BLK) - n)), upd=jnp.exp(lgv * n),
                         inter0=jnp.where(n < N_META, jnp.exp(lgv * jnp.maximum(float(N_META) - n, 0.0)), 0.0))
            else:
                c.update(inter=jnp.exp(lgv * (n + 1.0)), upd=jnp.exp(lgv * (float(BLK) - 1.0 - n)),
                         upd0=jnp.where(n < N_META, jnp.exp(lgv * jnp.maximum(float(N_META) - 1.0 - n, 0.0)), 0.0))
            consts.append(c)

        def chunk(c):
            rows = pl.ds(pl.multiple_of(c * BLK, BLK), BLK)
            state = [s_ref[g] for g in range(G)]
            outs, new_state = [], []
            for g in range(G):
                cols = slice(g * HD, (g + 1) * HD)
                cg = consts[g]
                qc, kc, vc = q_ref[rows, cols], k_ref[rows, cols], v_ref[rows, cols]
                a = _dot(qc, kc, 'nt') * cg['dmask']
                outs.append(_dot(a.astype(BF16), vc, 'nn') + _dot(qc, state[g].astype(BF16), 'nn') * cg['inter'])
                new_state.append(state[g] * cg['gl'] + _dot((kc.astype(F32) * cg['upd']).astype(BF16), vc, 'tn'))
            for g in range(G):
                o_ref[rows, g * HD:(g + 1) * HD] = outs[g].astype(o_ref.dtype)
                s_ref[g] = new_state[g]

        def first_chunk(with_state):
            for g in range(G):
                cols = slice(g * HD, (g + 1) * HD)
                cg = consts[g]
                q0, k0, v0 = q_ref[0:BLK, cols], k_ref[0:BLK, cols], v_ref[0:BLK, cols]
                o0 = _dot((_dot(q0, k0, 'nt') * cg['dmask0']).astype(BF16), v0, 'nn')
                if with_state:
                    o0 = o0 + _dot(q0, s_ref[g].astype(BF16), 'nn') * cg['inter0']
                else:
                    s_ref[g] = _dot((k0.astype(F32) * cg['upd0']).astype(BF16), v0, 'tn')
                o_ref[0:BLK, cols] = o0.astype(o_ref.dtype)

        if reverse:
            s_ref[...] = jnp.zeros_like(s_ref)

            def step(t, carry):
                chunk(nc - t)
                return carry

            lax.fori_loop(0, nc, step, 0)
            first_chunk(True)
        else:
            first_chunk(False)

            def step(t, carry):
                chunk(t + 1)
                return carry

            lax.fori_loop(0, nc, step, 0)

    col = pl.BlockSpec((L, G * HD), lambda h: (0, h))
    return pl.pallas_call(
        body, name=name, grid=(HEADS // G,), out_shape=jax.ShapeDtypeStruct((L, W), out_dtype),
        in_specs=[col, col, col, pl.BlockSpec((G, 8, HD), lambda h: (h, 0, 0))], out_specs=col,
        scratch_shapes=[pltpu.VMEM((G, HD, HD), F32)], compiler_params=_params(("parallel",)),
    )(q, k, v, lg)


ATT_SCALE = (HD + ROPE) ** -0.5
LOG2E = 1.4426950408889634
Q_PRESCALE = ATT_SCALE * LOG2E
NEG = -1e30


ATT_TILE = 384
ATT_HEADS_PER_STEP = 2


def _att_valid(nq, nk, row0, col0):
    r = lax.broadcasted_iota(jnp.int32, (nq, nk), 0) + row0
    c = lax.broadcasted_iota(jnp.int32, (nq, nk), 1) + col0
    return (c <= r) & ((c < N_META) | (c >= BLK))


def _attn_fwd(qm, kn, krr, vm, comm=None):
    L = qm.shape[0]
    W = HEADS * HD
    T = _tile(L, ATT_TILE, BLK)
    nb = L // T
    G = ATT_HEADS_PER_STEP
    n_cm = comm.n if comm is not None else 0

    def body(*refs):
        q_ref, kn_ref, kr_ref, v_ref = refs[:4]
        o_ref, lse_ref = refs[4 + n_cm:6 + n_cm]
        m_sc, l_sc, acc_sc = refs[6 + 2 * n_cm:9 + 2 * n_cm]
        if comm is not None:
            cm_refs = (refs[4:4 + n_cm], refs[6 + n_cm:6 + 2 * n_cm], refs[9 + 2 * n_cm:])
            first, last = _grid_edges((HEADS // G, nb))

            @pl.when(first)
            def _():
                comm.start(*cm_refs)

        i = pl.program_id(1)
        m_sc[...] = jnp.full_like(m_sc, NEG)
        l_sc[...] = jnp.zeros_like(l_sc)
        acc_sc[...] = jnp.zeros_like(acc_sc)

        def tile(j, masked):
            rows = pl.ds(pl.multiple_of(j * T, T), T)
            kr = kr_ref[rows, :]
            valid = _att_valid(T, T, i * T, j * T) if masked else None
            m_prev = [m_sc[g] for g in range(G)]
            l_prev = [l_sc[g] for g in range(G)]
            acc_prev = [acc_sc[g] for g in range(G)]
            m_new, l_new, acc_new = [], [], []
            for g in range(G):
                k = jnp.concatenate([kn_ref[rows, g * HD:(g + 1) * HD], kr], axis=1)
                s = _dot(q_ref[:, g * QH:(g + 1) * QH], k, 'nt')
                if masked:
                    s = jnp.where(valid, s, NEG)
                m_new.append(jnp.maximum(m_prev[g], jnp.max(s, axis=-1, keepdims=True)))
                p = jnp.exp2(s - m_new[g])
                alpha = jnp.exp2(m_prev[g] - m_new[g])
                l_new.append(alpha * l_prev[g] + jnp.sum(p, axis=-1, keepdims=True))
                acc_new.append(alpha * acc_prev[g] + _dot(p.astype(BF16), v_ref[rows, g * HD:(g + 1) * HD], 'nn'))
            for g in range(G):
                m_sc[g] = m_new[g]
                l_sc[g] = l_new[g]
                acc_sc[g] = acc_new[g]

        tile(0, True)

        def mid(j, carry):
            tile(j, False)
            return carry

        lax.fori_loop(1, i, mid, 0)

        @pl.when(i > 0)
        def _():
            tile(i, True)

        for g in range(G):
            l = l_sc[g]
            o_ref[:, g * HD:(g + 1) * HD] = (acc_sc[g] / l).astype(o_ref.dtype)
            lse_ref[g] = jnp.broadcast_to(m_sc[g] + jnp.log(l) * LOG2E, (T, HD))

        if comm is not None:
            @pl.when(last)
            def _():
                comm.finish(*cm_refs)

    cm_specs = comm.specs if comm is not None else []
    res = pl.pallas_call(
        body, name="attn_fwd", grid=(HEADS // G, nb),
        out_shape=[jax.ShapeDtypeStruct((L, W), BF16), jax.ShapeDtypeStruct((HEADS, L, HD), F32)]
        + (comm.out_shapes if comm is not None else []),
        in_specs=[pl.BlockSpec((T, G * QH), lambda h, i: (i, h)), pl.BlockSpec((L, G * HD), lambda h, i: (0, h)),
                  pl.BlockSpec((L, HD), lambda h, i: (0, 0)), pl.BlockSpec((L, G * HD), lambda h, i: (0, h))]
        + cm_specs,
        out_specs=[pl.BlockSpec((T, G * HD), lambda h, i: (i, h)),
                   pl.BlockSpec((G, T, HD), lambda h, i: (h, i, 0))] + cm_specs,
        scratch_shapes=[pltpu.VMEM((G, T, 1), F32), pltpu.VMEM((G, T, 1), F32), pltpu.VMEM((G, T, HD), F32)]
        + (comm.scratch if comm is not None else []),
        compiler_params=_params(("arbitrary", "arbitrary")),
    )(qm, kn, krr, vm, *(comm.arrays if comm is not None else []))
    return res[:2], res[2:]


def _attn_bwd(qm, kn, krr, vm, o, dcat, lse, comm=None):
    L = qm.shape[0]
    W = HEADS * HD
    T = _tile(L, ATT_TILE, BLK)
    nb = L // T
    n_cm = comm.n if comm is not None else 0

    def body(*refs):
        q_ref, kn_ref, kr_ref, v_ref, o_ref, do_ref, lse_ref = refs[:7]
        dq_ref, dkn_ref, dkr_ref, dv_ref = refs[7 + n_cm:11 + n_cm]
        dl_sc, dk_sc, dv_sc = refs[11 + 2 * n_cm:14 + 2 * n_cm]
        if comm is not None:
            cm_refs = (refs[7:7 + n_cm], refs[11 + n_cm:11 + 2 * n_cm], refs[14 + 2 * n_cm:])
            first, last = _grid_edges((HEADS, nb))

            @pl.when(first)
            def _():
                comm.start(*cm_refs)

        j = pl.program_id(1)

        @pl.when(j == 0)
        def _():
            dq_ref[...] = jnp.zeros_like(dq_ref)

            def rowsum(t, carry):
                rows = pl.ds(pl.multiple_of(t * T, T), T)
                dl_sc[rows, :] = jnp.sum(do_ref[rows, :].astype(F32) * o_ref[rows, :].astype(F32), axis=-1,
                                         keepdims=True)
                return carry

            lax.fori_loop(0, nb, rowsum, 0)

        k = jnp.concatenate([kn_ref[...], kr_ref[...]], axis=1)
        v = v_ref[...]
        dk_sc[...] = jnp.zeros_like(dk_sc)
        dv_sc[...] = jnp.zeros_like(dv_sc)

        def tile(i, masked):
            rows = pl.ds(pl.multiple_of(i * T, T), T)
            q = q_ref[rows, :]
            do = do_ref[rows, :]
            s = _dot(q, k, 'nt')
            if masked:
                s = jnp.where(_att_valid(T, T, i * T, j * T), s, NEG)
            p = jnp.exp2(s - lse_ref[rows, 0:1])
            dv_sc[...] += _dot(p.astype(BF16), do, 'tn')
            ds = (p * (_dot(do, v, 'nt') - dl_sc[rows, :])).astype(BF16)
            dk_sc[...] += _dot(ds, q, 'tn')
            dq_ref[rows, :] += _dot(ds, k, 'nn')

        tile(j, True)

        def rest(masked):
            def step(i, carry):
                tile(i, masked)
                return carry
            lax.fori_loop(j + 1, nb, step, 0)

        @pl.when(j == 0)
        def _():
            rest(True)

        @pl.when(j > 0)
        def _():
            rest(False)

        dk = dk_sc[...] * (1.0 / LOG2E)
        dkn_ref[...] = dk[:, 0:HD].astype(BF16)
        dkr_ref[...] = dk[:, HD:QH].astype(dkr_ref.dtype)
        dv_ref[...] = dv_sc[...].astype(BF16)

        if comm is not None:
            @pl.when(last)
            def _():
                comm.finish(*cm_refs)

    blk = pl.BlockSpec((T, HD), lambda h, j: (j, h))
    cm_specs = comm.specs if comm is not None else []
    res = pl.pallas_call(
        body, name="attn_bwd", grid=(HEADS, nb),
        out_shape=[jax.ShapeDtypeStruct((L, HEADS * QH), F32), jax.ShapeDtypeStruct((L, W), BF16),
                   jax.ShapeDtypeStruct((L, W), BF16), jax.ShapeDtypeStruct((L, W), BF16)]
        + (comm.out_shapes if comm is not None else []),
        in_specs=[pl.BlockSpec((L, QH), lambda h, j: (0, h)), blk, pl.BlockSpec((T, HD), lambda h, j: (j, 0)), blk,
                  pl.BlockSpec((L, HD), lambda h, j: (0, h)), pl.BlockSpec((L, HD), lambda h, j: (0, HEADS + h)),
                  pl.BlockSpec((None, L, HD), lambda h, j: (h, 0, 0))] + cm_specs,
        out_specs=[pl.BlockSpec((L, QH), lambda h, j: (0, h)), blk, blk, blk] + cm_specs,
        scratch_shapes=[pltpu.VMEM((L, 1), F32), pltpu.VMEM((T, QH), F32), pltpu.VMEM((T, HD), F32)]
        + (comm.scratch if comm is not None else []),
        compiler_params=_params(("arbitrary", "arbitrary")),
    )(qm, kn, krr, vm, o, dcat, lse, *(comm.arrays if comm is not None else []))
    return res[:4], res[4:]


def _unrope_q(dqm, tabs_m):
    L, W = dqm.shape
    tr = _tile(L, 384)

    def body(d_ref, cm_ref, sa_ref, sb_ref, out_ref):
        cm, sa, sb = cm_ref[...], sa_ref[...], sb_ref[...]
        for h in range(HEADS):
            out_ref[:, h * QH:h * QH + HD] = (d_ref[:, h * QH:h * QH + HD] * ATT_SCALE).astype(BF16)
            out_ref[:, h * QH + HD:(h + 1) * QH] = _rope_mla_t(d_ref[:, h * QH + HD:(h + 1) * QH] * ATT_SCALE, cm, sa,
                                                               sb).astype(BF16)

    row = pl.BlockSpec((tr, W), lambda i: (i, 0))
    tab = pl.BlockSpec((tr, HD), lambda i: (i, 0))
    return pl.pallas_call(
        body, name="unrope_q", grid=(L // tr,), out_shape=jax.ShapeDtypeStruct((L, W), BF16),
        in_specs=[row, tab, tab, tab], out_specs=row, compiler_params=_params(("parallel",)),
    )(dqm, *tabs_m)


def _q_up(cqn, wuq_p, tabs_m):
    L = cqn.shape[0]
    tm = _tile(L, 704)

    def ep(acc, cm, sa, sb):
        acc = acc * Q_PRESCALE
        parts = []
        for h in range(HEADS):
            parts.append(acc[:, h * QH:h * QH + HD])
            parts.append(_rope_mla(acc[:, h * QH + HD:(h + 1) * QH], cm, sa, sb))
        return (jnp.concatenate(parts, axis=1),)

    tab = pl.BlockSpec((tm, HD), lambda i, j: (i, 0))
    return _mm("mla_q_up", (L // tm, 1), ("parallel", "parallel"), None,
               [cqn, wuq_p], [pl.BlockSpec((tm, Q_RANK), lambda i, j: (i, 0)),
                              pl.BlockSpec((HEADS * QH, Q_RANK), lambda i, j: (0, 0))],
               [(0, 1, 'nt', 0)], [(tm, HEADS * QH)], list(tabs_m), [tab] * 3, ep,
               [jax.ShapeDtypeStruct((L, HEADS * QH), BF16)], [pl.BlockSpec((tm, HEADS * QH), lambda i, j: (i, 0))])[0]


def _mix_out(cat, w_out, h_in, post, next_norm):
    L, K = cat.shape
    D = w_out.shape[1]
    tm, tk = _tile(L, 384), K
    row = pl.BlockSpec((tm, D), lambda i, k: (i, 0))
    vec = pl.BlockSpec((1, D), lambda i, k: (0, 0))
    return _mm("mix_out", (L // tm, K // tk), ("parallel", "arbitrary"), 1,
               [cat, w_out], [pl.BlockSpec((tm, tk), lambda i, k: (i, k)), pl.BlockSpec((tk, D), lambda i, k: (k, 0))],
               [(0, 1, 'nn', 0)], [(tm, D)], [h_in, post, next_norm], [row, vec, vec], _resnorm_epilogue(1.0, True),
               [jax.ShapeDtypeStruct((L, D), F32)] * 2 + [jax.ShapeDtypeStruct((L, D), BF16)], [row, row, row])


ADAM_BLOCK_ELEMS = 512 * 704


def _adam_math(w, g, m, v):
    m = ADAM_B1 * m + (1.0 - ADAM_B1) * g
    v = ADAM_B2 * v + (1.0 - ADAM_B2) * (g * g)
    m_hat = m / (1.0 - ADAM_B1 ** ADAM_STEP)
    v_hat = v / (1.0 - ADAM_B2 ** ADAM_STEP)
    delta = -ADAM_LR * (m_hat / (jnp.sqrt(v_hat) + ADAM_EPS) + ADAM_WD * w)
    return delta, m, v


def _adam(name, w, m, v, g_slots=None, g=None):
    R, C = w.shape
    tr, tc = _tile(R, max(16, ADAM_BLOCK_ELEMS // C // 16 * 16), 16), C
    if tr * tc > ADAM_BLOCK_ELEMS:
        tr, tc = R, _tile(C, max(128, ADAM_BLOCK_ELEMS // R // 128 * 128), 128)
    from_slots = g_slots is not None

    def body(w_ref, m_ref, v_ref, g_ref, go_ref, d_ref, mo_ref, vo_ref):
        if from_slots:
            grad = g_ref[0].astype(F32)
            for s in range(1, N_DEV):
                grad = grad + g_ref[s].astype(F32)
        else:
            grad = g_ref[...]
        delta, mn, vn = _adam_math(w_ref[...], grad, m_ref[...], v_ref[...])
        go_ref[...] = grad
        d_ref[...] = delta
        mo_ref[...] = mn
        vo_ref[...] = vn

    row = pl.BlockSpec((tr, tc), lambda i, j: (i, j))
    gspec = pl.BlockSpec((N_DEV, tr, tc), lambda i, j: (0, i, j)) if from_slots else row
    return pl.pallas_call(
        body, name=name, grid=(R // tr, C // tc), out_shape=[jax.ShapeDtypeStruct((R, C), F32)] * 4,
        in_specs=[row, row, row, gspec], out_specs=[row] * 4, compiler_params=_params(("parallel", "parallel")),
    )(w, m, v, g_slots if from_slots else g)


def _unblock(gathered):
    n, r, c = gathered.shape
    return jnp.transpose(gathered, (1, 0, 2)).reshape(r, n * c)


def _reblock(full, c):
    r = full.shape[0]
    return jnp.transpose(full[:, :N_DEV * c].reshape(r, N_DEV, c), (1, 0, 2))


def _step(x, target, w, mom, vel):
    S, D = x.shape[1], x.shape[2]
    L = S + BLK
    def sq(a, n):
        if a.ndim == 2:
            return a
        if n in TRANSPOSED:
            a = jnp.swapaxes(a, 1, 2)
        return a.reshape(a.shape[1:])

    def unsq(o, n):
        o = o.reshape((1,) + o.shape)
        return jnp.swapaxes(o, 1, 2) if n in TRANSPOSED else o

    p = {n: sq(w[n], n) for n in WEIGHTS if n != 'meta_tokens'}
    gather = lambda names: _Exchange([p[n].astype(BF16) for n in names], False)
    in_s, uq_s = p['w_in'].shape[0], p['mla_w_uq'].shape[0]
    assert uq_s == HD + ROPE and N_DEV == HEADS, "a w_uq shard is one head's columns"
    tabs = _rope_tables(L)
    tabs_m = tabs[2:]
    lg = jnp.broadcast_to(jnp.log(1.0 - 2.0 ** (-5.0 - jnp.arange(HEADS, dtype=F32)))[:, None, None], (HEADS, 8, HD))

    wg1, meta = _exchange("gather_first", [p['ffn1_w_gate'].astype(BF16), w['meta_tokens']], False)
    h0 = jnp.concatenate([_unblock(meta), jnp.zeros((BLK - N_META, D), F32), x[0]], axis=0)
    a1 = _norm_fwd(h0, p['ffn1_pre_norm'])
    g1, (wu1,) = _ffn_gate(a1, wg1, comm=gather(['ffn1_w_up']))
    (u1, hid1), (wd1,) = _ffn_up_gated(a1, wu1, g1, comm=gather(['ffn1_w_down']))
    (f1, h1, um), (w_in_g,) = _ffn_down(hid1, wd1, h0, p['ffn1_post_norm'], next_norm=p['mix_pre_norm'],
                                        comm=gather(['w_in']))

    w_in = jnp.pad(w_in_g.reshape(N_DEV * in_s, D), ((0, D_INP - N_DEV * in_s), (0, 0)))
    proj, (uq_g, uk_g, uv_g, wout_g) = _mm_nt("mix_in", [(um, w_in)], BF16, tn_target=1664,
                                              comm=gather(['mla_w_uq', 'mla_w_uk', 'mla_w_uv', 'w_out']))
    wuq = jnp.pad(uq_g, ((0, 0), (0, QH - uq_s), (0, 0))).reshape(HEADS * QH, Q_RANK)
    wuk, wuv, w_out = _unblock(uk_g), _unblock(uv_g), wout_g.reshape(-1, D)
    qr, kr, vr, cqn, ckvn, krr = _prep(proj, tabs, p['mla_q_norm'], p['mla_kv_norm'])
    qm = _q_up(cqn, wuq, tabs_m)
    kn = _mm_nn("mla_k_up", ckvn, wuk, BF16)
    vm = _mm_nn("mla_v_up", ckvn, wuv, BF16)
    (o_mla, lse), (wg2, wu2) = _attn_fwd(qm, kn, krr, vm, comm=gather(['ffn2_w_gate', 'ffn2_w_up']))
    o_ret = _lin_attn("ret_fwd", qr, kr, vr, lg, False)
    ret = _post(o_ret, proj, p['ret_group_norm'])
    cat = jnp.concatenate([ret, o_mla], axis=1)
    m, h2, a2 = _mix_out(cat, w_out, h1, p['mix_post_norm'], p['ffn2_pre_norm'])

    (g2, u2, hid2), (wd2,) = _ffn_up(a2, wg2, wu2, comm=gather(['ffn2_w_down']))
    f2, h3 = _ffn_down(hid2, wd2, h2, p['ffn2_post_norm'])
    dh3, loss_blk = _loss(h3, target[0])

    scatter = lambda blocks: _Exchange(blocks, True)
    R, dsmall = {}, {}
    df2, dsmall['ffn2_post_norm'] = _norm_bwd(f2, p['ffn2_post_norm'], dh3, None, 0.5, BF16)
    dg2, du2 = _ffn_dhid(df2, wd2, g2, u2)
    dwd2 = _ffn_dwd(hid2, df2)
    (dwg2, dwu2), (R['ffn2_w_down'],) = _ffn_dwgu(a2, dg2, du2, comm=scatter([dwd2]))
    da2, (R['ffn2_w_gate'],) = _ffn_da(dg2, du2, wg2, wu2, comm=scatter([dwg2]))
    dh2, dsmall['ffn2_pre_norm'] = _norm_bwd(h2, p['ffn2_pre_norm'], da2, dh3, 1.0, F32)

    dm, dsmall['mix_post_norm'] = _norm_bwd(m, p['mix_post_norm'], dh2, None, 1.0, BF16)
    dcat = _mm_nt("mix_dcat", [(dm, w_out)], BF16)
    dwout = _mm_tn("mix_dwout", cat, [dm])[0]
    do_ret, drg, dsmall['ret_group_norm'] = _post_bwd(o_ret, proj, p['ret_group_norm'], dcat)
    dqr = _lin_attn("ret_dq", do_ret, vr, kr, lg, False, BF16)
    dkr = _lin_attn("ret_dk", vr, do_ret, qr, lg, True, BF16)
    dvr = _lin_attn("ret_dv", kr, qr, do_ret, lg, True, BF16)
    (dqm, dkn, dkr8, dvm), (R['ffn2_w_up'], R['w_out']) = _attn_bwd(
        qm, kn, krr, vm, o_mla, dcat, lse, comm=scatter([dwu2, dwout.reshape(N_DEV, -1, D)]))
    dqp = _unrope_q(dqm, tabs_m)
    dwuq = _mm_tn("mla_dwuq", dqp, [cqn])[0]
    dcqn = _mm_nn("mla_dcq", dqp, wuq, F32)
    dwuk, dwuv = _mm_tn("mla_dwukv", ckvn, [dkn, dvm])
    dckvn = _mm_nt("mla_dckv", [(dkn, wuk), (dvm, wuv)], F32)
    dproj, dsmall['mla_q_norm'], dsmall['mla_kv_norm'] = _prep_bwd(
        proj, dqr, dkr, dvr, drg, dcqn, dckvn, dkr8, tabs, p['mla_q_norm'], p['mla_kv_norm'])
    dwuq_b = dwuq.reshape(HEADS, QH, Q_RANK)[:, :uq_s]
    (dwin,), (R['mla_w_uq'], R['mla_w_uk'], R['mla_w_uv']) = _mm_tn(
        "mix_dwin", dproj, [um], comm=scatter([dwuq_b, _reblock(dwuk, p['mla_w_uk'].shape[1]),
                                               _reblock(dwuv, p['mla_w_uv'].shape[1])]))
    dwin_b = dwin[:N_DEV * in_s].reshape(N_DEV, in_s, D)
    half = D // 2
    dum, (r_win_a,) = _mm_nn("mix_du", dproj, w_in, F32, tn_target=512, comm=scatter([dwin_b[:, :, :half]]))
    dh1, dsmall['mix_pre_norm'] = _norm_bwd(h1, p['mix_pre_norm'], dum, dh2, 1.0, F32)

    df1, dsmall['ffn1_post_norm'] = _norm_bwd(f1, p['ffn1_post_norm'], dh1, None, 0.5, BF16)
    (dg1, du1), (r_win_b,) = _ffn_dhid(df1, wd1, g1, u1, comm=scatter([dwin_b[:, :, half:]]))
    R['w_in'] = jnp.concatenate([r_win_a, r_win_b], axis=2)
    dwd1 = _ffn_dwd(hid1, df1)
    (dwg1, dwu1), (R['ffn1_w_down'],) = _ffn_dwgu(a1, dg1, du1, comm=scatter([dwd1]))
    da1, tail = _ffn_da(dg1, du1, wg1, wu1, split=[dwg1, dwu1])
    dh0, dsmall['ffn1_pre_norm'] = _norm_bwd(h0, p['ffn1_pre_norm'], da1, dh1, 1.0, F32)

    def slab(a):
        a = a.reshape(-1, 128)
        return jnp.pad(a, ((0, (-a.shape[0]) % 8), (0, 0)))

    slab_rows = lambda n: -(-(p[n].shape[-1] // 128) // 8) * 8
    packed = jnp.concatenate([slab(dsmall[n]) for n in SMALL] + [slab(dh0[:N_META]), loss_blk], axis=0)
    red = _allreduce_small(packed)
    offs = sum(slab_rows(n) for n in SMALL)
    n_small = offs
    gmeta_full = red[offs:offs + N_META * D // 128].reshape(N_META, D)
    offs += N_META * D // 128
    loss = red[offs, 0]

    grad, delta, new_m, new_v = {}, {}, {}, {}
    meanwhile = []

    def update_big(names):
        for n in names:
            outs = _adam("adam_" + n, p[n], sq(mom[n], n), sq(vel[n], n), g_slots=R[n])
            meanwhile.append(outs[0])
            grad[n], delta[n], new_m[n], new_v[n] = [unsq(o, n) for o in outs]

    last = ['ffn1_w_gate', 'ffn1_w_up']
    update_big([n for n in BIG if n not in last])
    pack = lambda d: jnp.concatenate([slab(d[n]) for n in SMALL], axis=0)
    outs = _adam("adam_small", pack(w), pack(mom), pack(vel), g=red[:n_small])
    meanwhile.append(outs[0])
    offs = 0
    for n in SMALL:
        r = p[n].shape[-1] // 128
        grad[n], delta[n], new_m[n], new_v[n] = [o[offs:offs + r].reshape(w[n].shape) for o in outs]
        offs += slab_rows(n)
    dev = 4 * lax.axis_index("x") + 2 * lax.axis_index("y") + lax.axis_index("c")
    mcols = w['meta_tokens'].shape[1]
    gmeta = lax.dynamic_slice(gmeta_full, (0, dev * mcols), (N_META, mcols))
    outs = _adam("adam_meta", w['meta_tokens'], mom['meta_tokens'], vel['meta_tokens'], g=gmeta)
    grad['meta_tokens'], delta['meta_tokens'], new_m['meta_tokens'], new_v['meta_tokens'] = outs
    meanwhile.append(outs[0])
    R['ffn1_w_gate'], R['ffn1_w_up'] = _scatter_wait("scatter_tail_wait", [tail], list(meanwhile))
    update_big(last)

    return (loss, dh0[BLK:][None], *[grad[n] for n in WEIGHTS], *[delta[n] for n in WEIGHTS],
            *[new_m[n] for n in WEIGHTS], *[new_v[n] for n in WEIGHTS])


def kernel(x, meta_tokens, ffn1_pre_norm, ffn1_w_gate, ffn1_w_up, ffn1_w_down, ffn1_post_norm, mix_pre_norm, w_in, ret_group_norm, mla_q_norm, mla_w_uq, mla_kv_norm, mla_w_uk, mla_w_uv, w_out, mix_post_norm, ffn2_pre_norm, ffn2_w_gate, ffn2_w_up, ffn2_w_down, ffn2_post_norm, loss_target, m_meta_tokens, m_ffn1_pre_norm, m_ffn1_w_gate, m_ffn1_w_up, m_ffn1_w_down, m_ffn1_post_norm, m_mix_pre_norm, m_w_in, m_ret_group_norm, m_mla_q_norm, m_mla_w_uq, m_mla_kv_norm, m_mla_w_uk, m_mla_w_uv, m_w_out, m_mix_post_norm, m_ffn2_pre_norm, m_ffn2_w_gate, m_ffn2_w_up, m_ffn2_w_down, m_ffn2_post_norm, v_meta_tokens, v_ffn1_pre_norm, v_ffn1_w_gate, v_ffn1_w_up, v_ffn1_w_down, v_ffn1_post_norm, v_mix_pre_norm, v_w_in, v_ret_group_norm, v_mla_q_norm, v_mla_w_uq, v_mla_kv_norm, v_mla_w_uk, v_mla_w_uv, v_w_out, v_mix_post_norm, v_ffn2_pre_norm, v_ffn2_w_gate, v_ffn2_w_up, v_ffn2_w_down, v_ffn2_post_norm):
    w = dict(zip(WEIGHTS, (meta_tokens, ffn1_pre_norm, ffn1_w_gate, ffn1_w_up, ffn1_w_down, ffn1_post_norm,
                           mix_pre_norm, w_in, ret_group_norm, mla_q_norm, mla_w_uq, mla_kv_norm, mla_w_uk, mla_w_uv,
                           w_out, mix_post_norm, ffn2_pre_norm, ffn2_w_gate, ffn2_w_up, ffn2_w_down, ffn2_post_norm)))
    mom = dict(zip(WEIGHTS, (m_meta_tokens, m_ffn1_pre_norm, m_ffn1_w_gate, m_ffn1_w_up, m_ffn1_w_down,
                             m_ffn1_post_norm, m_mix_pre_norm, m_w_in, m_ret_group_norm, m_mla_q_norm, m_mla_w_uq,
                             m_mla_kv_norm, m_mla_w_uk, m_mla_w_uv, m_w_out, m_mix_post_norm, m_ffn2_pre_norm,
                             m_ffn2_w_gate, m_ffn2_w_up, m_ffn2_w_down, m_ffn2_post_norm)))
    vel = dict(zip(WEIGHTS, (v_meta_tokens, v_ffn1_pre_norm, v_ffn1_w_gate, v_ffn1_w_up, v_ffn1_w_down,
                             v_ffn1_post_norm, v_mix_pre_norm, v_w_in, v_ret_group_norm, v_mla_q_norm, v_mla_w_uq,
                             v_mla_kv_norm, v_mla_w_uk, v_mla_w_uv, v_w_out, v_mix_post_norm, v_ffn2_pre_norm,
                             v_ffn2_w_gate, v_ffn2_w_up, v_ffn2_w_down, v_ffn2_post_norm)))
    return _step(x, loss_target, w, mom, vel)
```

```python
import functools
import math

import jax
import jax.numpy as jnp
from jax import lax
from jax.experimental import pallas as pl
from jax.experimental.pallas import tpu as pltpu

N_DEV = 8
N_META = 16
BLK = 128
HEADS = 8
HD = 128
ROPE = 64
Q_RANK = 512
KV_RANK = 256
QH = 2 * HD
D_INP = 4 * HEADS * HD + Q_RANK + KV_RANK + BLK
ROPE_THETA = 10000.0
EPS = 1e-6
ADAM_LR = 0.001
ADAM_B1 = 0.9
ADAM_B2 = 0.999
ADAM_EPS = 1e-08
ADAM_WD = 0.01
ADAM_STEP = 10
V7X_VMEM_LIMIT = 48 * 1024 * 1024
MESH = pl.DeviceIdType.MESH
F32 = jnp.float32
BF16 = jnp.bfloat16

WEIGHTS = ['meta_tokens', 'ffn1_pre_norm', 'ffn1_w_gate', 'ffn1_w_up', 'ffn1_w_down', 'ffn1_post_norm',
           'mix_pre_norm', 'w_in', 'ret_group_norm', 'mla_q_norm', 'mla_w_uq', 'mla_kv_norm', 'mla_w_uk',
           'mla_w_uv', 'w_out', 'mix_post_norm', 'ffn2_pre_norm', 'ffn2_w_gate', 'ffn2_w_up', 'ffn2_w_down',
           'ffn2_post_norm']
SMALL = ['ffn1_pre_norm', 'ffn1_post_norm', 'mix_pre_norm', 'ret_group_norm', 'mla_q_norm', 'mla_kv_norm',
         'mix_post_norm', 'ffn2_pre_norm', 'ffn2_post_norm']
TRANSPOSED = ('ffn1_w_gate', 'ffn1_w_up', 'ffn2_w_gate', 'ffn2_w_up', 'w_in', 'mla_w_uq')
BIG = ['ffn1_w_gate', 'ffn1_w_up', 'ffn1_w_down', 'w_in', 'mla_w_uq', 'mla_w_uk', 'mla_w_uv', 'w_out',
       'ffn2_w_gate', 'ffn2_w_up', 'ffn2_w_down']

_DIMS = {'nn': (((1,), (0,)), ((), ())), 'nt': (((1,), (1,)), ((), ())), 'tn': (((0,), (0,)), ((), ()))}


def _tile(n, target, mult=16):
    best = None
    for t in range(mult, min(n, target) + 1, mult):
        if n % t == 0:
            best = t
    return best if best is not None else n


def _params(sem):
    return pltpu.CompilerParams(dimension_semantics=sem, vmem_limit_bytes=V7X_VMEM_LIMIT)


def _dot(a, b, dims):
    return lax.dot_general(a, b, _DIMS[dims], preferred_element_type=F32)


def _sigmoid(x):
    return 0.5 * jnp.tanh(0.5 * x) + 0.5


def _me_and_peers():
    x, y, c = lax.axis_index("x"), lax.axis_index("y"), lax.axis_index("c")

    def peer(j):
        px = 1 - x if (j >> 2) & 1 else x
        py = 1 - y if (j >> 1) & 1 else y
        pc = 1 - c if j & 1 else c
        return (px, py, pc), 4 * px + 2 * py + pc

    return 4 * x + 2 * y + c, peer


class _Exchange:
    def __init__(self, arrays, per_peer):
        self.arrays = list(arrays)
        self.per_peer = per_peer
        self.n = len(self.arrays)
        self.out_shapes = [jax.ShapeDtypeStruct((N_DEV,) + tuple(a.shape[1:] if per_peer else a.shape), a.dtype)
                           for a in self.arrays]
        self.specs = [pl.BlockSpec(memory_space=pl.ANY)] * self.n
        self.scratch = [pltpu.SemaphoreType.DMA((7 * self.n,)), pltpu.SemaphoreType.DMA((7 * self.n,)),
                        pltpu.SemaphoreType.DMA((self.n,))]

    def _copies(self, src, dst, sems):
        send_sems, recv_sems, local_sems = sems
        me, peer = _me_and_peers()
        sib, _ = peer(1)
        local, sends, recvs, passes = [], {}, {}, {}
        for k in range(self.n):
            own = src[k].at[me] if self.per_peer else src[k]
            local.append(pltpu.make_async_copy(own, dst[k].at[me], local_sems.at[k]))
            for j in range(1, N_DEV):
                pid, pidx = peer(j)
                out = src[k].at[pidx] if self.per_peer else src[k]
                sem = dict(send_sem=send_sems.at[k * 7 + j - 1], recv_sem=recv_sems.at[k * 7 + j - 1])
                recvs[k, j] = pltpu.make_async_remote_copy(src_ref=out, dst_ref=dst[k].at[pidx], device_id=pid,
                                                           device_id_type=MESH, **sem)
                if self.per_peer or j in (1, 2, 4, 6):
                    sends[k, j] = pltpu.make_async_remote_copy(src_ref=out, dst_ref=dst[k].at[me], device_id=pid,
                                                               device_id_type=MESH, **sem)
                else:
                    _, origin = peer(j ^ 1)
                    passes[k, j ^ 1] = pltpu.make_async_remote_copy(
                        src_ref=dst[k].at[origin], dst_ref=dst[k].at[origin], device_id=sib, device_id_type=MESH, **sem)
        return local, sends, recvs, passes

    def start(self, src, dst, sems):
        local, sends, _, _ = self._copies(src, dst, sems)
        for cp in local + list(sends.values()):
            cp.start()

    def finish(self, src, dst, sems):
        local, sends, recvs, passes = self._copies(src, dst, sems)
        for key, cp in passes.items():
            recvs[key].wait_recv()
            cp.start()
        for key, cp in recvs.items():
            if key not in passes:
                cp.wait_recv()
        for cp in list(sends.values()) + list(passes.values()):
            cp.wait_send()
        for cp in local:
            cp.wait()


def _grid_edges(grid):
    first, last = None, None
    for a, n in enumerate(grid):
        f, l = pl.program_id(a) == 0, pl.program_id(a) == n - 1
        first = f if first is None else first & f
        last = l if last is None else last & l
    return first, last


def _exchange(name, arrays, per_peer):
    ex = _Exchange(arrays, per_peer)
    n = ex.n

    def body(*refs):
        ex.start(refs[:n], refs[n:2 * n], refs[2 * n:])
        ex.finish(refs[:n], refs[n:2 * n], refs[2 * n:])

    return pl.pallas_call(body, name=name, out_shape=ex.out_shapes, in_specs=ex.specs, out_specs=ex.specs,
                          scratch_shapes=ex.scratch)(*arrays)


def _scatter_start(blocks):
    def body(src_ref, land_ref, send_sems, recv_sems, src_thru, land_thru, token, local_sem):
        me, peer = _me_and_peers()
        local = pltpu.make_async_copy(src_ref.at[me], land_ref.at[me], local_sem)
        local.start()
        for j in range(1, N_DEV):
            pid, pidx = peer(j)
            pltpu.make_async_remote_copy(src_ref=src_ref.at[pidx], dst_ref=land_ref.at[me],
                                         send_sem=send_sems.at[j - 1], recv_sem=recv_sems.at[j - 1],
                                         device_id=pid, device_id_type=MESH).start()
        local.wait()
        token[...] = jnp.zeros_like(token)

    hbm = pl.BlockSpec(memory_space=pltpu.HBM)
    sem = pl.BlockSpec(memory_space=pltpu.SEMAPHORE)
    return pl.pallas_call(
        body, name="scatter_tail_start",
        out_shape=(pltpu.SemaphoreType.DMA((7,)), pltpu.SemaphoreType.DMA((7,)), pltpu.HBM(blocks.shape, blocks.dtype),
                   pltpu.HBM(blocks.shape, blocks.dtype), jax.ShapeDtypeStruct((8, 128), F32)),
        in_specs=(hbm, hbm), out_specs=(sem, sem, hbm, hbm, pl.BlockSpec(memory_space=pltpu.VMEM)),
        input_output_aliases={0: 2, 1: 3}, scratch_shapes=[pltpu.SemaphoreType.DMA],
        compiler_params=pltpu.CompilerParams(has_side_effects=pltpu.SideEffectType.DATAFLOW_SIDE_EFFECTING),
    )(pltpu.with_memory_space_constraint(blocks, pltpu.HBM),
      pltpu.with_memory_space_constraint(lax.empty(blocks.shape, blocks.dtype), pltpu.HBM))


def _scatter_wait(send_sems, recv_sems, src_thru, land_thru, after):
    n_after = len(after)

    def body(src_ref, land_ref, send_sems, recv_sems, *rest):
        me, peer = _me_and_peers()
        for j in range(1, N_DEV):
            pid, pidx = peer(j)
            cp = pltpu.make_async_remote_copy(src_ref=src_ref.at[pidx], dst_ref=land_ref.at[pidx],
                                              send_sem=send_sems.at[j - 1], recv_sem=recv_sems.at[j - 1],
                                              device_id=pid, device_id_type=MESH)
            cp.wait_send()
            cp.wait_recv()

    hbm = pl.BlockSpec(memory_space=pltpu.HBM)
    sem = pl.BlockSpec(memory_space=pltpu.SEMAPHORE)
    return pl.pallas_call(
        body, name="scatter_tail_wait",
        out_shape=(pltpu.HBM(src_thru.shape, src_thru.dtype), pltpu.HBM(land_thru.shape, land_thru.dtype)),
        in_specs=(hbm, hbm, sem, sem) + (pl.BlockSpec(memory_space=pl.ANY),) * n_after, out_specs=(hbm, hbm),
        input_output_aliases={0: 0, 1: 1},
        compiler_params=pltpu.CompilerParams(has_side_effects=pltpu.SideEffectType.DATAFLOW_SIDE_EFFECTING),
    )(src_thru, land_thru, send_sems, recv_sems, *after)[1]


def _allreduce_small(v):
    rows = v.shape[0]

    def body(v_ref, out_ref, buf, send_sems, recv_sems):
        me, peer = _me_and_peers()
        buf[pl.ds(me, 1)] = v_ref[...][None]
        sends = []
        for j in range(1, N_DEV):
            pid, _ = peer(j)
            cp = pltpu.make_async_remote_copy(src_ref=v_ref, dst_ref=buf.at[me], send_sem=send_sems.at[j - 1],
                                              recv_sem=recv_sems.at[j - 1], device_id=pid, device_id_type=MESH)
            cp.start()
            sends.append(cp)
        for j in range(1, N_DEV):
            pid, pidx = peer(j)
            pltpu.make_async_remote_copy(src_ref=v_ref, dst_ref=buf.at[pidx], send_sem=send_sems.at[j - 1],
                                         recv_sem=recv_sems.at[j - 1], device_id=pid,
                                         device_id_type=MESH).wait_recv()
        for cp in sends:
            cp.wait_send()
        acc = buf[0]
        for s in range(1, N_DEV):
            acc = acc + buf[s]
        out_ref[...] = acc

    vm = pl.BlockSpec(memory_space=pltpu.VMEM)
    return pl.pallas_call(
        body, name="allreduce_small", out_shape=jax.ShapeDtypeStruct(v.shape, F32),
        in_specs=[vm], out_specs=vm,
        scratch_shapes=[pltpu.VMEM((N_DEV, rows, 128), F32), pltpu.SemaphoreType.DMA((7,)),
                        pltpu.SemaphoreType.DMA((7,))],
    )(v)


def _mm(name, grid, sem, k_axis, ops, op_specs, pairs, acc_shapes, extras, extra_specs, epilogue, outs, out_specs,
        comm=None):
    n_op, n_ex, n_out = len(ops), len(extras), len(outs)
    nk = grid[k_axis] if k_axis is not None else 1
    n_acc = len(acc_shapes) if nk > 1 else 0
    n_cm = comm.n if comm is not None else 0

    def body(*refs):
        op_refs = refs[:n_op]
        ex_refs = refs[n_op:n_op + n_ex]
        n_in = n_op + n_ex + n_cm
        out_refs = refs[n_in:n_in + n_out]
        acc_refs = refs[n_in + n_out + n_cm:n_in + n_out + n_cm + n_acc]
        if comm is not None:
            cm_refs = (refs[n_op + n_ex:n_in], refs[n_in + n_out:n_in + n_out + n_cm],
                       refs[n_in + n_out + n_cm + n_acc:])
            first, last = _grid_edges(grid)

            @pl.when(first)
            def _():
                comm.start(*cm_refs)

        def finish(vals):
            res = epilogue(*vals, *[e[...] for e in ex_refs])
            for o, r in zip(out_refs, res):
                o[...] = r.astype(o.dtype)

        if nk == 1:
            parts = [None] * len(acc_shapes)
            for li, ri, dims, ai in pairs:
                d = _dot(op_refs[li][...], op_refs[ri][...], dims)
                parts[ai] = d if parts[ai] is None else parts[ai] + d
            finish(parts)
        else:
            k = pl.program_id(k_axis)

            @pl.when(k == 0)
            def _():
                for a in acc_refs:
                    a[...] = jnp.zeros_like(a)

            for li, ri, dims, ai in pairs:
                acc_refs[ai][...] += _dot(op_refs[li][...], op_refs[ri][...], dims)

            @pl.when(k == nk - 1)
            def _():
                finish([a[...] for a in acc_refs])

        if comm is not None:
            @pl.when(last)
            def _():
                comm.finish(*cm_refs)

    scratch = [pltpu.VMEM(s, F32) for s in acc_shapes] if nk > 1 else []
    if comm is None:
        return pl.pallas_call(
            body, name=name, grid=grid, out_shape=outs,
            in_specs=list(op_specs) + list(extra_specs), out_specs=list(out_specs),
            scratch_shapes=scratch, compiler_params=_params(sem),
        )(*ops, *extras)
    res = pl.pallas_call(
        body, name=name, grid=grid, out_shape=list(outs) + comm.out_shapes,
        in_specs=list(op_specs) + list(extra_specs) + comm.specs, out_specs=list(out_specs) + comm.specs,
        scratch_shapes=scratch + comm.scratch, compiler_params=_params(("arbitrary",) * len(grid)),
    )(*ops, *extras, *comm.arrays)
    return res[:n_out], res[n_out:]


def _with_comm(res, comm, pick):
    if comm is None:
        return pick(res)
    return pick(res[0]), res[1]


def _mm_nn(name, a, w, out_dtype, tm_target=704, tn_target=1664, epilogue=None, extras=(), extra_specs=(), comm=None):
    L, K = a.shape
    N = w.shape[1]
    tm, tn = _tile(L, tm_target), _tile(N, tn_target, 128)
    ep = epilogue if epilogue is not None else (lambda acc: (acc,))
    res = _mm(name, (L // tm, N // tn), ("parallel", "parallel"), None,
              [a, w], [pl.BlockSpec((tm, K), lambda i, j: (i, 0)), pl.BlockSpec((K, tn), lambda i, j: (0, j))],
              [(0, 1, 'nn', 0)], [(tm, tn)], list(extras), list(extra_specs), ep,
              [jax.ShapeDtypeStruct((L, N), out_dtype)], [pl.BlockSpec((tm, tn), lambda i, j: (i, j))], comm=comm)
    return _with_comm(res, comm, lambda o: o[0])


def _mm_nt(name, pairs_aw, out_dtype, tm_target=704, tn_target=512, comm=None):
    L = pairs_aw[0][0].shape[0]
    N = pairs_aw[0][1].shape[0]
    tm, tn = _tile(L, tm_target), _tile(N, tn_target, 128)
    ops, specs, pairs = [], [], []
    for t, (a, w) in enumerate(pairs_aw):
        K = a.shape[1]
        ops += [a, w]
        specs += [pl.BlockSpec((tm, K), lambda i, j: (i, 0)), pl.BlockSpec((tn, K), lambda i, j: (j, 0))]
        pairs.append((2 * t, 2 * t + 1, 'nt', 0))
    res = _mm(name, (L // tm, N // tn), ("parallel", "parallel"), None, ops, specs, pairs, [(tm, tn)], [], [],
              lambda acc: (acc,), [jax.ShapeDtypeStruct((L, N), out_dtype)],
              [pl.BlockSpec((tm, tn), lambda i, j: (i, j))], comm=comm)
    return _with_comm(res, comm, lambda o: o[0])


def _mm_tn(name, a, bs, out_dtype=BF16, tk_target=1408, tn_target=1664, tm_target=2048, comm=None):
    L, M = a.shape
    N = bs[0].shape[1]
    tk, tn, tm = _tile(L, tk_target), _tile(N, tn_target, 128), _tile(M, tm_target, 128)
    nb = len(bs)
    ops = [a] + list(bs)
    specs = [pl.BlockSpec((tk, tm), lambda i, j, k: (k, i))] + [pl.BlockSpec((tk, tn), lambda i, j, k: (k, j))] * nb
    res = _mm(name, (M // tm, N // tn, L // tk), ("parallel", "parallel", "arbitrary"), 2, ops, specs,
              [(0, 1 + t, 'tn', t) for t in range(nb)], [(tm, tn)] * nb, [], [], lambda *acc: acc,
              [jax.ShapeDtypeStruct((M, N), out_dtype)] * nb,
              [pl.BlockSpec((tm, tn), lambda i, j, k: (i, j))] * nb, comm=comm)
    return _with_comm(res, comm, lambda o: o)


def _norm_fwd(x, w):
    L, D = x.shape
    tr = _tile(L, 512)

    def body(x_ref, w_ref, y_ref):
        v = x_ref[...]
        r = lax.rsqrt(jnp.mean(v * v, axis=-1, keepdims=True) + EPS)
        y_ref[...] = (v * r * w_ref[...]).astype(y_ref.dtype)

    return pl.pallas_call(
        body, name="norm_fwd", grid=(L // tr,), out_shape=jax.ShapeDtypeStruct((L, D), BF16),
        in_specs=[pl.BlockSpec((tr, D), lambda i: (i, 0)), pl.BlockSpec((1, D), lambda i: (0, 0))],
        out_specs=pl.BlockSpec((tr, D), lambda i: (i, 0)), compiler_params=_params(("parallel",)),
    )(x, w)


def _norm_bwd_math(x, w, dy):
    r = lax.rsqrt(jnp.mean(x * x, axis=-1, keepdims=True) + EPS)
    gy = dy * w
    dx = r * (gy - x * (r * r) * jnp.mean(gy * x, axis=-1, keepdims=True))
    dw = jnp.sum(dy * x * r, axis=0, keepdims=True)
    return dx, dw


def _norm_bwd(x, w, dy, res, scale, out_dtype):
    L, D = x.shape
    tr = _tile(L, 384)
    has_res = res is not None

    def body(*refs):
        x_ref, w_ref, dy_ref = refs[:3]
        res_ref = refs[3] if has_res else None
        dx_ref, dw_ref = refs[-2:]
        dx, dw = _norm_bwd_math(x_ref[...], w_ref[...], dy_ref[...].astype(F32))
        dx = scale * dx
        if has_res:
            dx = dx + res_ref[...]
        dx_ref[...] = dx.astype(dx_ref.dtype)

        @pl.when(pl.program_id(0) == 0)
        def _():
            dw_ref[...] = jnp.zeros_like(dw_ref)

        dw_ref[...] += scale * dw

    row = pl.BlockSpec((tr, D), lambda i: (i, 0))
    vec = pl.BlockSpec((1, D), lambda i: (0, 0))
    return pl.pallas_call(
        body, name="norm_bwd", grid=(L // tr,),
        out_shape=[jax.ShapeDtypeStruct((L, D), out_dtype), jax.ShapeDtypeStruct((1, D), F32)],
        in_specs=[row, vec, row] + ([row] if has_res else []), out_specs=[row, vec],
        compiler_params=_params(("arbitrary",)),
    )(*([x, w, dy] + ([res] if has_res else [])))


def _loss(h, target):
    L, D = h.shape

    def body(h_ref, t_ref, dh_ref, loss_ref):
        i = pl.program_id(0)

        @pl.when(i == 0)
        def _():
            dh_ref[...] = jnp.zeros_like(dh_ref)
            loss_ref[...] = jnp.zeros_like(loss_ref)

        @pl.when(i > 0)
        def _():
            diff = h_ref[...] - t_ref[...]
            dh_ref[...] = diff * (1.0 / D)
            loss_ref[...] += 0.5 * jnp.sum(diff * diff) * (1.0 / D)

    return pl.pallas_call(
        body, name="loss", grid=(L // BLK,),
        out_shape=[jax.ShapeDtypeStruct((L, D), F32), jax.ShapeDtypeStruct((8, 128), F32)],
        in_specs=[pl.BlockSpec((BLK, D), lambda i: (i, 0)),
                  pl.BlockSpec((BLK, D), lambda i: (jnp.maximum(i - 1, 0), 0))],
        out_specs=[pl.BlockSpec((BLK, D), lambda i: (i, 0)), pl.BlockSpec((8, 128), lambda i: (0, 0))],
        compiler_params=_params(("arbitrary",)),
    )(h, target)


def _ffn_up(a, wg, wu, comm=None):
    L, D = a.shape
    F = wg.shape[1]
    tm = _tile(L, 704)

    def ep(g, u):
        return g, u, g * _sigmoid(g) * u

    hspec = pl.BlockSpec((None, tm, F), lambda i, j: (j, i, 0))
    wspec = pl.BlockSpec((None, F, D), lambda i, j: (j, 0, 0))
    res = _mm("ffn_up", (L // tm, N_DEV), ("parallel", "parallel"), None,
              [a, wg, wu], [pl.BlockSpec((tm, D), lambda i, j: (i, 0)), wspec, wspec],
              [(0, 1, 'nt', 0), (0, 2, 'nt', 1)], [(tm, F)] * 2, [], [], ep,
              [jax.ShapeDtypeStruct((N_DEV, L, F), BF16)] * 3, [hspec] * 3, comm=comm)
    return _with_comm(res, comm, lambda o: o)


def _ffn_gate(a, wg, comm=None):
    L, D = a.shape
    F = wg.shape[1]
    tm = _tile(L, 704)
    res = _mm("ffn_gate", (L // tm, N_DEV), ("parallel", "parallel"), None,
              [a, wg], [pl.BlockSpec((tm, D), lambda i, j: (i, 0)), pl.BlockSpec((None, F, D), lambda i, j: (j, 0, 0))],
              [(0, 1, 'nt', 0)], [(tm, F)], [], [], lambda g: (g,),
              [jax.ShapeDtypeStruct((N_DEV, L, F), BF16)], [pl.BlockSpec((None, tm, F), lambda i, j: (j, i, 0))],
              comm=comm)
    return _with_comm(res, comm, lambda o: o[0])


def _ffn_up_gated(a, wu, g, comm=None):
    L, D = a.shape
    F = wu.shape[1]
    tm = _tile(L, 704)

    def ep(u, g_):
        g32 = g_.astype(F32)
        return u, g32 * _sigmoid(g32) * u

    hspec = pl.BlockSpec((None, tm, F), lambda i, j: (j, i, 0))
    res = _mm("ffn_up_gated", (L // tm, N_DEV), ("parallel", "parallel"), None,
              [a, wu], [pl.BlockSpec((tm, D), lambda i, j: (i, 0)), pl.BlockSpec((None, F, D), lambda i, j: (j, 0, 0))],
              [(0, 1, 'nt', 0)], [(tm, F)], [g], [hspec], ep,
              [jax.ShapeDtypeStruct((N_DEV, L, F), BF16)] * 2, [hspec, hspec], comm=comm)
    return _with_comm(res, comm, lambda o: o)


def _resnorm_epilogue(scale, with_next):
    def ep(acc, h, w, *w_next):
        r = lax.rsqrt(jnp.mean(acc * acc, axis=-1, keepdims=True) + EPS)
        h_out = h + scale * (acc * r * w)
        if not with_next:
            return acc, h_out
        r_next = lax.rsqrt(jnp.mean(h_out * h_out, axis=-1, keepdims=True) + EPS)
        return acc, h_out, h_out * r_next * w_next[0]
    return ep


def _ffn_down(hid, wd, h_in, post, next_norm=None, comm=None):
    _, L, F = hid.shape
    D = wd.shape[2]
    tm = _tile(L, 528)
    row = pl.BlockSpec((tm, D), lambda i, j: (i, 0))
    vec = pl.BlockSpec((1, D), lambda i, j: (0, 0))
    nxt = [] if next_norm is None else [next_norm]
    res = _mm("ffn_down", (L // tm, N_DEV), ("parallel", "arbitrary"), 1,
              [hid, wd], [pl.BlockSpec((None, tm, F), lambda i, j: (j, i, 0)),
                          pl.BlockSpec((None, F, D), lambda i, j: (j, 0, 0))],
              [(0, 1, 'nn', 0)], [(tm, D)], [h_in, post] + nxt, [row, vec] + [vec] * len(nxt),
              _resnorm_epilogue(0.5, bool(nxt)),
              [jax.ShapeDtypeStruct((L, D), F32)] * 2 + [jax.ShapeDtypeStruct((L, D), BF16)] * len(nxt),
              [row] * (2 + len(nxt)), comm=comm)
    return _with_comm(res, comm, lambda o: o)


def _ffn_dhid(df, wd, g, u, comm=None):
    L, D = df.shape
    F = wd.shape[1]
    tm = _tile(L, 704)

    def ep(dhid, g_, u_):
        g32, u32 = g_.astype(F32), u_.astype(F32)
        sg = _sigmoid(g32)
        return dhid * u32 * sg * (1.0 + g32 * (1.0 - sg)), dhid * g32 * sg

    hspec = pl.BlockSpec((None, tm, F), lambda i, j: (j, i, 0))
    res = _mm("ffn_dhid", (L // tm, N_DEV), ("parallel", "parallel"), None,
              [df, wd], [pl.BlockSpec((tm, D), lambda i, j: (i, 0)),
                         pl.BlockSpec((None, F, D), lambda i, j: (j, 0, 0))],
              [(0, 1, 'nt', 0)], [(tm, F)], [g, u], [hspec, hspec], ep,
              [jax.ShapeDtypeStruct((N_DEV, L, F), BF16)] * 2, [hspec, hspec], comm=comm)
    return _with_comm(res, comm, lambda o: o)


def _ffn_dwd(hid, df, comm=None):
    _, L, F = hid.shape
    D = df.shape[1]
    tk = _tile(L, 1408)
    res = _mm("ffn_dwd", (N_DEV, L // tk), ("parallel", "arbitrary"), 1,
              [hid, df], [pl.BlockSpec((None, tk, F), lambda j, k: (j, k, 0)),
                          pl.BlockSpec((tk, D), lambda j, k: (k, 0))],
              [(0, 1, 'tn', 0)], [(F, D)], [], [], lambda acc: (acc,),
              [jax.ShapeDtypeStruct((N_DEV, F, D), BF16)], [pl.BlockSpec((None, F, D), lambda j, k: (j, 0, 0))],
              comm=comm)
    return _with_comm(res, comm, lambda o: o[0])


def _ffn_dwgu(a, dg, du, comm=None):
    L, D = a.shape
    F = dg.shape[2]
    tk = _tile(L, 1408)
    hspec = pl.BlockSpec((None, tk, F), lambda j, k: (j, k, 0))
    wspec = pl.BlockSpec((None, F, D), lambda j, k: (j, 0, 0))
    res = _mm("ffn_dwgu", (N_DEV, L // tk), ("parallel", "arbitrary"), 1,
              [a, dg, du], [pl.BlockSpec((tk, D), lambda j, k: (k, 0)), hspec, hspec],
              [(1, 0, 'tn', 0), (2, 0, 'tn', 1)], [(F, D)] * 2, [], [], lambda *acc: acc,
              [jax.ShapeDtypeStruct((N_DEV, F, D), BF16)] * 2, [wspec, wspec], comm=comm)
    return _with_comm(res, comm, lambda o: o)


def _ffn_da(dg, du, wg, wu, comm=None):
    _, L, F = dg.shape
    D = wg.shape[2]
    tm = _tile(L, 704)
    hspec = pl.BlockSpec((None, tm, F), lambda i, j: (j, i, 0))
    wspec = pl.BlockSpec((None, F, D), lambda i, j: (j, 0, 0))
    row = pl.BlockSpec((tm, D), lambda i, j: (i, 0))
    res = _mm("ffn_da", (L // tm, N_DEV), ("parallel", "arbitrary"), 1,
              [dg, du, wg, wu], [hspec, hspec, wspec, wspec],
              [(0, 2, 'nn', 0), (1, 3, 'nn', 0)], [(tm, D)], [], [], lambda acc: (acc,),
              [jax.ShapeDtypeStruct((L, D), F32)], [row], comm=comm)
    return _with_comm(res, comm, lambda o: o[0])


def _rope_tables(L):
    rows = jnp.arange(L, dtype=F32)
    pos = jnp.where(rows < BLK, rows, rows - (BLK - N_META))
    inv_r = ROPE_THETA ** (-jnp.arange(0, HD, 2, dtype=F32) / HD)
    ang_r = pos[:, None] * inv_r[None, :]
    cr = jnp.concatenate([jnp.cos(ang_r), jnp.cos(ang_r)], axis=1)
    sr = jnp.concatenate([-jnp.sin(ang_r), jnp.sin(ang_r)], axis=1)
    inv_m = ROPE_THETA ** (-jnp.arange(0, ROPE, 2, dtype=F32) / ROPE)
    ang_m = pos[:, None] * inv_m[None, :]
    z32 = jnp.zeros((L, ROPE // 2), F32)
    z64 = jnp.zeros((L, HD - ROPE), F32)
    cm = jnp.concatenate([jnp.cos(ang_m), jnp.cos(ang_m), z64], axis=1)
    sa = jnp.concatenate([-jnp.sin(ang_m), z32, z64], axis=1)
    sb = jnp.concatenate([z32, jnp.sin(ang_m), z64], axis=1)
    return cr, sr, cm, sa, sb


def _rope_ret(x, cr, sr):
    return x * cr + pltpu.roll(x, HD // 2, 1) * sr


def _rope_ret_t(d, cr, sr):
    return d * cr + pltpu.roll(d * sr, HD // 2, 1)


def _rope_mla(x, cm, sa, sb):
    return x * cm + pltpu.roll(x, HD - ROPE // 2, 1) * sa + pltpu.roll(x, ROPE // 2, 1) * sb


def _rope_mla_t(d, cm, sa, sb):
    return d * cm + pltpu.roll(d * sa, ROPE // 2, 1) + pltpu.roll(d * sb, HD - ROPE // 2, 1)


C_RQ, C_RK, C_RV, C_RG = 0, HEADS * HD, 2 * HEADS * HD, 3 * HEADS * HD
C_CQ = 4 * HEADS * HD
C_CKV = C_CQ + Q_RANK
C_KR = C_CKV + KV_RANK
RET_K_SCALE = HD ** -0.5


def _prep(proj, tabs, qn, kvn):
    L = proj.shape[0]
    tr = _tile(L, 256)
    W = HEADS * HD

    def body(p_ref, cr_ref, sr_ref, cm_ref, sa_ref, sb_ref, qn_ref, kvn_ref, q_ref, k_ref, v_ref, cq_ref, ckv_ref,
             kr_ref):
        cr, sr = cr_ref[...], sr_ref[...]
        for h in range(HEADS):
            sl = slice(h * HD, (h + 1) * HD)
            q_ref[:, sl] = _rope_ret(p_ref[:, C_RQ + h * HD:C_RQ + (h + 1) * HD].astype(F32), cr, sr).astype(BF16)
            k_ref[:, sl] = (_rope_ret(p_ref[:, C_RK + h * HD:C_RK + (h + 1) * HD].astype(F32), cr, sr)
                            * RET_K_SCALE).astype(BF16)
        v_ref[...] = p_ref[:, C_RV:C_RV + W].astype(BF16)
        cq = p_ref[:, C_CQ:C_CQ + Q_RANK].astype(F32)
        cq_ref[...] = (cq * lax.rsqrt(jnp.mean(cq * cq, axis=-1, keepdims=True) + EPS) * qn_ref[...]).astype(BF16)
        ckv = p_ref[:, C_CKV:C_CKV + KV_RANK].astype(F32)
        ckv_ref[...] = (ckv * lax.rsqrt(jnp.mean(ckv * ckv, axis=-1, keepdims=True) + EPS)
                        * kvn_ref[...]).astype(BF16)
        kr_ref[...] = _rope_mla(p_ref[:, C_KR:C_KR + HD].astype(F32), cm_ref[...], sa_ref[...], sb_ref[...]).astype(BF16)

    row = lambda w: pl.BlockSpec((tr, w), lambda i: (i, 0))
    vec = lambda w: pl.BlockSpec((1, w), lambda i: (0, 0))
    return pl.pallas_call(
        body, name="mix_prep", grid=(L // tr,),
        out_shape=[jax.ShapeDtypeStruct((L, W), BF16)] * 3 + [jax.ShapeDtypeStruct((L, Q_RANK), BF16),
                                                              jax.ShapeDtypeStruct((L, KV_RANK), BF16),
                                                              jax.ShapeDtypeStruct((L, HD), BF16)],
        in_specs=[row(D_INP)] + [row(HD)] * 5 + [vec(Q_RANK), vec(KV_RANK)],
        out_specs=[row(W)] * 3 + [row(Q_RANK), row(KV_RANK), row(HD)],
        compiler_params=_params(("parallel",)),
    )(proj, *tabs, qn, kvn)


def _prep_bwd(proj, dq, dk, dv, drg, dcqn, dckvn, dkr8, tabs, qn, kvn):
    L = proj.shape[0]
    tr = _tile(L, 192)
    W = HEADS * HD

    def body(p_ref, dq_ref, dk_ref, dv_ref, drg_ref, dcq_ref, dckv_ref, dkr_ref, cr_ref, sr_ref, cm_ref, sa_ref,
             sb_ref, qn_ref, kvn_ref, dp_ref, dqn_ref, dkvn_ref):
        cr, sr = cr_ref[...], sr_ref[...]
        dkr = None
        for h in range(HEADS):
            sl = slice(h * HD, (h + 1) * HD)
            dp_ref[:, C_RQ + h * HD:C_RQ + (h + 1) * HD] = _rope_ret_t(dq_ref[:, sl].astype(F32), cr, sr).astype(BF16)
            dp_ref[:, C_RK + h * HD:C_RK + (h + 1) * HD] = (_rope_ret_t(dk_ref[:, sl].astype(F32), cr, sr)
                                                            * RET_K_SCALE).astype(BF16)
            part = dkr_ref[:, sl].astype(F32)
            dkr = part if dkr is None else dkr + part
        dp_ref[:, C_RV:C_RV + W] = dv_ref[...].astype(BF16)
        dp_ref[:, C_RG:C_RG + W] = drg_ref[...].astype(BF16)
        dcq, dqn = _norm_bwd_math(p_ref[:, C_CQ:C_CQ + Q_RANK].astype(F32), qn_ref[...], dcq_ref[...])
        dp_ref[:, C_CQ:C_CQ + Q_RANK] = dcq.astype(BF16)
        dckv, dkvn = _norm_bwd_math(p_ref[:, C_CKV:C_CKV + KV_RANK].astype(F32), kvn_ref[...], dckv_ref[...])
        dp_ref[:, C_CKV:C_CKV + KV_RANK] = dckv.astype(BF16)
        dp_ref[:, C_KR:C_KR + HD] = _rope_mla_t(dkr, cm_ref[...], sa_ref[...], sb_ref[...]).astype(BF16)

        @pl.when(pl.program_id(0) == 0)
        def _():
            dqn_ref[...] = jnp.zeros_like(dqn_ref)
            dkvn_ref[...] = jnp.zeros_like(dkvn_ref)

        dqn_ref[...] += dqn
        dkvn_ref[...] += dkvn

    row = lambda w: pl.BlockSpec((tr, w), lambda i: (i, 0))
    vec = lambda w: pl.BlockSpec((1, w), lambda i: (0, 0))
    return pl.pallas_call(
        body, name="mix_prep_bwd", grid=(L // tr,),
        out_shape=[jax.ShapeDtypeStruct((L, D_INP), BF16), jax.ShapeDtypeStruct((1, Q_RANK), F32),
                   jax.ShapeDtypeStruct((1, KV_RANK), F32)],
        in_specs=[row(D_INP)] + [row(W)] * 4 + [row(Q_RANK), row(KV_RANK), row(W)] + [row(HD)] * 5
                 + [vec(Q_RANK), vec(KV_RANK)],
        out_specs=[row(D_INP), vec(Q_RANK), vec(KV_RANK)],
        compiler_params=_params(("arbitrary",)),
    )(proj, dq, dk, dv, drg, dcqn, dckvn, dkr8, *tabs, qn, kvn)


def _post(o_ret, proj, gn):
    L, W = o_ret.shape
    tr = _tile(L, 384)

    def body(o_ref, rg_ref, gn_ref, out_ref):
        for h in range(HEADS):
            sl = slice(h * HD, (h + 1) * HD)
            o = o_ref[:, sl]
            rg = rg_ref[:, sl].astype(F32)
            n = o * lax.rsqrt(jnp.mean(o * o, axis=-1, keepdims=True) + EPS)
            out_ref[:, sl] = (n * gn_ref[:, sl] * (rg * _sigmoid(rg))).astype(BF16)

    row = pl.BlockSpec((tr, W), lambda i: (i, 0))
    return pl.pallas_call(
        body, name="ret_post", grid=(L // tr,), out_shape=jax.ShapeDtypeStruct((L, W), BF16),
        in_specs=[row, pl.BlockSpec((tr, W), lambda i: (i, C_RG // W)), pl.BlockSpec((1, W), lambda i: (0, 0))],
        out_specs=row, compiler_params=_params(("parallel",)),
    )(o_ret, proj, gn)


def _post_bwd(o_ret, proj, gn, dcat):
    L, W = o_ret.shape
    tr = _tile(L, 384)

    def body(o_ref, rg_ref, gn_ref, d_ref, do_ref, drg_ref, dgn_ref):
        @pl.when(pl.program_id(0) == 0)
        def _():
            dgn_ref[...] = jnp.zeros_like(dgn_ref)

        for h in range(HEADS):
            sl = slice(h * HD, (h + 1) * HD)
            o = o_ref[:, sl]
            rg = rg_ref[:, sl].astype(F32)
            d = d_ref[:, sl].astype(F32)
            gw = gn_ref[:, sl]
            r = lax.rsqrt(jnp.mean(o * o, axis=-1, keepdims=True) + EPS)
            n = o * r
            sg = _sigmoid(rg)
            si = rg * sg
            dn = d * gw * si
            dgn_ref[:, sl] += jnp.sum(d * n * si, axis=0, keepdims=True)
            drg_ref[:, sl] = (d * n * gw * sg * (1.0 + rg * (1.0 - sg))).astype(drg_ref.dtype)
            do_ref[:, sl] = (r * (dn - o * (r * r) * jnp.mean(dn * o, axis=-1, keepdims=True))).astype(BF16)

    row = pl.BlockSpec((tr, W), lambda i: (i, 0))
    vec = pl.BlockSpec((1, W), lambda i: (0, 0))
    return pl.pallas_call(
        body, name="ret_post_bwd", grid=(L // tr,),
        out_shape=[jax.ShapeDtypeStruct((L, W), BF16), jax.ShapeDtypeStruct((L, W), BF16),
                   jax.ShapeDtypeStruct((1, W), F32)],
        in_specs=[row, pl.BlockSpec((tr, W), lambda i: (i, C_RG // W)), vec, row],
        out_specs=[row, row, vec], compiler_params=_params(("arbitrary",)),
    )(o_ret, proj, gn, dcat)


RET_HEADS_PER_STEP = 4


def _lin_attn(name, q, k, v, lg, reverse, out_dtype=F32):
    L, W = q.shape
    nc = L // BLK - 1
    G = RET_HEADS_PER_STEP

    def body(q_ref, k_ref, v_ref, lg_ref, o_ref, s_ref):
        n = lax.broadcasted_iota(jnp.int32, (BLK, BLK), 0).astype(F32)
        m = lax.broadcasted_iota(jnp.int32, (BLK, BLK), 1).astype(F32)
        dist = (m - n) if reverse else (n - m)
        consts = []
        for g in range(G):
            lgv = lg_ref[g, 0:1, :]
            dmask = jnp.where(dist >= 0, jnp.exp(lgv * jnp.maximum(dist, 0.0)), 0.0)
            c = dict(dmask=dmask, dmask0=jnp.where((n < N_META) & (m < N_META), dmask, 0.0),
                     gl=jnp.exp(lgv * float(BLK)))
            if reverse:
                c.update(inter=jnp.exp(lgv * (float(BLK) - n)), upd=jnp.exp(lgv * n),
                         inter0=jnp.where(n < N_META, jnp.exp(lgv * jnp.maximum(float(N_META) - n, 0.0)), 0.0))
            else:
                c.update(inter=jnp.exp(lgv * (n + 1.0)), upd=jnp.exp(lgv * (float(BLK) - 1.0 - n)),
                         upd0=jnp.where(n < N_META, jnp.exp(lgv * jnp.maximum(float(N_META) - 1.0 - n, 0.0)), 0.0))
            consts.append(c)

        def chunk(c):
            rows = pl.ds(pl.multiple_of(c * BLK, BLK), BLK)
            state = [s_ref[g] for g in range(G)]
            outs, new_state = [], []
            for g in range(G):
                cols = slice(g * HD, (g + 1) * HD)
                cg = consts[g]
                qc, kc, vc = q_ref[rows, cols], k_ref[rows, cols], v_ref[rows, cols]
                a = _dot(qc, kc, 'nt') * cg['dmask']
                outs.append(_dot(a.astype(BF16), vc, 'nn') + _dot(qc, state[g].astype(BF16), 'nn') * cg['inter'])
                new_state.append(state[g] * cg['gl'] + _dot((kc.astype(F32) * cg['upd']).astype(BF16), vc, 'tn'))
            for g in range(G):
                o_ref[rows, g * HD:(g + 1) * HD] = outs[g].astype(o_ref.dtype)
                s_ref[g] = new_state[g]

        def first_chunk(with_state):
            for g in range(G):
                cols = slice(g * HD, (g + 1) * HD)
                cg = consts[g]
                q0, k0, v0 = q_ref[0:BLK, cols], k_ref[0:BLK, cols], v_ref[0:BLK, cols]
                o0 = _dot((_dot(q0, k0, 'nt') * cg['dmask0']).astype(BF16), v0, 'nn')
                if with_state:
                    o0 = o0 + _dot(q0, s_ref[g].astype(BF16), 'nn') * cg['inter0']
                else:
                    s_ref[g] = _dot((k0.astype(F32) * cg['upd0']).astype(BF16), v0, 'tn')
                o_ref[0:BLK, cols] = o0.astype(o_ref.dtype)

        if reverse:
            s_ref[...] = jnp.zeros_like(s_ref)

            def step(t, carry):
                chunk(nc - t)
                return carry

            lax.fori_loop(0, nc, step, 0)
            first_chunk(True)
        else:
            first_chunk(False)

            def step(t, carry):
                chunk(t + 1)
                return carry

            lax.fori_loop(0, nc, step, 0)

    col = pl.BlockSpec((L, G * HD), lambda h: (0, h))
    return pl.pallas_call(
        body, name=name, grid=(HEADS // G,), out_shape=jax.ShapeDtypeStruct((L, W), out_dtype),
        in_specs=[col, col, col, pl.BlockSpec((G, 8, HD), lambda h: (h, 0, 0))], out_specs=col,
        scratch_shapes=[pltpu.VMEM((G, HD, HD), F32)], compiler_params=_params(("parallel",)),
    )(q, k, v, lg)


ATT_SCALE = (HD + ROPE) ** -0.5
LOG2E = 1.4426950408889634
Q_PRESCALE = ATT_SCALE * LOG2E
NEG = -1e30


ATT_TILE = 384
ATT_HEADS_PER_STEP = 2


def _att_valid(nq, nk, row0, col0):
    r = lax.broadcasted_iota(jnp.int32, (nq, nk), 0) + row0
    c = lax.broadcasted_iota(jnp.int32, (nq, nk), 1) + col0
    return (c <= r) & ((c < N_META) | (c >= BLK))


def _attn_fwd(qm, kn, krr, vm, comm=None):
    L = qm.shape[0]
    W = HEADS * HD
    T = _tile(L, ATT_TILE, BLK)
    nb = L // T
    G = ATT_HEADS_PER_STEP
    n_cm = comm.n if comm is not None else 0

    def body(*refs):
        q_ref, kn_ref, kr_ref, v_ref = refs[:4]
        o_ref, lse_ref = refs[4 + n_cm:6 + n_cm]
        m_sc, l_sc, acc_sc = refs[6 + 2 * n_cm:9 + 2 * n_cm]
        if comm is not None:
            cm_refs = (refs[4:4 + n_cm], refs[6 + n_cm:6 + 2 * n_cm], refs[9 + 2 * n_cm:])
            first, last = _grid_edges((HEADS // G, nb))

            @pl.when(first)
            def _():
                comm.start(*cm_refs)

        i = pl.program_id(1)
        m_sc[...] = jnp.full_like(m_sc, NEG)
        l_sc[...] = jnp.zeros_like(l_sc)
        acc_sc[...] = jnp.zeros_like(acc_sc)

        def tile(j, masked):
            rows = pl.ds(pl.multiple_of(j * T, T), T)
            kr = kr_ref[rows, :]
            valid = _att_valid(T, T, i * T, j * T) if masked else None
            ones = jnp.ones((T, HD), BF16)
            m_prev = [m_sc[g] for g in range(G)]
            l_prev = [l_sc[g] for g in range(G)]
            acc_prev = [acc_sc[g] for g in range(G)]
            m_new, l_new, acc_new = [], [], []
            for g in range(G):
                k = jnp.concatenate([kn_ref[rows, g * HD:(g + 1) * HD], kr], axis=1)
                s = _dot(q_ref[:, g * QH:(g + 1) * QH], k, 'nt')
                if masked:
                    s = jnp.where(valid, s, NEG)
                m_new.append(jnp.maximum(m_prev[g], jnp.max(s, axis=-1, keepdims=True)))
                p = jnp.exp2(s - m_new[g])
                alpha = jnp.exp2(m_prev[g] - m_new[g])
                pv = _dot(p.astype(BF16), jnp.concatenate([v_ref[rows, g * HD:(g + 1) * HD], ones], axis=1), 'nn')
                l_new.append(alpha * l_prev[g] + pv[:, HD:HD + 1])
                acc_new.append(alpha * acc_prev[g] + pv[:, 0:HD])
            for g in range(G):
                m_sc[g] = m_new[g]
                l_sc[g] = l_new[g]
                acc_sc[g] = acc_new[g]

        tile(0, True)

        def mid(j, carry):
            tile(j, False)
            return carry

        lax.fori_loop(1, i, mid, 0)

        @pl.when(i > 0)
        def _():
            tile(i, True)

        for g in range(G):
            l = l_sc[g]
            o_ref[:, g * HD:(g + 1) * HD] = (acc_sc[g] / l).astype(o_ref.dtype)
            lse_ref[g] = jnp.broadcast_to(m_sc[g] + jnp.log(l) * LOG2E, (T, HD))

        if comm is not None:
            @pl.when(last)
            def _():
                comm.finish(*cm_refs)

    cm_specs = comm.specs if comm is not None else []
    res = pl.pallas_call(
        body, name="attn_fwd", grid=(HEADS // G, nb),
        out_shape=[jax.ShapeDtypeStruct((L, W), BF16), jax.ShapeDtypeStruct((HEADS, L, HD), F32)]
        + (comm.out_shapes if comm is not None else []),
        in_specs=[pl.BlockSpec((T, G * QH), lambda h, i: (i, h)), pl.BlockSpec((L, G * HD), lambda h, i: (0, h)),
                  pl.BlockSpec((L, HD), lambda h, i: (0, 0)), pl.BlockSpec((L, G * HD), lambda h, i: (0, h))]
        + cm_specs,
        out_specs=[pl.BlockSpec((T, G * HD), lambda h, i: (i, h)),
                   pl.BlockSpec((G, T, HD), lambda h, i: (h, i, 0))] + cm_specs,
        scratch_shapes=[pltpu.VMEM((G, T, 1), F32), pltpu.VMEM((G, T, 1), F32), pltpu.VMEM((G, T, HD), F32)]
        + (comm.scratch if comm is not None else []),
        compiler_params=_params(("arbitrary", "arbitrary")),
    )(qm, kn, krr, vm, *(comm.arrays if comm is not None else []))
    return res[:2], res[2:]


def _attn_bwd(qm, kn, krr, vm, o, dcat, lse, comm=None):
    L = qm.shape[0]
    W = HEADS * HD
    T = _tile(L, ATT_TILE, BLK)
    nb = L // T
    n_cm = comm.n if comm is not None else 0

    def body(*refs):
        q_ref, kn_ref, kr_ref, v_ref, o_ref, do_ref, lse_ref = refs[:7]
        dq_ref, dkn_ref, dkr_ref, dv_ref = refs[7 + n_cm:11 + n_cm]
        dl_sc, dk_sc, dv_sc = refs[11 + 2 * n_cm:14 + 2 * n_cm]
        if comm is not None:
            cm_refs = (refs[7:7 + n_cm], refs[11 + n_cm:11 + 2 * n_cm], refs[14 + 2 * n_cm:])
            first, last = _grid_edges((HEADS, nb))

            @pl.when(first)
            def _():
                comm.start(*cm_refs)

        j = pl.program_id(1)

        @pl.when(j == 0)
        def _():
            dq_ref[...] = jnp.zeros_like(dq_ref)

            def rowsum(t, carry):
                rows = pl.ds(pl.multiple_of(t * T, T), T)
                dl_sc[rows, :] = jnp.sum(do_ref[rows, :].astype(F32) * o_ref[rows, :].astype(F32), axis=-1,
                                         keepdims=True)
                return carry

            lax.fori_loop(0, nb, rowsum, 0)

        k = jnp.concatenate([kn_ref[...], kr_ref[...]], axis=1)
        v = v_ref[...]
        dk_sc[...] = jnp.zeros_like(dk_sc)
        dv_sc[...] = jnp.zeros_like(dv_sc)

        def tile(i, masked):
            rows = pl.ds(pl.multiple_of(i * T, T), T)
            q = q_ref[rows, :]
            do = do_ref[rows, :]
            s = _dot(q, k, 'nt')
            if masked:
                s = jnp.where(_att_valid(T, T, i * T, j * T), s, NEG)
            p = jnp.exp2(s - lse_ref[rows, 0:1])
            dv_sc[...] += _dot(p.astype(BF16), do, 'tn')
            ds = (p * (_dot(do, v, 'nt') - dl_sc[rows, :])).astype(BF16)
            dk_sc[...] += _dot(ds, q, 'tn')
            dq_ref[rows, :] += _dot(ds, k, 'nn')

        tile(j, True)

        def rest(masked):
            def step(i, carry):
                tile(i, masked)
                return carry
            lax.fori_loop(j + 1, nb, step, 0)

        @pl.when(j == 0)
        def _():
            rest(True)

        @pl.when(j > 0)
        def _():
            rest(False)

        dk = dk_sc[...] * (1.0 / LOG2E)
        dkn_ref[...] = dk[:, 0:HD].astype(BF16)
        dkr_ref[...] = dk[:, HD:QH].astype(dkr_ref.dtype)
        dv_ref[...] = dv_sc[...].astype(BF16)

        if comm is not None:
            @pl.when(last)
            def _():
                comm.finish(*cm_refs)

    blk = pl.BlockSpec((T, HD), lambda h, j: (j, h))
    cm_specs = comm.specs if comm is not None else []
    res = pl.pallas_call(
        body, name="attn_bwd", grid=(HEADS, nb),
        out_shape=[jax.ShapeDtypeStruct((L, HEADS * QH), F32), jax.ShapeDtypeStruct((L, W), BF16),
                   jax.ShapeDtypeStruct((L, W), BF16), jax.ShapeDtypeStruct((L, W), BF16)]
        + (comm.out_shapes if comm is not None else []),
        in_specs=[pl.BlockSpec((L, QH), lambda h, j: (0, h)), blk, pl.BlockSpec((T, HD), lambda h, j: (j, 0)), blk,
                  pl.BlockSpec((L, HD), lambda h, j: (0, h)), pl.BlockSpec((L, HD), lambda h, j: (0, HEADS + h)),
                  pl.BlockSpec((None, L, HD), lambda h, j: (h, 0, 0))] + cm_specs,
        out_specs=[pl.BlockSpec((L, QH), lambda h, j: (0, h)), blk, blk, blk] + cm_specs,
        scratch_shapes=[pltpu.VMEM((L, 1), F32), pltpu.VMEM((T, QH), F32), pltpu.VMEM((T, HD), F32)]
        + (comm.scratch if comm is not None else []),
        compiler_params=_params(("arbitrary", "arbitrary")),
    )(qm, kn, krr, vm, o, dcat, lse, *(comm.arrays if comm is not None else []))
    return res[:4], res[4:]


def _unrope_q(dqm, tabs_m):
    L, W = dqm.shape
    tr = _tile(L, 384)

    def body(d_ref, cm_ref, sa_ref, sb_ref, out_ref):
        cm, sa, sb = cm_ref[...], sa_ref[...], sb_ref[...]
        for h in range(HEADS):
            out_ref[:, h * QH:h * QH + HD] = (d_ref[:, h * QH:h * QH + HD] * ATT_SCALE).astype(BF16)
            out_ref[:, h * QH + HD:(h + 1) * QH] = _rope_mla_t(d_ref[:, h * QH + HD:(h + 1) * QH] * ATT_SCALE, cm, sa,
                                                               sb).astype(BF16)

    row = pl.BlockSpec((tr, W), lambda i: (i, 0))
    tab = pl.BlockSpec((tr, HD), lambda i: (i, 0))
    return pl.pallas_call(
        body, name="unrope_q", grid=(L // tr,), out_shape=jax.ShapeDtypeStruct((L, W), BF16),
        in_specs=[row, tab, tab, tab], out_specs=row, compiler_params=_params(("parallel",)),
    )(dqm, *tabs_m)


def _q_up(cqn, wuq_p, tabs_m):
    L = cqn.shape[0]
    tm = _tile(L, 704)

    def ep(acc, cm, sa, sb):
        acc = acc * Q_PRESCALE
        parts = []
        for h in range(HEADS):
            parts.append(acc[:, h * QH:h * QH + HD])
            parts.append(_rope_mla(acc[:, h * QH + HD:(h + 1) * QH], cm, sa, sb))
        return (jnp.concatenate(parts, axis=1),)

    tab = pl.BlockSpec((tm, HD), lambda i, j: (i, 0))
    return _mm("mla_q_up", (L // tm, 1), ("parallel", "parallel"), None,
               [cqn, wuq_p], [pl.BlockSpec((tm, Q_RANK), lambda i, j: (i, 0)),
                              pl.BlockSpec((HEADS * QH, Q_RANK), lambda i, j: (0, 0))],
               [(0, 1, 'nt', 0)], [(tm, HEADS * QH)], list(tabs_m), [tab] * 3, ep,
               [jax.ShapeDtypeStruct((L, HEADS * QH), BF16)], [pl.BlockSpec((tm, HEADS * QH), lambda i, j: (i, 0))])[0]


def _mix_out(cat, w_out, h_in, post, next_norm):
    L, K = cat.shape
    D = w_out.shape[1]
    tm, tk = _tile(L, 384), K
    row = pl.BlockSpec((tm, D), lambda i, k: (i, 0))
    vec = pl.BlockSpec((1, D), lambda i, k: (0, 0))
    return _mm("mix_out", (L // tm, K // tk), ("parallel", "arbitrary"), 1,
               [cat, w_out], [pl.BlockSpec((tm, tk), lambda i, k: (i, k)), pl.BlockSpec((tk, D), lambda i, k: (k, 0))],
               [(0, 1, 'nn', 0)], [(tm, D)], [h_in, post, next_norm], [row, vec, vec], _resnorm_epilogue(1.0, True),
               [jax.ShapeDtypeStruct((L, D), F32)] * 2 + [jax.ShapeDtypeStruct((L, D), BF16)], [row, row, row])


ADAM_BLOCK_ELEMS = 512 * 704


def _adam_math(w, g, m, v):
    m = ADAM_B1 * m + (1.0 - ADAM_B1) * g
    v = ADAM_B2 * v + (1.0 - ADAM_B2) * (g * g)
    m_hat = m / (1.0 - ADAM_B1 ** ADAM_STEP)
    v_hat = v / (1.0 - ADAM_B2 ** ADAM_STEP)
    delta = -ADAM_LR * (m_hat / (jnp.sqrt(v_hat) + ADAM_EPS) + ADAM_WD * w)
    return delta, m, v


def _adam(name, w, m, v, g_slots=None, g=None, after=None):
    R, C = w.shape
    tr, tc = _tile(R, max(16, ADAM_BLOCK_ELEMS // C // 16 * 16), 16), C
    if tr * tc > ADAM_BLOCK_ELEMS:
        tr, tc = R, _tile(C, max(128, ADAM_BLOCK_ELEMS // R // 128 * 128), 128)
    from_slots = g_slots is not None

    def body(w_ref, m_ref, v_ref, g_ref, *rest):
        go_ref, d_ref, mo_ref, vo_ref = rest[-4:]
        if from_slots:
            grad = g_ref[0].astype(F32)
            for s in range(1, N_DEV):
                grad = grad + g_ref[s].astype(F32)
        else:
            grad = g_ref[...]
        delta, mn, vn = _adam_math(w_ref[...], grad, m_ref[...], v_ref[...])
        go_ref[...] = grad
        d_ref[...] = delta
        mo_ref[...] = mn
        vo_ref[...] = vn

    row = pl.BlockSpec((tr, tc), lambda i, j: (i, j))
    gspec = pl.BlockSpec((N_DEV, tr, tc), lambda i, j: (0, i, j)) if from_slots else row
    order = [] if after is None else [after]
    return pl.pallas_call(
        body, name=name, grid=(R // tr, C // tc), out_shape=[jax.ShapeDtypeStruct((R, C), F32)] * 4,
        in_specs=[row, row, row, gspec] + [pl.BlockSpec(memory_space=pl.ANY)] * len(order), out_specs=[row] * 4,
        compiler_params=_params(("parallel", "parallel")),
    )(w, m, v, g_slots if from_slots else g, *order)


def _unblock(gathered):
    n, r, c = gathered.shape
    return jnp.transpose(gathered, (1, 0, 2)).reshape(r, n * c)


def _reblock(full, c):
    r = full.shape[0]
    return jnp.transpose(full[:, :N_DEV * c].reshape(r, N_DEV, c), (1, 0, 2))


def _step(x, target, w, mom, vel):
    S, D = x.shape[1], x.shape[2]
    L = S + BLK
    def sq(a, n):
        if a.ndim == 2:
            return a
        if n in TRANSPOSED:
            a = jnp.swapaxes(a, 1, 2)
        return a.reshape(a.shape[1:])

    def unsq(o, n):
        o = o.reshape((1,) + o.shape)
        return jnp.swapaxes(o, 1, 2) if n in TRANSPOSED else o

    p = {n: sq(w[n], n) for n in WEIGHTS if n != 'meta_tokens'}
    gather = lambda names: _Exchange([p[n].astype(BF16) for n in names], False)
    scatter = lambda blocks: _Exchange(blocks, True)
    in_s, uq_s = p['w_in'].shape[0], p['mla_w_uq'].shape[0]
    assert uq_s == HD + ROPE and N_DEV == HEADS, "a w_uq shard is one head's columns"
    tabs = _rope_tables(L)
    tabs_m = tabs[2:]
    lg = jnp.broadcast_to(jnp.log(1.0 - 2.0 ** (-5.0 - jnp.arange(HEADS, dtype=F32)))[:, None, None], (HEADS, 8, HD))
    R = {}

    wg1, meta = _exchange("gather_first", [p['ffn1_w_gate'].astype(BF16), w['meta_tokens']], False)
    h0 = jnp.concatenate([_unblock(meta), jnp.zeros((BLK - N_META, D), F32), x[0]], axis=0)
    a1 = _norm_fwd(h0, p['ffn1_pre_norm'])
    g1, (wu1,) = _ffn_gate(a1, wg1, comm=gather(['ffn1_w_up']))
    (u1, hid1), (wd1,) = _ffn_up_gated(a1, wu1, g1, comm=gather(['ffn1_w_down']))
    (f1, h1, um), (w_in_g,) = _ffn_down(hid1, wd1, h0, p['ffn1_post_norm'], next_norm=p['mix_pre_norm'],
                                        comm=gather(['w_in']))

    w_in = jnp.pad(w_in_g.reshape(N_DEV * in_s, D), ((0, D_INP - N_DEV * in_s), (0, 0)))
    proj, (uq_g, uk_g, uv_g, wout_g) = _mm_nt("mix_in", [(um, w_in)], BF16, tn_target=1664,
                                              comm=gather(['mla_w_uq', 'mla_w_uk', 'mla_w_uv', 'w_out']))
    wuq = jnp.pad(uq_g, ((0, 0), (0, QH - uq_s), (0, 0))).reshape(HEADS * QH, Q_RANK)
    wuk, wuv, w_out = _unblock(uk_g), _unblock(uv_g), wout_g.reshape(-1, D)
    qr, kr, vr, cqn, ckvn, krr = _prep(proj, tabs, p['mla_q_norm'], p['mla_kv_norm'])
    qm = _q_up(cqn, wuq, tabs_m)
    kn = _mm_nn("mla_k_up", ckvn, wuk, BF16)
    vm = _mm_nn("mla_v_up", ckvn, wuv, BF16)
    (o_mla, lse), (wg2, wu2) = _attn_fwd(qm, kn, krr, vm, comm=gather(['ffn2_w_gate', 'ffn2_w_up']))
    o_ret = _lin_attn("ret_fwd", qr, kr, vr, lg, False)
    ret = _post(o_ret, proj, p['ret_group_norm'])
    cat = jnp.concatenate([ret, o_mla], axis=1)
    m, h2, a2 = _mix_out(cat, w_out, h1, p['mix_post_norm'], p['ffn2_pre_norm'])

    (g2, u2, hid2), (wd2,) = _ffn_up(a2, wg2, wu2, comm=gather(['ffn2_w_down']))
    f2, h3 = _ffn_down(hid2, wd2, h2, p['ffn2_post_norm'])
    dh3, loss_blk = _loss(h3, target[0])

    dsmall = {}
    df2, dsmall['ffn2_post_norm'] = _norm_bwd(f2, p['ffn2_post_norm'], dh3, None, 0.5, BF16)
    dg2, du2 = _ffn_dhid(df2, wd2, g2, u2)
    dwd2 = _ffn_dwd(hid2, df2)
    dwg2, dwu2 = _ffn_dwgu(a2, dg2, du2)
    da2, (R['ffn2_w_down'],) = _ffn_da(dg2, du2, wg2, wu2, comm=scatter([dwd2]))
    dh2, dsmall['ffn2_pre_norm'] = _norm_bwd(h2, p['ffn2_pre_norm'], da2, dh3, 1.0, F32)

    dm, dsmall['mix_post_norm'] = _norm_bwd(m, p['mix_post_norm'], dh2, None, 1.0, BF16)
    dcat = _mm_nt("mix_dcat", [(dm, w_out)], BF16)
    dwout = _mm_tn("mix_dwout", cat, [dm])[0]
    do_ret, drg, dsmall['ret_group_norm'] = _post_bwd(o_ret, proj, p['ret_group_norm'], dcat)
    dqr = _lin_attn("ret_dq", do_ret, vr, kr, lg, False, BF16)
    dkr = _lin_attn("ret_dk", vr, do_ret, qr, lg, True, BF16)
    dvr = _lin_attn("ret_dv", kr, qr, do_ret, lg, True, BF16)
    (dqm, dkn, dkr8, dvm), (R['ffn2_w_gate'], R['ffn2_w_up'], R['w_out']) = _attn_bwd(
        qm, kn, krr, vm, o_mla, dcat, lse, comm=scatter([dwg2, dwu2, dwout.reshape(N_DEV, -1, D)]))
    dqp = _unrope_q(dqm, tabs_m)
    dwuq = _mm_tn("mla_dwuq", dqp, [cqn])[0]
    dcqn = _mm_nn("mla_dcq", dqp, wuq, F32)
    dwuk, dwuv = _mm_tn("mla_dwukv", ckvn, [dkn, dvm])
    dckvn = _mm_nt("mla_dckv", [(dkn, wuk), (dvm, wuv)], F32)
    dproj, dsmall['mla_q_norm'], dsmall['mla_kv_norm'] = _prep_bwd(
        proj, dqr, dkr, dvr, drg, dcqn, dckvn, dkr8, tabs, p['mla_q_norm'], p['mla_kv_norm'])
    dwuq_b = dwuq.reshape(HEADS, QH, Q_RANK)[:, :uq_s]
    (dwin,), (R['mla_w_uq'], R['mla_w_uk'], R['mla_w_uv']) = _mm_tn(
        "mix_dwin", dproj, [um], comm=scatter([dwuq_b, _reblock(dwuk, p['mla_w_uk'].shape[1]),
                                               _reblock(dwuv, p['mla_w_uv'].shape[1])]))
    dwin_b = dwin[:N_DEV * in_s].reshape(N_DEV, in_s, D)
    half = D // 2
    dum, (r_win_a,) = _mm_nn("mix_du", dproj, w_in, F32, tn_target=512, comm=scatter([dwin_b[:, :, :half]]))
    dh1, dsmall['mix_pre_norm'] = _norm_bwd(h1, p['mix_pre_norm'], dum, dh2, 1.0, F32)

    df1, dsmall['ffn1_post_norm'] = _norm_bwd(f1, p['ffn1_post_norm'], dh1, None, 0.5, BF16)
    (dg1, du1), (r_win_b,) = _ffn_dhid(df1, wd1, g1, u1, comm=scatter([dwin_b[:, :, half:]]))
    R['w_in'] = jnp.concatenate([r_win_a, r_win_b], axis=2)
    dwd1 = _ffn_dwd(hid1, df1)
    (dwg1, dwu1), (R['ffn1_w_down'],) = _ffn_dwgu(a1, dg1, du1, comm=scatter([dwd1]))
    da1, (R['ffn1_w_gate'],) = _ffn_da(dg1, du1, wg1, wu1, comm=scatter([dwg1]))
    dh0, dsmall['ffn1_pre_norm'] = _norm_bwd(h0, p['ffn1_pre_norm'], da1, dh1, 1.0, F32)
    tail_sems_s, tail_sems_r, tail_src, tail_land, token = _scatter_start(dwu1)

    def slab(a):
        a = a.reshape(-1, 128)
        return jnp.pad(a, ((0, (-a.shape[0]) % 8), (0, 0)))

    slab_rows = lambda n: -(-(p[n].shape[-1] // 128) // 8) * 8
    packed = jnp.concatenate([slab(dsmall[n]) for n in SMALL] + [slab(dh0[:N_META]), loss_blk], axis=0)
    red = _allreduce_small(packed + token[0, 0])
    offs = sum(slab_rows(n) for n in SMALL)
    n_small = offs
    gmeta_full = red[offs:offs + N_META * D // 128].reshape(N_META, D)
    offs += N_META * D // 128
    loss = red[offs, 0]

    grad, delta, new_m, new_v = {}, {}, {}, {}
    meanwhile = []
    for n in BIG:
        if n == 'ffn1_w_up':
            continue
        outs = _adam("adam_" + n, p[n], sq(mom[n], n), sq(vel[n], n), g_slots=R[n], after=token)
        meanwhile.append(outs[0])
        grad[n], delta[n], new_m[n], new_v[n] = [unsq(o, n) for o in outs]
    pack = lambda d: jnp.concatenate([slab(d[n]) for n in SMALL], axis=0)
    outs = _adam("adam_small", pack(w), pack(mom), pack(vel), g=red[:n_small])
    meanwhile.append(outs[0])
    offs = 0
    for n in SMALL:
        r = p[n].shape[-1] // 128
        grad[n], delta[n], new_m[n], new_v[n] = [o[offs:offs + r].reshape(w[n].shape) for o in outs]
        offs += slab_rows(n)
    dev = 4 * lax.axis_index("x") + 2 * lax.axis_index("y") + lax.axis_index("c")
    mcols = w['meta_tokens'].shape[1]
    gmeta = lax.dynamic_slice(gmeta_full, (0, dev * mcols), (N_META, mcols))
    outs = _adam("adam_meta", w['meta_tokens'], mom['meta_tokens'], vel['meta_tokens'], g=gmeta)
    grad['meta_tokens'], delta['meta_tokens'], new_m['meta_tokens'], new_v['meta_tokens'] = outs
    meanwhile.append(outs[0])
    n = 'ffn1_w_up'
    slots = _scatter_wait(tail_sems_s, tail_sems_r, tail_src, tail_land, meanwhile)
    outs = _adam("adam_" + n, p[n], sq(mom[n], n), sq(vel[n], n), g_slots=slots)
    grad[n], delta[n], new_m[n], new_v[n] = [unsq(o, n) for o in outs]

    return (loss, dh0[BLK:][None], *[grad[n] for n in WEIGHTS], *[delta[n] for n in WEIGHTS],
            *[new_m[n] for n in WEIGHTS], *[new_v[n] for n in WEIGHTS])


def kernel(x, meta_tokens, ffn1_pre_norm, ffn1_w_gate, ffn1_w_up, ffn1_w_down, ffn1_post_norm, mix_pre_norm, w_in, ret_group_norm, mla_q_norm, mla_w_uq, mla_kv_norm, mla_w_uk, mla_w_uv, w_out, mix_post_norm, ffn2_pre_norm, ffn2_w_gate, ffn2_w_up, ffn2_w_down, ffn2_post_norm, loss_target, m_meta_tokens, m_ffn1_pre_norm, m_ffn1_w_gate, m_ffn1_w_up, m_ffn1_w_down, m_ffn1_post_norm, m_mix_pre_norm, m_w_in, m_ret_group_norm, m_mla_q_norm, m_mla_w_uq, m_mla_kv_norm, m_mla_w_uk, m_mla_w_uv, m_w_out, m_mix_post_norm, m_ffn2_pre_norm, m_ffn2_w_gate, m_ffn2_w_up, m_ffn2_w_down, m_ffn2_post_norm, v_meta_tokens, v_ffn1_pre_norm, v_ffn1_w_gate, v_ffn1_w_up, v_ffn1_w_down, v_ffn1_post_norm, v_mix_pre_norm, v_w_in, v_ret_group_norm, v_mla_q_norm, v_mla_w_uq, v_mla_kv_norm, v_mla_w_uk, v_mla_w_uv, v_w_out, v_mix_post_norm, v_ffn2_pre_norm, v_ffn2_w_gate, v_ffn2_w_up, v_ffn2_w_down, v_ffn2_post_norm):
    w = dict(zip(WEIGHTS, (meta_tokens, ffn1_pre_norm, ffn1_w_gate, ffn1_w_up, ffn1_w_down, ffn1_post_norm,
                           mix_pre_norm, w_in, ret_group_norm, mla_q_norm, mla_w_uq, mla_kv_norm, mla_w_uk, mla_w_uv,
                           w_out, mix_post_norm, ffn2_pre_norm, ffn2_w_gate, ffn2_w_up, ffn2_w_down, ffn2_post_norm)))
    mom = dict(zip(WEIGHTS, (m_meta_tokens, m_ffn1_pre_norm, m_ffn1_w_gate, m_ffn1_w_up, m_ffn1_w_down,
                             m_ffn1_post_norm, m_mix_pre_norm, m_w_in, m_ret_group_norm, m_mla_q_norm, m_mla_w_uq,
                             m_mla_kv_norm, m_mla_w_uk, m_mla_w_uv, m_w_out, m_mix_post_norm, m_ffn2_pre_norm,
                             m_ffn2_w_gate, m_ffn2_w_up, m_ffn2_w_down, m_ffn2_post_norm)))
    vel = dict(zip(WEIGHTS, (v_meta_tokens, v_ffn1_pre_norm, v_ffn1_w_gate, v_ffn1_w_up, v_ffn1_w_down,
                             v_ffn1_post_norm, v_mix_pre_norm, v_w_in, v_ret_group_norm, v_mla_q_norm, v_mla_w_uq,
                             v_mla_kv_norm, v_mla_w_uk, v_mla_w_uv, v_w_out, v_mix_post_norm, v_ffn2_pre_norm,
                             v_ffn2_w_gate, v_ffn2_w_up, v_ffn2_w_down, v_ffn2_post_norm)))
    return _step(x, loss_target, w, mom, vel)
```

```python
import functools
import math

import jax
import jax.numpy as jnp
from jax import lax
from jax.experimental import pallas as pl
from jax.experimental.pallas import tpu as pltpu

N_DEV = 8
N_META = 16
BLK = 128
HEADS = 8
HD = 128
ROPE = 64
Q_RANK = 512
KV_RANK = 256
QH = 2 * HD
D_INP = 4 * HEADS * HD + Q_RANK + KV_RANK + BLK
ROPE_THETA = 10000.0
EPS = 1e-6
ADAM_LR = 0.001
ADAM_B1 = 0.9
ADAM_B2 = 0.999
ADAM_EPS = 1e-08
ADAM_WD = 0.01
ADAM_STEP = 10
V7X_VMEM_LIMIT = 48 * 1024 * 1024
MESH = pl.DeviceIdType.MESH
F32 = jnp.float32
BF16 = jnp.bfloat16

WEIGHTS = ['meta_tokens', 'ffn1_pre_norm', 'ffn1_w_gate', 'ffn1_w_up', 'ffn1_w_down', 'ffn1_post_norm',
           'mix_pre_norm', 'w_in', 'ret_group_norm', 'mla_q_norm', 'mla_w_uq', 'mla_kv_norm', 'mla_w_uk',
           'mla_w_uv', 'w_out', 'mix_post_norm', 'ffn2_pre_norm', 'ffn2_w_gate', 'ffn2_w_up', 'ffn2_w_down',
           'ffn2_post_norm']
SMALL = ['ffn1_pre_norm', 'ffn1_post_norm', 'mix_pre_norm', 'ret_group_norm', 'mla_q_norm', 'mla_kv_norm',
         'mix_post_norm', 'ffn2_pre_norm', 'ffn2_post_norm']
TRANSPOSED = ('ffn1_w_gate', 'ffn1_w_up', 'ffn2_w_gate', 'ffn2_w_up', 'w_in', 'mla_w_uq')
BIG = ['ffn1_w_gate', 'ffn1_w_up', 'ffn1_w_down', 'w_in', 'mla_w_uq', 'mla_w_uk', 'mla_w_uv', 'w_out',
       'ffn2_w_gate', 'ffn2_w_up', 'ffn2_w_down']

_DIMS = {'nn': (((1,), (0,)), ((), ())), 'nt': (((1,), (1,)), ((), ())), 'tn': (((0,), (0,)), ((), ()))}


def _tile(n, target, mult=16):
    best = None
    for t in range(mult, min(n, target) + 1, mult):
        if n % t == 0:
            best = t
    return best if best is not None else n


def _params(sem):
    return pltpu.CompilerParams(dimension_semantics=sem, vmem_limit_bytes=V7X_VMEM_LIMIT)


def _dot(a, b, dims):
    return lax.dot_general(a, b, _DIMS[dims], preferred_element_type=F32)


def _sigmoid(x):
    return 0.5 * jnp.tanh(0.5 * x) + 0.5


def _me_and_peers():
    x, y, c = lax.axis_index("x"), lax.axis_index("y"), lax.axis_index("c")

    def peer(j):
        px = 1 - x if (j >> 2) & 1 else x
        py = 1 - y if (j >> 1) & 1 else y
        pc = 1 - c if j & 1 else c
        return (px, py, pc), 4 * px + 2 * py + pc

    return 4 * x + 2 * y + c, peer


class _Exchange:
    def __init__(self, arrays, per_peer):
        self.arrays = list(arrays)
        self.per_peer = per_peer
        self.n = len(self.arrays)
        self.out_shapes = [jax.ShapeDtypeStruct((N_DEV,) + tuple(a.shape[1:] if per_peer else a.shape), a.dtype)
                           for a in self.arrays]
        self.specs = [pl.BlockSpec(memory_space=pl.ANY)] * self.n
        self.scratch = [pltpu.SemaphoreType.DMA((7 * self.n,)), pltpu.SemaphoreType.DMA((7 * self.n,)),
                        pltpu.SemaphoreType.DMA((self.n,))]

    def _copies(self, src, dst, sems):
        send_sems, recv_sems, local_sems = sems
        me, peer = _me_and_peers()
        sib, _ = peer(1)
        local, sends, recvs, passes = [], {}, {}, {}
        for k in range(self.n):
            own = src[k].at[me] if self.per_peer else src[k]
            local.append(pltpu.make_async_copy(own, dst[k].at[me], local_sems.at[k]))
            for j in range(1, N_DEV):
                pid, pidx = peer(j)
                out = src[k].at[pidx] if self.per_peer else src[k]
                sem = dict(send_sem=send_sems.at[k * 7 + j - 1], recv_sem=recv_sems.at[k * 7 + j - 1])
                recvs[k, j] = pltpu.make_async_remote_copy(src_ref=out, dst_ref=dst[k].at[pidx], device_id=pid,
                                                           device_id_type=MESH, **sem)
                if self.per_peer or j in (1, 2, 4, 6):
                    sends[k, j] = pltpu.make_async_remote_copy(src_ref=out, dst_ref=dst[k].at[me], device_id=pid,
                                                               device_id_type=MESH, **sem)
                else:
                    _, origin = peer(j ^ 1)
                    passes[k, j ^ 1] = pltpu.make_async_remote_copy(
                        src_ref=dst[k].at[origin], dst_ref=dst[k].at[origin], device_id=sib, device_id_type=MESH, **sem)
        return local, sends, recvs, passes

    def start(self, src, dst, sems):
        local, sends, _, _ = self._copies(src, dst, sems)
        for cp in local + list(sends.values()):
            cp.start()

    def finish(self, src, dst, sems):
        local, sends, recvs, passes = self._copies(src, dst, sems)
        for key, cp in passes.items():
            recvs[key].wait_recv()
            cp.start()
        for key, cp in recvs.items():
            if key not in passes:
                cp.wait_recv()
        for cp in list(sends.values()) + list(passes.values()):
            cp.wait_send()
        for cp in local:
            cp.wait()


def _grid_edges(grid):
    first, last = None, None
    for a, n in enumerate(grid):
        f, l = pl.program_id(a) == 0, pl.program_id(a) == n - 1
        first = f if first is None else first & f
        last = l if last is None else last & l
    return first, last


def _exchange(name, arrays, per_peer):
    ex = _Exchange(arrays, per_peer)
    n = ex.n

    def body(*refs):
        ex.start(refs[:n], refs[n:2 * n], refs[2 * n:])
        ex.finish(refs[:n], refs[n:2 * n], refs[2 * n:])

    return pl.pallas_call(body, name=name, out_shape=ex.out_shapes, in_specs=ex.specs, out_specs=ex.specs,
                          scratch_shapes=ex.scratch)(*arrays)


def _scatter_start(blocks):
    def body(src_ref, land_ref, send_sems, recv_sems, src_thru, land_thru, token, local_sem):
        me, peer = _me_and_peers()
        local = pltpu.make_async_copy(src_ref.at[me], land_ref.at[me], local_sem)
        local.start()
        for j in range(1, N_DEV):
            pid, pidx = peer(j)
            pltpu.make_async_remote_copy(src_ref=src_ref.at[pidx], dst_ref=land_ref.at[me],
                                         send_sem=send_sems.at[j - 1], recv_sem=recv_sems.at[j - 1],
                                         device_id=pid, device_id_type=MESH).start()
        local.wait()
        token[...] = jnp.zeros_like(token)

    hbm = pl.BlockSpec(memory_space=pltpu.HBM)
    sem = pl.BlockSpec(memory_space=pltpu.SEMAPHORE)
    return pl.pallas_call(
        body, name="scatter_tail_start",
        out_shape=(pltpu.SemaphoreType.DMA((7,)), pltpu.SemaphoreType.DMA((7,)), pltpu.HBM(blocks.shape, blocks.dtype),
                   pltpu.HBM(blocks.shape, blocks.dtype), jax.ShapeDtypeStruct((8, 128), F32)),
        in_specs=(hbm, hbm), out_specs=(sem, sem, hbm, hbm, pl.BlockSpec(memory_space=pltpu.VMEM)),
        input_output_aliases={0: 2, 1: 3}, scratch_shapes=[pltpu.SemaphoreType.DMA],
        compiler_params=pltpu.CompilerParams(has_side_effects=pltpu.SideEffectType.DATAFLOW_SIDE_EFFECTING),
    )(pltpu.with_memory_space_constraint(blocks, pltpu.HBM),
      pltpu.with_memory_space_constraint(lax.empty(blocks.shape, blocks.dtype), pltpu.HBM))


def _scatter_wait(send_sems, recv_sems, src_thru, land_thru, after):
    n_after = len(after)

    def body(src_ref, land_ref, send_sems, recv_sems, *rest):
        me, peer = _me_and_peers()
        for j in range(1, N_DEV):
            pid, pidx = peer(j)
            cp = pltpu.make_async_remote_copy(src_ref=src_ref.at[pidx], dst_ref=land_ref.at[pidx],
                                              send_sem=send_sems.at[j - 1], recv_sem=recv_sems.at[j - 1],
                                              device_id=pid, device_id_type=MESH)
            cp.wait_send()
            cp.wait_recv()

    hbm = pl.BlockSpec(memory_space=pltpu.HBM)
    sem = pl.BlockSpec(memory_space=pltpu.SEMAPHORE)
    return pl.pallas_call(
        body, name="scatter_tail_wait",
        out_shape=(pltpu.HBM(src_thru.shape, src_thru.dtype), pltpu.HBM(land_thru.shape, land_thru.dtype)),
        in_specs=(hbm, hbm, sem, sem) + (pl.BlockSpec(memory_space=pl.ANY),) * n_after, out_specs=(hbm, hbm),
        input_output_aliases={0: 0, 1: 1},
        compiler_params=pltpu.CompilerParams(has_side_effects=pltpu.SideEffectType.DATAFLOW_SIDE_EFFECTING),
    )(src_thru, land_thru, send_sems, recv_sems, *after)[1]


def _allreduce_small(v):
    rows = v.shape[0]

    def body(v_ref, out_ref, buf, send_sems, recv_sems):
        me, peer = _me_and_peers()
        buf[pl.ds(me, 1)] = v_ref[...][None]
        sends = []
        for j in range(1, N_DEV):
            pid, _ = peer(j)
            cp = pltpu.make_async_remote_copy(src_ref=v_ref, dst_ref=buf.at[me], send_sem=send_sems.at[j - 1],
                                              recv_sem=recv_sems.at[j - 1], device_id=pid, device_id_type=MESH)
            cp.start()
            sends.append(cp)
        for j in range(1, N_DEV):
            pid, pidx = peer(j)
            pltpu.make_async_remote_copy(src_ref=v_ref, dst_ref=buf.at[pidx], send_sem=send_sems.at[j - 1],
                                         recv_sem=recv_sems.at[j - 1], device_id=pid,
                                         device_id_type=MESH).wait_recv()
        for cp in sends:
            cp.wait_send()
        acc = buf[0]
        for s in range(1, N_DEV):
            acc = acc + buf[s]
        out_ref[...] = acc

    vm = pl.BlockSpec(memory_space=pltpu.VMEM)
    return pl.pallas_call(
        body, name="allreduce_small", out_shape=jax.ShapeDtypeStruct(v.shape, F32),
        in_specs=[vm], out_specs=vm,
        scratch_shapes=[pltpu.VMEM((N_DEV, rows, 128), F32), pltpu.SemaphoreType.DMA((7,)),
                        pltpu.SemaphoreType.DMA((7,))],
    )(v)


def _mm(name, grid, sem, k_axis, ops, op_specs, pairs, acc_shapes, extras, extra_specs, epilogue, outs, out_specs,
        comm=None):
    n_op, n_ex, n_out = len(ops), len(extras), len(outs)
    nk = grid[k_axis] if k_axis is not None else 1
    n_acc = len(acc_shapes) if nk > 1 else 0
    n_cm = comm.n if comm is not None else 0

    def body(*refs):
        op_refs = refs[:n_op]
        ex_refs = refs[n_op:n_op + n_ex]
        n_in = n_op + n_ex + n_cm
        out_refs = refs[n_in:n_in + n_out]
        acc_refs = refs[n_in + n_out + n_cm:n_in + n_out + n_cm + n_acc]
        if comm is not None:
            cm_refs = (refs[n_op + n_ex:n_in], refs[n_in + n_out:n_in + n_out + n_cm],
                       refs[n_in + n_out + n_cm + n_acc:])
            first, last = _grid_edges(grid)

            @pl.when(first)
            def _():
                comm.start(*cm_refs)

        def finish(vals):
            res = epilogue(*vals, *[e[...] for e in ex_refs])
            for o, r in zip(out_refs, res):
                o[...] = r.astype(o.dtype)

        if nk == 1:
            parts = [None] * len(acc_shapes)
            for li, ri, dims, ai in pairs:
                d = _dot(op_refs[li][...], op_refs[ri][...], dims)
                parts[ai] = d if parts[ai] is None else parts[ai] + d
            finish(parts)
        else:
            k = pl.program_id(k_axis)

            @pl.when(k == 0)
            def _():
                for a in acc_refs:
                    a[...] = jnp.zeros_like(a)

            for li, ri, dims, ai in pairs:
                acc_refs[ai][...] += _dot(op_refs[li][...], op_refs[ri][...], dims)

            @pl.when(k == nk - 1)
            def _():
                finish([a[...] for a in acc_refs])

        if comm is not None:
            @pl.when(last)
            def _():
                comm.finish(*cm_refs)

    scratch = [pltpu.VMEM(s, F32) for s in acc_shapes] if nk > 1 else []
    if comm is None:
        return pl.pallas_call(
            body, name=name, grid=grid, out_shape=outs,
            in_specs=list(op_specs) + list(extra_specs), out_specs=list(out_specs),
            scratch_shapes=scratch, compiler_params=_params(sem),
        )(*ops, *extras)
    res = pl.pallas_call(
        body, name=name, grid=grid, out_shape=list(outs) + comm.out_shapes,
        in_specs=list(op_specs) + list(extra_specs) + comm.specs, out_specs=list(out_specs) + comm.specs,
        scratch_shapes=scratch + comm.scratch, compiler_params=_params(("arbitrary",) * len(grid)),
    )(*ops, *extras, *comm.arrays)
    return res[:n_out], res[n_out:]


def _with_comm(res, comm, pick):
    if comm is None:
        return pick(res)
    return pick(res[0]), res[1]


def _mm_nn(name, a, w, out_dtype, tm_target=704, tn_target=1664, epilogue=None, extras=(), extra_specs=(), comm=None):
    L, K = a.shape
    N = w.shape[1]
    tm, tn = _tile(L, tm_target), _tile(N, tn_target, 128)
    ep = epilogue if epilogue is not None else (lambda acc: (acc,))
    res = _mm(name, (L // tm, N // tn), ("parallel", "parallel"), None,
              [a, w], [pl.BlockSpec((tm, K), lambda i, j: (i, 0)), pl.BlockSpec((K, tn), lambda i, j: (0, j))],
              [(0, 1, 'nn', 0)], [(tm, tn)], list(extras), list(extra_specs), ep,
              [jax.ShapeDtypeStruct((L, N), out_dtype)], [pl.BlockSpec((tm, tn), lambda i, j: (i, j))], comm=comm)
    return _with_comm(res, comm, lambda o: o[0])


def _mm_nt(name, pairs_aw, out_dtype, tm_target=704, tn_target=512, comm=None):
    L = pairs_aw[0][0].shape[0]
    N = pairs_aw[0][1].shape[0]
    tm, tn = _tile(L, tm_target), _tile(N, tn_target, 128)
    ops, specs, pairs = [], [], []
    for t, (a, w) in enumerate(pairs_aw):
        K = a.shape[1]
        ops += [a, w]
        specs += [pl.BlockSpec((tm, K), lambda i, j: (i, 0)), pl.BlockSpec((tn, K), lambda i, j: (j, 0))]
        pairs.append((2 * t, 2 * t + 1, 'nt', 0))
    res = _mm(name, (L // tm, N // tn), ("parallel", "parallel"), None, ops, specs, pairs, [(tm, tn)], [], [],
              lambda acc: (acc,), [jax.ShapeDtypeStruct((L, N), out_dtype)],
              [pl.BlockSpec((tm, tn), lambda i, j: (i, j))], comm=comm)
    return _with_comm(res, comm, lambda o: o[0])


def _mm_tn(name, a, bs, out_dtype=BF16, tk_target=1408, tn_target=1664, tm_target=2048, comm=None):
    L, M = a.shape
    N = bs[0].shape[1]
    tk, tn, tm = _tile(L, tk_target), _tile(N, tn_target, 128), _tile(M, tm_target, 128)
    nb = len(bs)
    ops = [a] + list(bs)
    specs = [pl.BlockSpec((tk, tm), lambda i, j, k: (k, i))] + [pl.BlockSpec((tk, tn), lambda i, j, k: (k, j))] * nb
    res = _mm(name, (M // tm, N // tn, L // tk), ("parallel", "parallel", "arbitrary"), 2, ops, specs,
              [(0, 1 + t, 'tn', t) for t in range(nb)], [(tm, tn)] * nb, [], [], lambda *acc: acc,
              [jax.ShapeDtypeStruct((M, N), out_dtype)] * nb,
              [pl.BlockSpec((tm, tn), lambda i, j, k: (i, j))] * nb, comm=comm)
    return _with_comm(res, comm, lambda o: o)


def _norm_fwd(x, w):
    L, D = x.shape
    tr = _tile(L, 512)

    def body(x_ref, w_ref, y_ref):
        v = x_ref[...]
        r = lax.rsqrt(jnp.mean(v * v, axis=-1, keepdims=True) + EPS)
        y_ref[...] = (v * r * w_ref[...]).astype(y_ref.dtype)

    return pl.pallas_call(
        body, name="norm_fwd", grid=(L // tr,), out_shape=jax.ShapeDtypeStruct((L, D), BF16),
        in_specs=[pl.BlockSpec((tr, D), lambda i: (i, 0)), pl.BlockSpec((1, D), lambda i: (0, 0))],
        out_specs=pl.BlockSpec((tr, D), lambda i: (i, 0)), compiler_params=_params(("parallel",)),
    )(x, w)


def _norm_bwd_math(x, w, dy):
    r = lax.rsqrt(jnp.mean(x * x, axis=-1, keepdims=True) + EPS)
    gy = dy * w
    dx = r * (gy - x * (r * r) * jnp.mean(gy * x, axis=-1, keepdims=True))
    dw = jnp.sum(dy * x * r, axis=0, keepdims=True)
    return dx, dw


def _norm_bwd(x, w, dy, res, scale, out_dtype):
    L, D = x.shape
    tr = _tile(L, 384)
    has_res = res is not None

    def body(*refs):
        x_ref, w_ref, dy_ref = refs[:3]
        res_ref = refs[3] if has_res else None
        dx_ref, dw_ref = refs[-2:]
        dx, dw = _norm_bwd_math(x_ref[...], w_ref[...], dy_ref[...].astype(F32))
        dx = scale * dx
        if has_res:
            dx = dx + res_ref[...]
        dx_ref[...] = dx.astype(dx_ref.dtype)

        @pl.when(pl.program_id(0) == 0)
        def _():
            dw_ref[...] = jnp.zeros_like(dw_ref)

        dw_ref[...] += scale * dw

    row = pl.BlockSpec((tr, D), lambda i: (i, 0))
    vec = pl.BlockSpec((1, D), lambda i: (0, 0))
    return pl.pallas_call(
        body, name="norm_bwd", grid=(L // tr,),
        out_shape=[jax.ShapeDtypeStruct((L, D), out_dtype), jax.ShapeDtypeStruct((1, D), F32)],
        in_specs=[row, vec, row] + ([row] if has_res else []), out_specs=[row, vec],
        compiler_params=_params(("arbitrary",)),
    )(*([x, w, dy] + ([res] if has_res else [])))


def _loss(h, target):
    L, D = h.shape

    def body(h_ref, t_ref, dh_ref, loss_ref):
        i = pl.program_id(0)

        @pl.when(i == 0)
        def _():
            dh_ref[...] = jnp.zeros_like(dh_ref)
            loss_ref[...] = jnp.zeros_like(loss_ref)

        @pl.when(i > 0)
        def _():
            diff = h_ref[...] - t_ref[...]
            dh_ref[...] = diff * (1.0 / D)
            loss_ref[...] += 0.5 * jnp.sum(diff * diff) * (1.0 / D)

    return pl.pallas_call(
        body, name="loss", grid=(L // BLK,),
        out_shape=[jax.ShapeDtypeStruct((L, D), F32), jax.ShapeDtypeStruct((8, 128), F32)],
        in_specs=[pl.BlockSpec((BLK, D), lambda i: (i, 0)),
                  pl.BlockSpec((BLK, D), lambda i: (jnp.maximum(i - 1, 0), 0))],
        out_specs=[pl.BlockSpec((BLK, D), lambda i: (i, 0)), pl.BlockSpec((8, 128), lambda i: (0, 0))],
        compiler_params=_params(("arbitrary",)),
    )(h, target)


def _ffn_up(a, wg, wu, comm=None):
    L, D = a.shape
    F = wg.shape[1]
    tm = _tile(L, 704)

    def ep(g, u):
        return g, u, g * _sigmoid(g) * u

    hspec = pl.BlockSpec((None, tm, F), lambda i, j: (j, i, 0))
    wspec = pl.BlockSpec((None, F, D), lambda i, j: (j, 0, 0))
    res = _mm("ffn_up", (L // tm, N_DEV), ("parallel", "parallel"), None,
              [a, wg, wu], [pl.BlockSpec((tm, D), lambda i, j: (i, 0)), wspec, wspec],
              [(0, 1, 'nt', 0), (0, 2, 'nt', 1)], [(tm, F)] * 2, [], [], ep,
              [jax.ShapeDtypeStruct((N_DEV, L, F), BF16)] * 3, [hspec] * 3, comm=comm)
    return _with_comm(res, comm, lambda o: o)


def _ffn_gate(a, wg, comm=None):
    L, D = a.shape
    F = wg.shape[1]
    tm = _tile(L, 704)
    res = _mm("ffn_gate", (L // tm, N_DEV), ("parallel", "parallel"), None,
              [a, wg], [pl.BlockSpec((tm, D), lambda i, j: (i, 0)), pl.BlockSpec((None, F, D), lambda i, j: (j, 0, 0))],
              [(0, 1, 'nt', 0)], [(tm, F)], [], [], lambda g: (g,),
              [jax.ShapeDtypeStruct((N_DEV, L, F), BF16)], [pl.BlockSpec((None, tm, F), lambda i, j: (j, i, 0))],
              comm=comm)
    return _with_comm(res, comm, lambda o: o[0])


def _ffn_up_gated(a, wu, g, comm=None):
    L, D = a.shape
    F = wu.shape[1]
    tm = _tile(L, 704)

    def ep(u, g_):
        g32 = g_.astype(F32)
        return u, g32 * _sigmoid(g32) * u

    hspec = pl.BlockSpec((None, tm, F), lambda i, j: (j, i, 0))
    res = _mm("ffn_up_gated", (L // tm, N_DEV), ("parallel", "parallel"), None,
              [a, wu], [pl.BlockSpec((tm, D), lambda i, j: (i, 0)), pl.BlockSpec((None, F, D), lambda i, j: (j, 0, 0))],
              [(0, 1, 'nt', 0)], [(tm, F)], [g], [hspec], ep,
              [jax.ShapeDtypeStruct((N_DEV, L, F), BF16)] * 2, [hspec, hspec], comm=comm)
    return _with_comm(res, comm, lambda o: o)


def _resnorm_epilogue(scale, with_next):
    def ep(acc, h, w, *w_next):
        r = lax.rsqrt(jnp.mean(acc * acc, axis=-1, keepdims=True) + EPS)
        h_out = h + scale * (acc * r * w)
        if not with_next:
            return acc, h_out
        r_next = lax.rsqrt(jnp.mean(h_out * h_out, axis=-1, keepdims=True) + EPS)
        return acc, h_out, h_out * r_next * w_next[0]
    return ep


def _ffn_down(hid, wd, h_in, post, next_norm=None, comm=None):
    _, L, F = hid.shape
    D = wd.shape[2]
    tm = _tile(L, 528)
    row = pl.BlockSpec((tm, D), lambda i, j: (i, 0))
    vec = pl.BlockSpec((1, D), lambda i, j: (0, 0))
    nxt = [] if next_norm is None else [next_norm]
    res = _mm("ffn_down", (L // tm, N_DEV), ("parallel", "arbitrary"), 1,
              [hid, wd], [pl.BlockSpec((None, tm, F), lambda i, j: (j, i, 0)),
                          pl.BlockSpec((None, F, D), lambda i, j: (j, 0, 0))],
              [(0, 1, 'nn', 0)], [(tm, D)], [h_in, post] + nxt, [row, vec] + [vec] * len(nxt),
              _resnorm_epilogue(0.5, bool(nxt)),
              [jax.ShapeDtypeStruct((L, D), F32)] * 2 + [jax.ShapeDtypeStruct((L, D), BF16)] * len(nxt),
              [row] * (2 + len(nxt)), comm=comm)
    return _with_comm(res, comm, lambda o: o)


def _ffn_dhid(df, wd, g, u, comm=None):
    L, D = df.shape
    F = wd.shape[1]
    tm = _tile(L, 704)

    def ep(dhid, g_, u_):
        g32, u32 = g_.astype(F32), u_.astype(F32)
        sg = _sigmoid(g32)
        return dhid * u32 * sg * (1.0 + g32 * (1.0 - sg)), dhid * g32 * sg

    hspec = pl.BlockSpec((None, tm, F), lambda i, j: (j, i, 0))
    res = _mm("ffn_dhid", (L // tm, N_DEV), ("parallel", "parallel"), None,
              [df, wd], [pl.BlockSpec((tm, D), lambda i, j: (i, 0)),
                         pl.BlockSpec((None, F, D), lambda i, j: (j, 0, 0))],
              [(0, 1, 'nt', 0)], [(tm, F)], [g, u], [hspec, hspec], ep,
              [jax.ShapeDtypeStruct((N_DEV, L, F), BF16)] * 2, [hspec, hspec], comm=comm)
    return _with_comm(res, comm, lambda o: o)


def _ffn_dwd(hid, df, comm=None):
    _, L, F = hid.shape
    D = df.shape[1]
    tk = _tile(L, 1408)
    res = _mm("ffn_dwd", (N_DEV, L // tk), ("parallel", "arbitrary"), 1,
              [hid, df], [pl.BlockSpec((None, tk, F), lambda j, k: (j, k, 0)),
                          pl.BlockSpec((tk, D), lambda j, k: (k, 0))],
              [(0, 1, 'tn', 0)], [(F, D)], [], [], lambda acc: (acc,),
              [jax.ShapeDtypeStruct((N_DEV, F, D), BF16)], [pl.BlockSpec((None, F, D), lambda j, k: (j, 0, 0))],
              comm=comm)
    return _with_comm(res, comm, lambda o: o[0])


def _ffn_dwgu(a, dg, du, comm=None):
    L, D = a.shape
    F = dg.shape[2]
    tk = _tile(L, 1408)
    hspec = pl.BlockSpec((None, tk, F), lambda j, k: (j, k, 0))
    wspec = pl.BlockSpec((None, F, D), lambda j, k: (j, 0, 0))
    res = _mm("ffn_dwgu", (N_DEV, L // tk), ("parallel", "arbitrary"), 1,
              [a, dg, du], [pl.BlockSpec((tk, D), lambda j, k: (k, 0)), hspec, hspec],
              [(1, 0, 'tn', 0), (2, 0, 'tn', 1)], [(F, D)] * 2, [], [], lambda *acc: acc,
              [jax.ShapeDtypeStruct((N_DEV, F, D), BF16)] * 2, [wspec, wspec], comm=comm)
    return _with_comm(res, comm, lambda o: o)


def _ffn_da(dg, du, wg, wu, comm=None):
    _, L, F = dg.shape
    D = wg.shape[2]
    tm = _tile(L, 704)
    hspec = pl.BlockSpec((None, tm, F), lambda i, j: (j, i, 0))
    wspec = pl.BlockSpec((None, F, D), lambda i, j: (j, 0, 0))
    row = pl.BlockSpec((tm, D), lambda i, j: (i, 0))
    res = _mm("ffn_da", (L // tm, N_DEV), ("parallel", "arbitrary"), 1,
              [dg, du, wg, wu], [hspec, hspec, wspec, wspec],
              [(0, 2, 'nn', 0), (1, 3, 'nn', 0)], [(tm, D)], [], [], lambda acc: (acc,),
              [jax.ShapeDtypeStruct((L, D), F32)], [row], comm=comm)
    return _with_comm(res, comm, lambda o: o[0])


def _rope_tables(L):
    rows = jnp.arange(L, dtype=F32)
    pos = jnp.where(rows < BLK, rows, rows - (BLK - N_META))
    inv_r = ROPE_THETA ** (-jnp.arange(0, HD, 2, dtype=F32) / HD)
    ang_r = pos[:, None] * inv_r[None, :]
    cr = jnp.concatenate([jnp.cos(ang_r), jnp.cos(ang_r)], axis=1)
    sr = jnp.concatenate([-jnp.sin(ang_r), jnp.sin(ang_r)], axis=1)
    inv_m = ROPE_THETA ** (-jnp.arange(0, ROPE, 2, dtype=F32) / ROPE)
    ang_m = pos[:, None] * inv_m[None, :]
    z32 = jnp.zeros((L, ROPE // 2), F32)
    z64 = jnp.zeros((L, HD - ROPE), F32)
    cm = jnp.concatenate([jnp.cos(ang_m), jnp.cos(ang_m), z64], axis=1)
    sa = jnp.concatenate([-jnp.sin(ang_m), z32, z64], axis=1)
    sb = jnp.concatenate([z32, jnp.sin(ang_m), z64], axis=1)
    return cr, sr, cm, sa, sb


def _rope_ret(x, cr, sr):
    return x * cr + pltpu.roll(x, HD // 2, 1) * sr


def _rope_ret_t(d, cr, sr):
    return d * cr + pltpu.roll(d * sr, HD // 2, 1)


def _rope_mla(x, cm, sa, sb):
    return x * cm + pltpu.roll(x, HD - ROPE // 2, 1) * sa + pltpu.roll(x, ROPE // 2, 1) * sb


def _rope_mla_t(d, cm, sa, sb):
    return d * cm + pltpu.roll(d * sa, ROPE // 2, 1) + pltpu.roll(d * sb, HD - ROPE // 2, 1)


C_RQ, C_RK, C_RV, C_RG = 0, HEADS * HD, 2 * HEADS * HD, 3 * HEADS * HD
C_CQ = 4 * HEADS * HD
C_CKV = C_CQ + Q_RANK
C_KR = C_CKV + KV_RANK
RET_K_SCALE = HD ** -0.5


def _prep(proj, tabs, qn, kvn):
    L = proj.shape[0]
    tr = _tile(L, 256)
    W = HEADS * HD

    def body(p_ref, cr_ref, sr_ref, cm_ref, sa_ref, sb_ref, qn_ref, kvn_ref, q_ref, k_ref, v_ref, cq_ref, ckv_ref,
             kr_ref):
        cr, sr = cr_ref[...], sr_ref[...]
        for h in range(HEADS):
            sl = slice(h * HD, (h + 1) * HD)
            q_ref[:, sl] = _rope_ret(p_ref[:, C_RQ + h * HD:C_RQ + (h + 1) * HD].astype(F32), cr, sr).astype(BF16)
            k_ref[:, sl] = (_rope_ret(p_ref[:, C_RK + h * HD:C_RK + (h + 1) * HD].astype(F32), cr, sr)
                            * RET_K_SCALE).astype(BF16)
        v_ref[...] = p_ref[:, C_RV:C_RV + W].astype(BF16)
        cq = p_ref[:, C_CQ:C_CQ + Q_RANK].astype(F32)
        cq_ref[...] = (cq * lax.rsqrt(jnp.mean(cq * cq, axis=-1, keepdims=True) + EPS) * qn_ref[...]).astype(BF16)
        ckv = p_ref[:, C_CKV:C_CKV + KV_RANK].astype(F32)
        ckv_ref[...] = (ckv * lax.rsqrt(jnp.mean(ckv * ckv, axis=-1, keepdims=True) + EPS)
                        * kvn_ref[...]).astype(BF16)
        kr_ref[...] = _rope_mla(p_ref[:, C_KR:C_KR + HD].astype(F32), cm_ref[...], sa_ref[...], sb_ref[...]).astype(BF16)

    row = lambda w: pl.BlockSpec((tr, w), lambda i: (i, 0))
    vec = lambda w: pl.BlockSpec((1, w), lambda i: (0, 0))
    return pl.pallas_call(
        body, name="mix_prep", grid=(L // tr,),
        out_shape=[jax.ShapeDtypeStruct((L, W), BF16)] * 3 + [jax.ShapeDtypeStruct((L, Q_RANK), BF16),
                                                              jax.ShapeDtypeStruct((L, KV_RANK), BF16),
                                                              jax.ShapeDtypeStruct((L, HD), BF16)],
        in_specs=[row(D_INP)] + [row(HD)] * 5 + [vec(Q_RANK), vec(KV_RANK)],
        out_specs=[row(W)] * 3 + [row(Q_RANK), row(KV_RANK), row(HD)],
        compiler_params=_params(("parallel",)),
    )(proj, *tabs, qn, kvn)


def _prep_bwd(proj, dq, dk, dv, drg, dcqn, dckvn, dkr8, tabs, qn, kvn):
    L = proj.shape[0]
    tr = _tile(L, 192)
    W = HEADS * HD

    def body(p_ref, dq_ref, dk_ref, dv_ref, drg_ref, dcq_ref, dckv_ref, dkr_ref, cr_ref, sr_ref, cm_ref, sa_ref,
             sb_ref, qn_ref, kvn_ref, dp_ref, dqn_ref, dkvn_ref):
        cr, sr = cr_ref[...], sr_ref[...]
        dkr = None
        for h in range(HEADS):
            sl = slice(h * HD, (h + 1) * HD)
            dp_ref[:, C_RQ + h * HD:C_RQ + (h + 1) * HD] = _rope_ret_t(dq_ref[:, sl].astype(F32), cr, sr).astype(BF16)
            dp_ref[:, C_RK + h * HD:C_RK + (h + 1) * HD] = (_rope_ret_t(dk_ref[:, sl].astype(F32), cr, sr)
                                                            * RET_K_SCALE).astype(BF16)
            part = dkr_ref[:, sl].astype(F32)
            dkr = part if dkr is None else dkr + part
        dp_ref[:, C_RV:C_RV + W] = dv_ref[...].astype(BF16)
        dp_ref[:, C_RG:C_RG + W] = drg_ref[...].astype(BF16)
        dcq, dqn = _norm_bwd_math(p_ref[:, C_CQ:C_CQ + Q_RANK].astype(F32), qn_ref[...], dcq_ref[...])
        dp_ref[:, C_CQ:C_CQ + Q_RANK] = dcq.astype(BF16)
        dckv, dkvn = _norm_bwd_math(p_ref[:, C_CKV:C_CKV + KV_RANK].astype(F32), kvn_ref[...], dckv_ref[...])
        dp_ref[:, C_CKV:C_CKV + KV_RANK] = dckv.astype(BF16)
        dp_ref[:, C_KR:C_KR + HD] = _rope_mla_t(dkr, cm_ref[...], sa_ref[...], sb_ref[...]).astype(BF16)

        @pl.when(pl.program_id(0) == 0)
        def _():
            dqn_ref[...] = jnp.zeros_like(dqn_ref)
            dkvn_ref[...] = jnp.zeros_like(dkvn_ref)

        dqn_ref[...] += dqn
        dkvn_ref[...] += dkvn

    row = lambda w: pl.BlockSpec((tr, w), lambda i: (i, 0))
    vec = lambda w: pl.BlockSpec((1, w), lambda i: (0, 0))
    return pl.pallas_call(
        body, name="mix_prep_bwd", grid=(L // tr,),
        out_shape=[jax.ShapeDtypeStruct((L, D_INP), BF16), jax.ShapeDtypeStruct((1, Q_RANK), F32),
                   jax.ShapeDtypeStruct((1, KV_RANK), F32)],
        in_specs=[row(D_INP)] + [row(W)] * 4 + [row(Q_RANK), row(KV_RANK), row(W)] + [row(HD)] * 5
                 + [vec(Q_RANK), vec(KV_RANK)],
        out_specs=[row(D_INP), vec(Q_RANK), vec(KV_RANK)],
        compiler_params=_params(("arbitrary",)),
    )(proj, dq, dk, dv, drg, dcqn, dckvn, dkr8, *tabs, qn, kvn)


def _post(o_ret, proj, gn):
    L, W = o_ret.shape
    tr = _tile(L, 384)

    def body(o_ref, rg_ref, gn_ref, out_ref):
        for h in range(HEADS):
            sl = slice(h * HD, (h + 1) * HD)
            o = o_ref[:, sl]
            rg = rg_ref[:, sl].astype(F32)
            n = o * lax.rsqrt(jnp.mean(o * o, axis=-1, keepdims=True) + EPS)
            out_ref[:, sl] = (n * gn_ref[:, sl] * (rg * _sigmoid(rg))).astype(BF16)

    row = pl.BlockSpec((tr, W), lambda i: (i, 0))
    return pl.pallas_call(
        body, name="ret_post", grid=(L // tr,), out_shape=jax.ShapeDtypeStruct((L, W), BF16),
        in_specs=[row, pl.BlockSpec((tr, W), lambda i: (i, C_RG // W)), pl.BlockSpec((1, W), lambda i: (0, 0))],
        out_specs=row, compiler_params=_params(("parallel",)),
    )(o_ret, proj, gn)


def _post_bwd(o_ret, proj, gn, dcat):
    L, W = o_ret.shape
    tr = _tile(L, 384)

    def body(o_ref, rg_ref, gn_ref, d_ref, do_ref, drg_ref, dgn_ref):
        @pl.when(pl.program_id(0) == 0)
        def _():
            dgn_ref[...] = jnp.zeros_like(dgn_ref)

        for h in range(HEADS):
            sl = slice(h * HD, (h + 1) * HD)
            o = o_ref[:, sl]
            rg = rg_ref[:, sl].astype(F32)
            d = d_ref[:, sl].astype(F32)
            gw = gn_ref[:, sl]
            r = lax.rsqrt(jnp.mean(o * o, axis=-1, keepdims=True) + EPS)
            n = o * r
            sg = _sigmoid(rg)
            si = rg * sg
            dn = d * gw * si
            dgn_ref[:, sl] += jnp.sum(d * n * si, axis=0, keepdims=True)
            drg_ref[:, sl] = (d * n * gw * sg * (1.0 + rg * (1.0 - sg))).astype(drg_ref.dtype)
            do_ref[:, sl] = (r * (dn - o * (r * r) * jnp.mean(dn * o, axis=-1, keepdims=True))).astype(BF16)

    row = pl.BlockSpec((tr, W), lambda i: (i, 0))
    vec = pl.BlockSpec((1, W), lambda i: (0, 0))
    return pl.pallas_call(
        body, name="ret_post_bwd", grid=(L // tr,),
        out_shape=[jax.ShapeDtypeStruct((L, W), BF16), jax.ShapeDtypeStruct((L, W), BF16),
                   jax.ShapeDtypeStruct((1, W), F32)],
        in_specs=[row, pl.BlockSpec((tr, W), lambda i: (i, C_RG // W)), vec, row],
        out_specs=[row, row, vec], compiler_params=_params(("arbitrary",)),
    )(o_ret, proj, gn, dcat)


RET_HEADS_PER_STEP = 4


def _lin_attn(name, q, k, v, lg, reverse, out_dtype=F32):
    L, W = q.shape
    nc = L // BLK - 1
    G = RET_HEADS_PER_STEP

    def body(q_ref, k_ref, v_ref, lg_ref, o_ref, s_ref):
        n = lax.broadcasted_iota(jnp.int32, (BLK, BLK), 0).astype(F32)
        m = lax.broadcasted_iota(jnp.int32, (BLK, BLK), 1).astype(F32)
        dist = (m - n) if reverse else (n - m)
        consts = []
        for g in range(G):
            lgv = lg_ref[g, 0:1, :]
            dmask = jnp.where(dist >= 0, jnp.exp(lgv * jnp.maximum(dist, 0.0)), 0.0)
            c = dict(dmask=dmask, dmask0=jnp.where((n < N_META) & (m < N_META), dmask, 0.0),
                     gl=jnp.exp(lgv * float(BLK)))
            if reverse:
                c.update(inter=jnp.exp(lgv * (float(BLK) - n)), upd=jnp.exp(lgv * n),
                         inter0=jnp.where(n < N_META, jnp.exp(lgv * jnp.maximum(float(N_META) - n, 0.0)), 0.0))
            else:
                c.update(inter=jnp.exp(lgv * (n + 1.0)), upd=jnp.exp(lgv * (float(BLK) - 1.0 - n)),
                         upd0=jnp.where(n < N_META, jnp.exp(lgv * jnp.maximum(float(N_META) - 1.0 - n, 0.0)), 0.0))
            consts.append(c)

        def chunk(c):
            rows = pl.ds(pl.multiple_of(c * BLK, BLK), BLK)
            state = [s_ref[g] for g in range(G)]
            outs, new_state = [], []
            for g in range(G):
                cols = slice(g * HD, (g + 1) * HD)
                cg = consts[g]
                qc, kc, vc = q_ref[rows, cols], k_ref[rows, cols], v_ref[rows, cols]
                a = _dot(qc, kc, 'nt') * cg['dmask']
                outs.append(_dot(a.astype(BF16), vc, 'nn') + _dot(qc, state[g].astype(BF16), 'nn') * cg['inter'])
                new_state.append(state[g] * cg['gl'] + _dot((kc.astype(F32) * cg['upd']).astype(BF16), vc, 'tn'))
            for g in range(G):
                o_ref[rows, g * HD:(g + 1) * HD] = outs[g].astype(o_ref.dtype)
                s_ref[g] = new_state[g]

        def first_chunk(with_state):
            for g in range(G):
                cols = slice(g * HD, (g + 1) * HD)
                cg = consts[g]
                q0, k0, v0 = q_ref[0:BLK, cols], k_ref[0:BLK, cols], v_ref[0:BLK, cols]
                o0 = _dot((_dot(q0, k0, 'nt') * cg['dmask0']).astype(BF16), v0, 'nn')
                if with_state:
                    o0 = o0 + _dot(q0, s_ref[g].astype(BF16), 'nn') * cg['inter0']
                else:
                    s_ref[g] = _dot((k0.astype(F32) * cg['upd0']).astype(BF16), v0, 'tn')
                o_ref[0:BLK, cols] = o0.astype(o_ref.dtype)

        if reverse:
            s_ref[...] = jnp.zeros_like(s_ref)

            def step(t, carry):
                chunk(nc - t)
                return carry

            lax.fori_loop(0, nc, step, 0)
            first_chunk(True)
        else:
            first_chunk(False)

            def step(t, carry):
                chunk(t + 1)
                return carry

            lax.fori_loop(0, nc, step, 0)

    col = pl.BlockSpec((L, G * HD), lambda h: (0, h))
    return pl.pallas_call(
        body, name=name, grid=(HEADS // G,), out_shape=jax.ShapeDtypeStruct((L, W), out_dtype),
        in_specs=[col, col, col, pl.BlockSpec((G, 8, HD), lambda h: (h, 0, 0))], out_specs=col,
        scratch_shapes=[pltpu.VMEM((G, HD, HD), F32)], compiler_params=_params(("parallel",)),
    )(q, k, v, lg)


ATT_SCALE = (HD + ROPE) ** -0.5
LOG2E = 1.4426950408889634
Q_PRESCALE = ATT_SCALE * LOG2E
NEG = -1e30


ATT_TILE = 384
ATT_HEADS_PER_STEP = 8
ATT_BWD_HEADS_PER_STEP = 2


def _att_valid(nq, nk, row0, col0):
    r = lax.broadcasted_iota(jnp.int32, (nq, nk), 0) + row0
    c = lax.broadcasted_iota(jnp.int32, (nq, nk), 1) + col0
    return (c <= r) & ((c < N_META) | (c >= BLK))


def _attn_fwd(qm, kn, krr, vm, comm=None):
    L = qm.shape[0]
    W = HEADS * HD
    T = _tile(L, ATT_TILE, BLK)
    nb = L // T
    G = ATT_HEADS_PER_STEP
    n_cm = comm.n if comm is not None else 0

    def body(*refs):
        q_ref, kn_ref, kr_ref, v_ref = refs[:4]
        o_ref, lse_ref = refs[4 + n_cm:6 + n_cm]
        m_sc, l_sc, acc_sc = refs[6 + 2 * n_cm:9 + 2 * n_cm]
        if comm is not None:
            cm_refs = (refs[4:4 + n_cm], refs[6 + n_cm:6 + 2 * n_cm], refs[9 + 2 * n_cm:])
            first, last = _grid_edges((HEADS // G, nb))

            @pl.when(first)
            def _():
                comm.start(*cm_refs)

        i = pl.program_id(1)
        m_sc[...] = jnp.full_like(m_sc, NEG)
        l_sc[...] = jnp.zeros_like(l_sc)
        acc_sc[...] = jnp.zeros_like(acc_sc)

        def tile(j, masked):
            rows = pl.ds(pl.multiple_of(j * T, T), T)
            kr = kr_ref[rows, :]
            valid = _att_valid(T, T, i * T, j * T) if masked else None
            ones = jnp.ones((T, HD), BF16)
            m_prev = [m_sc[g] for g in range(G)]
            l_prev = [l_sc[g] for g in range(G)]
            acc_prev = [acc_sc[g] for g in range(G)]
            m_new, l_new, acc_new = [], [], []
            for g in range(G):
                k = jnp.concatenate([kn_ref[rows, g * HD:(g + 1) * HD], kr], axis=1)
                s = _dot(q_ref[:, g * QH:(g + 1) * QH], k, 'nt')
                if masked:
                    s = jnp.where(valid, s, NEG)
                m_new.append(jnp.maximum(m_prev[g], jnp.max(s, axis=-1, keepdims=True)))
                p = jnp.exp2(s - m_new[g])
                alpha = jnp.exp2(m_prev[g] - m_new[g])
                pv = _dot(p.astype(BF16), jnp.concatenate([v_ref[rows, g * HD:(g + 1) * HD], ones], axis=1), 'nn')
                l_new.append(alpha * l_prev[g] + pv[:, HD:HD + 1])
                acc_new.append(alpha * acc_prev[g] + pv[:, 0:HD])
            for g in range(G):
                m_sc[g] = m_new[g]
                l_sc[g] = l_new[g]
                acc_sc[g] = acc_new[g]

        tile(0, True)

        def mid(j, carry):
            tile(j, False)
            return carry

        lax.fori_loop(1, i, mid, 0)

        @pl.when(i > 0)
        def _():
            tile(i, True)

        for g in range(G):
            l = l_sc[g]
            o_ref[:, g * HD:(g + 1) * HD] = (acc_sc[g] / l).astype(o_ref.dtype)
            lse_ref[g] = jnp.broadcast_to(m_sc[g] + jnp.log(l) * LOG2E, (T, HD))

        if comm is not None:
            @pl.when(last)
            def _():
                comm.finish(*cm_refs)

    cm_specs = comm.specs if comm is not None else []
    res = pl.pallas_call(
        body, name="attn_fwd", grid=(HEADS // G, nb),
        out_shape=[jax.ShapeDtypeStruct((L, W), BF16), jax.ShapeDtypeStruct((HEADS, L, HD), F32)]
        + (comm.out_shapes if comm is not None else []),
        in_specs=[pl.BlockSpec((T, G * QH), lambda h, i: (i, h)), pl.BlockSpec((L, G * HD), lambda h, i: (0, h)),
                  pl.BlockSpec((L, HD), lambda h, i: (0, 0)), pl.BlockSpec((L, G * HD), lambda h, i: (0, h))]
        + cm_specs,
        out_specs=[pl.BlockSpec((T, G * HD), lambda h, i: (i, h)),
                   pl.BlockSpec((G, T, HD), lambda h, i: (h, i, 0))] + cm_specs,
        scratch_shapes=[pltpu.VMEM((G, T, 1), F32), pltpu.VMEM((G, T, 1), F32), pltpu.VMEM((G, T, HD), F32)]
        + (comm.scratch if comm is not None else []),
        compiler_params=_params(("arbitrary", "arbitrary")),
    )(qm, kn, krr, vm, *(comm.arrays if comm is not None else []))
    return res[:2], res[2:]


def _attn_bwd(qm, kn, krr, vm, o, dcat, lse, comm=None):
    L = qm.shape[0]
    W = HEADS * HD
    T = _tile(L, ATT_TILE, BLK)
    nb = L // T
    G = ATT_BWD_HEADS_PER_STEP
    n_cm = comm.n if comm is not None else 0

    def body(*refs):
        q_ref, kn_ref, kr_ref, v_ref, o_ref, do_ref, lse_ref = refs[:7]
        dq_ref, dkn_ref, dkr_ref, dv_ref = refs[7 + n_cm:11 + n_cm]
        dl_sc, dk_sc, dv_sc = refs[11 + 2 * n_cm:14 + 2 * n_cm]
        if comm is not None:
            cm_refs = (refs[7:7 + n_cm], refs[11 + n_cm:11 + 2 * n_cm], refs[14 + 2 * n_cm:])
            first, last = _grid_edges((HEADS // G, nb))

            @pl.when(first)
            def _():
                comm.start(*cm_refs)

        j = pl.program_id(1)
        qs = lambda g: slice(g * QH, (g + 1) * QH)
        hs = lambda g: slice(g * HD, (g + 1) * HD)

        @pl.when(j == 0)
        def _():
            dq_ref[...] = jnp.zeros_like(dq_ref)

            def rowsum(t, carry):
                rows = pl.ds(pl.multiple_of(t * T, T), T)
                for g in range(G):
                    dl_sc[g, rows, :] = jnp.sum(do_ref[rows, hs(g)].astype(F32) * o_ref[rows, hs(g)].astype(F32),
                                                axis=-1, keepdims=True)
                return carry

            lax.fori_loop(0, nb, rowsum, 0)

        kr = kr_ref[...]
        ks = [jnp.concatenate([kn_ref[:, hs(g)], kr], axis=1) for g in range(G)]
        vs = [v_ref[:, hs(g)] for g in range(G)]
        dk_sc[...] = jnp.zeros_like(dk_sc)
        dv_sc[...] = jnp.zeros_like(dv_sc)

        def tile(i, masked):
            rows = pl.ds(pl.multiple_of(i * T, T), T)
            valid = _att_valid(T, T, i * T, j * T) if masked else None
            for g in range(G):
                q = q_ref[rows, qs(g)]
                do = do_ref[rows, hs(g)]
                s = _dot(q, ks[g], 'nt')
                if masked:
                    s = jnp.where(valid, s, NEG)
                p = jnp.exp2(s - lse_ref[g, rows, 0:1])
                dv_sc[g] += _dot(p.astype(BF16), do, 'tn')
                ds = (p * (_dot(do, vs[g], 'nt') - dl_sc[g, rows, :])).astype(BF16)
                dk_sc[g] += _dot(ds, q, 'tn')
                dq_ref[rows, qs(g)] += _dot(ds, ks[g], 'nn')

        tile(j, True)

        def rest(masked):
            def step(i, carry):
                tile(i, masked)
                return carry
            lax.fori_loop(j + 1, nb, step, 0)

        @pl.when(j == 0)
        def _():
            rest(True)

        @pl.when(j > 0)
        def _():
            rest(False)

        for g in range(G):
            dk = dk_sc[g] * (1.0 / LOG2E)
            dkn_ref[:, hs(g)] = dk[:, 0:HD].astype(BF16)
            dkr_ref[:, hs(g)] = dk[:, HD:QH].astype(dkr_ref.dtype)
            dv_ref[:, hs(g)] = dv_sc[g].astype(BF16)

        if comm is not None:
            @pl.when(last)
            def _():
                comm.finish(*cm_refs)

    blk = pl.BlockSpec((T, G * HD), lambda h, j: (j, h))
    once = pl.Buffered(1)
    cm_specs = comm.specs if comm is not None else []
    res = pl.pallas_call(
        body, name="attn_bwd", grid=(HEADS // G, nb),
        out_shape=[jax.ShapeDtypeStruct((L, HEADS * QH), F32), jax.ShapeDtypeStruct((L, W), BF16),
                   jax.ShapeDtypeStruct((L, W), BF16), jax.ShapeDtypeStruct((L, W), BF16)]
        + (comm.out_shapes if comm is not None else []),
        in_specs=[pl.BlockSpec((L, G * QH), lambda h, j: (0, h), pipeline_mode=once), blk,
                  pl.BlockSpec((T, HD), lambda h, j: (j, 0)), blk,
                  pl.BlockSpec((L, G * HD), lambda h, j: (0, h), pipeline_mode=once),
                  pl.BlockSpec((L, G * HD), lambda h, j: (0, HEADS // G + h), pipeline_mode=once),
                  pl.BlockSpec((G, L, HD), lambda h, j: (h, 0, 0), pipeline_mode=once)] + cm_specs,
        out_specs=[pl.BlockSpec((L, G * QH), lambda h, j: (0, h)), blk, blk, blk] + cm_specs,
        scratch_shapes=[pltpu.VMEM((G, L, 1), F32), pltpu.VMEM((G, T, QH), F32), pltpu.VMEM((G, T, HD), F32)]
        + (comm.scratch if comm is not None else []),
        compiler_params=_params(("arbitrary", "arbitrary")),
    )(qm, kn, krr, vm, o, dcat, lse, *(comm.arrays if comm is not None else []))
    return res[:4], res[4:]


def _unrope_q(dqm, tabs_m):
    L, W = dqm.shape
    tr = _tile(L, 384)

    def body(d_ref, cm_ref, sa_ref, sb_ref, out_ref):
        cm, sa, sb = cm_ref[...], sa_ref[...], sb_ref[...]
        for h in range(HEADS):
            out_ref[:, h * QH:h * QH + HD] = (d_ref[:, h * QH:h * QH + HD] * ATT_SCALE).astype(BF16)
            out_ref[:, h * QH + HD:(h + 1) * QH] = _rope_mla_t(d_ref[:, h * QH + HD:(h + 1) * QH] * ATT_SCALE, cm, sa,
                                                               sb).astype(BF16)

    row = pl.BlockSpec((tr, W), lambda i: (i, 0))
    tab = pl.BlockSpec((tr, HD), lambda i: (i, 0))
    return pl.pallas_call(
        body, name="unrope_q", grid=(L // tr,), out_shape=jax.ShapeDtypeStruct((L, W), BF16),
        in_specs=[row, tab, tab, tab], out_specs=row, compiler_params=_params(("parallel",)),
    )(dqm, *tabs_m)


def _q_up(cqn, wuq_p, tabs_m):
    L = cqn.shape[0]
    tm = _tile(L, 704)

    def ep(acc, cm, sa, sb):
        acc = acc * Q_PRESCALE
        parts = []
        for h in range(HEADS):
            parts.append(acc[:, h * QH:h * QH + HD])
            parts.append(_rope_mla(acc[:, h * QH + HD:(h + 1) * QH], cm, sa, sb))
        return (jnp.concatenate(parts, axis=1),)

    tab = pl.BlockSpec((tm, HD), lambda i, j: (i, 0))
    return _mm("mla_q_up", (L // tm, 1), ("parallel", "parallel"), None,
               [cqn, wuq_p], [pl.BlockSpec((tm, Q_RANK), lambda i, j: (i, 0)),
                              pl.BlockSpec((HEADS * QH, Q_RANK), lambda i, j: (0, 0))],
               [(0, 1, 'nt', 0)], [(tm, HEADS * QH)], list(tabs_m), [tab] * 3, ep,
               [jax.ShapeDtypeStruct((L, HEADS * QH), BF16)], [pl.BlockSpec((tm, HEADS * QH), lambda i, j: (i, 0))])[0]


def _mix_out(cat, w_out, h_in, post, next_norm):
    L, K = cat.shape
    D = w_out.shape[1]
    tm, tk = _tile(L, 384), K
    row = pl.BlockSpec((tm, D), lambda i, k: (i, 0))
    vec = pl.BlockSpec((1, D), lambda i, k: (0, 0))
    return _mm("mix_out", (L // tm, K // tk), ("parallel", "arbitrary"), 1,
               [cat, w_out], [pl.BlockSpec((tm, tk), lambda i, k: (i, k)), pl.BlockSpec((tk, D), lambda i, k: (k, 0))],
               [(0, 1, 'nn', 0)], [(tm, D)], [h_in, post, next_norm], [row, vec, vec], _resnorm_epilogue(1.0, True),
               [jax.ShapeDtypeStruct((L, D), F32)] * 2 + [jax.ShapeDtypeStruct((L, D), BF16)], [row, row, row])


ADAM_BLOCK_ELEMS = 512 * 704


def _adam_math(w, g, m, v):
    m = ADAM_B1 * m + (1.0 - ADAM_B1) * g
    v = ADAM_B2 * v + (1.0 - ADAM_B2) * (g * g)
    m_hat = m / (1.0 - ADAM_B1 ** ADAM_STEP)
    v_hat = v / (1.0 - ADAM_B2 ** ADAM_STEP)
    delta = -ADAM_LR * (m_hat / (jnp.sqrt(v_hat) + ADAM_EPS) + ADAM_WD * w)
    return delta, m, v


def _adam(name, w, m, v, g_slots=None, g=None, after=None):
    R, C = w.shape
    tr, tc = _tile(R, max(16, ADAM_BLOCK_ELEMS // C // 16 * 16), 16), C
    if tr * tc > ADAM_BLOCK_ELEMS:
        tr, tc = R, _tile(C, max(128, ADAM_BLOCK_ELEMS // R // 128 * 128), 128)
    from_slots = g_slots is not None

    def body(w_ref, m_ref, v_ref, g_ref, *rest):
        go_ref, d_ref, mo_ref, vo_ref = rest[-4:]
        if from_slots:
            grad = g_ref[0].astype(F32)
            for s in range(1, N_DEV):
                grad = grad + g_ref[s].astype(F32)
        else:
            grad = g_ref[...]
        delta, mn, vn = _adam_math(w_ref[...], grad, m_ref[...], v_ref[...])
        go_ref[...] = grad
        d_ref[...] = delta
        mo_ref[...] = mn
        vo_ref[...] = vn

    row = pl.BlockSpec((tr, tc), lambda i, j: (i, j))
    gspec = pl.BlockSpec((N_DEV, tr, tc), lambda i, j: (0, i, j)) if from_slots else row
    order = [] if after is None else [after]
    return pl.pallas_call(
        body, name=name, grid=(R // tr, C // tc), out_shape=[jax.ShapeDtypeStruct((R, C), F32)] * 4,
        in_specs=[row, row, row, gspec] + [pl.BlockSpec(memory_space=pl.ANY)] * len(order), out_specs=[row] * 4,
        compiler_params=_params(("parallel", "parallel")),
    )(w, m, v, g_slots if from_slots else g, *order)


def _unblock(gathered):
    n, r, c = gathered.shape
    return jnp.transpose(gathered, (1, 0, 2)).reshape(r, n * c)


def _reblock(full, c):
    r = full.shape[0]
    return jnp.transpose(full[:, :N_DEV * c].reshape(r, N_DEV, c), (1, 0, 2))


def _step(x, target, w, mom, vel):
    S, D = x.shape[1], x.shape[2]
    L = S + BLK
    def sq(a, n):
        if a.ndim == 2:
            return a
        if n in TRANSPOSED:
            a = jnp.swapaxes(a, 1, 2)
        return a.reshape(a.shape[1:])

    def unsq(o, n):
        o = o.reshape((1,) + o.shape)
        return jnp.swapaxes(o, 1, 2) if n in TRANSPOSED else o

    p = {n: sq(w[n], n) for n in WEIGHTS if n != 'meta_tokens'}
    gather = lambda names: _Exchange([p[n].astype(BF16) for n in names], False)
    scatter = lambda blocks: _Exchange(blocks, True)
    in_s, uq_s = p['w_in'].shape[0], p['mla_w_uq'].shape[0]
    assert uq_s == HD + ROPE and N_DEV == HEADS, "a w_uq shard is one head's columns"
    tabs = _rope_tables(L)
    tabs_m = tabs[2:]
    lg = jnp.broadcast_to(jnp.log(1.0 - 2.0 ** (-5.0 - jnp.arange(HEADS, dtype=F32)))[:, None, None], (HEADS, 8, HD))
    R = {}

    wg1, meta = _exchange("gather_first", [p['ffn1_w_gate'].astype(BF16), w['meta_tokens']], False)
    h0 = jnp.concatenate([_unblock(meta), jnp.zeros((BLK - N_META, D), F32), x[0]], axis=0)
    a1 = _norm_fwd(h0, p['ffn1_pre_norm'])
    g1, (wu1,) = _ffn_gate(a1, wg1, comm=gather(['ffn1_w_up']))
    (u1, hid1), (wd1,) = _ffn_up_gated(a1, wu1, g1, comm=gather(['ffn1_w_down']))
    (f1, h1, um), (w_in_g,) = _ffn_down(hid1, wd1, h0, p['ffn1_post_norm'], next_norm=p['mix_pre_norm'],
                                        comm=gather(['w_in']))

    w_in = jnp.pad(w_in_g.reshape(N_DEV * in_s, D), ((0, D_INP - N_DEV * in_s), (0, 0)))
    proj, (uq_g, uk_g, uv_g, wout_g) = _mm_nt("mix_in", [(um, w_in)], BF16, tn_target=1664,
                                              comm=gather(['mla_w_uq', 'mla_w_uk', 'mla_w_uv', 'w_out']))
    wuq = jnp.pad(uq_g, ((0, 0), (0, QH - uq_s), (0, 0))).reshape(HEADS * QH, Q_RANK)
    wuk, wuv, w_out = _unblock(uk_g), _unblock(uv_g), wout_g.reshape(-1, D)
    qr, kr, vr, cqn, ckvn, krr = _prep(proj, tabs, p['mla_q_norm'], p['mla_kv_norm'])
    qm = _q_up(cqn, wuq, tabs_m)
    kn = _mm_nn("mla_k_up", ckvn, wuk, BF16)
    vm = _mm_nn("mla_v_up", ckvn, wuv, BF16)
    (o_mla, lse), (wg2, wu2) = _attn_fwd(qm, kn, krr, vm, comm=gather(['ffn2_w_gate', 'ffn2_w_up']))
    o_ret = _lin_attn("ret_fwd", qr, kr, vr, lg, False)
    ret = _post(o_ret, proj, p['ret_group_norm'])
    cat = jnp.concatenate([ret, o_mla], axis=1)
    m, h2, a2 = _mix_out(cat, w_out, h1, p['mix_post_norm'], p['ffn2_pre_norm'])

    (g2, u2, hid2), (wd2,) = _ffn_up(a2, wg2, wu2, comm=gather(['ffn2_w_down']))
    f2, h3 = _ffn_down(hid2, wd2, h2, p['ffn2_post_norm'])
    dh3, loss_blk = _loss(h3, target[0])

    dsmall = {}
    df2, dsmall['ffn2_post_norm'] = _norm_bwd(f2, p['ffn2_post_norm'], dh3, None, 0.5, BF16)
    dg2, du2 = _ffn_dhid(df2, wd2, g2, u2)
    dwd2 = _ffn_dwd(hid2, df2)
    dwg2, dwu2 = _ffn_dwgu(a2, dg2, du2)
    da2, (R['ffn2_w_down'],) = _ffn_da(dg2, du2, wg2, wu2, comm=scatter([dwd2]))
    dh2, dsmall['ffn2_pre_norm'] = _norm_bwd(h2, p['ffn2_pre_norm'], da2, dh3, 1.0, F32)

    dm, dsmall['mix_post_norm'] = _norm_bwd(m, p['mix_post_norm'], dh2, None, 1.0, BF16)
    dcat = _mm_nt("mix_dcat", [(dm, w_out)], BF16)
    dwout = _mm_tn("mix_dwout", cat, [dm])[0]
    do_ret, drg, dsmall['ret_group_norm'] = _post_bwd(o_ret, proj, p['ret_group_norm'], dcat)
    dqr = _lin_attn("ret_dq", do_ret, vr, kr, lg, False, BF16)
    dkr = _lin_attn("ret_dk", vr, do_ret, qr, lg, True, BF16)
    dvr = _lin_attn("ret_dv", kr, qr, do_ret, lg, True, BF16)
    (dqm, dkn, dkr8, dvm), (R['ffn2_w_gate'], R['ffn2_w_up'], R['w_out']) = _attn_bwd(
        qm, kn, krr, vm, o_mla, dcat, lse, comm=scatter([dwg2, dwu2, dwout.reshape(N_DEV, -1, D)]))
    dqp = _unrope_q(dqm, tabs_m)
    dwuq = _mm_tn("mla_dwuq", dqp, [cqn])[0]
    dcqn = _mm_nn("mla_dcq", dqp, wuq, F32)
    dwuk, dwuv = _mm_tn("mla_dwukv", ckvn, [dkn, dvm])
    dckvn = _mm_nt("mla_dckv", [(dkn, wuk), (dvm, wuv)], F32)
    dproj, dsmall['mla_q_norm'], dsmall['mla_kv_norm'] = _prep_bwd(
        proj, dqr, dkr, dvr, drg, dcqn, dckvn, dkr8, tabs, p['mla_q_norm'], p['mla_kv_norm'])
    dwuq_b = dwuq.reshape(HEADS, QH, Q_RANK)[:, :uq_s]
    (dwin,), (R['mla_w_uq'], R['mla_w_uk'], R['mla_w_uv']) = _mm_tn(
        "mix_dwin", dproj, [um], comm=scatter([dwuq_b, _reblock(dwuk, p['mla_w_uk'].shape[1]),
                                               _reblock(dwuv, p['mla_w_uv'].shape[1])]))
    dwin_b = dwin[:N_DEV * in_s].reshape(N_DEV, in_s, D)
    half = D // 2
    dum, (r_win_a,) = _mm_nn("mix_du", dproj, w_in, F32, tn_target=512, comm=scatter([dwin_b[:, :, :half]]))
    dh1, dsmall['mix_pre_norm'] = _norm_bwd(h1, p['mix_pre_norm'], dum, dh2, 1.0, F32)

    df1, dsmall['ffn1_post_norm'] = _norm_bwd(f1, p['ffn1_post_norm'], dh1, None, 0.5, BF16)
    (dg1, du1), (r_win_b,) = _ffn_dhid(df1, wd1, g1, u1, comm=scatter([dwin_b[:, :, half:]]))
    R['w_in'] = jnp.concatenate([r_win_a, r_win_b], axis=2)
    dwd1 = _ffn_dwd(hid1, df1)
    (dwg1, dwu1), (R['ffn1_w_down'],) = _ffn_dwgu(a1, dg1, du1, comm=scatter([dwd1]))
    da1, (R['ffn1_w_gate'],) = _ffn_da(dg1, du1, wg1, wu1, comm=scatter([dwg1]))
    dh0, dsmall['ffn1_pre_norm'] = _norm_bwd(h0, p['ffn1_pre_norm'], da1, dh1, 1.0, F32)
    tail_sems_s, tail_sems_r, tail_src, tail_land, token = _scatter_start(dwu1)

    def slab(a):
        a = a.reshape(-1, 128)
        return jnp.pad(a, ((0, (-a.shape[0]) % 8), (0, 0)))

    slab_rows = lambda n: -(-(p[n].shape[-1] // 128) // 8) * 8
    packed = jnp.concatenate([slab(dsmall[n]) for n in SMALL] + [slab(dh0[:N_META]), loss_blk], axis=0)
    red = _allreduce_small(packed + token[0, 0])
    offs = sum(slab_rows(n) for n in SMALL)
    n_small = offs
    gmeta_full = red[offs:offs + N_META * D // 128].reshape(N_META, D)
    offs += N_META * D // 128
    loss = red[offs, 0]

    grad, delta, new_m, new_v = {}, {}, {}, {}
    meanwhile = []
    for n in BIG:
        if n == 'ffn1_w_up':
            continue
        outs = _adam("adam_" + n, p[n], sq(mom[n], n), sq(vel[n], n), g_slots=R[n], after=token)
        meanwhile.append(outs[0])
        grad[n], delta[n], new_m[n], new_v[n] = [unsq(o, n) for o in outs]
    pack = lambda d: jnp.concatenate([slab(d[n]) for n in SMALL], axis=0)
    outs = _adam("adam_small", pack(w), pack(mom), pack(vel), g=red[:n_small])
    meanwhile.append(outs[0])
    offs = 0
    for n in SMALL:
        r = p[n].shape[-1] // 128
        grad[n], delta[n], new_m[n], new_v[n] = [o[offs:offs + r].reshape(w[n].shape) for o in outs]
        offs += slab_rows(n)
    dev = 4 * lax.axis_index("x") + 2 * lax.axis_index("y") + lax.axis_index("c")
    mcols = w['meta_tokens'].shape[1]
    gmeta = lax.dynamic_slice(gmeta_full, (0, dev * mcols), (N_META, mcols))
    outs = _adam("adam_meta", w['meta_tokens'], mom['meta_tokens'], vel['meta_tokens'], g=gmeta)
    grad['meta_tokens'], delta['meta_tokens'], new_m['meta_tokens'], new_v['meta_tokens'] = outs
    meanwhile.append(outs[0])
    n = 'ffn1_w_up'
    slots = _scatter_wait(tail_sems_s, tail_sems_r, tail_src, tail_land, meanwhile)
    outs = _adam("adam_" + n, p[n], sq(mom[n], n), sq(vel[n], n), g_slots=slots)
    grad[n], delta[n], new_m[n], new_v[n] = [unsq(o, n) for o in outs]

    return (loss, dh0[BLK:][None], *[grad[n] for n in WEIGHTS], *[delta[n] for n in WEIGHTS],
            *[new_m[n] for n in WEIGHTS], *[new_v[n] for n in WEIGHTS])


def kernel(x, meta_tokens, ffn1_pre_norm, ffn1_w_gate, ffn1_w_up, ffn1_w_down, ffn1_post_norm, mix_pre_norm, w_in, ret_group_norm, mla_q_norm, mla_w_uq, mla_kv_norm, mla_w_uk, mla_w_uv, w_out, mix_post_norm, ffn2_pre_norm, ffn2_w_gate, ffn2_w_up, ffn2_w_down, ffn2_post_norm, loss_target, m_meta_tokens, m_ffn1_pre_norm, m_ffn1_w_gate, m_ffn1_w_up, m_ffn1_w_down, m_ffn1_post_norm, m_mix_pre_norm, m_w_in, m_ret_group_norm, m_mla_q_norm, m_mla_w_uq, m_mla_kv_norm, m_mla_w_uk, m_mla_w_uv, m_w_out, m_mix_post_norm, m_ffn2_pre_norm, m_ffn2_w_gate, m_ffn2_w_up, m_ffn2_w_down, m_ffn2_post_norm, v_meta_tokens, v_ffn1_pre_norm, v_ffn1_w_gate, v_ffn1_w_up, v_ffn1_w_down, v_ffn1_post_norm, v_mix_pre_norm, v_w_in, v_ret_group_norm, v_mla_q_norm, v_mla_w_uq, v_mla_kv_norm, v_mla_w_uk, v_mla_w_uv, v_w_out, v_mix_post_norm, v_ffn2_pre_norm, v_ffn2_w_gate, v_ffn2_w_up, v_ffn2_w_down, v_ffn2_post_norm):
    w = dict(zip(WEIGHTS, (meta_tokens, ffn1_pre_norm, ffn1_w_gate, ffn1_w_up, ffn1_w_down, ffn1_post_norm,
                           mix_pre_norm, w_in, ret_group_norm, mla_q_norm, mla_w_uq, mla_kv_norm, mla_w_uk, mla_w_uv,
                           w_out, mix_post_norm, ffn2_pre_norm, ffn2_w_gate, ffn2_w_up, ffn2_w_down, ffn2_post_norm)))
    mom = dict(zip(WEIGHTS, (m_meta_tokens, m_ffn1_pre_norm, m_ffn1_w_gate, m_ffn1_w_up, m_ffn1_w_down,
                             m_ffn1_post_norm, m_mix_pre_norm, m_w_in, m_ret_group_norm, m_mla_q_norm, m_mla_w_uq,
                             m_mla_kv_norm, m_mla_w_uk, m_mla_w_uv, m_w_out, m_mix_post_norm, m_ffn2_pre_norm,
                             m_ffn2_w_gate, m_ffn2_w_up, m_ffn2_w_down, m_ffn2_post_norm)))
    vel = dict(zip(WEIGHTS, (v_meta_tokens, v_ffn1_pre_norm, v_ffn1_w_gate, v_ffn1_w_up, v_ffn1_w_down,
                             v_ffn1_post_norm, v_mix_pre_norm, v_w_in, v_ret_group_norm, v_mla_q_norm, v_mla_w_uq,
                             v_mla_kv_norm, v_mla_w_uk, v_mla_w_uv, v_w_out, v_mix_post_norm, v_ffn2_pre_norm,
                             v_ffn2_w_gate, v_ffn2_w_up, v_ffn2_w_down, v_ffn2_post_norm)))
    return _step(x, loss_target, w, mom, vel)
```

```python
import functools
import math

import jax
import jax.numpy as jnp
from jax import lax
from jax.experimental import pallas as pl
from jax.experimental.pallas import tpu as pltpu

N_DEV = 8
N_META = 16
BLK = 128
HEADS = 8
HD = 128
ROPE = 64
Q_RANK = 512
KV_RANK = 256
QH = 2 * HD
D_INP = 4 * HEADS * HD + Q_RANK + KV_RANK + BLK
ROPE_THETA = 10000.0
EPS = 1e-6
ADAM_LR = 0.001
ADAM_B1 = 0.9
ADAM_B2 = 0.999
ADAM_EPS = 1e-08
ADAM_WD = 0.01
ADAM_STEP = 10
V7X_VMEM_LIMIT = 48 * 1024 * 1024
MESH = pl.DeviceIdType.MESH
F32 = jnp.float32
BF16 = jnp.bfloat16

WEIGHTS = ['meta_tokens', 'ffn1_pre_norm', 'ffn1_w_gate', 'ffn1_w_up', 'ffn1_w_down', 'ffn1_post_norm',
           'mix_pre_norm', 'w_in', 'ret_group_norm', 'mla_q_norm', 'mla_w_uq', 'mla_kv_norm', 'mla_w_uk',
           'mla_w_uv', 'w_out', 'mix_post_norm', 'ffn2_pre_norm', 'ffn2_w_gate', 'ffn2_w_up', 'ffn2_w_down',
           'ffn2_post_norm']
SMALL = ['ffn1_pre_norm', 'ffn1_post_norm', 'mix_pre_norm', 'ret_group_norm', 'mla_q_norm', 'mla_kv_norm',
         'mix_post_norm', 'ffn2_pre_norm', 'ffn2_post_norm']
TRANSPOSED = ('ffn1_w_gate', 'ffn1_w_up', 'ffn2_w_gate', 'ffn2_w_up', 'w_in', 'mla_w_uq')
BIG = ['ffn1_w_gate', 'ffn1_w_up', 'ffn1_w_down', 'w_in', 'mla_w_uq', 'mla_w_uk', 'mla_w_uv', 'w_out',
       'ffn2_w_gate', 'ffn2_w_up', 'ffn2_w_down']

_DIMS = {'nn': (((1,), (0,)), ((), ())), 'nt': (((1,), (1,)), ((), ())), 'tn': (((0,), (0,)), ((), ()))}


def _tile(n, target, mult=16):
    best = None
    for t in range(mult, min(n, target) + 1, mult):
        if n % t == 0:
            best = t
    return best if best is not None else n


def _params(sem):
    return pltpu.CompilerParams(dimension_semantics=sem, vmem_limit_bytes=V7X_VMEM_LIMIT)


def _dot(a, b, dims):
    return lax.dot_general(a, b, _DIMS[dims], preferred_element_type=F32)


def _sigmoid(x):
    return 0.5 * jnp.tanh(0.5 * x) + 0.5


def _me_and_peers():
    x, y, c = lax.axis_index("x"), lax.axis_index("y"), lax.axis_index("c")

    def peer(j):
        px = 1 - x if (j >> 2) & 1 else x
        py = 1 - y if (j >> 1) & 1 else y
        pc = 1 - c if j & 1 else c
        return (px, py, pc), 4 * px + 2 * py + pc

    return 4 * x + 2 * y + c, peer


class _Exchange:
    def __init__(self, arrays, per_peer):
        self.arrays = list(arrays)
        self.per_peer = per_peer
        self.n = len(self.arrays)
        self.out_shapes = [jax.ShapeDtypeStruct((N_DEV,) + tuple(a.shape[1:] if per_peer else a.shape), a.dtype)
                           for a in self.arrays]
        self.specs = [pl.BlockSpec(memory_space=pl.ANY)] * self.n
        self.scratch = [pltpu.SemaphoreType.DMA((7 * self.n,)), pltpu.SemaphoreType.DMA((7 * self.n,)),
                        pltpu.SemaphoreType.DMA((self.n,))]

    def _copies(self, src, dst, sems):
        send_sems, recv_sems, local_sems = sems
        me, peer = _me_and_peers()
        sib, _ = peer(1)
        local, sends, recvs, passes = [], {}, {}, {}
        for k in range(self.n):
            own = src[k].at[me] if self.per_peer else src[k]
            local.append(pltpu.make_async_copy(own, dst[k].at[me], local_sems.at[k]))
            for j in range(1, N_DEV):
                pid, pidx = peer(j)
                out = src[k].at[pidx] if self.per_peer else src[k]
                sem = dict(send_sem=send_sems.at[k * 7 + j - 1], recv_sem=recv_sems.at[k * 7 + j - 1])
                recvs[k, j] = pltpu.make_async_remote_copy(src_ref=out, dst_ref=dst[k].at[pidx], device_id=pid,
                                                           device_id_type=MESH, **sem)
                if self.per_peer or j in (1, 2, 4, 6):
                    sends[k, j] = pltpu.make_async_remote_copy(src_ref=out, dst_ref=dst[k].at[me], device_id=pid,
                                                               device_id_type=MESH, **sem)
                else:
                    _, origin = peer(j ^ 1)
                    passes[k, j ^ 1] = pltpu.make_async_remote_copy(
                        src_ref=dst[k].at[origin], dst_ref=dst[k].at[origin], device_id=sib, device_id_type=MESH, **sem)
        return local, sends, recvs, passes

    def start(self, src, dst, sems):
        local, sends, _, _ = self._copies(src, dst, sems)
        for cp in local + list(sends.values()):
            cp.start()

    def finish(self, src, dst, sems):
        local, sends, recvs, passes = self._copies(src, dst, sems)
        for key, cp in passes.items():
            recvs[key].wait_recv()
            cp.start()
        for key, cp in recvs.items():
            if key not in passes:
                cp.wait_recv()
        for cp in list(sends.values()) + list(passes.values()):
            cp.wait_send()
        for cp in local:
            cp.wait()


def _grid_edges(grid):
    first, last = None, None
    for a, n in enumerate(grid):
        f, l = pl.program_id(a) == 0, pl.program_id(a) == n - 1
        first = f if first is None else first & f
        last = l if last is None else last & l
    return first, last


def _exchange(name, arrays, per_peer):
    ex = _Exchange(arrays, per_peer)
    n = ex.n

    def body(*refs):
        ex.start(refs[:n], refs[n:2 * n], refs[2 * n:])
        ex.finish(refs[:n], refs[n:2 * n], refs[2 * n:])

    return pl.pallas_call(body, name=name, out_shape=ex.out_shapes, in_specs=ex.specs, out_specs=ex.specs,
                          scratch_shapes=ex.scratch)(*arrays)


def _scatter_start(blocks):
    def body(src_ref, land_ref, send_sems, recv_sems, src_thru, land_thru, token, local_sem):
        me, peer = _me_and_peers()
        local = pltpu.make_async_copy(src_ref.at[me], land_ref.at[me], local_sem)
        local.start()
        for j in range(1, N_DEV):
            pid, pidx = peer(j)
            pltpu.make_async_remote_copy(src_ref=src_ref.at[pidx], dst_ref=land_ref.at[me],
                                         send_sem=send_sems.at[j - 1], recv_sem=recv_sems.at[j - 1],
                                         device_id=pid, device_id_type=MESH).start()
        local.wait()
        token[...] = jnp.zeros_like(token)

    hbm = pl.BlockSpec(memory_space=pltpu.HBM)
    sem = pl.BlockSpec(memory_space=pltpu.SEMAPHORE)
    return pl.pallas_call(
        body, name="scatter_tail_start",
        out_shape=(pltpu.SemaphoreType.DMA((7,)), pltpu.SemaphoreType.DMA((7,)), pltpu.HBM(blocks.shape, blocks.dtype),
                   pltpu.HBM(blocks.shape, blocks.dtype), jax.ShapeDtypeStruct((8, 128), F32)),
        in_specs=(hbm, hbm), out_specs=(sem, sem, hbm, hbm, pl.BlockSpec(memory_space=pltpu.VMEM)),
        input_output_aliases={0: 2, 1: 3}, scratch_shapes=[pltpu.SemaphoreType.DMA],
        compiler_params=pltpu.CompilerParams(has_side_effects=pltpu.SideEffectType.DATAFLOW_SIDE_EFFECTING),
    )(pltpu.with_memory_space_constraint(blocks, pltpu.HBM),
      pltpu.with_memory_space_constraint(lax.empty(blocks.shape, blocks.dtype), pltpu.HBM))


def _scatter_wait(send_sems, recv_sems, src_thru, land_thru, after):
    n_after = len(after)

    def body(src_ref, land_ref, send_sems, recv_sems, *rest):
        me, peer = _me_and_peers()
        for j in range(1, N_DEV):
            pid, pidx = peer(j)
            cp = pltpu.make_async_remote_copy(src_ref=src_ref.at[pidx], dst_ref=land_ref.at[pidx],
                                              send_sem=send_sems.at[j - 1], recv_sem=recv_sems.at[j - 1],
                                              device_id=pid, device_id_type=MESH)
            cp.wait_send()
            cp.wait_recv()

    hbm = pl.BlockSpec(memory_space=pltpu.HBM)
    sem = pl.BlockSpec(memory_space=pltpu.SEMAPHORE)
    return pl.pallas_call(
        body, name="scatter_tail_wait",
        out_shape=(pltpu.HBM(src_thru.shape, src_thru.dtype), pltpu.HBM(land_thru.shape, land_thru.dtype)),
        in_specs=(hbm, hbm, sem, sem) + (pl.BlockSpec(memory_space=pl.ANY),) * n_after, out_specs=(hbm, hbm),
        input_output_aliases={0: 0, 1: 1},
        compiler_params=pltpu.CompilerParams(has_side_effects=pltpu.SideEffectType.DATAFLOW_SIDE_EFFECTING),
    )(src_thru, land_thru, send_sems, recv_sems, *after)[1]


def _allreduce_small(v):
    rows = v.shape[0]

    def body(v_ref, out_ref, buf, send_sems, recv_sems):
        me, peer = _me_and_peers()
        buf[pl.ds(me, 1)] = v_ref[...][None]
        sends = []
        for j in range(1, N_DEV):
            pid, _ = peer(j)
            cp = pltpu.make_async_remote_copy(src_ref=v_ref, dst_ref=buf.at[me], send_sem=send_sems.at[j - 1],
                                              recv_sem=recv_sems.at[j - 1], device_id=pid, device_id_type=MESH)
            cp.start()
            sends.append(cp)
        for j in range(1, N_DEV):
            pid, pidx = peer(j)
            pltpu.make_async_remote_copy(src_ref=v_ref, dst_ref=buf.at[pidx], send_sem=send_sems.at[j - 1],
                                         recv_sem=recv_sems.at[j - 1], device_id=pid,
                                         device_id_type=MESH).wait_recv()
        for cp in sends:
            cp.wait_send()
        acc = buf[0]
        for s in range(1, N_DEV):
            acc = acc + buf[s]
        out_ref[...] = acc

    vm = pl.BlockSpec(memory_space=pltpu.VMEM)
    return pl.pallas_call(
        body, name="allreduce_small", out_shape=jax.ShapeDtypeStruct(v.shape, F32),
        in_specs=[vm], out_specs=vm,
        scratch_shapes=[pltpu.VMEM((N_DEV, rows, 128), F32), pltpu.SemaphoreType.DMA((7,)),
                        pltpu.SemaphoreType.DMA((7,))],
    )(v)


def _mm(name, grid, sem, k_axis, ops, op_specs, pairs, acc_shapes, extras, extra_specs, epilogue, outs, out_specs,
        comm=None):
    n_op, n_ex, n_out = len(ops), len(extras), len(outs)
    nk = grid[k_axis] if k_axis is not None else 1
    n_acc = len(acc_shapes) if nk > 1 else 0
    n_cm = comm.n if comm is not None else 0

    def body(*refs):
        op_refs = refs[:n_op]
        ex_refs = refs[n_op:n_op + n_ex]
        n_in = n_op + n_ex + n_cm
        out_refs = refs[n_in:n_in + n_out]
        acc_refs = refs[n_in + n_out + n_cm:n_in + n_out + n_cm + n_acc]
        if comm is not None:
            cm_refs = (refs[n_op + n_ex:n_in], refs[n_in + n_out:n_in + n_out + n_cm],
                       refs[n_in + n_out + n_cm + n_acc:])
            first, last = _grid_edges(grid)

            @pl.when(first)
            def _():
                comm.start(*cm_refs)

        def finish(vals):
            res = epilogue(*vals, *[e[...] for e in ex_refs])
            for o, r in zip(out_refs, res):
                o[...] = r.astype(o.dtype)

        if nk == 1:
            parts = [None] * len(acc_shapes)
            for li, ri, dims, ai in pairs:
                d = _dot(op_refs[li][...], op_refs[ri][...], dims)
                parts[ai] = d if parts[ai] is None else parts[ai] + d
            finish(parts)
        else:
            k = pl.program_id(k_axis)

            @pl.when(k == 0)
            def _():
                for a in acc_refs:
                    a[...] = jnp.zeros_like(a)

            for li, ri, dims, ai in pairs:
                acc_refs[ai][...] += _dot(op_refs[li][...], op_refs[ri][...], dims)

            @pl.when(k == nk - 1)
            def _():
                finish([a[...] for a in acc_refs])

        if comm is not None:
            @pl.when(last)
            def _():
                comm.finish(*cm_refs)

    scratch = [pltpu.VMEM(s, F32) for s in acc_shapes] if nk > 1 else []
    if comm is None:
        return pl.pallas_call(
            body, name=name, grid=grid, out_shape=outs,
            in_specs=list(op_specs) + list(extra_specs), out_specs=list(out_specs),
            scratch_shapes=scratch, compiler_params=_params(sem),
        )(*ops, *extras)
    res = pl.pallas_call(
        body, name=name, grid=grid, out_shape=list(outs) + comm.out_shapes,
        in_specs=list(op_specs) + list(extra_specs) + comm.specs, out_specs=list(out_specs) + comm.specs,
        scratch_shapes=scratch + comm.scratch, compiler_params=_params(("arbitrary",) * len(grid)),
    )(*ops, *extras, *comm.arrays)
    return res[:n_out], res[n_out:]


def _with_comm(res, comm, pick):
    if comm is None:
        return pick(res)
    return pick(res[0]), res[1]


def _mm_nn(name, a, w, out_dtype, tm_target=704, tn_target=1664, epilogue=None, extras=(), extra_specs=(), comm=None):
    L, K = a.shape
    N = w.shape[1]
    tm, tn = _tile(L, tm_target), _tile(N, tn_target, 128)
    ep = epilogue if epilogue is not None else (lambda acc: (acc,))
    res = _mm(name, (L // tm, N // tn), ("parallel", "parallel"), None,
              [a, w], [pl.BlockSpec((tm, K), lambda i, j: (i, 0)), pl.BlockSpec((K, tn), lambda i, j: (0, j))],
              [(0, 1, 'nn', 0)], [(tm, tn)], list(extras), list(extra_specs), ep,
              [jax.ShapeDtypeStruct((L, N), out_dtype)], [pl.BlockSpec((tm, tn), lambda i, j: (i, j))], comm=comm)
    return _with_comm(res, comm, lambda o: o[0])


def _mm_nt(name, pairs_aw, out_dtype, tm_target=704, tn_target=512, comm=None):
    L = pairs_aw[0][0].shape[0]
    N = pairs_aw[0][1].shape[0]
    tm, tn = _tile(L, tm_target), _tile(N, tn_target, 128)
    ops, specs, pairs = [], [], []
    for t, (a, w) in enumerate(pairs_aw):
        K = a.shape[1]
        ops += [a, w]
        specs += [pl.BlockSpec((tm, K), lambda i, j: (i, 0)), pl.BlockSpec((tn, K), lambda i, j: (j, 0))]
        pairs.append((2 * t, 2 * t + 1, 'nt', 0))
    res = _mm(name, (L // tm, N // tn), ("parallel", "parallel"), None, ops, specs, pairs, [(tm, tn)], [], [],
              lambda acc: (acc,), [jax.ShapeDtypeStruct((L, N), out_dtype)],
              [pl.BlockSpec((tm, tn), lambda i, j: (i, j))], comm=comm)
    return _with_comm(res, comm, lambda o: o[0])


def _mm_tn(name, a, bs, out_dtype=BF16, tk_target=1408, tn_target=1664, tm_target=2048, comm=None):
    L, M = a.shape
    N = bs[0].shape[1]
    tk, tn, tm = _tile(L, tk_target), _tile(N, tn_target, 128), _tile(M, tm_target, 128)
    nb = len(bs)
    ops = [a] + list(bs)
    specs = [pl.BlockSpec((tk, tm), lambda i, j, k: (k, i))] + [pl.BlockSpec((tk, tn), lambda i, j, k: (k, j))] * nb
    res = _mm(name, (M // tm, N // tn, L // tk), ("parallel", "parallel", "arbitrary"), 2, ops, specs,
              [(0, 1 + t, 'tn', t) for t in range(nb)], [(tm, tn)] * nb, [], [], lambda *acc: acc,
              [jax.ShapeDtypeStruct((M, N), out_dtype)] * nb,
              [pl.BlockSpec((tm, tn), lambda i, j, k: (i, j))] * nb, comm=comm)
    return _with_comm(res, comm, lambda o: o)


def _norm_fwd(x, w):
    L, D = x.shape
    tr = _tile(L, 512)

    def body(x_ref, w_ref, y_ref):
        v = x_ref[...]
        r = lax.rsqrt(jnp.mean(v * v, axis=-1, keepdims=True) + EPS)
        y_ref[...] = (v * r * w_ref[...]).astype(y_ref.dtype)

    return pl.pallas_call(
        body, name="norm_fwd", grid=(L // tr,), out_shape=jax.ShapeDtypeStruct((L, D), BF16),
        in_specs=[pl.BlockSpec((tr, D), lambda i: (i, 0)), pl.BlockSpec((1, D), lambda i: (0, 0))],
        out_specs=pl.BlockSpec((tr, D), lambda i: (i, 0)), compiler_params=_params(("parallel",)),
    )(x, w)


def _norm_bwd_math(x, w, dy):
    r = lax.rsqrt(jnp.mean(x * x, axis=-1, keepdims=True) + EPS)
    gy = dy * w
    dx = r * (gy - x * (r * r) * jnp.mean(gy * x, axis=-1, keepdims=True))
    dw = jnp.sum(dy * x * r, axis=0, keepdims=True)
    return dx, dw


def _norm_bwd(x, w, dy, res, scale, out_dtype):
    L, D = x.shape
    tr = _tile(L, 384)
    has_res = res is not None

    def body(*refs):
        x_ref, w_ref, dy_ref = refs[:3]
        res_ref = refs[3] if has_res else None
        dx_ref, dw_ref = refs[-2:]
        dx, dw = _norm_bwd_math(x_ref[...], w_ref[...], dy_ref[...].astype(F32))
        dx = scale * dx
        if has_res:
            dx = dx + res_ref[...]
        dx_ref[...] = dx.astype(dx_ref.dtype)

        @pl.when(pl.program_id(0) == 0)
        def _():
            dw_ref[...] = jnp.zeros_like(dw_ref)

        dw_ref[...] += scale * dw

    row = pl.BlockSpec((tr, D), lambda i: (i, 0))
    vec = pl.BlockSpec((1, D), lambda i: (0, 0))
    return pl.pallas_call(
        body, name="norm_bwd", grid=(L // tr,),
        out_shape=[jax.ShapeDtypeStruct((L, D), out_dtype), jax.ShapeDtypeStruct((1, D), F32)],
        in_specs=[row, vec, row] + ([row] if has_res else []), out_specs=[row, vec],
        compiler_params=_params(("arbitrary",)),
    )(*([x, w, dy] + ([res] if has_res else [])))


def _loss(h, target):
    L, D = h.shape

    def body(h_ref, t_ref, dh_ref, loss_ref):
        i = pl.program_id(0)

        @pl.when(i == 0)
        def _():
            dh_ref[...] = jnp.zeros_like(dh_ref)
            loss_ref[...] = jnp.zeros_like(loss_ref)

        @pl.when(i > 0)
        def _():
            diff = h_ref[...] - t_ref[...]
            dh_ref[...] = diff * (1.0 / D)
            loss_ref[...] += 0.5 * jnp.sum(diff * diff) * (1.0 / D)

    return pl.pallas_call(
        body, name="loss", grid=(L // BLK,),
        out_shape=[jax.ShapeDtypeStruct((L, D), F32), jax.ShapeDtypeStruct((8, 128), F32)],
        in_specs=[pl.BlockSpec((BLK, D), lambda i: (i, 0)),
                  pl.BlockSpec((BLK, D), lambda i: (jnp.maximum(i - 1, 0), 0))],
        out_specs=[pl.BlockSpec((BLK, D), lambda i: (i, 0)), pl.BlockSpec((8, 128), lambda i: (0, 0))],
        compiler_params=_params(("arbitrary",)),
    )(h, target)


def _ffn_up(a, wg, wu, comm=None):
    L, D = a.shape
    F = wg.shape[1]
    tm = _tile(L, 704)

    def ep(g, u):
        return g, u, g * _sigmoid(g) * u

    hspec = pl.BlockSpec((None, tm, F), lambda i, j: (j, i, 0))
    wspec = pl.BlockSpec((None, F, D), lambda i, j: (j, 0, 0))
    res = _mm("ffn_up", (L // tm, N_DEV), ("parallel", "parallel"), None,
              [a, wg, wu], [pl.BlockSpec((tm, D), lambda i, j: (i, 0)), wspec, wspec],
              [(0, 1, 'nt', 0), (0, 2, 'nt', 1)], [(tm, F)] * 2, [], [], ep,
              [jax.ShapeDtypeStruct((N_DEV, L, F), BF16)] * 3, [hspec] * 3, comm=comm)
    return _with_comm(res, comm, lambda o: o)


def _ffn_gate(a, wg, comm=None):
    L, D = a.shape
    F = wg.shape[1]
    tm = _tile(L, 704)
    res = _mm("ffn_gate", (L // tm, N_DEV), ("parallel", "parallel"), None,
              [a, wg], [pl.BlockSpec((tm, D), lambda i, j: (i, 0)), pl.BlockSpec((None, F, D), lambda i, j: (j, 0, 0))],
              [(0, 1, 'nt', 0)], [(tm, F)], [], [], lambda g: (g,),
              [jax.ShapeDtypeStruct((N_DEV, L, F), BF16)], [pl.BlockSpec((None, tm, F), lambda i, j: (j, i, 0))],
              comm=comm)
    return _with_comm(res, comm, lambda o: o[0])


def _ffn_up_gated(a, wu, g, comm=None):
    L, D = a.shape
    F = wu.shape[1]
    tm = _tile(L, 704)

    def ep(u, g_):
        g32 = g_.astype(F32)
        return u, g32 * _sigmoid(g32) * u

    hspec = pl.BlockSpec((None, tm, F), lambda i, j: (j, i, 0))
    res = _mm("ffn_up_gated", (L // tm, N_DEV), ("parallel", "parallel"), None,
              [a, wu], [pl.BlockSpec((tm, D), lambda i, j: (i, 0)), pl.BlockSpec((None, F, D), lambda i, j: (j, 0, 0))],
              [(0, 1, 'nt', 0)], [(tm, F)], [g], [hspec], ep,
              [jax.ShapeDtypeStruct((N_DEV, L, F), BF16)] * 2, [hspec, hspec], comm=comm)
    return _with_comm(res, comm, lambda o: o)


def _resnorm_epilogue(scale, with_next):
    def ep(acc, h, w, *w_next):
        r = lax.rsqrt(jnp.mean(acc * acc, axis=-1, keepdims=True) + EPS)
        h_out = h + scale * (acc * r * w)
        if not with_next:
            return acc, h_out
        r_next = lax.rsqrt(jnp.mean(h_out * h_out, axis=-1, keepdims=True) + EPS)
        return acc, h_out, h_out * r_next * w_next[0]
    return ep


def _ffn_down(hid, wd, h_in, post, next_norm=None, comm=None):
    _, L, F = hid.shape
    D = wd.shape[2]
    tm = _tile(L, 528)
    row = pl.BlockSpec((tm, D), lambda i, j: (i, 0))
    vec = pl.BlockSpec((1, D), lambda i, j: (0, 0))
    nxt = [] if next_norm is None else [next_norm]
    res = _mm("ffn_down", (L // tm, N_DEV), ("parallel", "arbitrary"), 1,
              [hid, wd], [pl.BlockSpec((None, tm, F), lambda i, j: (j, i, 0)),
                          pl.BlockSpec((None, F, D), lambda i, j: (j, 0, 0))],
              [(0, 1, 'nn', 0)], [(tm, D)], [h_in, post] + nxt, [row, vec] + [vec] * len(nxt),
              _resnorm_epilogue(0.5, bool(nxt)),
              [jax.ShapeDtypeStruct((L, D), F32)] * 2 + [jax.ShapeDtypeStruct((L, D), BF16)] * len(nxt),
              [row] * (2 + len(nxt)), comm=comm)
    return _with_comm(res, comm, lambda o: o)


def _ffn_dhid(df, wd, g, u, comm=None):
    L, D = df.shape
    F = wd.shape[1]
    tm = _tile(L, 704)

    def ep(dhid, g_, u_):
        g32, u32 = g_.astype(F32), u_.astype(F32)
        sg = _sigmoid(g32)
        return dhid * u32 * sg * (1.0 + g32 * (1.0 - sg)), dhid * g32 * sg

    hspec = pl.BlockSpec((None, tm, F), lambda i, j: (j, i, 0))
    res = _mm("ffn_dhid", (L // tm, N_DEV), ("parallel", "parallel"), None,
              [df, wd], [pl.BlockSpec((tm, D), lambda i, j: (i, 0)),
                         pl.BlockSpec((None, F, D), lambda i, j: (j, 0, 0))],
              [(0, 1, 'nt', 0)], [(tm, F)], [g, u], [hspec, hspec], ep,
              [jax.ShapeDtypeStruct((N_DEV, L, F), BF16)] * 2, [hspec, hspec], comm=comm)
    return _with_comm(res, comm, lambda o: o)


def _ffn_dwd(hid, df, comm=None):
    _, L, F = hid.shape
    D = df.shape[1]
    tk = _tile(L, 1408)
    res = _mm("ffn_dwd", (N_DEV, L // tk), ("parallel", "arbitrary"), 1,
              [hid, df], [pl.BlockSpec((None, tk, F), lambda j, k: (j, k, 0)),
                          pl.BlockSpec((tk, D), lambda j, k: (k, 0))],
              [(0, 1, 'tn', 0)], [(F, D)], [], [], lambda acc: (acc,),
              [jax.ShapeDtypeStruct((N_DEV, F, D), BF16)], [pl.BlockSpec((None, F, D), lambda j, k: (j, 0, 0))],
              comm=comm)
    return _with_comm(res, comm, lambda o: o[0])


def _ffn_dwgu(a, dg, du, comm=None):
    L, D = a.shape
    F = dg.shape[2]
    tk = _tile(L, 1408)
    hspec = pl.BlockSpec((None, tk, F), lambda j, k: (j, k, 0))
    wspec = pl.BlockSpec((None, F, D), lambda j, k: (j, 0, 0))
    res = _mm("ffn_dwgu", (N_DEV, L // tk), ("parallel", "arbitrary"), 1,
              [a, dg, du], [pl.BlockSpec((tk, D), lambda j, k: (k, 0)), hspec, hspec],
              [(1, 0, 'tn', 0), (2, 0, 'tn', 1)], [(F, D)] * 2, [], [], lambda *acc: acc,
              [jax.ShapeDtypeStruct((N_DEV, F, D), BF16)] * 2, [wspec, wspec], comm=comm)
    return _with_comm(res, comm, lambda o: o)


def _ffn_da(dg, du, wg, wu, comm=None):
    _, L, F = dg.shape
    D = wg.shape[2]
    tm = _tile(L, 704)
    hspec = pl.BlockSpec((None, tm, F), lambda i, j: (j, i, 0))
    wspec = pl.BlockSpec((None, F, D), lambda i, j: (j, 0, 0))
    row = pl.BlockSpec((tm, D), lambda i, j: (i, 0))
    res = _mm("ffn_da", (L // tm, N_DEV), ("parallel", "arbitrary"), 1,
              [dg, du, wg, wu], [hspec, hspec, wspec, wspec],
              [(0, 2, 'nn', 0), (1, 3, 'nn', 0)], [(tm, D)], [], [], lambda acc: (acc,),
              [jax.ShapeDtypeStruct((L, D), F32)], [row], comm=comm)
    return _with_comm(res, comm, lambda o: o[0])


def _rope_tables(L):
    rows = jnp.arange(L, dtype=F32)
    pos = jnp.where(rows < BLK, rows, rows - (BLK - N_META))
    inv_r = ROPE_THETA ** (-jnp.arange(0, HD, 2, dtype=F32) / HD)
    ang_r = pos[:, None] * inv_r[None, :]
    cr = jnp.concatenate([jnp.cos(ang_r), jnp.cos(ang_r)], axis=1)
    sr = jnp.concatenate([-jnp.sin(ang_r), jnp.sin(ang_r)], axis=1)
    inv_m = ROPE_THETA ** (-jnp.arange(0, ROPE, 2, dtype=F32) / ROPE)
    ang_m = pos[:, None] * inv_m[None, :]
    z32 = jnp.zeros((L, ROPE // 2), F32)
    z64 = jnp.zeros((L, HD - ROPE), F32)
    cm = jnp.concatenate([jnp.cos(ang_m), jnp.cos(ang_m), z64], axis=1)
    sa = jnp.concatenate([-jnp.sin(ang_m), z32, z64], axis=1)
    sb = jnp.concatenate([z32, jnp.sin(ang_m), z64], axis=1)
    return cr, sr, cm, sa, sb


def _rope_ret(x, cr, sr):
    return x * cr + pltpu.roll(x, HD // 2, 1) * sr


def _rope_ret_t(d, cr, sr):
    return d * cr + pltpu.roll(d * sr, HD // 2, 1)


def _rope_mla(x, cm, sa, sb):
    return x * cm + pltpu.roll(x, HD - ROPE // 2, 1) * sa + pltpu.roll(x, ROPE // 2, 1) * sb


def _rope_mla_t(d, cm, sa, sb):
    return d * cm + pltpu.roll(d * sa, ROPE // 2, 1) + pltpu.roll(d * sb, HD - ROPE // 2, 1)


C_RQ, C_RK, C_RV, C_RG = 0, HEADS * HD, 2 * HEADS * HD, 3 * HEADS * HD
C_CQ = 4 * HEADS * HD
C_CKV = C_CQ + Q_RANK
C_KR = C_CKV + KV_RANK
RET_K_SCALE = HD ** -0.5


def _prep(proj, tabs, qn, kvn):
    L = proj.shape[0]
    tr = _tile(L, 256)
    W = HEADS * HD

    def body(p_ref, cr_ref, sr_ref, cm_ref, sa_ref, sb_ref, qn_ref, kvn_ref, q_ref, k_ref, v_ref, cq_ref, ckv_ref,
             kr_ref):
        cr, sr = cr_ref[...], sr_ref[...]
        for h in range(HEADS):
            sl = slice(h * HD, (h + 1) * HD)
            q_ref[:, sl] = _rope_ret(p_ref[:, C_RQ + h * HD:C_RQ + (h + 1) * HD].astype(F32), cr, sr).astype(BF16)
            k_ref[:, sl] = (_rope_ret(p_ref[:, C_RK + h * HD:C_RK + (h + 1) * HD].astype(F32), cr, sr)
                            * RET_K_SCALE).astype(BF16)
        v_ref[...] = p_ref[:, C_RV:C_RV + W].astype(BF16)
        cq = p_ref[:, C_CQ:C_CQ + Q_RANK].astype(F32)
        cq_ref[...] = (cq * lax.rsqrt(jnp.mean(cq * cq, axis=-1, keepdims=True) + EPS) * qn_ref[...]).astype(BF16)
        ckv = p_ref[:, C_CKV:C_CKV + KV_RANK].astype(F32)
        ckv_ref[...] = (ckv * lax.rsqrt(jnp.mean(ckv * ckv, axis=-1, keepdims=True) + EPS)
                        * kvn_ref[...]).astype(BF16)
        kr_ref[...] = _rope_mla(p_ref[:, C_KR:C_KR + HD].astype(F32), cm_ref[...], sa_ref[...], sb_ref[...]).astype(BF16)

    row = lambda w: pl.BlockSpec((tr, w), lambda i: (i, 0))
    vec = lambda w: pl.BlockSpec((1, w), lambda i: (0, 0))
    return pl.pallas_call(
        body, name="mix_prep", grid=(L // tr,),
        out_shape=[jax.ShapeDtypeStruct((L, W), BF16)] * 3 + [jax.ShapeDtypeStruct((L, Q_RANK), BF16),
                                                              jax.ShapeDtypeStruct((L, KV_RANK), BF16),
                                                              jax.ShapeDtypeStruct((L, HD), BF16)],
        in_specs=[row(D_INP)] + [row(HD)] * 5 + [vec(Q_RANK), vec(KV_RANK)],
        out_specs=[row(W)] * 3 + [row(Q_RANK), row(KV_RANK), row(HD)],
        compiler_params=_params(("parallel",)),
    )(proj, *tabs, qn, kvn)


def _prep_bwd(proj, dq, dk, dv, drg, dcqn, dckvn, dkr8, tabs, qn, kvn):
    L = proj.shape[0]
    tr = _tile(L, 192)
    W = HEADS * HD

    def body(p_ref, dq_ref, dk_ref, dv_ref, drg_ref, dcq_ref, dckv_ref, dkr_ref, cr_ref, sr_ref, cm_ref, sa_ref,
             sb_ref, qn_ref, kvn_ref, dp_ref, dqn_ref, dkvn_ref):
        cr, sr = cr_ref[...], sr_ref[...]
        dkr = None
        for h in range(HEADS):
            sl = slice(h * HD, (h + 1) * HD)
            dp_ref[:, C_RQ + h * HD:C_RQ + (h + 1) * HD] = _rope_ret_t(dq_ref[:, sl].astype(F32), cr, sr).astype(BF16)
            dp_ref[:, C_RK + h * HD:C_RK + (h + 1) * HD] = (_rope_ret_t(dk_ref[:, sl].astype(F32), cr, sr)
                                                            * RET_K_SCALE).astype(BF16)
            part = dkr_ref[:, sl].astype(F32)
            dkr = part if dkr is None else dkr + part
        dp_ref[:, C_RV:C_RV + W] = dv_ref[...].astype(BF16)
        dp_ref[:, C_RG:C_RG + W] = drg_ref[...].astype(BF16)
        dcq, dqn = _norm_bwd_math(p_ref[:, C_CQ:C_CQ + Q_RANK].astype(F32), qn_ref[...], dcq_ref[...])
        dp_ref[:, C_CQ:C_CQ + Q_RANK] = dcq.astype(BF16)
        dckv, dkvn = _norm_bwd_math(p_ref[:, C_CKV:C_CKV + KV_RANK].astype(F32), kvn_ref[...], dckv_ref[...])
        dp_ref[:, C_CKV:C_CKV + KV_RANK] = dckv.astype(BF16)
        dp_ref[:, C_KR:C_KR + HD] = _rope_mla_t(dkr, cm_ref[...], sa_ref[...], sb_ref[...]).astype(BF16)

        @pl.when(pl.program_id(0) == 0)
        def _():
            dqn_ref[...] = jnp.zeros_like(dqn_ref)
            dkvn_ref[...] = jnp.zeros_like(dkvn_ref)

        dqn_ref[...] += dqn
        dkvn_ref[...] += dkvn

    row = lambda w: pl.BlockSpec((tr, w), lambda i: (i, 0))
    vec = lambda w: pl.BlockSpec((1, w), lambda i: (0, 0))
    return pl.pallas_call(
        body, name="mix_prep_bwd", grid=(L // tr,),
        out_shape=[jax.ShapeDtypeStruct((L, D_INP), BF16), jax.ShapeDtypeStruct((1, Q_RANK), F32),
                   jax.ShapeDtypeStruct((1, KV_RANK), F32)],
        in_specs=[row(D_INP)] + [row(W)] * 4 + [row(Q_RANK), row(KV_RANK), row(W)] + [row(HD)] * 5
                 + [vec(Q_RANK), vec(KV_RANK)],
        out_specs=[row(D_INP), vec(Q_RANK), vec(KV_RANK)],
        compiler_params=_params(("arbitrary",)),
    )(proj, dq, dk, dv, drg, dcqn, dckvn, dkr8, *tabs, qn, kvn)


def _post(o_ret, proj, gn):
    L, W = o_ret.shape
    tr = _tile(L, 384)

    def body(o_ref, rg_ref, gn_ref, out_ref):
        for h in range(HEADS):
            sl = slice(h * HD, (h + 1) * HD)
            o = o_ref[:, sl]
            rg = rg_ref[:, sl].astype(F32)
            n = o * lax.rsqrt(jnp.mean(o * o, axis=-1, keepdims=True) + EPS)
            out_ref[:, sl] = (n * gn_ref[:, sl] * (rg * _sigmoid(rg))).astype(BF16)

    row = pl.BlockSpec((tr, W), lambda i: (i, 0))
    return pl.pallas_call(
        body, name="ret_post", grid=(L // tr,), out_shape=jax.ShapeDtypeStruct((L, W), BF16),
        in_specs=[row, pl.BlockSpec((tr, W), lambda i: (i, C_RG // W)), pl.BlockSpec((1, W), lambda i: (0, 0))],
        out_specs=row, compiler_params=_params(("parallel",)),
    )(o_ret, proj, gn)


def _post_bwd(o_ret, proj, gn, dcat):
    L, W = o_ret.shape
    tr = _tile(L, 384)

    def body(o_ref, rg_ref, gn_ref, d_ref, do_ref, drg_ref, dgn_ref):
        @pl.when(pl.program_id(0) == 0)
        def _():
            dgn_ref[...] = jnp.zeros_like(dgn_ref)

        for h in range(HEADS):
            sl = slice(h * HD, (h + 1) * HD)
            o = o_ref[:, sl]
            rg = rg_ref[:, sl].astype(F32)
            d = d_ref[:, sl].astype(F32)
            gw = gn_ref[:, sl]
            r = lax.rsqrt(jnp.mean(o * o, axis=-1, keepdims=True) + EPS)
            n = o * r
            sg = _sigmoid(rg)
            si = rg * sg
            dn = d * gw * si
            dgn_ref[:, sl] += jnp.sum(d * n * si, axis=0, keepdims=True)
            drg_ref[:, sl] = (d * n * gw * sg * (1.0 + rg * (1.0 - sg))).astype(drg_ref.dtype)
            do_ref[:, sl] = (r * (dn - o * (r * r) * jnp.mean(dn * o, axis=-1, keepdims=True))).astype(BF16)

    row = pl.BlockSpec((tr, W), lambda i: (i, 0))
    vec = pl.BlockSpec((1, W), lambda i: (0, 0))
    return pl.pallas_call(
        body, name="ret_post_bwd", grid=(L // tr,),
        out_shape=[jax.ShapeDtypeStruct((L, W), BF16), jax.ShapeDtypeStruct((L, W), BF16),
                   jax.ShapeDtypeStruct((1, W), F32)],
        in_specs=[row, pl.BlockSpec((tr, W), lambda i: (i, C_RG // W)), vec, row],
        out_specs=[row, row, vec], compiler_params=_params(("arbitrary",)),
    )(o_ret, proj, gn, dcat)


RET_HEADS_PER_STEP = 4


def _lin_attn(name, q, k, v, lg, reverse, out_dtype=F32, heads_per_step=None):
    L, W = q.shape
    nc = L // BLK - 1
    G = heads_per_step or RET_HEADS_PER_STEP

    def body(q_ref, k_ref, v_ref, lg_ref, o_ref, s_ref):
        n = lax.broadcasted_iota(jnp.int32, (BLK, BLK), 0).astype(F32)
        m = lax.broadcasted_iota(jnp.int32, (BLK, BLK), 1).astype(F32)
        dist = (m - n) if reverse else (n - m)
        consts = []
        for g in range(G):
            lgv = lg_ref[g, 0:1, :]
            dmask = jnp.where(dist >= 0, jnp.exp(lgv * jnp.maximum(dist, 0.0)), 0.0)
            c = dict(dmask=dmask, dmask0=jnp.where((n < N_META) & (m < N_META), dmask, 0.0),
                     gl=jnp.exp(lgv * float(BLK)))
            if reverse:
                c.update(inter=jnp.exp(lgv * (float(BLK) - n)), upd=jnp.exp(lgv * n),
                         inter0=jnp.where(n < N_META, jnp.exp(lgv * jnp.maximum(float(N_META) - n, 0.0)), 0.0))
            else:
                c.update(inter=jnp.exp(lgv * (n + 1.0)), upd=jnp.exp(lgv * (float(BLK) - 1.0 - n)),
                         upd0=jnp.where(n < N_META, jnp.exp(lgv * jnp.maximum(float(N_META) - 1.0 - n, 0.0)), 0.0))
            consts.append(c)

        def chunk(c):
            rows = pl.ds(pl.multiple_of(c * BLK, BLK), BLK)
            state = [s_ref[g] for g in range(G)]
            outs, new_state = [], []
            for g in range(G):
                cols = slice(g * HD, (g + 1) * HD)
                cg = consts[g]
                qc, kc, vc = q_ref[rows, cols], k_ref[rows, cols], v_ref[rows, cols]
                a = _dot(qc, kc, 'nt') * cg['dmask']
                outs.append(_dot(a.astype(BF16), vc, 'nn') + _dot(qc, state[g].astype(BF16), 'nn') * cg['inter'])
                new_state.append(state[g] * cg['gl'] + _dot((kc.astype(F32) * cg['upd']).astype(BF16), vc, 'tn'))
            for g in range(G):
                o_ref[rows, g * HD:(g + 1) * HD] = outs[g].astype(o_ref.dtype)
                s_ref[g] = new_state[g]

        def first_chunk(with_state):
            for g in range(G):
                cols = slice(g * HD, (g + 1) * HD)
                cg = consts[g]
                q0, k0, v0 = q_ref[0:BLK, cols], k_ref[0:BLK, cols], v_ref[0:BLK, cols]
                o0 = _dot((_dot(q0, k0, 'nt') * cg['dmask0']).astype(BF16), v0, 'nn')
                if with_state:
                    o0 = o0 + _dot(q0, s_ref[g].astype(BF16), 'nn') * cg['inter0']
                else:
                    s_ref[g] = _dot((k0.astype(F32) * cg['upd0']).astype(BF16), v0, 'tn')
                o_ref[0:BLK, cols] = o0.astype(o_ref.dtype)

        if reverse:
            s_ref[...] = jnp.zeros_like(s_ref)

            def step(t, carry):
                chunk(nc - t)
                return carry

            lax.fori_loop(0, nc, step, 0)
            first_chunk(True)
        else:
            first_chunk(False)

            def step(t, carry):
                chunk(t + 1)
                return carry

            lax.fori_loop(0, nc, step, 0)

    once = dict(pipeline_mode=pl.Buffered(1)) if G == HEADS else {}
    col = pl.BlockSpec((L, G * HD), lambda h: (0, h), **once)
    return pl.pallas_call(
        body, name=name, grid=(HEADS // G,), out_shape=jax.ShapeDtypeStruct((L, W), out_dtype),
        in_specs=[col, col, col, pl.BlockSpec((G, 8, HD), lambda h: (h, 0, 0))], out_specs=col,
        scratch_shapes=[pltpu.VMEM((G, HD, HD), F32)], compiler_params=_params(("parallel",)),
    )(q, k, v, lg)


ATT_SCALE = (HD + ROPE) ** -0.5
LOG2E = 1.4426950408889634
Q_PRESCALE = ATT_SCALE * LOG2E
NEG = -1e30


ATT_TILE = 384
ATT_HEADS_PER_STEP = 8
ATT_BWD_HEADS_PER_STEP = 2


def _att_valid(nq, nk, row0, col0):
    r = lax.broadcasted_iota(jnp.int32, (nq, nk), 0) + row0
    c = lax.broadcasted_iota(jnp.int32, (nq, nk), 1) + col0
    return (c <= r) & ((c < N_META) | (c >= BLK))


def _attn_fwd(qm, kn, krr, vm, comm=None):
    L = qm.shape[0]
    W = HEADS * HD
    T = _tile(L, ATT_TILE, BLK)
    nb = L // T
    G = ATT_HEADS_PER_STEP
    n_cm = comm.n if comm is not None else 0

    def body(*refs):
        q_ref, kn_ref, kr_ref, v_ref = refs[:4]
        o_ref, lse_ref = refs[4 + n_cm:6 + n_cm]
        m_sc, l_sc, acc_sc = refs[6 + 2 * n_cm:9 + 2 * n_cm]
        if comm is not None:
            cm_refs = (refs[4:4 + n_cm], refs[6 + n_cm:6 + 2 * n_cm], refs[9 + 2 * n_cm:])
            first, last = _grid_edges((HEADS // G, nb))

            @pl.when(first)
            def _():
                comm.start(*cm_refs)

        i = pl.program_id(1)
        m_sc[...] = jnp.full_like(m_sc, NEG)
        l_sc[...] = jnp.zeros_like(l_sc)
        acc_sc[...] = jnp.zeros_like(acc_sc)

        def tile(j, masked):
            rows = pl.ds(pl.multiple_of(j * T, T), T)
            kr = kr_ref[rows, :]
            valid = _att_valid(T, T, i * T, j * T) if masked else None
            ones = jnp.ones((T, HD), BF16)
            m_prev = [m_sc[g] for g in range(G)]
            l_prev = [l_sc[g] for g in range(G)]
            acc_prev = [acc_sc[g] for g in range(G)]
            m_new, l_new, acc_new = [], [], []
            for g in range(G):
                k = jnp.concatenate([kn_ref[rows, g * HD:(g + 1) * HD], kr], axis=1)
                s = _dot(q_ref[:, g * QH:(g + 1) * QH], k, 'nt')
                if masked:
                    s = jnp.where(valid, s, NEG)
                m_new.append(jnp.maximum(m_prev[g], jnp.max(s, axis=-1, keepdims=True)))
                p = jnp.exp2(s - m_new[g])
                alpha = jnp.exp2(m_prev[g] - m_new[g])
                pv = _dot(p.astype(BF16), jnp.concatenate([v_ref[rows, g * HD:(g + 1) * HD], ones], axis=1), 'nn')
                l_new.append(alpha * l_prev[g] + pv[:, HD:HD + 1])
                acc_new.append(alpha * acc_prev[g] + pv[:, 0:HD])
            for g in range(G):
                m_sc[g] = m_new[g]
                l_sc[g] = l_new[g]
                acc_sc[g] = acc_new[g]

        tile(0, True)

        def mid(j, carry):
            tile(j, False)
            return carry

        lax.fori_loop(1, i, mid, 0)

        @pl.when(i > 0)
        def _():
            tile(i, True)

        for g in range(G):
            l = l_sc[g]
            o_ref[:, g * HD:(g + 1) * HD] = (acc_sc[g] / l).astype(o_ref.dtype)
            lse_ref[g] = jnp.broadcast_to(m_sc[g] + jnp.log(l) * LOG2E, (T, HD))

        if comm is not None:
            @pl.when(last)
            def _():
                comm.finish(*cm_refs)

    cm_specs = comm.specs if comm is not None else []
    res = pl.pallas_call(
        body, name="attn_fwd", grid=(HEADS // G, nb),
        out_shape=[jax.ShapeDtypeStruct((L, W), BF16), jax.ShapeDtypeStruct((HEADS, L, HD), F32)]
        + (comm.out_shapes if comm is not None else []),
        in_specs=[pl.BlockSpec((T, G * QH), lambda h, i: (i, h)), pl.BlockSpec((L, G * HD), lambda h, i: (0, h)),
                  pl.BlockSpec((L, HD), lambda h, i: (0, 0)), pl.BlockSpec((L, G * HD), lambda h, i: (0, h))]
        + cm_specs,
        out_specs=[pl.BlockSpec((T, G * HD), lambda h, i: (i, h)),
                   pl.BlockSpec((G, T, HD), lambda h, i: (h, i, 0))] + cm_specs,
        scratch_shapes=[pltpu.VMEM((G, T, 1), F32), pltpu.VMEM((G, T, 1), F32), pltpu.VMEM((G, T, HD), F32)]
        + (comm.scratch if comm is not None else []),
        compiler_params=_params(("arbitrary", "arbitrary")),
    )(qm, kn, krr, vm, *(comm.arrays if comm is not None else []))
    return res[:2], res[2:]


def _attn_bwd(qm, kn, krr, vm, o, dcat, lse, comm=None):
    L = qm.shape[0]
    W = HEADS * HD
    T = _tile(L, ATT_TILE, BLK)
    nb = L // T
    G = ATT_BWD_HEADS_PER_STEP
    n_cm = comm.n if comm is not None else 0

    def body(*refs):
        q_ref, kn_ref, kr_ref, v_ref, o_ref, do_ref, lse_ref = refs[:7]
        dq_ref, dkn_ref, dkr_ref, dv_ref = refs[7 + n_cm:11 + n_cm]
        dl_sc, dk_sc, dv_sc = refs[11 + 2 * n_cm:14 + 2 * n_cm]
        if comm is not None:
            cm_refs = (refs[7:7 + n_cm], refs[11 + n_cm:11 + 2 * n_cm], refs[14 + 2 * n_cm:])
            first, last = _grid_edges((HEADS // G, nb))

            @pl.when(first)
            def _():
                comm.start(*cm_refs)

        j = pl.program_id(1)
        qs = lambda g: slice(g * QH, (g + 1) * QH)
        hs = lambda g: slice(g * HD, (g + 1) * HD)

        @pl.when(j == 0)
        def _():
            dq_ref[...] = jnp.zeros_like(dq_ref)

            def rowsum(t, carry):
                rows = pl.ds(pl.multiple_of(t * T, T), T)
                for g in range(G):
                    dl_sc[g, rows, :] = jnp.sum(do_ref[rows, hs(g)].astype(F32) * o_ref[rows, hs(g)].astype(F32),
                                                axis=-1, keepdims=True)
                return carry

            lax.fori_loop(0, nb, rowsum, 0)

        kr = kr_ref[...]
        ks = [jnp.concatenate([kn_ref[:, hs(g)], kr], axis=1) for g in range(G)]
        vs = [v_ref[:, hs(g)] for g in range(G)]
        dk_sc[...] = jnp.zeros_like(dk_sc)
        dv_sc[...] = jnp.zeros_like(dv_sc)

        def tile(i, masked):
            rows = pl.ds(pl.multiple_of(i * T, T), T)
            valid = _att_valid(T, T, i * T, j * T) if masked else None
            for g in range(G):
                q = q_ref[rows, qs(g)]
                do = do_ref[rows, hs(g)]
                s = _dot(q, ks[g], 'nt')
                if masked:
                    s = jnp.where(valid, s, NEG)
                p = jnp.exp2(s - lse_ref[g, rows, 0:1])
                dv_sc[g] += _dot(p.astype(BF16), do, 'tn')
                ds = (p * (_dot(do, vs[g], 'nt') - dl_sc[g, rows, :])).astype(BF16)
                dk_sc[g] += _dot(ds, q, 'tn')
                dq_ref[rows, qs(g)] += _dot(ds, ks[g], 'nn')

        tile(j, True)

        def rest(masked):
            def step(i, carry):
                tile(i, masked)
                return carry
            lax.fori_loop(j + 1, nb, step, 0)

        @pl.when(j == 0)
        def _():
            rest(True)

        @pl.when(j > 0)
        def _():
            rest(False)

        for g in range(G):
            dk = dk_sc[g] * (1.0 / LOG2E)
            dkn_ref[:, hs(g)] = dk[:, 0:HD].astype(BF16)
            dkr_ref[:, hs(g)] = dk[:, HD:QH].astype(dkr_ref.dtype)
            dv_ref[:, hs(g)] = dv_sc[g].astype(BF16)

        if comm is not None:
            @pl.when(last)
            def _():
                comm.finish(*cm_refs)

    blk = pl.BlockSpec((T, G * HD), lambda h, j: (j, h))
    once = pl.Buffered(1)
    cm_specs = comm.specs if comm is not None else []
    res = pl.pallas_call(
        body, name="attn_bwd", grid=(HEADS // G, nb),
        out_shape=[jax.ShapeDtypeStruct((L, HEADS * QH), F32), jax.ShapeDtypeStruct((L, W), BF16),
                   jax.ShapeDtypeStruct((L, W), BF16), jax.ShapeDtypeStruct((L, W), BF16)]
        + (comm.out_shapes if comm is not None else []),
        in_specs=[pl.BlockSpec((L, G * QH), lambda h, j: (0, h), pipeline_mode=once), blk,
                  pl.BlockSpec((T, HD), lambda h, j: (j, 0)), blk,
                  pl.BlockSpec((L, G * HD), lambda h, j: (0, h), pipeline_mode=once),
                  pl.BlockSpec((L, G * HD), lambda h, j: (0, HEADS // G + h), pipeline_mode=once),
                  pl.BlockSpec((G, L, HD), lambda h, j: (h, 0, 0), pipeline_mode=once)] + cm_specs,
        out_specs=[pl.BlockSpec((L, G * QH), lambda h, j: (0, h)), blk, blk, blk] + cm_specs,
        scratch_shapes=[pltpu.VMEM((G, L, 1), F32), pltpu.VMEM((G, T, QH), F32), pltpu.VMEM((G, T, HD), F32)]
        + (comm.scratch if comm is not None else []),
        compiler_params=_params(("arbitrary", "arbitrary")),
    )(qm, kn, krr, vm, o, dcat, lse, *(comm.arrays if comm is not None else []))
    return res[:4], res[4:]


def _unrope_q(dqm, tabs_m):
    L, W = dqm.shape
    tr = _tile(L, 384)

    def body(d_ref, cm_ref, sa_ref, sb_ref, out_ref):
        cm, sa, sb = cm_ref[...], sa_ref[...], sb_ref[...]
        for h in range(HEADS):
            out_ref[:, h * QH:h * QH + HD] = (d_ref[:, h * QH:h * QH + HD] * ATT_SCALE).astype(BF16)
            out_ref[:, h * QH + HD:(h + 1) * QH] = _rope_mla_t(d_ref[:, h * QH + HD:(h + 1) * QH] * ATT_SCALE, cm, sa,
                                                               sb).astype(BF16)

    row = pl.BlockSpec((tr, W), lambda i: (i, 0))
    tab = pl.BlockSpec((tr, HD), lambda i: (i, 0))
    return pl.pallas_call(
        body, name="unrope_q", grid=(L // tr,), out_shape=jax.ShapeDtypeStruct((L, W), BF16),
        in_specs=[row, tab, tab, tab], out_specs=row, compiler_params=_params(("parallel",)),
    )(dqm, *tabs_m)


def _q_up(cqn, wuq_p, tabs_m):
    L = cqn.shape[0]
    tm = _tile(L, 704)

    def ep(acc, cm, sa, sb):
        acc = acc * Q_PRESCALE
        parts = []
        for h in range(HEADS):
            parts.append(acc[:, h * QH:h * QH + HD])
            parts.append(_rope_mla(acc[:, h * QH + HD:(h + 1) * QH], cm, sa, sb))
        return (jnp.concatenate(parts, axis=1),)

    tab = pl.BlockSpec((tm, HD), lambda i, j: (i, 0))
    return _mm("mla_q_up", (L // tm, 1), ("parallel", "parallel"), None,
               [cqn, wuq_p], [pl.BlockSpec((tm, Q_RANK), lambda i, j: (i, 0)),
                              pl.BlockSpec((HEADS * QH, Q_RANK), lambda i, j: (0, 0))],
               [(0, 1, 'nt', 0)], [(tm, HEADS * QH)], list(tabs_m), [tab] * 3, ep,
               [jax.ShapeDtypeStruct((L, HEADS * QH), BF16)], [pl.BlockSpec((tm, HEADS * QH), lambda i, j: (i, 0))])[0]


def _mix_out(cat, w_out, h_in, post, next_norm):
    L, K = cat.shape
    D = w_out.shape[1]
    tm, tk = _tile(L, 384), K
    row = pl.BlockSpec((tm, D), lambda i, k: (i, 0))
    vec = pl.BlockSpec((1, D), lambda i, k: (0, 0))
    return _mm("mix_out", (L // tm, K // tk), ("parallel", "arbitrary"), 1,
               [cat, w_out], [pl.BlockSpec((tm, tk), lambda i, k: (i, k)), pl.BlockSpec((tk, D), lambda i, k: (k, 0))],
               [(0, 1, 'nn', 0)], [(tm, D)], [h_in, post, next_norm], [row, vec, vec], _resnorm_epilogue(1.0, True),
               [jax.ShapeDtypeStruct((L, D), F32)] * 2 + [jax.ShapeDtypeStruct((L, D), BF16)], [row, row, row])


ADAM_BLOCK_ELEMS = 512 * 704


def _adam_math(w, g, m, v):
    m = ADAM_B1 * m + (1.0 - ADAM_B1) * g
    v = ADAM_B2 * v + (1.0 - ADAM_B2) * (g * g)
    m_hat = m / (1.0 - ADAM_B1 ** ADAM_STEP)
    v_hat = v / (1.0 - ADAM_B2 ** ADAM_STEP)
    delta = -ADAM_LR * (m_hat / (jnp.sqrt(v_hat) + ADAM_EPS) + ADAM_WD * w)
    return delta, m, v


def _adam(name, w, m, v, g_slots=None, g=None, after=None):
    R, C = w.shape
    tr, tc = _tile(R, max(16, ADAM_BLOCK_ELEMS // C // 16 * 16), 16), C
    if tr * tc > ADAM_BLOCK_ELEMS:
        tr, tc = R, _tile(C, max(128, ADAM_BLOCK_ELEMS // R // 128 * 128), 128)
    from_slots = g_slots is not None

    def body(w_ref, m_ref, v_ref, g_ref, *rest):
        go_ref, d_ref, mo_ref, vo_ref = rest[-4:]
        if from_slots:
            grad = g_ref[0].astype(F32)
            for s in range(1, N_DEV):
                grad = grad + g_ref[s].astype(F32)
        else:
            grad = g_ref[...]
        delta, mn, vn = _adam_math(w_ref[...], grad, m_ref[...], v_ref[...])
        go_ref[...] = grad
        d_ref[...] = delta
        mo_ref[...] = mn
        vo_ref[...] = vn

    row = pl.BlockSpec((tr, tc), lambda i, j: (i, j))
    gspec = pl.BlockSpec((N_DEV, tr, tc), lambda i, j: (0, i, j)) if from_slots else row
    order = [] if after is None else [after]
    return pl.pallas_call(
        body, name=name, grid=(R // tr, C // tc), out_shape=[jax.ShapeDtypeStruct((R, C), F32)] * 4,
        in_specs=[row, row, row, gspec] + [pl.BlockSpec(memory_space=pl.ANY)] * len(order), out_specs=[row] * 4,
        compiler_params=_params(("parallel", "parallel")),
    )(w, m, v, g_slots if from_slots else g, *order)


def _unblock(gathered):
    n, r, c = gathered.shape
    return jnp.transpose(gathered, (1, 0, 2)).reshape(r, n * c)


def _reblock(full, c):
    r = full.shape[0]
    return jnp.transpose(full[:, :N_DEV * c].reshape(r, N_DEV, c), (1, 0, 2))


def _step(x, target, w, mom, vel):
    S, D = x.shape[1], x.shape[2]
    L = S + BLK
    def sq(a, n):
        if a.ndim == 2:
            return a
        if n in TRANSPOSED:
            a = jnp.swapaxes(a, 1, 2)
        return a.reshape(a.shape[1:])

    def unsq(o, n):
        o = o.reshape((1,) + o.shape)
        return jnp.swapaxes(o, 1, 2) if n in TRANSPOSED else o

    p = {n: sq(w[n], n) for n in WEIGHTS if n != 'meta_tokens'}
    gather = lambda names: _Exchange([p[n].astype(BF16) for n in names], False)
    scatter = lambda blocks: _Exchange(blocks, True)
    in_s, uq_s = p['w_in'].shape[0], p['mla_w_uq'].shape[0]
    assert uq_s == HD + ROPE and N_DEV == HEADS, "a w_uq shard is one head's columns"
    tabs = _rope_tables(L)
    tabs_m = tabs[2:]
    lg = jnp.broadcast_to(jnp.log(1.0 - 2.0 ** (-5.0 - jnp.arange(HEADS, dtype=F32)))[:, None, None], (HEADS, 8, HD))
    R = {}

    wg1, meta = _exchange("gather_first", [p['ffn1_w_gate'].astype(BF16), w['meta_tokens']], False)
    h0 = jnp.concatenate([_unblock(meta), jnp.zeros((BLK - N_META, D), F32), x[0]], axis=0)
    a1 = _norm_fwd(h0, p['ffn1_pre_norm'])
    g1, (wu1,) = _ffn_gate(a1, wg1, comm=gather(['ffn1_w_up']))
    (u1, hid1), (wd1,) = _ffn_up_gated(a1, wu1, g1, comm=gather(['ffn1_w_down']))
    (f1, h1, um), (w_in_g,) = _ffn_down(hid1, wd1, h0, p['ffn1_post_norm'], next_norm=p['mix_pre_norm'],
                                        comm=gather(['w_in']))

    w_in = jnp.pad(w_in_g.reshape(N_DEV * in_s, D), ((0, D_INP - N_DEV * in_s), (0, 0)))
    proj, (uq_g, uk_g, uv_g, wout_g) = _mm_nt("mix_in", [(um, w_in)], BF16, tn_target=1664,
                                              comm=gather(['mla_w_uq', 'mla_w_uk', 'mla_w_uv', 'w_out']))
    wuq = jnp.pad(uq_g, ((0, 0), (0, QH - uq_s), (0, 0))).reshape(HEADS * QH, Q_RANK)
    wuk, wuv, w_out = _unblock(uk_g), _unblock(uv_g), wout_g.reshape(-1, D)
    qr, kr, vr, cqn, ckvn, krr = _prep(proj, tabs, p['mla_q_norm'], p['mla_kv_norm'])
    qm = _q_up(cqn, wuq, tabs_m)
    kn = _mm_nn("mla_k_up", ckvn, wuk, BF16)
    vm = _mm_nn("mla_v_up", ckvn, wuv, BF16)
    (o_mla, lse), (wg2, wu2) = _attn_fwd(qm, kn, krr, vm, comm=gather(['ffn2_w_gate', 'ffn2_w_up']))
    o_ret = _lin_attn("ret_fwd", qr, kr, vr, lg, False)
    ret = _post(o_ret, proj, p['ret_group_norm'])
    cat = jnp.concatenate([ret, o_mla], axis=1)
    m, h2, a2 = _mix_out(cat, w_out, h1, p['mix_post_norm'], p['ffn2_pre_norm'])

    (g2, u2, hid2), (wd2,) = _ffn_up(a2, wg2, wu2, comm=gather(['ffn2_w_down']))
    f2, h3 = _ffn_down(hid2, wd2, h2, p['ffn2_post_norm'])
    dh3, loss_blk = _loss(h3, target[0])

    dsmall = {}
    df2, dsmall['ffn2_post_norm'] = _norm_bwd(f2, p['ffn2_post_norm'], dh3, None, 0.5, BF16)
    dg2, du2 = _ffn_dhid(df2, wd2, g2, u2)
    dwd2 = _ffn_dwd(hid2, df2)
    dwg2, dwu2 = _ffn_dwgu(a2, dg2, du2)
    da2, (R['ffn2_w_down'],) = _ffn_da(dg2, du2, wg2, wu2, comm=scatter([dwd2]))
    dh2, dsmall['ffn2_pre_norm'] = _norm_bwd(h2, p['ffn2_pre_norm'], da2, dh3, 1.0, F32)

    dm, dsmall['mix_post_norm'] = _norm_bwd(m, p['mix_post_norm'], dh2, None, 1.0, BF16)
    dcat = _mm_nt("mix_dcat", [(dm, w_out)], BF16)
    dwout = _mm_tn("mix_dwout", cat, [dm])[0]
    do_ret, drg, dsmall['ret_group_norm'] = _post_bwd(o_ret, proj, p['ret_group_norm'], dcat)
    dqr = _lin_attn("ret_dq", do_ret, vr, kr, lg, False, BF16, HEADS)
    dkr = _lin_attn("ret_dk", vr, do_ret, qr, lg, True, BF16, HEADS)
    dvr = _lin_attn("ret_dv", kr, qr, do_ret, lg, True, BF16, HEADS)
    (dqm, dkn, dkr8, dvm), (R['ffn2_w_gate'], R['ffn2_w_up']) = _attn_bwd(
        qm, kn, krr, vm, o_mla, dcat, lse, comm=scatter([dwg2, dwu2]))
    dqp = _unrope_q(dqm, tabs_m)
    dwuq = _mm_tn("mla_dwuq", dqp, [cqn])[0]
    dcqn = _mm_nn("mla_dcq", dqp, wuq, F32)
    dwuk, dwuv = _mm_tn("mla_dwukv", ckvn, [dkn, dvm])
    dckvn = _mm_nt("mla_dckv", [(dkn, wuk), (dvm, wuv)], F32)
    dproj, dsmall['mla_q_norm'], dsmall['mla_kv_norm'] = _prep_bwd(
        proj, dqr, dkr, dvr, drg, dcqn, dckvn, dkr8, tabs, p['mla_q_norm'], p['mla_kv_norm'])
    dwuq_b = dwuq.reshape(HEADS, QH, Q_RANK)[:, :uq_s]
    (dwin,), (R['w_out'], R['mla_w_uq'], R['mla_w_uk'], R['mla_w_uv']) = _mm_tn(
        "mix_dwin", dproj, [um], comm=scatter([dwout.reshape(N_DEV, -1, D), dwuq_b,
                                               _reblock(dwuk, p['mla_w_uk'].shape[1]),
                                               _reblock(dwuv, p['mla_w_uv'].shape[1])]))
    dwin_b = dwin[:N_DEV * in_s].reshape(N_DEV, in_s, D)
    half = D // 2
    dum, (r_win_a,) = _mm_nn("mix_du", dproj, w_in, F32, tn_target=512, comm=scatter([dwin_b[:, :, :half]]))
    dh1, dsmall['mix_pre_norm'] = _norm_bwd(h1, p['mix_pre_norm'], dum, dh2, 1.0, F32)

    df1, dsmall['ffn1_post_norm'] = _norm_bwd(f1, p['ffn1_post_norm'], dh1, None, 0.5, BF16)
    (dg1, du1), (r_win_b,) = _ffn_dhid(df1, wd1, g1, u1, comm=scatter([dwin_b[:, :, half:]]))
    R['w_in'] = jnp.concatenate([r_win_a, r_win_b], axis=2)
    dwd1 = _ffn_dwd(hid1, df1)
    (dwg1, dwu1), (R['ffn1_w_down'],) = _ffn_dwgu(a1, dg1, du1, comm=scatter([dwd1]))
    da1, (R['ffn1_w_gate'],) = _ffn_da(dg1, du1, wg1, wu1, comm=scatter([dwg1]))
    dh0, dsmall['ffn1_pre_norm'] = _norm_bwd(h0, p['ffn1_pre_norm'], da1, dh1, 1.0, F32)
    tail_sems_s, tail_sems_r, tail_src, tail_land, token = _scatter_start(dwu1)

    def slab(a):
        a = a.reshape(-1, 128)
        return jnp.pad(a, ((0, (-a.shape[0]) % 8), (0, 0)))

    slab_rows = lambda n: -(-(p[n].shape[-1] // 128) // 8) * 8
    packed = jnp.concatenate([slab(dsmall[n]) for n in SMALL] + [slab(dh0[:N_META]), loss_blk], axis=0)
    red = _allreduce_small(packed + token[0, 0])
    offs = sum(slab_rows(n) for n in SMALL)
    n_small = offs
    gmeta_full = red[offs:offs + N_META * D // 128].reshape(N_META, D)
    offs += N_META * D // 128
    loss = red[offs, 0]

    grad, delta, new_m, new_v = {}, {}, {}, {}
    meanwhile = []
    for n in BIG:
        if n == 'ffn1_w_up':
            continue
        outs = _adam("adam_" + n, p[n], sq(mom[n], n), sq(vel[n], n), g_slots=R[n], after=token)
        meanwhile.append(outs[0])
        grad[n], delta[n], new_m[n], new_v[n] = [unsq(o, n) for o in outs]
    pack = lambda d: jnp.concatenate([slab(d[n]) for n in SMALL], axis=0)
    outs = _adam("adam_small", pack(w), pack(mom), pack(vel), g=red[:n_small])
    meanwhile.append(outs[0])
    offs = 0
    for n in SMALL:
        r = p[n].shape[-1] // 128
        grad[n], delta[n], new_m[n], new_v[n] = [o[offs:offs + r].reshape(w[n].shape) for o in outs]
        offs += slab_rows(n)
    dev = 4 * lax.axis_index("x") + 2 * lax.axis_index("y") + lax.axis_index("c")
    mcols = w['meta_tokens'].shape[1]
    gmeta = lax.dynamic_slice(gmeta_full, (0, dev * mcols), (N_META, mcols))
    outs = _adam("adam_meta", w['meta_tokens'], mom['meta_tokens'], vel['meta_tokens'], g=gmeta)
    grad['meta_tokens'], delta['meta_tokens'], new_m['meta_tokens'], new_v['meta_tokens'] = outs
    meanwhile.append(outs[0])
    n = 'ffn1_w_up'
    slots = _scatter_wait(tail_sems_s, tail_sems_r, tail_src, tail_land, meanwhile)
    outs = _adam("adam_" + n, p[n], sq(mom[n], n), sq(vel[n], n), g_slots=slots)
    grad[n], delta[n], new_m[n], new_v[n] = [unsq(o, n) for o in outs]

    return (loss, dh0[BLK:][None], *[grad[n] for n in WEIGHTS], *[delta[n] for n in WEIGHTS],
            *[new_m[n] for n in WEIGHTS], *[new_v[n] for n in WEIGHTS])


def kernel(x, meta_tokens, ffn1_pre_norm, ffn1_w_gate, ffn1_w_up, ffn1_w_down, ffn1_post_norm, mix_pre_norm, w_in, ret_group_norm, mla_q_norm, mla_w_uq, mla_kv_norm, mla_w_uk, mla_w_uv, w_out, mix_post_norm, ffn2_pre_norm, ffn2_w_gate, ffn2_w_up, ffn2_w_down, ffn2_post_norm, loss_target, m_meta_tokens, m_ffn1_pre_norm, m_ffn1_w_gate, m_ffn1_w_up, m_ffn1_w_down, m_ffn1_post_norm, m_mix_pre_norm, m_w_in, m_ret_group_norm, m_mla_q_norm, m_mla_w_uq, m_mla_kv_norm, m_mla_w_uk, m_mla_w_uv, m_w_out, m_mix_post_norm, m_ffn2_pre_norm, m_ffn2_w_gate, m_ffn2_w_up, m_ffn2_w_down, m_ffn2_post_norm, v_meta_tokens, v_ffn1_pre_norm, v_ffn1_w_gate, v_ffn1_w_up, v_ffn1_w_down, v_ffn1_post_norm, v_mix_pre_norm, v_w_in, v_ret_group_norm, v_mla_q_norm, v_mla_w_uq, v_mla_kv_norm, v_mla_w_uk, v_mla_w_uv, v_w_out, v_mix_post_norm, v_ffn2_pre_norm, v_ffn2_w_gate, v_ffn2_w_up, v_ffn2_w_down, v_ffn2_post_norm):
    w = dict(zip(WEIGHTS, (meta_tokens, ffn1_pre_norm, ffn1_w_gate, ffn1_w_up, ffn1_w_down, ffn1_post_norm,
                           mix_pre_norm, w_in, ret_group_norm, mla_q_norm, mla_w_uq, mla_kv_norm, mla_w_uk, mla_w_uv,
                           w_out, mix_post_norm, ffn2_pre_norm, ffn2_w_gate, ffn2_w_up, ffn2_w_down, ffn2_post_norm)))
    mom = dict(zip(WEIGHTS, (m_meta_tokens, m_ffn1_pre_norm, m_ffn1_w_gate, m_ffn1_w_up, m_ffn1_w_down,
                             m_ffn1_post_norm, m_mix_pre_norm, m_w_in, m_ret_group_norm, m_mla_q_norm, m_mla_w_uq,
                             m_mla_kv_norm, m_mla_w_uk, m_mla_w_uv, m_w_out, m_mix_post_norm, m_ffn2_pre_norm,
                             m_ffn2_w_gate, m_ffn2_w_up, m_ffn2_w_down, m_ffn2_post_norm)))
    vel = dict(zip(WEIGHTS, (v_meta_tokens, v_ffn1_pre_norm, v_ffn1_w_gate, v_ffn1_w_up, v_ffn1_w_down,
                             v_ffn1_post_norm, v_mix_pre_norm, v_w_in, v_ret_group_norm, v_mla_q_norm, v_mla_w_uq,
                             v_mla_kv_norm, v_mla_w_uk, v_mla_w_uv, v_w_out, v_mix_post_norm, v_ffn2_pre_norm,
                             v_ffn2_w_gate, v_ffn2_w_up, v_ffn2_w_down, v_ffn2_post_norm)))
    return _step(x, loss_target, w, mom, vel)
```

```python
import functools
import math

import jax
import jax.numpy as jnp
from jax import lax
from jax.experimental import pallas as pl
from jax.experimental.pallas import tpu as pltpu

N_DEV = 8
N_META = 16
BLK = 128
HEADS = 8
HD = 128
ROPE = 64
Q_RANK = 512
KV_RANK = 256
QH = 2 * HD
D_INP = 4 * HEADS * HD + Q_RANK + KV_RANK + BLK
ROPE_THETA = 10000.0
EPS = 1e-6
ADAM_LR = 0.001
ADAM_B1 = 0.9
ADAM_B2 = 0.999
ADAM_EPS = 1e-08
ADAM_WD = 0.01
ADAM_STEP = 10
V7X_VMEM_LIMIT = 48 * 1024 * 1024
MESH = pl.DeviceIdType.MESH
F32 = jnp.float32
BF16 = jnp.bfloat16

WEIGHTS = ['meta_tokens', 'ffn1_pre_norm', 'ffn1_w_gate', 'ffn1_w_up', 'ffn1_w_down', 'ffn1_post_norm',
           'mix_pre_norm', 'w_in', 'ret_group_norm', 'mla_q_norm', 'mla_w_uq', 'mla_kv_norm', 'mla_w_uk',
           'mla_w_uv', 'w_out', 'mix_post_norm', 'ffn2_pre_norm', 'ffn2_w_gate', 'ffn2_w_up', 'ffn2_w_down',
           'ffn2_post_norm']
SMALL = ['ffn1_pre_norm', 'ffn1_post_norm', 'mix_pre_norm', 'ret_group_norm', 'mla_q_norm', 'mla_kv_norm',
         'mix_post_norm', 'ffn2_pre_norm', 'ffn2_post_norm']
TRANSPOSED = ('ffn1_w_gate', 'ffn1_w_up', 'ffn2_w_gate', 'ffn2_w_up', 'w_in', 'mla_w_uq')
BIG = ['ffn1_w_gate', 'ffn1_w_up', 'ffn1_w_down', 'w_in', 'mla_w_uq', 'mla_w_uk', 'mla_w_uv', 'w_out',
       'ffn2_w_gate', 'ffn2_w_up', 'ffn2_w_down']

_DIMS = {'nn': (((1,), (0,)), ((), ())), 'nt': (((1,), (1,)), ((), ())), 'tn': (((0,), (0,)), ((), ()))}


def _tile(n, target, mult=16):
    best = None
    for t in range(mult, min(n, target) + 1, mult):
        if n % t == 0:
            best = t
    return best if best is not None else n


def _params(sem):
    return pltpu.CompilerParams(dimension_semantics=sem, vmem_limit_bytes=V7X_VMEM_LIMIT)


def _dot(a, b, dims):
    return lax.dot_general(a, b, _DIMS[dims], preferred_element_type=F32)


def _sigmoid(x):
    return 0.5 * jnp.tanh(0.5 * x) + 0.5


def _me_and_peers():
    x, y, c = lax.axis_index("x"), lax.axis_index("y"), lax.axis_index("c")

    def peer(j):
        px = 1 - x if (j >> 2) & 1 else x
        py = 1 - y if (j >> 1) & 1 else y
        pc = 1 - c if j & 1 else c
        return (px, py, pc), 4 * px + 2 * py + pc

    return 4 * x + 2 * y + c, peer


class _Exchange:
    def __init__(self, arrays, per_peer):
        self.arrays = list(arrays)
        self.per_peer = per_peer
        self.n = len(self.arrays)
        self.out_shapes = [jax.ShapeDtypeStruct((N_DEV,) + tuple(a.shape[1:] if per_peer else a.shape), a.dtype)
                           for a in self.arrays]
        self.specs = [pl.BlockSpec(memory_space=pl.ANY)] * self.n
        self.scratch = [pltpu.SemaphoreType.DMA((7 * self.n,)), pltpu.SemaphoreType.DMA((7 * self.n,)),
                        pltpu.SemaphoreType.DMA((self.n,))]

    def _copies(self, src, dst, sems):
        send_sems, recv_sems, local_sems = sems
        me, peer = _me_and_peers()
        sib, _ = peer(1)
        local, sends, recvs, passes = [], {}, {}, {}
        for k in range(self.n):
            own = src[k].at[me] if self.per_peer else src[k]
            local.append(pltpu.make_async_copy(own, dst[k].at[me], local_sems.at[k]))
            for j in range(1, N_DEV):
                pid, pidx = peer(j)
                out = src[k].at[pidx] if self.per_peer else src[k]
                sem = dict(send_sem=send_sems.at[k * 7 + j - 1], recv_sem=recv_sems.at[k * 7 + j - 1])
                recvs[k, j] = pltpu.make_async_remote_copy(src_ref=out, dst_ref=dst[k].at[pidx], device_id=pid,
                                                           device_id_type=MESH, **sem)
                if self.per_peer or j in (1, 2, 4, 6):
                    sends[k, j] = pltpu.make_async_remote_copy(src_ref=out, dst_ref=dst[k].at[me], device_id=pid,
                                                               device_id_type=MESH, **sem)
                else:
                    _, origin = peer(j ^ 1)
                    passes[k, j ^ 1] = pltpu.make_async_remote_copy(
                        src_ref=dst[k].at[origin], dst_ref=dst[k].at[origin], device_id=sib, device_id_type=MESH, **sem)
        return local, sends, recvs, passes

    def start(self, src, dst, sems):
        local, sends, _, _ = self._copies(src, dst, sems)
        for cp in local + list(sends.values()):
            cp.start()

    def finish(self, src, dst, sems):
        local, sends, recvs, passes = self._copies(src, dst, sems)
        for key, cp in passes.items():
            recvs[key].wait_recv()
            cp.start()
        for key, cp in recvs.items():
            if key not in passes:
                cp.wait_recv()
        for cp in list(sends.values()) + list(passes.values()):
            cp.wait_send()
        for cp in local:
            cp.wait()


def _grid_edges(grid):
    first, last = None, None
    for a, n in enumerate(grid):
        f, l = pl.program_id(a) == 0, pl.program_id(a) == n - 1
        first = f if first is None else first & f
        last = l if last is None else last & l
    return first, last


def _exchange(name, arrays, per_peer):
    ex = _Exchange(arrays, per_peer)
    n = ex.n

    def body(*refs):
        ex.start(refs[:n], refs[n:2 * n], refs[2 * n:])
        ex.finish(refs[:n], refs[n:2 * n], refs[2 * n:])

    return pl.pallas_call(body, name=name, out_shape=ex.out_shapes, in_specs=ex.specs, out_specs=ex.specs,
                          scratch_shapes=ex.scratch)(*arrays)


def _scatter_start(blocks):
    def body(src_ref, land_ref, send_sems, recv_sems, src_thru, land_thru, token, local_sem):
        me, peer = _me_and_peers()
        local = pltpu.make_async_copy(src_ref.at[me], land_ref.at[me], local_sem)
        local.start()
        for j in range(1, N_DEV):
            pid, pidx = peer(j)
            pltpu.make_async_remote_copy(src_ref=src_ref.at[pidx], dst_ref=land_ref.at[me],
                                         send_sem=send_sems.at[j - 1], recv_sem=recv_sems.at[j - 1],
                                         device_id=pid, device_id_type=MESH).start()
        local.wait()
        token[...] = jnp.zeros_like(token)

    hbm = pl.BlockSpec(memory_space=pltpu.HBM)
    sem = pl.BlockSpec(memory_space=pltpu.SEMAPHORE)
    return pl.pallas_call(
        body, name="scatter_tail_start",
        out_shape=(pltpu.SemaphoreType.DMA((7,)), pltpu.SemaphoreType.DMA((7,)), pltpu.HBM(blocks.shape, blocks.dtype),
                   pltpu.HBM(blocks.shape, blocks.dtype), jax.ShapeDtypeStruct((8, 128), F32)),
        in_specs=(hbm, hbm), out_specs=(sem, sem, hbm, hbm, pl.BlockSpec(memory_space=pltpu.VMEM)),
        input_output_aliases={0: 2, 1: 3}, scratch_shapes=[pltpu.SemaphoreType.DMA],
        compiler_params=pltpu.CompilerParams(has_side_effects=pltpu.SideEffectType.DATAFLOW_SIDE_EFFECTING),
    )(pltpu.with_memory_space_constraint(blocks, pltpu.HBM),
      pltpu.with_memory_space_constraint(lax.empty(blocks.shape, blocks.dtype), pltpu.HBM))


def _scatter_wait(send_sems, recv_sems, src_thru, land_thru, after):
    n_after = len(after)

    def body(src_ref, land_ref, send_sems, recv_sems, *rest):
        me, peer = _me_and_peers()
        for j in range(1, N_DEV):
            pid, pidx = peer(j)
            cp = pltpu.make_async_remote_copy(src_ref=src_ref.at[pidx], dst_ref=land_ref.at[pidx],
                                              send_sem=send_sems.at[j - 1], recv_sem=recv_sems.at[j - 1],
                                              device_id=pid, device_id_type=MESH)
            cp.wait_send()
            cp.wait_recv()

    hbm = pl.BlockSpec(memory_space=pltpu.HBM)
    sem = pl.BlockSpec(memory_space=pltpu.SEMAPHORE)
    return pl.pallas_call(
        body, name="scatter_tail_wait",
        out_shape=(pltpu.HBM(src_thru.shape, src_thru.dtype), pltpu.HBM(land_thru.shape, land_thru.dtype)),
        in_specs=(hbm, hbm, sem, sem) + (pl.BlockSpec(memory_space=pl.ANY),) * n_after, out_specs=(hbm, hbm),
        input_output_aliases={0: 0, 1: 1},
        compiler_params=pltpu.CompilerParams(has_side_effects=pltpu.SideEffectType.DATAFLOW_SIDE_EFFECTING),
    )(src_thru, land_thru, send_sems, recv_sems, *after)[1]


def _allreduce_small(v):
    rows = v.shape[0]

    def body(v_ref, out_ref, buf, send_sems, recv_sems):
        me, peer = _me_and_peers()
        buf[pl.ds(me, 1)] = v_ref[...][None]
        sends = []
        for j in range(1, N_DEV):
            pid, _ = peer(j)
            cp = pltpu.make_async_remote_copy(src_ref=v_ref, dst_ref=buf.at[me], send_sem=send_sems.at[j - 1],
                                              recv_sem=recv_sems.at[j - 1], device_id=pid, device_id_type=MESH)
            cp.start()
            sends.append(cp)
        for j in range(1, N_DEV):
            pid, pidx = peer(j)
            pltpu.make_async_remote_copy(src_ref=v_ref, dst_ref=buf.at[pidx], send_sem=send_sems.at[j - 1],
                                         recv_sem=recv_sems.at[j - 1], device_id=pid,
                                         device_id_type=MESH).wait_recv()
        for cp in sends:
            cp.wait_send()
        acc = buf[0]
        for s in range(1, N_DEV):
            acc = acc + buf[s]
        out_ref[...] = acc

    vm = pl.BlockSpec(memory_space=pltpu.VMEM)
    return pl.pallas_call(
        body, name="allreduce_small", out_shape=jax.ShapeDtypeStruct(v.shape, F32),
        in_specs=[vm], out_specs=vm,
        scratch_shapes=[pltpu.VMEM((N_DEV, rows, 128), F32), pltpu.SemaphoreType.DMA((7,)),
                        pltpu.SemaphoreType.DMA((7,))],
    )(v)


def _mm(name, grid, sem, k_axis, ops, op_specs, pairs, acc_shapes, extras, extra_specs, epilogue, outs, out_specs,
        comm=None):
    n_op, n_ex, n_out = len(ops), len(extras), len(outs)
    nk = grid[k_axis] if k_axis is not None else 1
    n_acc = len(acc_shapes) if nk > 1 else 0
    n_cm = comm.n if comm is not None else 0

    def body(*refs):
        op_refs = refs[:n_op]
        ex_refs = refs[n_op:n_op + n_ex]
        n_in = n_op + n_ex + n_cm
        out_refs = refs[n_in:n_in + n_out]
        acc_refs = refs[n_in + n_out + n_cm:n_in + n_out + n_cm + n_acc]
        if comm is not None:
            cm_refs = (refs[n_op + n_ex:n_in], refs[n_in + n_out:n_in + n_out + n_cm],
                       refs[n_in + n_out + n_cm + n_acc:])
            first, last = _grid_edges(grid)

            @pl.when(first)
            def _():
                comm.start(*cm_refs)

        def finish(vals):
            res = epilogue(*vals, *[e[...] for e in ex_refs])
            for o, r in zip(out_refs, res):
                o[...] = r.astype(o.dtype)

        if nk == 1:
            parts = [None] * len(acc_shapes)
            for li, ri, dims, ai in pairs:
                d = _dot(op_refs[li][...], op_refs[ri][...], dims)
                parts[ai] = d if parts[ai] is None else parts[ai] + d
            finish(parts)
        else:
            k = pl.program_id(k_axis)

            @pl.when(k == 0)
            def _():
                for a in acc_refs:
                    a[...] = jnp.zeros_like(a)

            for li, ri, dims, ai in pairs:
                acc_refs[ai][...] += _dot(op_refs[li][...], op_refs[ri][...], dims)

            @pl.when(k == nk - 1)
            def _():
                finish([a[...] for a in acc_refs])

        if comm is not None:
            @pl.when(last)
            def _():
                comm.finish(*cm_refs)

    scratch = [pltpu.VMEM(s, F32) for s in acc_shapes] if nk > 1 else []
    if comm is None:
        return pl.pallas_call(
            body, name=name, grid=grid, out_shape=outs,
            in_specs=list(op_specs) + list(extra_specs), out_specs=list(out_specs),
            scratch_shapes=scratch, compiler_params=_params(sem),
        )(*ops, *extras)
    res = pl.pallas_call(
        body, name=name, grid=grid, out_shape=list(outs) + comm.out_shapes,
        in_specs=list(op_specs) + list(extra_specs) + comm.specs, out_specs=list(out_specs) + comm.specs,
        scratch_shapes=scratch + comm.scratch, compiler_params=_params(("arbitrary",) * len(grid)),
    )(*ops, *extras, *comm.arrays)
    return res[:n_out], res[n_out:]


def _with_comm(res, comm, pick):
    if comm is None:
        return pick(res)
    return pick(res[0]), res[1]


def _mm_nn(name, a, w, out_dtype, tm_target=704, tn_target=1664, epilogue=None, extras=(), extra_specs=(), comm=None):
    L, K = a.shape
    N = w.shape[1]
    tm, tn = _tile(L, tm_target), _tile(N, tn_target, 128)
    ep = epilogue if epilogue is not None else (lambda acc: (acc,))
    res = _mm(name, (L // tm, N // tn), ("parallel", "parallel"), None,
              [a, w], [pl.BlockSpec((tm, K), lambda i, j: (i, 0)), pl.BlockSpec((K, tn), lambda i, j: (0, j))],
              [(0, 1, 'nn', 0)], [(tm, tn)], list(extras), list(extra_specs), ep,
              [jax.ShapeDtypeStruct((L, N), out_dtype)], [pl.BlockSpec((tm, tn), lambda i, j: (i, j))], comm=comm)
    return _with_comm(res, comm, lambda o: o[0])


def _mm_nt(name, pairs_aw, out_dtype, tm_target=704, tn_target=512, comm=None):
    L = pairs_aw[0][0].shape[0]
    N = pairs_aw[0][1].shape[0]
    tm, tn = _tile(L, tm_target), _tile(N, tn_target, 128)
    ops, specs, pairs = [], [], []
    for t, (a, w) in enumerate(pairs_aw):
        K = a.shape[1]
        ops += [a, w]
        specs += [pl.BlockSpec((tm, K), lambda i, j: (i, 0)), pl.BlockSpec((tn, K), lambda i, j: (j, 0))]
        pairs.append((2 * t, 2 * t + 1, 'nt', 0))
    res = _mm(name, (L // tm, N // tn), ("parallel", "parallel"), None, ops, specs, pairs, [(tm, tn)], [], [],
              lambda acc: (acc,), [jax.ShapeDtypeStruct((L, N), out_dtype)],
              [pl.BlockSpec((tm, tn), lambda i, j: (i, j))], comm=comm)
    return _with_comm(res, comm, lambda o: o[0])


def _mm_tn(name, a, bs, out_dtype=BF16, tk_target=1408, tn_target=1664, tm_target=2048, comm=None):
    L, M = a.shape
    N = bs[0].shape[1]
    tk, tn, tm = _tile(L, tk_target), _tile(N, tn_target, 128), _tile(M, tm_target, 128)
    nb = len(bs)
    ops = [a] + list(bs)
    specs = [pl.BlockSpec((tk, tm), lambda i, j, k: (k, i))] + [pl.BlockSpec((tk, tn), lambda i, j, k: (k, j))] * nb
    res = _mm(name, (M // tm, N // tn, L // tk), ("parallel", "parallel", "arbitrary"), 2, ops, specs,
              [(0, 1 + t, 'tn', t) for t in range(nb)], [(tm, tn)] * nb, [], [], lambda *acc: acc,
              [jax.ShapeDtypeStruct((M, N), out_dtype)] * nb,
              [pl.BlockSpec((tm, tn), lambda i, j, k: (i, j))] * nb, comm=comm)
    return _with_comm(res, comm, lambda o: o)


def _norm_fwd(x, w):
    L, D = x.shape
    tr = _tile(L, 512)

    def body(x_ref, w_ref, y_ref):
        v = x_ref[...]
        r = lax.rsqrt(jnp.mean(v * v, axis=-1, keepdims=True) + EPS)
        y_ref[...] = (v * r * w_ref[...]).astype(y_ref.dtype)

    return pl.pallas_call(
        body, name="norm_fwd", grid=(L // tr,), out_shape=jax.ShapeDtypeStruct((L, D), BF16),
        in_specs=[pl.BlockSpec((tr, D), lambda i: (i, 0)), pl.BlockSpec((1, D), lambda i: (0, 0))],
        out_specs=pl.BlockSpec((tr, D), lambda i: (i, 0)), compiler_params=_params(("parallel",)),
    )(x, w)


def _norm_bwd_math(x, w, dy):
    r = lax.rsqrt(jnp.mean(x * x, axis=-1, keepdims=True) + EPS)
    gy = dy * w
    dx = r * (gy - x * (r * r) * jnp.mean(gy * x, axis=-1, keepdims=True))
    dw = jnp.sum(dy * x * r, axis=0, keepdims=True)
    return dx, dw


def _norm_bwd(x, w, dy, res, scale, out_dtype):
    L, D = x.shape
    tr = _tile(L, 384)
    has_res = res is not None

    def body(*refs):
        x_ref, w_ref, dy_ref = refs[:3]
        res_ref = refs[3] if has_res else None
        dx_ref, dw_ref = refs[-2:]
        dx, dw = _norm_bwd_math(x_ref[...], w_ref[...], dy_ref[...].astype(F32))
        dx = scale * dx
        if has_res:
            dx = dx + res_ref[...]
        dx_ref[...] = dx.astype(dx_ref.dtype)

        @pl.when(pl.program_id(0) == 0)
        def _():
            dw_ref[...] = jnp.zeros_like(dw_ref)

        dw_ref[...] += scale * dw

    row = pl.BlockSpec((tr, D), lambda i: (i, 0))
    vec = pl.BlockSpec((1, D), lambda i: (0, 0))
    return pl.pallas_call(
        body, name="norm_bwd", grid=(L // tr,),
        out_shape=[jax.ShapeDtypeStruct((L, D), out_dtype), jax.ShapeDtypeStruct((1, D), F32)],
        in_specs=[row, vec, row] + ([row] if has_res else []), out_specs=[row, vec],
        compiler_params=_params(("arbitrary",)),
    )(*([x, w, dy] + ([res] if has_res else [])))


def _loss(h, target):
    L, D = h.shape

    def body(h_ref, t_ref, dh_ref, loss_ref):
        i = pl.program_id(0)

        @pl.when(i == 0)
        def _():
            dh_ref[...] = jnp.zeros_like(dh_ref)
            loss_ref[...] = jnp.zeros_like(loss_ref)

        @pl.when(i > 0)
        def _():
            diff = h_ref[...] - t_ref[...]
            dh_ref[...] = diff * (1.0 / D)
            loss_ref[...] += 0.5 * jnp.sum(diff * diff) * (1.0 / D)

    return pl.pallas_call(
        body, name="loss", grid=(L // BLK,),
        out_shape=[jax.ShapeDtypeStruct((L, D), F32), jax.ShapeDtypeStruct((8, 128), F32)],
        in_specs=[pl.BlockSpec((BLK, D), lambda i: (i, 0)),
                  pl.BlockSpec((BLK, D), lambda i: (jnp.maximum(i - 1, 0), 0))],
        out_specs=[pl.BlockSpec((BLK, D), lambda i: (i, 0)), pl.BlockSpec((8, 128), lambda i: (0, 0))],
        compiler_params=_params(("arbitrary",)),
    )(h, target)


def _ffn_up(a, wg, wu, comm=None):
    L, D = a.shape
    F = wg.shape[1]
    tm = _tile(L, 704)

    def ep(g, u):
        return g, u, g * _sigmoid(g) * u

    hspec = pl.BlockSpec((None, tm, F), lambda i, j: (j, i, 0))
    wspec = pl.BlockSpec((None, F, D), lambda i, j: (j, 0, 0))
    res = _mm("ffn_up", (L // tm, N_DEV), ("parallel", "parallel"), None,
              [a, wg, wu], [pl.BlockSpec((tm, D), lambda i, j: (i, 0)), wspec, wspec],
              [(0, 1, 'nt', 0), (0, 2, 'nt', 1)], [(tm, F)] * 2, [], [], ep,
              [jax.ShapeDtypeStruct((N_DEV, L, F), BF16)] * 3, [hspec] * 3, comm=comm)
    return _with_comm(res, comm, lambda o: o)


def _ffn_gate(a, wg, comm=None):
    L, D = a.shape
    F = wg.shape[1]
    tm = _tile(L, 704)
    res = _mm("ffn_gate", (L // tm, N_DEV), ("parallel", "parallel"), None,
              [a, wg], [pl.BlockSpec((tm, D), lambda i, j: (i, 0)), pl.BlockSpec((None, F, D), lambda i, j: (j, 0, 0))],
              [(0, 1, 'nt', 0)], [(tm, F)], [], [], lambda g: (g,),
              [jax.ShapeDtypeStruct((N_DEV, L, F), BF16)], [pl.BlockSpec((None, tm, F), lambda i, j: (j, i, 0))],
              comm=comm)
    return _with_comm(res, comm, lambda o: o[0])


def _ffn_up_gated(a, wu, g, comm=None):
    L, D = a.shape
    F = wu.shape[1]
    tm = _tile(L, 704)

    def ep(u, g_):
        g32 = g_.astype(F32)
        return u, g32 * _sigmoid(g32) * u

    hspec = pl.BlockSpec((None, tm, F), lambda i, j: (j, i, 0))
    res = _mm("ffn_up_gated", (L // tm, N_DEV), ("parallel", "parallel"), None,
              [a, wu], [pl.BlockSpec((tm, D), lambda i, j: (i, 0)), pl.BlockSpec((None, F, D), lambda i, j: (j, 0, 0))],
              [(0, 1, 'nt', 0)], [(tm, F)], [g], [hspec], ep,
              [jax.ShapeDtypeStruct((N_DEV, L, F), BF16)] * 2, [hspec, hspec], comm=comm)
    return _with_comm(res, comm, lambda o: o)


def _resnorm_epilogue(scale, with_next):
    def ep(acc, h, w, *w_next):
        r = lax.rsqrt(jnp.mean(acc * acc, axis=-1, keepdims=True) + EPS)
        h_out = h + scale * (acc * r * w)
        if not with_next:
            return acc, h_out
        r_next = lax.rsqrt(jnp.mean(h_out * h_out, axis=-1, keepdims=True) + EPS)
        return acc, h_out, h_out * r_next * w_next[0]
    return ep


def _ffn_down(hid, wd, h_in, post, next_norm=None, comm=None):
    _, L, F = hid.shape
    D = wd.shape[2]
    tm = _tile(L, 528)
    row = pl.BlockSpec((tm, D), lambda i, j: (i, 0))
    vec = pl.BlockSpec((1, D), lambda i, j: (0, 0))
    nxt = [] if next_norm is None else [next_norm]
    res = _mm("ffn_down", (L // tm, N_DEV), ("parallel", "arbitrary"), 1,
              [hid, wd], [pl.BlockSpec((None, tm, F), lambda i, j: (j, i, 0)),
                          pl.BlockSpec((None, F, D), lambda i, j: (j, 0, 0))],
              [(0, 1, 'nn', 0)], [(tm, D)], [h_in, post] + nxt, [row, vec] + [vec] * len(nxt),
              _resnorm_epilogue(0.5, bool(nxt)),
              [jax.ShapeDtypeStruct((L, D), F32)] * 2 + [jax.ShapeDtypeStruct((L, D), BF16)] * len(nxt),
              [row] * (2 + len(nxt)), comm=comm)
    return _with_comm(res, comm, lambda o: o)


def _ffn_dhid(df, wd, g, u, comm=None):
    L, D = df.shape
    F = wd.shape[1]
    tm = _tile(L, 704)

    def ep(dhid, g_, u_):
        g32, u32 = g_.astype(F32), u_.astype(F32)
        sg = _sigmoid(g32)
        return dhid * u32 * sg * (1.0 + g32 * (1.0 - sg)), dhid * g32 * sg

    hspec = pl.BlockSpec((None, tm, F), lambda i, j: (j, i, 0))
    res = _mm("ffn_dhid", (L // tm, N_DEV), ("parallel", "parallel"), None,
              [df, wd], [pl.BlockSpec((tm, D), lambda i, j: (i, 0)),
                         pl.BlockSpec((None, F, D), lambda i, j: (j, 0, 0))],
              [(0, 1, 'nt', 0)], [(tm, F)], [g, u], [hspec, hspec], ep,
              [jax.ShapeDtypeStruct((N_DEV, L, F), BF16)] * 2, [hspec, hspec], comm=comm)
    return _with_comm(res, comm, lambda o: o)


def _ffn_dwd(hid, df, comm=None):
    _, L, F = hid.shape
    D = df.shape[1]
    tk = _tile(L, 1408)
    res = _mm("ffn_dwd", (N_DEV, L // tk), ("parallel", "arbitrary"), 1,
              [hid, df], [pl.BlockSpec((None, tk, F), lambda j, k: (j, k, 0)),
                          pl.BlockSpec((tk, D), lambda j, k: (k, 0))],
              [(0, 1, 'tn', 0)], [(F, D)], [], [], lambda acc: (acc,),
              [jax.ShapeDtypeStruct((N_DEV, F, D), BF16)], [pl.BlockSpec((None, F, D), lambda j, k: (j, 0, 0))],
              comm=comm)
    return _with_comm(res, comm, lambda o: o[0])


def _ffn_dwgu(a, dg, du, comm=None):
    L, D = a.shape
    F = dg.shape[2]
    tk = _tile(L, 1408)
    hspec = pl.BlockSpec((None, tk, F), lambda j, k: (j, k, 0))
    wspec = pl.BlockSpec((None, F, D), lambda j, k: (j, 0, 0))
    res = _mm("ffn_dwgu", (N_DEV, L // tk), ("parallel", "arbitrary"), 1,
              [a, dg, du], [pl.BlockSpec((tk, D), lambda j, k: (k, 0)), hspec, hspec],
              [(1, 0, 'tn', 0), (2, 0, 'tn', 1)], [(F, D)] * 2, [], [], lambda *acc: acc,
              [jax.ShapeDtypeStruct((N_DEV, F, D), BF16)] * 2, [wspec, wspec], comm=comm)
    return _with_comm(res, comm, lambda o: o)


def _ffn_da(dg, du, wg, wu, comm=None):
    _, L, F = dg.shape
    D = wg.shape[2]
    tm = _tile(L, 704)
    hspec = pl.BlockSpec((None, tm, F), lambda i, j: (j, i, 0))
    wspec = pl.BlockSpec((None, F, D), lambda i, j: (j, 0, 0))
    row = pl.BlockSpec((tm, D), lambda i, j: (i, 0))
    res = _mm("ffn_da", (L // tm, N_DEV), ("parallel", "arbitrary"), 1,
              [dg, du, wg, wu], [hspec, hspec, wspec, wspec],
              [(0, 2, 'nn', 0), (1, 3, 'nn', 0)], [(tm, D)], [], [], lambda acc: (acc,),
              [jax.ShapeDtypeStruct((L, D), F32)], [row], comm=comm)
    return _with_comm(res, comm, lambda o: o[0])


def _rope_tables(L):
    rows = jnp.arange(L, dtype=F32)
    pos = jnp.where(rows < BLK, rows, rows - (BLK - N_META))
    inv_r = ROPE_THETA ** (-jnp.arange(0, HD, 2, dtype=F32) / HD)
    ang_r = pos[:, None] * inv_r[None, :]
    cr = jnp.concatenate([jnp.cos(ang_r), jnp.cos(ang_r)], axis=1)
    sr = jnp.concatenate([-jnp.sin(ang_r), jnp.sin(ang_r)], axis=1)
    inv_m = ROPE_THETA ** (-jnp.arange(0, ROPE, 2, dtype=F32) / ROPE)
    ang_m = pos[:, None] * inv_m[None, :]
    z32 = jnp.zeros((L, ROPE // 2), F32)
    z64 = jnp.zeros((L, HD - ROPE), F32)
    cm = jnp.concatenate([jnp.cos(ang_m), jnp.cos(ang_m), z64], axis=1)
    sa = jnp.concatenate([-jnp.sin(ang_m), z32, z64], axis=1)
    sb = jnp.concatenate([z32, jnp.sin(ang_m), z64], axis=1)
    return cr, sr, cm, sa, sb


def _rope_ret(x, cr, sr):
    return x * cr + pltpu.roll(x, HD // 2, 1) * sr


def _rope_ret_t(d, cr, sr):
    return d * cr + pltpu.roll(d * sr, HD // 2, 1)


def _rope_mla(x, cm, sa, sb):
    return x * cm + pltpu.roll(x, HD - ROPE // 2, 1) * sa + pltpu.roll(x, ROPE // 2, 1) * sb


def _rope_mla_t(d, cm, sa, sb):
    return d * cm + pltpu.roll(d * sa, ROPE // 2, 1) + pltpu.roll(d * sb, HD - ROPE // 2, 1)


C_RQ, C_RK, C_RV, C_RG = 0, HEADS * HD, 2 * HEADS * HD, 3 * HEADS * HD
C_CQ = 4 * HEADS * HD
C_CKV = C_CQ + Q_RANK
C_KR = C_CKV + KV_RANK
RET_K_SCALE = HD ** -0.5


def _prep(proj, tabs, qn, kvn):
    L = proj.shape[0]
    tr = _tile(L, 256)
    W = HEADS * HD

    def body(p_ref, cr_ref, sr_ref, cm_ref, sa_ref, sb_ref, qn_ref, kvn_ref, q_ref, k_ref, v_ref, cq_ref, ckv_ref,
             kr_ref):
        cr, sr = cr_ref[...], sr_ref[...]
        for h in range(HEADS):
            sl = slice(h * HD, (h + 1) * HD)
            q_ref[:, sl] = _rope_ret(p_ref[:, C_RQ + h * HD:C_RQ + (h + 1) * HD].astype(F32), cr, sr).astype(BF16)
            k_ref[:, sl] = (_rope_ret(p_ref[:, C_RK + h * HD:C_RK + (h + 1) * HD].astype(F32), cr, sr)
                            * RET_K_SCALE).astype(BF16)
        v_ref[...] = p_ref[:, C_RV:C_RV + W].astype(BF16)
        cq = p_ref[:, C_CQ:C_CQ + Q_RANK].astype(F32)
        cq_ref[...] = (cq * lax.rsqrt(jnp.mean(cq * cq, axis=-1, keepdims=True) + EPS) * qn_ref[...]).astype(BF16)
        ckv = p_ref[:, C_CKV:C_CKV + KV_RANK].astype(F32)
        ckv_ref[...] = (ckv * lax.rsqrt(jnp.mean(ckv * ckv, axis=-1, keepdims=True) + EPS)
                        * kvn_ref[...]).astype(BF16)
        kr_ref[...] = _rope_mla(p_ref[:, C_KR:C_KR + HD].astype(F32), cm_ref[...], sa_ref[...], sb_ref[...]).astype(BF16)

    row = lambda w: pl.BlockSpec((tr, w), lambda i: (i, 0))
    vec = lambda w: pl.BlockSpec((1, w), lambda i: (0, 0))
    return pl.pallas_call(
        body, name="mix_prep", grid=(L // tr,),
        out_shape=[jax.ShapeDtypeStruct((L, W), BF16)] * 3 + [jax.ShapeDtypeStruct((L, Q_RANK), BF16),
                                                              jax.ShapeDtypeStruct((L, KV_RANK), BF16),
                                                              jax.ShapeDtypeStruct((L, HD), BF16)],
        in_specs=[row(D_INP)] + [row(HD)] * 5 + [vec(Q_RANK), vec(KV_RANK)],
        out_specs=[row(W)] * 3 + [row(Q_RANK), row(KV_RANK), row(HD)],
        compiler_params=_params(("parallel",)),
    )(proj, *tabs, qn, kvn)


def _prep_bwd(proj, dq, dk, dv, drg, dcqn, dckvn, dkr8, tabs, qn, kvn):
    L = proj.shape[0]
    tr = _tile(L, 192)
    W = HEADS * HD

    def body(p_ref, dq_ref, dk_ref, dv_ref, drg_ref, dcq_ref, dckv_ref, dkr_ref, cr_ref, sr_ref, cm_ref, sa_ref,
             sb_ref, qn_ref, kvn_ref, dp_ref, dqn_ref, dkvn_ref):
        cr, sr = cr_ref[...], sr_ref[...]
        dkr = None
        for h in range(HEADS):
            sl = slice(h * HD, (h + 1) * HD)
            dp_ref[:, C_RQ + h * HD:C_RQ + (h + 1) * HD] = _rope_ret_t(dq_ref[:, sl].astype(F32), cr, sr).astype(BF16)
            dp_ref[:, C_RK + h * HD:C_RK + (h + 1) * HD] = (_rope_ret_t(dk_ref[:, sl].astype(F32), cr, sr)
                                                            * RET_K_SCALE).astype(BF16)
            part = dkr_ref[:, sl].astype(F32)
            dkr = part if dkr is None else dkr + part
        dp_ref[:, C_RV:C_RV + W] = dv_ref[...].astype(BF16)
        dp_ref[:, C_RG:C_RG + W] = drg_ref[...].astype(BF16)
        dcq, dqn = _norm_bwd_math(p_ref[:, C_CQ:C_CQ + Q_RANK].astype(F32), qn_ref[...], dcq_ref[...])
        dp_ref[:, C_CQ:C_CQ + Q_RANK] = dcq.astype(BF16)
        dckv, dkvn = _norm_bwd_math(p_ref[:, C_CKV:C_CKV + KV_RANK].astype(F32), kvn_ref[...], dckv_ref[...])
        dp_ref[:, C_CKV:C_CKV + KV_RANK] = dckv.astype(BF16)
        dp_ref[:, C_KR:C_KR + HD] = _rope_mla_t(dkr, cm_ref[...], sa_ref[...], sb_ref[...]).astype(BF16)

        @pl.when(pl.program_id(0) == 0)
        def _():
            dqn_ref[...] = jnp.zeros_like(dqn_ref)
            dkvn_ref[...] = jnp.zeros_like(dkvn_ref)

        dqn_ref[...] += dqn
        dkvn_ref[...] += dkvn

    row = lambda w: pl.BlockSpec((tr, w), lambda i: (i, 0))
    vec = lambda w: pl.BlockSpec((1, w), lambda i: (0, 0))
    return pl.pallas_call(
        body, name="mix_prep_bwd", grid=(L // tr,),
        out_shape=[jax.ShapeDtypeStruct((L, D_INP), BF16), jax.ShapeDtypeStruct((1, Q_RANK), F32),
                   jax.ShapeDtypeStruct((1, KV_RANK), F32)],
        in_specs=[row(D_INP)] + [row(W)] * 4 + [row(Q_RANK), row(KV_RANK), row(W)] + [row(HD)] * 5
                 + [vec(Q_RANK), vec(KV_RANK)],
        out_specs=[row(D_INP), vec(Q_RANK), vec(KV_RANK)],
        compiler_params=_params(("arbitrary",)),
    )(proj, dq, dk, dv, drg, dcqn, dckvn, dkr8, *tabs, qn, kvn)


def _post(o_ret, proj, gn):
    L, W = o_ret.shape
    tr = _tile(L, 384)

    def body(o_ref, rg_ref, gn_ref, out_ref):
        for h in range(HEADS):
            sl = slice(h * HD, (h + 1) * HD)
            o = o_ref[:, sl]
            rg = rg_ref[:, sl].astype(F32)
            n = o * lax.rsqrt(jnp.mean(o * o, axis=-1, keepdims=True) + EPS)
            out_ref[:, sl] = (n * gn_ref[:, sl] * (rg * _sigmoid(rg))).astype(BF16)

    row = pl.BlockSpec((tr, W), lambda i: (i, 0))
    return pl.pallas_call(
        body, name="ret_post", grid=(L // tr,), out_shape=jax.ShapeDtypeStruct((L, W), BF16),
        in_specs=[row, pl.BlockSpec((tr, W), lambda i: (i, C_RG // W)), pl.BlockSpec((1, W), lambda i: (0, 0))],
        out_specs=row, compiler_params=_params(("parallel",)),
    )(o_ret, proj, gn)


def _post_bwd(o_ret, proj, gn, dcat):
    L, W = o_ret.shape
    tr = _tile(L, 384)

    def body(o_ref, rg_ref, gn_ref, d_ref, do_ref, drg_ref, dgn_ref):
        @pl.when(pl.program_id(0) == 0)
        def _():
            dgn_ref[...] = jnp.zeros_like(dgn_ref)

        for h in range(HEADS):
            sl = slice(h * HD, (h + 1) * HD)
            o = o_ref[:, sl]
            rg = rg_ref[:, sl].astype(F32)
            d = d_ref[:, sl].astype(F32)
            gw = gn_ref[:, sl]
            r = lax.rsqrt(jnp.mean(o * o, axis=-1, keepdims=True) + EPS)
            n = o * r
            sg = _sigmoid(rg)
            si = rg * sg
            dn = d * gw * si
            dgn_ref[:, sl] += jnp.sum(d * n * si, axis=0, keepdims=True)
            drg_ref[:, sl] = (d * n * gw * sg * (1.0 + rg * (1.0 - sg))).astype(drg_ref.dtype)
            do_ref[:, sl] = (r * (dn - o * (r * r) * jnp.mean(dn * o, axis=-1, keepdims=True))).astype(BF16)

    row = pl.BlockSpec((tr, W), lambda i: (i, 0))
    vec = pl.BlockSpec((1, W), lambda i: (0, 0))
    return pl.pallas_call(
        body, name="ret_post_bwd", grid=(L // tr,),
        out_shape=[jax.ShapeDtypeStruct((L, W), BF16), jax.ShapeDtypeStruct((L, W), BF16),
                   jax.ShapeDtypeStruct((1, W), F32)],
        in_specs=[row, pl.BlockSpec((tr, W), lambda i: (i, C_RG // W)), vec, row],
        out_specs=[row, row, vec], compiler_params=_params(("arbitrary",)),
    )(o_ret, proj, gn, dcat)


RET_HEADS_PER_STEP = 4


def _lin_attn(name, q, k, v, lg, reverse, out_dtype=F32):
    L, W = q.shape
    nc = L // BLK - 1
    G = RET_HEADS_PER_STEP

    def body(q_ref, k_ref, v_ref, lg_ref, o_ref, s_ref):
        n = lax.broadcasted_iota(jnp.int32, (BLK, BLK), 0).astype(F32)
        m = lax.broadcasted_iota(jnp.int32, (BLK, BLK), 1).astype(F32)
        dist = (m - n) if reverse else (n - m)
        consts = []
        for g in range(G):
            lgv = lg_ref[g, 0:1, :]
            dmask = jnp.where(dist >= 0, jnp.exp(lgv * jnp.maximum(dist, 0.0)), 0.0)
            c = dict(dmask=dmask, dmask0=jnp.where((n < N_META) & (m < N_META), dmask, 0.0),
                     gl=jnp.exp(lgv * float(BLK)))
            if reverse:
                c.update(inter=jnp.exp(lgv * (float(BLK) - n)), upd=jnp.exp(lgv * n),
                         inter0=jnp.where(n < N_META, jnp.exp(lgv * jnp.maximum(float(N_META) - n, 0.0)), 0.0))
            else:
                c.update(inter=jnp.exp(lgv * (n + 1.0)), upd=jnp.exp(lgv * (float(BLK) - 1.0 - n)),
                         upd0=jnp.where(n < N_META, jnp.exp(lgv * jnp.maximum(float(N_META) - 1.0 - n, 0.0)), 0.0))
            consts.append(c)

        def chunk(c):
            rows = pl.ds(pl.multiple_of(c * BLK, BLK), BLK)
            state = [s_ref[g] for g in range(G)]
            outs, new_state = [], []
            for g in range(G):
                cols = slice(g * HD, (g + 1) * HD)
                cg = consts[g]
                qc, kc, vc = q_ref[rows, cols], k_ref[rows, cols], v_ref[rows, cols]
                a = _dot(qc, kc, 'nt') * cg['dmask']
                outs.append(_dot(a.astype(BF16), vc, 'nn') + _dot(qc, state[g].astype(BF16), 'nn') * cg['inter'])
                new_state.append(state[g] * cg['gl'] + _dot((kc.astype(F32) * cg['upd']).astype(BF16), vc, 'tn'))
            for g in range(G):
                o_ref[rows, g * HD:(g + 1) * HD] = outs[g].astype(o_ref.dtype)
                s_ref[g] = new_state[g]

        def first_chunk(with_state):
            for g in range(G):
                cols = slice(g * HD, (g + 1) * HD)
                cg = consts[g]
                q0, k0, v0 = q_ref[0:BLK, cols], k_ref[0:BLK, cols], v_ref[0:BLK, cols]
                o0 = _dot((_dot(q0, k0, 'nt') * cg['dmask0']).astype(BF16), v0, 'nn')
                if with_state:
                    o0 = o0 + _dot(q0, s_ref[g].astype(BF16), 'nn') * cg['inter0']
                else:
                    s_ref[g] = _dot((k0.astype(F32) * cg['upd0']).astype(BF16), v0, 'tn')
                o_ref[0:BLK, cols] = o0.astype(o_ref.dtype)

        if reverse:
            s_ref[...] = jnp.zeros_like(s_ref)

            def step(t, carry):
                chunk(nc - t)
                return carry

            lax.fori_loop(0, nc, step, 0)
            first_chunk(True)
        else:
            first_chunk(False)

            def step(t, carry):
                chunk(t + 1)
                return carry

            lax.fori_loop(0, nc, step, 0)

    col = pl.BlockSpec((L, G * HD), lambda h: (0, h))
    return pl.pallas_call(
        body, name=name, grid=(HEADS // G,), out_shape=jax.ShapeDtypeStruct((L, W), out_dtype),
        in_specs=[col, col, col, pl.BlockSpec((G, 8, HD), lambda h: (h, 0, 0))], out_specs=col,
        scratch_shapes=[pltpu.VMEM((G, HD, HD), F32)], compiler_params=_params(("parallel",)),
    )(q, k, v, lg)


ATT_SCALE = (HD + ROPE) ** -0.5
LOG2E = 1.4426950408889634
Q_PRESCALE = ATT_SCALE * LOG2E
NEG = -1e30


ATT_TILE = 384
ATT_HEADS_PER_STEP = 8
ATT_BWD_HEADS_PER_STEP = 4


def _att_valid(nq, nk, row0, col0):
    r = lax.broadcasted_iota(jnp.int32, (nq, nk), 0) + row0
    c = lax.broadcasted_iota(jnp.int32, (nq, nk), 1) + col0
    return (c <= r) & ((c < N_META) | (c >= BLK))


def _store_rows(ref, g, first, col):
    wide = jnp.broadcast_to(col, (col.shape[0], HD))
    for c in range(col.shape[0] // BLK):
        ref[g, first + c] = jnp.transpose(wide[c * BLK:(c + 1) * BLK, :])[0:8, :]


def _load_row(ref, g, first, n):
    return jnp.concatenate([ref[g, first + c, 0:1, :] for c in range(n)], axis=1)


def _attn_fwd(qm, kn, krr, vm, comm=None):
    L = qm.shape[0]
    W = HEADS * HD
    T = _tile(L, ATT_TILE, BLK)
    nb = L // T
    G = ATT_HEADS_PER_STEP
    n_cm = comm.n if comm is not None else 0

    def body(*refs):
        q_ref, kn_ref, kr_ref, v_ref = refs[:4]
        o_ref, lse_ref = refs[4 + n_cm:6 + n_cm]
        m_sc, l_sc, acc_sc = refs[6 + 2 * n_cm:9 + 2 * n_cm]
        if comm is not None:
            cm_refs = (refs[4:4 + n_cm], refs[6 + n_cm:6 + 2 * n_cm], refs[9 + 2 * n_cm:])
            first, last = _grid_edges((HEADS // G, nb))

            @pl.when(first)
            def _():
                comm.start(*cm_refs)

        i = pl.program_id(1)
        m_sc[...] = jnp.full_like(m_sc, NEG)
        l_sc[...] = jnp.zeros_like(l_sc)
        acc_sc[...] = jnp.zeros_like(acc_sc)

        def tile(j, masked):
            rows = pl.ds(pl.multiple_of(j * T, T), T)
            kr = kr_ref[rows, :]
            valid = _att_valid(T, T, i * T, j * T) if masked else None
            ones = jnp.ones((T, HD), BF16)
            m_prev = [m_sc[g] for g in range(G)]
            l_prev = [l_sc[g] for g in range(G)]
            acc_prev = [acc_sc[g] for g in range(G)]
            m_new, l_new, acc_new = [], [], []
            for g in range(G):
                k = jnp.concatenate([kn_ref[rows, g * HD:(g + 1) * HD], kr], axis=1)
                s = _dot(q_ref[:, g * QH:(g + 1) * QH], k, 'nt')
                if masked:
                    s = jnp.where(valid, s, NEG)
                m_new.append(jnp.maximum(m_prev[g], jnp.max(s, axis=-1, keepdims=True)))
                p = jnp.exp2(s - m_new[g])
                alpha = jnp.exp2(m_prev[g] - m_new[g])
                pv = _dot(p.astype(BF16), jnp.concatenate([v_ref[rows, g * HD:(g + 1) * HD], ones], axis=1), 'nn')
                l_new.append(alpha * l_prev[g] + pv[:, HD:HD + 1])
                acc_new.append(alpha * acc_prev[g] + pv[:, 0:HD])
            for g in range(G):
                m_sc[g] = m_new[g]
                l_sc[g] = l_new[g]
                acc_sc[g] = acc_new[g]

        tile(0, True)

        def mid(j, carry):
            tile(j, False)
            return carry

        lax.fori_loop(1, i, mid, 0)

        @pl.when(i > 0)
        def _():
            tile(i, True)

        for g in range(G):
            l = l_sc[g]
            o_ref[:, g * HD:(g + 1) * HD] = (acc_sc[g] / l).astype(o_ref.dtype)
            _store_rows(lse_ref, g, 0, m_sc[g] + jnp.log(l) * LOG2E)

        if comm is not None:
            @pl.when(last)
            def _():
                comm.finish(*cm_refs)

    cm_specs = comm.specs if comm is not None else []
    res = pl.pallas_call(
        body, name="attn_fwd", grid=(HEADS // G, nb),
        out_shape=[jax.ShapeDtypeStruct((L, W), BF16), jax.ShapeDtypeStruct((HEADS, L // BLK, 8, HD), F32)]
        + (comm.out_shapes if comm is not None else []),
        in_specs=[pl.BlockSpec((T, G * QH), lambda h, i: (i, h)), pl.BlockSpec((L, G * HD), lambda h, i: (0, h)),
                  pl.BlockSpec((L, HD), lambda h, i: (0, 0)), pl.BlockSpec((L, G * HD), lambda h, i: (0, h))]
        + cm_specs,
        out_specs=[pl.BlockSpec((T, G * HD), lambda h, i: (i, h)),
                   pl.BlockSpec((G, T // BLK, 8, HD), lambda h, i: (h, i, 0, 0))] + cm_specs,
        scratch_shapes=[pltpu.VMEM((G, T, 1), F32), pltpu.VMEM((G, T, 1), F32), pltpu.VMEM((G, T, HD), F32)]
        + (comm.scratch if comm is not None else []),
        compiler_params=_params(("arbitrary", "arbitrary")),
    )(qm, kn, krr, vm, *(comm.arrays if comm is not None else []))
    return res[:2], res[2:]


def _attn_bwd(qm, kn, krr, vm, o, dcat, lse, comm=None):
    L = qm.shape[0]
    W = HEADS * HD
    T = _tile(L, ATT_TILE, BLK)
    nb, nr = L // T, T // BLK
    G = ATT_BWD_HEADS_PER_STEP
    n_cm = comm.n if comm is not None else 0

    def body(*refs):
        q_ref, kn_ref, kr_ref, v_ref, o_ref, do_ref, lse_ref = refs[:7]
        dq_ref, dkn_ref, dkr_ref, dv_ref = refs[7 + n_cm:11 + n_cm]
        dl_sc, dk_sc, dv_sc = refs[11 + 2 * n_cm:14 + 2 * n_cm]
        if comm is not None:
            cm_refs = (refs[7:7 + n_cm], refs[11 + n_cm:11 + 2 * n_cm], refs[14 + 2 * n_cm:])
            first, last = _grid_edges((HEADS // G, nb))

            @pl.when(first)
            def _():
                comm.start(*cm_refs)

        j = pl.program_id(1)
        qs = lambda g: slice(g * QH, (g + 1) * QH)
        hs = lambda g: slice(g * HD, (g + 1) * HD)

        @pl.when(j == 0)
        def _():
            dq_ref[...] = jnp.zeros_like(dq_ref)

            def rowsum(t, carry):
                rows = pl.ds(pl.multiple_of(t * T, T), T)
                for g in range(G):
                    _store_rows(dl_sc, g, t * nr, jnp.sum(
                        do_ref[rows, hs(g)].astype(F32) * o_ref[rows, hs(g)].astype(F32), axis=-1, keepdims=True))
                return carry

            lax.fori_loop(0, nb, rowsum, 0)

        kr = kr_ref[...]
        ks = [jnp.concatenate([kn_ref[:, hs(g)], kr], axis=1) for g in range(G)]
        vs = [v_ref[:, hs(g)] for g in range(G)]
        dk_sc[...] = jnp.zeros_like(dk_sc)
        dv_sc[...] = jnp.zeros_like(dv_sc)

        def tile(i, masked):
            rows = pl.ds(pl.multiple_of(i * T, T), T)
            if masked:
                key = lax.broadcasted_iota(jnp.int32, (T, T), 0) + j * T
                qry = lax.broadcasted_iota(jnp.int32, (T, T), 1) + i * T
                valid = (key <= qry) & ((key < N_META) | (key >= BLK))
            for g in range(G):
                q = q_ref[rows, qs(g)]
                do = do_ref[rows, hs(g)]
                s = _dot(ks[g], q, 'nt')
                if masked:
                    s = jnp.where(valid, s, NEG)
                p = jnp.exp2(s - _load_row(lse_ref, g, i * nr, nr))
                dv_sc[g] += _dot(p.astype(BF16), do, 'nn')
                ds = (p * (_dot(vs[g], do, 'nt') - _load_row(dl_sc, g, i * nr, nr))).astype(BF16)
                dk_sc[g] += _dot(ds, q, 'nn')
                dq_ref[rows, qs(g)] += _dot(ds, ks[g], 'tn')

        tile(j, True)

        def rest(masked):
            def step(i, carry):
                tile(i, masked)
                return carry
            lax.fori_loop(j + 1, nb, step, 0)

        @pl.when(j == 0)
        def _():
            rest(True)

        @pl.when(j > 0)
        def _():
            rest(False)

        for g in range(G):
            dk = dk_sc[g] * (1.0 / LOG2E)
            dkn_ref[:, hs(g)] = dk[:, 0:HD].astype(BF16)
            dkr_ref[:, hs(g)] = dk[:, HD:QH].astype(dkr_ref.dtype)
            dv_ref[:, hs(g)] = dv_sc[g].astype(BF16)

        if comm is not None:
            @pl.when(last)
            def _():
                comm.finish(*cm_refs)

    blk = pl.BlockSpec((T, G * HD), lambda h, j: (j, h))
    once = pl.Buffered(1)
    cm_specs = comm.specs if comm is not None else []
    res = pl.pallas_call(
        body, name="attn_bwd", grid=(HEADS // G, nb),
        out_shape=[jax.ShapeDtypeStruct((L, HEADS * QH), F32), jax.ShapeDtypeStruct((L, W), BF16),
                   jax.ShapeDtypeStruct((L, W), BF16), jax.ShapeDtypeStruct((L, W), BF16)]
        + (comm.out_shapes if comm is not None else []),
        in_specs=[pl.BlockSpec((L, G * QH), lambda h, j: (0, h), pipeline_mode=once), blk,
                  pl.BlockSpec((T, HD), lambda h, j: (j, 0)), blk,
                  pl.BlockSpec((L, G * HD), lambda h, j: (0, h), pipeline_mode=once),
                  pl.BlockSpec((L, G * HD), lambda h, j: (0, HEADS // G + h), pipeline_mode=once),
                  pl.BlockSpec((G, L // BLK, 8, HD), lambda h, j: (h, 0, 0, 0))] + cm_specs,
        out_specs=[pl.BlockSpec((L, G * QH), lambda h, j: (0, h), pipeline_mode=once), blk, blk, blk] + cm_specs,
        scratch_shapes=[pltpu.VMEM((G, L // BLK, 8, HD), F32), pltpu.VMEM((G, T, QH), F32),
                        pltpu.VMEM((G, T, HD), F32)]
        + (comm.scratch if comm is not None else []),
        compiler_params=_params(("arbitrary", "arbitrary")),
    )(qm, kn, krr, vm, o, dcat, lse, *(comm.arrays if comm is not None else []))
    return res[:4], res[4:]


def _unrope_q(dqm, tabs_m):
    L, W = dqm.shape
    tr = _tile(L, 384)

    def body(d_ref, cm_ref, sa_ref, sb_ref, out_ref):
        cm, sa, sb = cm_ref[...], sa_ref[...], sb_ref[...]
        for h in range(HEADS):
            out_ref[:, h * QH:h * QH + HD] = (d_ref[:, h * QH:h * QH + HD] * ATT_SCALE).astype(BF16)
            out_ref[:, h * QH + HD:(h + 1) * QH] = _rope_mla_t(d_ref[:, h * QH + HD:(h + 1) * QH] * ATT_SCALE, cm, sa,
                                                               sb).astype(BF16)

    row = pl.BlockSpec((tr, W), lambda i: (i, 0))
    tab = pl.BlockSpec((tr, HD), lambda i: (i, 0))
    return pl.pallas_call(
        body, name="unrope_q", grid=(L // tr,), out_shape=jax.ShapeDtypeStruct((L, W), BF16),
        in_specs=[row, tab, tab, tab], out_specs=row, compiler_params=_params(("parallel",)),
    )(dqm, *tabs_m)


def _q_up(cqn, wuq_p, tabs_m):
    L = cqn.shape[0]
    tm = _tile(L, 704)

    def ep(acc, cm, sa, sb):
        acc = acc * Q_PRESCALE
        parts = []
        for h in range(HEADS):
            parts.append(acc[:, h * QH:h * QH + HD])
            parts.append(_rope_mla(acc[:, h * QH + HD:(h + 1) * QH], cm, sa, sb))
        return (jnp.concatenate(parts, axis=1),)

    tab = pl.BlockSpec((tm, HD), lambda i, j: (i, 0))
    return _mm("mla_q_up", (L // tm, 1), ("parallel", "parallel"), None,
               [cqn, wuq_p], [pl.BlockSpec((tm, Q_RANK), lambda i, j: (i, 0)),
                              pl.BlockSpec((HEADS * QH, Q_RANK), lambda i, j: (0, 0))],
               [(0, 1, 'nt', 0)], [(tm, HEADS * QH)], list(tabs_m), [tab] * 3, ep,
               [jax.ShapeDtypeStruct((L, HEADS * QH), BF16)], [pl.BlockSpec((tm, HEADS * QH), lambda i, j: (i, 0))])[0]


def _mix_out(cat, w_out, h_in, post, next_norm):
    L, K = cat.shape
    D = w_out.shape[1]
    tm, tk = _tile(L, 384), K
    row = pl.BlockSpec((tm, D), lambda i, k: (i, 0))
    vec = pl.BlockSpec((1, D), lambda i, k: (0, 0))
    return _mm("mix_out", (L // tm, K // tk), ("parallel", "arbitrary"), 1,
               [cat, w_out], [pl.BlockSpec((tm, tk), lambda i, k: (i, k)), pl.BlockSpec((tk, D), lambda i, k: (k, 0))],
               [(0, 1, 'nn', 0)], [(tm, D)], [h_in, post, next_norm], [row, vec, vec], _resnorm_epilogue(1.0, True),
               [jax.ShapeDtypeStruct((L, D), F32)] * 2 + [jax.ShapeDtypeStruct((L, D), BF16)], [row, row, row])


ADAM_BLOCK_ELEMS = 512 * 704


def _adam_math(w, g, m, v):
    m = ADAM_B1 * m + (1.0 - ADAM_B1) * g
    v = ADAM_B2 * v + (1.0 - ADAM_B2) * (g * g)
    m_hat = m / (1.0 - ADAM_B1 ** ADAM_STEP)
    v_hat = v / (1.0 - ADAM_B2 ** ADAM_STEP)
    delta = -ADAM_LR * (m_hat / (jnp.sqrt(v_hat) + ADAM_EPS) + ADAM_WD * w)
    return delta, m, v


def _adam(name, w, m, v, g_slots=None, g=None, after=None):
    R, C = w.shape
    tr, tc = _tile(R, max(16, ADAM_BLOCK_ELEMS // C // 16 * 16), 16), C
    if tr * tc > ADAM_BLOCK_ELEMS:
        tr, tc = R, _tile(C, max(128, ADAM_BLOCK_ELEMS // R // 128 * 128), 128)
    from_slots = g_slots is not None

    def body(w_ref, m_ref, v_ref, g_ref, *rest):
        go_ref, d_ref, mo_ref, vo_ref = rest[-4:]
        if from_slots:
            grad = g_ref[0].astype(F32)
            for s in range(1, N_DEV):
                grad = grad + g_ref[s].astype(F32)
        else:
            grad = g_ref[...]
        delta, mn, vn = _adam_math(w_ref[...], grad, m_ref[...], v_ref[...])
        go_ref[...] = grad
        d_ref[...] = delta
        mo_ref[...] = mn
        vo_ref[...] = vn

    row = pl.BlockSpec((tr, tc), lambda i, j: (i, j))
    gspec = pl.BlockSpec((N_DEV, tr, tc), lambda i, j: (0, i, j)) if from_slots else row
    order = [] if after is None else [after]
    return pl.pallas_call(
        body, name=name, grid=(R // tr, C // tc), out_shape=[jax.ShapeDtypeStruct((R, C), F32)] * 4,
        in_specs=[row, row, row, gspec] + [pl.BlockSpec(memory_space=pl.ANY)] * len(order), out_specs=[row] * 4,
        compiler_params=_params(("parallel", "parallel")),
    )(w, m, v, g_slots if from_slots else g, *order)


def _unblock(gathered):
    n, r, c = gathered.shape
    return jnp.transpose(gathered, (1, 0, 2)).reshape(r, n * c)


def _reblock(full, c):
    r = full.shape[0]
    return jnp.transpose(full[:, :N_DEV * c].reshape(r, N_DEV, c), (1, 0, 2))


def _step(x, target, w, mom, vel):
    S, D = x.shape[1], x.shape[2]
    L = S + BLK
    def sq(a, n):
        if a.ndim == 2:
            return a
        if n in TRANSPOSED:
            a = jnp.swapaxes(a, 1, 2)
        return a.reshape(a.shape[1:])

    def unsq(o, n):
        o = o.reshape((1,) + o.shape)
        return jnp.swapaxes(o, 1, 2) if n in TRANSPOSED else o

    p = {n: sq(w[n], n) for n in WEIGHTS if n != 'meta_tokens'}
    gather = lambda names: _Exchange([p[n].astype(BF16) for n in names], False)
    scatter = lambda blocks: _Exchange(blocks, True)
    in_s, uq_s = p['w_in'].shape[0], p['mla_w_uq'].shape[0]
    assert uq_s == HD + ROPE and N_DEV == HEADS, "a w_uq shard is one head's columns"
    tabs = _rope_tables(L)
    tabs_m = tabs[2:]
    lg = jnp.broadcast_to(jnp.log(1.0 - 2.0 ** (-5.0 - jnp.arange(HEADS, dtype=F32)))[:, None, None], (HEADS, 8, HD))
    R = {}

    wg1, meta = _exchange("gather_first", [p['ffn1_w_gate'].astype(BF16), w['meta_tokens']], False)
    h0 = jnp.concatenate([_unblock(meta), jnp.zeros((BLK - N_META, D), F32), x[0]], axis=0)
    a1 = _norm_fwd(h0, p['ffn1_pre_norm'])
    g1, (wu1,) = _ffn_gate(a1, wg1, comm=gather(['ffn1_w_up']))
    (u1, hid1), (wd1,) = _ffn_up_gated(a1, wu1, g1, comm=gather(['ffn1_w_down']))
    (f1, h1, um), (w_in_g,) = _ffn_down(hid1, wd1, h0, p['ffn1_post_norm'], next_norm=p['mix_pre_norm'],
                                        comm=gather(['w_in']))

    w_in = jnp.pad(w_in_g.reshape(N_DEV * in_s, D), ((0, D_INP - N_DEV * in_s), (0, 0)))
    proj, (uq_g, uk_g, uv_g, wout_g) = _mm_nt("mix_in", [(um, w_in)], BF16, tn_target=1664,
                                              comm=gather(['mla_w_uq', 'mla_w_uk', 'mla_w_uv', 'w_out']))
    wuq = jnp.pad(uq_g, ((0, 0), (0, QH - uq_s), (0, 0))).reshape(HEADS * QH, Q_RANK)
    wuk, wuv, w_out = _unblock(uk_g), _unblock(uv_g), wout_g.reshape(-1, D)
    qr, kr, vr, cqn, ckvn, krr = _prep(proj, tabs, p['mla_q_norm'], p['mla_kv_norm'])
    qm = _q_up(cqn, wuq, tabs_m)
    kn = _mm_nn("mla_k_up", ckvn, wuk, BF16)
    vm = _mm_nn("mla_v_up", ckvn, wuv, BF16)
    (o_mla, lse), (wg2, wu2) = _attn_fwd(qm, kn, krr, vm, comm=gather(['ffn2_w_gate', 'ffn2_w_up']))
    o_ret = _lin_attn("ret_fwd", qr, kr, vr, lg, False)
    ret = _post(o_ret, proj, p['ret_group_norm'])
    cat = jnp.concatenate([ret, o_mla], axis=1)
    m, h2, a2 = _mix_out(cat, w_out, h1, p['mix_post_norm'], p['ffn2_pre_norm'])

    (g2, u2, hid2), (wd2,) = _ffn_up(a2, wg2, wu2, comm=gather(['ffn2_w_down']))
    f2, h3 = _ffn_down(hid2, wd2, h2, p['ffn2_post_norm'])
    dh3, loss_blk = _loss(h3, target[0])

    dsmall = {}
    df2, dsmall['ffn2_post_norm'] = _norm_bwd(f2, p['ffn2_post_norm'], dh3, None, 0.5, BF16)
    dg2, du2 = _ffn_dhid(df2, wd2, g2, u2)
    dwd2 = _ffn_dwd(hid2, df2)
    (dwg2, dwu2), (R['ffn2_w_down'],) = _ffn_dwgu(a2, dg2, du2, comm=scatter([dwd2]))
    da2, (R['ffn2_w_gate'],) = _ffn_da(dg2, du2, wg2, wu2, comm=scatter([dwg2]))
    dh2, dsmall['ffn2_pre_norm'] = _norm_bwd(h2, p['ffn2_pre_norm'], da2, dh3, 1.0, F32)

    dm, dsmall['mix_post_norm'] = _norm_bwd(m, p['mix_post_norm'], dh2, None, 1.0, BF16)
    dcat = _mm_nt("mix_dcat", [(dm, w_out)], BF16)
    dwout = _mm_tn("mix_dwout", cat, [dm])[0]
    do_ret, drg, dsmall['ret_group_norm'] = _post_bwd(o_ret, proj, p['ret_group_norm'], dcat)
    dqr = _lin_attn("ret_dq", do_ret, vr, kr, lg, False, BF16)
    dkr = _lin_attn("ret_dk", vr, do_ret, qr, lg, True, BF16)
    dvr = _lin_attn("ret_dv", kr, qr, do_ret, lg, True, BF16)
    (dqm, dkn, dkr8, dvm), (R['ffn2_w_up'], R['w_out']) = _attn_bwd(
        qm, kn, krr, vm, o_mla, dcat, lse, comm=scatter([dwu2, dwout.reshape(N_DEV, -1, D)]))
    dqp = _unrope_q(dqm, tabs_m)
    dwuq = _mm_tn("mla_dwuq", dqp, [cqn])[0]
    dcqn = _mm_nn("mla_dcq", dqp, wuq, F32)
    dwuk, dwuv = _mm_tn("mla_dwukv", ckvn, [dkn, dvm])
    dckvn = _mm_nt("mla_dckv", [(dkn, wuk), (dvm, wuv)], F32)
    dproj, dsmall['mla_q_norm'], dsmall['mla_kv_norm'] = _prep_bwd(
        proj, dqr, dkr, dvr, drg, dcqn, dckvn, dkr8, tabs, p['mla_q_norm'], p['mla_kv_norm'])
    dwuq_b = dwuq.reshape(HEADS, QH, Q_RANK)[:, :uq_s]
    (dwin,), (R['mla_w_uq'], R['mla_w_uk'], R['mla_w_uv']) = _mm_tn(
        "mix_dwin", dproj, [um], comm=scatter([dwuq_b, _reblock(dwuk, p['mla_w_uk'].shape[1]),
                                               _reblock(dwuv, p['mla_w_uv'].shape[1])]))
    dwin_b = dwin[:N_DEV * in_s].reshape(N_DEV, in_s, D)
    half = D // 2
    dum, (r_win_a,) = _mm_nn("mix_du", dproj, w_in, F32, tn_target=512, comm=scatter([dwin_b[:, :, :half]]))
    dh1, dsmall['mix_pre_norm'] = _norm_bwd(h1, p['mix_pre_norm'], dum, dh2, 1.0, F32)

    df1, dsmall['ffn1_post_norm'] = _norm_bwd(f1, p['ffn1_post_norm'], dh1, None, 0.5, BF16)
    (dg1, du1), (r_win_b,) = _ffn_dhid(df1, wd1, g1, u1, comm=scatter([dwin_b[:, :, half:]]))
    R['w_in'] = jnp.concatenate([r_win_a, r_win_b], axis=2)
    dwd1 = _ffn_dwd(hid1, df1)
    (dwg1, dwu1), (R['ffn1_w_down'],) = _ffn_dwgu(a1, dg1, du1, comm=scatter([dwd1]))
    da1, (R['ffn1_w_gate'],) = _ffn_da(dg1, du1, wg1, wu1, comm=scatter([dwg1]))
    dh0, dsmall['ffn1_pre_norm'] = _norm_bwd(h0, p['ffn1_pre_norm'], da1, dh1, 1.0, F32)
    tail_sems_s, tail_sems_r, tail_src, tail_land, token = _scatter_start(dwu1)

    def slab(a):
        a = a.reshape(-1, 128)
        return jnp.pad(a, ((0, (-a.shape[0]) % 8), (0, 0)))

    slab_rows = lambda n: -(-(p[n].shape[-1] // 128) // 8) * 8
    packed = jnp.concatenate([slab(dsmall[n]) for n in SMALL] + [slab(dh0[:N_META]), loss_blk], axis=0)
    red = _allreduce_small(packed + token[0, 0])
    offs = sum(slab_rows(n) for n in SMALL)
    n_small = offs
    gmeta_full = red[offs:offs + N_META * D // 128].reshape(N_META, D)
    offs += N_META * D // 128
    loss = red[offs, 0]

    grad, delta, new_m, new_v = {}, {}, {}, {}
    meanwhile = []
    for n in BIG:
        if n == 'ffn1_w_up':
            continue
        outs = _adam("adam_" + n, p[n], sq(mom[n], n), sq(vel[n], n), g_slots=R[n], after=token)
        meanwhile.append(outs[0])
        grad[n], delta[n], new_m[n], new_v[n] = [unsq(o, n) for o in outs]
    pack = lambda d: jnp.concatenate([slab(d[n]) for n in SMALL], axis=0)
    outs = _adam("adam_small", pack(w), pack(mom), pack(vel), g=red[:n_small])
    meanwhile.append(outs[0])
    offs = 0
    for n in SMALL:
        r = p[n].shape[-1] // 128
        grad[n], delta[n], new_m[n], new_v[n] = [o[offs:offs + r].reshape(w[n].shape) for o in outs]
        offs += slab_rows(n)
    dev = 4 * lax.axis_index("x") + 2 * lax.axis_index("y") + lax.axis_index("c")
    mcols = w['meta_tokens'].shape[1]
    gmeta = lax.dynamic_slice(gmeta_full, (0, dev * mcols), (N_META, mcols))
    outs = _adam("adam_meta", w['meta_tokens'], mom['meta_tokens'], vel['meta_tokens'], g=gmeta)
    grad['meta_tokens'], delta['meta_tokens'], new_m['meta_tokens'], new_v['meta_tokens'] = outs
    meanwhile.append(outs[0])
    n = 'ffn1_w_up'
    slots = _scatter_wait(tail_sems_s, tail_sems_r, tail_src, tail_land, meanwhile)
    outs = _adam("adam_" + n, p[n], sq(mom[n], n), sq(vel[n], n), g_slots=slots)
    grad[n], delta[n], new_m[n], new_v[n] = [unsq(o, n) for o in outs]

    return (loss, dh0[BLK:][None], *[grad[n] for n in WEIGHTS], *[delta[n] for n in WEIGHTS],
            *[new_m[n] for n in WEIGHTS], *[new_v[n] for n in WEIGHTS])


def kernel(x, meta_tokens, ffn1_pre_norm, ffn1_w_gate, ffn1_w_up, ffn1_w_down, ffn1_post_norm, mix_pre_norm, w_in, ret_group_norm, mla_q_norm, mla_w_uq, mla_kv_norm, mla_w_uk, mla_w_uv, w_out, mix_post_norm, ffn2_pre_norm, ffn2_w_gate, ffn2_w_up, ffn2_w_down, ffn2_post_norm, loss_target, m_meta_tokens, m_ffn1_pre_norm, m_ffn1_w_gate, m_ffn1_w_up, m_ffn1_w_down, m_ffn1_post_norm, m_mix_pre_norm, m_w_in, m_ret_group_norm, m_mla_q_norm, m_mla_w_uq, m_mla_kv_norm, m_mla_w_uk, m_mla_w_uv, m_w_out, m_mix_post_norm, m_ffn2_pre_norm, m_ffn2_w_gate, m_ffn2_w_up, m_ffn2_w_down, m_ffn2_post_norm, v_meta_tokens, v_ffn1_pre_norm, v_ffn1_w_gate, v_ffn1_w_up, v_ffn1_w_down, v_ffn1_post_norm, v_mix_pre_norm, v_w_in, v_ret_group_norm, v_mla_q_norm, v_mla_w_uq, v_mla_kv_norm, v_mla_w_uk, v_mla_w_uv, v_w_out, v_mix_post_norm, v_ffn2_pre_norm, v_ffn2_w_gate, v_ffn2_w_up, v_ffn2_w_down, v_ffn2_post_norm):
    w = dict(zip(WEIGHTS, (meta_tokens, ffn1_pre_norm, ffn1_w_gate, ffn1_w_up, ffn1_w_down, ffn1_post_norm,
                           mix_pre_norm, w_in, ret_group_norm, mla_q_norm, mla_w_uq, mla_kv_norm, mla_w_uk, mla_w_uv,
                           w_out, mix_post_norm, ffn2_pre_norm, ffn2_w_gate, ffn2_w_up, ffn2_w_down, ffn2_post_norm)))
    mom = dict(zip(WEIGHTS, (m_meta_tokens, m_ffn1_pre_norm, m_ffn1_w_gate, m_ffn1_w_up, m_ffn1_w_down,
                             m_ffn1_post_norm, m_mix_pre_norm, m_w_in, m_ret_group_norm, m_mla_q_norm, m_mla_w_uq,
                             m_mla_kv_norm, m_mla_w_uk, m_mla_w_uv, m_w_out, m_mix_post_norm, m_ffn2_pre_norm,
                             m_ffn2_w_gate, m_ffn2_w_up, m_ffn2_w_down, m_ffn2_post_norm)))
    vel = dict(zip(WEIGHTS, (v_meta_tokens, v_ffn1_pre_norm, v_ffn1_w_gate, v_ffn1_w_up, v_ffn1_w_down,
                             v_ffn1_post_norm, v_mix_pre_norm, v_w_in, v_ret_group_norm, v_mla_q_norm, v_mla_w_uq,
                             v_mla_kv_norm, v_mla_w_uk, v_mla_w_uv, v_w_out, v_mix_post_norm, v_ffn2_pre_norm,
                             v_ffn2_w_gate, v_ffn2_w_up, v_ffn2_w_down, v_ffn2_post_norm)))
    return _step(x, loss_target, w, mom, vel)
```

```python
import functools
import math

import jax
import jax.numpy as jnp
from jax import lax
from jax.experimental import pallas as pl
from jax.experimental.pallas import tpu as pltpu

N_DEV = 8
N_META = 16
BLK = 128
HEADS = 8
HD = 128
ROPE = 64
Q_RANK = 512
KV_RANK = 256
QH = 2 * HD
D_INP = 4 * HEADS * HD + Q_RANK + KV_RANK + BLK
ROPE_THETA = 10000.0
EPS = 1e-6
ADAM_LR = 0.001
ADAM_B1 = 0.9
ADAM_B2 = 0.999
ADAM_EPS = 1e-08
ADAM_WD = 0.01
ADAM_STEP = 10
V7X_VMEM_LIMIT = 48 * 1024 * 1024
MESH = pl.DeviceIdType.MESH
F32 = jnp.float32
BF16 = jnp.bfloat16

WEIGHTS = ['meta_tokens', 'ffn1_pre_norm', 'ffn1_w_gate', 'ffn1_w_up', 'ffn1_w_down', 'ffn1_post_norm',
           'mix_pre_norm', 'w_in', 'ret_group_norm', 'mla_q_norm', 'mla_w_uq', 'mla_kv_norm', 'mla_w_uk',
           'mla_w_uv', 'w_out', 'mix_post_norm', 'ffn2_pre_norm', 'ffn2_w_gate', 'ffn2_w_up', 'ffn2_w_down',
           'ffn2_post_norm']
SMALL = ['ffn1_pre_norm', 'ffn1_post_norm', 'mix_pre_norm', 'ret_group_norm', 'mla_q_norm', 'mla_kv_norm',
         'mix_post_norm', 'ffn2_pre_norm', 'ffn2_post_norm']
TRANSPOSED = ('ffn1_w_gate', 'ffn1_w_up', 'ffn2_w_gate', 'ffn2_w_up', 'w_in', 'mla_w_uq')
BIG = ['ffn1_w_gate', 'ffn1_w_up', 'ffn1_w_down', 'w_in', 'mla_w_uq', 'mla_w_uk', 'mla_w_uv', 'w_out',
       'ffn2_w_gate', 'ffn2_w_up', 'ffn2_w_down']

_DIMS = {'nn': (((1,), (0,)), ((), ())), 'nt': (((1,), (1,)), ((), ())), 'tn': (((0,), (0,)), ((), ()))}


def _tile(n, target, mult=16):
    best = None
    for t in range(mult, min(n, target) + 1, mult):
        if n % t == 0:
            best = t
    return best if best is not None else n


def _params(sem):
    return pltpu.CompilerParams(dimension_semantics=sem, vmem_limit_bytes=V7X_VMEM_LIMIT)


def _dot(a, b, dims):
    return lax.dot_general(a, b, _DIMS[dims], preferred_element_type=F32)


def _sigmoid(x):
    return 0.5 * jnp.tanh(0.5 * x) + 0.5


def _me_and_peers():
    x, y, c = lax.axis_index("x"), lax.axis_index("y"), lax.axis_index("c")

    def peer(j):
        px = 1 - x if (j >> 2) & 1 else x
        py = 1 - y if (j >> 1) & 1 else y
        pc = 1 - c if j & 1 else c
        return (px, py, pc), 4 * px + 2 * py + pc

    return 4 * x + 2 * y + c, peer


class _Exchange:
    def __init__(self, arrays, per_peer):
        self.arrays = list(arrays)
        self.per_peer = per_peer
        self.n = len(self.arrays)
        self.out_shapes = [jax.ShapeDtypeStruct((N_DEV,) + tuple(a.shape[1:] if per_peer else a.shape), a.dtype)
                           for a in self.arrays]
        self.specs = [pl.BlockSpec(memory_space=pl.ANY)] * self.n
        self.scratch = [pltpu.SemaphoreType.DMA((7 * self.n,)), pltpu.SemaphoreType.DMA((7 * self.n,)),
                        pltpu.SemaphoreType.DMA((self.n,))]

    def _copies(self, src, dst, sems):
        send_sems, recv_sems, local_sems = sems
        me, peer = _me_and_peers()
        sib, _ = peer(1)
        local, sends, recvs, passes = [], {}, {}, {}
        for k in range(self.n):
            own = src[k].at[me] if self.per_peer else src[k]
            local.append(pltpu.make_async_copy(own, dst[k].at[me], local_sems.at[k]))
            for j in range(1, N_DEV):
                pid, pidx = peer(j)
                out = src[k].at[pidx] if self.per_peer else src[k]
                sem = dict(send_sem=send_sems.at[k * 7 + j - 1], recv_sem=recv_sems.at[k * 7 + j - 1])
                recvs[k, j] = pltpu.make_async_remote_copy(src_ref=out, dst_ref=dst[k].at[pidx], device_id=pid,
                                                           device_id_type=MESH, **sem)
                if self.per_peer or j in (1, 2, 4, 6):
                    sends[k, j] = pltpu.make_async_remote_copy(src_ref=out, dst_ref=dst[k].at[me], device_id=pid,
                                                               device_id_type=MESH, **sem)
                else:
                    _, origin = peer(j ^ 1)
                    passes[k, j ^ 1] = pltpu.make_async_remote_copy(
                        src_ref=dst[k].at[origin], dst_ref=dst[k].at[origin], device_id=sib, device_id_type=MESH, **sem)
        return local, sends, recvs, passes

    def start(self, src, dst, sems):
        local, sends, _, _ = self._copies(src, dst, sems)
        for cp in local + list(sends.values()):
            cp.start()

    def finish(self, src, dst, sems):
        local, sends, recvs, passes = self._copies(src, dst, sems)
        for key, cp in passes.items():
            recvs[key].wait_recv()
            cp.start()
        for key, cp in recvs.items():
            if key not in passes:
                cp.wait_recv()
        for cp in list(sends.values()) + list(passes.values()):
            cp.wait_send()
        for cp in local:
            cp.wait()


def _grid_edges(grid):
    first, last = None, None
    for a, n in enumerate(grid):
        f, l = pl.program_id(a) == 0, pl.program_id(a) == n - 1
        first = f if first is None else first & f
        last = l if last is None else last & l
    return first, last


def _exchange(name, arrays, per_peer):
    ex = _Exchange(arrays, per_peer)
    n = ex.n

    def body(*refs):
        ex.start(refs[:n], refs[n:2 * n], refs[2 * n:])
        ex.finish(refs[:n], refs[n:2 * n], refs[2 * n:])

    return pl.pallas_call(body, name=name, out_shape=ex.out_shapes, in_specs=ex.specs, out_specs=ex.specs,
                          scratch_shapes=ex.scratch)(*arrays)


def _scatter_start(blocks):
    def body(src_ref, land_ref, send_sems, recv_sems, src_thru, land_thru, token, local_sem):
        me, peer = _me_and_peers()
        local = pltpu.make_async_copy(src_ref.at[me], land_ref.at[me], local_sem)
        local.start()
        for j in range(1, N_DEV):
            pid, pidx = peer(j)
            pltpu.make_async_remote_copy(src_ref=src_ref.at[pidx], dst_ref=land_ref.at[me],
                                         send_sem=send_sems.at[j - 1], recv_sem=recv_sems.at[j - 1],
                                         device_id=pid, device_id_type=MESH).start()
        local.wait()
        token[...] = jnp.zeros_like(token)

    hbm = pl.BlockSpec(memory_space=pltpu.HBM)
    sem = pl.BlockSpec(memory_space=pltpu.SEMAPHORE)
    return pl.pallas_call(
        body, name="scatter_tail_start",
        out_shape=(pltpu.SemaphoreType.DMA((7,)), pltpu.SemaphoreType.DMA((7,)), pltpu.HBM(blocks.shape, blocks.dtype),
                   pltpu.HBM(blocks.shape, blocks.dtype), jax.ShapeDtypeStruct((8, 128), F32)),
        in_specs=(hbm, hbm), out_specs=(sem, sem, hbm, hbm, pl.BlockSpec(memory_space=pltpu.VMEM)),
        input_output_aliases={0: 2, 1: 3}, scratch_shapes=[pltpu.SemaphoreType.DMA],
        compiler_params=pltpu.CompilerParams(has_side_effects=pltpu.SideEffectType.DATAFLOW_SIDE_EFFECTING),
    )(pltpu.with_memory_space_constraint(blocks, pltpu.HBM),
      pltpu.with_memory_space_constraint(lax.empty(blocks.shape, blocks.dtype), pltpu.HBM))


def _scatter_wait(send_sems, recv_sems, src_thru, land_thru, after):
    n_after = len(after)

    def body(src_ref, land_ref, send_sems, recv_sems, *rest):
        me, peer = _me_and_peers()
        for j in range(1, N_DEV):
            pid, pidx = peer(j)
            cp = pltpu.make_async_remote_copy(src_ref=src_ref.at[pidx], dst_ref=land_ref.at[pidx],
                                              send_sem=send_sems.at[j - 1], recv_sem=recv_sems.at[j - 1],
                                              device_id=pid, device_id_type=MESH)
            cp.wait_send()
            cp.wait_recv()

    hbm = pl.BlockSpec(memory_space=pltpu.HBM)
    sem = pl.BlockSpec(memory_space=pltpu.SEMAPHORE)
    return pl.pallas_call(
        body, name="scatter_tail_wait",
        out_shape=(pltpu.HBM(src_thru.shape, src_thru.dtype), pltpu.HBM(land_thru.shape, land_thru.dtype)),
        in_specs=(hbm, hbm, sem, sem) + (pl.BlockSpec(memory_space=pl.ANY),) * n_after, out_specs=(hbm, hbm),
        input_output_aliases={0: 0, 1: 1},
        compiler_params=pltpu.CompilerParams(has_side_effects=pltpu.SideEffectType.DATAFLOW_SIDE_EFFECTING),
    )(src_thru, land_thru, send_sems, recv_sems, *after)[1]


def _allreduce_small(v):
    rows = v.shape[0]

    def body(v_ref, out_ref, buf, send_sems, recv_sems):
        me, peer = _me_and_peers()
        buf[pl.ds(me, 1)] = v_ref[...][None]
        sends = []
        for j in range(1, N_DEV):
            pid, _ = peer(j)
            cp = pltpu.make_async_remote_copy(src_ref=v_ref, dst_ref=buf.at[me], send_sem=send_sems.at[j - 1],
                                              recv_sem=recv_sems.at[j - 1], device_id=pid, device_id_type=MESH)
            cp.start()
            sends.append(cp)
        for j in range(1, N_DEV):
            pid, pidx = peer(j)
            pltpu.make_async_remote_copy(src_ref=v_ref, dst_ref=buf.at[pidx], send_sem=send_sems.at[j - 1],
                                         recv_sem=recv_sems.at[j - 1], device_id=pid,
                                         device_id_type=MESH).wait_recv()
        for cp in sends:
            cp.wait_send()
        acc = buf[0]
        for s in range(1, N_DEV):
            acc = acc + buf[s]
        out_ref[...] = acc

    vm = pl.BlockSpec(memory_space=pltpu.VMEM)
    return pl.pallas_call(
        body, name="allreduce_small", out_shape=jax.ShapeDtypeStruct(v.shape, F32),
        in_specs=[vm], out_specs=vm,
        scratch_shapes=[pltpu.VMEM((N_DEV, rows, 128), F32), pltpu.SemaphoreType.DMA((7,)),
                        pltpu.SemaphoreType.DMA((7,))],
    )(v)


def _mm(name, grid, sem, k_axis, ops, op_specs, pairs, acc_shapes, extras, extra_specs, epilogue, outs, out_specs,
        comm=None):
    n_op, n_ex, n_out = len(ops), len(extras), len(outs)
    nk = grid[k_axis] if k_axis is not None else 1
    n_acc = len(acc_shapes) if nk > 1 else 0
    n_cm = comm.n if comm is not None else 0

    def body(*refs):
        op_refs = refs[:n_op]
        ex_refs = refs[n_op:n_op + n_ex]
        n_in = n_op + n_ex + n_cm
        out_refs = refs[n_in:n_in + n_out]
        acc_refs = refs[n_in + n_out + n_cm:n_in + n_out + n_cm + n_acc]
        if comm is not None:
            cm_refs = (refs[n_op + n_ex:n_in], refs[n_in + n_out:n_in + n_out + n_cm],
                       refs[n_in + n_out + n_cm + n_acc:])
            first, last = _grid_edges(grid)

            @pl.when(first)
            def _():
                comm.start(*cm_refs)

        def finish(vals):
            res = epilogue(*vals, *[e[...] for e in ex_refs])
            for o, r in zip(out_refs, res):
                o[...] = r.astype(o.dtype)

        if nk == 1:
            parts = [None] * len(acc_shapes)
            for li, ri, dims, ai in pairs:
                d = _dot(op_refs[li][...], op_refs[ri][...], dims)
                parts[ai] = d if parts[ai] is None else parts[ai] + d
            finish(parts)
        else:
            k = pl.program_id(k_axis)

            @pl.when(k == 0)
            def _():
                for a in acc_refs:
                    a[...] = jnp.zeros_like(a)

            for li, ri, dims, ai in pairs:
                acc_refs[ai][...] += _dot(op_refs[li][...], op_refs[ri][...], dims)

            @pl.when(k == nk - 1)
            def _():
                finish([a[...] for a in acc_refs])

        if comm is not None:
            @pl.when(last)
            def _():
                comm.finish(*cm_refs)

    scratch = [pltpu.VMEM(s, F32) for s in acc_shapes] if nk > 1 else []
    if comm is None:
        return pl.pallas_call(
            body, name=name, grid=grid, out_shape=outs,
            in_specs=list(op_specs) + list(extra_specs), out_specs=list(out_specs),
            scratch_shapes=scratch, compiler_params=_params(sem),
        )(*ops, *extras)
    res = pl.pallas_call(
        body, name=name, grid=grid, out_shape=list(outs) + comm.out_shapes,
        in_specs=list(op_specs) + list(extra_specs) + comm.specs, out_specs=list(out_specs) + comm.specs,
        scratch_shapes=scratch + comm.scratch, compiler_params=_params(("arbitrary",) * len(grid)),
    )(*ops, *extras, *comm.arrays)
    return res[:n_out], res[n_out:]


def _with_comm(res, comm, pick):
    if comm is None:
        return pick(res)
    return pick(res[0]), res[1]


def _mm_nn(name, a, w, out_dtype, tm_target=704, tn_target=1664, epilogue=None, extras=(), extra_specs=(), comm=None):
    L, K = a.shape
    N = w.shape[1]
    tm, tn = _tile(L, tm_target), _tile(N, tn_target, 128)
    ep = epilogue if epilogue is not None else (lambda acc: (acc,))
    res = _mm(name, (L // tm, N // tn), ("parallel", "parallel"), None,
              [a, w], [pl.BlockSpec((tm, K), lambda i, j: (i, 0)), pl.BlockSpec((K, tn), lambda i, j: (0, j))],
              [(0, 1, 'nn', 0)], [(tm, tn)], list(extras), list(extra_specs), ep,
              [jax.ShapeDtypeStruct((L, N), out_dtype)], [pl.BlockSpec((tm, tn), lambda i, j: (i, j))], comm=comm)
    return _with_comm(res, comm, lambda o: o[0])


def _mm_nt(name, pairs_aw, out_dtype, tm_target=704, tn_target=512, comm=None):
    L = pairs_aw[0][0].shape[0]
    N = pairs_aw[0][1].shape[0]
    tm, tn = _tile(L, tm_target), _tile(N, tn_target, 128)
    ops, specs, pairs = [], [], []
    for t, (a, w) in enumerate(pairs_aw):
        K = a.shape[1]
        ops += [a, w]
        specs += [pl.BlockSpec((tm, K), lambda i, j: (i, 0)), pl.BlockSpec((tn, K), lambda i, j: (j, 0))]
        pairs.append((2 * t, 2 * t + 1, 'nt', 0))
    res = _mm(name, (L // tm, N // tn), ("parallel", "parallel"), None, ops, specs, pairs, [(tm, tn)], [], [],
              lambda acc: (acc,), [jax.ShapeDtypeStruct((L, N), out_dtype)],
              [pl.BlockSpec((tm, tn), lambda i, j: (i, j))], comm=comm)
    return _with_comm(res, comm, lambda o: o[0])


def _mm_tn(name, a, bs, out_dtype=BF16, tk_target=1408, tn_target=1664, tm_target=2048, comm=None):
    L, M = a.shape
    N = bs[0].shape[1]
    tk, tn, tm = _tile(L, tk_target), _tile(N, tn_target, 128), _tile(M, tm_target, 128)
    nb = len(bs)
    ops = [a] + list(bs)
    specs = [pl.BlockSpec((tk, tm), lambda i, j, k: (k, i))] + [pl.BlockSpec((tk, tn), lambda i, j, k: (k, j))] * nb
    res = _mm(name, (M // tm, N // tn, L // tk), ("parallel", "parallel", "arbitrary"), 2, ops, specs,
              [(0, 1 + t, 'tn', t) for t in range(nb)], [(tm, tn)] * nb, [], [], lambda *acc: acc,
              [jax.ShapeDtypeStruct((M, N), out_dtype)] * nb,
              [pl.BlockSpec((tm, tn), lambda i, j, k: (i, j))] * nb, comm=comm)
    return _with_comm(res, comm, lambda o: o)


def _norm_fwd(x, w):
    L, D = x.shape
    tr = _tile(L, 512)

    def body(x_ref, w_ref, y_ref):
        v = x_ref[...]
        r = lax.rsqrt(jnp.mean(v * v, axis=-1, keepdims=True) + EPS)
        y_ref[...] = (v * r * w_ref[...]).astype(y_ref.dtype)

    return pl.pallas_call(
        body, name="norm_fwd", grid=(L // tr,), out_shape=jax.ShapeDtypeStruct((L, D), BF16),
        in_specs=[pl.BlockSpec((tr, D), lambda i: (i, 0)), pl.BlockSpec((1, D), lambda i: (0, 0))],
        out_specs=pl.BlockSpec((tr, D), lambda i: (i, 0)), compiler_params=_params(("parallel",)),
    )(x, w)


def _norm_bwd_math(x, w, dy):
    r = lax.rsqrt(jnp.mean(x * x, axis=-1, keepdims=True) + EPS)
    gy = dy * w
    dx = r * (gy - x * (r * r) * jnp.mean(gy * x, axis=-1, keepdims=True))
    dw = jnp.sum(dy * x * r, axis=0, keepdims=True)
    return dx, dw


def _norm_bwd(x, w, dy, res, scale, out_dtype):
    L, D = x.shape
    tr = _tile(L, 384)
    has_res = res is not None

    def body(*refs):
        x_ref, w_ref, dy_ref = refs[:3]
        res_ref = refs[3] if has_res else None
        dx_ref, dw_ref = refs[-2:]
        dx, dw = _norm_bwd_math(x_ref[...], w_ref[...], dy_ref[...].astype(F32))
        dx = scale * dx
        if has_res:
            dx = dx + res_ref[...]
        dx_ref[...] = dx.astype(dx_ref.dtype)

        @pl.when(pl.program_id(0) == 0)
        def _():
            dw_ref[...] = jnp.zeros_like(dw_ref)

        dw_ref[...] += scale * dw

    row = pl.BlockSpec((tr, D), lambda i: (i, 0))
    vec = pl.BlockSpec((1, D), lambda i: (0, 0))
    return pl.pallas_call(
        body, name="norm_bwd", grid=(L // tr,),
        out_shape=[jax.ShapeDtypeStruct((L, D), out_dtype), jax.ShapeDtypeStruct((1, D), F32)],
        in_specs=[row, vec, row] + ([row] if has_res else []), out_specs=[row, vec],
        compiler_params=_params(("arbitrary",)),
    )(*([x, w, dy] + ([res] if has_res else [])))


def _loss(h, target):
    L, D = h.shape

    def body(h_ref, t_ref, dh_ref, loss_ref):
        i = pl.program_id(0)

        @pl.when(i == 0)
        def _():
            dh_ref[...] = jnp.zeros_like(dh_ref)
            loss_ref[...] = jnp.zeros_like(loss_ref)

        @pl.when(i > 0)
        def _():
            diff = h_ref[...] - t_ref[...]
            dh_ref[...] = diff * (1.0 / D)
            loss_ref[...] += 0.5 * jnp.sum(diff * diff) * (1.0 / D)

    return pl.pallas_call(
        body, name="loss", grid=(L // BLK,),
        out_shape=[jax.ShapeDtypeStruct((L, D), F32), jax.ShapeDtypeStruct((8, 128), F32)],
        in_specs=[pl.BlockSpec((BLK, D), lambda i: (i, 0)),
                  pl.BlockSpec((BLK, D), lambda i: (jnp.maximum(i - 1, 0), 0))],
        out_specs=[pl.BlockSpec((BLK, D), lambda i: (i, 0)), pl.BlockSpec((8, 128), lambda i: (0, 0))],
        compiler_params=_params(("arbitrary",)),
    )(h, target)


def _ffn_up(a, wg, wu, comm=None):
    L, D = a.shape
    F = wg.shape[1]
    tm = _tile(L, 704)

    def ep(g, u):
        return g, u, g * _sigmoid(g) * u

    hspec = pl.BlockSpec((None, tm, F), lambda i, j: (j, i, 0))
    wspec = pl.BlockSpec((None, F, D), lambda i, j: (j, 0, 0))
    res = _mm("ffn_up", (L // tm, N_DEV), ("parallel", "parallel"), None,
              [a, wg, wu], [pl.BlockSpec((tm, D), lambda i, j: (i, 0)), wspec, wspec],
              [(0, 1, 'nt', 0), (0, 2, 'nt', 1)], [(tm, F)] * 2, [], [], ep,
              [jax.ShapeDtypeStruct((N_DEV, L, F), BF16)] * 3, [hspec] * 3, comm=comm)
    return _with_comm(res, comm, lambda o: o)


def _ffn_gate(a, wg, comm=None):
    L, D = a.shape
    F = wg.shape[1]
    tm = _tile(L, 704)
    res = _mm("ffn_gate", (L // tm, N_DEV), ("parallel", "parallel"), None,
              [a, wg], [pl.BlockSpec((tm, D), lambda i, j: (i, 0)), pl.BlockSpec((None, F, D), lambda i, j: (j, 0, 0))],
              [(0, 1, 'nt', 0)], [(tm, F)], [], [], lambda g: (g,),
              [jax.ShapeDtypeStruct((N_DEV, L, F), BF16)], [pl.BlockSpec((None, tm, F), lambda i, j: (j, i, 0))],
              comm=comm)
    return _with_comm(res, comm, lambda o: o[0])


def _ffn_up_gated(a, wu, g, comm=None):
    L, D = a.shape
    F = wu.shape[1]
    tm = _tile(L, 704)

    def ep(u, g_):
        g32 = g_.astype(F32)
        return u, g32 * _sigmoid(g32) * u

    hspec = pl.BlockSpec((None, tm, F), lambda i, j: (j, i, 0))
    res = _mm("ffn_up_gated", (L // tm, N_DEV), ("parallel", "parallel"), None,
              [a, wu], [pl.BlockSpec((tm, D), lambda i, j: (i, 0)), pl.BlockSpec((None, F, D), lambda i, j: (j, 0, 0))],
              [(0, 1, 'nt', 0)], [(tm, F)], [g], [hspec], ep,
              [jax.ShapeDtypeStruct((N_DEV, L, F), BF16)] * 2, [hspec, hspec], comm=comm)
    return _with_comm(res, comm, lambda o: o)


def _resnorm_epilogue(scale, with_next):
    def ep(acc, h, w, *w_next):
        r = lax.rsqrt(jnp.mean(acc * acc, axis=-1, keepdims=True) + EPS)
        h_out = h + scale * (acc * r * w)
        if not with_next:
            return acc, h_out
        r_next = lax.rsqrt(jnp.mean(h_out * h_out, axis=-1, keepdims=True) + EPS)
        return acc, h_out, h_out * r_next * w_next[0]
    return ep


def _ffn_down(hid, wd, h_in, post, next_norm=None, comm=None):
    _, L, F = hid.shape
    D = wd.shape[2]
    tm = _tile(L, 528)
    row = pl.BlockSpec((tm, D), lambda i, j: (i, 0))
    vec = pl.BlockSpec((1, D), lambda i, j: (0, 0))
    nxt = [] if next_norm is None else [next_norm]
    res = _mm("ffn_down", (L // tm, N_DEV), ("parallel", "arbitrary"), 1,
              [hid, wd], [pl.BlockSpec((None, tm, F), lambda i, j: (j, i, 0)),
                          pl.BlockSpec((None, F, D), lambda i, j: (j, 0, 0))],
              [(0, 1, 'nn', 0)], [(tm, D)], [h_in, post] + nxt, [row, vec] + [vec] * len(nxt),
              _resnorm_epilogue(0.5, bool(nxt)),
              [jax.ShapeDtypeStruct((L, D), F32)] * 2 + [jax.ShapeDtypeStruct((L, D), BF16)] * len(nxt),
              [row] * (2 + len(nxt)), comm=comm)
    return _with_comm(res, comm, lambda o: o)


def _ffn_dhid(df, wd, g, u, comm=None):
    L, D = df.shape
    F = wd.shape[1]
    tm = _tile(L, 704)

    def ep(dhid, g_, u_):
        g32, u32 = g_.astype(F32), u_.astype(F32)
        sg = _sigmoid(g32)
        return dhid * u32 * sg * (1.0 + g32 * (1.0 - sg)), dhid * g32 * sg

    hspec = pl.BlockSpec((None, tm, F), lambda i, j: (j, i, 0))
    res = _mm("ffn_dhid", (L // tm, N_DEV), ("parallel", "parallel"), None,
              [df, wd], [pl.BlockSpec((tm, D), lambda i, j: (i, 0)),
                         pl.BlockSpec((None, F, D), lambda i, j: (j, 0, 0))],
              [(0, 1, 'nt', 0)], [(tm, F)], [g, u], [hspec, hspec], ep,
              [jax.ShapeDtypeStruct((N_DEV, L, F), BF16)] * 2, [hspec, hspec], comm=comm)
    return _with_comm(res, comm, lambda o: o)


def _ffn_dwd(hid, df, comm=None):
    _, L, F = hid.shape
    D = df.shape[1]
    tk = _tile(L, 1408)
    res = _mm("ffn_dwd", (N_DEV, L // tk), ("parallel", "arbitrary"), 1,
              [hid, df], [pl.BlockSpec((None, tk, F), lambda j, k: (j, k, 0)),
                          pl.BlockSpec((tk, D), lambda j, k: (k, 0))],
              [(0, 1, 'tn', 0)], [(F, D)], [], [], lambda acc: (acc,),
              [jax.ShapeDtypeStruct((N_DEV, F, D), BF16)], [pl.BlockSpec((None, F, D), lambda j, k: (j, 0, 0))],
              comm=comm)
    return _with_comm(res, comm, lambda o: o[0])


def _ffn_dwgu(a, dg, du, comm=None):
    L, D = a.shape
    F = dg.shape[2]
    tk = _tile(L, 1408)
    hspec = pl.BlockSpec((None, tk, F), lambda j, k: (j, k, 0))
    wspec = pl.BlockSpec((None, F, D), lambda j, k: (j, 0, 0))
    res = _mm("ffn_dwgu", (N_DEV, L // tk), ("parallel", "arbitrary"), 1,
              [a, dg, du], [pl.BlockSpec((tk, D), lambda j, k: (k, 0)), hspec, hspec],
              [(1, 0, 'tn', 0), (2, 0, 'tn', 1)], [(F, D)] * 2, [], [], lambda *acc: acc,
              [jax.ShapeDtypeStruct((N_DEV, F, D), BF16)] * 2, [wspec, wspec], comm=comm)
    return _with_comm(res, comm, lambda o: o)


def _ffn_da(dg, du, wg, wu, comm=None):
    _, L, F = dg.shape
    D = wg.shape[2]
    tm = _tile(L, 704)
    hspec = pl.BlockSpec((None, tm, F), lambda i, j: (j, i, 0))
    wspec = pl.BlockSpec((None, F, D), lambda i, j: (j, 0, 0))
    row = pl.BlockSpec((tm, D), lambda i, j: (i, 0))
    res = _mm("ffn_da", (L // tm, N_DEV), ("parallel", "arbitrary"), 1,
              [dg, du, wg, wu], [hspec, hspec, wspec, wspec],
              [(0, 2, 'nn', 0), (1, 3, 'nn', 0)], [(tm, D)], [], [], lambda acc: (acc,),
              [jax.ShapeDtypeStruct((L, D), F32)], [row], comm=comm)
    return _with_comm(res, comm, lambda o: o[0])


def _rope_tables(L):
    rows = jnp.arange(L, dtype=F32)
    pos = jnp.where(rows < BLK, rows, rows - (BLK - N_META))
    inv_r = ROPE_THETA ** (-jnp.arange(0, HD, 2, dtype=F32) / HD)
    ang_r = pos[:, None] * inv_r[None, :]
    cr = jnp.concatenate([jnp.cos(ang_r), jnp.cos(ang_r)], axis=1)
    sr = jnp.concatenate([-jnp.sin(ang_r), jnp.sin(ang_r)], axis=1)
    inv_m = ROPE_THETA ** (-jnp.arange(0, ROPE, 2, dtype=F32) / ROPE)
    ang_m = pos[:, None] * inv_m[None, :]
    z32 = jnp.zeros((L, ROPE // 2), F32)
    z64 = jnp.zeros((L, HD - ROPE), F32)
    cm = jnp.concatenate([jnp.cos(ang_m), jnp.cos(ang_m), z64], axis=1)
    sa = jnp.concatenate([-jnp.sin(ang_m), z32, z64], axis=1)
    sb = jnp.concatenate([z32, jnp.sin(ang_m), z64], axis=1)
    return cr, sr, cm, sa, sb


def _rope_ret(x, cr, sr):
    return x * cr + pltpu.roll(x, HD // 2, 1) * sr


def _rope_ret_t(d, cr, sr):
    return d * cr + pltpu.roll(d * sr, HD // 2, 1)


def _rope_mla(x, cm, sa, sb):
    return x * cm + pltpu.roll(x, HD - ROPE // 2, 1) * sa + pltpu.roll(x, ROPE // 2, 1) * sb


def _rope_mla_t(d, cm, sa, sb):
    return d * cm + pltpu.roll(d * sa, ROPE // 2, 1) + pltpu.roll(d * sb, HD - ROPE // 2, 1)


C_RQ, C_RK, C_RV, C_RG = 0, HEADS * HD, 2 * HEADS * HD, 3 * HEADS * HD
C_CQ = 4 * HEADS * HD
C_CKV = C_CQ + Q_RANK
C_KR = C_CKV + KV_RANK
RET_K_SCALE = HD ** -0.5


def _prep(proj, tabs, qn, kvn):
    L = proj.shape[0]
    tr = _tile(L, 256)
    W = HEADS * HD

    def body(p_ref, cr_ref, sr_ref, cm_ref, sa_ref, sb_ref, qn_ref, kvn_ref, q_ref, k_ref, v_ref, cq_ref, ckv_ref,
             kr_ref):
        cr, sr = cr_ref[...], sr_ref[...]
        for h in range(HEADS):
            sl = slice(h * HD, (h + 1) * HD)
            q_ref[:, sl] = _rope_ret(p_ref[:, C_RQ + h * HD:C_RQ + (h + 1) * HD].astype(F32), cr, sr).astype(BF16)
            k_ref[:, sl] = (_rope_ret(p_ref[:, C_RK + h * HD:C_RK + (h + 1) * HD].astype(F32), cr, sr)
                            * RET_K_SCALE).astype(BF16)
        v_ref[...] = p_ref[:, C_RV:C_RV + W].astype(BF16)
        cq = p_ref[:, C_CQ:C_CQ + Q_RANK].astype(F32)
        cq_ref[...] = (cq * lax.rsqrt(jnp.mean(cq * cq, axis=-1, keepdims=True) + EPS) * qn_ref[...]).astype(BF16)
        ckv = p_ref[:, C_CKV:C_CKV + KV_RANK].astype(F32)
        ckv_ref[...] = (ckv * lax.rsqrt(jnp.mean(ckv * ckv, axis=-1, keepdims=True) + EPS)
                        * kvn_ref[...]).astype(BF16)
        kr_ref[...] = _rope_mla(p_ref[:, C_KR:C_KR + HD].astype(F32), cm_ref[...], sa_ref[...], sb_ref[...]).astype(BF16)

    row = lambda w: pl.BlockSpec((tr, w), lambda i: (i, 0))
    vec = lambda w: pl.BlockSpec((1, w), lambda i: (0, 0))
    return pl.pallas_call(
        body, name="mix_prep", grid=(L // tr,),
        out_shape=[jax.ShapeDtypeStruct((L, W), BF16)] * 3 + [jax.ShapeDtypeStruct((L, Q_RANK), BF16),
                                                              jax.ShapeDtypeStruct((L, KV_RANK), BF16),
                                                              jax.ShapeDtypeStruct((L, HD), BF16)],
        in_specs=[row(D_INP)] + [row(HD)] * 5 + [vec(Q_RANK), vec(KV_RANK)],
        out_specs=[row(W)] * 3 + [row(Q_RANK), row(KV_RANK), row(HD)],
        compiler_params=_params(("parallel",)),
    )(proj, *tabs, qn, kvn)


def _prep_bwd(proj, dq, dk, dv, drg, dcqn, dckvn, dkr8, tabs, qn, kvn):
    L = proj.shape[0]
    tr = _tile(L, 192)
    W = HEADS * HD

    def body(p_ref, dq_ref, dk_ref, dv_ref, drg_ref, dcq_ref, dckv_ref, dkr_ref, cr_ref, sr_ref, cm_ref, sa_ref,
             sb_ref, qn_ref, kvn_ref, dp_ref, dqn_ref, dkvn_ref):
        cr, sr = cr_ref[...], sr_ref[...]
        dkr = None
        for h in range(HEADS):
            sl = slice(h * HD, (h + 1) * HD)
            dp_ref[:, C_RQ + h * HD:C_RQ + (h + 1) * HD] = _rope_ret_t(dq_ref[:, sl].astype(F32), cr, sr).astype(BF16)
            dp_ref[:, C_RK + h * HD:C_RK + (h + 1) * HD] = (_rope_ret_t(dk_ref[:, sl].astype(F32), cr, sr)
                                                            * RET_K_SCALE).astype(BF16)
            part = dkr_ref[:, sl].astype(F32)
            dkr = part if dkr is None else dkr + part
        dp_ref[:, C_RV:C_RV + W] = dv_ref[...].astype(BF16)
        dp_ref[:, C_RG:C_RG + W] = drg_ref[...].astype(BF16)
        dcq, dqn = _norm_bwd_math(p_ref[:, C_CQ:C_CQ + Q_RANK].astype(F32), qn_ref[...], dcq_ref[...])
        dp_ref[:, C_CQ:C_CQ + Q_RANK] = dcq.astype(BF16)
        dckv, dkvn = _norm_bwd_math(p_ref[:, C_CKV:C_CKV + KV_RANK].astype(F32), kvn_ref[...], dckv_ref[...])
        dp_ref[:, C_CKV:C_CKV + KV_RANK] = dckv.astype(BF16)
        dp_ref[:, C_KR:C_KR + HD] = _rope_mla_t(dkr, cm_ref[...], sa_ref[...], sb_ref[...]).astype(BF16)

        @pl.when(pl.program_id(0) == 0)
        def _():
            dqn_ref[...] = jnp.zeros_like(dqn_ref)
            dkvn_ref[...] = jnp.zeros_like(dkvn_ref)

        dqn_ref[...] += dqn
        dkvn_ref[...] += dkvn

    row = lambda w: pl.BlockSpec((tr, w), lambda i: (i, 0))
    vec = lambda w: pl.BlockSpec((1, w), lambda i: (0, 0))
    return pl.pallas_call(
        body, name="mix_prep_bwd", grid=(L // tr,),
        out_shape=[jax.ShapeDtypeStruct((L, D_INP), BF16), jax.ShapeDtypeStruct((1, Q_RANK), F32),
                   jax.ShapeDtypeStruct((1, KV_RANK), F32)],
        in_specs=[row(D_INP)] + [row(W)] * 4 + [row(Q_RANK), row(KV_RANK), row(W)] + [row(HD)] * 5
                 + [vec(Q_RANK), vec(KV_RANK)],
        out_specs=[row(D_INP), vec(Q_RANK), vec(KV_RANK)],
        compiler_params=_params(("arbitrary",)),
    )(proj, dq, dk, dv, drg, dcqn, dckvn, dkr8, *tabs, qn, kvn)


def _post(o_ret, proj, gn):
    L, W = o_ret.shape
    tr = _tile(L, 384)

    def body(o_ref, rg_ref, gn_ref, out_ref):
        for h in range(HEADS):
            sl = slice(h * HD, (h + 1) * HD)
            o = o_ref[:, sl]
            rg = rg_ref[:, sl].astype(F32)
            n = o * lax.rsqrt(jnp.mean(o * o, axis=-1, keepdims=True) + EPS)
            out_ref[:, sl] = (n * gn_ref[:, sl] * (rg * _sigmoid(rg))).astype(BF16)

    row = pl.BlockSpec((tr, W), lambda i: (i, 0))
    return pl.pallas_call(
        body, name="ret_post", grid=(L // tr,), out_shape=jax.ShapeDtypeStruct((L, W), BF16),
        in_specs=[row, pl.BlockSpec((tr, W), lambda i: (i, C_RG // W)), pl.BlockSpec((1, W), lambda i: (0, 0))],
        out_specs=row, compiler_params=_params(("parallel",)),
    )(o_ret, proj, gn)


def _post_bwd(o_ret, proj, gn, dcat):
    L, W = o_ret.shape
    tr = _tile(L, 384)

    def body(o_ref, rg_ref, gn_ref, d_ref, do_ref, drg_ref, dgn_ref):
        @pl.when(pl.program_id(0) == 0)
        def _():
            dgn_ref[...] = jnp.zeros_like(dgn_ref)

        for h in range(HEADS):
            sl = slice(h * HD, (h + 1) * HD)
            o = o_ref[:, sl]
            rg = rg_ref[:, sl].astype(F32)
            d = d_ref[:, sl].astype(F32)
            gw = gn_ref[:, sl]
            r = lax.rsqrt(jnp.mean(o * o, axis=-1, keepdims=True) + EPS)
            n = o * r
            sg = _sigmoid(rg)
            si = rg * sg
            dn = d * gw * si
            dgn_ref[:, sl] += jnp.sum(d * n * si, axis=0, keepdims=True)
            drg_ref[:, sl] = (d * n * gw * sg * (1.0 + rg * (1.0 - sg))).astype(drg_ref.dtype)
            do_ref[:, sl] = (r * (dn - o * (r * r) * jnp.mean(dn * o, axis=-1, keepdims=True))).astype(BF16)

    row = pl.BlockSpec((tr, W), lambda i: (i, 0))
    vec = pl.BlockSpec((1, W), lambda i: (0, 0))
    return pl.pallas_call(
        body, name="ret_post_bwd", grid=(L // tr,),
        out_shape=[jax.ShapeDtypeStruct((L, W), BF16), jax.ShapeDtypeStruct((L, W), BF16),
                   jax.ShapeDtypeStruct((1, W), F32)],
        in_specs=[row, pl.BlockSpec((tr, W), lambda i: (i, C_RG // W)), vec, row],
        out_specs=[row, row, vec], compiler_params=_params(("arbitrary",)),
    )(o_ret, proj, gn, dcat)


RET_HEADS_PER_STEP = 4


def _ret_scans(name, arrays, scans, lg, out_dtype, single_buffered):
    L, W = arrays[0].shape
    nc = L // BLK - 1
    G = RET_HEADS_PER_STEP
    na, ns = len(arrays), len(scans)

    def body(*refs):
        in_refs, lg_ref, o_refs, s_ref = refs[:na], refs[na], refs[na + 1:na + 1 + ns], refs[-1]
        n = lax.broadcasted_iota(jnp.int32, (BLK, BLK), 0).astype(F32)
        m = lax.broadcasted_iota(jnp.int32, (BLK, BLK), 1).astype(F32)
        meta = (n < N_META) & (m < N_META)
        ways = {way for _, _, _, way in scans}
        consts = []
        for g in range(G):
            lgv = lg_ref[g, 0:1, :]
            c = dict(gl=jnp.exp(lgv * float(BLK)))
            if 'f' in ways:
                dmask = jnp.where(n >= m, jnp.exp(lgv * jnp.maximum(n - m, 0.0)), 0.0)
                c['f'] = dict(
                    dmask=dmask, dmask0=jnp.where(meta, dmask, 0.0), inter=jnp.exp(lgv * (n + 1.0)),
                    upd=jnp.exp(lgv * (float(BLK) - 1.0 - n)),
                    upd0=jnp.where(n < N_META, jnp.exp(lgv * jnp.maximum(float(N_META) - 1.0 - n, 0.0)), 0.0))
            if 'r' in ways:
                dmask = jnp.where(m >= n, jnp.exp(lgv * jnp.maximum(m - n, 0.0)), 0.0)
                c['r'] = dict(
                    dmask=dmask, dmask0=jnp.where(meta, dmask, 0.0), inter=jnp.exp(lgv * (float(BLK) - n)),
                    upd=jnp.exp(lgv * n),
                    inter0=jnp.where(n < N_META, jnp.exp(lgv * jnp.maximum(float(N_META) - n, 0.0)), 0.0))
            consts.append(c)

        def chunk(chunk_of):
            results = []
            for g in range(G):
                cols = slice(g * HD, (g + 1) * HD)
                for s, (qi, ki, vi, way) in enumerate(scans):
                    rows = pl.ds(pl.multiple_of(chunk_of[way] * BLK, BLK), BLK)
                    cg, state = consts[g][way], s_ref[s, g]
                    qc, kc, vc = in_refs[qi][rows, cols], in_refs[ki][rows, cols], in_refs[vi][rows, cols]
                    a = _dot(qc, kc, 'nt') * cg['dmask']
                    out = _dot(a.astype(BF16), vc, 'nn') + _dot(qc, state.astype(BF16), 'nn') * cg['inter']
                    new = state * consts[g]['gl'] + _dot((kc.astype(F32) * cg['upd']).astype(BF16), vc, 'tn')
                    results.append((s, g, rows, cols, out, new))
            for s, g, rows, cols, out, new in results:
                o_refs[s][rows, cols] = out.astype(out_dtype)
                s_ref[s, g] = new

        def first_chunk(s):
            qi, ki, vi, way = scans[s]
            for g in range(G):
                cols = slice(g * HD, (g + 1) * HD)
                cg = consts[g][way]
                q0, k0, v0 = in_refs[qi][0:BLK, cols], in_refs[ki][0:BLK, cols], in_refs[vi][0:BLK, cols]
                o0 = _dot((_dot(q0, k0, 'nt') * cg['dmask0']).astype(BF16), v0, 'nn')
                if way == 'r':
                    o0 = o0 + _dot(q0, s_ref[s, g].astype(BF16), 'nn') * cg['inter0']
                else:
                    s_ref[s, g] = _dot((k0.astype(F32) * cg['upd0']).astype(BF16), v0, 'tn')
                o_refs[s][0:BLK, cols] = o0.astype(out_dtype)

        s_ref[...] = jnp.zeros_like(s_ref)
        for s in range(ns):
            if scans[s][3] == 'f':
                first_chunk(s)

        def step(t, carry):
            chunk({'f': t + 1, 'r': nc - t})
            return carry

        lax.fori_loop(0, nc, step, 0)
        for s in range(ns):
            if scans[s][3] == 'r':
                first_chunk(s)

    mode = dict(pipeline_mode=pl.Buffered(1)) if single_buffered else {}
    col = pl.BlockSpec((L, G * HD), lambda h: (0, h), **mode)
    return pl.pallas_call(
        body, name=name, grid=(HEADS // G,), out_shape=[jax.ShapeDtypeStruct((L, W), out_dtype)] * ns,
        in_specs=[col] * na + [pl.BlockSpec((G, 8, HD), lambda h: (h, 0, 0))], out_specs=[col] * ns,
        scratch_shapes=[pltpu.VMEM((ns, G, HD, HD), F32)], compiler_params=_params(("parallel",)),
    )(*arrays, lg)


ATT_SCALE = (HD + ROPE) ** -0.5
LOG2E = 1.4426950408889634
Q_PRESCALE = ATT_SCALE * LOG2E
NEG = -1e30


ATT_TILE = 384
ATT_HEADS_PER_STEP = 8
ATT_BWD_HEADS_PER_STEP = 4


def _att_valid(nq, nk, row0, col0):
    r = lax.broadcasted_iota(jnp.int32, (nq, nk), 0) + row0
    c = lax.broadcasted_iota(jnp.int32, (nq, nk), 1) + col0
    return (c <= r) & ((c < N_META) | (c >= BLK))


def _store_rows(ref, g, first, col):
    wide = jnp.broadcast_to(col, (col.shape[0], HD))
    for c in range(col.shape[0] // BLK):
        ref[g, first + c] = jnp.transpose(wide[c * BLK:(c + 1) * BLK, :])[0:8, :]


def _load_row(ref, g, first, n):
    return jnp.concatenate([ref[g, first + c, 0:1, :] for c in range(n)], axis=1)


def _attn_fwd(qm, kn, krr, vm, comm=None):
    L = qm.shape[0]
    W = HEADS * HD
    T = _tile(L, ATT_TILE, BLK)
    nb = L // T
    G = ATT_HEADS_PER_STEP
    n_cm = comm.n if comm is not None else 0

    def body(*refs):
        q_ref, kn_ref, kr_ref, v_ref = refs[:4]
        o_ref, lse_ref = refs[4 + n_cm:6 + n_cm]
        m_sc, l_sc, acc_sc = refs[6 + 2 * n_cm:9 + 2 * n_cm]
        if comm is not None:
            cm_refs = (refs[4:4 + n_cm], refs[6 + n_cm:6 + 2 * n_cm], refs[9 + 2 * n_cm:])
            first, last = _grid_edges((HEADS // G, nb))

            @pl.when(first)
            def _():
                comm.start(*cm_refs)

        i = pl.program_id(1)
        m_sc[...] = jnp.full_like(m_sc, NEG)
        l_sc[...] = jnp.zeros_like(l_sc)
        acc_sc[...] = jnp.zeros_like(acc_sc)

        def tile(j, masked):
            rows = pl.ds(pl.multiple_of(j * T, T), T)
            kr = kr_ref[rows, :]
            valid = _att_valid(T, T, i * T, j * T) if masked else None
            ones = jnp.ones((T, HD), BF16)
            m_prev = [m_sc[g] for g in range(G)]
            l_prev = [l_sc[g] for g in range(G)]
            acc_prev = [acc_sc[g] for g in range(G)]
            m_new, l_new, acc_new = [], [], []
            for g in range(G):
                k = jnp.concatenate([kn_ref[rows, g * HD:(g + 1) * HD], kr], axis=1)
                s = _dot(q_ref[:, g * QH:(g + 1) * QH], k, 'nt')
                if masked:
                    s = jnp.where(valid, s, NEG)
                m_new.append(jnp.maximum(m_prev[g], jnp.max(s, axis=-1, keepdims=True)))
                p = jnp.exp2(s - m_new[g])
                alpha = jnp.exp2(m_prev[g] - m_new[g])
                pv = _dot(p.astype(BF16), jnp.concatenate([v_ref[rows, g * HD:(g + 1) * HD], ones], axis=1), 'nn')
                l_new.append(alpha * l_prev[g] + pv[:, HD:HD + 1])
                acc_new.append(alpha * acc_prev[g] + pv[:, 0:HD])
            for g in range(G):
                m_sc[g] = m_new[g]
                l_sc[g] = l_new[g]
                acc_sc[g] = acc_new[g]

        tile(0, True)

        def mid(j, carry):
            tile(j, False)
            return carry

        lax.fori_loop(1, i, mid, 0)

        @pl.when(i > 0)
        def _():
            tile(i, True)

        for g in range(G):
            l = l_sc[g]
            o_ref[:, g * HD:(g + 1) * HD] = (acc_sc[g] / l).astype(o_ref.dtype)
            _store_rows(lse_ref, g, 0, m_sc[g] + jnp.log(l) * LOG2E)

        if comm is not None:
            @pl.when(last)
            def _():
                comm.finish(*cm_refs)

    cm_specs = comm.specs if comm is not None else []
    res = pl.pallas_call(
        body, name="attn_fwd", grid=(HEADS // G, nb),
        out_shape=[jax.ShapeDtypeStruct((L, W), BF16), jax.ShapeDtypeStruct((HEADS, L // BLK, 8, HD), F32)]
        + (comm.out_shapes if comm is not None else []),
        in_specs=[pl.BlockSpec((T, G * QH), lambda h, i: (i, h)), pl.BlockSpec((L, G * HD), lambda h, i: (0, h)),
                  pl.BlockSpec((L, HD), lambda h, i: (0, 0)), pl.BlockSpec((L, G * HD), lambda h, i: (0, h))]
        + cm_specs,
        out_specs=[pl.BlockSpec((T, G * HD), lambda h, i: (i, h)),
                   pl.BlockSpec((G, T // BLK, 8, HD), lambda h, i: (h, i, 0, 0))] + cm_specs,
        scratch_shapes=[pltpu.VMEM((G, T, 1), F32), pltpu.VMEM((G, T, 1), F32), pltpu.VMEM((G, T, HD), F32)]
        + (comm.scratch if comm is not None else []),
        compiler_params=_params(("arbitrary", "arbitrary")),
    )(qm, kn, krr, vm, *(comm.arrays if comm is not None else []))
    return res[:2], res[2:]


def _attn_bwd(qm, kn, krr, vm, o, dcat, lse, comm=None):
    L = qm.shape[0]
    W = HEADS * HD
    T = _tile(L, ATT_TILE, BLK)
    nb, nr = L // T, T // BLK
    G = ATT_BWD_HEADS_PER_STEP
    n_cm = comm.n if comm is not None else 0

    def body(*refs):
        q_ref, kn_ref, kr_ref, v_ref, o_ref, do_ref, lse_ref = refs[:7]
        dq_ref, dkn_ref, dkr_ref, dv_ref = refs[7 + n_cm:11 + n_cm]
        dl_sc, dk_sc, dv_sc = refs[11 + 2 * n_cm:14 + 2 * n_cm]
        if comm is not None:
            cm_refs = (refs[7:7 + n_cm], refs[11 + n_cm:11 + 2 * n_cm], refs[14 + 2 * n_cm:])
            first, last = _grid_edges((HEADS // G, nb))

            @pl.when(first)
            def _():
                comm.start(*cm_refs)

        j = pl.program_id(1)
        qs = lambda g: slice(g * QH, (g + 1) * QH)
        hs = lambda g: slice(g * HD, (g + 1) * HD)

        @pl.when(j == 0)
        def _():
            dq_ref[...] = jnp.zeros_like(dq_ref)

            def rowsum(t, carry):
                rows = pl.ds(pl.multiple_of(t * T, T), T)
                for g in range(G):
                    _store_rows(dl_sc, g, t * nr, jnp.sum(
                        do_ref[rows, hs(g)].astype(F32) * o_ref[rows, hs(g)].astype(F32), axis=-1, keepdims=True))
                return carry

            lax.fori_loop(0, nb, rowsum, 0)

        kr = kr_ref[...]
        ks = [jnp.concatenate([kn_ref[:, hs(g)], kr], axis=1) for g in range(G)]
        vs = [v_ref[:, hs(g)] for g in range(G)]
        dk_sc[...] = jnp.zeros_like(dk_sc)
        dv_sc[...] = jnp.zeros_like(dv_sc)

        def tile(i, masked):
            rows = pl.ds(pl.multiple_of(i * T, T), T)
            if masked:
                key = lax.broadcasted_iota(jnp.int32, (T, T), 0) + j * T
                qry = lax.broadcasted_iota(jnp.int32, (T, T), 1) + i * T
                valid = (key <= qry) & ((key < N_META) | (key >= BLK))
            for g in range(G):
                q = q_ref[rows, qs(g)]
                do = do_ref[rows, hs(g)]
                s = _dot(ks[g], q, 'nt')
                if masked:
                    s = jnp.where(valid, s, NEG)
                p = jnp.exp2(s - _load_row(lse_ref, g, i * nr, nr))
                dv_sc[g] += _dot(p.astype(BF16), do, 'nn')
                ds = (p * (_dot(vs[g], do, 'nt') - _load_row(dl_sc, g, i * nr, nr))).astype(BF16)
                dk_sc[g] += _dot(ds, q, 'nn')
                dq_ref[rows, qs(g)] += _dot(ds, ks[g], 'tn')

        tile(j, True)

        def rest(masked):
            def step(i, carry):
                tile(i, masked)
                return carry
            lax.fori_loop(j + 1, nb, step, 0)

        @pl.when(j == 0)
        def _():
            rest(True)

        @pl.when(j > 0)
        def _():
            rest(False)

        for g in range(G):
            dk = dk_sc[g] * (1.0 / LOG2E)
            dkn_ref[:, hs(g)] = dk[:, 0:HD].astype(BF16)
            dkr_ref[:, hs(g)] = dk[:, HD:QH].astype(dkr_ref.dtype)
            dv_ref[:, hs(g)] = dv_sc[g].astype(BF16)

        if comm is not None:
            @pl.when(last)
            def _():
                comm.finish(*cm_refs)

    blk = pl.BlockSpec((T, G * HD), lambda h, j: (j, h))
    once = pl.Buffered(1)
    cm_specs = comm.specs if comm is not None else []
    res = pl.pallas_call(
        body, name="attn_bwd", grid=(HEADS // G, nb),
        out_shape=[jax.ShapeDtypeStruct((L, HEADS * QH), F32), jax.ShapeDtypeStruct((L, W), BF16),
                   jax.ShapeDtypeStruct((L, W), BF16), jax.ShapeDtypeStruct((L, W), BF16)]
        + (comm.out_shapes if comm is not None else []),
        in_specs=[pl.BlockSpec((L, G * QH), lambda h, j: (0, h), pipeline_mode=once), blk,
                  pl.BlockSpec((T, HD), lambda h, j: (j, 0)), blk,
                  pl.BlockSpec((L, G * HD), lambda h, j: (0, h), pipeline_mode=once),
                  pl.BlockSpec((L, G * HD), lambda h, j: (0, HEADS // G + h), pipeline_mode=once),
                  pl.BlockSpec((G, L // BLK, 8, HD), lambda h, j: (h, 0, 0, 0))] + cm_specs,
        out_specs=[pl.BlockSpec((L, G * QH), lambda h, j: (0, h), pipeline_mode=once), blk, blk, blk] + cm_specs,
        scratch_shapes=[pltpu.VMEM((G, L // BLK, 8, HD), F32), pltpu.VMEM((G, T, QH), F32),
                        pltpu.VMEM((G, T, HD), F32)]
        + (comm.scratch if comm is not None else []),
        compiler_params=_params(("arbitrary", "arbitrary")),
    )(qm, kn, krr, vm, o, dcat, lse, *(comm.arrays if comm is not None else []))
    return res[:4], res[4:]


def _unrope_q(dqm, tabs_m):
    L, W = dqm.shape
    tr = _tile(L, 384)

    def body(d_ref, cm_ref, sa_ref, sb_ref, out_ref):
        cm, sa, sb = cm_ref[...], sa_ref[...], sb_ref[...]
        for h in range(HEADS):
            out_ref[:, h * QH:h * QH + HD] = (d_ref[:, h * QH:h * QH + HD] * ATT_SCALE).astype(BF16)
            out_ref[:, h * QH + HD:(h + 1) * QH] = _rope_mla_t(d_ref[:, h * QH + HD:(h + 1) * QH] * ATT_SCALE, cm, sa,
                                                               sb).astype(BF16)

    row = pl.BlockSpec((tr, W), lambda i: (i, 0))
    tab = pl.BlockSpec((tr, HD), lambda i: (i, 0))
    return pl.pallas_call(
        body, name="unrope_q", grid=(L // tr,), out_shape=jax.ShapeDtypeStruct((L, W), BF16),
        in_specs=[row, tab, tab, tab], out_specs=row, compiler_params=_params(("parallel",)),
    )(dqm, *tabs_m)


def _q_up(cqn, wuq_p, tabs_m):
    L = cqn.shape[0]
    tm = _tile(L, 704)

    def ep(acc, cm, sa, sb):
        acc = acc * Q_PRESCALE
        parts = []
        for h in range(HEADS):
            parts.append(acc[:, h * QH:h * QH + HD])
            parts.append(_rope_mla(acc[:, h * QH + HD:(h + 1) * QH], cm, sa, sb))
        return (jnp.concatenate(parts, axis=1),)

    tab = pl.BlockSpec((tm, HD), lambda i, j: (i, 0))
    return _mm("mla_q_up", (L // tm, 1), ("parallel", "parallel"), None,
               [cqn, wuq_p], [pl.BlockSpec((tm, Q_RANK), lambda i, j: (i, 0)),
                              pl.BlockSpec((HEADS * QH, Q_RANK), lambda i, j: (0, 0))],
               [(0, 1, 'nt', 0)], [(tm, HEADS * QH)], list(tabs_m), [tab] * 3, ep,
               [jax.ShapeDtypeStruct((L, HEADS * QH), BF16)], [pl.BlockSpec((tm, HEADS * QH), lambda i, j: (i, 0))])[0]


def _mix_out(cat, w_out, h_in, post, next_norm):
    L, K = cat.shape
    D = w_out.shape[1]
    tm, tk = _tile(L, 384), K
    row = pl.BlockSpec((tm, D), lambda i, k: (i, 0))
    vec = pl.BlockSpec((1, D), lambda i, k: (0, 0))
    return _mm("mix_out", (L // tm, K // tk), ("parallel", "arbitrary"), 1,
               [cat, w_out], [pl.BlockSpec((tm, tk), lambda i, k: (i, k)), pl.BlockSpec((tk, D), lambda i, k: (k, 0))],
               [(0, 1, 'nn', 0)], [(tm, D)], [h_in, post, next_norm], [row, vec, vec], _resnorm_epilogue(1.0, True),
               [jax.ShapeDtypeStruct((L, D), F32)] * 2 + [jax.ShapeDtypeStruct((L, D), BF16)], [row, row, row])


ADAM_BLOCK_ELEMS = 512 * 704


def _adam_math(w, g, m, v):
    m = ADAM_B1 * m + (1.0 - ADAM_B1) * g
    v = ADAM_B2 * v + (1.0 - ADAM_B2) * (g * g)
    m_hat = m / (1.0 - ADAM_B1 ** ADAM_STEP)
    v_hat = v / (1.0 - ADAM_B2 ** ADAM_STEP)
    delta = -ADAM_LR * (m_hat / (jnp.sqrt(v_hat) + ADAM_EPS) + ADAM_WD * w)
    return delta, m, v


def _adam(name, w, m, v, g_slots=None, g=None, after=None):
    R, C = w.shape
    tr, tc = _tile(R, max(16, ADAM_BLOCK_ELEMS // C // 16 * 16), 16), C
    if tr * tc > ADAM_BLOCK_ELEMS:
        tr, tc = R, _tile(C, max(128, ADAM_BLOCK_ELEMS // R // 128 * 128), 128)
    from_slots = g_slots is not None

    def body(w_ref, m_ref, v_ref, g_ref, *rest):
        go_ref, d_ref, mo_ref, vo_ref = rest[-4:]
        if from_slots:
            grad = g_ref[0].astype(F32)
            for s in range(1, N_DEV):
                grad = grad + g_ref[s].astype(F32)
        else:
            grad = g_ref[...]
        delta, mn, vn = _adam_math(w_ref[...], grad, m_ref[...], v_ref[...])
        go_ref[...] = grad
        d_ref[...] = delta
        mo_ref[...] = mn
        vo_ref[...] = vn

    row = pl.BlockSpec((tr, tc), lambda i, j: (i, j))
    gspec = pl.BlockSpec((N_DEV, tr, tc), lambda i, j: (0, i, j)) if from_slots else row
    order = [] if after is None else [after]
    return pl.pallas_call(
        body, name=name, grid=(R // tr, C // tc), out_shape=[jax.ShapeDtypeStruct((R, C), F32)] * 4,
        in_specs=[row, row, row, gspec] + [pl.BlockSpec(memory_space=pl.ANY)] * len(order), out_specs=[row] * 4,
        compiler_params=_params(("parallel", "parallel")),
    )(w, m, v, g_slots if from_slots else g, *order)


def _unblock(gathered):
    n, r, c = gathered.shape
    return jnp.transpose(gathered, (1, 0, 2)).reshape(r, n * c)


def _reblock(full, c):
    r = full.shape[0]
    return jnp.transpose(full[:, :N_DEV * c].reshape(r, N_DEV, c), (1, 0, 2))


def _step(x, target, w, mom, vel):
    S, D = x.shape[1], x.shape[2]
    L = S + BLK
    def sq(a, n):
        if a.ndim == 2:
            return a
        if n in TRANSPOSED:
            a = jnp.swapaxes(a, 1, 2)
        return a.reshape(a.shape[1:])

    def unsq(o, n):
        o = o.reshape((1,) + o.shape)
        return jnp.swapaxes(o, 1, 2) if n in TRANSPOSED else o

    p = {n: sq(w[n], n) for n in WEIGHTS if n != 'meta_tokens'}
    gather = lambda names: _Exchange([p[n].astype(BF16) for n in names], False)
    scatter = lambda blocks: _Exchange(blocks, True)
    in_s, uq_s = p['w_in'].shape[0], p['mla_w_uq'].shape[0]
    assert uq_s == HD + ROPE and N_DEV == HEADS, "a w_uq shard is one head's columns"
    tabs = _rope_tables(L)
    tabs_m = tabs[2:]
    lg = jnp.broadcast_to(jnp.log(1.0 - 2.0 ** (-5.0 - jnp.arange(HEADS, dtype=F32)))[:, None, None], (HEADS, 8, HD))
    R = {}

    wg1, meta = _exchange("gather_first", [p['ffn1_w_gate'].astype(BF16), w['meta_tokens']], False)
    h0 = jnp.concatenate([_unblock(meta), jnp.zeros((BLK - N_META, D), F32), x[0]], axis=0)
    a1 = _norm_fwd(h0, p['ffn1_pre_norm'])
    g1, (wu1,) = _ffn_gate(a1, wg1, comm=gather(['ffn1_w_up']))
    (u1, hid1), (wd1,) = _ffn_up_gated(a1, wu1, g1, comm=gather(['ffn1_w_down']))
    (f1, h1, um), (w_in_g,) = _ffn_down(hid1, wd1, h0, p['ffn1_post_norm'], next_norm=p['mix_pre_norm'],
                                        comm=gather(['w_in']))

    w_in = jnp.pad(w_in_g.reshape(N_DEV * in_s, D), ((0, D_INP - N_DEV * in_s), (0, 0)))
    proj, (uq_g, uk_g, uv_g, wout_g) = _mm_nt("mix_in", [(um, w_in)], BF16, tn_target=1664,
                                              comm=gather(['mla_w_uq', 'mla_w_uk', 'mla_w_uv', 'w_out']))
    wuq = jnp.pad(uq_g, ((0, 0), (0, QH - uq_s), (0, 0))).reshape(HEADS * QH, Q_RANK)
    wuk, wuv, w_out = _unblock(uk_g), _unblock(uv_g), wout_g.reshape(-1, D)
    qr, kr, vr, cqn, ckvn, krr = _prep(proj, tabs, p['mla_q_norm'], p['mla_kv_norm'])
    qm = _q_up(cqn, wuq, tabs_m)
    kn = _mm_nn("mla_k_up", ckvn, wuk, BF16)
    vm = _mm_nn("mla_v_up", ckvn, wuv, BF16)
    (o_mla, lse), (wg2, wu2) = _attn_fwd(qm, kn, krr, vm, comm=gather(['ffn2_w_gate', 'ffn2_w_up']))
    o_ret, = _ret_scans("ret_fwd", [qr, kr, vr], [(0, 1, 2, 'f')], lg, F32, False)
    ret = _post(o_ret, proj, p['ret_group_norm'])
    cat = jnp.concatenate([ret, o_mla], axis=1)
    m, h2, a2 = _mix_out(cat, w_out, h1, p['mix_post_norm'], p['ffn2_pre_norm'])

    (g2, u2, hid2), (wd2,) = _ffn_up(a2, wg2, wu2, comm=gather(['ffn2_w_down']))
    f2, h3 = _ffn_down(hid2, wd2, h2, p['ffn2_post_norm'])
    dh3, loss_blk = _loss(h3, target[0])

    dsmall = {}
    df2, dsmall['ffn2_post_norm'] = _norm_bwd(f2, p['ffn2_post_norm'], dh3, None, 0.5, BF16)
    dg2, du2 = _ffn_dhid(df2, wd2, g2, u2)
    dwd2 = _ffn_dwd(hid2, df2)
    (dwg2, dwu2), (R['ffn2_w_down'],) = _ffn_dwgu(a2, dg2, du2, comm=scatter([dwd2]))
    da2, (R['ffn2_w_gate'],) = _ffn_da(dg2, du2, wg2, wu2, comm=scatter([dwg2]))
    dh2, dsmall['ffn2_pre_norm'] = _norm_bwd(h2, p['ffn2_pre_norm'], da2, dh3, 1.0, F32)

    dm, dsmall['mix_post_norm'] = _norm_bwd(m, p['mix_post_norm'], dh2, None, 1.0, BF16)
    dcat = _mm_nt("mix_dcat", [(dm, w_out)], BF16)
    dwout = _mm_tn("mix_dwout", cat, [dm])[0]
    do_ret, drg, dsmall['ret_group_norm'] = _post_bwd(o_ret, proj, p['ret_group_norm'], dcat)
    dqr, dkr, dvr = _ret_scans("ret_bwd", [do_ret, qr, kr, vr], [(0, 3, 2, 'f'), (3, 0, 1, 'r'), (2, 1, 0, 'r')],
                               lg, BF16, True)
    (dqm, dkn, dkr8, dvm), (R['ffn2_w_up'], R['w_out']) = _attn_bwd(
        qm, kn, krr, vm, o_mla, dcat, lse, comm=scatter([dwu2, dwout.reshape(N_DEV, -1, D)]))
    dqp = _unrope_q(dqm, tabs_m)
    dwuq = _mm_tn("mla_dwuq", dqp, [cqn])[0]
    dcqn = _mm_nn("mla_dcq", dqp, wuq, F32)
    dwuk, dwuv = _mm_tn("mla_dwukv", ckvn, [dkn, dvm])
    dckvn = _mm_nt("mla_dckv", [(dkn, wuk), (dvm, wuv)], F32)
    dproj, dsmall['mla_q_norm'], dsmall['mla_kv_norm'] = _prep_bwd(
        proj, dqr, dkr, dvr, drg, dcqn, dckvn, dkr8, tabs, p['mla_q_norm'], p['mla_kv_norm'])
    dwuq_b = dwuq.reshape(HEADS, QH, Q_RANK)[:, :uq_s]
    (dwin,), (R['mla_w_uq'], R['mla_w_uk'], R['mla_w_uv']) = _mm_tn(
        "mix_dwin", dproj, [um], comm=scatter([dwuq_b, _reblock(dwuk, p['mla_w_uk'].shape[1]),
                                               _reblock(dwuv, p['mla_w_uv'].shape[1])]))
    dwin_b = dwin[:N_DEV * in_s].reshape(N_DEV, in_s, D)
    half = D // 2
    dum, (r_win_a,) = _mm_nn("mix_du", dproj, w_in, F32, tn_target=512, comm=scatter([dwin_b[:, :, :half]]))
    dh1, dsmall['mix_pre_norm'] = _norm_bwd(h1, p['mix_pre_norm'], dum, dh2, 1.0, F32)

    df1, dsmall['ffn1_post_norm'] = _norm_bwd(f1, p['ffn1_post_norm'], dh1, None, 0.5, BF16)
    (dg1, du1), (r_win_b,) = _ffn_dhid(df1, wd1, g1, u1, comm=scatter([dwin_b[:, :, half:]]))
    R['w_in'] = jnp.concatenate([r_win_a, r_win_b], axis=2)
    dwd1 = _ffn_dwd(hid1, df1)
    (dwg1, dwu1), (R['ffn1_w_down'],) = _ffn_dwgu(a1, dg1, du1, comm=scatter([dwd1]))
    da1, (R['ffn1_w_gate'],) = _ffn_da(dg1, du1, wg1, wu1, comm=scatter([dwg1]))
    dh0, dsmall['ffn1_pre_norm'] = _norm_bwd(h0, p['ffn1_pre_norm'], da1, dh1, 1.0, F32)
    tail_sems_s, tail_sems_r, tail_src, tail_land, token = _scatter_start(dwu1)

    def slab(a):
        a = a.reshape(-1, 128)
        return jnp.pad(a, ((0, (-a.shape[0]) % 8), (0, 0)))

    slab_rows = lambda n: -(-(p[n].shape[-1] // 128) // 8) * 8
    packed = jnp.concatenate([slab(dsmall[n]) for n in SMALL] + [slab(dh0[:N_META]), loss_blk], axis=0)
    red = _allreduce_small(packed + token[0, 0])
    offs = sum(slab_rows(n) for n in SMALL)
    n_small = offs
    gmeta_full = red[offs:offs + N_META * D // 128].reshape(N_META, D)
    offs += N_META * D // 128
    loss = red[offs, 0]

    grad, delta, new_m, new_v = {}, {}, {}, {}
    meanwhile = []
    for n in BIG:
        if n == 'ffn1_w_up':
            continue
        outs = _adam("adam_" + n, p[n], sq(mom[n], n), sq(vel[n], n), g_slots=R[n], after=token)
        meanwhile.append(outs[0])
        grad[n], delta[n], new_m[n], new_v[n] = [unsq(o, n) for o in outs]
    pack = lambda d: jnp.concatenate([slab(d[n]) for n in SMALL], axis=0)
    outs = _adam("adam_small", pack(w), pack(mom), pack(vel), g=red[:n_small])
    meanwhile.append(outs[0])
    offs = 0
    for n in SMALL:
        r = p[n].shape[-1] // 128
        grad[n], delta[n], new_m[n], new_v[n] = [o[offs:offs + r].reshape(w[n].shape) for o in outs]
        offs += slab_rows(n)
    dev = 4 * lax.axis_index("x") + 2 * lax.axis_index("y") + lax.axis_index("c")
    mcols = w['meta_tokens'].shape[1]
    gmeta = lax.dynamic_slice(gmeta_full, (0, dev * mcols), (N_META, mcols))
    outs = _adam("adam_meta", w['meta_tokens'], mom['meta_tokens'], vel['meta_tokens'], g=gmeta)
    grad['meta_tokens'], delta['meta_tokens'], new_m['meta_tokens'], new_v['meta_tokens'] = outs
    meanwhile.append(outs[0])
    n = 'ffn1_w_up'
    slots = _scatter_wait(tail_sems_s, tail_sems_r, tail_src, tail_land, meanwhile)
    outs = _adam("adam_" + n, p[n], sq(mom[n], n), sq(vel[n], n), g_slots=slots)
    grad[n], delta[n], new_m[n], new_v[n] = [unsq(o, n) for o in outs]

    return (loss, dh0[BLK:][None], *[grad[n] for n in WEIGHTS], *[delta[n] for n in WEIGHTS],
            *[new_m[n] for n in WEIGHTS], *[new_v[n] for n in WEIGHTS])


def kernel(x, meta_tokens, ffn1_pre_norm, ffn1_w_gate, ffn1_w_up, ffn1_w_down, ffn1_post_norm, mix_pre_norm, w_in, ret_group_norm, mla_q_norm, mla_w_uq, mla_kv_norm, mla_w_uk, mla_w_uv, w_out, mix_post_norm, ffn2_pre_norm, ffn2_w_gate, ffn2_w_up, ffn2_w_down, ffn2_post_norm, loss_target, m_meta_tokens, m_ffn1_pre_norm, m_ffn1_w_gate, m_ffn1_w_up, m_ffn1_w_down, m_ffn1_post_norm, m_mix_pre_norm, m_w_in, m_ret_group_norm, m_mla_q_norm, m_mla_w_uq, m_mla_kv_norm, m_mla_w_uk, m_mla_w_uv, m_w_out, m_mix_post_norm, m_ffn2_pre_norm, m_ffn2_w_gate, m_ffn2_w_up, m_ffn2_w_down, m_ffn2_post_norm, v_meta_tokens, v_ffn1_pre_norm, v_ffn1_w_gate, v_ffn1_w_up, v_ffn1_w_down, v_ffn1_post_norm, v_mix_pre_norm, v_w_in, v_ret_group_norm, v_mla_q_norm, v_mla_w_uq, v_mla_kv_norm, v_mla_w_uk, v_mla_w_uv, v_w_out, v_mix_post_norm, v_ffn2_pre_norm, v_ffn2_w_gate, v_ffn2_w_up, v_ffn2_w_down, v_ffn2_post_norm):
    w = dict(zip(WEIGHTS, (meta_tokens, ffn1_pre_norm, ffn1_w_gate, ffn1_w_up, ffn1_w_down, ffn1_post_norm,
                           mix_pre_norm, w_in, ret_group_norm, mla_q_norm, mla_w_uq, mla_kv_norm, mla_w_uk, mla_w_uv,
                           w_out, mix_post_norm, ffn2_pre_norm, ffn2_w_gate, ffn2_w_up, ffn2_w_down, ffn2_post_norm)))
    mom = dict(zip(WEIGHTS, (m_meta_tokens, m_ffn1_pre_norm, m_ffn1_w_gate, m_ffn1_w_up, m_ffn1_w_down,
                             m_ffn1_post_norm, m_mix_pre_norm, m_w_in, m_ret_group_norm, m_mla_q_norm, m_mla_w_uq,
                             m_mla_kv_norm, m_mla_w_uk, m_mla_w_uv, m_w_out, m_mix_post_norm, m_ffn2_pre_norm,
                             m_ffn2_w_gate, m_ffn2_w_up, m_ffn2_w_down, m_ffn2_post_norm)))
    vel = dict(zip(WEIGHTS, (v_meta_tokens, v_ffn1_pre_norm, v_ffn1_w_gate, v_ffn1_w_up, v_ffn1_w_down,
                             v_ffn1_post_norm, v_mix_pre_norm, v_w_in, v_ret_group_norm, v_mla_q_norm, v_mla_w_uq,
                             v_mla_kv_norm, v_mla_w_uk, v_mla_w_uv, v_w_out, v_mix_post_norm, v_ffn2_pre_norm,
                             v_ffn2_w_gate, v_ffn2_w_up, v_ffn2_w_down, v_ffn2_post_norm)))
    return _step(x, loss_target, w, mom, vel)
```

```python
import functools
import math

import jax
import jax.numpy as jnp
from jax import lax
from jax.experimental import pallas as pl
from jax.experimental.pallas import tpu as pltpu

N_DEV = 8
N_META = 16
BLK = 128
HEADS = 8
HD = 128
ROPE = 64
Q_RANK = 512
KV_RANK = 256
QH = 2 * HD
D_INP = 4 * HEADS * HD + Q_RANK + KV_RANK + BLK
ROPE_THETA = 10000.0
EPS = 1e-6
ADAM_LR = 0.001
ADAM_B1 = 0.9
ADAM_B2 = 0.999
ADAM_EPS = 1e-08
ADAM_WD = 0.01
ADAM_STEP = 10
V7X_VMEM_LIMIT = 48 * 1024 * 1024
MESH = pl.DeviceIdType.MESH
F32 = jnp.float32
BF16 = jnp.bfloat16

WEIGHTS = ['meta_tokens', 'ffn1_pre_norm', 'ffn1_w_gate', 'ffn1_w_up', 'ffn1_w_down', 'ffn1_post_norm',
           'mix_pre_norm', 'w_in', 'ret_group_norm', 'mla_q_norm', 'mla_w_uq', 'mla_kv_norm', 'mla_w_uk',
           'mla_w_uv', 'w_out', 'mix_post_norm', 'ffn2_pre_norm', 'ffn2_w_gate', 'ffn2_w_up', 'ffn2_w_down',
           'ffn2_post_norm']
SMALL = ['ffn1_pre_norm', 'ffn1_post_norm', 'mix_pre_norm', 'ret_group_norm', 'mla_q_norm', 'mla_kv_norm',
         'mix_post_norm', 'ffn2_pre_norm', 'ffn2_post_norm']
TRANSPOSED = ('ffn1_w_gate', 'ffn1_w_up', 'ffn2_w_gate', 'ffn2_w_up', 'w_in', 'mla_w_uq')
BIG = ['ffn1_w_gate', 'ffn1_w_up', 'ffn1_w_down', 'w_in', 'mla_w_uq', 'mla_w_uk', 'mla_w_uv', 'w_out',
       'ffn2_w_gate', 'ffn2_w_up', 'ffn2_w_down']

_DIMS = {'nn': (((1,), (0,)), ((), ())), 'nt': (((1,), (1,)), ((), ())), 'tn': (((0,), (0,)), ((), ()))}


def _tile(n, target, mult=16):
    best = None
    for t in range(mult, min(n, target) + 1, mult):
        if n % t == 0:
            best = t
    return best if best is not None else n


def _params(sem):
    return pltpu.CompilerParams(dimension_semantics=sem, vmem_limit_bytes=V7X_VMEM_LIMIT)


def _dot(a, b, dims):
    return lax.dot_general(a, b, _DIMS[dims], preferred_element_type=F32)


def _sigmoid(x):
    return 0.5 * jnp.tanh(0.5 * x) + 0.5


def _me_and_peers():
    x, y, c = lax.axis_index("x"), lax.axis_index("y"), lax.axis_index("c")

    def peer(j):
        px = 1 - x if (j >> 2) & 1 else x
        py = 1 - y if (j >> 1) & 1 else y
        pc = 1 - c if j & 1 else c
        return (px, py, pc), 4 * px + 2 * py + pc

    return 4 * x + 2 * y + c, peer


class _Exchange:
    def __init__(self, arrays, per_peer):
        self.arrays = list(arrays)
        self.per_peer = per_peer
        self.n = len(self.arrays)
        self.out_shapes = [jax.ShapeDtypeStruct((N_DEV,) + tuple(a.shape[1:] if per_peer else a.shape), a.dtype)
                           for a in self.arrays]
        self.specs = [pl.BlockSpec(memory_space=pl.ANY)] * self.n
        self.scratch = [pltpu.SemaphoreType.DMA((7 * self.n,)), pltpu.SemaphoreType.DMA((7 * self.n,)),
                        pltpu.SemaphoreType.DMA((self.n,))]

    def _copies(self, src, dst, sems):
        send_sems, recv_sems, local_sems = sems
        me, peer = _me_and_peers()
        sib, _ = peer(1)
        local, sends, recvs, passes = [], {}, {}, {}
        for k in range(self.n):
            own = src[k].at[me] if self.per_peer else src[k]
            local.append(pltpu.make_async_copy(own, dst[k].at[me], local_sems.at[k]))
            for j in range(1, N_DEV):
                pid, pidx = peer(j)
                out = src[k].at[pidx] if self.per_peer else src[k]
                sem = dict(send_sem=send_sems.at[k * 7 + j - 1], recv_sem=recv_sems.at[k * 7 + j - 1])
                recvs[k, j] = pltpu.make_async_remote_copy(src_ref=out, dst_ref=dst[k].at[pidx], device_id=pid,
                                                           device_id_type=MESH, **sem)
                if self.per_peer or j in (1, 2, 4, 6):
                    sends[k, j] = pltpu.make_async_remote_copy(src_ref=out, dst_ref=dst[k].at[me], device_id=pid,
                                                               device_id_type=MESH, **sem)
                else:
                    _, origin = peer(j ^ 1)
                    passes[k, j ^ 1] = pltpu.make_async_remote_copy(
                        src_ref=dst[k].at[origin], dst_ref=dst[k].at[origin], device_id=sib, device_id_type=MESH, **sem)
        return local, sends, recvs, passes

    def start(self, src, dst, sems):
        local, sends, _, _ = self._copies(src, dst, sems)
        for cp in local + list(sends.values()):
            cp.start()

    def finish(self, src, dst, sems):
        local, sends, recvs, passes = self._copies(src, dst, sems)
        for key, cp in passes.items():
            recvs[key].wait_recv()
            cp.start()
        for key, cp in recvs.items():
            if key not in passes:
                cp.wait_recv()
        for cp in list(sends.values()) + list(passes.values()):
            cp.wait_send()
        for cp in local:
            cp.wait()


def _grid_edges(grid):
    first, last = None, None
    for a, n in enumerate(grid):
        f, l = pl.program_id(a) == 0, pl.program_id(a) == n - 1
        first = f if first is None else first & f
        last = l if last is None else last & l
    return first, last


def _exchange(name, arrays, per_peer):
    ex = _Exchange(arrays, per_peer)
    n = ex.n

    def body(*refs):
        ex.start(refs[:n], refs[n:2 * n], refs[2 * n:])
        ex.finish(refs[:n], refs[n:2 * n], refs[2 * n:])

    return pl.pallas_call(body, name=name, out_shape=ex.out_shapes, in_specs=ex.specs, out_specs=ex.specs,
                          scratch_shapes=ex.scratch)(*arrays)


def _scatter_start(blocks):
    def body(src_ref, land_ref, send_sems, recv_sems, src_thru, land_thru, token, local_sem):
        me, peer = _me_and_peers()
        local = pltpu.make_async_copy(src_ref.at[me], land_ref.at[me], local_sem)
        local.start()
        for j in range(1, N_DEV):
            pid, pidx = peer(j)
            pltpu.make_async_remote_copy(src_ref=src_ref.at[pidx], dst_ref=land_ref.at[me],
                                         send_sem=send_sems.at[j - 1], recv_sem=recv_sems.at[j - 1],
                                         device_id=pid, device_id_type=MESH).start()
        local.wait()
        token[...] = jnp.zeros_like(token)

    hbm = pl.BlockSpec(memory_space=pltpu.HBM)
    sem = pl.BlockSpec(memory_space=pltpu.SEMAPHORE)
    return pl.pallas_call(
        body, name="scatter_tail_start",
        out_shape=(pltpu.SemaphoreType.DMA((7,)), pltpu.SemaphoreType.DMA((7,)), pltpu.HBM(blocks.shape, blocks.dtype),
                   pltpu.HBM(blocks.shape, blocks.dtype), jax.ShapeDtypeStruct((8, 128), F32)),
        in_specs=(hbm, hbm), out_specs=(sem, sem, hbm, hbm, pl.BlockSpec(memory_space=pltpu.VMEM)),
        input_output_aliases={0: 2, 1: 3}, scratch_shapes=[pltpu.SemaphoreType.DMA],
        compiler_params=pltpu.CompilerParams(has_side_effects=pltpu.SideEffectType.DATAFLOW_SIDE_EFFECTING),
    )(pltpu.with_memory_space_constraint(blocks, pltpu.HBM),
      pltpu.with_memory_space_constraint(lax.empty(blocks.shape, blocks.dtype), pltpu.HBM))


def _scatter_wait(send_sems, recv_sems, src_thru, land_thru, after):
    n_after = len(after)

    def body(src_ref, land_ref, send_sems, recv_sems, *rest):
        me, peer = _me_and_peers()
        for j in range(1, N_DEV):
            pid, pidx = peer(j)
            cp = pltpu.make_async_remote_copy(src_ref=src_ref.at[pidx], dst_ref=land_ref.at[pidx],
                                              send_sem=send_sems.at[j - 1], recv_sem=recv_sems.at[j - 1],
                                              device_id=pid, device_id_type=MESH)
            cp.wait_send()
            cp.wait_recv()

    hbm = pl.BlockSpec(memory_space=pltpu.HBM)
    sem = pl.BlockSpec(memory_space=pltpu.SEMAPHORE)
    return pl.pallas_call(
        body, name="scatter_tail_wait",
        out_shape=(pltpu.HBM(src_thru.shape, src_thru.dtype), pltpu.HBM(land_thru.shape, land_thru.dtype)),
        in_specs=(hbm, hbm, sem, sem) + (pl.BlockSpec(memory_space=pl.ANY),) * n_after, out_specs=(hbm, hbm),
        input_output_aliases={0: 0, 1: 1},
        compiler_params=pltpu.CompilerParams(has_side_effects=pltpu.SideEffectType.DATAFLOW_SIDE_EFFECTING),
    )(src_thru, land_thru, send_sems, recv_sems, *after)[1]


def _allreduce_small(v):
    rows = v.shape[0]

    def body(v_ref, out_ref, buf, send_sems, recv_sems):
        me, peer = _me_and_peers()
        buf[pl.ds(me, 1)] = v_ref[...][None]
        sends = []
        for j in range(1, N_DEV):
            pid, _ = peer(j)
            cp = pltpu.make_async_remote_copy(src_ref=v_ref, dst_ref=buf.at[me], send_sem=send_sems.at[j - 1],
                                              recv_sem=recv_sems.at[j - 1], device_id=pid, device_id_type=MESH)
            cp.start()
            sends.append(cp)
        for j in range(1, N_DEV):
            pid, pidx = peer(j)
            pltpu.make_async_remote_copy(src_ref=v_ref, dst_ref=buf.at[pidx], send_sem=send_sems.at[j - 1],
                                         recv_sem=recv_sems.at[j - 1], device_id=pid,
                                         device_id_type=MESH).wait_recv()
        for cp in sends:
            cp.wait_send()
        acc = buf[0]
        for s in range(1, N_DEV):
            acc = acc + buf[s]
        out_ref[...] = acc

    vm = pl.BlockSpec(memory_space=pltpu.VMEM)
    return pl.pallas_call(
        body, name="allreduce_small", out_shape=jax.ShapeDtypeStruct(v.shape, F32),
        in_specs=[vm], out_specs=vm,
        scratch_shapes=[pltpu.VMEM((N_DEV, rows, 128), F32), pltpu.SemaphoreType.DMA((7,)),
                        pltpu.SemaphoreType.DMA((7,))],
    )(v)


def _mm(name, grid, sem, k_axis, ops, op_specs, pairs, acc_shapes, extras, extra_specs, epilogue, outs, out_specs,
        comm=None):
    n_op, n_ex, n_out = len(ops), len(extras), len(outs)
    nk = grid[k_axis] if k_axis is not None else 1
    n_acc = len(acc_shapes) if nk > 1 else 0
    n_cm = comm.n if comm is not None else 0

    def body(*refs):
        op_refs = refs[:n_op]
        ex_refs = refs[n_op:n_op + n_ex]
        n_in = n_op + n_ex + n_cm
        out_refs = refs[n_in:n_in + n_out]
        acc_refs = refs[n_in + n_out + n_cm:n_in + n_out + n_cm + n_acc]
        if comm is not None:
            cm_refs = (refs[n_op + n_ex:n_in], refs[n_in + n_out:n_in + n_out + n_cm],
                       refs[n_in + n_out + n_cm + n_acc:])
            first, last = _grid_edges(grid)

            @pl.when(first)
            def _():
                comm.start(*cm_refs)

        def finish(vals):
            res = epilogue(*vals, *[e[...] for e in ex_refs])
            for o, r in zip(out_refs, res):
                o[...] = r.astype(o.dtype)

        if nk == 1:
            parts = [None] * len(acc_shapes)
            for li, ri, dims, ai in pairs:
                d = _dot(op_refs[li][...], op_refs[ri][...], dims)
                parts[ai] = d if parts[ai] is None else parts[ai] + d
            finish(parts)
        else:
            k = pl.program_id(k_axis)

            @pl.when(k == 0)
            def _():
                for a in acc_refs:
                    a[...] = jnp.zeros_like(a)

            for li, ri, dims, ai in pairs:
                acc_refs[ai][...] += _dot(op_refs[li][...], op_refs[ri][...], dims)

            @pl.when(k == nk - 1)
            def _():
                finish([a[...] for a in acc_refs])

        if comm is not None:
            @pl.when(last)
            def _():
                comm.finish(*cm_refs)

    scratch = [pltpu.VMEM(s, F32) for s in acc_shapes] if nk > 1 else []
    if comm is None:
        return pl.pallas_call(
            body, name=name, grid=grid, out_shape=outs,
            in_specs=list(op_specs) + list(extra_specs), out_specs=list(out_specs),
            scratch_shapes=scratch, compiler_params=_params(sem),
        )(*ops, *extras)
    res = pl.pallas_call(
        body, name=name, grid=grid, out_shape=list(outs) + comm.out_shapes,
        in_specs=list(op_specs) + list(extra_specs) + comm.specs, out_specs=list(out_specs) + comm.specs,
        scratch_shapes=scratch + comm.scratch, compiler_params=_params(("arbitrary",) * len(grid)),
    )(*ops, *extras, *comm.arrays)
    return res[:n_out], res[n_out:]


def _with_comm(res, comm, pick):
    if comm is None:
        return pick(res)
    return pick(res[0]), res[1]


def _mm_nn(name, a, w, out_dtype, tm_target=704, tn_target=1664, epilogue=None, extras=(), extra_specs=(), comm=None):
    L, K = a.shape
    N = w.shape[1]
    tm, tn = _tile(L, tm_target), _tile(N, tn_target, 128)
    ep = epilogue if epilogue is not None else (lambda acc: (acc,))
    res = _mm(name, (L // tm, N // tn), ("parallel", "parallel"), None,
              [a, w], [pl.BlockSpec((tm, K), lambda i, j: (i, 0)), pl.BlockSpec((K, tn), lambda i, j: (0, j))],
              [(0, 1, 'nn', 0)], [(tm, tn)], list(extras), list(extra_specs), ep,
              [jax.ShapeDtypeStruct((L, N), out_dtype)], [pl.BlockSpec((tm, tn), lambda i, j: (i, j))], comm=comm)
    return _with_comm(res, comm, lambda o: o[0])


def _mm_nt(name, pairs_aw, out_dtype, tm_target=704, tn_target=512, comm=None):
    L = pairs_aw[0][0].shape[0]
    N = pairs_aw[0][1].shape[0]
    tm, tn = _tile(L, tm_target), _tile(N, tn_target, 128)
    ops, specs, pairs = [], [], []
    for t, (a, w) in enumerate(pairs_aw):
        K = a.shape[1]
        ops += [a, w]
        specs += [pl.BlockSpec((tm, K), lambda i, j: (i, 0)), pl.BlockSpec((tn, K), lambda i, j: (j, 0))]
        pairs.append((2 * t, 2 * t + 1, 'nt', 0))
    res = _mm(name, (L // tm, N // tn), ("parallel", "parallel"), None, ops, specs, pairs, [(tm, tn)], [], [],
              lambda acc: (acc,), [jax.ShapeDtypeStruct((L, N), out_dtype)],
              [pl.BlockSpec((tm, tn), lambda i, j: (i, j))], comm=comm)
    return _with_comm(res, comm, lambda o: o[0])


def _mm_tn(name, a, bs, out_dtype=BF16, tk_target=1408, tn_target=1664, tm_target=2048, comm=None):
    L, M = a.shape
    N = bs[0].shape[1]
    tk, tn, tm = _tile(L, tk_target), _tile(N, tn_target, 128), _tile(M, tm_target, 128)
    nb = len(bs)
    ops = [a] + list(bs)
    specs = [pl.BlockSpec((tk, tm), lambda i, j, k: (k, i))] + [pl.BlockSpec((tk, tn), lambda i, j, k: (k, j))] * nb
    res = _mm(name, (M // tm, N // tn, L // tk), ("parallel", "parallel", "arbitrary"), 2, ops, specs,
              [(0, 1 + t, 'tn', t) for t in range(nb)], [(tm, tn)] * nb, [], [], lambda *acc: acc,
              [jax.ShapeDtypeStruct((M, N), out_dtype)] * nb,
              [pl.BlockSpec((tm, tn), lambda i, j, k: (i, j))] * nb, comm=comm)
    return _with_comm(res, comm, lambda o: o)


def _norm_fwd(x, w):
    L, D = x.shape
    tr = _tile(L, 512)

    def body(x_ref, w_ref, y_ref):
        v = x_ref[...]
        r = lax.rsqrt(jnp.mean(v * v, axis=-1, keepdims=True) + EPS)
        y_ref[...] = (v * r * w_ref[...]).astype(y_ref.dtype)

    return pl.pallas_call(
        body, name="norm_fwd", grid=(L // tr,), out_shape=jax.ShapeDtypeStruct((L, D), BF16),
        in_specs=[pl.BlockSpec((tr, D), lambda i: (i, 0)), pl.BlockSpec((1, D), lambda i: (0, 0))],
        out_specs=pl.BlockSpec((tr, D), lambda i: (i, 0)), compiler_params=_params(("parallel",)),
    )(x, w)


def _norm_bwd_math(x, w, dy):
    r = lax.rsqrt(jnp.mean(x * x, axis=-1, keepdims=True) + EPS)
    gy = dy * w
    dx = r * (gy - x * (r * r) * jnp.mean(gy * x, axis=-1, keepdims=True))
    dw = jnp.sum(dy * x * r, axis=0, keepdims=True)
    return dx, dw


def _norm_bwd(x, w, dy, res, scale, out_dtype):
    L, D = x.shape
    tr = _tile(L, 384)
    has_res = res is not None

    def body(*refs):
        x_ref, w_ref, dy_ref = refs[:3]
        res_ref = refs[3] if has_res else None
        dx_ref, dw_ref = refs[-2:]
        dx, dw = _norm_bwd_math(x_ref[...], w_ref[...], dy_ref[...].astype(F32))
        dx = scale * dx
        if has_res:
            dx = dx + res_ref[...]
        dx_ref[...] = dx.astype(dx_ref.dtype)

        @pl.when(pl.program_id(0) == 0)
        def _():
            dw_ref[...] = jnp.zeros_like(dw_ref)

        dw_ref[...] += scale * dw

    row = pl.BlockSpec((tr, D), lambda i: (i, 0))
    vec = pl.BlockSpec((1, D), lambda i: (0, 0))
    return pl.pallas_call(
        body, name="norm_bwd", grid=(L // tr,),
        out_shape=[jax.ShapeDtypeStruct((L, D), out_dtype), jax.ShapeDtypeStruct((1, D), F32)],
        in_specs=[row, vec, row] + ([row] if has_res else []), out_specs=[row, vec],
        compiler_params=_params(("arbitrary",)),
    )(*([x, w, dy] + ([res] if has_res else [])))


def _loss(h, target):
    L, D = h.shape

    def body(h_ref, t_ref, dh_ref, loss_ref):
        i = pl.program_id(0)

        @pl.when(i == 0)
        def _():
            dh_ref[...] = jnp.zeros_like(dh_ref)
            loss_ref[...] = jnp.zeros_like(loss_ref)

        @pl.when(i > 0)
        def _():
            diff = h_ref[...] - t_ref[...]
            dh_ref[...] = diff * (1.0 / D)
            loss_ref[...] += 0.5 * jnp.sum(diff * diff) * (1.0 / D)

    return pl.pallas_call(
        body, name="loss", grid=(L // BLK,),
        out_shape=[jax.ShapeDtypeStruct((L, D), F32), jax.ShapeDtypeStruct((8, 128), F32)],
        in_specs=[pl.BlockSpec((BLK, D), lambda i: (i, 0)),
                  pl.BlockSpec((BLK, D), lambda i: (jnp.maximum(i - 1, 0), 0))],
        out_specs=[pl.BlockSpec((BLK, D), lambda i: (i, 0)), pl.BlockSpec((8, 128), lambda i: (0, 0))],
        compiler_params=_params(("arbitrary",)),
    )(h, target)


def _ffn_up(a, wg, wu, comm=None):
    L, D = a.shape
    F = wg.shape[1]
    tm = _tile(L, 704)

    def ep(g, u):
        return g, u, g * _sigmoid(g) * u

    hspec = pl.BlockSpec((None, tm, F), lambda i, j: (j, i, 0))
    wspec = pl.BlockSpec((None, F, D), lambda i, j: (j, 0, 0))
    res = _mm("ffn_up", (L // tm, N_DEV), ("parallel", "parallel"), None,
              [a, wg, wu], [pl.BlockSpec((tm, D), lambda i, j: (i, 0)), wspec, wspec],
              [(0, 1, 'nt', 0), (0, 2, 'nt', 1)], [(tm, F)] * 2, [], [], ep,
              [jax.ShapeDtypeStruct((N_DEV, L, F), BF16)] * 3, [hspec] * 3, comm=comm)
    return _with_comm(res, comm, lambda o: o)


def _ffn_gate(a, wg, comm=None):
    L, D = a.shape
    F = wg.shape[1]
    tm = _tile(L, 704)
    res = _mm("ffn_gate", (L // tm, N_DEV), ("parallel", "parallel"), None,
              [a, wg], [pl.BlockSpec((tm, D), lambda i, j: (i, 0)), pl.BlockSpec((None, F, D), lambda i, j: (j, 0, 0))],
              [(0, 1, 'nt', 0)], [(tm, F)], [], [], lambda g: (g,),
              [jax.ShapeDtypeStruct((N_DEV, L, F), BF16)], [pl.BlockSpec((None, tm, F), lambda i, j: (j, i, 0))],
              comm=comm)
    return _with_comm(res, comm, lambda o: o[0])


def _ffn_up_gated(a, wu, g, comm=None):
    L, D = a.shape
    F = wu.shape[1]
    tm = _tile(L, 704)

    def ep(u, g_):
        g32 = g_.astype(F32)
        return u, g32 * _sigmoid(g32) * u

    hspec = pl.BlockSpec((None, tm, F), lambda i, j: (j, i, 0))
    res = _mm("ffn_up_gated", (L // tm, N_DEV), ("parallel", "parallel"), None,
              [a, wu], [pl.BlockSpec((tm, D), lambda i, j: (i, 0)), pl.BlockSpec((None, F, D), lambda i, j: (j, 0, 0))],
              [(0, 1, 'nt', 0)], [(tm, F)], [g], [hspec], ep,
              [jax.ShapeDtypeStruct((N_DEV, L, F), BF16)] * 2, [hspec, hspec], comm=comm)
    return _with_comm(res, comm, lambda o: o)


def _resnorm_epilogue(scale, with_next):
    def ep(acc, h, w, *w_next):
        r = lax.rsqrt(jnp.mean(acc * acc, axis=-1, keepdims=True) + EPS)
        h_out = h + scale * (acc * r * w)
        if not with_next:
            return acc, h_out
        r_next = lax.rsqrt(jnp.mean(h_out * h_out, axis=-1, keepdims=True) + EPS)
        return acc, h_out, h_out * r_next * w_next[0]
    return ep


def _ffn_down(hid, wd, h_in, post, next_norm=None, comm=None):
    _, L, F = hid.shape
    D = wd.shape[2]
    tm = _tile(L, 528)
    row = pl.BlockSpec((tm, D), lambda i, j: (i, 0))
    vec = pl.BlockSpec((1, D), lambda i, j: (0, 0))
    nxt = [] if next_norm is None else [next_norm]
    res = _mm("ffn_down", (L // tm, N_DEV), ("parallel", "arbitrary"), 1,
              [hid, wd], [pl.BlockSpec((None, tm, F), lambda i, j: (j, i, 0)),
                          pl.BlockSpec((None, F, D), lambda i, j: (j, 0, 0))],
              [(0, 1, 'nn', 0)], [(tm, D)], [h_in, post] + nxt, [row, vec] + [vec] * len(nxt),
              _resnorm_epilogue(0.5, bool(nxt)),
              [jax.ShapeDtypeStruct((L, D), F32)] * 2 + [jax.ShapeDtypeStruct((L, D), BF16)] * len(nxt),
              [row] * (2 + len(nxt)), comm=comm)
    return _with_comm(res, comm, lambda o: o)


def _ffn_dhid(df, wd, g, u, comm=None):
    L, D = df.shape
    F = wd.shape[1]
    tm = _tile(L, 704)

    def ep(dhid, g_, u_):
        g32, u32 = g_.astype(F32), u_.astype(F32)
        sg = _sigmoid(g32)
        return dhid * u32 * sg * (1.0 + g32 * (1.0 - sg)), dhid * g32 * sg

    hspec = pl.BlockSpec((None, tm, F), lambda i, j: (j, i, 0))
    res = _mm("ffn_dhid", (L // tm, N_DEV), ("parallel", "parallel"), None,
              [df, wd], [pl.BlockSpec((tm, D), lambda i, j: (i, 0)),
                         pl.BlockSpec((None, F, D), lambda i, j: (j, 0, 0))],
              [(0, 1, 'nt', 0)], [(tm, F)], [g, u], [hspec, hspec], ep,
              [jax.ShapeDtypeStruct((N_DEV, L, F), BF16)] * 2, [hspec, hspec], comm=comm)
    return _with_comm(res, comm, lambda o: o)


def _ffn_dwd(hid, df, comm=None):
    _, L, F = hid.shape
    D = df.shape[1]
    tk = _tile(L, 1408)
    res = _mm("ffn_dwd", (N_DEV, L // tk), ("parallel", "arbitrary"), 1,
              [hid, df], [pl.BlockSpec((None, tk, F), lambda j, k: (j, k, 0)),
                          pl.BlockSpec((tk, D), lambda j, k: (k, 0))],
              [(0, 1, 'tn', 0)], [(F, D)], [], [], lambda acc: (acc,),
              [jax.ShapeDtypeStruct((N_DEV, F, D), BF16)], [pl.BlockSpec((None, F, D), lambda j, k: (j, 0, 0))],
              comm=comm)
    return _with_comm(res, comm, lambda o: o[0])


def _ffn_dwgu(a, dg, du, comm=None):
    L, D = a.shape
    F = dg.shape[2]
    tk = _tile(L, 1408)
    hspec = pl.BlockSpec((None, tk, F), lambda j, k: (j, k, 0))
    wspec = pl.BlockSpec((None, F, D), lambda j, k: (j, 0, 0))
    res = _mm("ffn_dwgu", (N_DEV, L // tk), ("parallel", "arbitrary"), 1,
              [a, dg, du], [pl.BlockSpec((tk, D), lambda j, k: (k, 0)), hspec, hspec],
              [(1, 0, 'tn', 0), (2, 0, 'tn', 1)], [(F, D)] * 2, [], [], lambda *acc: acc,
              [jax.ShapeDtypeStruct((N_DEV, F, D), BF16)] * 2, [wspec, wspec], comm=comm)
    return _with_comm(res, comm, lambda o: o)


def _ffn_da(dg, du, wg, wu, comm=None):
    _, L, F = dg.shape
    D = wg.shape[2]
    tm = _tile(L, 704)
    hspec = pl.BlockSpec((None, tm, F), lambda i, j: (j, i, 0))
    wspec = pl.BlockSpec((None, F, D), lambda i, j: (j, 0, 0))
    row = pl.BlockSpec((tm, D), lambda i, j: (i, 0))
    res = _mm("ffn_da", (L // tm, N_DEV), ("parallel", "arbitrary"), 1,
              [dg, du, wg, wu], [hspec, hspec, wspec, wspec],
              [(0, 2, 'nn', 0), (1, 3, 'nn', 0)], [(tm, D)], [], [], lambda acc: (acc,),
              [jax.ShapeDtypeStruct((L, D), F32)], [row], comm=comm)
    return _with_comm(res, comm, lambda o: o[0])


def _rope_tables(L):
    rows = jnp.arange(L, dtype=F32)
    pos = jnp.where(rows < BLK, rows, rows - (BLK - N_META))
    inv_r = ROPE_THETA ** (-jnp.arange(0, HD, 2, dtype=F32) / HD)
    ang_r = pos[:, None] * inv_r[None, :]
    cr = jnp.concatenate([jnp.cos(ang_r), jnp.cos(ang_r)], axis=1)
    sr = jnp.concatenate([-jnp.sin(ang_r), jnp.sin(ang_r)], axis=1)
    inv_m = ROPE_THETA ** (-jnp.arange(0, ROPE, 2, dtype=F32) / ROPE)
    ang_m = pos[:, None] * inv_m[None, :]
    z32 = jnp.zeros((L, ROPE // 2), F32)
    z64 = jnp.zeros((L, HD - ROPE), F32)
    cm = jnp.concatenate([jnp.cos(ang_m), jnp.cos(ang_m), z64], axis=1)
    sa = jnp.concatenate([-jnp.sin(ang_m), z32, z64], axis=1)
    sb = jnp.concatenate([z32, jnp.sin(ang_m), z64], axis=1)
    return cr, sr, cm, sa, sb


def _rope_ret(x, cr, sr):
    return x * cr + pltpu.roll(x, HD // 2, 1) * sr


def _rope_ret_t(d, cr, sr):
    return d * cr + pltpu.roll(d * sr, HD // 2, 1)


def _rope_mla(x, cm, sa, sb):
    return x * cm + pltpu.roll(x, HD - ROPE // 2, 1) * sa + pltpu.roll(x, ROPE // 2, 1) * sb


def _rope_mla_t(d, cm, sa, sb):
    return d * cm + pltpu.roll(d * sa, ROPE // 2, 1) + pltpu.roll(d * sb, HD - ROPE // 2, 1)


C_RQ, C_RK, C_RV, C_RG = 0, HEADS * HD, 2 * HEADS * HD, 3 * HEADS * HD
C_CQ = 4 * HEADS * HD
C_CKV = C_CQ + Q_RANK
C_KR = C_CKV + KV_RANK
RET_K_SCALE = HD ** -0.5


def _prep(proj, tabs, qn, kvn):
    L = proj.shape[0]
    tr = _tile(L, 256)
    W = HEADS * HD

    def body(p_ref, cr_ref, sr_ref, cm_ref, sa_ref, sb_ref, qn_ref, kvn_ref, q_ref, k_ref, v_ref, cq_ref, ckv_ref,
             kr_ref):
        cr, sr = cr_ref[...], sr_ref[...]
        for h in range(HEADS):
            sl = slice(h * HD, (h + 1) * HD)
            q_ref[:, sl] = _rope_ret(p_ref[:, C_RQ + h * HD:C_RQ + (h + 1) * HD].astype(F32), cr, sr).astype(BF16)
            k_ref[:, sl] = (_rope_ret(p_ref[:, C_RK + h * HD:C_RK + (h + 1) * HD].astype(F32), cr, sr)
                            * RET_K_SCALE).astype(BF16)
        v_ref[...] = p_ref[:, C_RV:C_RV + W].astype(BF16)
        cq = p_ref[:, C_CQ:C_CQ + Q_RANK].astype(F32)
        cq_ref[...] = (cq * lax.rsqrt(jnp.mean(cq * cq, axis=-1, keepdims=True) + EPS) * qn_ref[...]).astype(BF16)
        ckv = p_ref[:, C_CKV:C_CKV + KV_RANK].astype(F32)
        ckv_ref[...] = (ckv * lax.rsqrt(jnp.mean(ckv * ckv, axis=-1, keepdims=True) + EPS)
                        * kvn_ref[...]).astype(BF16)
        kr_ref[...] = _rope_mla(p_ref[:, C_KR:C_KR + HD].astype(F32), cm_ref[...], sa_ref[...], sb_ref[...]).astype(BF16)

    row = lambda w: pl.BlockSpec((tr, w), lambda i: (i, 0))
    vec = lambda w: pl.BlockSpec((1, w), lambda i: (0, 0))
    return pl.pallas_call(
        body, name="mix_prep", grid=(L // tr,),
        out_shape=[jax.ShapeDtypeStruct((L, W), BF16)] * 3 + [jax.ShapeDtypeStruct((L, Q_RANK), BF16),
                                                              jax.ShapeDtypeStruct((L, KV_RANK), BF16),
                                                              jax.ShapeDtypeStruct((L, HD), BF16)],
        in_specs=[row(D_INP)] + [row(HD)] * 5 + [vec(Q_RANK), vec(KV_RANK)],
        out_specs=[row(W)] * 3 + [row(Q_RANK), row(KV_RANK), row(HD)],
        compiler_params=_params(("parallel",)),
    )(proj, *tabs, qn, kvn)


def _prep_bwd(proj, dq, dk, dv, drg, dcqn, dckvn, dkr8, tabs, qn, kvn):
    L = proj.shape[0]
    tr = _tile(L, 192)
    W = HEADS * HD

    def body(p_ref, dq_ref, dk_ref, dv_ref, drg_ref, dcq_ref, dckv_ref, dkr_ref, cr_ref, sr_ref, cm_ref, sa_ref,
             sb_ref, qn_ref, kvn_ref, dp_ref, dqn_ref, dkvn_ref):
        cr, sr = cr_ref[...], sr_ref[...]
        dkr = None
        for h in range(HEADS):
            sl = slice(h * HD, (h + 1) * HD)
            dp_ref[:, C_RQ + h * HD:C_RQ + (h + 1) * HD] = _rope_ret_t(dq_ref[:, sl].astype(F32), cr, sr).astype(BF16)
            dp_ref[:, C_RK + h * HD:C_RK + (h + 1) * HD] = (_rope_ret_t(dk_ref[:, sl].astype(F32), cr, sr)
                                                            * RET_K_SCALE).astype(BF16)
            part = dkr_ref[:, sl].astype(F32)
            dkr = part if dkr is None else dkr + part
        dp_ref[:, C_RV:C_RV + W] = dv_ref[...].astype(BF16)
        dp_ref[:, C_RG:C_RG + W] = drg_ref[...].astype(BF16)
        dcq, dqn = _norm_bwd_math(p_ref[:, C_CQ:C_CQ + Q_RANK].astype(F32), qn_ref[...], dcq_ref[...])
        dp_ref[:, C_CQ:C_CQ + Q_RANK] = dcq.astype(BF16)
        dckv, dkvn = _norm_bwd_math(p_ref[:, C_CKV:C_CKV + KV_RANK].astype(F32), kvn_ref[...], dckv_ref[...])
        dp_ref[:, C_CKV:C_CKV + KV_RANK] = dckv.astype(BF16)
        dp_ref[:, C_KR:C_KR + HD] = _rope_mla_t(dkr, cm_ref[...], sa_ref[...], sb_ref[...]).astype(BF16)

        @pl.when(pl.program_id(0) == 0)
        def _():
            dqn_ref[...] = jnp.zeros_like(dqn_ref)
            dkvn_ref[...] = jnp.zeros_like(dkvn_ref)

        dqn_ref[...] += dqn
        dkvn_ref[...] += dkvn

    row = lambda w: pl.BlockSpec((tr, w), lambda i: (i, 0))
    vec = lambda w: pl.BlockSpec((1, w), lambda i: (0, 0))
    return pl.pallas_call(
        body, name="mix_prep_bwd", grid=(L // tr,),
        out_shape=[jax.ShapeDtypeStruct((L, D_INP), BF16), jax.ShapeDtypeStruct((1, Q_RANK), F32),
                   jax.ShapeDtypeStruct((1, KV_RANK), F32)],
        in_specs=[row(D_INP)] + [row(W)] * 4 + [row(Q_RANK), row(KV_RANK), row(W)] + [row(HD)] * 5
                 + [vec(Q_RANK), vec(KV_RANK)],
        out_specs=[row(D_INP), vec(Q_RANK), vec(KV_RANK)],
        compiler_params=_params(("arbitrary",)),
    )(proj, dq, dk, dv, drg, dcqn, dckvn, dkr8, *tabs, qn, kvn)


def _post(o_ret, proj, gn):
    L, W = o_ret.shape
    tr = _tile(L, 384)

    def body(o_ref, rg_ref, gn_ref, out_ref):
        for h in range(HEADS):
            sl = slice(h * HD, (h + 1) * HD)
            o = o_ref[:, sl]
            rg = rg_ref[:, sl].astype(F32)
            n = o * lax.rsqrt(jnp.mean(o * o, axis=-1, keepdims=True) + EPS)
            out_ref[:, sl] = (n * gn_ref[:, sl] * (rg * _sigmoid(rg))).astype(BF16)

    row = pl.BlockSpec((tr, W), lambda i: (i, 0))
    return pl.pallas_call(
        body, name="ret_post", grid=(L // tr,), out_shape=jax.ShapeDtypeStruct((L, W), BF16),
        in_specs=[row, pl.BlockSpec((tr, W), lambda i: (i, C_RG // W)), pl.BlockSpec((1, W), lambda i: (0, 0))],
        out_specs=row, compiler_params=_params(("parallel",)),
    )(o_ret, proj, gn)


def _post_bwd(o_ret, proj, gn, dcat):
    L, W = o_ret.shape
    tr = _tile(L, 384)

    def body(o_ref, rg_ref, gn_ref, d_ref, do_ref, drg_ref, dgn_ref):
        @pl.when(pl.program_id(0) == 0)
        def _():
            dgn_ref[...] = jnp.zeros_like(dgn_ref)

        for h in range(HEADS):
            sl = slice(h * HD, (h + 1) * HD)
            o = o_ref[:, sl]
            rg = rg_ref[:, sl].astype(F32)
            d = d_ref[:, sl].astype(F32)
            gw = gn_ref[:, sl]
            r = lax.rsqrt(jnp.mean(o * o, axis=-1, keepdims=True) + EPS)
            n = o * r
            sg = _sigmoid(rg)
            si = rg * sg
            dn = d * gw * si
            dgn_ref[:, sl] += jnp.sum(d * n * si, axis=0, keepdims=True)
            drg_ref[:, sl] = (d * n * gw * sg * (1.0 + rg * (1.0 - sg))).astype(drg_ref.dtype)
            do_ref[:, sl] = (r * (dn - o * (r * r) * jnp.mean(dn * o, axis=-1, keepdims=True))).astype(BF16)

    row = pl.BlockSpec((tr, W), lambda i: (i, 0))
    vec = pl.BlockSpec((1, W), lambda i: (0, 0))
    return pl.pallas_call(
        body, name="ret_post_bwd", grid=(L // tr,),
        out_shape=[jax.ShapeDtypeStruct((L, W), BF16), jax.ShapeDtypeStruct((L, W), BF16),
                   jax.ShapeDtypeStruct((1, W), F32)],
        in_specs=[row, pl.BlockSpec((tr, W), lambda i: (i, C_RG // W)), vec, row],
        out_specs=[row, row, vec], compiler_params=_params(("arbitrary",)),
    )(o_ret, proj, gn, dcat)


RET_HEADS_PER_STEP = 4
RET_CHUNK = 256


def _ret_scans(name, arrays, scans, lg, out_dtype, single_buffered):
    L, W = arrays[0].shape
    G, C = RET_HEADS_PER_STEP, RET_CHUNK
    assert (L - BLK) % C == 0
    nc = (L - BLK) // C
    na, ns = len(arrays), len(scans)

    def body(*refs):
        in_refs, lg_ref, o_refs, s_ref = refs[:na], refs[na], refs[na + 1:na + 1 + ns], refs[-1]
        n = lax.broadcasted_iota(jnp.int32, (C, C), 0).astype(F32)
        m = lax.broadcasted_iota(jnp.int32, (C, C), 1).astype(F32)
        r = lax.broadcasted_iota(jnp.int32, (C, HD), 0).astype(F32)
        n0 = lax.broadcasted_iota(jnp.int32, (BLK, BLK), 0).astype(F32)
        m0 = lax.broadcasted_iota(jnp.int32, (BLK, BLK), 1).astype(F32)
        r0 = lax.broadcasted_iota(jnp.int32, (BLK, HD), 0).astype(F32)
        meta = (n0 < N_META) & (m0 < N_META)
        ways = {way for _, _, _, way in scans}
        consts = []
        for g in range(G):
            lgv = lg_ref[g, 0:1, 0:1]
            c = dict(gl=jnp.exp(lgv * float(C)))
            if 'f' in ways:
                c['f'] = dict(
                    dmask=jnp.where(n >= m, jnp.exp(lgv * jnp.maximum(n - m, 0.0)), 0.0),
                    dmask0=jnp.where(meta & (n0 >= m0), jnp.exp(lgv * jnp.maximum(n0 - m0, 0.0)), 0.0),
                    inter=jnp.exp(lgv * (r + 1.0)), upd=jnp.exp(lgv * (float(C) - 1.0 - r)),
                    upd0=jnp.where(r0 < N_META, jnp.exp(lgv * jnp.maximum(float(N_META) - 1.0 - r0, 0.0)), 0.0))
            if 'r' in ways:
                c['r'] = dict(
                    dmask=jnp.where(m >= n, jnp.exp(lgv * jnp.maximum(m - n, 0.0)), 0.0),
                    dmask0=jnp.where(meta & (m0 >= n0), jnp.exp(lgv * jnp.maximum(m0 - n0, 0.0)), 0.0),
                    inter=jnp.exp(lgv * (float(C) - r)), upd=jnp.exp(lgv * r),
                    inter0=jnp.where(r0 < N_META, jnp.exp(lgv * jnp.maximum(float(N_META) - r0, 0.0)), 0.0))
            consts.append(c)

        def chunk(chunk_of):
            results = []
            for g in range(G):
                cols = slice(g * HD, (g + 1) * HD)
                for s, (qi, ki, vi, way) in enumerate(scans):
                    rows = pl.ds(pl.multiple_of(BLK + chunk_of[way] * C, BLK), C)
                    cg, state = consts[g][way], s_ref[s, g]
                    qc, kc, vc = in_refs[qi][rows, cols], in_refs[ki][rows, cols], in_refs[vi][rows, cols]
                    a = _dot(qc, kc, 'nt') * cg['dmask']
                    out = _dot(a.astype(BF16), vc, 'nn') + _dot(qc, state.astype(BF16), 'nn') * cg['inter']
                    new = state * consts[g]['gl'] + _dot((kc.astype(F32) * cg['upd']).astype(BF16), vc, 'tn')
                    results.append((s, g, rows, cols, out, new))
            for s, g, rows, cols, out, new in results:
                o_refs[s][rows, cols] = out.astype(out_dtype)
                s_ref[s, g] = new

        def first_chunk(s):
            qi, ki, vi, way = scans[s]
            for g in range(G):
                cols = slice(g * HD, (g + 1) * HD)
                cg = consts[g][way]
                q0, k0, v0 = in_refs[qi][0:BLK, cols], in_refs[ki][0:BLK, cols], in_refs[vi][0:BLK, cols]
                o0 = _dot((_dot(q0, k0, 'nt') * cg['dmask0']).astype(BF16), v0, 'nn')
                if way == 'r':
                    o0 = o0 + _dot(q0, s_ref[s, g].astype(BF16), 'nn') * cg['inter0']
                else:
                    s_ref[s, g] = _dot((k0.astype(F32) * cg['upd0']).astype(BF16), v0, 'tn')
                o_refs[s][0:BLK, cols] = o0.astype(out_dtype)

        s_ref[...] = jnp.zeros_like(s_ref)
        for s in range(ns):
            if scans[s][3] == 'f':
                first_chunk(s)

        def step(t, carry):
            chunk({'f': t, 'r': nc - 1 - t})
            return carry

        lax.fori_loop(0, nc, step, 0)
        for s in range(ns):
            if scans[s][3] == 'r':
                first_chunk(s)

    mode = dict(pipeline_mode=pl.Buffered(1)) if single_buffered else {}
    col = pl.BlockSpec((L, G * HD), lambda h: (0, h), **mode)
    return pl.pallas_call(
        body, name=name, grid=(HEADS // G,), out_shape=[jax.ShapeDtypeStruct((L, W), out_dtype)] * ns,
        in_specs=[col] * na + [pl.BlockSpec((G, 8, HD), lambda h: (h, 0, 0))], out_specs=[col] * ns,
        scratch_shapes=[pltpu.VMEM((ns, G, HD, HD), F32)], compiler_params=_params(("parallel",)),
    )(*arrays, lg)


ATT_SCALE = (HD + ROPE) ** -0.5
LOG2E = 1.4426950408889634
Q_PRESCALE = ATT_SCALE * LOG2E
NEG = -1e30


ATT_TILE = 384
ATT_HEADS_PER_STEP = 8
ATT_BWD_HEADS_PER_STEP = 4


def _att_valid(nq, nk, row0, col0):
    r = lax.broadcasted_iota(jnp.int32, (nq, nk), 0) + row0
    c = lax.broadcasted_iota(jnp.int32, (nq, nk), 1) + col0
    return (c <= r) & ((c < N_META) | (c >= BLK))


def _store_rows(ref, g, first, col):
    wide = jnp.broadcast_to(col, (col.shape[0], HD))
    for c in range(col.shape[0] // BLK):
        ref[g, first + c] = jnp.transpose(wide[c * BLK:(c + 1) * BLK, :])[0:8, :]


def _load_row(ref, g, first, n):
    return jnp.concatenate([ref[g, first + c, 0:1, :] for c in range(n)], axis=1)


def _attn_fwd(qm, kn, krr, vm, comm=None):
    L = qm.shape[0]
    W = HEADS * HD
    T = _tile(L, ATT_TILE, BLK)
    nb = L // T
    G = ATT_HEADS_PER_STEP
    n_cm = comm.n if comm is not None else 0

    def body(*refs):
        q_ref, kn_ref, kr_ref, v_ref = refs[:4]
        o_ref, lse_ref = refs[4 + n_cm:6 + n_cm]
        m_sc, l_sc, acc_sc = refs[6 + 2 * n_cm:9 + 2 * n_cm]
        if comm is not None:
            cm_refs = (refs[4:4 + n_cm], refs[6 + n_cm:6 + 2 * n_cm], refs[9 + 2 * n_cm:])
            first, last = _grid_edges((HEADS // G, nb))

            @pl.when(first)
            def _():
                comm.start(*cm_refs)

        i = pl.program_id(1)
        m_sc[...] = jnp.full_like(m_sc, NEG)
        l_sc[...] = jnp.zeros_like(l_sc)
        acc_sc[...] = jnp.zeros_like(acc_sc)

        def tile(j, masked):
            rows = pl.ds(pl.multiple_of(j * T, T), T)
            kr = kr_ref[rows, :]
            valid = _att_valid(T, T, i * T, j * T) if masked else None
            ones = jnp.ones((T, HD), BF16)
            m_prev = [m_sc[g] for g in range(G)]
            l_prev = [l_sc[g] for g in range(G)]
            acc_prev = [acc_sc[g] for g in range(G)]
            m_new, l_new, acc_new = [], [], []
            for g in range(G):
                k = jnp.concatenate([kn_ref[rows, g * HD:(g + 1) * HD], kr], axis=1)
                s = _dot(q_ref[:, g * QH:(g + 1) * QH], k, 'nt')
                if masked:
                    s = jnp.where(valid, s, NEG)
                m_new.append(jnp.maximum(m_prev[g], jnp.max(s, axis=-1, keepdims=True)))
                p = jnp.exp2(s - m_new[g])
                alpha = jnp.exp2(m_prev[g] - m_new[g])
                pv = _dot(p.astype(BF16), jnp.concatenate([v_ref[rows, g * HD:(g + 1) * HD], ones], axis=1), 'nn')
                l_new.append(alpha * l_prev[g] + pv[:, HD:HD + 1])
                acc_new.append(alpha * acc_prev[g] + pv[:, 0:HD])
            for g in range(G):
                m_sc[g] = m_new[g]
                l_sc[g] = l_new[g]
                acc_sc[g] = acc_new[g]

        tile(0, True)

        def mid(j, carry):
            tile(j, False)
            return carry

        lax.fori_loop(1, i, mid, 0)

        @pl.when(i > 0)
        def _():
            tile(i, True)

        for g in range(G):
            l = l_sc[g]
            o_ref[:, g * HD:(g + 1) * HD] = (acc_sc[g] / l).astype(o_ref.dtype)
            _store_rows(lse_ref, g, 0, m_sc[g] + jnp.log(l) * LOG2E)

        if comm is not None:
            @pl.when(last)
            def _():
                comm.finish(*cm_refs)

    cm_specs = comm.specs if comm is not None else []
    res = pl.pallas_call(
        body, name="attn_fwd", grid=(HEADS // G, nb),
        out_shape=[jax.ShapeDtypeStruct((L, W), BF16), jax.ShapeDtypeStruct((HEADS, L // BLK, 8, HD), F32)]
        + (comm.out_shapes if comm is not None else []),
        in_specs=[pl.BlockSpec((T, G * QH), lambda h, i: (i, h)), pl.BlockSpec((L, G * HD), lambda h, i: (0, h)),
                  pl.BlockSpec((L, HD), lambda h, i: (0, 0)), pl.BlockSpec((L, G * HD), lambda h, i: (0, h))]
        + cm_specs,
        out_specs=[pl.BlockSpec((T, G * HD), lambda h, i: (i, h)),
                   pl.BlockSpec((G, T // BLK, 8, HD), lambda h, i: (h, i, 0, 0))] + cm_specs,
        scratch_shapes=[pltpu.VMEM((G, T, 1), F32), pltpu.VMEM((G, T, 1), F32), pltpu.VMEM((G, T, HD), F32)]
        + (comm.scratch if comm is not None else []),
        compiler_params=_params(("arbitrary", "arbitrary")),
    )(qm, kn, krr, vm, *(comm.arrays if comm is not None else []))
    return res[:2], res[2:]


def _attn_bwd(qm, kn, krr, vm, o, dcat, lse, comm=None):
    L = qm.shape[0]
    W = HEADS * HD
    T = _tile(L, ATT_TILE, BLK)
    nb, nr = L // T, T // BLK
    G = ATT_BWD_HEADS_PER_STEP
    n_cm = comm.n if comm is not None else 0

    def body(*refs):
        q_ref, kn_ref, kr_ref, v_ref, o_ref, do_ref, lse_ref = refs[:7]
        dq_ref, dkn_ref, dkr_ref, dv_ref = refs[7 + n_cm:11 + n_cm]
        dl_sc, dk_sc, dv_sc = refs[11 + 2 * n_cm:14 + 2 * n_cm]
        if comm is not None:
            cm_refs = (refs[7:7 + n_cm], refs[11 + n_cm:11 + 2 * n_cm], refs[14 + 2 * n_cm:])
            first, last = _grid_edges((HEADS // G, nb))

            @pl.when(first)
            def _():
                comm.start(*cm_refs)

        j = pl.program_id(1)
        qs = lambda g: slice(g * QH, (g + 1) * QH)
        hs = lambda g: slice(g * HD, (g + 1) * HD)

        @pl.when(j == 0)
        def _():
            dq_ref[...] = jnp.zeros_like(dq_ref)

            def rowsum(t, carry):
                rows = pl.ds(pl.multiple_of(t * T, T), T)
                for g in range(G):
                    _store_rows(dl_sc, g, t * nr, jnp.sum(
                        do_ref[rows, hs(g)].astype(F32) * o_ref[rows, hs(g)].astype(F32), axis=-1, keepdims=True))
                return carry

            lax.fori_loop(0, nb, rowsum, 0)

        kr = kr_ref[...]
        ks = [jnp.concatenate([kn_ref[:, hs(g)], kr], axis=1) for g in range(G)]
        vs = [v_ref[:, hs(g)] for g in range(G)]
        dk_sc[...] = jnp.zeros_like(dk_sc)
        dv_sc[...] = jnp.zeros_like(dv_sc)

        def tile(i, masked):
            rows = pl.ds(pl.multiple_of(i * T, T), T)
            if masked:
                key = lax.broadcasted_iota(jnp.int32, (T, T), 0) + j * T
                qry = lax.broadcasted_iota(jnp.int32, (T, T), 1) + i * T
                valid = (key <= qry) & ((key < N_META) | (key >= BLK))
            for g in range(G):
                q = q_ref[rows, qs(g)]
                do = do_ref[rows, hs(g)]
                s = _dot(ks[g], q, 'nt')
                if masked:
                    s = jnp.where(valid, s, NEG)
                p = jnp.exp2(s - _load_row(lse_ref, g, i * nr, nr))
                dv_sc[g] += _dot(p.astype(BF16), do, 'nn')
                ds = (p * (_dot(vs[g], do, 'nt') - _load_row(dl_sc, g, i * nr, nr))).astype(BF16)
                dk_sc[g] += _dot(ds, q, 'nn')
                dq_ref[rows, qs(g)] += _dot(ds, ks[g], 'tn')

        tile(j, True)

        def rest(masked):
            def step(i, carry):
                tile(i, masked)
                return carry
            lax.fori_loop(j + 1, nb, step, 0)

        @pl.when(j == 0)
        def _():
            rest(True)

        @pl.when(j > 0)
        def _():
            rest(False)

        for g in range(G):
            dk = dk_sc[g] * (1.0 / LOG2E)
            dkn_ref[:, hs(g)] = dk[:, 0:HD].astype(BF16)
            dkr_ref[:, hs(g)] = dk[:, HD:QH].astype(dkr_ref.dtype)
            dv_ref[:, hs(g)] = dv_sc[g].astype(BF16)

        if comm is not None:
            @pl.when(last)
            def _():
                comm.finish(*cm_refs)

    blk = pl.BlockSpec((T, G * HD), lambda h, j: (j, h))
    once = pl.Buffered(1)
    cm_specs = comm.specs if comm is not None else []
    res = pl.pallas_call(
        body, name="attn_bwd", grid=(HEADS // G, nb),
        out_shape=[jax.ShapeDtypeStruct((L, HEADS * QH), F32), jax.ShapeDtypeStruct((L, W), BF16),
                   jax.ShapeDtypeStruct((L, W), BF16), jax.ShapeDtypeStruct((L, W), BF16)]
        + (comm.out_shapes if comm is not None else []),
        in_specs=[pl.BlockSpec((L, G * QH), lambda h, j: (0, h), pipeline_mode=once), blk,
                  pl.BlockSpec((T, HD), lambda h, j: (j, 0)), blk,
                  pl.BlockSpec((L, G * HD), lambda h, j: (0, h), pipeline_mode=once),
                  pl.BlockSpec((L, G * HD), lambda h, j: (0, HEADS // G + h), pipeline_mode=once),
                  pl.BlockSpec((G, L // BLK, 8, HD), lambda h, j: (h, 0, 0, 0))] + cm_specs,
        out_specs=[pl.BlockSpec((L, G * QH), lambda h, j: (0, h), pipeline_mode=once), blk, blk, blk] + cm_specs,
        scratch_shapes=[pltpu.VMEM((G, L // BLK, 8, HD), F32), pltpu.VMEM((G, T, QH), F32),
                        pltpu.VMEM((G, T, HD), F32)]
        + (comm.scratch if comm is not None else []),
        compiler_params=_params(("arbitrary", "arbitrary")),
    )(qm, kn, krr, vm, o, dcat, lse, *(comm.arrays if comm is not None else []))
    return res[:4], res[4:]


def _unrope_q(dqm, tabs_m):
    L, W = dqm.shape
    tr = _tile(L, 384)

    def body(d_ref, cm_ref, sa_ref, sb_ref, out_ref):
        cm, sa, sb = cm_ref[...], sa_ref[...], sb_ref[...]
        for h in range(HEADS):
            out_ref[:, h * QH:h * QH + HD] = (d_ref[:, h * QH:h * QH + HD] * ATT_SCALE).astype(BF16)
            out_ref[:, h * QH + HD:(h + 1) * QH] = _rope_mla_t(d_ref[:, h * QH + HD:(h + 1) * QH] * ATT_SCALE, cm, sa,
                                                               sb).astype(BF16)

    row = pl.BlockSpec((tr, W), lambda i: (i, 0))
    tab = pl.BlockSpec((tr, HD), lambda i: (i, 0))
    return pl.pallas_call(
        body, name="unrope_q", grid=(L // tr,), out_shape=jax.ShapeDtypeStruct((L, W), BF16),
        in_specs=[row, tab, tab, tab], out_specs=row, compiler_params=_params(("parallel",)),
    )(dqm, *tabs_m)


def _q_up(cqn, wuq_p, tabs_m):
    L = cqn.shape[0]
    tm = _tile(L, 704)

    def ep(acc, cm, sa, sb):
        acc = acc * Q_PRESCALE
        parts = []
        for h in range(HEADS):
            parts.append(acc[:, h * QH:h * QH + HD])
            parts.append(_rope_mla(acc[:, h * QH + HD:(h + 1) * QH], cm, sa, sb))
        return (jnp.concatenate(parts, axis=1),)

    tab = pl.BlockSpec((tm, HD), lambda i, j: (i, 0))
    return _mm("mla_q_up", (L // tm, 1), ("parallel", "parallel"), None,
               [cqn, wuq_p], [pl.BlockSpec((tm, Q_RANK), lambda i, j: (i, 0)),
                              pl.BlockSpec((HEADS * QH, Q_RANK), lambda i, j: (0, 0))],
               [(0, 1, 'nt', 0)], [(tm, HEADS * QH)], list(tabs_m), [tab] * 3, ep,
               [jax.ShapeDtypeStruct((L, HEADS * QH), BF16)], [pl.BlockSpec((tm, HEADS * QH), lambda i, j: (i, 0))])[0]


def _mix_out(cat, w_out, h_in, post, next_norm):
    L, K = cat.shape
    D = w_out.shape[1]
    tm, tk = _tile(L, 384), K
    row = pl.BlockSpec((tm, D), lambda i, k: (i, 0))
    vec = pl.BlockSpec((1, D), lambda i, k: (0, 0))
    return _mm("mix_out", (L // tm, K // tk), ("parallel", "arbitrary"), 1,
               [cat, w_out], [pl.BlockSpec((tm, tk), lambda i, k: (i, k)), pl.BlockSpec((tk, D), lambda i, k: (k, 0))],
               [(0, 1, 'nn', 0)], [(tm, D)], [h_in, post, next_norm], [row, vec, vec], _resnorm_epilogue(1.0, True),
               [jax.ShapeDtypeStruct((L, D), F32)] * 2 + [jax.ShapeDtypeStruct((L, D), BF16)], [row, row, row])


ADAM_BLOCK_ELEMS = 512 * 704


def _adam_math(w, g, m, v):
    m = ADAM_B1 * m + (1.0 - ADAM_B1) * g
    v = ADAM_B2 * v + (1.0 - ADAM_B2) * (g * g)
    m_hat = m / (1.0 - ADAM_B1 ** ADAM_STEP)
    v_hat = v / (1.0 - ADAM_B2 ** ADAM_STEP)
    delta = -ADAM_LR * (m_hat / (jnp.sqrt(v_hat) + ADAM_EPS) + ADAM_WD * w)
    return delta, m, v


def _adam(name, w, m, v, g_slots=None, g=None, after=None):
    R, C = w.shape
    tr, tc = _tile(R, max(16, ADAM_BLOCK_ELEMS // C // 16 * 16), 16), C
    if tr * tc > ADAM_BLOCK_ELEMS:
        tr, tc = R, _tile(C, max(128, ADAM_BLOCK_ELEMS // R // 128 * 128), 128)
    from_slots = g_slots is not None

    def body(w_ref, m_ref, v_ref, g_ref, *rest):
        go_ref, d_ref, mo_ref, vo_ref = rest[-4:]
        if from_slots:
            grad = g_ref[0].astype(F32)
            for s in range(1, N_DEV):
                grad = grad + g_ref[s].astype(F32)
        else:
            grad = g_ref[...]
        delta, mn, vn = _adam_math(w_ref[...], grad, m_ref[...], v_ref[...])
        go_ref[...] = grad
        d_ref[...] = delta
        mo_ref[...] = mn
        vo_ref[...] = vn

    row = pl.BlockSpec((tr, tc), lambda i, j: (i, j))
    gspec = pl.BlockSpec((N_DEV, tr, tc), lambda i, j: (0, i, j)) if from_slots else row
    order = [] if after is None else [after]
    return pl.pallas_call(
        body, name=name, grid=(R // tr, C // tc), out_shape=[jax.ShapeDtypeStruct((R, C), F32)] * 4,
        in_specs=[row, row, row, gspec] + [pl.BlockSpec(memory_space=pl.ANY)] * len(order), out_specs=[row] * 4,
        compiler_params=_params(("parallel", "parallel")),
    )(w, m, v, g_slots if from_slots else g, *order)


def _unblock(gathered):
    n, r, c = gathered.shape
    return jnp.transpose(gathered, (1, 0, 2)).reshape(r, n * c)


def _reblock(full, c):
    r = full.shape[0]
    return jnp.transpose(full[:, :N_DEV * c].reshape(r, N_DEV, c), (1, 0, 2))


def _step(x, target, w, mom, vel):
    S, D = x.shape[1], x.shape[2]
    L = S + BLK
    def sq(a, n):
        if a.ndim == 2:
            return a
        if n in TRANSPOSED:
            a = jnp.swapaxes(a, 1, 2)
        return a.reshape(a.shape[1:])

    def unsq(o, n):
        o = o.reshape((1,) + o.shape)
        return jnp.swapaxes(o, 1, 2) if n in TRANSPOSED else o

    p = {n: sq(w[n], n) for n in WEIGHTS if n != 'meta_tokens'}
    gather = lambda names: _Exchange([p[n].astype(BF16) for n in names], False)
    scatter = lambda blocks: _Exchange(blocks, True)
    in_s, uq_s = p['w_in'].shape[0], p['mla_w_uq'].shape[0]
    assert uq_s == HD + ROPE and N_DEV == HEADS, "a w_uq shard is one head's columns"
    tabs = _rope_tables(L)
    tabs_m = tabs[2:]
    lg = jnp.broadcast_to(jnp.log(1.0 - 2.0 ** (-5.0 - jnp.arange(HEADS, dtype=F32)))[:, None, None], (HEADS, 8, HD))
    R = {}

    wg1, meta = _exchange("gather_first", [p['ffn1_w_gate'].astype(BF16), w['meta_tokens']], False)
    h0 = jnp.concatenate([_unblock(meta), jnp.zeros((BLK - N_META, D), F32), x[0]], axis=0)
    a1 = _norm_fwd(h0, p['ffn1_pre_norm'])
    g1, (wu1,) = _ffn_gate(a1, wg1, comm=gather(['ffn1_w_up']))
    (u1, hid1), (wd1,) = _ffn_up_gated(a1, wu1, g1, comm=gather(['ffn1_w_down']))
    (f1, h1, um), (w_in_g,) = _ffn_down(hid1, wd1, h0, p['ffn1_post_norm'], next_norm=p['mix_pre_norm'],
                                        comm=gather(['w_in']))

    w_in = jnp.pad(w_in_g.reshape(N_DEV * in_s, D), ((0, D_INP - N_DEV * in_s), (0, 0)))
    proj, (uq_g, uk_g, uv_g, wout_g) = _mm_nt("mix_in", [(um, w_in)], BF16, tn_target=1664,
                                              comm=gather(['mla_w_uq', 'mla_w_uk', 'mla_w_uv', 'w_out']))
    wuq = jnp.pad(uq_g, ((0, 0), (0, QH - uq_s), (0, 0))).reshape(HEADS * QH, Q_RANK)
    wuk, wuv, w_out = _unblock(uk_g), _unblock(uv_g), wout_g.reshape(-1, D)
    qr, kr, vr, cqn, ckvn, krr = _prep(proj, tabs, p['mla_q_norm'], p['mla_kv_norm'])
    qm = _q_up(cqn, wuq, tabs_m)
    kn = _mm_nn("mla_k_up", ckvn, wuk, BF16)
    vm = _mm_nn("mla_v_up", ckvn, wuv, BF16)
    (o_mla, lse), (wg2, wu2) = _attn_fwd(qm, kn, krr, vm, comm=gather(['ffn2_w_gate', 'ffn2_w_up']))
    o_ret, = _ret_scans("ret_fwd", [qr, kr, vr], [(0, 1, 2, 'f')], lg, F32, False)
    ret = _post(o_ret, proj, p['ret_group_norm'])
    cat = jnp.concatenate([ret, o_mla], axis=1)
    m, h2, a2 = _mix_out(cat, w_out, h1, p['mix_post_norm'], p['ffn2_pre_norm'])

    (g2, u2, hid2), (wd2,) = _ffn_up(a2, wg2, wu2, comm=gather(['ffn2_w_down']))
    f2, h3 = _ffn_down(hid2, wd2, h2, p['ffn2_post_norm'])
    dh3, loss_blk = _loss(h3, target[0])

    dsmall = {}
    df2, dsmall['ffn2_post_norm'] = _norm_bwd(f2, p['ffn2_post_norm'], dh3, None, 0.5, BF16)
    dg2, du2 = _ffn_dhid(df2, wd2, g2, u2)
    dwd2 = _ffn_dwd(hid2, df2)
    (dwg2, dwu2), (R['ffn2_w_down'],) = _ffn_dwgu(a2, dg2, du2, comm=scatter([dwd2]))
    da2, (R['ffn2_w_gate'],) = _ffn_da(dg2, du2, wg2, wu2, comm=scatter([dwg2]))
    dh2, dsmall['ffn2_pre_norm'] = _norm_bwd(h2, p['ffn2_pre_norm'], da2, dh3, 1.0, F32)

    dm, dsmall['mix_post_norm'] = _norm_bwd(m, p['mix_post_norm'], dh2, None, 1.0, BF16)
    dcat = _mm_nt("mix_dcat", [(dm, w_out)], BF16)
    dwout = _mm_tn("mix_dwout", cat, [dm])[0]
    do_ret, drg, dsmall['ret_group_norm'] = _post_bwd(o_ret, proj, p['ret_group_norm'], dcat)
    dqr, dkr, dvr = _ret_scans("ret_bwd", [do_ret, qr, kr, vr], [(0, 3, 2, 'f'), (3, 0, 1, 'r'), (2, 1, 0, 'r')],
                               lg, BF16, True)
    (dqm, dkn, dkr8, dvm), (R['ffn2_w_up'], R['w_out']) = _attn_bwd(
        qm, kn, krr, vm, o_mla, dcat, lse, comm=scatter([dwu2, dwout.reshape(N_DEV, -1, D)]))
    dqp = _unrope_q(dqm, tabs_m)
    dwuq = _mm_tn("mla_dwuq", dqp, [cqn])[0]
    dcqn = _mm_nn("mla_dcq", dqp, wuq, F32)
    dwuk, dwuv = _mm_tn("mla_dwukv", ckvn, [dkn, dvm])
    dckvn = _mm_nt("mla_dckv", [(dkn, wuk), (dvm, wuv)], F32)
    dproj, dsmall['mla_q_norm'], dsmall['mla_kv_norm'] = _prep_bwd(
        proj, dqr, dkr, dvr, drg, dcqn, dckvn, dkr8, tabs, p['mla_q_norm'], p['mla_kv_norm'])
    dwuq_b = dwuq.reshape(HEADS, QH, Q_RANK)[:, :uq_s]
    (dwin,), (R['mla_w_uq'], R['mla_w_uk'], R['mla_w_uv']) = _mm_tn(
        "mix_dwin", dproj, [um], comm=scatter([dwuq_b, _reblock(dwuk, p['mla_w_uk'].shape[1]),
                                               _reblock(dwuv, p['mla_w_uv'].shape[1])]))
    dwin_b = dwin[:N_DEV * in_s].reshape(N_DEV, in_s, D)
    half = D // 2
    dum, (r_win_a,) = _mm_nn("mix_du", dproj, w_in, F32, tn_target=512, comm=scatter([dwin_b[:, :, :half]]))
    dh1, dsmall['mix_pre_norm'] = _norm_bwd(h1, p['mix_pre_norm'], dum, dh2, 1.0, F32)

    df1, dsmall['ffn1_post_norm'] = _norm_bwd(f1, p['ffn1_post_norm'], dh1, None, 0.5, BF16)
    (dg1, du1), (r_win_b,) = _ffn_dhid(df1, wd1, g1, u1, comm=scatter([dwin_b[:, :, half:]]))
    R['w_in'] = jnp.concatenate([r_win_a, r_win_b], axis=2)
    dwd1 = _ffn_dwd(hid1, df1)
    (dwg1, dwu1), (R['ffn1_w_down'],) = _ffn_dwgu(a1, dg1, du1, comm=scatter([dwd1]))
    da1, (R['ffn1_w_gate'],) = _ffn_da(dg1, du1, wg1, wu1, comm=scatter([dwg1]))
    dh0, dsmall['ffn1_pre_norm'] = _norm_bwd(h0, p['ffn1_pre_norm'], da1, dh1, 1.0, F32)
    tail_sems_s, tail_sems_r, tail_src, tail_land, token = _scatter_start(dwu1)

    def slab(a):
        a = a.reshape(-1, 128)
        return jnp.pad(a, ((0, (-a.shape[0]) % 8), (0, 0)))

    slab_rows = lambda n: -(-(p[n].shape[-1] // 128) // 8) * 8
    packed = jnp.concatenate([slab(dsmall[n]) for n in SMALL] + [slab(dh0[:N_META]), loss_blk], axis=0)
    red = _allreduce_small(packed + token[0, 0])
    offs = sum(slab_rows(n) for n in SMALL)
    n_small = offs
    gmeta_full = red[offs:offs + N_META * D // 128].reshape(N_META, D)
    offs += N_META * D // 128
    loss = red[offs, 0]

    grad, delta, new_m, new_v = {}, {}, {}, {}
    meanwhile = []
    for n in BIG:
        if n == 'ffn1_w_up':
            continue
        outs = _adam("adam_" + n, p[n], sq(mom[n], n), sq(vel[n], n), g_slots=R[n], after=token)
        meanwhile.append(outs[0])
        grad[n], delta[n], new_m[n], new_v[n] = [unsq(o, n) for o in outs]
    pack = lambda d: jnp.concatenate([slab(d[n]) for n in SMALL], axis=0)
    outs = _adam("adam_small", pack(w), pack(mom), pack(vel), g=red[:n_small])
    meanwhile.append(outs[0])
    offs = 0
    for n in SMALL:
        r = p[n].shape[-1] // 128
        grad[n], delta[n], new_m[n], new_v[n] = [o[offs:offs + r].reshape(w[n].shape) for o in outs]
        offs += slab_rows(n)
    dev = 4 * lax.axis_index("x") + 2 * lax.axis_index("y") + lax.axis_index("c")
    mcols = w['meta_tokens'].shape[1]
    gmeta = lax.dynamic_slice(gmeta_full, (0, dev * mcols), (N_META, mcols))
    outs = _adam("adam_meta", w['meta_tokens'], mom['meta_tokens'], vel['meta_tokens'], g=gmeta)
    grad['meta_tokens'], delta['meta_tokens'], new_m['meta_tokens'], new_v['meta_tokens'] = outs
    meanwhile.append(outs[0])
    n = 'ffn1_w_up'
    slots = _scatter_wait(tail_sems_s, tail_sems_r, tail_src, tail_land, meanwhile)
    outs = _adam("adam_" + n, p[n], sq(mom[n], n), sq(vel[n], n), g_slots=slots)
    grad[n], delta[n], new_m[n], new_v[n] = [unsq(o, n) for o in outs]

    return (loss, dh0[BLK:][None], *[grad[n] for n in WEIGHTS], *[delta[n] for n in WEIGHTS],
            *[new_m[n] for n in WEIGHTS], *[new_v[n] for n in WEIGHTS])


def kernel(x, meta_tokens, ffn1_pre_norm, ffn1_w_gate, ffn1_w_up, ffn1_w_down, ffn1_post_norm, mix_pre_norm, w_in, ret_group_norm, mla_q_norm, mla_w_uq, mla_kv_norm, mla_w_uk, mla_w_uv, w_out, mix_post_norm, ffn2_pre_norm, ffn2_w_gate, ffn2_w_up, ffn2_w_down, ffn2_post_norm, loss_target, m_meta_tokens, m_ffn1_pre_norm, m_ffn1_w_gate, m_ffn1_w_up, m_ffn1_w_down, m_ffn1_post_norm, m_mix_pre_norm, m_w_in, m_ret_group_norm, m_mla_q_norm, m_mla_w_uq, m_mla_kv_norm, m_mla_w_uk, m_mla_w_uv, m_w_out, m_mix_post_norm, m_ffn2_pre_norm, m_ffn2_w_gate, m_ffn2_w_up, m_ffn2_w_down, m_ffn2_post_norm, v_meta_tokens, v_ffn1_pre_norm, v_ffn1_w_gate, v_ffn1_w_up, v_ffn1_w_down, v_ffn1_post_norm, v_mix_pre_norm, v_w_in, v_ret_group_norm, v_mla_q_norm, v_mla_w_uq, v_mla_kv_norm, v_mla_w_uk, v_mla_w_uv, v_w_out, v_mix_post_norm, v_ffn2_pre_norm, v_ffn2_w_gate, v_ffn2_w_up, v_ffn2_w_down, v_ffn2_post_norm):
    w = dict(zip(WEIGHTS, (meta_tokens, ffn1_pre_norm, ffn1_w_gate, ffn1_w_up, ffn1_w_down, ffn1_post_norm,
                           mix_pre_norm, w_in, ret_group_norm, mla_q_norm, mla_w_uq, mla_kv_norm, mla_w_uk, mla_w_uv,
                           w_out, mix_post_norm, ffn2_pre_norm, ffn2_w_gate, ffn2_w_up, ffn2_w_down, ffn2_post_norm)))
    mom = dict(zip(WEIGHTS, (m_meta_tokens, m_ffn1_pre_norm, m_ffn1_w_gate, m_ffn1_w_up, m_ffn1_w_down,
                             m_ffn1_post_norm, m_mix_pre_norm, m_w_in, m_ret_group_norm, m_mla_q_norm, m_mla_w_uq,
                             m_mla_kv_norm, m_mla_w_uk, m_mla_w_uv, m_w_out, m_mix_post_norm, m_ffn2_pre_norm,
                             m_ffn2_w_gate, m_ffn2_w_up, m_ffn2_w_down, m_ffn2_post_norm)))
    vel = dict(zip(WEIGHTS, (v_meta_tokens, v_ffn1_pre_norm, v_ffn1_w_gate, v_ffn1_w_up, v_ffn1_w_down,
                             v_ffn1_post_norm, v_mix_pre_norm, v_w_in, v_ret_group_norm, v_mla_q_norm, v_mla_w_uq,
                             v_mla_kv_norm, v_mla_w_uk, v_mla_w_uv, v_w_out, v_mix_post_norm, v_ffn2_pre_norm,
                             v_ffn2_w_gate, v_ffn2_w_up, v_ffn2_w_down, v_ffn2_post_norm)))
    return _step(x, loss_target, w, mom, vel)
```

```python
import functools
import math

import jax
import jax.numpy as jnp
from jax import lax
from jax.experimental import pallas as pl
from jax.experimental.pallas import tpu as pltpu

N_DEV = 8
N_META = 16
BLK = 128
HEADS = 8
HD = 128
ROPE = 64
Q_RANK = 512
KV_RANK = 256
QH = 2 * HD
D_INP = 4 * HEADS * HD + Q_RANK + KV_RANK + BLK
ROPE_THETA = 10000.0
EPS = 1e-6
ADAM_LR = 0.001
ADAM_B1 = 0.9
ADAM_B2 = 0.999
ADAM_EPS = 1e-08
ADAM_WD = 0.01
ADAM_STEP = 10
V7X_VMEM_LIMIT = 48 * 1024 * 1024
MESH = pl.DeviceIdType.MESH
F32 = jnp.float32
BF16 = jnp.bfloat16

WEIGHTS = ['meta_tokens', 'ffn1_pre_norm', 'ffn1_w_gate', 'ffn1_w_up', 'ffn1_w_down', 'ffn1_post_norm',
           'mix_pre_norm', 'w_in', 'ret_group_norm', 'mla_q_norm', 'mla_w_uq', 'mla_kv_norm', 'mla_w_uk',
           'mla_w_uv', 'w_out', 'mix_post_norm', 'ffn2_pre_norm', 'ffn2_w_gate', 'ffn2_w_up', 'ffn2_w_down',
           'ffn2_post_norm']
SMALL = ['ffn1_pre_norm', 'ffn1_post_norm', 'mix_pre_norm', 'ret_group_norm', 'mla_q_norm', 'mla_kv_norm',
         'mix_post_norm', 'ffn2_pre_norm', 'ffn2_post_norm']
TRANSPOSED = ('ffn1_w_gate', 'ffn1_w_up', 'ffn2_w_gate', 'ffn2_w_up', 'w_in', 'mla_w_uq')
BIG = ['ffn1_w_gate', 'ffn1_w_up', 'ffn1_w_down', 'w_in', 'mla_w_uq', 'mla_w_uk', 'mla_w_uv', 'w_out',
       'ffn2_w_gate', 'ffn2_w_up', 'ffn2_w_down']

_DIMS = {'nn': (((1,), (0,)), ((), ())), 'nt': (((1,), (1,)), ((), ())), 'tn': (((0,), (0,)), ((), ()))}


def _tile(n, target, mult=16):
    best = None
    for t in range(mult, min(n, target) + 1, mult):
        if n % t == 0:
            best = t
    return best if best is not None else n


def _params(sem):
    return pltpu.CompilerParams(dimension_semantics=sem, vmem_limit_bytes=V7X_VMEM_LIMIT)


def _dot(a, b, dims):
    return lax.dot_general(a, b, _DIMS[dims], preferred_element_type=F32)


def _sigmoid(x):
    return 0.5 * jnp.tanh(0.5 * x) + 0.5


def _me_and_peers():
    x, y, c = lax.axis_index("x"), lax.axis_index("y"), lax.axis_index("c")

    def peer(j):
        px = 1 - x if (j >> 2) & 1 else x
        py = 1 - y if (j >> 1) & 1 else y
        pc = 1 - c if j & 1 else c
        return (px, py, pc), 4 * px + 2 * py + pc

    return 4 * x + 2 * y + c, peer


class _Exchange:
    def __init__(self, arrays, per_peer):
        self.arrays = list(arrays)
        self.per_peer = per_peer
        self.n = len(self.arrays)
        self.out_shapes = [jax.ShapeDtypeStruct((N_DEV,) + tuple(a.shape[1:] if per_peer else a.shape), a.dtype)
                           for a in self.arrays]
        self.specs = [pl.BlockSpec(memory_space=pl.ANY)] * self.n
        self.scratch = [pltpu.SemaphoreType.DMA((7 * self.n,)), pltpu.SemaphoreType.DMA((7 * self.n,)),
                        pltpu.SemaphoreType.DMA((self.n,))]

    def _copies(self, src, dst, sems):
        send_sems, recv_sems, local_sems = sems
        me, peer = _me_and_peers()
        sib, _ = peer(1)
        local, sends, recvs, passes = [], {}, {}, {}
        for k in range(self.n):
            own = src[k].at[me] if self.per_peer else src[k]
            local.append(pltpu.make_async_copy(own, dst[k].at[me], local_sems.at[k]))
            for j in range(1, N_DEV):
                pid, pidx = peer(j)
                out = src[k].at[pidx] if self.per_peer else src[k]
                sem = dict(send_sem=send_sems.at[k * 7 + j - 1], recv_sem=recv_sems.at[k * 7 + j - 1])
                recvs[k, j] = pltpu.make_async_remote_copy(src_ref=out, dst_ref=dst[k].at[pidx], device_id=pid,
                                                           device_id_type=MESH, **sem)
                if self.per_peer or j in (1, 2, 4, 6):
                    sends[k, j] = pltpu.make_async_remote_copy(src_ref=out, dst_ref=dst[k].at[me], device_id=pid,
                                                               device_id_type=MESH, **sem)
                else:
                    _, origin = peer(j ^ 1)
                    passes[k, j ^ 1] = pltpu.make_async_remote_copy(
                        src_ref=dst[k].at[origin], dst_ref=dst[k].at[origin], device_id=sib, device_id_type=MESH, **sem)
        return local, sends, recvs, passes

    def start(self, src, dst, sems):
        local, sends, _, _ = self._copies(src, dst, sems)
        for cp in local + list(sends.values()):
            cp.start()

    def finish(self, src, dst, sems):
        local, sends, recvs, passes = self._copies(src, dst, sems)
        for key, cp in passes.items():
            recvs[key].wait_recv()
            cp.start()
        for key, cp in recvs.items():
            if key not in passes:
                cp.wait_recv()
        for cp in list(sends.values()) + list(passes.values()):
            cp.wait_send()
        for cp in local:
            cp.wait()


def _grid_edges(grid):
    first, last = None, None
    for a, n in enumerate(grid):
        f, l = pl.program_id(a) == 0, pl.program_id(a) == n - 1
        first = f if first is None else first & f
        last = l if last is None else last & l
    return first, last


def _exchange(name, arrays, per_peer):
    ex = _Exchange(arrays, per_peer)
    n = ex.n

    def body(*refs):
        ex.start(refs[:n], refs[n:2 * n], refs[2 * n:])
        ex.finish(refs[:n], refs[n:2 * n], refs[2 * n:])

    return pl.pallas_call(body, name=name, out_shape=ex.out_shapes, in_specs=ex.specs, out_specs=ex.specs,
                          scratch_shapes=ex.scratch)(*arrays)


def _scatter_start(blocks):
    def body(src_ref, land_ref, send_sems, recv_sems, src_thru, land_thru, token, local_sem):
        me, peer = _me_and_peers()
        local = pltpu.make_async_copy(src_ref.at[me], land_ref.at[me], local_sem)
        local.start()
        for j in range(1, N_DEV):
            pid, pidx = peer(j)
            pltpu.make_async_remote_copy(src_ref=src_ref.at[pidx], dst_ref=land_ref.at[me],
                                         send_sem=send_sems.at[j - 1], recv_sem=recv_sems.at[j - 1],
                                         device_id=pid, device_id_type=MESH).start()
        local.wait()
        token[...] = jnp.zeros_like(token)

    hbm = pl.BlockSpec(memory_space=pltpu.HBM)
    sem = pl.BlockSpec(memory_space=pltpu.SEMAPHORE)
    return pl.pallas_call(
        body, name="scatter_tail_start",
        out_shape=(pltpu.SemaphoreType.DMA((7,)), pltpu.SemaphoreType.DMA((7,)), pltpu.HBM(blocks.shape, blocks.dtype),
                   pltpu.HBM(blocks.shape, blocks.dtype), jax.ShapeDtypeStruct((8, 128), F32)),
        in_specs=(hbm, hbm), out_specs=(sem, sem, hbm, hbm, pl.BlockSpec(memory_space=pltpu.VMEM)),
        input_output_aliases={0: 2, 1: 3}, scratch_shapes=[pltpu.SemaphoreType.DMA],
        compiler_params=pltpu.CompilerParams(has_side_effects=pltpu.SideEffectType.DATAFLOW_SIDE_EFFECTING),
    )(pltpu.with_memory_space_constraint(blocks, pltpu.HBM),
      pltpu.with_memory_space_constraint(lax.empty(blocks.shape, blocks.dtype), pltpu.HBM))


def _scatter_wait(send_sems, recv_sems, src_thru, land_thru, after):
    n_after = len(after)

    def body(src_ref, land_ref, send_sems, recv_sems, *rest):
        me, peer = _me_and_peers()
        for j in range(1, N_DEV):
            pid, pidx = peer(j)
            cp = pltpu.make_async_remote_copy(src_ref=src_ref.at[pidx], dst_ref=land_ref.at[pidx],
                                              send_sem=send_sems.at[j - 1], recv_sem=recv_sems.at[j - 1],
                                              device_id=pid, device_id_type=MESH)
            cp.wait_send()
            cp.wait_recv()

    hbm = pl.BlockSpec(memory_space=pltpu.HBM)
    sem = pl.BlockSpec(memory_space=pltpu.SEMAPHORE)
    return pl.pallas_call(
        body, name="scatter_tail_wait",
        out_shape=(pltpu.HBM(src_thru.shape, src_thru.dtype), pltpu.HBM(land_thru.shape, land_thru.dtype)),
        in_specs=(hbm, hbm, sem, sem) + (pl.BlockSpec(memory_space=pl.ANY),) * n_after, out_specs=(hbm, hbm),
        input_output_aliases={0: 0, 1: 1},
        compiler_params=pltpu.CompilerParams(has_side_effects=pltpu.SideEffectType.DATAFLOW_SIDE_EFFECTING),
    )(src_thru, land_thru, send_sems, recv_sems, *after)[1]


def _allreduce_small(v):
    rows = v.shape[0]

    def body(v_ref, out_ref, buf, send_sems, recv_sems):
        me, peer = _me_and_peers()
        buf[pl.ds(me, 1)] = v_ref[...][None]
        sends = []
        for j in range(1, N_DEV):
            pid, _ = peer(j)
            cp = pltpu.make_async_remote_copy(src_ref=v_ref, dst_ref=buf.at[me], send_sem=send_sems.at[j - 1],
                                              recv_sem=recv_sems.at[j - 1], device_id=pid, device_id_type=MESH)
            cp.start()
            sends.append(cp)
        for j in range(1, N_DEV):
            pid, pidx = peer(j)
            pltpu.make_async_remote_copy(src_ref=v_ref, dst_ref=buf.at[pidx], send_sem=send_sems.at[j - 1],
                                         recv_sem=recv_sems.at[j - 1], device_id=pid,
                                         device_id_type=MESH).wait_recv()
        for cp in sends:
            cp.wait_send()
        acc = buf[0]
        for s in range(1, N_DEV):
            acc = acc + buf[s]
        out_ref[...] = acc

    vm = pl.BlockSpec(memory_space=pltpu.VMEM)
    return pl.pallas_call(
        body, name="allreduce_small", out_shape=jax.ShapeDtypeStruct(v.shape, F32),
        in_specs=[vm], out_specs=vm,
        scratch_shapes=[pltpu.VMEM((N_DEV, rows, 128), F32), pltpu.SemaphoreType.DMA((7,)),
                        pltpu.SemaphoreType.DMA((7,))],
    )(v)


def _mm(name, grid, sem, k_axis, ops, op_specs, pairs, acc_shapes, extras, extra_specs, epilogue, outs, out_specs,
        comm=None):
    n_op, n_ex, n_out = len(ops), len(extras), len(outs)
    nk = grid[k_axis] if k_axis is not None else 1
    n_acc = len(acc_shapes) if nk > 1 else 0
    n_cm = comm.n if comm is not None else 0

    def body(*refs):
        op_refs = refs[:n_op]
        ex_refs = refs[n_op:n_op + n_ex]
        n_in = n_op + n_ex + n_cm
        out_refs = refs[n_in:n_in + n_out]
        acc_refs = refs[n_in + n_out + n_cm:n_in + n_out + n_cm + n_acc]
        if comm is not None:
            cm_refs = (refs[n_op + n_ex:n_in], refs[n_in + n_out:n_in + n_out + n_cm],
                       refs[n_in + n_out + n_cm + n_acc:])
            first, last = _grid_edges(grid)

            @pl.when(first)
            def _():
                comm.start(*cm_refs)

        def finish(vals):
            res = epilogue(*vals, *[e[...] for e in ex_refs])
            for o, r in zip(out_refs, res):
                o[...] = r.astype(o.dtype)

        if nk == 1:
            parts = [None] * len(acc_shapes)
            for li, ri, dims, ai in pairs:
                d = _dot(op_refs[li][...], op_refs[ri][...], dims)
                parts[ai] = d if parts[ai] is None else parts[ai] + d
            finish(parts)
        else:
            k = pl.program_id(k_axis)

            @pl.when(k == 0)
            def _():
                for a in acc_refs:
                    a[...] = jnp.zeros_like(a)

            for li, ri, dims, ai in pairs:
                acc_refs[ai][...] += _dot(op_refs[li][...], op_refs[ri][...], dims)

            @pl.when(k == nk - 1)
            def _():
                finish([a[...] for a in acc_refs])

        if comm is not None:
            @pl.when(last)
            def _():
                comm.finish(*cm_refs)

    scratch = [pltpu.VMEM(s, F32) for s in acc_shapes] if nk > 1 else []
    if comm is None:
        return pl.pallas_call(
            body, name=name, grid=grid, out_shape=outs,
            in_specs=list(op_specs) + list(extra_specs), out_specs=list(out_specs),
            scratch_shapes=scratch, compiler_params=_params(sem),
        )(*ops, *extras)
    res = pl.pallas_call(
        body, name=name, grid=grid, out_shape=list(outs) + comm.out_shapes,
        in_specs=list(op_specs) + list(extra_specs) + comm.specs, out_specs=list(out_specs) + comm.specs,
        scratch_shapes=scratch + comm.scratch, compiler_params=_params(("arbitrary",) * len(grid)),
    )(*ops, *extras, *comm.arrays)
    return res[:n_out], res[n_out:]


def _with_comm(res, comm, pick):
    if comm is None:
        return pick(res)
    return pick(res[0]), res[1]


def _mm_nn(name, a, w, out_dtype, tm_target=704, tn_target=1664, epilogue=None, extras=(), extra_specs=(), comm=None):
    L, K = a.shape
    N = w.shape[1]
    tm, tn = _tile(L, tm_target), _tile(N, tn_target, 128)
    ep = epilogue if epilogue is not None else (lambda acc: (acc,))
    res = _mm(name, (L // tm, N // tn), ("parallel", "parallel"), None,
              [a, w], [pl.BlockSpec((tm, K), lambda i, j: (i, 0)), pl.BlockSpec((K, tn), lambda i, j: (0, j))],
              [(0, 1, 'nn', 0)], [(tm, tn)], list(extras), list(extra_specs), ep,
              [jax.ShapeDtypeStruct((L, N), out_dtype)], [pl.BlockSpec((tm, tn), lambda i, j: (i, j))], comm=comm)
    return _with_comm(res, comm, lambda o: o[0])


def _mm_nt(name, pairs_aw, out_dtype, tm_target=704, tn_target=512, comm=None):
    L = pairs_aw[0][0].shape[0]
    N = pairs_aw[0][1].shape[0]
    tm, tn = _tile(L, tm_target), _tile(N, tn_target, 128)
    ops, specs, pairs = [], [], []
    for t, (a, w) in enumerate(pairs_aw):
        K = a.shape[1]
        ops += [a, w]
        specs += [pl.BlockSpec((tm, K), lambda i, j: (i, 0)), pl.BlockSpec((tn, K), lambda i, j: (j, 0))]
        pairs.append((2 * t, 2 * t + 1, 'nt', 0))
    res = _mm(name, (L // tm, N // tn), ("parallel", "parallel"), None, ops, specs, pairs, [(tm, tn)], [], [],
              lambda acc: (acc,), [jax.ShapeDtypeStruct((L, N), out_dtype)],
              [pl.BlockSpec((tm, tn), lambda i, j: (i, j))], comm=comm)
    return _with_comm(res, comm, lambda o: o[0])


def _mm_tn(name, a, bs, out_dtype=BF16, tk_target=1408, tn_target=1664, tm_target=2048, comm=None):
    L, M = a.shape
    N = bs[0].shape[1]
    tk, tn, tm = _tile(L, tk_target), _tile(N, tn_target, 128), _tile(M, tm_target, 128)
    nb = len(bs)
    ops = [a] + list(bs)
    specs = [pl.BlockSpec((tk, tm), lambda i, j, k: (k, i))] + [pl.BlockSpec((tk, tn), lambda i, j, k: (k, j))] * nb
    res = _mm(name, (M // tm, N // tn, L // tk), ("parallel", "parallel", "arbitrary"), 2, ops, specs,
              [(0, 1 + t, 'tn', t) for t in range(nb)], [(tm, tn)] * nb, [], [], lambda *acc: acc,
              [jax.ShapeDtypeStruct((M, N), out_dtype)] * nb,
              [pl.BlockSpec((tm, tn), lambda i, j, k: (i, j))] * nb, comm=comm)
    return _with_comm(res, comm, lambda o: o)


def _norm_fwd(x, w):
    L, D = x.shape
    tr = _tile(L, 512)

    def body(x_ref, w_ref, y_ref):
        v = x_ref[...]
        r = lax.rsqrt(jnp.mean(v * v, axis=-1, keepdims=True) + EPS)
        y_ref[...] = (v * r * w_ref[...]).astype(y_ref.dtype)

    return pl.pallas_call(
        body, name="norm_fwd", grid=(L // tr,), out_shape=jax.ShapeDtypeStruct((L, D), BF16),
        in_specs=[pl.BlockSpec((tr, D), lambda i: (i, 0)), pl.BlockSpec((1, D), lambda i: (0, 0))],
        out_specs=pl.BlockSpec((tr, D), lambda i: (i, 0)), compiler_params=_params(("parallel",)),
    )(x, w)


def _norm_bwd_math(x, w, dy):
    r = lax.rsqrt(jnp.mean(x * x, axis=-1, keepdims=True) + EPS)
    gy = dy * w
    dx = r * (gy - x * (r * r) * jnp.mean(gy * x, axis=-1, keepdims=True))
    dw = jnp.sum(dy * x * r, axis=0, keepdims=True)
    return dx, dw


def _norm_bwd(x, w, dy, res, scale, out_dtype):
    L, D = x.shape
    tr = _tile(L, 384)
    has_res = res is not None

    def body(*refs):
        x_ref, w_ref, dy_ref = refs[:3]
        res_ref = refs[3] if has_res else None
        dx_ref, dw_ref = refs[-2:]
        dx, dw = _norm_bwd_math(x_ref[...], w_ref[...], dy_ref[...].astype(F32))
        dx = scale * dx
        if has_res:
            dx = dx + res_ref[...]
        dx_ref[...] = dx.astype(dx_ref.dtype)

        @pl.when(pl.program_id(0) == 0)
        def _():
            dw_ref[...] = jnp.zeros_like(dw_ref)

        dw_ref[...] += scale * dw

    row = pl.BlockSpec((tr, D), lambda i: (i, 0))
    vec = pl.BlockSpec((1, D), lambda i: (0, 0))
    return pl.pallas_call(
        body, name="norm_bwd", grid=(L // tr,),
        out_shape=[jax.ShapeDtypeStruct((L, D), out_dtype), jax.ShapeDtypeStruct((1, D), F32)],
        in_specs=[row, vec, row] + ([row] if has_res else []), out_specs=[row, vec],
        compiler_params=_params(("arbitrary",)),
    )(*([x, w, dy] + ([res] if has_res else [])))


def _loss(h, target):
    L, D = h.shape

    def body(h_ref, t_ref, dh_ref, loss_ref):
        i = pl.program_id(0)

        @pl.when(i == 0)
        def _():
            dh_ref[...] = jnp.zeros_like(dh_ref)
            loss_ref[...] = jnp.zeros_like(loss_ref)

        @pl.when(i > 0)
        def _():
            diff = h_ref[...] - t_ref[...]
            dh_ref[...] = diff * (1.0 / D)
            loss_ref[...] += 0.5 * jnp.sum(diff * diff) * (1.0 / D)

    return pl.pallas_call(
        body, name="loss", grid=(L // BLK,),
        out_shape=[jax.ShapeDtypeStruct((L, D), F32), jax.ShapeDtypeStruct((8, 128), F32)],
        in_specs=[pl.BlockSpec((BLK, D), lambda i: (i, 0)),
                  pl.BlockSpec((BLK, D), lambda i: (jnp.maximum(i - 1, 0), 0))],
        out_specs=[pl.BlockSpec((BLK, D), lambda i: (i, 0)), pl.BlockSpec((8, 128), lambda i: (0, 0))],
        compiler_params=_params(("arbitrary",)),
    )(h, target)


def _ffn_up(a, wg, wu, comm=None):
    L, D = a.shape
    F = wg.shape[1]
    tm = _tile(L, 704)

    def ep(g, u):
        return g, u, g * _sigmoid(g) * u

    hspec = pl.BlockSpec((None, tm, F), lambda i, j: (j, i, 0))
    wspec = pl.BlockSpec((None, F, D), lambda i, j: (j, 0, 0))
    res = _mm("ffn_up", (L // tm, N_DEV), ("parallel", "parallel"), None,
              [a, wg, wu], [pl.BlockSpec((tm, D), lambda i, j: (i, 0)), wspec, wspec],
              [(0, 1, 'nt', 0), (0, 2, 'nt', 1)], [(tm, F)] * 2, [], [], ep,
              [jax.ShapeDtypeStruct((N_DEV, L, F), BF16)] * 3, [hspec] * 3, comm=comm)
    return _with_comm(res, comm, lambda o: o)


def _ffn_gate(a, wg, comm=None):
    L, D = a.shape
    F = wg.shape[1]
    tm = _tile(L, 704)
    res = _mm("ffn_gate", (L // tm, N_DEV), ("parallel", "parallel"), None,
              [a, wg], [pl.BlockSpec((tm, D), lambda i, j: (i, 0)), pl.BlockSpec((None, F, D), lambda i, j: (j, 0, 0))],
              [(0, 1, 'nt', 0)], [(tm, F)], [], [], lambda g: (g,),
              [jax.ShapeDtypeStruct((N_DEV, L, F), BF16)], [pl.BlockSpec((None, tm, F), lambda i, j: (j, i, 0))],
              comm=comm)
    return _with_comm(res, comm, lambda o: o[0])


def _ffn_up_gated(a, wu, g, comm=None):
    L, D = a.shape
    F = wu.shape[1]
    tm = _tile(L, 704)

    def ep(u, g_):
        g32 = g_.astype(F32)
        return u, g32 * _sigmoid(g32) * u

    hspec = pl.BlockSpec((None, tm, F), lambda i, j: (j, i, 0))
    res = _mm("ffn_up_gated", (L // tm, N_DEV), ("parallel", "parallel"), None,
              [a, wu], [pl.BlockSpec((tm, D), lambda i, j: (i, 0)), pl.BlockSpec((None, F, D), lambda i, j: (j, 0, 0))],
              [(0, 1, 'nt', 0)], [(tm, F)], [g], [hspec], ep,
              [jax.ShapeDtypeStruct((N_DEV, L, F), BF16)] * 2, [hspec, hspec], comm=comm)
    return _with_comm(res, comm, lambda o: o)


def _resnorm_epilogue(scale, with_next):
    def ep(acc, h, w, *w_next):
        r = lax.rsqrt(jnp.mean(acc * acc, axis=-1, keepdims=True) + EPS)
        h_out = h + scale * (acc * r * w)
        if not with_next:
            return acc, h_out
        r_next = lax.rsqrt(jnp.mean(h_out * h_out, axis=-1, keepdims=True) + EPS)
        return acc, h_out, h_out * r_next * w_next[0]
    return ep


def _ffn_down(hid, wd, h_in, post, next_norm=None, comm=None):
    _, L, F = hid.shape
    D = wd.shape[2]
    tm = _tile(L, 528)
    row = pl.BlockSpec((tm, D), lambda i, j: (i, 0))
    vec = pl.BlockSpec((1, D), lambda i, j: (0, 0))
    nxt = [] if next_norm is None else [next_norm]
    res = _mm("ffn_down", (L // tm, N_DEV), ("parallel", "arbitrary"), 1,
              [hid, wd], [pl.BlockSpec((None, tm, F), lambda i, j: (j, i, 0)),
                          pl.BlockSpec((None, F, D), lambda i, j: (j, 0, 0))],
              [(0, 1, 'nn', 0)], [(tm, D)], [h_in, post] + nxt, [row, vec] + [vec] * len(nxt),
              _resnorm_epilogue(0.5, bool(nxt)),
              [jax.ShapeDtypeStruct((L, D), F32)] * 2 + [jax.ShapeDtypeStruct((L, D), BF16)] * len(nxt),
              [row] * (2 + len(nxt)), comm=comm)
    return _with_comm(res, comm, lambda o: o)


def _ffn_dhid(df, wd, g, u, comm=None):
    L, D = df.shape
    F = wd.shape[1]
    tm = _tile(L, 704)

    def ep(dhid, g_, u_):
        g32, u32 = g_.astype(F32), u_.astype(F32)
        sg = _sigmoid(g32)
        return dhid * u32 * sg * (1.0 + g32 * (1.0 - sg)), dhid * g32 * sg

    hspec = pl.BlockSpec((None, tm, F), lambda i, j: (j, i, 0))
    res = _mm("ffn_dhid", (L // tm, N_DEV), ("parallel", "parallel"), None,
              [df, wd], [pl.BlockSpec((tm, D), lambda i, j: (i, 0)),
                         pl.BlockSpec((None, F, D), lambda i, j: (j, 0, 0))],
              [(0, 1, 'nt', 0)], [(tm, F)], [g, u], [hspec, hspec], ep,
              [jax.ShapeDtypeStruct((N_DEV, L, F), BF16)] * 2, [hspec, hspec], comm=comm)
    return _with_comm(res, comm, lambda o: o)


def _ffn_dwd(hid, df, comm=None):
    _, L, F = hid.shape
    D = df.shape[1]
    tk = _tile(L, 1408)
    res = _mm("ffn_dwd", (N_DEV, L // tk), ("parallel", "arbitrary"), 1,
              [hid, df], [pl.BlockSpec((None, tk, F), lambda j, k: (j, k, 0)),
                          pl.BlockSpec((tk, D), lambda j, k: (k, 0))],
              [(0, 1, 'tn', 0)], [(F, D)], [], [], lambda acc: (acc,),
              [jax.ShapeDtypeStruct((N_DEV, F, D), BF16)], [pl.BlockSpec((None, F, D), lambda j, k: (j, 0, 0))],
              comm=comm)
    return _with_comm(res, comm, lambda o: o[0])


def _ffn_dwgu(a, dg, du, comm=None):
    L, D = a.shape
    F = dg.shape[2]
    tk = _tile(L, 1408)
    hspec = pl.BlockSpec((None, tk, F), lambda j, k: (j, k, 0))
    wspec = pl.BlockSpec((None, F, D), lambda j, k: (j, 0, 0))
    res = _mm("ffn_dwgu", (N_DEV, L // tk), ("parallel", "arbitrary"), 1,
              [a, dg, du], [pl.BlockSpec((tk, D), lambda j, k: (k, 0)), hspec, hspec],
              [(1, 0, 'tn', 0), (2, 0, 'tn', 1)], [(F, D)] * 2, [], [], lambda *acc: acc,
              [jax.ShapeDtypeStruct((N_DEV, F, D), BF16)] * 2, [wspec, wspec], comm=comm)
    return _with_comm(res, comm, lambda o: o)


def _ffn_da(dg, du, wg, wu, comm=None):
    _, L, F = dg.shape
    D = wg.shape[2]
    tm = _tile(L, 704)
    hspec = pl.BlockSpec((None, tm, F), lambda i, j: (j, i, 0))
    wspec = pl.BlockSpec((None, F, D), lambda i, j: (j, 0, 0))
    row = pl.BlockSpec((tm, D), lambda i, j: (i, 0))
    res = _mm("ffn_da", (L // tm, N_DEV), ("parallel", "arbitrary"), 1,
              [dg, du, wg, wu], [hspec, hspec, wspec, wspec],
              [(0, 2, 'nn', 0), (1, 3, 'nn', 0)], [(tm, D)], [], [], lambda acc: (acc,),
              [jax.ShapeDtypeStruct((L, D), F32)], [row], comm=comm)
    return _with_comm(res, comm, lambda o: o[0])


def _rope_tables(L):
    rows = jnp.arange(L, dtype=F32)
    pos = jnp.where(rows < BLK, rows, rows - (BLK - N_META))
    inv_r = ROPE_THETA ** (-jnp.arange(0, HD, 2, dtype=F32) / HD)
    ang_r = pos[:, None] * inv_r[None, :]
    cr = jnp.concatenate([jnp.cos(ang_r), jnp.cos(ang_r)], axis=1)
    sr = jnp.concatenate([-jnp.sin(ang_r), jnp.sin(ang_r)], axis=1)
    inv_m = ROPE_THETA ** (-jnp.arange(0, ROPE, 2, dtype=F32) / ROPE)
    ang_m = pos[:, None] * inv_m[None, :]
    z32 = jnp.zeros((L, ROPE // 2), F32)
    z64 = jnp.zeros((L, HD - ROPE), F32)
    cm = jnp.concatenate([jnp.cos(ang_m), jnp.cos(ang_m), z64], axis=1)
    sa = jnp.concatenate([-jnp.sin(ang_m), z32, z64], axis=1)
    sb = jnp.concatenate([z32, jnp.sin(ang_m), z64], axis=1)
    return cr, sr, cm, sa, sb


def _rope_ret(x, cr, sr):
    return x * cr + pltpu.roll(x, HD // 2, 1) * sr


def _rope_ret_t(d, cr, sr):
    return d * cr + pltpu.roll(d * sr, HD // 2, 1)


def _rope_mla(x, cm, sa, sb):
    return x * cm + pltpu.roll(x, HD - ROPE // 2, 1) * sa + pltpu.roll(x, ROPE // 2, 1) * sb


def _rope_mla_t(d, cm, sa, sb):
    return d * cm + pltpu.roll(d * sa, ROPE // 2, 1) + pltpu.roll(d * sb, HD - ROPE // 2, 1)


C_RQ, C_RK, C_RV, C_RG = 0, HEADS * HD, 2 * HEADS * HD, 3 * HEADS * HD
C_CQ = 4 * HEADS * HD
C_CKV = C_CQ + Q_RANK
C_KR = C_CKV + KV_RANK
RET_K_SCALE = HD ** -0.5


def _prep(proj, tabs, qn, kvn):
    L = proj.shape[0]
    tr = _tile(L, 256)
    W = HEADS * HD

    def body(p_ref, cr_ref, sr_ref, cm_ref, sa_ref, sb_ref, qn_ref, kvn_ref, q_ref, k_ref, v_ref, cq_ref, ckv_ref,
             kr_ref):
        cr, sr = cr_ref[...], sr_ref[...]
        for h in range(HEADS):
            sl = slice(h * HD, (h + 1) * HD)
            q_ref[:, sl] = _rope_ret(p_ref[:, C_RQ + h * HD:C_RQ + (h + 1) * HD].astype(F32), cr, sr).astype(BF16)
            k_ref[:, sl] = (_rope_ret(p_ref[:, C_RK + h * HD:C_RK + (h + 1) * HD].astype(F32), cr, sr)
                            * RET_K_SCALE).astype(BF16)
        v_ref[...] = p_ref[:, C_RV:C_RV + W].astype(BF16)
        cq = p_ref[:, C_CQ:C_CQ + Q_RANK].astype(F32)
        cq_ref[...] = (cq * lax.rsqrt(jnp.mean(cq * cq, axis=-1, keepdims=True) + EPS) * qn_ref[...]).astype(BF16)
        ckv = p_ref[:, C_CKV:C_CKV + KV_RANK].astype(F32)
        ckv_ref[...] = (ckv * lax.rsqrt(jnp.mean(ckv * ckv, axis=-1, keepdims=True) + EPS)
                        * kvn_ref[...]).astype(BF16)
        kr_ref[...] = _rope_mla(p_ref[:, C_KR:C_KR + HD].astype(F32), cm_ref[...], sa_ref[...], sb_ref[...]).astype(BF16)

    row = lambda w: pl.BlockSpec((tr, w), lambda i: (i, 0))
    vec = lambda w: pl.BlockSpec((1, w), lambda i: (0, 0))
    return pl.pallas_call(
        body, name="mix_prep", grid=(L // tr,),
        out_shape=[jax.ShapeDtypeStruct((L, W), BF16)] * 3 + [jax.ShapeDtypeStruct((L, Q_RANK), BF16),
                                                              jax.ShapeDtypeStruct((L, KV_RANK), BF16),
                                                              jax.ShapeDtypeStruct((L, HD), BF16)],
        in_specs=[row(D_INP)] + [row(HD)] * 5 + [vec(Q_RANK), vec(KV_RANK)],
        out_specs=[row(W)] * 3 + [row(Q_RANK), row(KV_RANK), row(HD)],
        compiler_params=_params(("parallel",)),
    )(proj, *tabs, qn, kvn)


def _prep_bwd(proj, dq, dk, dv, drg, dcqn, dckvn, dkr8, tabs, qn, kvn):
    L = proj.shape[0]
    tr = _tile(L, 192)
    W = HEADS * HD

    def body(p_ref, dq_ref, dk_ref, dv_ref, drg_ref, dcq_ref, dckv_ref, dkr_ref, cr_ref, sr_ref, cm_ref, sa_ref,
             sb_ref, qn_ref, kvn_ref, dp_ref, dqn_ref, dkvn_ref):
        cr, sr = cr_ref[...], sr_ref[...]
        dkr = None
        for h in range(HEADS):
            sl = slice(h * HD, (h + 1) * HD)
            dp_ref[:, C_RQ + h * HD:C_RQ + (h + 1) * HD] = _rope_ret_t(dq_ref[:, sl].astype(F32), cr, sr).astype(BF16)
            dp_ref[:, C_RK + h * HD:C_RK + (h + 1) * HD] = (_rope_ret_t(dk_ref[:, sl].astype(F32), cr, sr)
                                                            * RET_K_SCALE).astype(BF16)
            part = dkr_ref[:, sl].astype(F32)
            dkr = part if dkr is None else dkr + part
        dp_ref[:, C_RV:C_RV + W] = dv_ref[...].astype(BF16)
        dp_ref[:, C_RG:C_RG + W] = drg_ref[...].astype(BF16)
        dcq, dqn = _norm_bwd_math(p_ref[:, C_CQ:C_CQ + Q_RANK].astype(F32), qn_ref[...], dcq_ref[...])
        dp_ref[:, C_CQ:C_CQ + Q_RANK] = dcq.astype(BF16)
        dckv, dkvn = _norm_bwd_math(p_ref[:, C_CKV:C_CKV + KV_RANK].astype(F32), kvn_ref[...], dckv_ref[...])
        dp_ref[:, C_CKV:C_CKV + KV_RANK] = dckv.astype(BF16)
        dp_ref[:, C_KR:C_KR + HD] = _rope_mla_t(dkr, cm_ref[...], sa_ref[...], sb_ref[...]).astype(BF16)

        @pl.when(pl.program_id(0) == 0)
        def _():
            dqn_ref[...] = jnp.zeros_like(dqn_ref)
            dkvn_ref[...] = jnp.zeros_like(dkvn_ref)

        dqn_ref[...] += dqn
        dkvn_ref[...] += dkvn

    row = lambda w: pl.BlockSpec((tr, w), lambda i: (i, 0))
    vec = lambda w: pl.BlockSpec((1, w), lambda i: (0, 0))
    return pl.pallas_call(
        body, name="mix_prep_bwd", grid=(L // tr,),
        out_shape=[jax.ShapeDtypeStruct((L, D_INP), BF16), jax.ShapeDtypeStruct((1, Q_RANK), F32),
                   jax.ShapeDtypeStruct((1, KV_RANK), F32)],
        in_specs=[row(D_INP)] + [row(W)] * 4 + [row(Q_RANK), row(KV_RANK), row(W)] + [row(HD)] * 5
                 + [vec(Q_RANK), vec(KV_RANK)],
        out_specs=[row(D_INP), vec(Q_RANK), vec(KV_RANK)],
        compiler_params=_params(("arbitrary",)),
    )(proj, dq, dk, dv, drg, dcqn, dckvn, dkr8, *tabs, qn, kvn)


def _post(o_ret, proj, gn):
    L, W = o_ret.shape
    tr = _tile(L, 384)

    def body(o_ref, rg_ref, gn_ref, out_ref):
        for h in range(HEADS):
            sl = slice(h * HD, (h + 1) * HD)
            o = o_ref[:, sl]
            rg = rg_ref[:, sl].astype(F32)
            n = o * lax.rsqrt(jnp.mean(o * o, axis=-1, keepdims=True) + EPS)
            out_ref[:, sl] = (n * gn_ref[:, sl] * (rg * _sigmoid(rg))).astype(BF16)

    row = pl.BlockSpec((tr, W), lambda i: (i, 0))
    return pl.pallas_call(
        body, name="ret_post", grid=(L // tr,), out_shape=jax.ShapeDtypeStruct((L, W), BF16),
        in_specs=[row, pl.BlockSpec((tr, W), lambda i: (i, C_RG // W)), pl.BlockSpec((1, W), lambda i: (0, 0))],
        out_specs=row, compiler_params=_params(("parallel",)),
    )(o_ret, proj, gn)


def _post_bwd(o_ret, proj, gn, dcat):
    L, W = o_ret.shape
    tr = _tile(L, 384)

    def body(o_ref, rg_ref, gn_ref, d_ref, do_ref, drg_ref, dgn_ref):
        @pl.when(pl.program_id(0) == 0)
        def _():
            dgn_ref[...] = jnp.zeros_like(dgn_ref)

        for h in range(HEADS):
            sl = slice(h * HD, (h + 1) * HD)
            o = o_ref[:, sl]
            rg = rg_ref[:, sl].astype(F32)
            d = d_ref[:, sl].astype(F32)
            gw = gn_ref[:, sl]
            r = lax.rsqrt(jnp.mean(o * o, axis=-1, keepdims=True) + EPS)
            n = o * r
            sg = _sigmoid(rg)
            si = rg * sg
            dn = d * gw * si
            dgn_ref[:, sl] += jnp.sum(d * n * si, axis=0, keepdims=True)
            drg_ref[:, sl] = (d * n * gw * sg * (1.0 + rg * (1.0 - sg))).astype(drg_ref.dtype)
            do_ref[:, sl] = (r * (dn - o * (r * r) * jnp.mean(dn * o, axis=-1, keepdims=True))).astype(BF16)

    row = pl.BlockSpec((tr, W), lambda i: (i, 0))
    vec = pl.BlockSpec((1, W), lambda i: (0, 0))
    return pl.pallas_call(
        body, name="ret_post_bwd", grid=(L // tr,),
        out_shape=[jax.ShapeDtypeStruct((L, W), BF16), jax.ShapeDtypeStruct((L, W), BF16),
                   jax.ShapeDtypeStruct((1, W), F32)],
        in_specs=[row, pl.BlockSpec((tr, W), lambda i: (i, C_RG // W)), vec, row],
        out_specs=[row, row, vec], compiler_params=_params(("arbitrary",)),
    )(o_ret, proj, gn, dcat)


RET_HEADS_PER_STEP = 4
RET_CHUNK = 512


def _ret_scans(name, arrays, scans, lg, out_dtype, single_buffered):
    L, W = arrays[0].shape
    G, C = RET_HEADS_PER_STEP, RET_CHUNK
    assert (L - BLK) % C == 0
    nc = (L - BLK) // C
    na, ns = len(arrays), len(scans)

    def body(*refs):
        in_refs, lg_ref, o_refs, s_ref = refs[:na], refs[na], refs[na + 1:na + 1 + ns], refs[-1]
        n = lax.broadcasted_iota(jnp.int32, (C, C), 0).astype(F32)
        m = lax.broadcasted_iota(jnp.int32, (C, C), 1).astype(F32)
        r = lax.broadcasted_iota(jnp.int32, (C, HD), 0).astype(F32)
        n0 = lax.broadcasted_iota(jnp.int32, (BLK, BLK), 0).astype(F32)
        m0 = lax.broadcasted_iota(jnp.int32, (BLK, BLK), 1).astype(F32)
        r0 = lax.broadcasted_iota(jnp.int32, (BLK, HD), 0).astype(F32)
        meta = (n0 < N_META) & (m0 < N_META)
        ways = {way for _, _, _, way in scans}
        consts = []
        for g in range(G):
            lgv = lg_ref[g, 0:1, 0:1]
            c = dict(gl=jnp.exp(lgv * float(C)))
            if 'f' in ways:
                c['f'] = dict(
                    dmask=jnp.where(n >= m, jnp.exp(lgv * jnp.maximum(n - m, 0.0)), 0.0),
                    dmask0=jnp.where(meta & (n0 >= m0), jnp.exp(lgv * jnp.maximum(n0 - m0, 0.0)), 0.0),
                    inter=jnp.exp(lgv * (r + 1.0)), upd=jnp.exp(lgv * (float(C) - 1.0 - r)),
                    upd0=jnp.where(r0 < N_META, jnp.exp(lgv * jnp.maximum(float(N_META) - 1.0 - r0, 0.0)), 0.0))
            if 'r' in ways:
                c['r'] = dict(
                    dmask=jnp.where(m >= n, jnp.exp(lgv * jnp.maximum(m - n, 0.0)), 0.0),
                    dmask0=jnp.where(meta & (m0 >= n0), jnp.exp(lgv * jnp.maximum(m0 - n0, 0.0)), 0.0),
                    inter=jnp.exp(lgv * (float(C) - r)), upd=jnp.exp(lgv * r),
                    inter0=jnp.where(r0 < N_META, jnp.exp(lgv * jnp.maximum(float(N_META) - r0, 0.0)), 0.0))
            consts.append(c)

        def chunk(chunk_of):
            results = []
            for g in range(G):
                cols = slice(g * HD, (g + 1) * HD)
                for s, (qi, ki, vi, way) in enumerate(scans):
                    rows = pl.ds(pl.multiple_of(BLK + chunk_of[way] * C, BLK), C)
                    cg, state = consts[g][way], s_ref[s, g]
                    qc, kc, vc = in_refs[qi][rows, cols], in_refs[ki][rows, cols], in_refs[vi][rows, cols]
                    a = _dot(qc, kc, 'nt') * cg['dmask']
                    out = _dot(a.astype(BF16), vc, 'nn') + _dot(qc, state.astype(BF16), 'nn') * cg['inter']
                    new = state * consts[g]['gl'] + _dot((kc.astype(F32) * cg['upd']).astype(BF16), vc, 'tn')
                    results.append((s, g, rows, cols, out, new))
            for s, g, rows, cols, out, new in results:
                o_refs[s][rows, cols] = out.astype(out_dtype)
                s_ref[s, g] = new

        def first_chunk(s):
            qi, ki, vi, way = scans[s]
            for g in range(G):
                cols = slice(g * HD, (g + 1) * HD)
                cg = consts[g][way]
                q0, k0, v0 = in_refs[qi][0:BLK, cols], in_refs[ki][0:BLK, cols], in_refs[vi][0:BLK, cols]
                o0 = _dot((_dot(q0, k0, 'nt') * cg['dmask0']).astype(BF16), v0, 'nn')
                if way == 'r':
                    o0 = o0 + _dot(q0, s_ref[s, g].astype(BF16), 'nn') * cg['inter0']
                else:
                    s_ref[s, g] = _dot((k0.astype(F32) * cg['upd0']).astype(BF16), v0, 'tn')
                o_refs[s][0:BLK, cols] = o0.astype(out_dtype)

        s_ref[...] = jnp.zeros_like(s_ref)
        for s in range(ns):
            if scans[s][3] == 'f':
                first_chunk(s)

        def step(t, carry):
            chunk({'f': t, 'r': nc - 1 - t})
            return carry

        lax.fori_loop(0, nc, step, 0)
        for s in range(ns):
            if scans[s][3] == 'r':
                first_chunk(s)

    mode = dict(pipeline_mode=pl.Buffered(1)) if single_buffered else {}
    col = pl.BlockSpec((L, G * HD), lambda h: (0, h), **mode)
    return pl.pallas_call(
        body, name=name, grid=(HEADS // G,), out_shape=[jax.ShapeDtypeStruct((L, W), out_dtype)] * ns,
        in_specs=[col] * na + [pl.BlockSpec((G, 8, HD), lambda h: (h, 0, 0))], out_specs=[col] * ns,
        scratch_shapes=[pltpu.VMEM((ns, G, HD, HD), F32)], compiler_params=_params(("parallel",)),
    )(*arrays, lg)


ATT_SCALE = (HD + ROPE) ** -0.5
LOG2E = 1.4426950408889634
Q_PRESCALE = ATT_SCALE * LOG2E
NEG = -1e30


ATT_TILE = 384
ATT_HEADS_PER_STEP = 8
ATT_BWD_HEADS_PER_STEP = 4


def _att_valid(nq, nk, row0, col0):
    r = lax.broadcasted_iota(jnp.int32, (nq, nk), 0) + row0
    c = lax.broadcasted_iota(jnp.int32, (nq, nk), 1) + col0
    return (c <= r) & ((c < N_META) | (c >= BLK))


def _store_rows(ref, g, first, col):
    wide = jnp.broadcast_to(col, (col.shape[0], HD))
    for c in range(col.shape[0] // BLK):
        ref[g, first + c] = jnp.transpose(wide[c * BLK:(c + 1) * BLK, :])[0:8, :]


def _load_row(ref, g, first, n):
    return jnp.concatenate([ref[g, first + c, 0:1, :] for c in range(n)], axis=1)


def _attn_fwd(qm, kn, krr, vm, comm=None):
    L = qm.shape[0]
    W = HEADS * HD
    T = _tile(L, ATT_TILE, BLK)
    nb = L // T
    G = ATT_HEADS_PER_STEP
    n_cm = comm.n if comm is not None else 0

    def body(*refs):
        q_ref, kn_ref, kr_ref, v_ref = refs[:4]
        o_ref, lse_ref = refs[4 + n_cm:6 + n_cm]
        m_sc, l_sc, acc_sc = refs[6 + 2 * n_cm:9 + 2 * n_cm]
        if comm is not None:
            cm_refs = (refs[4:4 + n_cm], refs[6 + n_cm:6 + 2 * n_cm], refs[9 + 2 * n_cm:])
            first, last = _grid_edges((HEADS // G, nb))

            @pl.when(first)
            def _():
                comm.start(*cm_refs)

        i = pl.program_id(1)
        m_sc[...] = jnp.full_like(m_sc, NEG)
        l_sc[...] = jnp.zeros_like(l_sc)
        acc_sc[...] = jnp.zeros_like(acc_sc)

        def tile(j, masked):
            rows = pl.ds(pl.multiple_of(j * T, T), T)
            kr = kr_ref[rows, :]
            valid = _att_valid(T, T, i * T, j * T) if masked else None
            ones = jnp.ones((T, HD), BF16)
            m_prev = [m_sc[g] for g in range(G)]
            l_prev = [l_sc[g] for g in range(G)]
            acc_prev = [acc_sc[g] for g in range(G)]
            m_new, l_new, acc_new = [], [], []
            for g in range(G):
                k = jnp.concatenate([kn_ref[rows, g * HD:(g + 1) * HD], kr], axis=1)
                s = _dot(q_ref[:, g * QH:(g + 1) * QH], k, 'nt')
                if masked:
                    s = jnp.where(valid, s, NEG)
                m_new.append(jnp.maximum(m_prev[g], jnp.max(s, axis=-1, keepdims=True)))
                p = jnp.exp2(s - m_new[g])
                alpha = jnp.exp2(m_prev[g] - m_new[g])
                pv = _dot(p.astype(BF16), jnp.concatenate([v_ref[rows, g * HD:(g + 1) * HD], ones], axis=1), 'nn')
                l_new.append(alpha * l_prev[g] + pv[:, HD:HD + 1])
                acc_new.append(alpha * acc_prev[g] + pv[:, 0:HD])
            for g in range(G):
                m_sc[g] = m_new[g]
                l_sc[g] = l_new[g]
                acc_sc[g] = acc_new[g]

        tile(0, True)

        def mid(j, carry):
            tile(j, False)
            return carry

        lax.fori_loop(1, i, mid, 0)

        @pl.when(i > 0)
        def _():
            tile(i, True)

        for g in range(G):
            l = l_sc[g]
            o_ref[:, g * HD:(g + 1) * HD] = (acc_sc[g] / l).astype(o_ref.dtype)
            _store_rows(lse_ref, g, 0, m_sc[g] + jnp.log(l) * LOG2E)

        if comm is not None:
            @pl.when(last)
            def _():
                comm.finish(*cm_refs)

    cm_specs = comm.specs if comm is not None else []
    res = pl.pallas_call(
        body, name="attn_fwd", grid=(HEADS // G, nb),
        out_shape=[jax.ShapeDtypeStruct((L, W), BF16), jax.ShapeDtypeStruct((HEADS, L // BLK, 8, HD), F32)]
        + (comm.out_shapes if comm is not None else []),
        in_specs=[pl.BlockSpec((T, G * QH), lambda h, i: (i, h)), pl.BlockSpec((L, G * HD), lambda h, i: (0, h)),
                  pl.BlockSpec((L, HD), lambda h, i: (0, 0)), pl.BlockSpec((L, G * HD), lambda h, i: (0, h))]
        + cm_specs,
        out_specs=[pl.BlockSpec((T, G * HD), lambda h, i: (i, h)),
                   pl.BlockSpec((G, T // BLK, 8, HD), lambda h, i: (h, i, 0, 0))] + cm_specs,
        scratch_shapes=[pltpu.VMEM((G, T, 1), F32), pltpu.VMEM((G, T, 1), F32), pltpu.VMEM((G, T, HD), F32)]
        + (comm.scratch if comm is not None else []),
        compiler_params=_params(("arbitrary", "arbitrary")),
    )(qm, kn, krr, vm, *(comm.arrays if comm is not None else []))
    return res[:2], res[2:]


def _attn_bwd(qm, kn, krr, vm, o, dcat, lse, comm=None):
    L = qm.shape[0]
    W = HEADS * HD
    T = _tile(L, ATT_TILE, BLK)
    nb, nr = L // T, T // BLK
    G = ATT_BWD_HEADS_PER_STEP
    n_cm = comm.n if comm is not None else 0

    def body(*refs):
        q_ref, kn_ref, kr_ref, v_ref, o_ref, do_ref, lse_ref = refs[:7]
        dq_ref, dkn_ref, dkr_ref, dv_ref = refs[7 + n_cm:11 + n_cm]
        dl_sc, dk_sc, dv_sc = refs[11 + 2 * n_cm:14 + 2 * n_cm]
        if comm is not None:
            cm_refs = (refs[7:7 + n_cm], refs[11 + n_cm:11 + 2 * n_cm], refs[14 + 2 * n_cm:])
            first, last = _grid_edges((HEADS // G, nb))

            @pl.when(first)
            def _():
                comm.start(*cm_refs)

        j = pl.program_id(1)
        qs = lambda g: slice(g * QH, (g + 1) * QH)
        hs = lambda g: slice(g * HD, (g + 1) * HD)

        @pl.when(j == 0)
        def _():
            dq_ref[...] = jnp.zeros_like(dq_ref)

            def rowsum(t, carry):
                rows = pl.ds(pl.multiple_of(t * T, T), T)
                for g in range(G):
                    _store_rows(dl_sc, g, t * nr, jnp.sum(
                        do_ref[rows, hs(g)].astype(F32) * o_ref[rows, hs(g)].astype(F32), axis=-1, keepdims=True))
                return carry

            lax.fori_loop(0, nb, rowsum, 0)

        kr = kr_ref[...]
        ks = [jnp.concatenate([kn_ref[:, hs(g)], kr], axis=1) for g in range(G)]
        vs = [v_ref[:, hs(g)] for g in range(G)]
        dk_sc[...] = jnp.zeros_like(dk_sc)
        dv_sc[...] = jnp.zeros_like(dv_sc)

        def tile(i, masked):
            rows = pl.ds(pl.multiple_of(i * T, T), T)
            if masked:
                key = lax.broadcasted_iota(jnp.int32, (T, T), 0) + j * T
                qry = lax.broadcasted_iota(jnp.int32, (T, T), 1) + i * T
                valid = (key <= qry) & ((key < N_META) | (key >= BLK))
            for g in range(G):
                q = q_ref[rows, qs(g)]
                do = do_ref[rows, hs(g)]
                s = _dot(ks[g], q, 'nt')
                if masked:
                    s = jnp.where(valid, s, NEG)
                p = jnp.exp2(s - _load_row(lse_ref, g, i * nr, nr))
                dv_sc[g] += _dot(p.astype(BF16), do, 'nn')
                ds = (p * (_dot(vs[g], do, 'nt') - _load_row(dl_sc, g, i * nr, nr))).astype(BF16)
                dk_sc[g] += _dot(ds, q, 'nn')
                dq_ref[rows, qs(g)] += _dot(ds, ks[g], 'tn')

        tile(j, True)

        def rest(masked):
            def step(i, carry):
                tile(i, masked)
                return carry
            lax.fori_loop(j + 1, nb, step, 0)

        @pl.when(j == 0)
        def _():
            rest(True)

        @pl.when(j > 0)
        def _():
            rest(False)

        for g in range(G):
            dk = dk_sc[g] * (1.0 / LOG2E)
            dkn_ref[:, hs(g)] = dk[:, 0:HD].astype(BF16)
            dkr_ref[:, hs(g)] = dk[:, HD:QH].astype(dkr_ref.dtype)
            dv_ref[:, hs(g)] = dv_sc[g].astype(BF16)

        if comm is not None:
            @pl.when(last)
            def _():
                comm.finish(*cm_refs)

    blk = pl.BlockSpec((T, G * HD), lambda h, j: (j, h))
    once = pl.Buffered(1)
    cm_specs = comm.specs if comm is not None else []
    res = pl.pallas_call(
        body, name="attn_bwd", grid=(HEADS // G, nb),
        out_shape=[jax.ShapeDtypeStruct((L, HEADS * QH), F32), jax.ShapeDtypeStruct((L, W), BF16),
                   jax.ShapeDtypeStruct((L, W), BF16), jax.ShapeDtypeStruct((L, W), BF16)]
        + (comm.out_shapes if comm is not None else []),
        in_specs=[pl.BlockSpec((L, G * QH), lambda h, j: (0, h), pipeline_mode=once), blk,
                  pl.BlockSpec((T, HD), lambda h, j: (j, 0)), blk,
                  pl.BlockSpec((L, G * HD), lambda h, j: (0, h), pipeline_mode=once),
                  pl.BlockSpec((L, G * HD), lambda h, j: (0, HEADS // G + h), pipeline_mode=once),
                  pl.BlockSpec((G, L // BLK, 8, HD), lambda h, j: (h, 0, 0, 0))] + cm_specs,
        out_specs=[pl.BlockSpec((L, G * QH), lambda h, j: (0, h), pipeline_mode=once), blk, blk, blk] + cm_specs,
        scratch_shapes=[pltpu.VMEM((G, L // BLK, 8, HD), F32), pltpu.VMEM((G, T, QH), F32),
                        pltpu.VMEM((G, T, HD), F32)]
        + (comm.scratch if comm is not None else []),
        compiler_params=_params(("arbitrary", "arbitrary")),
    )(qm, kn, krr, vm, o, dcat, lse, *(comm.arrays if comm is not None else []))
    return res[:4], res[4:]


def _unrope_q(dqm, tabs_m):
    L, W = dqm.shape
    tr = _tile(L, 384)

    def body(d_ref, cm_ref, sa_ref, sb_ref, out_ref):
        cm, sa, sb = cm_ref[...], sa_ref[...], sb_ref[...]
        for h in range(HEADS):
            out_ref[:, h * QH:h * QH + HD] = (d_ref[:, h * QH:h * QH + HD] * ATT_SCALE).astype(BF16)
            out_ref[:, h * QH + HD:(h + 1) * QH] = _rope_mla_t(d_ref[:, h * QH + HD:(h + 1) * QH] * ATT_SCALE, cm, sa,
                                                               sb).astype(BF16)

    row = pl.BlockSpec((tr, W), lambda i: (i, 0))
    tab = pl.BlockSpec((tr, HD), lambda i: (i, 0))
    return pl.pallas_call(
        body, name="unrope_q", grid=(L // tr,), out_shape=jax.ShapeDtypeStruct((L, W), BF16),
        in_specs=[row, tab, tab, tab], out_specs=row, compiler_params=_params(("parallel",)),
    )(dqm, *tabs_m)


def _q_up(cqn, wuq_p, tabs_m):
    L = cqn.shape[0]
    tm = _tile(L, 704)

    def ep(acc, cm, sa, sb):
        acc = acc * Q_PRESCALE
        parts = []
        for h in range(HEADS):
            parts.append(acc[:, h * QH:h * QH + HD])
            parts.append(_rope_mla(acc[:, h * QH + HD:(h + 1) * QH], cm, sa, sb))
        return (jnp.concatenate(parts, axis=1),)

    tab = pl.BlockSpec((tm, HD), lambda i, j: (i, 0))
    return _mm("mla_q_up", (L // tm, 1), ("parallel", "parallel"), None,
               [cqn, wuq_p], [pl.BlockSpec((tm, Q_RANK), lambda i, j: (i, 0)),
                              pl.BlockSpec((HEADS * QH, Q_RANK), lambda i, j: (0, 0))],
               [(0, 1, 'nt', 0)], [(tm, HEADS * QH)], list(tabs_m), [tab] * 3, ep,
               [jax.ShapeDtypeStruct((L, HEADS * QH), BF16)], [pl.BlockSpec((tm, HEADS * QH), lambda i, j: (i, 0))])[0]


def _mix_out(cat, w_out, h_in, post, next_norm):
    L, K = cat.shape
    D = w_out.shape[1]
    tm, tk = _tile(L, 384), K
    row = pl.BlockSpec((tm, D), lambda i, k: (i, 0))
    vec = pl.BlockSpec((1, D), lambda i, k: (0, 0))
    return _mm("mix_out", (L // tm, K // tk), ("parallel", "arbitrary"), 1,
               [cat, w_out], [pl.BlockSpec((tm, tk), lambda i, k: (i, k)), pl.BlockSpec((tk, D), lambda i, k: (k, 0))],
               [(0, 1, 'nn', 0)], [(tm, D)], [h_in, post, next_norm], [row, vec, vec], _resnorm_epilogue(1.0, True),
               [jax.ShapeDtypeStruct((L, D), F32)] * 2 + [jax.ShapeDtypeStruct((L, D), BF16)], [row, row, row])


ADAM_BLOCK_ELEMS = 512 * 704


def _adam_math(w, g, m, v):
    m = ADAM_B1 * m + (1.0 - ADAM_B1) * g
    v = ADAM_B2 * v + (1.0 - ADAM_B2) * (g * g)
    m_hat = m / (1.0 - ADAM_B1 ** ADAM_STEP)
    v_hat = v / (1.0 - ADAM_B2 ** ADAM_STEP)
    delta = -ADAM_LR * (m_hat / (jnp.sqrt(v_hat) + ADAM_EPS) + ADAM_WD * w)
    return delta, m, v


def _adam(name, w, m, v, g_slots=None, g=None, after=None):
    R, C = w.shape
    tr, tc = _tile(R, max(16, ADAM_BLOCK_ELEMS // C // 16 * 16), 16), C
    if tr * tc > ADAM_BLOCK_ELEMS:
        tr, tc = R, _tile(C, max(128, ADAM_BLOCK_ELEMS // R // 128 * 128), 128)
    from_slots = g_slots is not None

    def body(w_ref, m_ref, v_ref, g_ref, *rest):
        go_ref, d_ref, mo_ref, vo_ref = rest[-4:]
        if from_slots:
            grad = g_ref[0].astype(F32)
            for s in range(1, N_DEV):
                grad = grad + g_ref[s].astype(F32)
        else:
            grad = g_ref[...]
        delta, mn, vn = _adam_math(w_ref[...], grad, m_ref[...], v_ref[...])
        go_ref[...] = grad
        d_ref[...] = delta
        mo_ref[...] = mn
        vo_ref[...] = vn

    row = pl.BlockSpec((tr, tc), lambda i, j: (i, j))
    gspec = pl.BlockSpec((N_DEV, tr, tc), lambda i, j: (0, i, j)) if from_slots else row
    order = [] if after is None else [after]
    return pl.pallas_call(
        body, name=name, grid=(R // tr, C // tc), out_shape=[jax.ShapeDtypeStruct((R, C), F32)] * 4,
        in_specs=[row, row, row, gspec] + [pl.BlockSpec(memory_space=pl.ANY)] * len(order), out_specs=[row] * 4,
        compiler_params=_params(("parallel", "parallel")),
    )(w, m, v, g_slots if from_slots else g, *order)


def _unblock(gathered):
    n, r, c = gathered.shape
    return jnp.transpose(gathered, (1, 0, 2)).reshape(r, n * c)


def _reblock(full, c):
    r = full.shape[0]
    return jnp.transpose(full[:, :N_DEV * c].reshape(r, N_DEV, c), (1, 0, 2))


def _step(x, target, w, mom, vel):
    S, D = x.shape[1], x.shape[2]
    L = S + BLK
    def sq(a, n):
        if a.ndim == 2:
            return a
        if n in TRANSPOSED:
            a = jnp.swapaxes(a, 1, 2)
        return a.reshape(a.shape[1:])

    def unsq(o, n):
        o = o.reshape((1,) + o.shape)
        return jnp.swapaxes(o, 1, 2) if n in TRANSPOSED else o

    p = {n: sq(w[n], n) for n in WEIGHTS if n != 'meta_tokens'}
    gather = lambda names: _Exchange([p[n].astype(BF16) for n in names], False)
    scatter = lambda blocks: _Exchange(blocks, True)
    in_s, uq_s = p['w_in'].shape[0], p['mla_w_uq'].shape[0]
    assert uq_s == HD + ROPE and N_DEV == HEADS, "a w_uq shard is one head's columns"
    tabs = _rope_tables(L)
    tabs_m = tabs[2:]
    lg = jnp.broadcast_to(jnp.log(1.0 - 2.0 ** (-5.0 - jnp.arange(HEADS, dtype=F32)))[:, None, None], (HEADS, 8, HD))
    R = {}

    wg1, meta = _exchange("gather_first", [p['ffn1_w_gate'].astype(BF16), w['meta_tokens']], False)
    h0 = jnp.concatenate([_unblock(meta), jnp.zeros((BLK - N_META, D), F32), x[0]], axis=0)
    a1 = _norm_fwd(h0, p['ffn1_pre_norm'])
    g1, (wu1,) = _ffn_gate(a1, wg1, comm=gather(['ffn1_w_up']))
    (u1, hid1), (wd1,) = _ffn_up_gated(a1, wu1, g1, comm=gather(['ffn1_w_down']))
    (f1, h1, um), (w_in_g,) = _ffn_down(hid1, wd1, h0, p['ffn1_post_norm'], next_norm=p['mix_pre_norm'],
                                        comm=gather(['w_in']))

    w_in = jnp.pad(w_in_g.reshape(N_DEV * in_s, D), ((0, D_INP - N_DEV * in_s), (0, 0)))
    proj, (uq_g, uk_g, uv_g, wout_g) = _mm_nt("mix_in", [(um, w_in)], BF16, tn_target=1664,
                                              comm=gather(['mla_w_uq', 'mla_w_uk', 'mla_w_uv', 'w_out']))
    wuq = jnp.pad(uq_g, ((0, 0), (0, QH - uq_s), (0, 0))).reshape(HEADS * QH, Q_RANK)
    wuk, wuv, w_out = _unblock(uk_g), _unblock(uv_g), wout_g.reshape(-1, D)
    qr, kr, vr, cqn, ckvn, krr = _prep(proj, tabs, p['mla_q_norm'], p['mla_kv_norm'])
    qm = _q_up(cqn, wuq, tabs_m)
    kn = _mm_nn("mla_k_up", ckvn, wuk, BF16)
    vm = _mm_nn("mla_v_up", ckvn, wuv, BF16)
    (o_mla, lse), (wg2, wu2) = _attn_fwd(qm, kn, krr, vm, comm=gather(['ffn2_w_gate', 'ffn2_w_up']))
    o_ret, = _ret_scans("ret_fwd", [qr, kr, vr], [(0, 1, 2, 'f')], lg, F32, False)
    ret = _post(o_ret, proj, p['ret_group_norm'])
    cat = jnp.concatenate([ret, o_mla], axis=1)
    m, h2, a2 = _mix_out(cat, w_out, h1, p['mix_post_norm'], p['ffn2_pre_norm'])

    (g2, u2, hid2), (wd2,) = _ffn_up(a2, wg2, wu2, comm=gather(['ffn2_w_down']))
    f2, h3 = _ffn_down(hid2, wd2, h2, p['ffn2_post_norm'])
    dh3, loss_blk = _loss(h3, target[0])

    dsmall = {}
    df2, dsmall['ffn2_post_norm'] = _norm_bwd(f2, p['ffn2_post_norm'], dh3, None, 0.5, BF16)
    dg2, du2 = _ffn_dhid(df2, wd2, g2, u2)
    dwd2 = _ffn_dwd(hid2, df2)
    (dwg2, dwu2), (R['ffn2_w_down'],) = _ffn_dwgu(a2, dg2, du2, comm=scatter([dwd2]))
    da2, (R['ffn2_w_gate'],) = _ffn_da(dg2, du2, wg2, wu2, comm=scatter([dwg2]))
    dh2, dsmall['ffn2_pre_norm'] = _norm_bwd(h2, p['ffn2_pre_norm'], da2, dh3, 1.0, F32)

    dm, dsmall['mix_post_norm'] = _norm_bwd(m, p['mix_post_norm'], dh2, None, 1.0, BF16)
    dcat = _mm_nt("mix_dcat", [(dm, w_out)], BF16)
    dwout = _mm_tn("mix_dwout", cat, [dm])[0]
    do_ret, drg, dsmall['ret_group_norm'] = _post_bwd(o_ret, proj, p['ret_group_norm'], dcat)
    dqr, dkr, dvr = _ret_scans("ret_bwd", [do_ret, qr, kr, vr], [(0, 3, 2, 'f'), (3, 0, 1, 'r'), (2, 1, 0, 'r')],
                               lg, BF16, True)
    (dqm, dkn, dkr8, dvm), (R['ffn2_w_up'], R['w_out']) = _attn_bwd(
        qm, kn, krr, vm, o_mla, dcat, lse, comm=scatter([dwu2, dwout.reshape(N_DEV, -1, D)]))
    dqp = _unrope_q(dqm, tabs_m)
    dwuq = _mm_tn("mla_dwuq", dqp, [cqn])[0]
    dcqn = _mm_nn("mla_dcq", dqp, wuq, F32)
    dwuk, dwuv = _mm_tn("mla_dwukv", ckvn, [dkn, dvm])
    dckvn = _mm_nt("mla_dckv", [(dkn, wuk), (dvm, wuv)], F32)
    dproj, dsmall['mla_q_norm'], dsmall['mla_kv_norm'] = _prep_bwd(
        proj, dqr, dkr, dvr, drg, dcqn, dckvn, dkr8, tabs, p['mla_q_norm'], p['mla_kv_norm'])
    dwuq_b = dwuq.reshape(HEADS, QH, Q_RANK)[:, :uq_s]
    (dwin,), (R['mla_w_uq'], R['mla_w_uk'], R['mla_w_uv']) = _mm_tn(
        "mix_dwin", dproj, [um], comm=scatter([dwuq_b, _reblock(dwuk, p['mla_w_uk'].shape[1]),
                                               _reblock(dwuv, p['mla_w_uv'].shape[1])]))
    dwin_b = dwin[:N_DEV * in_s].reshape(N_DEV, in_s, D)
    half = D // 2
    dum, (r_win_a,) = _mm_nn("mix_du", dproj, w_in, F32, tn_target=512, comm=scatter([dwin_b[:, :, :half]]))
    dh1, dsmall['mix_pre_norm'] = _norm_bwd(h1, p['mix_pre_norm'], dum, dh2, 1.0, F32)

    df1, dsmall['ffn1_post_norm'] = _norm_bwd(f1, p['ffn1_post_norm'], dh1, None, 0.5, BF16)
    (dg1, du1), (r_win_b,) = _ffn_dhid(df1, wd1, g1, u1, comm=scatter([dwin_b[:, :, half:]]))
    R['w_in'] = jnp.concatenate([r_win_a, r_win_b], axis=2)
    dwd1 = _ffn_dwd(hid1, df1)
    (dwg1, dwu1), (R['ffn1_w_down'],) = _ffn_dwgu(a1, dg1, du1, comm=scatter([dwd1]))
    da1, (R['ffn1_w_gate'],) = _ffn_da(dg1, du1, wg1, wu1, comm=scatter([dwg1]))
    dh0, dsmall['ffn1_pre_norm'] = _norm_bwd(h0, p['ffn1_pre_norm'], da1, dh1, 1.0, F32)
    tail_sems_s, tail_sems_r, tail_src, tail_land, token = _scatter_start(dwu1)

    def slab(a):
        a = a.reshape(-1, 128)
        return jnp.pad(a, ((0, (-a.shape[0]) % 8), (0, 0)))

    slab_rows = lambda n: -(-(p[n].shape[-1] // 128) // 8) * 8
    packed = jnp.concatenate([slab(dsmall[n]) for n in SMALL] + [slab(dh0[:N_META]), loss_blk], axis=0)
    red = _allreduce_small(packed + token[0, 0])
    offs = sum(slab_rows(n) for n in SMALL)
    n_small = offs
    gmeta_full = red[offs:offs + N_META * D // 128].reshape(N_META, D)
    offs += N_META * D // 128
    loss = red[offs, 0]

    grad, delta, new_m, new_v = {}, {}, {}, {}
    meanwhile = []
    for n in BIG:
        if n == 'ffn1_w_up':
            continue
        outs = _adam("adam_" + n, p[n], sq(mom[n], n), sq(vel[n], n), g_slots=R[n], after=token)
        meanwhile.append(outs[0])
        grad[n], delta[n], new_m[n], new_v[n] = [unsq(o, n) for o in outs]
    pack = lambda d: jnp.concatenate([slab(d[n]) for n in SMALL], axis=0)
    outs = _adam("adam_small", pack(w), pack(mom), pack(vel), g=red[:n_small])
    meanwhile.append(outs[0])
    offs = 0
    for n in SMALL:
        r = p[n].shape[-1] // 128
        grad[n], delta[n], new_m[n], new_v[n] = [o[offs:offs + r].reshape(w[n].shape) for o in outs]
        offs += slab_rows(n)
    dev = 4 * lax.axis_index("x") + 2 * lax.axis_index("y") + lax.axis_index("c")
    mcols = w['meta_tokens'].shape[1]
    gmeta = lax.dynamic_slice(gmeta_full, (0, dev * mcols), (N_META, mcols))
    outs = _adam("adam_meta", w['meta_tokens'], mom['meta_tokens'], vel['meta_tokens'], g=gmeta)
    grad['meta_tokens'], delta['meta_tokens'], new_m['meta_tokens'], new_v['meta_tokens'] = outs
    meanwhile.append(outs[0])
    n = 'ffn1_w_up'
    slots = _scatter_wait(tail_sems_s, tail_sems_r, tail_src, tail_land, meanwhile)
    outs = _adam("adam_" + n, p[n], sq(mom[n], n), sq(vel[n], n), g_slots=slots)
    grad[n], delta[n], new_m[n], new_v[n] = [unsq(o, n) for o in outs]

    return (loss, dh0[BLK:][None], *[grad[n] for n in WEIGHTS], *[delta[n] for n in WEIGHTS],
            *[new_m[n] for n in WEIGHTS], *[new_v[n] for n in WEIGHTS])


def kernel(x, meta_tokens, ffn1_pre_norm, ffn1_w_gate, ffn1_w_up, ffn1_w_down, ffn1_post_norm, mix_pre_norm, w_in, ret_group_norm, mla_q_norm, mla_w_uq, mla_kv_norm, mla_w_uk, mla_w_uv, w_out, mix_post_norm, ffn2_pre_norm, ffn2_w_gate, ffn2_w_up, ffn2_w_down, ffn2_post_norm, loss_target, m_meta_tokens, m_ffn1_pre_norm, m_ffn1_w_gate, m_ffn1_w_up, m_ffn1_w_down, m_ffn1_post_norm, m_mix_pre_norm, m_w_in, m_ret_group_norm, m_mla_q_norm, m_mla_w_uq, m_mla_kv_norm, m_mla_w_uk, m_mla_w_uv, m_w_out, m_mix_post_norm, m_ffn2_pre_norm, m_ffn2_w_gate, m_ffn2_w_up, m_ffn2_w_down, m_ffn2_post_norm, v_meta_tokens, v_ffn1_pre_norm, v_ffn1_w_gate, v_ffn1_w_up, v_ffn1_w_down, v_ffn1_post_norm, v_mix_pre_norm, v_w_in, v_ret_group_norm, v_mla_q_norm, v_mla_w_uq, v_mla_kv_norm, v_mla_w_uk, v_mla_w_uv, v_w_out, v_mix_post_norm, v_ffn2_pre_norm, v_ffn2_w_gate, v_ffn2_w_up, v_ffn2_w_down, v_ffn2_post_norm):
    w = dict(zip(WEIGHTS, (meta_tokens, ffn1_pre_norm, ffn1_w_gate, ffn1_w_up, ffn1_w_down, ffn1_post_norm,
                           mix_pre_norm, w_in, ret_group_norm, mla_q_norm, mla_w_uq, mla_kv_norm, mla_w_uk, mla_w_uv,
                           w_out, mix_post_norm, ffn2_pre_norm, ffn2_w_gate, ffn2_w_up, ffn2_w_down, ffn2_post_norm)))
    mom = dict(zip(WEIGHTS, (m_meta_tokens, m_ffn1_pre_norm, m_ffn1_w_gate, m_ffn1_w_up, m_ffn1_w_down,
                             m_ffn1_post_norm, m_mix_pre_norm, m_w_in, m_ret_group_norm, m_mla_q_norm, m_mla_w_uq,
                             m_mla_kv_norm, m_mla_w_uk, m_mla_w_uv, m_w_out, m_mix_post_norm, m_ffn2_pre_norm,
                             m_ffn2_w_gate, m_ffn2_w_up, m_ffn2_w_down, m_ffn2_post_norm)))
    vel = dict(zip(WEIGHTS, (v_meta_tokens, v_ffn1_pre_norm, v_ffn1_w_gate, v_ffn1_w_up, v_ffn1_w_down,
                             v_ffn1_post_norm, v_mix_pre_norm, v_w_in, v_ret_group_norm, v_mla_q_norm, v_mla_w_uq,
                             v_mla_kv_norm, v_mla_w_uk, v_mla_w_uv, v_w_out, v_mix_post_norm, v_ffn2_pre_norm,
                             v_ffn2_w_gate, v_ffn2_w_up, v_ffn2_w_down, v_ffn2_post_norm)))
    return _step(x, loss_target, w, mom, vel)
```

```python
import functools
import math

import jax
import jax.numpy as jnp
from jax import lax
from jax.experimental import pallas as pl
from jax.experimental.pallas import tpu as pltpu

N_DEV = 8
N_META = 16
BLK = 128
HEADS = 8
HD = 128
ROPE = 64
Q_RANK = 512
KV_RANK = 256
QH = 2 * HD
D_INP = 4 * HEADS * HD + Q_RANK + KV_RANK + BLK
ROPE_THETA = 10000.0
EPS = 1e-6
ADAM_LR = 0.001
ADAM_B1 = 0.9
ADAM_B2 = 0.999
ADAM_EPS = 1e-08
ADAM_WD = 0.01
ADAM_STEP = 10
V7X_VMEM_LIMIT = 48 * 1024 * 1024
MESH = pl.DeviceIdType.MESH
F32 = jnp.float32
BF16 = jnp.bfloat16

WEIGHTS = ['meta_tokens', 'ffn1_pre_norm', 'ffn1_w_gate', 'ffn1_w_up', 'ffn1_w_down', 'ffn1_post_norm',
           'mix_pre_norm', 'w_in', 'ret_group_norm', 'mla_q_norm', 'mla_w_uq', 'mla_kv_norm', 'mla_w_uk',
           'mla_w_uv', 'w_out', 'mix_post_norm', 'ffn2_pre_norm', 'ffn2_w_gate', 'ffn2_w_up', 'ffn2_w_down',
           'ffn2_post_norm']
SMALL = ['ffn1_pre_norm', 'ffn1_post_norm', 'mix_pre_norm', 'ret_group_norm', 'mla_q_norm', 'mla_kv_norm',
         'mix_post_norm', 'ffn2_pre_norm', 'ffn2_post_norm']
TRANSPOSED = ('ffn1_w_gate', 'ffn1_w_up', 'ffn2_w_gate', 'ffn2_w_up', 'w_in', 'mla_w_uq')
BIG = ['ffn1_w_gate', 'ffn1_w_up', 'ffn1_w_down', 'w_in', 'mla_w_uq', 'mla_w_uk', 'mla_w_uv', 'w_out',
       'ffn2_w_gate', 'ffn2_w_up', 'ffn2_w_down']

_DIMS = {'nn': (((1,), (0,)), ((), ())), 'nt': (((1,), (1,)), ((), ())), 'tn': (((0,), (0,)), ((), ()))}


def _tile(n, target, mult=16):
    best = None
    for t in range(mult, min(n, target) + 1, mult):
        if n % t == 0:
            best = t
    return best if best is not None else n


def _params(sem):
    return pltpu.CompilerParams(dimension_semantics=sem, vmem_limit_bytes=V7X_VMEM_LIMIT)


def _dot(a, b, dims):
    return lax.dot_general(a, b, _DIMS[dims], preferred_element_type=F32)


def _sigmoid(x):
    return 0.5 * jnp.tanh(0.5 * x) + 0.5


def _me_and_peers():
    x, y, c = lax.axis_index("x"), lax.axis_index("y"), lax.axis_index("c")

    def peer(j):
        px = 1 - x if (j >> 2) & 1 else x
        py = 1 - y if (j >> 1) & 1 else y
        pc = 1 - c if j & 1 else c
        return (px, py, pc), 4 * px + 2 * py + pc

    return 4 * x + 2 * y + c, peer


class _Exchange:
    def __init__(self, arrays, per_peer):
        self.arrays = list(arrays)
        self.per_peer = per_peer
        self.n = len(self.arrays)
        self.out_shapes = [jax.ShapeDtypeStruct((N_DEV,) + tuple(a.shape[1:] if per_peer else a.shape), a.dtype)
                           for a in self.arrays]
        self.specs = [pl.BlockSpec(memory_space=pl.ANY)] * self.n
        self.scratch = [pltpu.SemaphoreType.DMA((7 * self.n,)), pltpu.SemaphoreType.DMA((7 * self.n,)),
                        pltpu.SemaphoreType.DMA((self.n,))]

    def _copies(self, src, dst, sems):
        send_sems, recv_sems, local_sems = sems
        me, peer = _me_and_peers()
        sib, _ = peer(1)
        local, sends, recvs, passes = [], {}, {}, {}
        for k in range(self.n):
            own = src[k].at[me] if self.per_peer else src[k]
            local.append(pltpu.make_async_copy(own, dst[k].at[me], local_sems.at[k]))
            for j in range(1, N_DEV):
                pid, pidx = peer(j)
                out = src[k].at[pidx] if self.per_peer else src[k]
                sem = dict(send_sem=send_sems.at[k * 7 + j - 1], recv_sem=recv_sems.at[k * 7 + j - 1])
                recvs[k, j] = pltpu.make_async_remote_copy(src_ref=out, dst_ref=dst[k].at[pidx], device_id=pid,
                                                           device_id_type=MESH, **sem)
                if self.per_peer or j in (1, 2, 4, 6):
                    sends[k, j] = pltpu.make_async_remote_copy(src_ref=out, dst_ref=dst[k].at[me], device_id=pid,
                                                               device_id_type=MESH, **sem)
                else:
                    _, origin = peer(j ^ 1)
                    passes[k, j ^ 1] = pltpu.make_async_remote_copy(
                        src_ref=dst[k].at[origin], dst_ref=dst[k].at[origin], device_id=sib, device_id_type=MESH, **sem)
        return local, sends, recvs, passes

    def start(self, src, dst, sems):
        local, sends, _, _ = self._copies(src, dst, sems)
        for cp in local + list(sends.values()):
            cp.start()

    def finish(self, src, dst, sems):
        local, sends, recvs, passes = self._copies(src, dst, sems)
        for key, cp in passes.items():
            recvs[key].wait_recv()
            cp.start()
        for key, cp in recvs.items():
            if key not in passes:
                cp.wait_recv()
        for cp in list(sends.values()) + list(passes.values()):
            cp.wait_send()
        for cp in local:
            cp.wait()


def _grid_edges(grid):
    first, last = None, None
    for a, n in enumerate(grid):
        f, l = pl.program_id(a) == 0, pl.program_id(a) == n - 1
        first = f if first is None else first & f
        last = l if last is None else last & l
    return first, last


def _exchange(name, arrays, per_peer):
    ex = _Exchange(arrays, per_peer)
    n = ex.n

    def body(*refs):
        ex.start(refs[:n], refs[n:2 * n], refs[2 * n:])
        ex.finish(refs[:n], refs[n:2 * n], refs[2 * n:])

    return pl.pallas_call(body, name=name, out_shape=ex.out_shapes, in_specs=ex.specs, out_specs=ex.specs,
                          scratch_shapes=ex.scratch)(*arrays)


def _scatter_start(blocks):
    def body(src_ref, land_ref, send_sems, recv_sems, src_thru, land_thru, token, local_sem):
        me, peer = _me_and_peers()
        local = pltpu.make_async_copy(src_ref.at[me], land_ref.at[me], local_sem)
        local.start()
        for j in range(1, N_DEV):
            pid, pidx = peer(j)
            pltpu.make_async_remote_copy(src_ref=src_ref.at[pidx], dst_ref=land_ref.at[me],
                                         send_sem=send_sems.at[j - 1], recv_sem=recv_sems.at[j - 1],
                                         device_id=pid, device_id_type=MESH).start()
        local.wait()
        token[...] = jnp.zeros_like(token)

    hbm = pl.BlockSpec(memory_space=pltpu.HBM)
    sem = pl.BlockSpec(memory_space=pltpu.SEMAPHORE)
    return pl.pallas_call(
        body, name="scatter_tail_start",
        out_shape=(pltpu.SemaphoreType.DMA((7,)), pltpu.SemaphoreType.DMA((7,)), pltpu.HBM(blocks.shape, blocks.dtype),
                   pltpu.HBM(blocks.shape, blocks.dtype), jax.ShapeDtypeStruct((8, 128), F32)),
        in_specs=(hbm, hbm), out_specs=(sem, sem, hbm, hbm, pl.BlockSpec(memory_space=pltpu.VMEM)),
        input_output_aliases={0: 2, 1: 3}, scratch_shapes=[pltpu.SemaphoreType.DMA],
        compiler_params=pltpu.CompilerParams(has_side_effects=pltpu.SideEffectType.DATAFLOW_SIDE_EFFECTING),
    )(pltpu.with_memory_space_constraint(blocks, pltpu.HBM),
      pltpu.with_memory_space_constraint(lax.empty(blocks.shape, blocks.dtype), pltpu.HBM))


def _scatter_wait(send_sems, recv_sems, src_thru, land_thru, after):
    n_after = len(after)

    def body(src_ref, land_ref, send_sems, recv_sems, *rest):
        me, peer = _me_and_peers()
        for j in range(1, N_DEV):
            pid, pidx = peer(j)
            cp = pltpu.make_async_remote_copy(src_ref=src_ref.at[pidx], dst_ref=land_ref.at[pidx],
                                              send_sem=send_sems.at[j - 1], recv_sem=recv_sems.at[j - 1],
                                              device_id=pid, device_id_type=MESH)
            cp.wait_send()
            cp.wait_recv()

    hbm = pl.BlockSpec(memory_space=pltpu.HBM)
    sem = pl.BlockSpec(memory_space=pltpu.SEMAPHORE)
    return pl.pallas_call(
        body, name="scatter_tail_wait",
        out_shape=(pltpu.HBM(src_thru.shape, src_thru.dtype), pltpu.HBM(land_thru.shape, land_thru.dtype)),
        in_specs=(hbm, hbm, sem, sem) + (pl.BlockSpec(memory_space=pl.ANY),) * n_after, out_specs=(hbm, hbm),
        input_output_aliases={0: 0, 1: 1},
        compiler_params=pltpu.CompilerParams(has_side_effects=pltpu.SideEffectType.DATAFLOW_SIDE_EFFECTING),
    )(src_thru, land_thru, send_sems, recv_sems, *after)[1]


def _allreduce_small(v):
    rows = v.shape[0]

    def body(v_ref, out_ref, buf, send_sems, recv_sems):
        me, peer = _me_and_peers()
        buf[pl.ds(me, 1)] = v_ref[...][None]
        sends = []
        for j in range(1, N_DEV):
            pid, _ = peer(j)
            cp = pltpu.make_async_remote_copy(src_ref=v_ref, dst_ref=buf.at[me], send_sem=send_sems.at[j - 1],
                                              recv_sem=recv_sems.at[j - 1], device_id=pid, device_id_type=MESH)
            cp.start()
            sends.append(cp)
        for j in range(1, N_DEV):
            pid, pidx = peer(j)
            pltpu.make_async_remote_copy(src_ref=v_ref, dst_ref=buf.at[pidx], send_sem=send_sems.at[j - 1],
                                         recv_sem=recv_sems.at[j - 1], device_id=pid,
                                         device_id_type=MESH).wait_recv()
        for cp in sends:
            cp.wait_send()
        acc = buf[0]
        for s in range(1, N_DEV):
            acc = acc + buf[s]
        out_ref[...] = acc

    vm = pl.BlockSpec(memory_space=pltpu.VMEM)
    return pl.pallas_call(
        body, name="allreduce_small", out_shape=jax.ShapeDtypeStruct(v.shape, F32),
        in_specs=[vm], out_specs=vm,
        scratch_shapes=[pltpu.VMEM((N_DEV, rows, 128), F32), pltpu.SemaphoreType.DMA((7,)),
                        pltpu.SemaphoreType.DMA((7,))],
    )(v)


def _mm(name, grid, sem, k_axis, ops, op_specs, pairs, acc_shapes, extras, extra_specs, epilogue, outs, out_specs,
        comm=None):
    n_op, n_ex, n_out = len(ops), len(extras), len(outs)
    nk = grid[k_axis] if k_axis is not None else 1
    n_acc = len(acc_shapes) if nk > 1 else 0
    n_cm = comm.n if comm is not None else 0

    def body(*refs):
        op_refs = refs[:n_op]
        ex_refs = refs[n_op:n_op + n_ex]
        n_in = n_op + n_ex + n_cm
        out_refs = refs[n_in:n_in + n_out]
        acc_refs = refs[n_in + n_out + n_cm:n_in + n_out + n_cm + n_acc]
        if comm is not None:
            cm_refs = (refs[n_op + n_ex:n_in], refs[n_in + n_out:n_in + n_out + n_cm],
                       refs[n_in + n_out + n_cm + n_acc:])
            first, last = _grid_edges(grid)

            @pl.when(first)
            def _():
                comm.start(*cm_refs)

        def finish(vals):
            res = epilogue(*vals, *[e[...] for e in ex_refs])
            for o, r in zip(out_refs, res):
                o[...] = r.astype(o.dtype)

        if nk == 1:
            parts = [None] * len(acc_shapes)
            for li, ri, dims, ai in pairs:
                d = _dot(op_refs[li][...], op_refs[ri][...], dims)
                parts[ai] = d if parts[ai] is None else parts[ai] + d
            finish(parts)
        else:
            k = pl.program_id(k_axis)

            @pl.when(k == 0)
            def _():
                for a in acc_refs:
                    a[...] = jnp.zeros_like(a)

            for li, ri, dims, ai in pairs:
                acc_refs[ai][...] += _dot(op_refs[li][...], op_refs[ri][...], dims)

            @pl.when(k == nk - 1)
            def _():
                finish([a[...] for a in acc_refs])

        if comm is not None:
            @pl.when(last)
            def _():
                comm.finish(*cm_refs)

    scratch = [pltpu.VMEM(s, F32) for s in acc_shapes] if nk > 1 else []
    if comm is None:
        return pl.pallas_call(
            body, name=name, grid=grid, out_shape=outs,
            in_specs=list(op_specs) + list(extra_specs), out_specs=list(out_specs),
            scratch_shapes=scratch, compiler_params=_params(sem),
        )(*ops, *extras)
    res = pl.pallas_call(
        body, name=name, grid=grid, out_shape=list(outs) + comm.out_shapes,
        in_specs=list(op_specs) + list(extra_specs) + comm.specs, out_specs=list(out_specs) + comm.specs,
        scratch_shapes=scratch + comm.scratch, compiler_params=_params(("arbitrary",) * len(grid)),
    )(*ops, *extras, *comm.arrays)
    return res[:n_out], res[n_out:]


def _with_comm(res, comm, pick):
    if comm is None:
        return pick(res)
    return pick(res[0]), res[1]


def _mm_nn(name, a, w, out_dtype, tm_target=704, tn_target=1664, epilogue=None, extras=(), extra_specs=(), comm=None):
    L, K = a.shape
    N = w.shape[1]
    tm, tn = _tile(L, tm_target), _tile(N, tn_target, 128)
    ep = epilogue if epilogue is not None else (lambda acc: (acc,))
    res = _mm(name, (L // tm, N // tn), ("parallel", "parallel"), None,
              [a, w], [pl.BlockSpec((tm, K), lambda i, j: (i, 0)), pl.BlockSpec((K, tn), lambda i, j: (0, j))],
              [(0, 1, 'nn', 0)], [(tm, tn)], list(extras), list(extra_specs), ep,
              [jax.ShapeDtypeStruct((L, N), out_dtype)], [pl.BlockSpec((tm, tn), lambda i, j: (i, j))], comm=comm)
    return _with_comm(res, comm, lambda o: o[0])


def _mm_nt(name, pairs_aw, out_dtype, tm_target=704, tn_target=512, comm=None):
    L = pairs_aw[0][0].shape[0]
    N = pairs_aw[0][1].shape[0]
    tm, tn = _tile(L, tm_target), _tile(N, tn_target, 128)
    ops, specs, pairs = [], [], []
    for t, (a, w) in enumerate(pairs_aw):
        K = a.shape[1]
        ops += [a, w]
        specs += [pl.BlockSpec((tm, K), lambda i, j: (i, 0)), pl.BlockSpec((tn, K), lambda i, j: (j, 0))]
        pairs.append((2 * t, 2 * t + 1, 'nt', 0))
    res = _mm(name, (L // tm, N // tn), ("parallel", "parallel"), None, ops, specs, pairs, [(tm, tn)], [], [],
              lambda acc: (acc,), [jax.ShapeDtypeStruct((L, N), out_dtype)],
              [pl.BlockSpec((tm, tn), lambda i, j: (i, j))], comm=comm)
    return _with_comm(res, comm, lambda o: o[0])


def _mm_tn(name, a, bs, out_dtype=BF16, tk_target=1408, tn_target=1664, tm_target=2048, comm=None):
    L, M = a.shape
    N = bs[0].shape[1]
    tk, tn, tm = _tile(L, tk_target), _tile(N, tn_target, 128), _tile(M, tm_target, 128)
    nb = len(bs)
    ops = [a] + list(bs)
    specs = [pl.BlockSpec((tk, tm), lambda i, j, k: (k, i))] + [pl.BlockSpec((tk, tn), lambda i, j, k: (k, j))] * nb
    res = _mm(name, (M // tm, N // tn, L // tk), ("parallel", "parallel", "arbitrary"), 2, ops, specs,
              [(0, 1 + t, 'tn', t) for t in range(nb)], [(tm, tn)] * nb, [], [], lambda *acc: acc,
              [jax.ShapeDtypeStruct((M, N), out_dtype)] * nb,
              [pl.BlockSpec((tm, tn), lambda i, j, k: (i, j))] * nb, comm=comm)
    return _with_comm(res, comm, lambda o: o)


def _norm_fwd(x, w):
    L, D = x.shape
    tr = _tile(L, 512)

    def body(x_ref, w_ref, y_ref):
        v = x_ref[...]
        r = lax.rsqrt(jnp.mean(v * v, axis=-1, keepdims=True) + EPS)
        y_ref[...] = (v * r * w_ref[...]).astype(y_ref.dtype)

    return pl.pallas_call(
        body, name="norm_fwd", grid=(L // tr,), out_shape=jax.ShapeDtypeStruct((L, D), BF16),
        in_specs=[pl.BlockSpec((tr, D), lambda i: (i, 0)), pl.BlockSpec((1, D), lambda i: (0, 0))],
        out_specs=pl.BlockSpec((tr, D), lambda i: (i, 0)), compiler_params=_params(("parallel",)),
    )(x, w)


def _norm_bwd_math(x, w, dy):
    r = lax.rsqrt(jnp.mean(x * x, axis=-1, keepdims=True) + EPS)
    gy = dy * w
    dx = r * (gy - x * (r * r) * jnp.mean(gy * x, axis=-1, keepdims=True))
    dw = jnp.sum(dy * x * r, axis=0, keepdims=True)
    return dx, dw


def _norm_bwd(x, w, dy, res, scale, out_dtype):
    L, D = x.shape
    tr = _tile(L, 384)
    has_res = res is not None

    def body(*refs):
        x_ref, w_ref, dy_ref = refs[:3]
        res_ref = refs[3] if has_res else None
        dx_ref, dw_ref = refs[-2:]
        dx, dw = _norm_bwd_math(x_ref[...], w_ref[...], dy_ref[...].astype(F32))
        dx = scale * dx
        if has_res:
            dx = dx + res_ref[...]
        dx_ref[...] = dx.astype(dx_ref.dtype)

        @pl.when(pl.program_id(0) == 0)
        def _():
            dw_ref[...] = jnp.zeros_like(dw_ref)

        dw_ref[...] += scale * dw

    row = pl.BlockSpec((tr, D), lambda i: (i, 0))
    vec = pl.BlockSpec((1, D), lambda i: (0, 0))
    return pl.pallas_call(
        body, name="norm_bwd", grid=(L // tr,),
        out_shape=[jax.ShapeDtypeStruct((L, D), out_dtype), jax.ShapeDtypeStruct((1, D), F32)],
        in_specs=[row, vec, row] + ([row] if has_res else []), out_specs=[row, vec],
        compiler_params=_params(("arbitrary",)),
    )(*([x, w, dy] + ([res] if has_res else [])))


def _loss(h, target):
    L, D = h.shape

    def body(h_ref, t_ref, dh_ref, loss_ref):
        i = pl.program_id(0)

        @pl.when(i == 0)
        def _():
            dh_ref[...] = jnp.zeros_like(dh_ref)
            loss_ref[...] = jnp.zeros_like(loss_ref)

        @pl.when(i > 0)
        def _():
            diff = h_ref[...] - t_ref[...]
            dh_ref[...] = diff * (1.0 / D)
            loss_ref[...] += 0.5 * jnp.sum(diff * diff) * (1.0 / D)

    return pl.pallas_call(
        body, name="loss", grid=(L // BLK,),
        out_shape=[jax.ShapeDtypeStruct((L, D), F32), jax.ShapeDtypeStruct((8, 128), F32)],
        in_specs=[pl.BlockSpec((BLK, D), lambda i: (i, 0)),
                  pl.BlockSpec((BLK, D), lambda i: (jnp.maximum(i - 1, 0), 0))],
        out_specs=[pl.BlockSpec((BLK, D), lambda i: (i, 0)), pl.BlockSpec((8, 128), lambda i: (0, 0))],
        compiler_params=_params(("arbitrary",)),
    )(h, target)


def _ffn_up(a, wg, wu, comm=None):
    L, D = a.shape
    F = wg.shape[1]
    tm = _tile(L, 704)

    def ep(g, u):
        return g, u, g * _sigmoid(g) * u

    hspec = pl.BlockSpec((None, tm, F), lambda i, j: (j, i, 0))
    wspec = pl.BlockSpec((None, F, D), lambda i, j: (j, 0, 0))
    res = _mm("ffn_up", (L // tm, N_DEV), ("parallel", "parallel"), None,
              [a, wg, wu], [pl.BlockSpec((tm, D), lambda i, j: (i, 0)), wspec, wspec],
              [(0, 1, 'nt', 0), (0, 2, 'nt', 1)], [(tm, F)] * 2, [], [], ep,
              [jax.ShapeDtypeStruct((N_DEV, L, F), BF16)] * 3, [hspec] * 3, comm=comm)
    return _with_comm(res, comm, lambda o: o)


def _ffn_gate(a, wg, comm=None):
    L, D = a.shape
    F = wg.shape[1]
    tm = _tile(L, 704)
    res = _mm("ffn_gate", (L // tm, N_DEV), ("parallel", "parallel"), None,
              [a, wg], [pl.BlockSpec((tm, D), lambda i, j: (i, 0)), pl.BlockSpec((None, F, D), lambda i, j: (j, 0, 0))],
              [(0, 1, 'nt', 0)], [(tm, F)], [], [], lambda g: (g,),
              [jax.ShapeDtypeStruct((N_DEV, L, F), BF16)], [pl.BlockSpec((None, tm, F), lambda i, j: (j, i, 0))],
              comm=comm)
    return _with_comm(res, comm, lambda o: o[0])


def _ffn_up_gated(a, wu, g, comm=None):
    L, D = a.shape
    F = wu.shape[1]
    tm = _tile(L, 704)

    def ep(u, g_):
        g32 = g_.astype(F32)
        return u, g32 * _sigmoid(g32) * u

    hspec = pl.BlockSpec((None, tm, F), lambda i, j: (j, i, 0))
    res = _mm("ffn_up_gated", (L // tm, N_DEV), ("parallel", "parallel"), None,
              [a, wu], [pl.BlockSpec((tm, D), lambda i, j: (i, 0)), pl.BlockSpec((None, F, D), lambda i, j: (j, 0, 0))],
              [(0, 1, 'nt', 0)], [(tm, F)], [g], [hspec], ep,
              [jax.ShapeDtypeStruct((N_DEV, L, F), BF16)] * 2, [hspec, hspec], comm=comm)
    return _with_comm(res, comm, lambda o: o)


def _resnorm_epilogue(scale, with_next):
    def ep(acc, h, w, *w_next):
        r = lax.rsqrt(jnp.mean(acc * acc, axis=-1, keepdims=True) + EPS)
        h_out = h + scale * (acc * r * w)
        if not with_next:
            return acc, h_out
        r_next = lax.rsqrt(jnp.mean(h_out * h_out, axis=-1, keepdims=True) + EPS)
        return acc, h_out, h_out * r_next * w_next[0]
    return ep


def _ffn_down(hid, wd, h_in, post, next_norm=None, comm=None):
    _, L, F = hid.shape
    D = wd.shape[2]
    tm = _tile(L, 528)
    row = pl.BlockSpec((tm, D), lambda i, j: (i, 0))
    vec = pl.BlockSpec((1, D), lambda i, j: (0, 0))
    nxt = [] if next_norm is None else [next_norm]
    res = _mm("ffn_down", (L // tm, N_DEV), ("parallel", "arbitrary"), 1,
              [hid, wd], [pl.BlockSpec((None, tm, F), lambda i, j: (j, i, 0)),
                          pl.BlockSpec((None, F, D), lambda i, j: (j, 0, 0))],
              [(0, 1, 'nn', 0)], [(tm, D)], [h_in, post] + nxt, [row, vec] + [vec] * len(nxt),
              _resnorm_epilogue(0.5, bool(nxt)),
              [jax.ShapeDtypeStruct((L, D), F32)] * 2 + [jax.ShapeDtypeStruct((L, D), BF16)] * len(nxt),
              [row] * (2 + len(nxt)), comm=comm)
    return _with_comm(res, comm, lambda o: o)


def _ffn_dhid(df, wd, g, u, comm=None):
    L, D = df.shape
    F = wd.shape[1]
    tm = _tile(L, 704)

    def ep(dhid, g_, u_):
        g32, u32 = g_.astype(F32), u_.astype(F32)
        sg = _sigmoid(g32)
        return dhid * u32 * sg * (1.0 + g32 * (1.0 - sg)), dhid * g32 * sg

    hspec = pl.BlockSpec((None, tm, F), lambda i, j: (j, i, 0))
    res = _mm("ffn_dhid", (L // tm, N_DEV), ("parallel", "parallel"), None,
              [df, wd], [pl.BlockSpec((tm, D), lambda i, j: (i, 0)),
                         pl.BlockSpec((None, F, D), lambda i, j: (j, 0, 0))],
              [(0, 1, 'nt', 0)], [(tm, F)], [g, u], [hspec, hspec], ep,
              [jax.ShapeDtypeStruct((N_DEV, L, F), BF16)] * 2, [hspec, hspec], comm=comm)
    return _with_comm(res, comm, lambda o: o)


def _ffn_dwd(hid, df, comm=None):
    _, L, F = hid.shape
    D = df.shape[1]
    tk = _tile(L, 1408)
    res = _mm("ffn_dwd", (N_DEV, L // tk), ("parallel", "arbitrary"), 1,
              [hid, df], [pl.BlockSpec((None, tk, F), lambda j, k: (j, k, 0)),
                          pl.BlockSpec((tk, D), lambda j, k: (k, 0))],
              [(0, 1, 'tn', 0)], [(F, D)], [], [], lambda acc: (acc,),
              [jax.ShapeDtypeStruct((N_DEV, F, D), BF16)], [pl.BlockSpec((None, F, D), lambda j, k: (j, 0, 0))],
              comm=comm)
    return _with_comm(res, comm, lambda o: o[0])


def _ffn_dwgu(a, dg, du, comm=None):
    L, D = a.shape
    F = dg.shape[2]
    tk = _tile(L, 1408)
    hspec = pl.BlockSpec((None, tk, F), lambda j, k: (j, k, 0))
    wspec = pl.BlockSpec((None, F, D), lambda j, k: (j, 0, 0))
    res = _mm("ffn_dwgu", (N_DEV, L // tk), ("parallel", "arbitrary"), 1,
              [a, dg, du], [pl.BlockSpec((tk, D), lambda j, k: (k, 0)), hspec, hspec],
              [(1, 0, 'tn', 0), (2, 0, 'tn', 1)], [(F, D)] * 2, [], [], lambda *acc: acc,
              [jax.ShapeDtypeStruct((N_DEV, F, D), BF16)] * 2, [wspec, wspec], comm=comm)
    return _with_comm(res, comm, lambda o: o)


def _ffn_da(dg, du, wg, wu, comm=None):
    _, L, F = dg.shape
    D = wg.shape[2]
    tm = _tile(L, 704)
    hspec = pl.BlockSpec((None, tm, F), lambda i, j: (j, i, 0))
    wspec = pl.BlockSpec((None, F, D), lambda i, j: (j, 0, 0))
    row = pl.BlockSpec((tm, D), lambda i, j: (i, 0))
    res = _mm("ffn_da", (L // tm, N_DEV), ("parallel", "arbitrary"), 1,
              [dg, du, wg, wu], [hspec, hspec, wspec, wspec],
              [(0, 2, 'nn', 0), (1, 3, 'nn', 0)], [(tm, D)], [], [], lambda acc: (acc,),
              [jax.ShapeDtypeStruct((L, D), F32)], [row], comm=comm)
    return _with_comm(res, comm, lambda o: o[0])


def _rope_tables(L):
    rows = jnp.arange(L, dtype=F32)
    pos = jnp.where(rows < BLK, rows, rows - (BLK - N_META))
    inv_r = ROPE_THETA ** (-jnp.arange(0, HD, 2, dtype=F32) / HD)
    ang_r = pos[:, None] * inv_r[None, :]
    cr = jnp.concatenate([jnp.cos(ang_r), jnp.cos(ang_r)], axis=1)
    sr = jnp.concatenate([-jnp.sin(ang_r), jnp.sin(ang_r)], axis=1)
    inv_m = ROPE_THETA ** (-jnp.arange(0, ROPE, 2, dtype=F32) / ROPE)
    ang_m = pos[:, None] * inv_m[None, :]
    z32 = jnp.zeros((L, ROPE // 2), F32)
    z64 = jnp.zeros((L, HD - ROPE), F32)
    cm = jnp.concatenate([jnp.cos(ang_m), jnp.cos(ang_m), z64], axis=1)
    sa = jnp.concatenate([-jnp.sin(ang_m), z32, z64], axis=1)
    sb = jnp.concatenate([z32, jnp.sin(ang_m), z64], axis=1)
    return cr, sr, cm, sa, sb


def _rope_ret(x, cr, sr):
    return x * cr + pltpu.roll(x, HD // 2, 1) * sr


def _rope_ret_t(d, cr, sr):
    return d * cr + pltpu.roll(d * sr, HD // 2, 1)


def _rope_mla(x, cm, sa, sb):
    return x * cm + pltpu.roll(x, HD - ROPE // 2, 1) * sa + pltpu.roll(x, ROPE // 2, 1) * sb


def _rope_mla_t(d, cm, sa, sb):
    return d * cm + pltpu.roll(d * sa, ROPE // 2, 1) + pltpu.roll(d * sb, HD - ROPE // 2, 1)


C_RQ, C_RK, C_RV, C_RG = 0, HEADS * HD, 2 * HEADS * HD, 3 * HEADS * HD
C_CQ = 4 * HEADS * HD
C_CKV = C_CQ + Q_RANK
C_KR = C_CKV + KV_RANK
RET_K_SCALE = HD ** -0.5


def _prep(proj, tabs, qn, kvn):
    L = proj.shape[0]
    tr = _tile(L, 256)
    W = HEADS * HD

    def body(p_ref, cr_ref, sr_ref, cm_ref, sa_ref, sb_ref, qn_ref, kvn_ref, q_ref, k_ref, v_ref, cq_ref, ckv_ref,
             kr_ref):
        cr, sr = cr_ref[...], sr_ref[...]
        for h in range(HEADS):
            sl = slice(h * HD, (h + 1) * HD)
            q_ref[:, sl] = _rope_ret(p_ref[:, C_RQ + h * HD:C_RQ + (h + 1) * HD].astype(F32), cr, sr).astype(BF16)
            k_ref[:, sl] = (_rope_ret(p_ref[:, C_RK + h * HD:C_RK + (h + 1) * HD].astype(F32), cr, sr)
                            * RET_K_SCALE).astype(BF16)
        v_ref[...] = p_ref[:, C_RV:C_RV + W].astype(BF16)
        cq = p_ref[:, C_CQ:C_CQ + Q_RANK].astype(F32)
        cq_ref[...] = (cq * lax.rsqrt(jnp.mean(cq * cq, axis=-1, keepdims=True) + EPS) * qn_ref[...]).astype(BF16)
        ckv = p_ref[:, C_CKV:C_CKV + KV_RANK].astype(F32)
        ckv_ref[...] = (ckv * lax.rsqrt(jnp.mean(ckv * ckv, axis=-1, keepdims=True) + EPS)
                        * kvn_ref[...]).astype(BF16)
        kr_ref[...] = _rope_mla(p_ref[:, C_KR:C_KR + HD].astype(F32), cm_ref[...], sa_ref[...], sb_ref[...]).astype(BF16)

    row = lambda w: pl.BlockSpec((tr, w), lambda i: (i, 0))
    vec = lambda w: pl.BlockSpec((1, w), lambda i: (0, 0))
    return pl.pallas_call(
        body, name="mix_prep", grid=(L // tr,),
        out_shape=[jax.ShapeDtypeStruct((L, W), BF16)] * 3 + [jax.ShapeDtypeStruct((L, Q_RANK), BF16),
                                                              jax.ShapeDtypeStruct((L, KV_RANK), BF16),
                                                              jax.ShapeDtypeStruct((L, HD), BF16)],
        in_specs=[row(D_INP)] + [row(HD)] * 5 + [vec(Q_RANK), vec(KV_RANK)],
        out_specs=[row(W)] * 3 + [row(Q_RANK), row(KV_RANK), row(HD)],
        compiler_params=_params(("parallel",)),
    )(proj, *tabs, qn, kvn)


def _prep_bwd(proj, dq, dk, dv, drg, dcqn, dckvn, dkr8, tabs, qn, kvn):
    L = proj.shape[0]
    tr = _tile(L, 192)
    W = HEADS * HD

    def body(p_ref, dq_ref, dk_ref, dv_ref, drg_ref, dcq_ref, dckv_ref, dkr_ref, cr_ref, sr_ref, cm_ref, sa_ref,
             sb_ref, qn_ref, kvn_ref, dp_ref, dqn_ref, dkvn_ref):
        cr, sr = cr_ref[...], sr_ref[...]
        dkr = None
        for h in range(HEADS):
            sl = slice(h * HD, (h + 1) * HD)
            dp_ref[:, C_RQ + h * HD:C_RQ + (h + 1) * HD] = _rope_ret_t(dq_ref[:, sl].astype(F32), cr, sr).astype(BF16)
            dp_ref[:, C_RK + h * HD:C_RK + (h + 1) * HD] = (_rope_ret_t(dk_ref[:, sl].astype(F32), cr, sr)
                                                            * RET_K_SCALE).astype(BF16)
            part = dkr_ref[:, sl].astype(F32)
            dkr = part if dkr is None else dkr + part
        dp_ref[:, C_RV:C_RV + W] = dv_ref[...].astype(BF16)
        dp_ref[:, C_RG:C_RG + W] = drg_ref[...].astype(BF16)
        dcq, dqn = _norm_bwd_math(p_ref[:, C_CQ:C_CQ + Q_RANK].astype(F32), qn_ref[...], dcq_ref[...])
        dp_ref[:, C_CQ:C_CQ + Q_RANK] = dcq.astype(BF16)
        dckv, dkvn = _norm_bwd_math(p_ref[:, C_CKV:C_CKV + KV_RANK].astype(F32), kvn_ref[...], dckv_ref[...])
        dp_ref[:, C_CKV:C_CKV + KV_RANK] = dckv.astype(BF16)
        dp_ref[:, C_KR:C_KR + HD] = _rope_mla_t(dkr, cm_ref[...], sa_ref[...], sb_ref[...]).astype(BF16)

        @pl.when(pl.program_id(0) == 0)
        def _():
            dqn_ref[...] = jnp.zeros_like(dqn_ref)
            dkvn_ref[...] = jnp.zeros_like(dkvn_ref)

        dqn_ref[...] += dqn
        dkvn_ref[...] += dkvn

    row = lambda w: pl.BlockSpec((tr, w), lambda i: (i, 0))
    vec = lambda w: pl.BlockSpec((1, w), lambda i: (0, 0))
    return pl.pallas_call(
        body, name="mix_prep_bwd", grid=(L // tr,),
        out_shape=[jax.ShapeDtypeStruct((L, D_INP), BF16), jax.ShapeDtypeStruct((1, Q_RANK), F32),
                   jax.ShapeDtypeStruct((1, KV_RANK), F32)],
        in_specs=[row(D_INP)] + [row(W)] * 4 + [row(Q_RANK), row(KV_RANK), row(W)] + [row(HD)] * 5
                 + [vec(Q_RANK), vec(KV_RANK)],
        out_specs=[row(D_INP), vec(Q_RANK), vec(KV_RANK)],
        compiler_params=_params(("arbitrary",)),
    )(proj, dq, dk, dv, drg, dcqn, dckvn, dkr8, *tabs, qn, kvn)


def _post(o_ret, proj, gn):
    L, W = o_ret.shape
    tr = _tile(L, 384)

    def body(o_ref, rg_ref, gn_ref, out_ref):
        for h in range(HEADS):
            sl = slice(h * HD, (h + 1) * HD)
            o = o_ref[:, sl]
            rg = rg_ref[:, sl].astype(F32)
            n = o * lax.rsqrt(jnp.mean(o * o, axis=-1, keepdims=True) + EPS)
            out_ref[:, sl] = (n * gn_ref[:, sl] * (rg * _sigmoid(rg))).astype(BF16)

    row = pl.BlockSpec((tr, W), lambda i: (i, 0))
    return pl.pallas_call(
        body, name="ret_post", grid=(L // tr,), out_shape=jax.ShapeDtypeStruct((L, W), BF16),
        in_specs=[row, pl.BlockSpec((tr, W), lambda i: (i, C_RG // W)), pl.BlockSpec((1, W), lambda i: (0, 0))],
        out_specs=row, compiler_params=_params(("parallel",)),
    )(o_ret, proj, gn)


def _post_bwd(o_ret, proj, gn, dcat):
    L, W = o_ret.shape
    tr = _tile(L, 384)

    def body(o_ref, rg_ref, gn_ref, d_ref, do_ref, drg_ref, dgn_ref):
        @pl.when(pl.program_id(0) == 0)
        def _():
            dgn_ref[...] = jnp.zeros_like(dgn_ref)

        for h in range(HEADS):
            sl = slice(h * HD, (h + 1) * HD)
            o = o_ref[:, sl]
            rg = rg_ref[:, sl].astype(F32)
            d = d_ref[:, sl].astype(F32)
            gw = gn_ref[:, sl]
            r = lax.rsqrt(jnp.mean(o * o, axis=-1, keepdims=True) + EPS)
            n = o * r
            sg = _sigmoid(rg)
            si = rg * sg
            dn = d * gw * si
            dgn_ref[:, sl] += jnp.sum(d * n * si, axis=0, keepdims=True)
            drg_ref[:, sl] = (d * n * gw * sg * (1.0 + rg * (1.0 - sg))).astype(drg_ref.dtype)
            do_ref[:, sl] = (r * (dn - o * (r * r) * jnp.mean(dn * o, axis=-1, keepdims=True))).astype(BF16)

    row = pl.BlockSpec((tr, W), lambda i: (i, 0))
    vec = pl.BlockSpec((1, W), lambda i: (0, 0))
    return pl.pallas_call(
        body, name="ret_post_bwd", grid=(L // tr,),
        out_shape=[jax.ShapeDtypeStruct((L, W), BF16), jax.ShapeDtypeStruct((L, W), BF16),
                   jax.ShapeDtypeStruct((1, W), F32)],
        in_specs=[row, pl.BlockSpec((tr, W), lambda i: (i, C_RG // W)), vec, row],
        out_specs=[row, row, vec], compiler_params=_params(("arbitrary",)),
    )(o_ret, proj, gn, dcat)


RET_HEADS_PER_STEP = 4
RET_CHUNK = 256


def _ret_scans(name, arrays, scans, lg, out_dtype, heads_per_step):
    L, W = arrays[0].shape
    G, C = heads_per_step, RET_CHUNK
    assert (L - BLK) % C == 0
    nc = (L - BLK) // C
    na, ns = len(arrays), len(scans)

    def body(*refs):
        in_refs, lg_ref, o_refs, s_ref = refs[:na], refs[na], refs[na + 1:na + 1 + ns], refs[-1]
        n = lax.broadcasted_iota(jnp.int32, (C, C), 0).astype(F32)
        m = lax.broadcasted_iota(jnp.int32, (C, C), 1).astype(F32)
        r = lax.broadcasted_iota(jnp.int32, (C, HD), 0).astype(F32)
        n0 = lax.broadcasted_iota(jnp.int32, (BLK, BLK), 0).astype(F32)
        m0 = lax.broadcasted_iota(jnp.int32, (BLK, BLK), 1).astype(F32)
        r0 = lax.broadcasted_iota(jnp.int32, (BLK, HD), 0).astype(F32)
        meta = (n0 < N_META) & (m0 < N_META)
        ways = {way for _, _, _, way in scans}
        consts = []
        for g in range(G):
            lgv = lg_ref[g, 0:1, 0:1]
            c = dict(gl=jnp.exp(lgv * float(C)))
            if 'f' in ways:
                c['f'] = dict(
                    dmask=jnp.where(n >= m, jnp.exp(lgv * jnp.maximum(n - m, 0.0)), 0.0),
                    dmask0=jnp.where(meta & (n0 >= m0), jnp.exp(lgv * jnp.maximum(n0 - m0, 0.0)), 0.0),
                    inter=jnp.exp(lgv * (r + 1.0)), upd=jnp.exp(lgv * (float(C) - 1.0 - r)),
                    upd0=jnp.where(r0 < N_META, jnp.exp(lgv * jnp.maximum(float(N_META) - 1.0 - r0, 0.0)), 0.0))
            if 'r' in ways:
                c['r'] = dict(
                    dmask=jnp.where(m >= n, jnp.exp(lgv * jnp.maximum(m - n, 0.0)), 0.0),
                    dmask0=jnp.where(meta & (m0 >= n0), jnp.exp(lgv * jnp.maximum(m0 - n0, 0.0)), 0.0),
                    inter=jnp.exp(lgv * (float(C) - r)), upd=jnp.exp(lgv * r),
                    inter0=jnp.where(r0 < N_META, jnp.exp(lgv * jnp.maximum(float(N_META) - r0, 0.0)), 0.0))
            consts.append(c)

        def chunk(chunk_of):
            results = []
            for g in range(G):
                cols = slice(g * HD, (g + 1) * HD)
                for s, (qi, ki, vi, way) in enumerate(scans):
                    rows = pl.ds(pl.multiple_of(BLK + chunk_of[way] * C, BLK), C)
                    cg, state = consts[g][way], s_ref[s, g]
                    qc, kc, vc = in_refs[qi][rows, cols], in_refs[ki][rows, cols], in_refs[vi][rows, cols]
                    a = _dot(qc, kc, 'nt') * cg['dmask']
                    out = _dot(a.astype(BF16), vc, 'nn') + _dot(qc, state.astype(BF16), 'nn') * cg['inter']
                    new = state * consts[g]['gl'] + _dot((kc.astype(F32) * cg['upd']).astype(BF16), vc, 'tn')
                    results.append((s, g, rows, cols, out, new))
            for s, g, rows, cols, out, new in results:
                o_refs[s][rows, cols] = out.astype(out_dtype)
                s_ref[s, g] = new

        def first_chunk(s):
            qi, ki, vi, way = scans[s]
            for g in range(G):
                cols = slice(g * HD, (g + 1) * HD)
                cg = consts[g][way]
                q0, k0, v0 = in_refs[qi][0:BLK, cols], in_refs[ki][0:BLK, cols], in_refs[vi][0:BLK, cols]
                o0 = _dot((_dot(q0, k0, 'nt') * cg['dmask0']).astype(BF16), v0, 'nn')
                if way == 'r':
                    o0 = o0 + _dot(q0, s_ref[s, g].astype(BF16), 'nn') * cg['inter0']
                else:
                    s_ref[s, g] = _dot((k0.astype(F32) * cg['upd0']).astype(BF16), v0, 'tn')
                o_refs[s][0:BLK, cols] = o0.astype(out_dtype)

        s_ref[...] = jnp.zeros_like(s_ref)
        for s in range(ns):
            if scans[s][3] == 'f':
                first_chunk(s)

        def step(t, carry):
            chunk({'f': t, 'r': nc - 1 - t})
            return carry

        lax.fori_loop(0, nc, step, 0)
        for s in range(ns):
            if scans[s][3] == 'r':
                first_chunk(s)

    col = pl.BlockSpec((L, G * HD), lambda h: (0, h))
    return pl.pallas_call(
        body, name=name, grid=(HEADS // G,), out_shape=[jax.ShapeDtypeStruct((L, W), out_dtype)] * ns,
        in_specs=[col] * na + [pl.BlockSpec((G, 8, HD), lambda h: (h, 0, 0))], out_specs=[col] * ns,
        scratch_shapes=[pltpu.VMEM((ns, G, HD, HD), F32)], compiler_params=_params(("parallel",)),
    )(*arrays, lg)


ATT_SCALE = (HD + ROPE) ** -0.5
LOG2E = 1.4426950408889634
Q_PRESCALE = ATT_SCALE * LOG2E
NEG = -1e30


ATT_TILE = 384
ATT_HEADS_PER_STEP = 8
ATT_BWD_HEADS_PER_STEP = 4


def _att_valid(nq, nk, row0, col0):
    r = lax.broadcasted_iota(jnp.int32, (nq, nk), 0) + row0
    c = lax.broadcasted_iota(jnp.int32, (nq, nk), 1) + col0
    return (c <= r) & ((c < N_META) | (c >= BLK))


def _store_rows(ref, g, first, col):
    wide = jnp.broadcast_to(col, (col.shape[0], HD))
    for c in range(col.shape[0] // BLK):
        ref[g, first + c] = jnp.transpose(wide[c * BLK:(c + 1) * BLK, :])[0:8, :]


def _load_row(ref, g, first, n):
    return jnp.concatenate([ref[g, first + c, 0:1, :] for c in range(n)], axis=1)


def _attn_fwd(qm, kn, krr, vm, comm=None):
    L = qm.shape[0]
    W = HEADS * HD
    T = _tile(L, ATT_TILE, BLK)
    nb = L // T
    G = ATT_HEADS_PER_STEP
    n_cm = comm.n if comm is not None else 0

    def body(*refs):
        q_ref, kn_ref, kr_ref, v_ref = refs[:4]
        o_ref, lse_ref = refs[4 + n_cm:6 + n_cm]
        m_sc, l_sc, acc_sc = refs[6 + 2 * n_cm:9 + 2 * n_cm]
        if comm is not None:
            cm_refs = (refs[4:4 + n_cm], refs[6 + n_cm:6 + 2 * n_cm], refs[9 + 2 * n_cm:])
            first, last = _grid_edges((HEADS // G, nb))

            @pl.when(first)
            def _():
                comm.start(*cm_refs)

        i = pl.program_id(1)
        m_sc[...] = jnp.full_like(m_sc, NEG)
        l_sc[...] = jnp.zeros_like(l_sc)
        acc_sc[...] = jnp.zeros_like(acc_sc)

        def tile(j, masked):
            rows = pl.ds(pl.multiple_of(j * T, T), T)
            kr = kr_ref[rows, :]
            valid = _att_valid(T, T, i * T, j * T) if masked else None
            ones = jnp.ones((T, HD), BF16)
            m_prev = [m_sc[g] for g in range(G)]
            l_prev = [l_sc[g] for g in range(G)]
            acc_prev = [acc_sc[g] for g in range(G)]
            m_new, l_new, acc_new = [], [], []
            for g in range(G):
                k = jnp.concatenate([kn_ref[rows, g * HD:(g + 1) * HD], kr], axis=1)
                s = _dot(q_ref[:, g * QH:(g + 1) * QH], k, 'nt')
                if masked:
                    s = jnp.where(valid, s, NEG)
                m_new.append(jnp.maximum(m_prev[g], jnp.max(s, axis=-1, keepdims=True)))
                p = jnp.exp2(s - m_new[g])
                alpha = jnp.exp2(m_prev[g] - m_new[g])
                pv = _dot(p.astype(BF16), jnp.concatenate([v_ref[rows, g * HD:(g + 1) * HD], ones], axis=1), 'nn')
                l_new.append(alpha * l_prev[g] + pv[:, HD:HD + 1])
                acc_new.append(alpha * acc_prev[g] + pv[:, 0:HD])
            for g in range(G):
                m_sc[g] = m_new[g]
                l_sc[g] = l_new[g]
                acc_sc[g] = acc_new[g]

        tile(0, True)

        def mid(j, carry):
            tile(j, False)
            return carry

        lax.fori_loop(1, i, mid, 0)

        @pl.when(i > 0)
        def _():
            tile(i, True)

        for g in range(G):
            l = l_sc[g]
            o_ref[:, g * HD:(g + 1) * HD] = (acc_sc[g] / l).astype(o_ref.dtype)
            _store_rows(lse_ref, g, 0, m_sc[g] + jnp.log(l) * LOG2E)

        if comm is not None:
            @pl.when(last)
            def _():
                comm.finish(*cm_refs)

    cm_specs = comm.specs if comm is not None else []
    res = pl.pallas_call(
        body, name="attn_fwd", grid=(HEADS // G, nb),
        out_shape=[jax.ShapeDtypeStruct((L, W), BF16), jax.ShapeDtypeStruct((HEADS, L // BLK, 8, HD), F32)]
        + (comm.out_shapes if comm is not None else []),
        in_specs=[pl.BlockSpec((T, G * QH), lambda h, i: (i, h)), pl.BlockSpec((L, G * HD), lambda h, i: (0, h)),
                  pl.BlockSpec((L, HD), lambda h, i: (0, 0)), pl.BlockSpec((L, G * HD), lambda h, i: (0, h))]
        + cm_specs,
        out_specs=[pl.BlockSpec((T, G * HD), lambda h, i: (i, h)),
                   pl.BlockSpec((G, T // BLK, 8, HD), lambda h, i: (h, i, 0, 0))] + cm_specs,
        scratch_shapes=[pltpu.VMEM((G, T, 1), F32), pltpu.VMEM((G, T, 1), F32), pltpu.VMEM((G, T, HD), F32)]
        + (comm.scratch if comm is not None else []),
        compiler_params=_params(("arbitrary", "arbitrary")),
    )(qm, kn, krr, vm, *(comm.arrays if comm is not None else []))
    return res[:2], res[2:]


def _attn_bwd(qm, kn, krr, vm, o, dcat, lse, comm=None):
    L = qm.shape[0]
    W = HEADS * HD
    T = _tile(L, ATT_TILE, BLK)
    nb, nr = L // T, T // BLK
    G = ATT_BWD_HEADS_PER_STEP
    n_cm = comm.n if comm is not None else 0

    def body(*refs):
        q_ref, kn_ref, kr_ref, v_ref, o_ref, do_ref, lse_ref = refs[:7]
        dq_ref, dkn_ref, dkr_ref, dv_ref = refs[7 + n_cm:11 + n_cm]
        dl_sc, dk_sc, dv_sc = refs[11 + 2 * n_cm:14 + 2 * n_cm]
        if comm is not None:
            cm_refs = (refs[7:7 + n_cm], refs[11 + n_cm:11 + 2 * n_cm], refs[14 + 2 * n_cm:])
            first, last = _grid_edges((HEADS // G, nb))

            @pl.when(first)
            def _():
                comm.start(*cm_refs)

        j = pl.program_id(1)
        qs = lambda g: slice(g * QH, (g + 1) * QH)
        hs = lambda g: slice(g * HD, (g + 1) * HD)

        @pl.when(j == 0)
        def _():
            dq_ref[...] = jnp.zeros_like(dq_ref)

            def rowsum(t, carry):
                rows = pl.ds(pl.multiple_of(t * T, T), T)
                for g in range(G):
                    _store_rows(dl_sc, g, t * nr, jnp.sum(
                        do_ref[rows, hs(g)].astype(F32) * o_ref[rows, hs(g)].astype(F32), axis=-1, keepdims=True))
                return carry

            lax.fori_loop(0, nb, rowsum, 0)

        kr = kr_ref[...]
        ks = [jnp.concatenate([kn_ref[:, hs(g)], kr], axis=1) for g in range(G)]
        vs = [v_ref[:, hs(g)] for g in range(G)]
        dk_sc[...] = jnp.zeros_like(dk_sc)
        dv_sc[...] = jnp.zeros_like(dv_sc)

        def tile(i, masked):
            rows = pl.ds(pl.multiple_of(i * T, T), T)
            if masked:
                key = lax.broadcasted_iota(jnp.int32, (T, T), 0) + j * T
                qry = lax.broadcasted_iota(jnp.int32, (T, T), 1) + i * T
                valid = (key <= qry) & ((key < N_META) | (key >= BLK))
            for g in range(G):
                q = q_ref[rows, qs(g)]
                do = do_ref[rows, hs(g)]
                s = _dot(ks[g], q, 'nt')
                if masked:
                    s = jnp.where(valid, s, NEG)
                p = jnp.exp2(s - _load_row(lse_ref, g, i * nr, nr))
                dv_sc[g] += _dot(p.astype(BF16), do, 'nn')
                ds = (p * (_dot(vs[g], do, 'nt') - _load_row(dl_sc, g, i * nr, nr))).astype(BF16)
                dk_sc[g] += _dot(ds, q, 'nn')
                dq_ref[rows, qs(g)] += _dot(ds, ks[g], 'tn')

        tile(j, True)

        def rest(masked):
            def step(i, carry):
                tile(i, masked)
                return carry
            lax.fori_loop(j + 1, nb, step, 0)

        @pl.when(j == 0)
        def _():
            rest(True)

        @pl.when(j > 0)
        def _():
            rest(False)

        for g in range(G):
            dk = dk_sc[g] * (1.0 / LOG2E)
            dkn_ref[:, hs(g)] = dk[:, 0:HD].astype(BF16)
            dkr_ref[:, hs(g)] = dk[:, HD:QH].astype(dkr_ref.dtype)
            dv_ref[:, hs(g)] = dv_sc[g].astype(BF16)

        if comm is not None:
            @pl.when(last)
            def _():
                comm.finish(*cm_refs)

    blk = pl.BlockSpec((T, G * HD), lambda h, j: (j, h))
    once = pl.Buffered(1)
    cm_specs = comm.specs if comm is not None else []
    res = pl.pallas_call(
        body, name="attn_bwd", grid=(HEADS // G, nb),
        out_shape=[jax.ShapeDtypeStruct((L, HEADS * QH), F32), jax.ShapeDtypeStruct((L, W), BF16),
                   jax.ShapeDtypeStruct((L, W), BF16), jax.ShapeDtypeStruct((L, W), BF16)]
        + (comm.out_shapes if comm is not None else []),
        in_specs=[pl.BlockSpec((L, G * QH), lambda h, j: (0, h), pipeline_mode=once), blk,
                  pl.BlockSpec((T, HD), lambda h, j: (j, 0)), blk,
                  pl.BlockSpec((L, G * HD), lambda h, j: (0, h), pipeline_mode=once),
                  pl.BlockSpec((L, G * HD), lambda h, j: (0, HEADS // G + h), pipeline_mode=once),
                  pl.BlockSpec((G, L // BLK, 8, HD), lambda h, j: (h, 0, 0, 0))] + cm_specs,
        out_specs=[pl.BlockSpec((L, G * QH), lambda h, j: (0, h), pipeline_mode=once), blk, blk, blk] + cm_specs,
        scratch_shapes=[pltpu.VMEM((G, L // BLK, 8, HD), F32), pltpu.VMEM((G, T, QH), F32),
                        pltpu.VMEM((G, T, HD), F32)]
        + (comm.scratch if comm is not None else []),
        compiler_params=_params(("arbitrary", "arbitrary")),
    )(qm, kn, krr, vm, o, dcat, lse, *(comm.arrays if comm is not None else []))
    return res[:4], res[4:]


def _unrope_q(dqm, tabs_m):
    L, W = dqm.shape
    tr = _tile(L, 384)

    def body(d_ref, cm_ref, sa_ref, sb_ref, out_ref):
        cm, sa, sb = cm_ref[...], sa_ref[...], sb_ref[...]
        for h in range(HEADS):
            out_ref[:, h * QH:h * QH + HD] = (d_ref[:, h * QH:h * QH + HD] * ATT_SCALE).astype(BF16)
            out_ref[:, h * QH + HD:(h + 1) * QH] = _rope_mla_t(d_ref[:, h * QH + HD:(h + 1) * QH] * ATT_SCALE, cm, sa,
                                                               sb).astype(BF16)

    row = pl.BlockSpec((tr, W), lambda i: (i, 0))
    tab = pl.BlockSpec((tr, HD), lambda i: (i, 0))
    return pl.pallas_call(
        body, name="unrope_q", grid=(L // tr,), out_shape=jax.ShapeDtypeStruct((L, W), BF16),
        in_specs=[row, tab, tab, tab], out_specs=row, compiler_params=_params(("parallel",)),
    )(dqm, *tabs_m)


def _q_up(cqn, wuq_p, tabs_m):
    L = cqn.shape[0]
    tm = _tile(L, 704)

    def ep(acc, cm, sa, sb):
        acc = acc * Q_PRESCALE
        parts = []
        for h in range(HEADS):
            parts.append(acc[:, h * QH:h * QH + HD])
            parts.append(_rope_mla(acc[:, h * QH + HD:(h + 1) * QH], cm, sa, sb))
        return (jnp.concatenate(parts, axis=1),)

    tab = pl.BlockSpec((tm, HD), lambda i, j: (i, 0))
    return _mm("mla_q_up", (L // tm, 1), ("parallel", "parallel"), None,
               [cqn, wuq_p], [pl.BlockSpec((tm, Q_RANK), lambda i, j: (i, 0)),
                              pl.BlockSpec((HEADS * QH, Q_RANK), lambda i, j: (0, 0))],
               [(0, 1, 'nt', 0)], [(tm, HEADS * QH)], list(tabs_m), [tab] * 3, ep,
               [jax.ShapeDtypeStruct((L, HEADS * QH), BF16)], [pl.BlockSpec((tm, HEADS * QH), lambda i, j: (i, 0))])[0]


def _mix_out(cat, w_out, h_in, post, next_norm):
    L, K = cat.shape
    D = w_out.shape[1]
    tm, tk = _tile(L, 384), K
    row = pl.BlockSpec((tm, D), lambda i, k: (i, 0))
    vec = pl.BlockSpec((1, D), lambda i, k: (0, 0))
    return _mm("mix_out", (L // tm, K // tk), ("parallel", "arbitrary"), 1,
               [cat, w_out], [pl.BlockSpec((tm, tk), lambda i, k: (i, k)), pl.BlockSpec((tk, D), lambda i, k: (k, 0))],
               [(0, 1, 'nn', 0)], [(tm, D)], [h_in, post, next_norm], [row, vec, vec], _resnorm_epilogue(1.0, True),
               [jax.ShapeDtypeStruct((L, D), F32)] * 2 + [jax.ShapeDtypeStruct((L, D), BF16)], [row, row, row])


ADAM_BLOCK_ELEMS = 512 * 704


def _adam_math(w, g, m, v):
    m = ADAM_B1 * m + (1.0 - ADAM_B1) * g
    v = ADAM_B2 * v + (1.0 - ADAM_B2) * (g * g)
    m_hat = m / (1.0 - ADAM_B1 ** ADAM_STEP)
    v_hat = v / (1.0 - ADAM_B2 ** ADAM_STEP)
    delta = -ADAM_LR * (m_hat / (jnp.sqrt(v_hat) + ADAM_EPS) + ADAM_WD * w)
    return delta, m, v


def _adam(name, w, m, v, g_slots=None, g=None, after=None):
    R, C = w.shape
    tr, tc = _tile(R, max(16, ADAM_BLOCK_ELEMS // C // 16 * 16), 16), C
    if tr * tc > ADAM_BLOCK_ELEMS:
        tr, tc = R, _tile(C, max(128, ADAM_BLOCK_ELEMS // R // 128 * 128), 128)
    from_slots = g_slots is not None

    def body(w_ref, m_ref, v_ref, g_ref, *rest):
        go_ref, d_ref, mo_ref, vo_ref = rest[-4:]
        if from_slots:
            grad = g_ref[0].astype(F32)
            for s in range(1, N_DEV):
                grad = grad + g_ref[s].astype(F32)
        else:
            grad = g_ref[...]
        delta, mn, vn = _adam_math(w_ref[...], grad, m_ref[...], v_ref[...])
        go_ref[...] = grad
        d_ref[...] = delta
        mo_ref[...] = mn
        vo_ref[...] = vn

    row = pl.BlockSpec((tr, tc), lambda i, j: (i, j))
    gspec = pl.BlockSpec((N_DEV, tr, tc), lambda i, j: (0, i, j)) if from_slots else row
    order = [] if after is None else [after]
    return pl.pallas_call(
        body, name=name, grid=(R // tr, C // tc), out_shape=[jax.ShapeDtypeStruct((R, C), F32)] * 4,
        in_specs=[row, row, row, gspec] + [pl.BlockSpec(memory_space=pl.ANY)] * len(order), out_specs=[row] * 4,
        compiler_params=_params(("parallel", "parallel")),
    )(w, m, v, g_slots if from_slots else g, *order)


def _unblock(gathered):
    n, r, c = gathered.shape
    return jnp.transpose(gathered, (1, 0, 2)).reshape(r, n * c)


def _reblock(full, c):
    r = full.shape[0]
    return jnp.transpose(full[:, :N_DEV * c].reshape(r, N_DEV, c), (1, 0, 2))


def _step(x, target, w, mom, vel):
    S, D = x.shape[1], x.shape[2]
    L = S + BLK
    def sq(a, n):
        if a.ndim == 2:
            return a
        if n in TRANSPOSED:
            a = jnp.swapaxes(a, 1, 2)
        return a.reshape(a.shape[1:])

    def unsq(o, n):
        o = o.reshape((1,) + o.shape)
        return jnp.swapaxes(o, 1, 2) if n in TRANSPOSED else o

    p = {n: sq(w[n], n) for n in WEIGHTS if n != 'meta_tokens'}
    gather = lambda names: _Exchange([p[n].astype(BF16) for n in names], False)
    scatter = lambda blocks: _Exchange(blocks, True)
    in_s, uq_s = p['w_in'].shape[0], p['mla_w_uq'].shape[0]
    assert uq_s == HD + ROPE and N_DEV == HEADS, "a w_uq shard is one head's columns"
    tabs = _rope_tables(L)
    tabs_m = tabs[2:]
    lg = jnp.broadcast_to(jnp.log(1.0 - 2.0 ** (-5.0 - jnp.arange(HEADS, dtype=F32)))[:, None, None], (HEADS, 8, HD))
    R = {}

    wg1, meta = _exchange("gather_first", [p['ffn1_w_gate'].astype(BF16), w['meta_tokens']], False)
    h0 = jnp.concatenate([_unblock(meta), jnp.zeros((BLK - N_META, D), F32), x[0]], axis=0)
    a1 = _norm_fwd(h0, p['ffn1_pre_norm'])
    g1, (wu1,) = _ffn_gate(a1, wg1, comm=gather(['ffn1_w_up']))
    (u1, hid1), (wd1,) = _ffn_up_gated(a1, wu1, g1, comm=gather(['ffn1_w_down']))
    (f1, h1, um), (w_in_g,) = _ffn_down(hid1, wd1, h0, p['ffn1_post_norm'], next_norm=p['mix_pre_norm'],
                                        comm=gather(['w_in']))

    w_in = jnp.pad(w_in_g.reshape(N_DEV * in_s, D), ((0, D_INP - N_DEV * in_s), (0, 0)))
    proj, (uq_g, uk_g, uv_g, wout_g) = _mm_nt("mix_in", [(um, w_in)], BF16, tn_target=1664,
                                              comm=gather(['mla_w_uq', 'mla_w_uk', 'mla_w_uv', 'w_out']))
    wuq = jnp.pad(uq_g, ((0, 0), (0, QH - uq_s), (0, 0))).reshape(HEADS * QH, Q_RANK)
    wuk, wuv, w_out = _unblock(uk_g), _unblock(uv_g), wout_g.reshape(-1, D)
    qr, kr, vr, cqn, ckvn, krr = _prep(proj, tabs, p['mla_q_norm'], p['mla_kv_norm'])
    qm = _q_up(cqn, wuq, tabs_m)
    kn = _mm_nn("mla_k_up", ckvn, wuk, BF16)
    vm = _mm_nn("mla_v_up", ckvn, wuv, BF16)
    (o_mla, lse), (wg2, wu2) = _attn_fwd(qm, kn, krr, vm, comm=gather(['ffn2_w_gate', 'ffn2_w_up']))
    o_ret, = _ret_scans("ret_fwd", [qr, kr, vr], [(0, 1, 2, 'f')], lg, F32, RET_HEADS_PER_STEP)
    ret = _post(o_ret, proj, p['ret_group_norm'])
    cat = jnp.concatenate([ret, o_mla], axis=1)
    m, h2, a2 = _mix_out(cat, w_out, h1, p['mix_post_norm'], p['ffn2_pre_norm'])

    (g2, u2, hid2), (wd2,) = _ffn_up(a2, wg2, wu2, comm=gather(['ffn2_w_down']))
    f2, h3 = _ffn_down(hid2, wd2, h2, p['ffn2_post_norm'])
    dh3, loss_blk = _loss(h3, target[0])

    dsmall = {}
    df2, dsmall['ffn2_post_norm'] = _norm_bwd(f2, p['ffn2_post_norm'], dh3, None, 0.5, BF16)
    dg2, du2 = _ffn_dhid(df2, wd2, g2, u2)
    dwd2 = _ffn_dwd(hid2, df2)
    (dwg2, dwu2), (R['ffn2_w_down'],) = _ffn_dwgu(a2, dg2, du2, comm=scatter([dwd2]))
    da2, (R['ffn2_w_gate'],) = _ffn_da(dg2, du2, wg2, wu2, comm=scatter([dwg2]))
    dh2, dsmall['ffn2_pre_norm'] = _norm_bwd(h2, p['ffn2_pre_norm'], da2, dh3, 1.0, F32)

    dm, dsmall['mix_post_norm'] = _norm_bwd(m, p['mix_post_norm'], dh2, None, 1.0, BF16)
    dcat = _mm_nt("mix_dcat", [(dm, w_out)], BF16)
    dwout = _mm_tn("mix_dwout", cat, [dm])[0]
    do_ret, drg, dsmall['ret_group_norm'] = _post_bwd(o_ret, proj, p['ret_group_norm'], dcat)
    dqr, dkr, dvr = _ret_scans("ret_bwd", [do_ret, qr, kr, vr], [(0, 3, 2, 'f'), (3, 0, 1, 'r'), (2, 1, 0, 'r')],
                               lg, BF16, RET_HEADS_PER_STEP // 2)
    (dqm, dkn, dkr8, dvm), (R['ffn2_w_up'], R['w_out']) = _attn_bwd(
        qm, kn, krr, vm, o_mla, dcat, lse, comm=scatter([dwu2, dwout.reshape(N_DEV, -1, D)]))
    dqp = _unrope_q(dqm, tabs_m)
    dwuq = _mm_tn("mla_dwuq", dqp, [cqn])[0]
    dcqn = _mm_nn("mla_dcq", dqp, wuq, F32)
    dwuk, dwuv = _mm_tn("mla_dwukv", ckvn, [dkn, dvm])
    dckvn = _mm_nt("mla_dckv", [(dkn, wuk), (dvm, wuv)], F32)
    dproj, dsmall['mla_q_norm'], dsmall['mla_kv_norm'] = _prep_bwd(
        proj, dqr, dkr, dvr, drg, dcqn, dckvn, dkr8, tabs, p['mla_q_norm'], p['mla_kv_norm'])
    dwuq_b = dwuq.reshape(HEADS, QH, Q_RANK)[:, :uq_s]
    (dwin,), (R['mla_w_uq'], R['mla_w_uk'], R['mla_w_uv']) = _mm_tn(
        "mix_dwin", dproj, [um], comm=scatter([dwuq_b, _reblock(dwuk, p['mla_w_uk'].shape[1]),
                                               _reblock(dwuv, p['mla_w_uv'].shape[1])]))
    dwin_b = dwin[:N_DEV * in_s].reshape(N_DEV, in_s, D)
    half = D // 2
    dum, (r_win_a,) = _mm_nn("mix_du", dproj, w_in, F32, tn_target=512, comm=scatter([dwin_b[:, :, :half]]))
    dh1, dsmall['mix_pre_norm'] = _norm_bwd(h1, p['mix_pre_norm'], dum, dh2, 1.0, F32)

    df1, dsmall['ffn1_post_norm'] = _norm_bwd(f1, p['ffn1_post_norm'], dh1, None, 0.5, BF16)
    (dg1, du1), (r_win_b,) = _ffn_dhid(df1, wd1, g1, u1, comm=scatter([dwin_b[:, :, half:]]))
    R['w_in'] = jnp.concatenate([r_win_a, r_win_b], axis=2)
    dwd1 = _ffn_dwd(hid1, df1)
    (dwg1, dwu1), (R['ffn1_w_down'],) = _ffn_dwgu(a1, dg1, du1, comm=scatter([dwd1]))
    da1, (R['ffn1_w_gate'],) = _ffn_da(dg1, du1, wg1, wu1, comm=scatter([dwg1]))
    dh0, dsmall['ffn1_pre_norm'] = _norm_bwd(h0, p['ffn1_pre_norm'], da1, dh1, 1.0, F32)
    tail_sems_s, tail_sems_r, tail_src, tail_land, token = _scatter_start(dwu1)

    def slab(a):
        a = a.reshape(-1, 128)
        return jnp.pad(a, ((0, (-a.shape[0]) % 8), (0, 0)))

    slab_rows = lambda n: -(-(p[n].shape[-1] // 128) // 8) * 8
    packed = jnp.concatenate([slab(dsmall[n]) for n in SMALL] + [slab(dh0[:N_META]), loss_blk], axis=0)
    red = _allreduce_small(packed + token[0, 0])
    offs = sum(slab_rows(n) for n in SMALL)
    n_small = offs
    gmeta_full = red[offs:offs + N_META * D // 128].reshape(N_META, D)
    offs += N_META * D // 128
    loss = red[offs, 0]

    grad, delta, new_m, new_v = {}, {}, {}, {}
    meanwhile = []
    for n in BIG:
        if n == 'ffn1_w_up':
            continue
        outs = _adam("adam_" + n, p[n], sq(mom[n], n), sq(vel[n], n), g_slots=R[n], after=token)
        meanwhile.append(outs[0])
        grad[n], delta[n], new_m[n], new_v[n] = [unsq(o, n) for o in outs]
    pack = lambda d: jnp.concatenate([slab(d[n]) for n in SMALL], axis=0)
    outs = _adam("adam_small", pack(w), pack(mom), pack(vel), g=red[:n_small])
    meanwhile.append(outs[0])
    offs = 0
    for n in SMALL:
        r = p[n].shape[-1] // 128
        grad[n], delta[n], new_m[n], new_v[n] = [o[offs:offs + r].reshape(w[n].shape) for o in outs]
        offs += slab_rows(n)
    dev = 4 * lax.axis_index("x") + 2 * lax.axis_index("y") + lax.axis_index("c")
    mcols = w['meta_tokens'].shape[1]
    gmeta = lax.dynamic_slice(gmeta_full, (0, dev * mcols), (N_META, mcols))
    outs = _adam("adam_meta", w['meta_tokens'], mom['meta_tokens'], vel['meta_tokens'], g=gmeta)
    grad['meta_tokens'], delta['meta_tokens'], new_m['meta_tokens'], new_v['meta_tokens'] = outs
    meanwhile.append(outs[0])
    n = 'ffn1_w_up'
    slots = _scatter_wait(tail_sems_s, tail_sems_r, tail_src, tail_land, meanwhile)
    outs = _adam("adam_" + n, p[n], sq(mom[n], n), sq(vel[n], n), g_slots=slots)
    grad[n], delta[n], new_m[n], new_v[n] = [unsq(o, n) for o in outs]

    return (loss, dh0[BLK:][None], *[grad[n] for n in WEIGHTS], *[delta[n] for n in WEIGHTS],
            *[new_m[n] for n in WEIGHTS], *[new_v[n] for n in WEIGHTS])


def kernel(x, meta_tokens, ffn1_pre_norm, ffn1_w_gate, ffn1_w_up, ffn1_w_down, ffn1_post_norm, mix_pre_norm, w_in, ret_group_norm, mla_q_norm, mla_w_uq, mla_kv_norm, mla_w_uk, mla_w_uv, w_out, mix_post_norm, ffn2_pre_norm, ffn2_w_gate, ffn2_w_up, ffn2_w_down, ffn2_post_norm, loss_target, m_meta_tokens, m_ffn1_pre_norm, m_ffn1_w_gate, m_ffn1_w_up, m_ffn1_w_down, m_ffn1_post_norm, m_mix_pre_norm, m_w_in, m_ret_group_norm, m_mla_q_norm, m_mla_w_uq, m_mla_kv_norm, m_mla_w_uk, m_mla_w_uv, m_w_out, m_mix_post_norm, m_ffn2_pre_norm, m_ffn2_w_gate, m_ffn2_w_up, m_ffn2_w_down, m_ffn2_post_norm, v_meta_tokens, v_ffn1_pre_norm, v_ffn1_w_gate, v_ffn1_w_up, v_ffn1_w_down, v_ffn1_post_norm, v_mix_pre_norm, v_w_in, v_ret_group_norm, v_mla_q_norm, v_mla_w_uq, v_mla_kv_norm, v_mla_w_uk, v_mla_w_uv, v_w_out, v_mix_post_norm, v_ffn2_pre_norm, v_ffn2_w_gate, v_ffn2_w_up, v_ffn2_w_down, v_ffn2_post_norm):
    w = dict(zip(WEIGHTS, (meta_tokens, ffn1_pre_norm, ffn1_w_gate, ffn1_w_up, ffn1_w_down, ffn1_post_norm,
                           mix_pre_norm, w_in, ret_group_norm, mla_q_norm, mla_w_uq, mla_kv_norm, mla_w_uk, mla_w_uv,
                           w_out, mix_post_norm, ffn2_pre_norm, ffn2_w_gate, ffn2_w_up, ffn2_w_down, ffn2_post_norm)))
    mom = dict(zip(WEIGHTS, (m_meta_tokens, m_ffn1_pre_norm, m_ffn1_w_gate, m_ffn1_w_up, m_ffn1_w_down,
                             m_ffn1_post_norm, m_mix_pre_norm, m_w_in, m_ret_group_norm, m_mla_q_norm, m_mla_w_uq,
                             m_mla_kv_norm, m_mla_w_uk, m_mla_w_uv, m_w_out, m_mix_post_norm, m_ffn2_pre_norm,
                             m_ffn2_w_gate, m_ffn2_w_up, m_ffn2_w_down, m_ffn2_post_norm)))
    vel = dict(zip(WEIGHTS, (v_meta_tokens, v_ffn1_pre_norm, v_ffn1_w_gate, v_ffn1_w_up, v_ffn1_w_down,
                             v_ffn1_post_norm, v_mix_pre_norm, v_w_in, v_ret_group_norm, v_mla_q_norm, v_mla_w_uq,
                             v_mla_kv_norm, v_mla_w_uk, v_mla_w_uv, v_w_out, v_mix_post_norm, v_ffn2_pre_norm,
                             v_ffn2_w_gate, v_ffn2_w_up, v_ffn2_w_down, v_ffn2_post_norm)))
    return _step(x, loss_target, w, mom, vel)
```
